```python
import jax, jax.numpy as jnp
from jax import lax
import numpy as np

D_MODEL = 1024
BATCH = 8
SEQ = 2048
DEPTH = 2

N_MIXERS = 2
SB_HEADS = 16
SB_HEAD_DIM = D_MODEL // SB_HEADS
Q_BLOCK = 128
D_RNN = (D_MODEL * 5) // 4
LRU_BLOCKS = 10
LRU_BLOCK_W = D_RNN // LRU_BLOCKS
CONV_W = 4
LRU_C = 8.0
D_FF = ((8 * D_MODEL // 3 + 127) // 128) * 128
N_SUB = 3
MACARON_W = 0.5
EPS = 1e-6
N_SB_LAYERS = (DEPTH + 1) // 2
N_LRU_LAYERS = DEPTH // 2

kernel_name = "hybrid_stickbreak_rglru_macaron_adaln"


def rmsnorm(x, g):
    xf = x.astype(jnp.float32)
    inv = lax.rsqrt(jnp.mean(xf * xf, axis=-1, keepdims=True) + EPS)
    return (xf * inv).astype(x.dtype) * g


def sublayer_input(x, g, mod_j):
    shift, scale, gate = mod_j[:, 0], mod_j[:, 1], mod_j[:, 2]
    h = rmsnorm(x, g) * (1 + scale[:, None, :]) + shift[:, None, :]
    return h, (1 + gate)[:, None, :]


def swiglu(h, w_gu, w_down):
    g, u = jnp.split(h @ w_gu, 2, axis=-1)
    return (jax.nn.silu(g) * u) @ w_down


def stick_breaking_attention(h, w_qkv, w_o):
    B, S, _ = h.shape
    qkv = (h @ w_qkv).reshape(B, S, 3, SB_HEADS, SB_HEAD_DIM)
    qkv = jnp.transpose(qkv, (2, 0, 3, 1, 4)).astype(jnp.float32)
    q = qkv[0] * (SB_HEAD_DIM ** -0.5)
    k, v = qkv[1], qkv[2]
    outs = []
    for blk in range(S // Q_BLOCK):
        t0 = blk * Q_BLOCK
        L = t0 + Q_BLOCK
        z = jnp.einsum('bhqd,bhkd->bhqk', q[:, :, t0:L], k[:, :, :L])
        t_idx = t0 + jnp.arange(Q_BLOCK)
        s_idx = jnp.arange(L)
        mask = s_idx[None, :] < t_idx[:, None]
        log_keep = jnp.where(mask, jax.nn.log_sigmoid(-z), 0.0)
        later = lax.cumsum(log_keep, axis=3, reverse=True) - log_keep
        w = jnp.where(mask, jnp.exp(jax.nn.log_sigmoid(z) + later), 0.0)
        outs.append(jnp.einsum('bhqk,bhkd->bhqd', w, v[:, :, :L]))
    o = jnp.concatenate(outs, axis=2)
    o = jnp.transpose(o, (0, 2, 1, 3)).reshape(B, S, D_MODEL).astype(h.dtype)
    return o @ w_o


def _lin_rec_combine(left, right):
    a1, b1 = left
    a2, b2 = right
    return a1 * a2, a2 * b1 + b2


def rglru_block(h, w_in, conv_w, conv_b, w_r, b_r, w_i, b_i, lam, w_out):
    B, S, _ = h.shape
    gate, xb = jnp.split(h @ w_in, 2, axis=-1)
    xp = jnp.pad(xb, ((0, 0), (CONV_W - 1, 0), (0, 0)))
    xc = conv_b + xp[:, 0:S] * conv_w[0]
    for tap in range(1, CONV_W):
        xc = xc + xp[:, tap:tap + S] * conv_w[tap]
    xblk = xc.reshape(B, S, LRU_BLOCKS, LRU_BLOCK_W)
    r = jax.nn.sigmoid(jnp.einsum('bsnk,nkj->bsnj', xblk, w_r).reshape(B, S, D_RNN) + b_r)
    i = jax.nn.sigmoid(jnp.einsum('bsnk,nkj->bsnj', xblk, w_i).reshape(B, S, D_RNN) + b_i)
    log_a = -LRU_C * r.astype(jnp.float32) * jax.nn.softplus(-lam.astype(jnp.float32))
    a = jnp.exp(log_a)
    b = jnp.sqrt(-jnp.expm1(2.0 * log_a)) * (i * xc).astype(jnp.float32)
    _, hs = lax.associative_scan(_lin_rec_combine, (a, b), axis=1)
    y = jax.nn.gelu(gate) * hs.astype(h.dtype)
    return y @ w_out


def _fwd_setup_inputs(seed: int = 0) -> dict:
    key = jax.random.key(seed)
    ks = jax.random.split(key, 24)
    D = D_MODEL
    f32 = jnp.float32
    nrm = lambda k, shape, s: jax.random.normal(k, shape, f32) * s
    x = nrm(ks[0], (BATCH, SEQ, D), 1.0)
    c = nrm(ks[1], (BATCH, D), 1.0)
    mod_w = nrm(ks[2], (DEPTH, D, N_SUB * 3 * D), 0.2 * D ** -0.5)
    mod_b = nrm(ks[3], (DEPTH, N_SUB * 3 * D), 0.05)
    norm_g = 1.0 + nrm(ks[4], (DEPTH, N_SUB, D), 0.02)
    ffn_w_gu = nrm(ks[5], (DEPTH, 2, D, 2 * D_FF), D ** -0.5)
    ffn_w_down = nrm(ks[6], (DEPTH, 2, D_FF, D), D_FF ** -0.5)
    sb_w_qkv = nrm(ks[7], (N_SB_LAYERS, D, 3 * D), D ** -0.5)
    sb_w_o = nrm(ks[8], (N_SB_LAYERS, D, D), D ** -0.5)
    lru_w_in = nrm(ks[9], (N_LRU_LAYERS, D, 2 * D_RNN), D ** -0.5)
    lru_conv_w = nrm(ks[10], (N_LRU_LAYERS, CONV_W, D_RNN), CONV_W ** -0.5)
    lru_conv_b = nrm(ks[11], (N_LRU_LAYERS, D_RNN), 0.02)
    lru_w_r = nrm(ks[12], (N_LRU_LAYERS, LRU_BLOCKS, LRU_BLOCK_W, LRU_BLOCK_W), LRU_BLOCK_W ** -0.5)
    lru_b_r = nrm(ks[13], (N_LRU_LAYERS, D_RNN), 0.1)
    lru_w_i = nrm(ks[14], (N_LRU_LAYERS, LRU_BLOCKS, LRU_BLOCK_W, LRU_BLOCK_W), LRU_BLOCK_W ** -0.5)
    lru_b_i = nrm(ks[15], (N_LRU_LAYERS, D_RNN), 0.1)
    a_c = jax.random.uniform(ks[16], (N_LRU_LAYERS, D_RNN), f32, 0.9, 0.999)
    a0 = a_c ** (1.0 / LRU_C)
    lru_lambda = jnp.log(a0) - jnp.log1p(-a0)
    lru_w_out = nrm(ks[17], (N_LRU_LAYERS, D_RNN, D), D_RNN ** -0.5)
    final_norm_g = 1.0 + nrm(ks[18], (D,), 0.02)
    return {"x": x, "c": c, "mod_w": mod_w, "mod_b": mod_b, "norm_g": norm_g,
            "ffn_w_gu": ffn_w_gu, "ffn_w_down": ffn_w_down,
            "sb_w_qkv": sb_w_qkv, "sb_w_o": sb_w_o,
            "lru_w_in": lru_w_in, "lru_conv_w": lru_conv_w, "lru_conv_b": lru_conv_b,
            "lru_w_r": lru_w_r, "lru_b_r": lru_b_r, "lru_w_i": lru_w_i, "lru_b_i": lru_b_i,
            "lru_lambda": lru_lambda, "lru_w_out": lru_w_out, "final_norm_g": final_norm_g}


def _fwd_reference(x, c, mod_w, mod_b, norm_g, ffn_w_gu, ffn_w_down, sb_w_qkv, sb_w_o,
              lru_w_in, lru_conv_w, lru_conv_b, lru_w_r, lru_b_r, lru_w_i, lru_b_i,
              lru_lambda, lru_w_out, final_norm_g):
    B = x.shape[0]
    c_act = jax.nn.silu(c)
    for layer in range(DEPTH):
        mod = (c_act @ mod_w[layer] + mod_b[layer]).reshape(B, N_SUB, 3, D_MODEL)
        h, g = sublayer_input(x, norm_g[layer, 0], mod[:, 0])
        x = x + MACARON_W * g * swiglu(h, ffn_w_gu[layer, 0], ffn_w_down[layer, 0])
        h, g = sublayer_input(x, norm_g[layer, 1], mod[:, 1])
        j = layer // N_MIXERS
        if layer % N_MIXERS == 0:
            y = stick_breaking_attention(h, sb_w_qkv[j], sb_w_o[j])
        else:
            y = rglru_block(h, lru_w_in[j], lru_conv_w[j], lru_conv_b[j], lru_w_r[j], lru_b_r[j],
                            lru_w_i[j], lru_b_i[j], lru_lambda[j], lru_w_out[j])
        x = x + g * y
        h, g = sublayer_input(x, norm_g[layer, 2], mod[:, 2])
        x = x + MACARON_W * g * swiglu(h, ffn_w_gu[layer, 1], ffn_w_down[layer, 1])
    return rmsnorm(x, final_norm_g)


import jax as _jax
import jax.numpy as _jnp

TWIN_FORMAT = 'train_step'
FWD_PARAMS = ['x', 'c', 'mod_w', 'mod_b', 'norm_g', 'ffn_w_gu', 'ffn_w_down', 'sb_w_qkv', 'sb_w_o', 'lru_w_in', 'lru_conv_w', 'lru_conv_b', 'lru_w_r', 'lru_b_r', 'lru_w_i', 'lru_b_i', 'lru_lambda', 'lru_w_out', 'final_norm_g']
TWIN_WEIGHTS = ['mod_w', 'mod_b', 'norm_g', 'ffn_w_gu', 'ffn_w_down', 'sb_w_qkv', 'sb_w_o', 'lru_w_in', 'lru_conv_w', 'lru_conv_b', 'lru_w_r', 'lru_b_r', 'lru_w_i', 'lru_b_i', 'lru_lambda', 'lru_w_out', 'final_norm_g']
TWIN_DIFF_INPUT = 'x'
TWIN_INPUTS = ['x', 'c', 'mod_w', 'mod_b', 'norm_g', 'ffn_w_gu', 'ffn_w_down', 'sb_w_qkv', 'sb_w_o', 'lru_w_in', 'lru_conv_w', 'lru_conv_b', 'lru_w_r', 'lru_b_r', 'lru_w_i', 'lru_b_i', 'lru_lambda', 'lru_w_out', 'final_norm_g', 'loss_target', 'm_mod_w', 'm_mod_b', 'm_norm_g', 'm_ffn_w_gu', 'm_ffn_w_down', 'm_sb_w_qkv', 'm_sb_w_o', 'm_lru_w_in', 'm_lru_conv_w', 'm_lru_conv_b', 'm_lru_w_r', 'm_lru_b_r', 'm_lru_w_i', 'm_lru_b_i', 'm_lru_lambda', 'm_lru_w_out', 'm_final_norm_g', 'v_mod_w', 'v_mod_b', 'v_norm_g', 'v_ffn_w_gu', 'v_ffn_w_down', 'v_sb_w_qkv', 'v_sb_w_o', 'v_lru_w_in', 'v_lru_conv_w', 'v_lru_conv_b', 'v_lru_w_r', 'v_lru_b_r', 'v_lru_w_i', 'v_lru_b_i', 'v_lru_lambda', 'v_lru_w_out', 'v_final_norm_g']
TWIN_OUTPUTS = ['loss', 'grad_x', 'grad_mod_w', 'grad_mod_b', 'grad_norm_g', 'grad_ffn_w_gu', 'grad_ffn_w_down', 'grad_sb_w_qkv', 'grad_sb_w_o', 'grad_lru_w_in', 'grad_lru_conv_w', 'grad_lru_conv_b', 'grad_lru_w_r', 'grad_lru_b_r', 'grad_lru_w_i', 'grad_lru_b_i', 'grad_lru_lambda', 'grad_lru_w_out', 'grad_final_norm_g', 'delta_mod_w', 'delta_mod_b', 'delta_norm_g', 'delta_ffn_w_gu', 'delta_ffn_w_down', 'delta_sb_w_qkv', 'delta_sb_w_o', 'delta_lru_w_in', 'delta_lru_conv_w', 'delta_lru_conv_b', 'delta_lru_w_r', 'delta_lru_b_r', 'delta_lru_w_i', 'delta_lru_b_i', 'delta_lru_lambda', 'delta_lru_w_out', 'delta_final_norm_g', 'new_m_mod_w', 'new_m_mod_b', 'new_m_norm_g', 'new_m_ffn_w_gu', 'new_m_ffn_w_down', 'new_m_sb_w_qkv', 'new_m_sb_w_o', 'new_m_lru_w_in', 'new_m_lru_conv_w', 'new_m_lru_conv_b', 'new_m_lru_w_r', 'new_m_lru_b_r', 'new_m_lru_w_i', 'new_m_lru_b_i', 'new_m_lru_lambda', 'new_m_lru_w_out', 'new_m_final_norm_g', 'new_v_mod_w', 'new_v_mod_b', 'new_v_norm_g', 'new_v_ffn_w_gu', 'new_v_ffn_w_down', 'new_v_sb_w_qkv', 'new_v_sb_w_o', 'new_v_lru_w_in', 'new_v_lru_conv_w', 'new_v_lru_conv_b', 'new_v_lru_w_r', 'new_v_lru_b_r', 'new_v_lru_w_i', 'new_v_lru_b_i', 'new_v_lru_lambda', 'new_v_lru_w_out', 'new_v_final_norm_g']
TWIN_LEAF_KINDS = {'loss': 'loss', 'grad_x': 'grad_x', 'grad_mod_w': 'grad_w', 'grad_mod_b': 'grad_w', 'grad_norm_g': 'grad_w', 'grad_ffn_w_gu': 'grad_w', 'grad_ffn_w_down': 'grad_w', 'grad_sb_w_qkv': 'grad_w', 'grad_sb_w_o': 'grad_w', 'grad_lru_w_in': 'grad_w', 'grad_lru_conv_w': 'grad_w', 'grad_lru_conv_b': 'grad_w', 'grad_lru_w_r': 'grad_w', 'grad_lru_b_r': 'grad_w', 'grad_lru_w_i': 'grad_w', 'grad_lru_b_i': 'grad_w', 'grad_lru_lambda': 'grad_w', 'grad_lru_w_out': 'grad_w', 'grad_final_norm_g': 'grad_w', 'delta_mod_w': 'delta_w', 'delta_mod_b': 'delta_w', 'delta_norm_g': 'delta_w', 'delta_ffn_w_gu': 'delta_w', 'delta_ffn_w_down': 'delta_w', 'delta_sb_w_qkv': 'delta_w', 'delta_sb_w_o': 'delta_w', 'delta_lru_w_in': 'delta_w', 'delta_lru_conv_w': 'delta_w', 'delta_lru_conv_b': 'delta_w', 'delta_lru_w_r': 'delta_w', 'delta_lru_b_r': 'delta_w', 'delta_lru_w_i': 'delta_w', 'delta_lru_b_i': 'delta_w', 'delta_lru_lambda': 'delta_w', 'delta_lru_w_out': 'delta_w', 'delta_final_norm_g': 'delta_w', 'new_m_mod_w': 'new_m', 'new_m_mod_b': 'new_m', 'new_m_norm_g': 'new_m', 'new_m_ffn_w_gu': 'new_m', 'new_m_ffn_w_down': 'new_m', 'new_m_sb_w_qkv': 'new_m', 'new_m_sb_w_o': 'new_m', 'new_m_lru_w_in': 'new_m', 'new_m_lru_conv_w': 'new_m', 'new_m_lru_conv_b': 'new_m', 'new_m_lru_w_r': 'new_m', 'new_m_lru_b_r': 'new_m', 'new_m_lru_w_i': 'new_m', 'new_m_lru_b_i': 'new_m', 'new_m_lru_lambda': 'new_m', 'new_m_lru_w_out': 'new_m', 'new_m_final_norm_g': 'new_m', 'new_v_mod_w': 'new_v', 'new_v_mod_b': 'new_v', 'new_v_norm_g': 'new_v', 'new_v_ffn_w_gu': 'new_v', 'new_v_ffn_w_down': 'new_v', 'new_v_sb_w_qkv': 'new_v', 'new_v_sb_w_o': 'new_v', 'new_v_lru_w_in': 'new_v', 'new_v_lru_conv_w': 'new_v', 'new_v_lru_conv_b': 'new_v', 'new_v_lru_w_r': 'new_v', 'new_v_lru_b_r': 'new_v', 'new_v_lru_w_i': 'new_v', 'new_v_lru_b_i': 'new_v', 'new_v_lru_lambda': 'new_v', 'new_v_lru_w_out': 'new_v', 'new_v_final_norm_g': 'new_v'}


def _forward(args):
    return _fwd_reference(*[args[k] for k in FWD_PARAMS])


def _output_shape():
    out = _jax.eval_shape(lambda: _forward(_fwd_setup_inputs(0)))
    return out.shape, out.dtype

N_MICROBATCH = 1
ADAM_LR = 0.001
ADAM_B1 = 0.9
ADAM_B2 = 0.999
ADAM_EPS = 1e-08
ADAM_WD = 0.01
ADAM_STEP = 10
PER_EXAMPLE_BATCH_AXIS = {'x': 0, 'c': 0, 'loss_target': 0}
SHARED_INPUTS = []
_WEIGHT_DTYPES = {'mod_w': _jnp.float32, 'mod_b': _jnp.float32, 'norm_g': _jnp.float32, 'ffn_w_gu': _jnp.float32, 'ffn_w_down': _jnp.float32, 'sb_w_qkv': _jnp.float32, 'sb_w_o': _jnp.float32, 'lru_w_in': _jnp.float32, 'lru_conv_w': _jnp.float32, 'lru_conv_b': _jnp.float32, 'lru_w_r': _jnp.float32, 'lru_b_r': _jnp.float32, 'lru_w_i': _jnp.float32, 'lru_b_i': _jnp.float32, 'lru_lambda': _jnp.float32, 'lru_w_out': _jnp.float32, 'final_norm_g': _jnp.float32}
MOMENT_SCALE = {'mod_w': 7.941352e-02, 'mod_b': 1.334936e-01, 'norm_g': 6.632010e-02, 'ffn_w_gu': 2.117441e-02, 'ffn_w_down': 3.459931e-02, 'sb_w_qkv': 5.814435e-02, 'sb_w_o': 8.695341e-02, 'lru_w_in': 6.738488e-02, 'lru_conv_w': 6.522836e-02, 'lru_conv_b': 3.337079e-01, 'lru_w_r': 9.925198e-03, 'lru_b_r': 1.275731e-02, 'lru_w_i': 1.816240e-02, 'lru_b_i': 2.598996e-02, 'lru_lambda': 3.043240e-02, 'lru_w_out': 7.636588e-02, 'final_norm_g': 1.604721e+01}


def _to_microbatches(a, axis):
    t = _jnp.moveaxis(a, axis, 0)
    t = t.reshape((N_MICROBATCH, t.shape[0] // N_MICROBATCH) + t.shape[1:])
    return _jnp.moveaxis(t, 1, axis + 1)


def setup_inputs(seed: int = 0) -> dict:
    inp = _fwd_setup_inputs(seed)
    key = _jax.random.fold_in(_jax.random.key(seed), 7919)
    shape, _ = _output_shape()
    out = dict(inp)
    out["loss_target"] = _jax.random.normal(_jax.random.fold_in(key, 0), shape, _jnp.float32)
    for i, name in enumerate(TWIN_WEIGHTS):
        w = inp[name].astype(_jnp.float32)
        if MOMENT_SCALE is None:
            s = _jnp.sqrt(_jnp.mean(_jnp.square(w)) + 1e-30)
        else:
            s = MOMENT_SCALE[name]
        km, kv = _jax.random.split(_jax.random.fold_in(key, i + 1))
        out[name] = w
        out["m_" + name] = s * _jax.random.normal(km, w.shape, _jnp.float32)
        out["v_" + name] = (s * s) * _jax.random.uniform(kv, w.shape, _jnp.float32, 0.5, 1.5)
    if N_MICROBATCH > 1:
        for name, axis in PER_EXAMPLE_BATCH_AXIS.items():
            out[name] = _to_microbatches(out[name], axis)
    return {'x': out['x'], 'c': out['c'], 'mod_w': out['mod_w'], 'mod_b': out['mod_b'], 'norm_g': out['norm_g'], 'ffn_w_gu': out['ffn_w_gu'], 'ffn_w_down': out['ffn_w_down'], 'sb_w_qkv': out['sb_w_qkv'], 'sb_w_o': out['sb_w_o'], 'lru_w_in': out['lru_w_in'], 'lru_conv_w': out['lru_conv_w'], 'lru_conv_b': out['lru_conv_b'], 'lru_w_r': out['lru_w_r'], 'lru_b_r': out['lru_b_r'], 'lru_w_i': out['lru_w_i'], 'lru_b_i': out['lru_b_i'], 'lru_lambda': out['lru_lambda'], 'lru_w_out': out['lru_w_out'], 'final_norm_g': out['final_norm_g'], 'loss_target': out['loss_target'], 'm_mod_w': out['m_mod_w'], 'm_mod_b': out['m_mod_b'], 'm_norm_g': out['m_norm_g'], 'm_ffn_w_gu': out['m_ffn_w_gu'], 'm_ffn_w_down': out['m_ffn_w_down'], 'm_sb_w_qkv': out['m_sb_w_qkv'], 'm_sb_w_o': out['m_sb_w_o'], 'm_lru_w_in': out['m_lru_w_in'], 'm_lru_conv_w': out['m_lru_conv_w'], 'm_lru_conv_b': out['m_lru_conv_b'], 'm_lru_w_r': out['m_lru_w_r'], 'm_lru_b_r': out['m_lru_b_r'], 'm_lru_w_i': out['m_lru_w_i'], 'm_lru_b_i': out['m_lru_b_i'], 'm_lru_lambda': out['m_lru_lambda'], 'm_lru_w_out': out['m_lru_w_out'], 'm_final_norm_g': out['m_final_norm_g'], 'v_mod_w': out['v_mod_w'], 'v_mod_b': out['v_mod_b'], 'v_norm_g': out['v_norm_g'], 'v_ffn_w_gu': out['v_ffn_w_gu'], 'v_ffn_w_down': out['v_ffn_w_down'], 'v_sb_w_qkv': out['v_sb_w_qkv'], 'v_sb_w_o': out['v_sb_w_o'], 'v_lru_w_in': out['v_lru_w_in'], 'v_lru_conv_w': out['v_lru_conv_w'], 'v_lru_conv_b': out['v_lru_conv_b'], 'v_lru_w_r': out['v_lru_w_r'], 'v_lru_b_r': out['v_lru_b_r'], 'v_lru_w_i': out['v_lru_w_i'], 'v_lru_b_i': out['v_lru_b_i'], 'v_lru_lambda': out['v_lru_lambda'], 'v_lru_w_out': out['v_lru_w_out'], 'v_final_norm_g': out['v_final_norm_g']}


def _loss(weights, diff, rest, loss_target):
    with _jax.named_scope("forward"):
        args = {**rest, TWIN_DIFF_INPUT: diff, **{k: w.astype(_WEIGHT_DTYPES[k]) for k, w in weights.items()}}
        y = _forward(args)
    with _jax.named_scope("loss_head"):
        err = _jnp.square(y.astype(_jnp.float32) - loss_target)
        return 0.5 * _jnp.sum(_jnp.mean(err, axis=-1)) if err.ndim else 0.5 * err


def _adamw(w, g, m, v):
    m = ADAM_B1 * m + (1.0 - ADAM_B1) * g
    v = ADAM_B2 * v + (1.0 - ADAM_B2) * _jnp.square(g)
    m_hat = m / (1.0 - ADAM_B1 ** ADAM_STEP)
    v_hat = v / (1.0 - ADAM_B2 ** ADAM_STEP)
    delta = -ADAM_LR * (m_hat / (_jnp.sqrt(v_hat) + ADAM_EPS) + ADAM_WD * w)
    return delta, m, v


def reference(x, c, mod_w, mod_b, norm_g, ffn_w_gu, ffn_w_down, sb_w_qkv, sb_w_o, lru_w_in, lru_conv_w, lru_conv_b, lru_w_r, lru_b_r, lru_w_i, lru_b_i, lru_lambda, lru_w_out, final_norm_g, loss_target, m_mod_w, m_mod_b, m_norm_g, m_ffn_w_gu, m_ffn_w_down, m_sb_w_qkv, m_sb_w_o, m_lru_w_in, m_lru_conv_w, m_lru_conv_b, m_lru_w_r, m_lru_b_r, m_lru_w_i, m_lru_b_i, m_lru_lambda, m_lru_w_out, m_final_norm_g, v_mod_w, v_mod_b, v_norm_g, v_ffn_w_gu, v_ffn_w_down, v_sb_w_qkv, v_sb_w_o, v_lru_w_in, v_lru_conv_w, v_lru_conv_b, v_lru_w_r, v_lru_b_r, v_lru_w_i, v_lru_b_i, v_lru_lambda, v_lru_w_out, v_final_norm_g):
    given = dict(x=x, c=c, mod_w=mod_w, mod_b=mod_b, norm_g=norm_g, ffn_w_gu=ffn_w_gu, ffn_w_down=ffn_w_down, sb_w_qkv=sb_w_qkv, sb_w_o=sb_w_o, lru_w_in=lru_w_in, lru_conv_w=lru_conv_w, lru_conv_b=lru_conv_b, lru_w_r=lru_w_r, lru_b_r=lru_b_r, lru_w_i=lru_w_i, lru_b_i=lru_b_i, lru_lambda=lru_lambda, lru_w_out=lru_w_out, final_norm_g=final_norm_g, loss_target=loss_target, m_mod_w=m_mod_w, m_mod_b=m_mod_b, m_norm_g=m_norm_g, m_ffn_w_gu=m_ffn_w_gu, m_ffn_w_down=m_ffn_w_down, m_sb_w_qkv=m_sb_w_qkv, m_sb_w_o=m_sb_w_o, m_lru_w_in=m_lru_w_in, m_lru_conv_w=m_lru_conv_w, m_lru_conv_b=m_lru_conv_b, m_lru_w_r=m_lru_w_r, m_lru_b_r=m_lru_b_r, m_lru_w_i=m_lru_w_i, m_lru_b_i=m_lru_b_i, m_lru_lambda=m_lru_lambda, m_lru_w_out=m_lru_w_out, m_final_norm_g=m_final_norm_g, v_mod_w=v_mod_w, v_mod_b=v_mod_b, v_norm_g=v_norm_g, v_ffn_w_gu=v_ffn_w_gu, v_ffn_w_down=v_ffn_w_down, v_sb_w_qkv=v_sb_w_qkv, v_sb_w_o=v_sb_w_o, v_lru_w_in=v_lru_w_in, v_lru_conv_w=v_lru_conv_w, v_lru_conv_b=v_lru_conv_b, v_lru_w_r=v_lru_w_r, v_lru_b_r=v_lru_b_r, v_lru_w_i=v_lru_w_i, v_lru_b_i=v_lru_b_i, v_lru_lambda=v_lru_lambda, v_lru_w_out=v_lru_w_out, v_final_norm_g=v_final_norm_g)
    weights = {n: given[n] for n in TWIN_WEIGHTS}
    shared = {n: given[n] for n in SHARED_INPUTS}
    per_example = {n: given[n] for n in ['x', 'c']}
    grad_fn = _jax.value_and_grad(_loss, argnums=(0, 1))

    def one_microbatch(ex, loss_target):
        ex = dict(ex)
        diff = ex.pop(TWIN_DIFF_INPUT)
        return grad_fn(weights, diff, {**shared, **ex}, loss_target)

    if N_MICROBATCH == 1:
        loss, (grad_w, grad_x) = one_microbatch(per_example, given["loss_target"])
    else:
        def body(carry, xs):
            loss_sum, grad_sum = carry
            l_k, (gw_k, gx_k) = one_microbatch(xs[0], xs[1])
            with _jax.named_scope("update"):
                return (loss_sum + l_k, _jax.tree.map(_jnp.add, grad_sum, gw_k)), gx_k

        init = (_jnp.zeros((), _jnp.float32), _jax.tree.map(_jnp.zeros_like, weights))
        (loss, grad_w), grad_x = _jax.lax.scan(body, init, (per_example, given["loss_target"]))
    with _jax.named_scope("update"):
        delta_w, new_m, new_v = {}, {}, {}
        for n in TWIN_WEIGHTS:
            delta_w[n], new_m[n], new_v[n] = _adamw(weights[n], grad_w[n], given["m_" + n], given["v_" + n])
    return (loss, grad_x, *[grad_w[n] for n in TWIN_WEIGHTS], *[delta_w[n] for n in TWIN_WEIGHTS],
            *[new_m[n] for n in TWIN_WEIGHTS], *[new_v[n] for n in TWIN_WEIGHTS])
```

```python
import functools
import math

import jax
import jax.numpy as jnp
from jax import lax
from jax.experimental import pallas as pl
from jax.experimental.pallas import tpu as pltpu

F32 = jnp.float32
BF16 = jnp.bfloat16
SDS = jax.ShapeDtypeStruct
MESH = pl.DeviceIdType.MESH
ANY = pl.BlockSpec(memory_space=pl.ANY)

N_DEV = 8
LANES = 128
HEAD_DIM = 64
LRU_BLOCK_W = 128
LRU_C = 8.0
MACARON_W = 0.5
NORM_EPS = 1e-6
ADAM_LR = 0.001
ADAM_B1 = 0.9
ADAM_B2 = 0.999
ADAM_EPS = 1e-08
ADAM_WD = 0.01
ADAM_STEP = 10
VMEM_LIMIT = 56 * 1024 * 1024
GELU_C = math.sqrt(2.0 / math.pi)
GELU_K = 0.044715

DIMS = {
    "nn": (((1,), (0,)), ((), ())),
    "nt": (((1,), (1,)), ((), ())),
    "tn": (((0,), (0,)), ((), ())),
}


def _pcall(body, **kw):
    return pl.pallas_call(body, **kw)


def _params(sem=None):
    return pltpu.CompilerParams(dimension_semantics=sem, vmem_limit_bytes=VMEM_LIMIT)


def _tile(n, prefs):
    for p in prefs:
        if n % p == 0:
            return p
    return n


def _dot(a, b, dims):
    return lax.dot_general(a, b, DIMS[dims], preferred_element_type=F32)


def _softplus(z):
    return jnp.maximum(z, 0.0) + jnp.log(1.0 + jnp.exp(-jnp.abs(z)))


def _mesh_pos():
    return lax.axis_index("x"), lax.axis_index("y"), lax.axis_index("c")


def _allgather(xs, name):
    p, r, cdim = xs.shape

    def body(x_ref, out_ref, send_sems, recv_sems, local_sem):
        x, y, c = _mesh_pos()
        me, sibling = (x, y, c), (x, y, 1 - c)
        chips = [(1 - x, y), (x, 1 - y), (1 - x, 1 - y)]

        def rows(px, py, pc):
            return out_ref.at[:, 4 * px + 2 * py + pc]

        def copy(k, block, to, src=None):
            return pltpu.make_async_remote_copy(
                src_ref=rows(*block) if src is None else src, dst_ref=rows(*block),
                send_sem=send_sems.at[k], recv_sem=recv_sems.at[k], device_id=to, device_id_type=MESH)

        mine = pltpu.make_async_copy(x_ref, rows(*me), local_sem)
        mine.start()
        first = [copy(0, me, sibling, src=x_ref)]
        first += [copy(1 + j, me, (*chip, c), src=x_ref) for j, chip in enumerate(chips)]
        for cp in first:
            cp.start()
        passed = [copy(4 + j, (*chip, c), sibling) for j, chip in enumerate(chips)]
        for j, chip in enumerate(chips):
            copy(1 + j, (*chip, c), me).wait_recv()
            passed[j].start()
        copy(0, sibling, me).wait_recv()
        for j, chip in enumerate(chips):
            copy(4 + j, (*chip, 1 - c), me).wait_recv()
        for cp in first + passed:
            cp.wait_send()
        mine.wait()

    return _pcall(
        body, name=name, out_shape=SDS((p, N_DEV, r, cdim), xs.dtype), in_specs=[ANY], out_specs=ANY,
        scratch_shapes=[pltpu.SemaphoreType.DMA((7,)), pltpu.SemaphoreType.DMA((7,)), pltpu.SemaphoreType.DMA(())],
    )(xs)


def _exchange(srcs, n_slots, route, name):
    n = len(srcs)

    def body(*refs):
        src_refs, out_refs, (send_sems, recv_sems) = refs[:n], refs[n:2 * n], refs[2 * n:]
        x, y, c = _mesh_pos()
        copies = []
        for a in range(n):
            for s in range(n_slots):
                chunk, target = route(x, y, c, s)
                copies.append(pltpu.make_async_remote_copy(
                    src_ref=src_refs[a].at[:, chunk], dst_ref=out_refs[a].at[s],
                    send_sem=send_sems.at[a * n_slots + s], recv_sem=recv_sems.at[a * n_slots + s],
                    device_id=target, device_id_type=MESH))
        for cp in copies:
            cp.start()
        for cp in copies:
            cp.wait_recv()
        for cp in copies:
            cp.wait_send()

    return _pcall(
        body, name=name, out_shape=[SDS((n_slots, g.shape[0]) + g.shape[2:], g.dtype) for g in srcs],
        in_specs=[ANY] * n, out_specs=[ANY] * n,
        scratch_shapes=[pltpu.SemaphoreType.DMA((n * n_slots,)), pltpu.SemaphoreType.DMA((n * n_slots,))],
    )(*srcs)


def _sibling_route(x, y, c, k):
    return 2 * k + 1 - c, (x, y, 1 - c)


def _chip_route(x, y, c, j):
    px, py = [(1 - x, y), (x, 1 - y), (1 - x, 1 - y)][j]
    return 2 * px + py, (px, py, c)


def _pair_sum(grads, recv4, c_idx, name):
    p, _, r, cdim = grads.shape
    tr = _tile(r, (512, 256, 176, 160, 128, 64, 32, 16))

    def body(c_ref, a_ref, b_ref, o_ref):
        o_ref[...] = (a_ref[...].astype(F32) + b_ref[...].astype(F32)).astype(o_ref.dtype)

    blk = (None, None, tr, cdim)
    grid_spec = pltpu.PrefetchScalarGridSpec(
        num_scalar_prefetch=1, grid=(4, p, r // tr),
        in_specs=[pl.BlockSpec(blk, lambda k, q, i, c_ref: (q, 2 * k + c_ref[0], i, 0)),
                  pl.BlockSpec(blk, lambda k, q, i, c_ref: (k, q, i, 0))],
        out_specs=pl.BlockSpec(blk, lambda k, q, i, c_ref: (q, k, i, 0)))
    return _pcall(body, name=name, grid_spec=grid_spec, out_shape=SDS((p, 4, r, cdim), grads.dtype),
                  compiler_params=_params(("parallel", "parallel", "parallel")))(c_idx, grads, recv4)


def _mm(name, ins, prods, n_acc, acc_shape, epi_idx, epilogue, out_shapes, out_specs, grid, dims, fill=None):
    n_in, n_out, nk = len(ins), len(out_shapes), grid[2]
    aliases = {}
    if fill is not None:
        ins = list(ins) + [(fill, ANY)]
        aliases = {n_in: 0}

    n_refs_in = len(ins)

    def body(*refs):
        in_refs, out_refs, acc_refs = refs[:n_in], refs[n_refs_in:n_refs_in + n_out], refs[n_refs_in + n_out:]
        k = pl.program_id(2)

        @pl.when(k == 0)
        def _():
            for acc in acc_refs:
                acc[...] = jnp.zeros_like(acc)

        for ia, ib, iacc in prods:
            acc_refs[iacc][...] += _dot(in_refs[ia][...], in_refs[ib][...], dims)

        @pl.when(k == nk - 1)
        def _():
            outs = epilogue([acc[...] for acc in acc_refs], [in_refs[i][...] for i in epi_idx])
            for o_ref, o in zip(out_refs, outs):
                if isinstance(o, tuple):
                    for plane, part in enumerate(o):
                        o_ref[plane] = part.astype(o_ref.dtype)
                else:
                    o_ref[...] = o.astype(o_ref.dtype)

    return _pcall(
        body, name=name, grid=grid, in_specs=[s for _, s in ins], out_specs=out_specs, out_shape=out_shapes,
        scratch_shapes=[pltpu.VMEM(acc_shape, F32) for _ in range(n_acc)], input_output_aliases=aliases,
        compiler_params=_params(("parallel", "parallel", "arbitrary")),
    )(*[a for a, _ in ins])


def _plain(accs, _):
    return accs


def _mm_nn(name, a, b, out_dtype, extras=(), epilogue=_plain, n_out=1):
    m, kd = a.shape
    n = b.shape[1]
    tm, tn, tk = _tile(m, (512, 256, 128)), _tile(n, (512, 256, 128)), _tile(kd, (1024, 512, 256, 128))
    ins = [(a, pl.BlockSpec((tm, tk), lambda i, j, k: (i, k))), (b, pl.BlockSpec((tk, tn), lambda i, j, k: (k, j)))]
    for arr, kind in extras:
        if kind == "tile":
            ins.append((arr, pl.BlockSpec((tm, tn), lambda i, j, k: (i, j))))
        else:
            ins.append((arr, pl.BlockSpec((1, tn), lambda i, j, k: (0, j))))
    dts = out_dtype if isinstance(out_dtype, (list, tuple)) else [out_dtype] * n_out
    return _mm(name, ins, [(0, 1, 0)], 1, (tm, tn), list(range(2, len(ins))), epilogue,
               [SDS((m, n), dt) for dt in dts], [pl.BlockSpec((tm, tn), lambda i, j, k: (i, j)) for _ in dts],
               (m // tm, n // tn, kd // tk), "nn")


def _mm_nt(name, a, b, out_dtype):
    m, kd = a.shape
    n = b.shape[0]
    tm, tn, tk = _tile(m, (512, 256, 128)), _tile(n, (512, 256, 128)), _tile(kd, (1024, 512, 256, 128))
    ins = [(a, pl.BlockSpec((tm, tk), lambda i, j, k: (i, k))), (b, pl.BlockSpec((tn, tk), lambda i, j, k: (j, k)))]
    return _mm(name, ins, [(0, 1, 0)], 1, (tm, tn), [], _plain, [SDS((m, n), out_dtype)],
               [pl.BlockSpec((tm, tn), lambda i, j, k: (i, j))], (m // tm, n // tn, kd // tk), "nt")[0]


def _mm_tn(name, a, b, out_dtype):
    t, m = a.shape
    n = b.shape[1]
    tm, tn, tk = _tile(m, (512, 256, 128)), _tile(n, (512, 256, 128)), _tile(t, (512, 256, 128))
    ins = [(a, pl.BlockSpec((tk, tm), lambda i, j, k: (k, i))), (b, pl.BlockSpec((tk, tn), lambda i, j, k: (k, j)))]
    return _mm(name, ins, [(0, 1, 0)], 1, (tm, tn), [], _plain, [SDS((m, n), out_dtype)],
               [pl.BlockSpec((tm, tn), lambda i, j, k: (i, j))], (m // tm, n // tn, t // tk), "tn")[0]


def _mm_nt_stack(name, a3, b, out_dtype):
    cc, m, kd = a3.shape
    n = b.shape[0]
    tm, tn, tk = _tile(m, (512, 256, 128)), _tile(n, (512, 256, 128)), _tile(kd, (1024, 512, 256, 128))
    nk = kd // tk
    ins = [(a3, pl.BlockSpec((None, tm, tk), lambda i, j, k: (k // nk, i, k % nk))),
           (b, pl.BlockSpec((tn, tk), lambda i, j, k: (j, k)))]
    return _mm(name, ins, [(0, 1, 0)], 1, (tm, tn), [], _plain, [SDS((m, n), out_dtype)],
               [pl.BlockSpec((tm, tn), lambda i, j, k: (i, j))], (m // tm, n // tn, cc * nk), "nt")[0]


def _mm_tn_stack(name, a, b3, out_dtype):
    t, m = a.shape
    cc, _, n = b3.shape
    tm, tn, tk = _tile(m, (512, 256, 128)), _tile(n, (512, 256, 128)), _tile(t, (512, 256, 128))
    nj = n // tn
    ins = [(a, pl.BlockSpec((tk, tm), lambda i, j, k: (k, i))),
           (b3, pl.BlockSpec((None, tk, tn), lambda i, j, k: (j // nj, k, j % nj)))]
    return _mm(name, ins, [(0, 1, 0)], 1, (tm, tn), [], _plain, [SDS((m, cc * n), out_dtype)],
               [pl.BlockSpec((tm, tn), lambda i, j, k: (i, j))], (m // tm, cc * nj, t // tk), "tn")[0]


def _mm_nn_chunks(name, a, wc, out_dtype):
    m, kd = a.shape
    nch, _, cw = wc.shape
    tm, tk = _tile(m, (512, 256, 128)), _tile(kd, (1024, 512, 256, 128))
    ins = [(a, pl.BlockSpec((tm, tk), lambda i, j, k: (i, k))), (wc, pl.BlockSpec((None, tk, cw), lambda i, j, k: (j, k, 0)))]
    return _mm(name, ins, [(0, 1, 0)], 1, (tm, cw), [], _plain, [SDS((m, nch * cw), out_dtype)],
               [pl.BlockSpec((tm, cw), lambda i, j, k: (i, j))], (m // tm, nch, kd // tk), "nn")[0]


def _mm_nt_stack_chunks(name, a3, wc, out_dtype):
    s, m, n = a3.shape
    nch, d, cw = wc.shape
    unit = math.gcd(n, cw)
    nu_a, nu_w = n // unit, cw // unit
    tm, tn = _tile(m, (512, 256, 128)), _tile(d, (512, 256, 128))
    ins = [(a3, pl.BlockSpec((None, tm, unit), lambda i, j, k: (k // nu_a, i, k % nu_a))),
           (wc, pl.BlockSpec((None, tn, unit), lambda i, j, k: (k // nu_w, j, k % nu_w)))]
    return _mm(name, ins, [(0, 1, 0)], 1, (tm, tn), [], _plain, [SDS((m, d), out_dtype)],
               [pl.BlockSpec((tm, tn), lambda i, j, k: (i, j))], (m // tm, d // tn, s * nu_a), "nt")[0]


def _mm_tn_stack_chunks(name, a, b3, cw, out_dtype):
    t, m = a.shape
    s, _, n = b3.shape
    unit = math.gcd(n, cw)
    nu_b, nu_w = n // unit, cw // unit
    tm, tk = _tile(m, (512, 256, 128)), _tile(t, (512, 256, 128))
    ins = [(a, pl.BlockSpec((tk, tm), lambda i, j, k: (k, i))),
           (b3, pl.BlockSpec((None, tk, unit), lambda i, j, k: (j // nu_b, k, j % nu_b)))]
    return _mm(name, ins, [(0, 1, 0)], 1, (tm, unit), [], _plain, [SDS((s * n // cw, m, cw), out_dtype)],
               [pl.BlockSpec((None, tm, unit), lambda i, j, k: (j // nu_w, i, j % nu_w))],
               (m // tm, s * nu_b, t // tk), "tn")[0]


def _chunks_to_cols(name, wc, eye3):
    nch, d, cw = wc.shape
    n = nch * cw
    tm, tn = _tile(d, (512, 256, 128)), _tile(n, (512, 256, 128))
    ins = [(wc, pl.BlockSpec((None, tm, cw), lambda i, j, k: (k, i, 0))),
           (eye3, pl.BlockSpec((None, cw, tn), lambda i, j, k: (k, 0, j)))]
    return _mm(name, ins, [(0, 1, 0)], 1, (tm, tn), [], _plain, [SDS((d, n), wc.dtype)],
               [pl.BlockSpec((tm, tn), lambda i, j, k: (i, j))], (d // tm, n // tn, nch), "nn")[0]


def _cols_to_chunks(name, full, eye3):
    d, n = full.shape
    nch, cw, _ = eye3.shape
    tm, tk = _tile(d, (512, 256, 128)), _tile(n, (1024, 640, 512, 256, 128))
    ins = [(full, pl.BlockSpec((tm, tk), lambda i, j, k: (i, k))),
           (eye3, pl.BlockSpec((None, cw, tk), lambda i, j, k: (j, 0, k)))]
    return _mm(name, ins, [(0, 1, 0)], 1, (tm, cw), [], _plain, [SDS((nch, d, cw), full.dtype)],
               [pl.BlockSpec((None, tm, cw), lambda i, j, k: (j, i, 0))], (d // tm, nch, n // tk), "nt")[0]


def _row_tile(t):
    return _tile(t, (256, 128, 64, 32, 16, 8))


def _norm_fwd(x, g, scale1p, shift, name):
    t, d = x.shape
    tr = _row_tile(t)

    def body(x_ref, g_ref, s_ref, b_ref, h_ref):
        xv = x_ref[...]
        inv = lax.rsqrt(jnp.mean(xv * xv, axis=-1, keepdims=True) + NORM_EPS)
        h_ref[...] = ((xv * inv) * g_ref[...] * s_ref[...] + b_ref[...]).astype(h_ref.dtype)

    vec = pl.BlockSpec((1, d), lambda i: (0, 0))
    return _pcall(body, name=name, grid=(t // tr,), in_specs=[pl.BlockSpec((tr, d), lambda i: (i, 0)), vec, vec, vec],
                  out_specs=pl.BlockSpec((tr, d), lambda i: (i, 0)), out_shape=SDS((t, d), BF16),
                  compiler_params=_params(("parallel",)))(x, g, scale1p, shift)


def _adaln_bwd(dh, x, y, dxo, g, scale1p, w_sub, gw_prev, name):
    t, d = x.shape
    tr = _row_tile(t)

    def body(dh_ref, x_ref, y_ref, dxo_ref, g_ref, s_ref, gw_ref, dx_ref, dyp_ref, sums_ref):
        i = pl.program_id(0)

        @pl.when(i == 0)
        def _():
            sums_ref[...] = jnp.zeros_like(sums_ref)

        xv, dhv, dxov = x_ref[...], dh_ref[...], dxo_ref[...]
        inv = lax.rsqrt(jnp.mean(xv * xv, axis=-1, keepdims=True) + NORM_EPS)
        xn = xv * inv
        gv = g_ref[...]
        dn = dhv * s_ref[...]
        dxn = dn * gv
        dx = inv * (dxn - xn * jnp.mean(dxn * xn, axis=-1, keepdims=True)) + dxov
        dx_ref[...] = dx
        dyp_ref[...] = (gw_ref[...] * dx).astype(dyp_ref.dtype)
        sums_ref[0:1, :] += jnp.sum(dhv, axis=0, keepdims=True)
        sums_ref[1:2, :] += jnp.sum(dhv * (xn * gv), axis=0, keepdims=True)
        sums_ref[2:3, :] += jnp.sum(w_sub * y_ref[...] * dxov, axis=0, keepdims=True)
        sums_ref[3:4, :] += jnp.sum(dn * xn, axis=0, keepdims=True)

    blk = pl.BlockSpec((tr, d), lambda i: (i, 0))
    vec = pl.BlockSpec((1, d), lambda i: (0, 0))
    return _pcall(
        body, name=name, grid=(t // tr,), in_specs=[blk, blk, blk, blk, vec, vec, vec],
        out_specs=[blk, blk, pl.BlockSpec((8, d), lambda i: (0, 0))],
        out_shape=[SDS((t, d), F32), SDS((t, d), BF16), SDS((8, d), F32)],
        compiler_params=_params(("arbitrary",)))(dh, x, y, dxo, g, scale1p, gw_prev)


def _loss_head(x, target, gf, gw_prev):
    t, d = x.shape
    tr = _row_tile(t)
    nt = t // tr

    def body(x_ref, tg_ref, g_ref, gw_ref, dx_ref, dyp_ref, sums_ref):
        i = pl.program_id(0)

        @pl.when(i == 0)
        def _():
            sums_ref[...] = jnp.zeros_like(sums_ref)

        xv = x_ref[...]
        inv = lax.rsqrt(jnp.mean(xv * xv, axis=-1, keepdims=True) + NORM_EPS)
        xn = xv * inv
        gv = g_ref[...]
        err = xn * gv - tg_ref[...]
        dyv = err * (1.0 / d)
        dxn = dyv * gv
        dx = inv * (dxn - xn * jnp.mean(dxn * xn, axis=-1, keepdims=True))
        dx_ref[...] = dx
        dyp_ref[...] = (gw_ref[...] * dx).astype(dyp_ref.dtype)
        sums_ref[0:1, :] += jnp.sum(dyv * xn, axis=0, keepdims=True)
        sums_ref[1:2, :] += jnp.sum(err * err, axis=0, keepdims=True)

        @pl.when(i == nt - 1)
        def _():
            tot = jnp.sum(sums_ref[1:2, :], axis=1, keepdims=True) * (0.5 / d)
            sums_ref[1:2, :] = jnp.broadcast_to(tot, (1, d))

    blk = pl.BlockSpec((tr, d), lambda i: (i, 0))
    vec = pl.BlockSpec((1, d), lambda i: (0, 0))
    return _pcall(
        body, name="loss_head", grid=(nt,), in_specs=[blk, blk, vec, vec],
        out_specs=[blk, blk, pl.BlockSpec((8, d), lambda i: (0, 0))],
        out_shape=[SDS((t, d), F32), SDS((t, d), BF16), SDS((8, d), F32)],
        compiler_params=_params(("arbitrary",)))(x, target, gf, gw_prev)


HIDDEN_CHUNKS = N_DEV // 2


def _ffn_fwd(tag, lj, x, h, wgu, wd4, gmul):
    t, d = h.shape
    fc, nc = wgu.shape[3], HIDDEN_CHUNKS
    tm, tn, tk = _tile(t, (512, 256, 128)), _tile(d, (512, 256, 128)), _tile(d, (1024, 512, 256, 128))

    def epi_gu(accs, _):
        gpre, up = accs
        return (gpre, up), gpre * jax.nn.sigmoid(gpre) * up

    wblk = (None, None, tk, fc)
    ins = [(h, pl.BlockSpec((tm, tk), lambda i, c, k: (i, k))),
           (wgu, pl.BlockSpec(wblk, lambda i, c, k: (lj, c, k, 0))),
           (wgu, pl.BlockSpec(wblk, lambda i, c, k: (lj, c + nc, k, 0)))]
    gu2, a = _mm(tag + "_gu", ins, [(0, 1, 0), (0, 2, 1)], 2, (tm, fc), [], epi_gu,
                 [SDS((2, nc, t, fc), BF16), SDS((nc, t, fc), BF16)],
                 [pl.BlockSpec((2, None, tm, fc), lambda i, c, k: (0, c, i, 0)),
                  pl.BlockSpec((None, tm, fc), lambda i, c, k: (c, i, 0))],
                 (t // tm, nc, d // tk), "nn")

    def epi_down(accs, ex):
        (yv,), (xv, gm) = accs, ex
        return yv, xv + MACARON_W * gm * yv

    ins = [(a, pl.BlockSpec((None, tm, fc), lambda i, j, k: (k, i, 0))),
           (wd4, pl.BlockSpec((None, None, fc, tn), lambda i, j, k: (lj, k, 0, j))),
           (x, pl.BlockSpec((tm, tn), lambda i, j, k: (i, j))), (gmul, pl.BlockSpec((1, tn), lambda i, j, k: (0, j)))]
    oblk = pl.BlockSpec((tm, tn), lambda i, j, k: (i, j))
    y, x_new = _mm(tag + "_down", ins, [(0, 1, 0)], 1, (tm, tn), [2, 3], epi_down, [SDS((t, d), F32)] * 2,
                   [oblk, oblk], (t // tm, d // tn, nc), "nn")
    return gu2, a, y, x_new


def _ffn_bwd(tag, lj, dy, h, gu2, a, wgu, wd4, g_gu, g_d4):
    t, d = dy.shape
    fc, nc = wgu.shape[3], HIDDEN_CHUNKS
    tm, tn, tk = _tile(t, (512, 256, 128)), _tile(d, (512, 256, 128)), _tile(d, (1024, 512, 256, 128))
    tt = _tile(t, (512, 256, 128))

    def epi_da(accs, ex):
        (da,), (gu,) = accs, ex
        gpre, up = gu[0].astype(F32), gu[1].astype(F32)
        s = jax.nn.sigmoid(gpre)
        silu = gpre * s
        dg = da * up * (s * (1.0 + gpre * (1.0 - s)))
        return ((dg, da * silu),)

    gblk = pl.BlockSpec((2, None, tm, fc), lambda i, c, k: (0, c, i, 0))
    ins = [(dy, pl.BlockSpec((tm, tk), lambda i, c, k: (i, k))),
           (wd4, pl.BlockSpec((None, None, fc, tk), lambda i, c, k: (lj, c, 0, k))), (gu2, gblk)]
    dgu2 = _mm(tag + "_da", ins, [(0, 1, 0)], 1, (tm, fc), [2], epi_da, [SDS((2, nc, t, fc), BF16)], [gblk],
               (t // tm, nc, d // tk), "nt")[0]

    ins = [(a, pl.BlockSpec((None, tt, fc), lambda c, j, k: (c, k, 0))), (dy, pl.BlockSpec((tt, tn), lambda c, j, k: (k, j)))]
    g_d4 = _mm(tag + "_dwd", ins, [(0, 1, 0)], 1, (fc, tn), [], _plain, [SDS(g_d4.shape, BF16)],
               [pl.BlockSpec((None, None, fc, tn), lambda c, j, k: (lj, c, 0, j))], (nc, d // tn, t // tt), "tn", fill=g_d4)[0]

    ins = [(dgu2, pl.BlockSpec((None, None, tm, fc), lambda i, j, k: (k // nc, k % nc, i, 0))),
           (wgu, pl.BlockSpec((None, None, tn, fc), lambda i, j, k: (lj, k, j, 0)))]
    dh = _mm(tag + "_dh", ins, [(0, 1, 0)], 1, (tm, tn), [], _plain, [SDS((t, d), F32)],
             [pl.BlockSpec((tm, tn), lambda i, j, k: (i, j))], (t // tm, d // tn, 2 * nc), "nt")[0]

    ins = [(h, pl.BlockSpec((tt, tn), lambda i, c, k: (k, i))),
           (dgu2, pl.BlockSpec((None, None, tt, fc), lambda i, c, k: (c // nc, c % nc, k, 0)))]
    g_gu = _mm(tag + "_dwgu", ins, [(0, 1, 0)], 1, (tn, fc), [], _plain, [SDS(g_gu.shape, BF16)],
               [pl.BlockSpec((None, None, tn, fc), lambda i, c, k: (lj, c, i, 0))], (d // tn, 2 * nc, t // tt), "tn",
               fill=g_gu)[0]
    return dh, g_gu, g_d4


def _sb_block(t):
    return 256 if t >= 1024 else 128


def _split_hi_lo(v):
    hi = v.astype(BF16)
    return hi, (v - hi.astype(F32)).astype(BF16)


def _sb_fwd(qkv, d):
    t = qkv.shape[0]
    blk = _sb_block(t)
    nq = t // blk
    npair = d // LANES
    scale = HEAD_DIM ** -0.5

    def body(q_ref, k_ref, v_ref, o_ref, l_ref):
        lane = lax.broadcasted_iota(jnp.int32, (blk, LANES), 1)
        head0 = lane < HEAD_DIM
        row = lax.broadcasted_iota(jnp.int32, (blk, blk), 0)
        col = lax.broadcasted_iota(jnp.int32, (blk, blk), 1)
        causal = col < row
        after = (row > col).astype(BF16)

        def tile(qh, kb, carry, masked):
            cl, oacc = carry
            start = pl.multiple_of(kb * blk, blk)
            kv = k_ref[pl.ds(start, blk), :]
            vv = v_ref[pl.ds(start, blk), :]
            z = _dot(qh, kv, "nt") * scale
            sp = _softplus(z)
            lk = jnp.where(causal, -sp, 0.0) if masked else -sp
            hi, lo = _split_hi_lo(lk)
            later = _dot(hi, after, "nn") + _dot(lo, after, "nn") + cl
            logw = z - sp + later
            if masked:
                logw = jnp.where(causal, logw, -1e30)
            w = jnp.exp(logw)
            oacc = oacc + _dot(w.astype(BF16), vv, "nn")
            return cl + jnp.sum(lk, axis=1, keepdims=True), oacc

        def qblock(qi, _):
            qstart = pl.multiple_of(qi * blk, blk)
            qv = q_ref[pl.ds(qstart, blk), :]
            outs = []
            for hh in range(2):
                qh = jnp.where(head0 if hh == 0 else ~head0, qv, jnp.zeros_like(qv))
                carry = (jnp.zeros((blk, 1), F32), jnp.zeros((blk, LANES), F32))
                carry = tile(qh, qi, carry, True)
                carry = lax.fori_loop(0, qi, lambda j, cr: tile(qh, qi - 1 - j, cr, False), carry)
                outs.append(carry)
            o_ref[pl.ds(qstart, blk), :] = jnp.where(head0, outs[0][1], outs[1][1]).astype(o_ref.dtype)
            l_ref[pl.ds(qstart, blk), :] = jnp.where(head0, outs[0][0], outs[1][0])
            return 0

        lax.fori_loop(0, nq, qblock, 0)

    return _pcall(
        body, name="sb_fwd", grid=(npair,),
        in_specs=[pl.BlockSpec((t, LANES), lambda p: (0, p)), pl.BlockSpec((t, LANES), lambda p: (0, npair + p)),
                  pl.BlockSpec((t, LANES), lambda p: (0, 2 * npair + p))],
        out_specs=[pl.BlockSpec((t, LANES), lambda p: (0, p)), pl.BlockSpec((t, LANES), lambda p: (0, p))],
        out_shape=[SDS((t, d), BF16), SDS((t, d), F32)],
        compiler_params=_params(("parallel",)))(qkv, qkv, qkv)


def _sb_bwd(qkv, do, ltot, d):
    t = qkv.shape[0]
    blk = _sb_block(t)
    nq = t // blk
    npair = d // LANES
    scale = HEAD_DIM ** -0.5

    def body(q_ref, k_ref, v_ref, do_ref, l_ref, out_ref, dq_s, dk_s, dv_s):
        lane = lax.broadcasted_iota(jnp.int32, (blk, LANES), 1)
        head0 = lane < HEAD_DIM
        row = lax.broadcasted_iota(jnp.int32, (blk, blk), 0)
        col = lax.broadcasted_iota(jnp.int32, (blk, blk), 1)
        causal = col < row
        upto = (row <= col).astype(BF16)
        before = (row < col).astype(BF16)
        dk_s[...] = jnp.zeros_like(dk_s)
        dv_s[...] = jnp.zeros_like(dv_s)

        def tile(qh, doh, lt, kb, carry, masked):
            plk, pda, dqacc = carry
            start = pl.multiple_of(kb * blk, blk)
            kv = k_ref[pl.ds(start, blk), :]
            vv = v_ref[pl.ds(start, blk), :]
            z = _dot(qh, kv, "nt") * scale
            sp = _softplus(z)
            lk = jnp.where(causal, -sp, 0.0) if masked else -sp
            hi, lo = _split_hi_lo(lk)
            later = lt - (plk + _dot(hi, upto, "nn") + _dot(lo, upto, "nn"))
            logw = z - sp + later
            if masked:
                logw = jnp.where(causal, logw, -1e30)
            w = jnp.exp(logw)
            da = _dot(doh, vv, "nt") * w
            dhi, dlo = _split_hi_lo(da)
            pex = pda + _dot(dhi, before, "nn") + _dot(dlo, before, "nn")
            sig = jnp.exp(z - sp)
            dz = da * (1.0 - sig) - sig * pex
            if masked:
                dz = jnp.where(causal, dz, 0.0)
            dzs = (dz * scale).astype(BF16)
            dqacc = dqacc + _dot(dzs, kv, "nn")
            dk_s[pl.ds(start, blk), :] += _dot(dzs, qh, "tn")
            dv_s[pl.ds(start, blk), :] += _dot(w.astype(BF16), doh, "tn")
            return plk + jnp.sum(lk, axis=1, keepdims=True), pda + jnp.sum(da, axis=1, keepdims=True), dqacc

        def qblock(qi, _):
            qstart = pl.multiple_of(qi * blk, blk)
            qv = q_ref[pl.ds(qstart, blk), :]
            dov = do_ref[pl.ds(qstart, blk), :]
            lv = l_ref[pl.ds(qstart, blk), :]
            dqs = []
            for hh in range(2):
                sel = head0 if hh == 0 else ~head0
                qh = jnp.where(sel, qv, jnp.zeros_like(qv))
                doh = jnp.where(sel, dov, jnp.zeros_like(dov))
                lt = jnp.max(jnp.where(sel, lv, -jnp.inf), axis=1, keepdims=True)
                carry = (jnp.zeros((blk, 1), F32), jnp.zeros((blk, 1), F32), jnp.zeros((blk, LANES), F32))
                carry = lax.fori_loop(0, qi, lambda kb, cr: tile(qh, doh, lt, kb, cr, False), carry)
                carry = tile(qh, doh, lt, qi, carry, True)
                dqs.append(carry[2])
            dq_s[pl.ds(qstart, blk), :] = jnp.where(head0, dqs[0], dqs[1])
            return 0

        lax.fori_loop(0, nq, qblock, 0)
        out_ref[0] = dq_s[...].astype(out_ref.dtype)
        out_ref[1] = dk_s[...].astype(out_ref.dtype)
        out_ref[2] = dv_s[...].astype(out_ref.dtype)

    col_blk = lambda off: pl.BlockSpec((t, LANES), lambda p: (0, off + p))
    return _pcall(
        body, name="sb_bwd", grid=(npair,),
        in_specs=[col_blk(0), col_blk(npair), col_blk(2 * npair), col_blk(0), col_blk(0)],
        out_specs=pl.BlockSpec((3, t, LANES), lambda p: (0, 0, p)),
        out_shape=SDS((3, t, d), BF16),
        scratch_shapes=[pltpu.VMEM((t, LANES), F32) for _ in range(3)],
        compiler_params=_params(("parallel",)))(qkv, qkv, qkv, do, ltot)


def _roll_rows(v, shift):
    return pltpu.roll(v, shift, 0)


def _shift_down(v, dist, fill, row):
    return jnp.where(row >= dist, _roll_rows(v, dist), fill)


def _shift_up(v, dist, fill, row):
    t = v.shape[0]
    return jnp.where(row < t - dist, _roll_rows(v, t - dist), fill)


def _lru_gates(xb, small, wr, wi, row):
    xs = [_shift_down(xb, 3 - tap, 0.0, row) if tap < 3 else xb for tap in range(4)]
    xc = small[4:5, :] + xs[0] * small[0:1, :]
    for tap in range(1, 4):
        xc = xc + xs[tap] * small[tap:tap + 1, :]
    xcb = xc.astype(BF16)
    r = jax.nn.sigmoid(_dot(xcb, wr, "nn") + small[5:6, :])
    ig = jax.nn.sigmoid(_dot(xcb, wi, "nn") + small[6:7, :])
    sp = _softplus(-small[7:8, :])
    la = -LRU_C * r * sp
    a = jnp.exp(la)
    th = jnp.tanh(la)
    mult = jnp.sqrt(-2.0 * th / (1.0 - th))
    return xs, xc, xcb, r, ig, sp, a, mult


def _gelu_parts(gate):
    inner = GELU_C * (gate + GELU_K * gate * gate * gate)
    th = jnp.tanh(inner)
    gelu = 0.5 * gate * (1.0 + th)
    dgelu = 0.5 * (1.0 + th) + 0.5 * gate * (1.0 - th * th) * GELU_C * (1.0 + 3.0 * GELU_K * gate * gate)
    return gelu, dgelu


def _scan_steps(t):
    steps, dist = [], 1
    while dist < t:
        steps.append(dist)
        dist *= 2
    return steps


def _lru_fwd(gx, small, wr, wi):
    t = gx.shape[0]
    r_dim = gx.shape[1] // 2
    nb = r_dim // LRU_BLOCK_W

    def body(gate_ref, xb_ref, small_ref, wr_ref, wi_ref, y_ref, hs_ref):
        row = lax.broadcasted_iota(jnp.int32, (t, LRU_BLOCK_W), 0)
        xb = xb_ref[...]
        _, xc, _, _, ig, _, a, mult = _lru_gates(xb, small_ref, wr_ref[...], wi_ref[...], row)
        b = mult * (ig * xc)
        for dist in _scan_steps(t):
            b = a * _shift_down(b, dist, 0.0, row) + b
            a = a * _shift_down(a, dist, 1.0, row)
        hs_ref[...] = b
        gelu, _ = _gelu_parts(gate_ref[...])
        y_ref[...] = (gelu * b).astype(y_ref.dtype)

    colb = lambda off: pl.BlockSpec((t, LRU_BLOCK_W), lambda n: (0, off + n))
    wspec = pl.BlockSpec((None, LRU_BLOCK_W, LRU_BLOCK_W), lambda n: (n, 0, 0))
    return _pcall(
        body, name="lru_fwd", grid=(nb,),
        in_specs=[colb(0), colb(nb), pl.BlockSpec((8, LRU_BLOCK_W), lambda n: (0, n)), wspec, wspec],
        out_specs=[colb(0), colb(0)], out_shape=[SDS((t, r_dim), BF16), SDS((t, r_dim), F32)],
        compiler_params=_params(("parallel",)))(gx, gx, small, wr, wi)


def _lru_bwd(gx, hs, dy, small, wr, wi):
    t = gx.shape[0]
    r_dim = gx.shape[1] // 2
    nb = r_dim // LRU_BLOCK_W

    def body(gate_ref, xb_ref, hs_ref, dy_ref, small_ref, wr_ref, wi_ref, dgx_ref, dsm_ref, dwr_ref, dwi_ref):
        row = lax.broadcasted_iota(jnp.int32, (t, LRU_BLOCK_W), 0)
        xb, hsv, dyv, smallv = xb_ref[...], hs_ref[...], dy_ref[...], small_ref
        wrv, wiv = wr_ref[...], wi_ref[...]
        xs, xc, xcb, r, ig, sp, a, mult = _lru_gates(xb, smallv, wrv, wiv, row)
        gelu, dgelu = _gelu_parts(gate_ref[...])
        dgx_ref[0] = (dyv * hsv * dgelu).astype(dgx_ref.dtype)
        dacc = dyv * gelu
        an = _shift_up(a, 1, 1.0, row)
        for dist in _scan_steps(t):
            dacc = dacc + an * _shift_up(dacc, dist, 0.0, row)
            an = an * _shift_up(an, dist, 1.0, row)
        da = dacc * _shift_down(hsv, 1, 0.0, row)
        dmult = dacc * (ig * xc)
        dixc = dacc * mult
        dla = da * a - dmult * (a * a) / mult
        dr = dla * (-LRU_C * sp)
        dsp = jnp.sum(dla * (-LRU_C * r), axis=0, keepdims=True)
        dpr = dr * r * (1.0 - r)
        dpi = dixc * xc * ig * (1.0 - ig)
        dprb, dpib = dpr.astype(BF16), dpi.astype(BF16)
        dwr_ref[...] = _dot(xcb, dprb, "tn")
        dwi_ref[...] = _dot(xcb, dpib, "tn")
        dxc = dixc * ig + _dot(dprb, wrv, "nt") + _dot(dpib, wiv, "nt")
        dxb = dxc * smallv[3:4, :]
        for tap in range(3):
            dxb = dxb + _shift_up(dxc, 3 - tap, 0.0, row) * smallv[tap:tap + 1, :]
        dgx_ref[1] = dxb.astype(dgx_ref.dtype)
        lam = smallv[7:8, :]
        rows = [jnp.sum(dxc * xs[tap], axis=0, keepdims=True) for tap in range(4)]
        rows.append(jnp.sum(dxc, axis=0, keepdims=True))
        rows.append(jnp.sum(dpr, axis=0, keepdims=True))
        rows.append(jnp.sum(dpi, axis=0, keepdims=True))
        rows.append(-dsp * jax.nn.sigmoid(-lam))
        for k, rv in enumerate(rows):
            dsm_ref[k:k + 1, :] = rv

    colb = lambda off: pl.BlockSpec((t, LRU_BLOCK_W), lambda n: (0, off + n))
    wspec = pl.BlockSpec((None, LRU_BLOCK_W, LRU_BLOCK_W), lambda n: (n, 0, 0))
    sspec = pl.BlockSpec((8, LRU_BLOCK_W), lambda n: (0, n))
    return _pcall(
        body, name="lru_bwd", grid=(nb,),
        in_specs=[colb(0), colb(nb), colb(0), colb(0), sspec, wspec, wspec],
        out_specs=[pl.BlockSpec((2, t, LRU_BLOCK_W), lambda n: (0, 0, n)), sspec, wspec, wspec],
        out_shape=[SDS((2, t, r_dim), BF16), SDS((8, r_dim), F32), SDS((nb, LRU_BLOCK_W, LRU_BLOCK_W), F32),
                   SDS((nb, LRU_BLOCK_W, LRU_BLOCK_W), F32)],
        compiler_params=_params(("parallel",)))(gx, gx, hs, dy, small, wr, wi)


def _adam(w, g, m, v):
    m2 = ADAM_B1 * m + (1.0 - ADAM_B1) * g
    v2 = ADAM_B2 * v + (1.0 - ADAM_B2) * (g * g)
    m_hat = m2 / (1.0 - ADAM_B1 ** ADAM_STEP)
    v_hat = v2 / (1.0 - ADAM_B2 ** ADAM_STEP)
    return -ADAM_LR * (m_hat / (jnp.sqrt(v_hat) + ADAM_EPS) + ADAM_WD * w), m2, v2


def _mod_fwd(c_all, mod_w, mod_b_cols):
    nl, d, cols = mod_w.shape
    nbatch = c_all.shape[0]

    def body(c_ref, w_ref, b_ref, o_ref):
        cv = c_ref[...]
        ca = (cv * jax.nn.sigmoid(cv)).astype(BF16)
        o_ref[...] = _dot(ca, w_ref[...].astype(BF16), "nn") + b_ref[...]

    return _pcall(
        body, name="mod_fwd", grid=(nl,),
        in_specs=[pl.BlockSpec((nbatch, d), lambda l: (0, 0)), pl.BlockSpec((None, d, cols), lambda l: (l, 0, 0)),
                  pl.BlockSpec((None, 1, cols), lambda l: (l, 0, 0))],
        out_specs=pl.BlockSpec((None, nbatch, cols), lambda l: (l, 0, 0)), out_shape=SDS((nl, nbatch, cols), F32),
        compiler_params=_params(("parallel",)))(c_all, mod_w, mod_b_cols)


def _mod_w_update(c_all, dmod_cols, w, m, v):
    nl, d, cols = w.shape
    nbatch = c_all.shape[0]
    tr = _tile(d, (256, 128))

    def body(c_ref, dm_ref, w_ref, m_ref, v_ref, g_ref, dl_ref, m2_ref, v2_ref):
        cv = c_ref[...]
        ca = (cv * jax.nn.sigmoid(cv)).astype(BF16)
        g = _dot(ca, dm_ref[...].astype(BF16), "tn")
        g_ref[...] = g
        dl_ref[...], m2_ref[...], v2_ref[...] = _adam(w_ref[...], g, m_ref[...], v_ref[...])

    wblk = pl.BlockSpec((None, tr, cols), lambda l, i: (l, i, 0))
    return _pcall(
        body, name="mod_w_update", grid=(nl, d // tr),
        in_specs=[pl.BlockSpec((nbatch, tr), lambda l, i: (0, i)), pl.BlockSpec((None, nbatch, cols), lambda l, i: (l, 0, 0)),
                  wblk, wblk, wblk],
        out_specs=[wblk] * 4, out_shape=[SDS(w.shape, F32)] * 4,
        compiler_params=_params(("parallel", "parallel")))(c_all, dmod_cols, w, m, v)


def _adam_update(name, w, m, v, gparts):
    rows, cols = w.shape
    tr = _tile(rows, (256, 128, 64, 32, 16, 8))
    npart = len(gparts)

    def body(*refs):
        w_ref, m_ref, v_ref = refs[:3]
        g_refs = refs[3:3 + npart]
        g_ref, dl_ref, m2_ref, v2_ref = refs[3 + npart:]
        g = g_refs[0][...].astype(F32)
        for gr in g_refs[1:]:
            g = g + gr[...].astype(F32)
        g_ref[...] = g
        dl_ref[...], m2_ref[...], v2_ref[...] = _adam(w_ref[...], g, m_ref[...], v_ref[...])

    blk = pl.BlockSpec((tr, cols), lambda i: (i, 0))
    return _pcall(body, name=name, grid=(rows // tr,), in_specs=[blk] * (3 + npart), out_specs=[blk] * 4,
                  out_shape=[SDS((rows, cols), F32)] * 4, compiler_params=_params(("parallel",)))(w, m, v, *gparts)


def _adam_shard(name, w, m, v, part4, recv3, chip_idx):
    p, r, cdim = w.shape
    tr = _tile(r, (256, 176, 160, 128, 64, 32, 16))

    def body(chip_ref, w_ref, m_ref, v_ref, own_ref, r0_ref, r1_ref, r2_ref, g_ref, dl_ref, m2_ref, v2_ref):
        g = own_ref[...].astype(F32) + r0_ref[...].astype(F32) + r1_ref[...].astype(F32) + r2_ref[...].astype(F32)
        g_ref[...] = g
        dl_ref[...], m2_ref[...], v2_ref[...] = _adam(w_ref[...], g, m_ref[...], v_ref[...])

    blk = pl.BlockSpec((None, tr, cdim), lambda q, i, chip_ref: (q, i, 0))
    blk4 = (None, None, tr, cdim)
    slot = lambda s: pl.BlockSpec(blk4, lambda q, i, chip_ref: (s, q, i, 0))
    grid_spec = pltpu.PrefetchScalarGridSpec(
        num_scalar_prefetch=1, grid=(p, r // tr),
        in_specs=[blk, blk, blk, pl.BlockSpec(blk4, lambda q, i, chip_ref: (q, chip_ref[0], i, 0)), slot(0), slot(1), slot(2)],
        out_specs=[blk] * 4)
    return _pcall(body, name=name, grid_spec=grid_spec, out_shape=[SDS((p, r, cdim), F32)] * 4,
                  compiler_params=_params(("parallel", "parallel")))(chip_idx, w, m, v, part4, recv3, recv3, recv3)


def _sum_devices(gathered, name):
    _, rows, cols = gathered.shape
    tr = _tile(rows, (512, 256, 128, 64, 32, 16, 8))

    def body(g_ref, o_ref):
        acc = g_ref[0]
        for k in range(1, N_DEV):
            acc = acc + g_ref[k]
        o_ref[...] = acc

    return _pcall(body, name=name, grid=(rows // tr,), in_specs=[pl.BlockSpec((N_DEV, tr, cols), lambda i: (0, i, 0))],
                  out_specs=pl.BlockSpec((tr, cols), lambda i: (i, 0)), out_shape=SDS((rows, cols), F32),
                  compiler_params=_params(("parallel",)))(gathered)


def _pack_flat(parts, width, row_mult, dtype):
    flat = jnp.concatenate([p.reshape(-1).astype(dtype) for p in parts])
    unit = width * row_mult
    pad = (-flat.shape[0]) % unit
    if pad:
        flat = jnp.concatenate([flat, jnp.zeros((pad,), dtype)])
    return flat.reshape(-1, width)


def _unpack_flat(flat, shapes):
    out, off = [], 0
    for shp in shapes:
        size = math.prod(shp)
        out.append(flat[off:off + size].reshape(shp))
        off += size
    return out


def kernel(x, c, mod_w, mod_b, norm_g, ffn_w_gu, ffn_w_down, sb_w_qkv, sb_w_o, lru_w_in, lru_conv_w, lru_conv_b, lru_w_r, lru_b_r, lru_w_i, lru_b_i, lru_lambda, lru_w_out, final_norm_g, loss_target, m_mod_w, m_mod_b, m_norm_g, m_ffn_w_gu, m_ffn_w_down, m_sb_w_qkv, m_sb_w_o, m_lru_w_in, m_lru_conv_w, m_lru_conv_b, m_lru_w_r, m_lru_b_r, m_lru_w_i, m_lru_b_i, m_lru_lambda, m_lru_w_out, m_final_norm_g, v_mod_w, v_mod_b, v_norm_g, v_ffn_w_gu, v_ffn_w_down, v_sb_w_qkv, v_sb_w_o, v_lru_w_in, v_lru_conv_w, v_lru_conv_b, v_lru_w_r, v_lru_b_r, v_lru_w_i, v_lru_b_i, v_lru_lambda, v_lru_w_out, v_final_norm_g):
    weights = dict(mod_w=mod_w, mod_b=mod_b, norm_g=norm_g, ffn_w_gu=ffn_w_gu, ffn_w_down=ffn_w_down, sb_w_qkv=sb_w_qkv,
                   sb_w_o=sb_w_o, lru_w_in=lru_w_in, lru_conv_w=lru_conv_w, lru_conv_b=lru_conv_b, lru_w_r=lru_w_r,
                   lru_b_r=lru_b_r, lru_w_i=lru_w_i, lru_b_i=lru_b_i, lru_lambda=lru_lambda, lru_w_out=lru_w_out,
                   final_norm_g=final_norm_g)
    mom_m = dict(mod_w=m_mod_w, mod_b=m_mod_b, norm_g=m_norm_g, ffn_w_gu=m_ffn_w_gu, ffn_w_down=m_ffn_w_down,
                 sb_w_qkv=m_sb_w_qkv, sb_w_o=m_sb_w_o, lru_w_in=m_lru_w_in, lru_conv_w=m_lru_conv_w,
                 lru_conv_b=m_lru_conv_b, lru_w_r=m_lru_w_r, lru_b_r=m_lru_b_r, lru_w_i=m_lru_w_i, lru_b_i=m_lru_b_i,
                 lru_lambda=m_lru_lambda, lru_w_out=m_lru_w_out, final_norm_g=m_final_norm_g)
    mom_v = dict(mod_w=v_mod_w, mod_b=v_mod_b, norm_g=v_norm_g, ffn_w_gu=v_ffn_w_gu, ffn_w_down=v_ffn_w_down,
                 sb_w_qkv=v_sb_w_qkv, sb_w_o=v_sb_w_o, lru_w_in=v_lru_w_in, lru_conv_w=v_lru_conv_w,
                 lru_conv_b=v_lru_conv_b, lru_w_r=v_lru_w_r, lru_b_r=v_lru_b_r, lru_w_i=v_lru_w_i, lru_b_i=v_lru_b_i,
                 lru_lambda=v_lru_lambda, lru_w_out=v_lru_w_out, final_norm_g=v_final_norm_g)
    names = list(weights)

    t, d = x.shape[1], x.shape[2]
    n_layers = mod_w.shape[0]
    r_dim = lru_w_out.shape[1] * N_DEV
    ng, rs = d // N_DEV, r_dim // N_DEV
    mod_cols = mod_w.shape[2]
    nblk = lru_w_r.shape[1]
    xi, yi, ci = _mesh_pos()
    me = 4 * xi + 2 * yi + ci
    chip = 2 * xi + yi
    x2, target = x.reshape(t, d), loss_target.reshape(t, d)

    lru_small_shard = jnp.concatenate([lru_conv_w[0], lru_conv_b, lru_b_r, lru_b_i, lru_lambda], axis=0)
    small1 = _pack_flat([c, norm_g, lru_small_shard], LANES, 8, F32)
    n_small1 = small1.shape[0]
    all1 = _allgather(small1[None], "gather_small").reshape(N_DEV, n_small1 * LANES)
    c_all = all1[:, :d]
    norm_full = jnp.transpose(all1[:, d:d + 6 * ng].reshape(N_DEV, n_layers, 3, ng), (1, 2, 0, 3)).reshape(n_layers, 3, d)
    lru_small = jnp.transpose(all1[:, d + 6 * ng:d + 6 * ng + 8 * rs].reshape(N_DEV, 8, rs), (1, 0, 2)).reshape(8, r_dim)

    mod_b_cols = lax.dynamic_slice_in_dim(mod_b, me * mod_cols, mod_cols, axis=1).reshape(n_layers, 1, mod_cols)
    mod_part = _mod_fwd(c_all, mod_w, mod_b_cols)
    mod_all = _allgather(mod_part, "gather_mod")
    mod_mine = lax.dynamic_index_in_dim(mod_all, me, axis=2, keepdims=False)
    mod_mine = mod_mine.reshape(n_layers, 3, 3, d)

    assert sb_w_qkv.shape[0] == 1 and lru_w_in.shape[0] == 1, "one stick-breaking and one RG-LRU layer"
    big = ["ffn_w_gu", "ffn_w_down", "sb_w_qkv", "sb_w_o", "lru_w_in", "lru_w_out"]
    shard3 = {n: weights[n].reshape((-1,) + weights[n].shape[-2:]) for n in big}
    gathered = {n: _allgather(shard3[n].astype(BF16), "gather_" + n) for n in big}
    n_ffn = shard3["ffn_w_gu"].shape[0]
    fc = shard3["ffn_w_gu"].shape[2]
    w_gu = gathered["ffn_w_gu"]
    w_d4 = gathered["ffn_w_down"].reshape(n_ffn, HIDDEN_CHUNKS, fc, d)
    w_qkv = gathered["sb_w_qkv"][0]
    w_o = gathered["sb_w_o"].reshape(d, d)
    cw_in = shard3["lru_w_in"].shape[2]
    eye3 = jnp.eye(N_DEV * cw_in, dtype=BF16).reshape(N_DEV, cw_in, N_DEV * cw_in)
    w_in = _chunks_to_cols("lru_w_in_cols", gathered["lru_w_in"][0], eye3)
    w_out = gathered["lru_w_out"].reshape(r_dim, d)
    wr_b, wi_b = lru_w_r[0].astype(BF16), lru_w_i[0].astype(BF16)

    saved = []
    xcur = x2
    for layer in range(n_layers):
        for sub in range(3):
            gvec = norm_full[layer, sub].reshape(1, d)
            shift = mod_mine[layer, sub, 0].reshape(1, d)
            scale1p = 1.0 + mod_mine[layer, sub, 1].reshape(1, d)
            gmul = 1.0 + mod_mine[layer, sub, 2].reshape(1, d)
            tag = f"l{layer}s{sub}"
            h = _norm_fwd(xcur, gvec, scale1p, shift, tag + "_norm")
            rec = dict(x=xcur, h=h, g=gvec, scale1p=scale1p, gmul=gmul, w=MACARON_W if sub != 1 else 1.0)
            if sub != 1:
                lj = layer * 2 + sub // 2
                gu2, a, yv, xcur = _ffn_fwd(tag, lj, xcur, h, w_gu, w_d4, gmul)
                rec.update(kind="ffn", lj=lj, gu2=gu2, a=a, y=yv)
            elif layer % 2 == 0:
                qkv = _mm_nn_chunks(tag + "_qkv", h, w_qkv, BF16)
                o, ltot = _sb_fwd(qkv, d)
                yv, xcur = _mm_nn(tag + "_wo", o, w_o, F32, extras=[(xcur, "tile"), (gmul, "row")],
                                  epilogue=lambda accs, ex: (accs[0], ex[0] + ex[1] * accs[0]), n_out=2)
                rec.update(kind="sb", qkv=qkv, o=o, ltot=ltot, y=yv)
            else:
                gx = _mm_nn(tag + "_win", h, w_in, F32)[0]
                ymix, hs = _lru_fwd(gx, lru_small, wr_b, wi_b)
                yv, xcur = _mm_nn(tag + "_wout", ymix, w_out, F32, extras=[(xcur, "tile"), (gmul, "row")],
                                  epilogue=lambda accs, ex: (accs[0], ex[0] + ex[1] * accs[0]), n_out=2)
                rec.update(kind="lru", gx=gx, hs=hs, ymix=ymix, y=yv)
            saved.append(rec)

    last = saved[-1]
    dxo, dy, head_sums = _loss_head(xcur, target, final_norm_g.reshape(1, d), (last["w"] * last["gmul"]))
    loss = lax.psum(head_sums[1, 0], ("x", "y", "c"))
    dgf = head_sums[0]

    grads = {}
    g_gu = jnp.zeros(w_gu.shape, BF16)
    g_d4 = jnp.zeros(w_d4.shape, BF16)
    dmod = [[None] * 3 for _ in range(n_layers)]
    dnorm = [[None] * 3 for _ in range(n_layers)]
    dlru_small = dwr = dwi = None
    for idx in reversed(range(len(saved))):
        rec = saved[idx]
        layer, sub = divmod(idx, 3)
        tag = f"l{layer}s{sub}b"
        if rec["kind"] == "ffn":
            dh, g_gu, g_d4 = _ffn_bwd(tag, rec["lj"], dy, rec["h"], rec["gu2"], rec["a"], w_gu, w_d4, g_gu, g_d4)
        elif rec["kind"] == "sb":
            do = _mm_nt(tag + "_do", dy, w_o, BF16)
            grads["sb_w_o"] = _mm_tn(tag + "_dwo", rec["o"], dy, BF16).reshape(gathered["sb_w_o"].shape)
            dqkv3 = _sb_bwd(rec["qkv"], do, rec["ltot"], d)
            dh = _mm_nt_stack_chunks(tag + "_dh", dqkv3, w_qkv, F32)
            grads["sb_w_qkv"] = _mm_tn_stack_chunks(tag + "_dwqkv", rec["h"], dqkv3, w_qkv.shape[2], BF16)[None]
        else:
            dymix = _mm_nt(tag + "_dymix", dy, w_out, F32)
            grads["lru_w_out"] = _mm_tn(tag + "_dwout", rec["ymix"], dy, BF16).reshape(gathered["lru_w_out"].shape)
            dgx2, dlru_small, dwr, dwi = _lru_bwd(rec["gx"], rec["hs"], dymix, lru_small, wr_b, wi_b)
            dh = _mm_nt_stack(tag + "_dh", dgx2, w_in, F32)
            dw_in = _mm_tn_stack(tag + "_dwin", rec["h"], dgx2, BF16)
            grads["lru_w_in"] = _cols_to_chunks("lru_w_in_chunks", dw_in, eye3)[None]
        prev = saved[idx - 1] if idx > 0 else None
        gw_prev = (prev["w"] * prev["gmul"]) if prev is not None else jnp.zeros((1, d), F32)
        dxo, dy, sums = _adaln_bwd(dh, rec["x"], rec["y"], dxo, rec["g"], rec["scale1p"], rec["w"], gw_prev, tag + "_adaln")
        dmod[layer][sub] = sums[0:3]
        dnorm[layer][sub] = sums[3]
    grad_x = dxo.reshape(x.shape)

    dmod_mine = jnp.stack([jnp.stack(dmod[layer]) for layer in range(n_layers)])
    dnorm_mine = jnp.stack([jnp.stack(dnorm[layer]) for layer in range(n_layers)])
    small_shapes = [(n_layers, 9 * d), (n_layers, 3, d), (8, r_dim), (d,), (nblk, LRU_BLOCK_W, LRU_BLOCK_W),
                    (nblk, LRU_BLOCK_W, LRU_BLOCK_W)]
    small3 = _pack_flat([dmod_mine, dnorm_mine, dlru_small, dgf, dwr, dwi], LANES, 512, F32)
    n_small3 = small3.shape[0]
    all3 = _allgather(small3[None], "gather_small_grads").reshape(N_DEV, n_small3, LANES)
    gsum = _sum_devices(all3, "sum_small_grads").reshape(-1)
    g_mod_b, g_norm_full, g_lru_small, g_final, g_wr, g_wi = _unpack_flat(gsum, small_shapes)
    dmod_all = all3.reshape(N_DEV, -1)[:, :n_layers * 9 * d].reshape(N_DEV, n_layers, N_DEV, mod_cols)
    dmod_cols = jnp.transpose(lax.dynamic_index_in_dim(dmod_all, me, axis=2, keepdims=False), (1, 0, 2))

    out_g, out_d, out_m, out_v = {}, {}, {}, {}
    out_g["mod_w"], out_d["mod_w"], out_m["mod_w"], out_v["mod_w"] = _mod_w_update(c_all, dmod_cols, mod_w, m_mod_w, v_mod_w)

    g_norm_shard = lax.dynamic_slice_in_dim(g_norm_full, me * ng, ng, axis=2)
    g_lru_shard = lax.dynamic_slice_in_dim(g_lru_small, me * rs, rs, axis=1)
    small_grads = dict(mod_b=g_mod_b, norm_g=g_norm_shard, lru_conv_w=g_lru_shard[0:4].reshape(lru_conv_w.shape),
                       lru_conv_b=g_lru_shard[4:5], lru_b_r=g_lru_shard[5:6], lru_b_i=g_lru_shard[6:7],
                       lru_lambda=g_lru_shard[7:8], final_norm_g=g_final, lru_w_r=g_wr.reshape(lru_w_r.shape),
                       lru_w_i=g_wi.reshape(lru_w_i.shape))
    small_names = list(small_grads)
    sw = _pack_flat([weights[n] for n in small_names], LANES, 256, F32)
    sg = _pack_flat([small_grads[n] for n in small_names], LANES, 256, F32)
    sm = _pack_flat([mom_m[n] for n in small_names], LANES, 256, F32)
    sv = _pack_flat([mom_v[n] for n in small_names], LANES, 256, F32)
    s_outs = _adam_update("adam_small", sw, sm, sv, [sg])
    small_shapes2 = [weights[n].shape for n in small_names]
    for dst, flat in zip((out_g, out_d, out_m, out_v), s_outs):
        for n, arr in zip(small_names, _unpack_flat(flat.reshape(-1), small_shapes2)):
            dst[n] = arr

    grads["ffn_w_gu"] = g_gu
    grads["ffn_w_down"] = g_d4.reshape(gathered["ffn_w_down"].shape)
    c_idx = jnp.reshape(ci, (1,)).astype(jnp.int32)
    chip_idx = jnp.reshape(chip, (1,)).astype(jnp.int32)
    recv4 = _exchange([grads[n] for n in big], 4, _sibling_route, "rs_sibling")
    part4 = [_pair_sum(grads[n], r4, c_idx, "rs_pair_sum_" + n) for n, r4 in zip(big, recv4)]
    recv3 = _exchange(part4, 3, _chip_route, "rs_chips")
    for n, p4, r3 in zip(big, part4, recv3):
        shp = weights[n].shape
        view = lambda arr: arr.reshape(shard3[n].shape)
        outs = _adam_shard("adam_" + n, view(weights[n]), view(mom_m[n]), view(mom_v[n]), p4, r3, chip_idx)
        out_g[n], out_d[n], out_m[n], out_v[n] = [o.reshape(shp) for o in outs]

    return (loss, grad_x, *[out_g[n] for n in names], *[out_d[n] for n in names], *[out_m[n] for n in names],
            *[out_v[n] for n in names])
```

```python
import functools
import math

import jax
import jax.numpy as jnp
from jax import lax
from jax.experimental import pallas as pl
from jax.experimental.pallas import tpu as pltpu

F32 = jnp.float32
BF16 = jnp.bfloat16
SDS = jax.ShapeDtypeStruct
MESH = pl.DeviceIdType.MESH
ANY = pl.BlockSpec(memory_space=pl.ANY)

N_DEV = 8
LANES = 128
HEAD_DIM = 64
LRU_BLOCK_W = 128
LRU_C = 8.0
MACARON_W = 0.5
NORM_EPS = 1e-6
ADAM_LR = 0.001
ADAM_B1 = 0.9
ADAM_B2 = 0.999
ADAM_EPS = 1e-08
ADAM_WD = 0.01
ADAM_STEP = 10
VMEM_LIMIT = 56 * 1024 * 1024
GELU_C = math.sqrt(2.0 / math.pi)
GELU_K = 0.044715

DIMS = {
    "nn": (((1,), (0,)), ((), ())),
    "nt": (((1,), (1,)), ((), ())),
    "tn": (((0,), (0,)), ((), ())),
}


def _pcall(body, **kw):
    return pl.pallas_call(body, **kw)


def _params(sem=None):
    return pltpu.CompilerParams(dimension_semantics=sem, vmem_limit_bytes=VMEM_LIMIT)


def _tile(n, prefs):
    for p in prefs:
        if n % p == 0:
            return p
    return n


def _dot(a, b, dims):
    return lax.dot_general(a, b, DIMS[dims], preferred_element_type=F32)


def _softplus(z):
    return jnp.maximum(z, 0.0) + jnp.log(1.0 + jnp.exp(-jnp.abs(z)))


def _mesh_pos():
    return lax.axis_index("x"), lax.axis_index("y"), lax.axis_index("c")


def _allgather(xs, name, cols=False):
    return _run_comm(_gather_plan([xs], [cols]), name)[0]


class _CommPlan:
    def __init__(self, ins, outs, n_remote, n_local, phases):
        self.ins, self.outs, self.n_remote, self.n_local, self.phases = ins, outs, n_remote, n_local, phases

    def scratch(self):
        return [pltpu.SemaphoreType.DMA((self.n_remote,)), pltpu.SemaphoreType.DMA((self.n_remote,)),
                pltpu.SemaphoreType.DMA((max(self.n_local, 1),))]


def _run_comm(plan, name):
    n_in, n_out = len(plan.ins), len(plan.outs)

    def body(*refs):
        in_refs, out_refs, sems = refs[:n_in], refs[n_in:n_in + n_out], refs[n_in + n_out:]
        for phase in plan.phases:
            phase(in_refs, out_refs, *sems)

    return _pcall(body, name=name, out_shape=plan.outs, in_specs=[ANY] * n_in, out_specs=[ANY] * n_out,
                  scratch_shapes=plan.scratch())(*plan.ins)


def _col_window(ref, idx, width):
    return ref.at[:, :, pl.ds(pl.multiple_of(idx * width, math.gcd(width, LANES)), width)]


def _gather_plan(shards, cols):
    n = len(shards)
    outs = [SDS((s.shape[0], s.shape[1], N_DEV * s.shape[2]) if cl else (s.shape[0], N_DEV) + s.shape[1:], s.dtype)
            for s, cl in zip(shards, cols)]

    def copies(a, in_refs, out_refs, send_sems, recv_sems, local_sems):
        x, y, c = _mesh_pos()
        sibling = (x, y, 1 - c)
        chips = [(1 - x, y), (x, 1 - y), (1 - x, 1 - y)]
        width = shards[a].shape[2]

        def block(px, py, pc):
            idx = 4 * px + 2 * py + pc
            return _col_window(out_refs[a], idx, width) if cols[a] else out_refs[a].at[:, idx]

        def copy(k, owner, to, src=None):
            return pltpu.make_async_remote_copy(
                src_ref=block(*owner) if src is None else src, dst_ref=block(*owner),
                send_sem=send_sems.at[7 * a + k], recv_sem=recv_sems.at[7 * a + k], device_id=to, device_id_type=MESH)

        me = (x, y, c)
        first = [copy(0, me, sibling, src=in_refs[a])]
        first += [copy(1 + j, me, (*chip, c), src=in_refs[a]) for j, chip in enumerate(chips)]
        passed = [copy(4 + j, (*chip, c), sibling) for j, chip in enumerate(chips)]
        landed = [copy(1 + j, (*chip, c), me) for j, chip in enumerate(chips)]
        from_sibling = [copy(0, sibling, me)] + [copy(4 + j, (*chip, 1 - c), me) for j, chip in enumerate(chips)]
        mine = pltpu.make_async_copy(in_refs[a], block(*me), local_sems.at[a])
        return first, passed, landed, from_sibling, mine

    def start(*refs):
        for a in range(n):
            first, _, _, _, mine = copies(a, *refs)
            mine.start()
            for cp in first:
                cp.start()

    def pass_on(*refs):
        for a in range(n):
            _, passed, landed, _, _ = copies(a, *refs)
            for cp, fwd in zip(landed, passed):
                cp.wait_recv()
                fwd.start()

    def finish(*refs):
        for a in range(n):
            first, passed, _, from_sibling, mine = copies(a, *refs)
            for cp in from_sibling:
                cp.wait_recv()
            for cp in first + passed:
                cp.wait_send()
            mine.wait()

    return _CommPlan(list(shards), outs, 7 * n, n, [start, pass_on, finish])


def _exchange_plan(srcs, cols, n_slots, route):
    n = len(srcs)
    outs = []
    for g, cl in zip(srcs, cols):
        shard = (g.shape[0], g.shape[1], g.shape[2] // N_DEV) if cl else (g.shape[0],) + g.shape[2:]
        outs.append(SDS((n_slots,) + shard, g.dtype))

    def copies(in_refs, out_refs, send_sems, recv_sems, local_sems):
        x, y, c = _mesh_pos()
        made = []
        for a in range(n):
            for s in range(n_slots):
                chunk, target = route(x, y, c, s)
                src = _col_window(in_refs[a], chunk, outs[a].shape[3]) if cols[a] else in_refs[a].at[:, chunk]
                made.append(pltpu.make_async_remote_copy(
                    src_ref=src, dst_ref=out_refs[a].at[s], send_sem=send_sems.at[a * n_slots + s],
                    recv_sem=recv_sems.at[a * n_slots + s], device_id=target, device_id_type=MESH))
        return made

    def start(*refs):
        for cp in copies(*refs):
            cp.start()

    def nothing(*refs):
        pass

    def finish(*refs):
        made = copies(*refs)
        for cp in made:
            cp.wait_recv()
        for cp in made:
            cp.wait_send()

    return _CommPlan(list(srcs), outs, n * n_slots, 0, [start, nothing, finish])


def _sibling_route(x, y, c, k):
    return 2 * k + 1 - c, (x, y, 1 - c)


def _chip_route(x, y, c, j):
    px, py = [(1 - x, y), (x, 1 - y), (1 - x, 1 - y)][j]
    return 2 * px + py, (px, py, c)


def _pair_sum(grads, recv4, c_idx, name, cols=False):
    _, p, r, cdim = recv4.shape
    tr = _tile(r, (512, 256, 176, 160, 128, 64, 32, 16))

    def body(c_ref, a_ref, b_ref, o_ref):
        o_ref[...] = (a_ref[...].astype(F32) + b_ref[...].astype(F32)).astype(o_ref.dtype)

    blk = (None, None, tr, cdim)
    if cols:
        own = pl.BlockSpec((None, tr, cdim), lambda k, q, i, c_ref: (q, i, 2 * k + c_ref[0]))
    else:
        own = pl.BlockSpec(blk, lambda k, q, i, c_ref: (q, 2 * k + c_ref[0], i, 0))
    grid_spec = pltpu.PrefetchScalarGridSpec(
        num_scalar_prefetch=1, grid=(4, p, r // tr),
        in_specs=[own, pl.BlockSpec(blk, lambda k, q, i, c_ref: (k, q, i, 0))],
        out_specs=pl.BlockSpec(blk, lambda k, q, i, c_ref: (q, k, i, 0)))
    return _pcall(body, name=name, grid_spec=grid_spec, out_shape=SDS((p, 4, r, cdim), grads.dtype),
                  compiler_params=_params(("parallel", "parallel", "parallel")))(c_idx, grads, recv4)


def _mm(name, ins, prods, n_acc, acc_shape, epi_idx, epilogue, out_shapes, out_specs, grid, dims, fill=None):
    n_in, n_out, nk = len(ins), len(out_shapes), grid[2]
    aliases = {}
    if fill is not None:
        ins = list(ins) + [(fill, ANY)]
        aliases = {n_in: 0}

    n_refs_in = len(ins)

    def body(*refs):
        in_refs, out_refs, acc_refs = refs[:n_in], refs[n_refs_in:n_refs_in + n_out], refs[n_refs_in + n_out:]

        def finish(accs):
            outs = epilogue(accs, [in_refs[i][...] for i in epi_idx])
            for o_ref, o in zip(out_refs, outs):
                if isinstance(o, tuple):
                    for plane, part in enumerate(o):
                        o_ref[plane] = part.astype(o_ref.dtype)
                else:
                    o_ref[...] = o.astype(o_ref.dtype)

        if nk == 1:
            accs = [None] * n_acc
            for ia, ib, iacc in prods:
                term = _dot(in_refs[ia][...], in_refs[ib][...], dims)
                accs[iacc] = term if accs[iacc] is None else accs[iacc] + term
            finish(accs)
            return

        k = pl.program_id(2)

        @pl.when(k == 0)
        def _():
            for acc in acc_refs:
                acc[...] = jnp.zeros_like(acc)

        for ia, ib, iacc in prods:
            acc_refs[iacc][...] += _dot(in_refs[ia][...], in_refs[ib][...], dims)

        @pl.when(k == nk - 1)
        def _():
            finish([acc[...] for acc in acc_refs])

    return _pcall(
        body, name=name, grid=grid, in_specs=[s for _, s in ins], out_specs=out_specs, out_shape=out_shapes,
        scratch_shapes=[pltpu.VMEM(acc_shape, F32) for _ in range(n_acc if nk > 1 else 0)], input_output_aliases=aliases,
        compiler_params=_params(("parallel", "parallel", "arbitrary")),
    )(*[a for a, _ in ins])


def _plain(accs, _):
    return accs


def _mm_nn(name, a, b, out_dtype, extras=(), epilogue=_plain, n_out=1):
    m, kd = a.shape
    n = b.shape[1]
    tm, tn, tk = _tile(m, (1024, 512, 256, 128)), _tile(n, (640, 512, 256, 128)), _tile(kd, (1280, 1024, 512, 256, 128))
    ins = [(a, pl.BlockSpec((tm, tk), lambda i, j, k: (i, k))), (b, pl.BlockSpec((tk, tn), lambda i, j, k: (k, j)))]
    for arr, kind in extras:
        if kind == "tile":
            ins.append((arr, pl.BlockSpec((tm, tn), lambda i, j, k: (i, j))))
        else:
            ins.append((arr, pl.BlockSpec((1, tn), lambda i, j, k: (0, j))))
    dts = out_dtype if isinstance(out_dtype, (list, tuple)) else [out_dtype] * n_out
    return _mm(name, ins, [(0, 1, 0)], 1, (tm, tn), list(range(2, len(ins))), epilogue,
               [SDS((m, n), dt) for dt in dts], [pl.BlockSpec((tm, tn), lambda i, j, k: (i, j)) for _ in dts],
               (m // tm, n // tn, kd // tk), "nn")


def _mm_nt(name, a, b, out_dtype):
    m, kd = a.shape
    n = b.shape[0]
    tm, tn, tk = _tile(m, (1024, 512, 256, 128)), _tile(n, (640, 512, 256, 128)), _tile(kd, (1024, 512, 256, 128))
    ins = [(a, pl.BlockSpec((tm, tk), lambda i, j, k: (i, k))), (b, pl.BlockSpec((tn, tk), lambda i, j, k: (j, k)))]
    return _mm(name, ins, [(0, 1, 0)], 1, (tm, tn), [], _plain, [SDS((m, n), out_dtype)],
               [pl.BlockSpec((tm, tn), lambda i, j, k: (i, j))], (m // tm, n // tn, kd // tk), "nt")[0]


def _mm_tn(name, a, b, out_dtype):
    t, m = a.shape
    n = b.shape[1]
    tm, tn, tk = _tile(m, (640, 512, 256, 128)), _tile(n, (1024, 512, 256, 128)), t
    ins = [(a, pl.BlockSpec((tk, tm), lambda i, j, k: (k, i))), (b, pl.BlockSpec((tk, tn), lambda i, j, k: (k, j)))]
    return _mm(name, ins, [(0, 1, 0)], 1, (tm, tn), [], _plain, [SDS((m, n), out_dtype)],
               [pl.BlockSpec((tm, tn), lambda i, j, k: (i, j))], (m // tm, n // tn, t // tk), "tn")[0]


def _mm_nt_stack(name, a3, b, out_dtype):
    cc, m, kd = a3.shape
    n = b.shape[0]
    tm, tn, tk = _tile(m, (1024, 512, 256, 128)), _tile(n, (1024, 512, 256, 128)), _tile(kd, (1280, 1024, 512, 256, 128))
    nk = kd // tk
    ins = [(a3, pl.BlockSpec((None, tm, tk), lambda i, j, k: (k // nk, i, k % nk))),
           (b, pl.BlockSpec((tn, tk), lambda i, j, k: (j, k)))]
    return _mm(name, ins, [(0, 1, 0)], 1, (tm, tn), [], _plain, [SDS((m, n), out_dtype)],
               [pl.BlockSpec((tm, tn), lambda i, j, k: (i, j))], (m // tm, n // tn, cc * nk), "nt")[0]


def _mm_tn_stack(name, a, b3, out_dtype):
    t, m = a.shape
    cc, _, n = b3.shape
    tm, tn, tk = _tile(m, (512, 256, 128)), _tile(n, (1280, 1024, 512, 256, 128)), t
    nj = n // tn
    ins = [(a, pl.BlockSpec((tk, tm), lambda i, j, k: (k, i))),
           (b3, pl.BlockSpec((None, tk, tn), lambda i, j, k: (j // nj, k, j % nj)))]
    return _mm(name, ins, [(0, 1, 0)], 1, (tm, tn), [], _plain, [SDS((m, cc * n), out_dtype)],
               [pl.BlockSpec((tm, tn), lambda i, j, k: (i, j))], (m // tm, cc * nj, t // tk), "tn")[0]


def _chunks_to_cols(name, wc, eye2):
    nch, d, cw = wc.shape
    tm = _tile(d, (1024, 512, 256, 128))
    ins = [(wc, pl.BlockSpec((None, tm, cw), lambda i, j, k: (2 * j + k, i, 0))),
           (eye2, pl.BlockSpec((None, cw, 2 * cw), lambda i, j, k: (k, 0, 0)))]
    return _mm(name, ins, [(0, 1, 0)], 1, (tm, 2 * cw), [], _plain, [SDS((d, nch * cw), wc.dtype)],
               [pl.BlockSpec((tm, 2 * cw), lambda i, j, k: (i, j))], (d // tm, nch // 2, 2), "nn")[0]


def _cols_to_chunks(name, full, eye2):
    d, n = full.shape
    _, cw, _ = eye2.shape
    nch = n // cw
    tm = _tile(d, (1024, 512, 256, 128))
    ins = [(full, pl.BlockSpec((tm, 2 * cw), lambda i, j, k: (i, j // 2))),
           (eye2, pl.BlockSpec((None, cw, 2 * cw), lambda i, j, k: (j % 2, 0, 0)))]
    return _mm(name, ins, [(0, 1, 0)], 1, (tm, cw), [], _plain, [SDS((nch, d, cw), full.dtype)],
               [pl.BlockSpec((None, tm, cw), lambda i, j, k: (j, i, 0))], (d // tm, nch, 1), "nt")[0]


def _row_tile(t):
    return _tile(t, (256, 128, 64, 32, 16, 8))


def _norm_fwd(x, g, scale1p, shift, name):
    t, d = x.shape
    tr = _row_tile(t)

    def body(x_ref, g_ref, s_ref, b_ref, h_ref):
        xv = x_ref[...]
        inv = lax.rsqrt(jnp.mean(xv * xv, axis=-1, keepdims=True) + NORM_EPS)
        h_ref[...] = ((xv * inv) * g_ref[...] * s_ref[...] + b_ref[...]).astype(h_ref.dtype)

    vec = pl.BlockSpec((1, d), lambda i: (0, 0))
    return _pcall(body, name=name, grid=(t // tr,), in_specs=[pl.BlockSpec((tr, d), lambda i: (i, 0)), vec, vec, vec],
                  out_specs=pl.BlockSpec((tr, d), lambda i: (i, 0)), out_shape=SDS((t, d), BF16),
                  compiler_params=_params(("parallel",)))(x, g, scale1p, shift)


def _adaln_bwd(dh, x, y, dxo, g, scale1p, w_sub, gw_prev, name):
    t, d = x.shape
    tr = _row_tile(t)

    def body(dh_ref, x_ref, y_ref, dxo_ref, g_ref, s_ref, gw_ref, dx_ref, dyp_ref, sums_ref):
        i = pl.program_id(0)

        @pl.when(i == 0)
        def _():
            sums_ref[...] = jnp.zeros_like(sums_ref)

        xv, dhv, dxov = x_ref[...], dh_ref[...], dxo_ref[...]
        inv = lax.rsqrt(jnp.mean(xv * xv, axis=-1, keepdims=True) + NORM_EPS)
        xn = xv * inv
        gv = g_ref[...]
        dn = dhv * s_ref[...]
        dxn = dn * gv
        dx = inv * (dxn - xn * jnp.mean(dxn * xn, axis=-1, keepdims=True)) + dxov
        dx_ref[...] = dx
        dyp_ref[...] = (gw_ref[...] * dx).astype(dyp_ref.dtype)
        sums_ref[0:1, :] += jnp.sum(dhv, axis=0, keepdims=True)
        sums_ref[1:2, :] += jnp.sum(dhv * (xn * gv), axis=0, keepdims=True)
        sums_ref[2:3, :] += jnp.sum(w_sub * y_ref[...] * dxov, axis=0, keepdims=True)
        sums_ref[3:4, :] += jnp.sum(dn * xn, axis=0, keepdims=True)

    blk = pl.BlockSpec((tr, d), lambda i: (i, 0))
    vec = pl.BlockSpec((1, d), lambda i: (0, 0))
    return _pcall(
        body, name=name, grid=(t // tr,), in_specs=[blk, blk, blk, blk, vec, vec, vec],
        out_specs=[blk, blk, pl.BlockSpec((8, d), lambda i: (0, 0))],
        out_shape=[SDS((t, d), F32), SDS((t, d), BF16), SDS((8, d), F32)],
        compiler_params=_params(("arbitrary",)))(dh, x, y, dxo, g, scale1p, gw_prev)


def _loss_head(x, target, gf, gw_prev):
    t, d = x.shape
    tr = _row_tile(t)
    nt = t // tr

    def body(x_ref, tg_ref, g_ref, gw_ref, dx_ref, dyp_ref, sums_ref):
        i = pl.program_id(0)

        @pl.when(i == 0)
        def _():
            sums_ref[...] = jnp.zeros_like(sums_ref)

        xv = x_ref[...]
        inv = lax.rsqrt(jnp.mean(xv * xv, axis=-1, keepdims=True) + NORM_EPS)
        xn = xv * inv
        gv = g_ref[...]
        err = xn * gv - tg_ref[...]
        dyv = err * (1.0 / d)
        dxn = dyv * gv
        dx = inv * (dxn - xn * jnp.mean(dxn * xn, axis=-1, keepdims=True))
        dx_ref[...] = dx
        dyp_ref[...] = (gw_ref[...] * dx).astype(dyp_ref.dtype)
        sums_ref[0:1, :] += jnp.sum(dyv * xn, axis=0, keepdims=True)
        sums_ref[1:2, :] += jnp.sum(err * err, axis=0, keepdims=True)

        @pl.when(i == nt - 1)
        def _():
            tot = jnp.sum(sums_ref[1:2, :], axis=1, keepdims=True) * (0.5 / d)
            sums_ref[1:2, :] = jnp.broadcast_to(tot, (1, d))

    blk = pl.BlockSpec((tr, d), lambda i: (i, 0))
    vec = pl.BlockSpec((1, d), lambda i: (0, 0))
    return _pcall(
        body, name="loss_head", grid=(nt,), in_specs=[blk, blk, vec, vec],
        out_specs=[blk, blk, pl.BlockSpec((8, d), lambda i: (0, 0))],
        out_shape=[SDS((t, d), F32), SDS((t, d), BF16), SDS((8, d), F32)],
        compiler_params=_params(("arbitrary",)))(x, target, gf, gw_prev)


HIDDEN_CHUNKS = N_DEV // 2


def _ffn_fwd(tag, lj, x, h, wgu, wd4, gmul):
    t, d = h.shape
    fc, nc = wgu.shape[3], HIDDEN_CHUNKS
    tm, tn, tk = _tile(t, (1024, 512, 256, 128)), _tile(d, (1024, 512, 256, 128)), _tile(d, (1024, 512, 256, 128))

    def epi_gu(accs, _):
        gpre, up = accs
        return (gpre, up), gpre * jax.nn.sigmoid(gpre) * up

    wblk = (None, None, tk, fc)
    ins = [(h, pl.BlockSpec((tm, tk), lambda i, c, k: (i, k))),
           (wgu, pl.BlockSpec(wblk, lambda i, c, k: (lj, c, k, 0))),
           (wgu, pl.BlockSpec(wblk, lambda i, c, k: (lj, c + nc, k, 0)))]
    gu2, a = _mm(tag + "_gu", ins, [(0, 1, 0), (0, 2, 1)], 2, (tm, fc), [], epi_gu,
                 [SDS((2, nc, t, fc), BF16), SDS((nc, t, fc), BF16)],
                 [pl.BlockSpec((2, None, tm, fc), lambda i, c, k: (0, c, i, 0)),
                  pl.BlockSpec((None, tm, fc), lambda i, c, k: (c, i, 0))],
                 (t // tm, nc, d // tk), "nn")

    def epi_down(accs, ex):
        (yv,), (xv, gm) = accs, ex
        return yv, xv + MACARON_W * gm * yv

    ins = [(a, pl.BlockSpec((None, tm, fc), lambda i, j, k: (k, i, 0))),
           (wd4, pl.BlockSpec((None, None, fc, tn), lambda i, j, k: (lj, k, 0, j))),
           (x, pl.BlockSpec((tm, tn), lambda i, j, k: (i, j))), (gmul, pl.BlockSpec((1, tn), lambda i, j, k: (0, j)))]
    oblk = pl.BlockSpec((tm, tn), lambda i, j, k: (i, j))
    y, x_new = _mm(tag + "_down", ins, [(0, 1, 0)], 1, (tm, tn), [2, 3], epi_down, [SDS((t, d), BF16), SDS((t, d), F32)],
                   [oblk, oblk], (t // tm, d // tn, nc), "nn")
    return gu2, a, y, x_new


def _ffn_bwd(tag, lj, dy, h, gu2, a, wgu, wd4):
    t, d = dy.shape
    fc, nc = wgu.shape[3], HIDDEN_CHUNKS
    tm, tn, tk = _tile(t, (1024, 512, 256, 128)), _tile(d, (1024, 512, 256, 128)), _tile(d, (1024, 512, 256, 128))
    tt = t

    def epi_da(accs, ex):
        (da,), (gu,) = accs, ex
        gpre, up = gu[0].astype(F32), gu[1].astype(F32)
        s = jax.nn.sigmoid(gpre)
        silu = gpre * s
        dg = da * up * (s * (1.0 + gpre * (1.0 - s)))
        return ((dg, da * silu),)

    gblk = pl.BlockSpec((2, None, tm, fc), lambda i, c, k: (0, c, i, 0))
    ins = [(dy, pl.BlockSpec((tm, tk), lambda i, c, k: (i, k))),
           (wd4, pl.BlockSpec((None, None, fc, tk), lambda i, c, k: (lj, c, 0, k))), (gu2, gblk)]
    dgu2 = _mm(tag + "_da", ins, [(0, 1, 0)], 1, (tm, fc), [2], epi_da, [SDS((2, nc, t, fc), BF16)], [gblk],
               (t // tm, nc, d // tk), "nt")[0]

    ins = [(a, pl.BlockSpec((None, tt, fc), lambda c, j, k: (c, k, 0))), (dy, pl.BlockSpec((tt, tn), lambda c, j, k: (k, j)))]
    dwd = _mm(tag + "_dwd", ins, [(0, 1, 0)], 1, (fc, tn), [], _plain, [SDS((1, nc, fc, d), BF16)],
              [pl.BlockSpec((None, None, fc, tn), lambda c, j, k: (0, c, 0, j))], (nc, d // tn, t // tt), "tn")[0]

    ins = [(dgu2, pl.BlockSpec((None, None, tm, fc), lambda i, j, k: (k // nc, k % nc, i, 0))),
           (wgu, pl.BlockSpec((None, None, tn, fc), lambda i, j, k: (lj, k, j, 0)))]
    dh = _mm(tag + "_dh", ins, [(0, 1, 0)], 1, (tm, tn), [], _plain, [SDS((t, d), F32)],
             [pl.BlockSpec((tm, tn), lambda i, j, k: (i, j))], (t // tm, d // tn, 2 * nc), "nt")[0]

    ins = [(h, pl.BlockSpec((tt, tn), lambda i, c, k: (k, i))),
           (dgu2, pl.BlockSpec((None, None, tt, fc), lambda i, c, k: (c // nc, c % nc, k, 0)))]
    dwgu = _mm(tag + "_dwgu", ins, [(0, 1, 0)], 1, (tn, fc), [], _plain, [SDS((1, 2 * nc, d, fc), BF16)],
               [pl.BlockSpec((None, None, tn, fc), lambda i, c, k: (0, c, i, 0))], (d // tn, 2 * nc, t // tt), "tn")[0]
    return dh, dwgu, dwd


def _sb_block(t):
    return 256 if t >= 1024 else 128


def _split_hi_lo(v):
    hi = v.astype(BF16)
    return hi, (v - hi.astype(F32)).astype(BF16)


def _host_call(core, name, steps, ins, in_specs, out_shapes, out_specs, scratch, plan):
    n_in, n_out, n_scr = len(ins), len(out_shapes), len(scratch)
    c_ins, c_outs = (plan.ins, plan.outs) if plan else ([], [])
    n_cin, n_cout = len(c_ins), len(c_outs)

    def body(*refs):
        in_refs, c_in = refs[:n_in], refs[n_in:n_in + n_cin]
        rest = refs[n_in + n_cin:]
        out_refs, c_out = rest[:n_out], rest[n_out:n_out + n_cout]
        rest = rest[n_out + n_cout:]
        scr, sems = rest[:n_scr], rest[n_scr:]
        step = pl.program_id(0)
        if plan:
            @pl.when(step == 0)
            def _():
                plan.phases[0](c_in, c_out, *sems)

            @pl.when(step == steps // 2)
            def _():
                plan.phases[1](c_in, c_out, *sems)

        core(in_refs, out_refs, scr)
        if plan:
            @pl.when(step == steps - 1)
            def _():
                plan.phases[2](c_in, c_out, *sems)

    return _pcall(
        body, name=name, grid=(steps,), in_specs=list(in_specs) + [ANY] * n_cin,
        out_specs=list(out_specs) + [ANY] * n_cout, out_shape=list(out_shapes) + list(c_outs),
        scratch_shapes=list(scratch) + (plan.scratch() if plan else []),
        compiler_params=_params(("arbitrary",)))(*ins, *c_ins)


def _sb_fwd(qkv, d, plan=None):
    t = qkv.shape[0]
    blk = _sb_block(t)
    nq = t // blk
    npair = d // LANES
    scale = HEAD_DIM ** -0.5

    def body(in_refs, out_refs, _):
        (q_ref, k_ref, v_ref), (o_ref, l_ref) = in_refs, out_refs
        lane = lax.broadcasted_iota(jnp.int32, (blk, LANES), 1)
        head0 = lane < HEAD_DIM
        row = lax.broadcasted_iota(jnp.int32, (blk, blk), 0)
        col = lax.broadcasted_iota(jnp.int32, (blk, blk), 1)
        causal = col < row
        after = (row > col).astype(BF16)

        def tile(qh, kb, carry, masked):
            cl, oacc = carry
            start = pl.multiple_of(kb * blk, blk)
            kv = k_ref[pl.ds(start, blk), :]
            vv = v_ref[pl.ds(start, blk), :]
            z = _dot(qh, kv, "nt") * scale
            sp = _softplus(z)
            lk = jnp.where(causal, -sp, 0.0) if masked else -sp
            hi, lo = _split_hi_lo(lk)
            later = _dot(hi, after, "nn") + _dot(lo, after, "nn") + cl
            logw = z - sp + later
            if masked:
                logw = jnp.where(causal, logw, -1e30)
            w = jnp.exp(logw)
            oacc = oacc + _dot(w.astype(BF16), vv, "nn")
            return cl + jnp.sum(lk, axis=1, keepdims=True), oacc

        def qblock(qi, _):
            qstart = pl.multiple_of(qi * blk, blk)
            qv = q_ref[pl.ds(qstart, blk), :]
            outs = []
            for hh in range(2):
                qh = jnp.where(head0 if hh == 0 else ~head0, qv, jnp.zeros_like(qv))
                carry = (jnp.zeros((blk, 1), F32), jnp.zeros((blk, LANES), F32))
                carry = tile(qh, qi, carry, True)
                carry = lax.fori_loop(0, qi, lambda j, cr: tile(qh, qi - 1 - j, cr, False), carry)
                outs.append(carry)
            o_ref[pl.ds(qstart, blk), :] = jnp.where(head0, outs[0][1], outs[1][1]).astype(o_ref.dtype)
            l_ref[pl.ds(qstart, blk), :] = jnp.where(head0, outs[0][0], outs[1][0])
            return 0

        lax.fori_loop(0, nq, qblock, 0)

    return _host_call(
        body, "sb_fwd", npair, [qkv, qkv, qkv],
        [pl.BlockSpec((t, LANES), lambda p: (0, p)), pl.BlockSpec((t, LANES), lambda p: (0, npair + p)),
         pl.BlockSpec((t, LANES), lambda p: (0, 2 * npair + p))],
        [SDS((t, d), BF16), SDS((t, d), F32)],
        [pl.BlockSpec((t, LANES), lambda p: (0, p)), pl.BlockSpec((t, LANES), lambda p: (0, p))], [], plan)


def _sb_bwd(qkv, do, ltot, d, plan=None):
    t = qkv.shape[0]
    blk = _sb_block(t)
    nq = t // blk
    npair = d // LANES
    scale = HEAD_DIM ** -0.5

    def body(in_refs, out_refs, scr):
        (q_ref, k_ref, v_ref, do_ref, l_ref), (out_ref,), (dq_s, dk_s, dv_s) = in_refs, out_refs, scr
        lane = lax.broadcasted_iota(jnp.int32, (blk, LANES), 1)
        head0 = lane < HEAD_DIM
        row = lax.broadcasted_iota(jnp.int32, (blk, blk), 0)
        col = lax.broadcasted_iota(jnp.int32, (blk, blk), 1)
        causal = col < row
        upto = (row <= col).astype(BF16)
        before = (row < col).astype(BF16)
        dk_s[...] = jnp.zeros_like(dk_s)
        dv_s[...] = jnp.zeros_like(dv_s)

        def tile(qh, doh, lt, kb, carry, masked):
            plk, pda, dqacc = carry
            start = pl.multiple_of(kb * blk, blk)
            kv = k_ref[pl.ds(start, blk), :]
            vv = v_ref[pl.ds(start, blk), :]
            z = _dot(qh, kv, "nt") * scale
            sp = _softplus(z)
            lk = jnp.where(causal, -sp, 0.0) if masked else -sp
            hi, lo = _split_hi_lo(lk)
            later = lt - (plk + _dot(hi, upto, "nn") + _dot(lo, upto, "nn"))
            logw = z - sp + later
            if masked:
                logw = jnp.where(causal, logw, -1e30)
            w = jnp.exp(logw)
            da = _dot(doh, vv, "nt") * w
            dhi, dlo = _split_hi_lo(da)
            pex = pda + _dot(dhi, before, "nn") + _dot(dlo, before, "nn")
            sig = jnp.exp(z - sp)
            dz = da * (1.0 - sig) - sig * pex
            if masked:
                dz = jnp.where(causal, dz, 0.0)
            dzs = (dz * scale).astype(BF16)
            dqacc = dqacc + _dot(dzs, kv, "nn")
            dk_s[pl.ds(start, blk), :] += _dot(dzs, qh, "tn")
            dv_s[pl.ds(start, blk), :] += _dot(w.astype(BF16), doh, "tn")
            return plk + jnp.sum(lk, axis=1, keepdims=True), pda + jnp.sum(da, axis=1, keepdims=True), dqacc

        def qblock(qi, _):
            qstart = pl.multiple_of(qi * blk, blk)
            qv = q_ref[pl.ds(qstart, blk), :]
            dov = do_ref[pl.ds(qstart, blk), :]
            lv = l_ref[pl.ds(qstart, blk), :]
            dqs = []
            for hh in range(2):
                sel = head0 if hh == 0 else ~head0
                qh = jnp.where(sel, qv, jnp.zeros_like(qv))
                doh = jnp.where(sel, dov, jnp.zeros_like(dov))
                lt = jnp.max(jnp.where(sel, lv, -jnp.inf), axis=1, keepdims=True)
                carry = (jnp.zeros((blk, 1), F32), jnp.zeros((blk, 1), F32), jnp.zeros((blk, LANES), F32))
                carry = lax.fori_loop(0, qi, lambda kb, cr: tile(qh, doh, lt, kb, cr, False), carry)
                carry = tile(qh, doh, lt, qi, carry, True)
                dqs.append(carry[2])
            dq_s[pl.ds(qstart, blk), :] = jnp.where(head0, dqs[0], dqs[1])
            return 0

        lax.fori_loop(0, nq, qblock, 0)
        out_ref[0] = dq_s[...].astype(out_ref.dtype)
        out_ref[1] = dk_s[...].astype(out_ref.dtype)
        out_ref[2] = dv_s[...].astype(out_ref.dtype)

    col_blk = lambda off: pl.BlockSpec((t, LANES), lambda p: (0, off + p))
    return _host_call(
        body, "sb_bwd", npair, [qkv, qkv, qkv, do, ltot],
        [col_blk(0), col_blk(npair), col_blk(2 * npair), col_blk(0), col_blk(0)],
        [SDS((3, t, d), BF16)], [pl.BlockSpec((3, t, LANES), lambda p: (0, 0, p))],
        [pltpu.VMEM((t, LANES), F32) for _ in range(3)], plan)


def _roll_rows(v, shift):
    return pltpu.roll(v, shift, 0)


def _shift_down(v, dist, fill, row):
    return jnp.where(row >= dist, _roll_rows(v, dist), fill)


def _shift_up(v, dist, fill, row):
    t = v.shape[0]
    return jnp.where(row < t - dist, _roll_rows(v, t - dist), fill)


def _lru_gates(xb, small, wr, wi, row):
    xs = [_shift_down(xb, 3 - tap, 0.0, row) if tap < 3 else xb for tap in range(4)]
    xc = small[4:5, :] + xs[0] * small[0:1, :]
    for tap in range(1, 4):
        xc = xc + xs[tap] * small[tap:tap + 1, :]
    xcb = xc.astype(BF16)
    r = jax.nn.sigmoid(_dot(xcb, wr, "nn") + small[5:6, :])
    ig = jax.nn.sigmoid(_dot(xcb, wi, "nn") + small[6:7, :])
    sp = _softplus(-small[7:8, :])
    la = -LRU_C * r * sp
    a = jnp.exp(la)
    th = jnp.tanh(la)
    mult = jnp.sqrt(-2.0 * th / (1.0 - th))
    return xs, xc, xcb, r, ig, sp, a, mult


def _gelu_parts(gate):
    inner = GELU_C * (gate + GELU_K * gate * gate * gate)
    th = jnp.tanh(inner)
    gelu = 0.5 * gate * (1.0 + th)
    dgelu = 0.5 * (1.0 + th) + 0.5 * gate * (1.0 - th * th) * GELU_C * (1.0 + 3.0 * GELU_K * gate * gate)
    return gelu, dgelu


def _scan_steps(t):
    steps, dist = [], 1
    while dist < t:
        steps.append(dist)
        dist *= 2
    return steps


def _lru_fwd(gx, small, wr, wi):
    t = gx.shape[0]
    r_dim = gx.shape[1] // 2
    nb = r_dim // LRU_BLOCK_W

    def body(gate_ref, xb_ref, small_ref, wr_ref, wi_ref, y_ref, hs_ref):
        row = lax.broadcasted_iota(jnp.int32, (t, LRU_BLOCK_W), 0)
        xb = xb_ref[...]
        _, xc, _, _, ig, _, a, mult = _lru_gates(xb, small_ref, wr_ref[...], wi_ref[...], row)
        b = mult * (ig * xc)
        for dist in _scan_steps(t):
            b = a * _shift_down(b, dist, 0.0, row) + b
            a = a * _shift_down(a, dist, 1.0, row)
        hs_ref[...] = b
        gelu, _ = _gelu_parts(gate_ref[...])
        y_ref[...] = (gelu * b).astype(y_ref.dtype)

    colb = lambda off: pl.BlockSpec((t, LRU_BLOCK_W), lambda n: (0, off + n))
    wspec = pl.BlockSpec((None, LRU_BLOCK_W, LRU_BLOCK_W), lambda n: (n, 0, 0))
    return _pcall(
        body, name="lru_fwd", grid=(nb,),
        in_specs=[colb(0), colb(nb), pl.BlockSpec((8, LRU_BLOCK_W), lambda n: (0, n)), wspec, wspec],
        out_specs=[colb(0), colb(0)], out_shape=[SDS((t, r_dim), BF16), SDS((t, r_dim), F32)],
        compiler_params=_params(("parallel",)))(gx, gx, small, wr, wi)


def _lru_bwd(gx, hs, dy, small, wr, wi):
    t = gx.shape[0]
    r_dim = gx.shape[1] // 2
    nb = r_dim // LRU_BLOCK_W

    def body(gate_ref, xb_ref, hs_ref, dy_ref, small_ref, wr_ref, wi_ref, dgx_ref, dsm_ref, dwr_ref, dwi_ref):
        row = lax.broadcasted_iota(jnp.int32, (t, LRU_BLOCK_W), 0)
        xb, hsv, dyv, smallv = xb_ref[...], hs_ref[...], dy_ref[...], small_ref
        wrv, wiv = wr_ref[...], wi_ref[...]
        xs, xc, xcb, r, ig, sp, a, mult = _lru_gates(xb, smallv, wrv, wiv, row)
        gelu, dgelu = _gelu_parts(gate_ref[...])
        dgx_ref[0] = (dyv * hsv * dgelu).astype(dgx_ref.dtype)
        dacc = dyv * gelu
        an = _shift_up(a, 1, 1.0, row)
        for dist in _scan_steps(t):
            dacc = dacc + an * _shift_up(dacc, dist, 0.0, row)
            an = an * _shift_up(an, dist, 1.0, row)
        da = dacc * _shift_down(hsv, 1, 0.0, row)
        dmult = dacc * (ig * xc)
        dixc = dacc * mult
        dla = da * a - dmult * (a * a) / mult
        dr = dla * (-LRU_C * sp)
        dsp = jnp.sum(dla * (-LRU_C * r), axis=0, keepdims=True)
        dpr = dr * r * (1.0 - r)
        dpi = dixc * xc * ig * (1.0 - ig)
        dprb, dpib = dpr.astype(BF16), dpi.astype(BF16)
        dwr_ref[...] = _dot(xcb, dprb, "tn")
        dwi_ref[...] = _dot(xcb, dpib, "tn")
        dxc = dixc * ig + _dot(dprb, wrv, "nt") + _dot(dpib, wiv, "nt")
        dxb = dxc * smallv[3:4, :]
        for tap in range(3):
            dxb = dxb + _shift_up(dxc, 3 - tap, 0.0, row) * smallv[tap:tap + 1, :]
        dgx_ref[1] = dxb.astype(dgx_ref.dtype)
        lam = smallv[7:8, :]
        rows = [jnp.sum(dxc * xs[tap], axis=0, keepdims=True) for tap in range(4)]
        rows.append(jnp.sum(dxc, axis=0, keepdims=True))
        rows.append(jnp.sum(dpr, axis=0, keepdims=True))
        rows.append(jnp.sum(dpi, axis=0, keepdims=True))
        rows.append(-dsp * jax.nn.sigmoid(-lam))
        for k, rv in enumerate(rows):
            dsm_ref[k:k + 1, :] = rv

    colb = lambda off: pl.BlockSpec((t, LRU_BLOCK_W), lambda n: (0, off + n))
    wspec = pl.BlockSpec((None, LRU_BLOCK_W, LRU_BLOCK_W), lambda n: (n, 0, 0))
    sspec = pl.BlockSpec((8, LRU_BLOCK_W), lambda n: (0, n))
    return _pcall(
        body, name="lru_bwd", grid=(nb,),
        in_specs=[colb(0), colb(nb), colb(0), colb(0), sspec, wspec, wspec],
        out_specs=[pl.BlockSpec((2, t, LRU_BLOCK_W), lambda n: (0, 0, n)), sspec, wspec, wspec],
        out_shape=[SDS((2, t, r_dim), BF16), SDS((8, r_dim), F32), SDS((nb, LRU_BLOCK_W, LRU_BLOCK_W), F32),
                   SDS((nb, LRU_BLOCK_W, LRU_BLOCK_W), F32)],
        compiler_params=_params(("parallel",)))(gx, gx, hs, dy, small, wr, wi)


def _adam(w, g, m, v):
    m2 = ADAM_B1 * m + (1.0 - ADAM_B1) * g
    v2 = ADAM_B2 * v + (1.0 - ADAM_B2) * (g * g)
    m_hat = m2 / (1.0 - ADAM_B1 ** ADAM_STEP)
    v_hat = v2 / (1.0 - ADAM_B2 ** ADAM_STEP)
    return -ADAM_LR * (m_hat / (jnp.sqrt(v_hat) + ADAM_EPS) + ADAM_WD * w), m2, v2


def _mod_fwd(c_all, mod_w, mod_b_cols):
    nl, d, cols = mod_w.shape
    nbatch = c_all.shape[0]

    def body(c_ref, w_ref, b_ref, o_ref):
        cv = c_ref[...]
        ca = (cv * jax.nn.sigmoid(cv)).astype(BF16)
        o_ref[...] = _dot(ca, w_ref[...].astype(BF16), "nn") + b_ref[...]

    return _pcall(
        body, name="mod_fwd", grid=(nl,),
        in_specs=[pl.BlockSpec((nbatch, d), lambda l: (0, 0)), pl.BlockSpec((None, d, cols), lambda l: (l, 0, 0)),
                  pl.BlockSpec((None, 1, cols), lambda l: (l, 0, 0))],
        out_specs=pl.BlockSpec((None, nbatch, cols), lambda l: (l, 0, 0)), out_shape=SDS((nl, nbatch, cols), F32),
        compiler_params=_params(("parallel",)))(c_all, mod_w, mod_b_cols)


def _mod_w_update(c_all, dmod_cols, w, m, v):
    nl, d, cols = w.shape
    nbatch = c_all.shape[0]
    tr = _tile(d, (256, 128))

    def body(c_ref, dm_ref, w_ref, m_ref, v_ref, g_ref, dl_ref, m2_ref, v2_ref):
        cv = c_ref[...]
        ca = (cv * jax.nn.sigmoid(cv)).astype(BF16)
        g = _dot(ca, dm_ref[...].astype(BF16), "tn")
        g_ref[...] = g
        dl_ref[...], m2_ref[...], v2_ref[...] = _adam(w_ref[...], g, m_ref[...], v_ref[...])

    wblk = pl.BlockSpec((None, tr, cols), lambda l, i: (l, i, 0))
    return _pcall(
        body, name="mod_w_update", grid=(nl, d // tr),
        in_specs=[pl.BlockSpec((nbatch, tr), lambda l, i: (0, i)), pl.BlockSpec((None, nbatch, cols), lambda l, i: (l, 0, 0)),
                  wblk, wblk, wblk],
        out_specs=[wblk] * 4, out_shape=[SDS(w.shape, F32)] * 4,
        compiler_params=_params(("parallel", "parallel")))(c_all, dmod_cols, w, m, v)


def _adam_update(name, w, m, v, gparts):
    rows, cols = w.shape
    tr = _tile(rows, (256, 128, 64, 32, 16, 8))
    npart = len(gparts)

    def body(*refs):
        w_ref, m_ref, v_ref = refs[:3]
        g_refs = refs[3:3 + npart]
        g_ref, dl_ref, m2_ref, v2_ref = refs[3 + npart:]
        g = g_refs[0][...].astype(F32)
        for gr in g_refs[1:]:
            g = g + gr[...].astype(F32)
        g_ref[...] = g
        dl_ref[...], m2_ref[...], v2_ref[...] = _adam(w_ref[...], g, m_ref[...], v_ref[...])

    blk = pl.BlockSpec((tr, cols), lambda i: (i, 0))
    return _pcall(body, name=name, grid=(rows // tr,), in_specs=[blk] * (3 + npart), out_specs=[blk] * 4,
                  out_shape=[SDS((rows, cols), F32)] * 4, compiler_params=_params(("parallel",)))(w, m, v, *gparts)


def _adam_shard(name, w, m, v, part4, recv3, chip_idx, first=0, fills=None):
    p, r, cdim = w.shape
    pg = part4.shape[0]
    tr = _tile(r, (256, 176, 160, 128, 64, 32, 16))

    def body(chip_ref, w_ref, m_ref, v_ref, own_ref, r0_ref, r1_ref, r2_ref, *rest):
        g_ref, dl_ref, m2_ref, v2_ref = rest[-4:]
        g = own_ref[...].astype(F32) + r0_ref[...].astype(F32) + r1_ref[...].astype(F32) + r2_ref[...].astype(F32)
        g_ref[...] = g
        dl_ref[...], m2_ref[...], v2_ref[...] = _adam(w_ref[...], g, m_ref[...], v_ref[...])

    blk = pl.BlockSpec((None, tr, cdim), lambda q, i, chip_ref: (first + q, i, 0))
    blk4 = (None, None, tr, cdim)
    slot = lambda s: pl.BlockSpec(blk4, lambda q, i, chip_ref: (s, q, i, 0))
    fills = list(fills or [])
    grid_spec = pltpu.PrefetchScalarGridSpec(
        num_scalar_prefetch=1, grid=(pg, r // tr),
        in_specs=[blk, blk, blk, pl.BlockSpec(blk4, lambda q, i, chip_ref: (q, chip_ref[0], i, 0)), slot(0), slot(1), slot(2)]
        + [ANY] * len(fills),
        out_specs=[blk] * 4)
    return _pcall(body, name=name, grid_spec=grid_spec, out_shape=[SDS((p, r, cdim), F32)] * 4,
                  input_output_aliases={8 + k: k for k in range(len(fills))},
                  compiler_params=_params(("parallel", "parallel")))(chip_idx, w, m, v, part4, recv3, recv3, recv3, *fills)


def _sum_devices(gathered, name):
    _, rows, cols = gathered.shape
    tr = _tile(rows, (512, 256, 128, 64, 32, 16, 8))

    def body(g_ref, o_ref):
        acc = g_ref[0]
        for k in range(1, N_DEV):
            acc = acc + g_ref[k]
        o_ref[...] = acc

    return _pcall(body, name=name, grid=(rows // tr,), in_specs=[pl.BlockSpec((N_DEV, tr, cols), lambda i: (0, i, 0))],
                  out_specs=pl.BlockSpec((tr, cols), lambda i: (i, 0)), out_shape=SDS((rows, cols), F32),
                  compiler_params=_params(("parallel",)))(gathered)


def _pack_flat(parts, width, row_mult, dtype):
    flat = jnp.concatenate([p.reshape(-1).astype(dtype) for p in parts])
    unit = width * row_mult
    pad = (-flat.shape[0]) % unit
    if pad:
        flat = jnp.concatenate([flat, jnp.zeros((pad,), dtype)])
    return flat.reshape(-1, width)


def _unpack_flat(flat, shapes):
    out, off = [], 0
    for shp in shapes:
        size = math.prod(shp)
        out.append(flat[off:off + size].reshape(shp))
        off += size
    return out


def kernel(x, c, mod_w, mod_b, norm_g, ffn_w_gu, ffn_w_down, sb_w_qkv, sb_w_o, lru_w_in, lru_conv_w, lru_conv_b, lru_w_r, lru_b_r, lru_w_i, lru_b_i, lru_lambda, lru_w_out, final_norm_g, loss_target, m_mod_w, m_mod_b, m_norm_g, m_ffn_w_gu, m_ffn_w_down, m_sb_w_qkv, m_sb_w_o, m_lru_w_in, m_lru_conv_w, m_lru_conv_b, m_lru_w_r, m_lru_b_r, m_lru_w_i, m_lru_b_i, m_lru_lambda, m_lru_w_out, m_final_norm_g, v_mod_w, v_mod_b, v_norm_g, v_ffn_w_gu, v_ffn_w_down, v_sb_w_qkv, v_sb_w_o, v_lru_w_in, v_lru_conv_w, v_lru_conv_b, v_lru_w_r, v_lru_b_r, v_lru_w_i, v_lru_b_i, v_lru_lambda, v_lru_w_out, v_final_norm_g):
    weights = dict(mod_w=mod_w, mod_b=mod_b, norm_g=norm_g, ffn_w_gu=ffn_w_gu, ffn_w_down=ffn_w_down, sb_w_qkv=sb_w_qkv,
                   sb_w_o=sb_w_o, lru_w_in=lru_w_in, lru_conv_w=lru_conv_w, lru_conv_b=lru_conv_b, lru_w_r=lru_w_r,
                   lru_b_r=lru_b_r, lru_w_i=lru_w_i, lru_b_i=lru_b_i, lru_lambda=lru_lambda, lru_w_out=lru_w_out,
                   final_norm_g=final_norm_g)
    mom_m = dict(mod_w=m_mod_w, mod_b=m_mod_b, norm_g=m_norm_g, ffn_w_gu=m_ffn_w_gu, ffn_w_down=m_ffn_w_down,
                 sb_w_qkv=m_sb_w_qkv, sb_w_o=m_sb_w_o, lru_w_in=m_lru_w_in, lru_conv_w=m_lru_conv_w,
                 lru_conv_b=m_lru_conv_b, lru_w_r=m_lru_w_r, lru_b_r=m_lru_b_r, lru_w_i=m_lru_w_i, lru_b_i=m_lru_b_i,
                 lru_lambda=m_lru_lambda, lru_w_out=m_lru_w_out, final_norm_g=m_final_norm_g)
    mom_v = dict(mod_w=v_mod_w, mod_b=v_mod_b, norm_g=v_norm_g, ffn_w_gu=v_ffn_w_gu, ffn_w_down=v_ffn_w_down,
                 sb_w_qkv=v_sb_w_qkv, sb_w_o=v_sb_w_o, lru_w_in=v_lru_w_in, lru_conv_w=v_lru_conv_w,
                 lru_conv_b=v_lru_conv_b, lru_w_r=v_lru_w_r, lru_b_r=v_lru_b_r, lru_w_i=v_lru_w_i, lru_b_i=v_lru_b_i,
                 lru_lambda=v_lru_lambda, lru_w_out=v_lru_w_out, final_norm_g=v_final_norm_g)
    names = list(weights)

    t, d = x.shape[1], x.shape[2]
    n_layers = mod_w.shape[0]
    r_dim = lru_w_out.shape[1] * N_DEV
    ng, rs = d // N_DEV, r_dim // N_DEV
    mod_cols = mod_w.shape[2]
    nblk = lru_w_r.shape[1]
    xi, yi, ci = _mesh_pos()
    me = 4 * xi + 2 * yi + ci
    chip = 2 * xi + yi
    x2, target = x.reshape(t, d), loss_target.reshape(t, d)

    lru_small_shard = jnp.concatenate([lru_conv_w[0], lru_conv_b, lru_b_r, lru_b_i, lru_lambda], axis=0)
    small1 = _pack_flat([c, norm_g, lru_small_shard], LANES, 8, F32)
    n_small1 = small1.shape[0]
    all1 = _allgather(small1[None], "gather_small").reshape(N_DEV, n_small1 * LANES)
    c_all = all1[:, :d]
    norm_full = jnp.transpose(all1[:, d:d + 6 * ng].reshape(N_DEV, n_layers, 3, ng), (1, 2, 0, 3)).reshape(n_layers, 3, d)
    lru_small = jnp.transpose(all1[:, d + 6 * ng:d + 6 * ng + 8 * rs].reshape(N_DEV, 8, rs), (1, 0, 2)).reshape(8, r_dim)

    mod_b_cols = lax.dynamic_slice_in_dim(mod_b, me * mod_cols, mod_cols, axis=1).reshape(n_layers, 1, mod_cols)
    mod_part = _mod_fwd(c_all, mod_w, mod_b_cols)
    mod_all = _allgather(mod_part, "gather_mod")
    mod_mine = lax.dynamic_index_in_dim(mod_all, me, axis=2, keepdims=False)
    mod_mine = mod_mine.reshape(n_layers, 3, 3, d)

    assert sb_w_qkv.shape[0] == 1 and lru_w_in.shape[0] == 1, "one stick-breaking and one RG-LRU layer"
    n_ffn = 2 * n_layers
    fc = ffn_w_gu.shape[3]
    cw_in = lru_w_in.shape[2]
    pieces = {("ffn_w_gu", q): ffn_w_gu[q // 2, q % 2][None] for q in range(n_ffn)}
    pieces.update({("ffn_w_down", q): ffn_w_down[q // 2, q % 2][None] for q in range(n_ffn)})
    pieces.update({("sb_w_qkv", 0): sb_w_qkv, ("sb_w_o", 0): sb_w_o, ("lru_w_in", 0): lru_w_in, ("lru_w_out", 0): lru_w_out})
    col_window = {("sb_w_qkv", 0)}
    early = [("ffn_w_gu", 0), ("ffn_w_down", 0), ("sb_w_qkv", 0), ("sb_w_o", 0)]
    late = [key for key in pieces if key not in early]
    gathered = {}

    def gather_plan(keys):
        return _gather_plan([pieces[key].astype(BF16) for key in keys], [key in col_window for key in keys])

    gathered.update(zip(early, _run_comm(gather_plan(early), "gather_early")))
    wr_b, wi_b = lru_w_r[0].astype(BF16), lru_w_i[0].astype(BF16)
    eye2 = jnp.eye(2 * cw_in, dtype=BF16).reshape(2, cw_in, 2 * cw_in)

    def w_gu(q):
        return gathered[("ffn_w_gu", q)]

    def w_d4(q):
        return gathered[("ffn_w_down", q)].reshape(1, HIDDEN_CHUNKS, fc, d)

    saved = []
    xcur = x2
    for layer in range(n_layers):
        for sub in range(3):
            gvec = norm_full[layer, sub].reshape(1, d)
            shift = mod_mine[layer, sub, 0].reshape(1, d)
            scale1p = 1.0 + mod_mine[layer, sub, 1].reshape(1, d)
            gmul = 1.0 + mod_mine[layer, sub, 2].reshape(1, d)
            tag = f"l{layer}s{sub}"
            h = _norm_fwd(xcur, gvec, scale1p, shift, tag + "_norm")
            rec = dict(x=xcur, h=h, g=gvec, scale1p=scale1p, gmul=gmul, w=MACARON_W if sub != 1 else 1.0)
            if sub != 1:
                lj = layer * 2 + sub // 2
                gu2, a, yv, xcur = _ffn_fwd(tag, 0, xcur, h, w_gu(lj), w_d4(lj), gmul)
                rec.update(kind="ffn", lj=lj, gu2=gu2, a=a, y=yv)
            elif layer % 2 == 0:
                w_qkv = gathered[("sb_w_qkv", 0)][0]
                w_o = gathered[("sb_w_o", 0)].reshape(d, d)
                qkv = _mm_nn(tag + "_qkv", h, w_qkv, BF16)[0]
                o, ltot, *landed = _sb_fwd(qkv, d, gather_plan(late))
                gathered.update(zip(late, landed))
                yv, xcur = _mm_nn(tag + "_wo", o, w_o, [BF16, F32], extras=[(xcur, "tile"), (gmul, "row")],
                                  epilogue=lambda accs, ex: (accs[0], ex[0] + ex[1] * accs[0]))
                rec.update(kind="sb", qkv=qkv, o=o, ltot=ltot, y=yv, w_qkv=w_qkv, w_o=w_o)
            else:
                w_in = _chunks_to_cols("lru_w_in_cols", gathered[("lru_w_in", 0)][0], eye2)
                w_out = gathered[("lru_w_out", 0)].reshape(r_dim, d)
                gx = _mm_nn(tag + "_win", h, w_in, F32)[0]
                ymix, hs = _lru_fwd(gx, lru_small, wr_b, wi_b)
                yv, xcur = _mm_nn(tag + "_wout", ymix, w_out, [BF16, F32], extras=[(xcur, "tile"), (gmul, "row")],
                                  epilogue=lambda accs, ex: (accs[0], ex[0] + ex[1] * accs[0]))
                rec.update(kind="lru", gx=gx, hs=hs, ymix=ymix, y=yv, w_in=w_in, w_out=w_out)
            saved.append(rec)

    last = saved[-1]
    dxo, dy, head_sums = _loss_head(xcur, target, final_norm_g.reshape(1, d), (last["w"] * last["gmul"]))
    loss = lax.psum(head_sums[1, 0], ("x", "y", "c"))
    dgf = head_sums[0]

    c_idx = jnp.reshape(ci, (1,)).astype(jnp.int32)
    chip_idx = jnp.reshape(chip, (1,)).astype(jnp.int32)
    grads, reduced = {}, {}

    def pair_sums(keys, tag):
        arrs, cols = [grads[key] for key in keys], [key in col_window for key in keys]
        recv4 = _run_comm(_exchange_plan(arrs, cols, 4, _sibling_route), "rs_sibling_" + tag)
        return [_pair_sum(g, r4, c_idx, f"rs_pair_sum_{key[0]}{key[1]}", cols=cl)
                for key, g, r4, cl in zip(keys, arrs, recv4, cols)]

    dmod = [[None] * 3 for _ in range(n_layers)]
    dnorm = [[None] * 3 for _ in range(n_layers)]
    dlru_small = dwr = dwi = None
    for idx in reversed(range(len(saved))):
        rec = saved[idx]
        layer, sub = divmod(idx, 3)
        tag = f"l{layer}s{sub}b"
        if rec["kind"] == "ffn":
            lj = rec["lj"]
            dh, dwgu, dwd = _ffn_bwd(tag, 0, dy, rec["h"], rec["gu2"], rec["a"], w_gu(lj), w_d4(lj))
            grads[("ffn_w_gu", lj)] = dwgu
            grads[("ffn_w_down", lj)] = dwd.reshape(gathered[("ffn_w_down", lj)].shape)
        elif rec["kind"] == "sb":
            ready = list(grads)
            part4 = pair_sums(ready, "ready")
            do = _mm_nt(tag + "_do", dy, rec["w_o"], BF16)
            dwo = _mm_tn(tag + "_dwo", rec["o"], dy, BF16)
            dqkv3, *recv3 = _sb_bwd(rec["qkv"], do, rec["ltot"], d, _exchange_plan(part4, [False] * len(ready), 3, _chip_route))
            reduced.update({key: (p4, r3) for key, p4, r3 in zip(ready, part4, recv3)})
            grads[("sb_w_o", 0)] = dwo.reshape(gathered[("sb_w_o", 0)].shape)
            dh = _mm_nt_stack(tag + "_dh", dqkv3, rec["w_qkv"], F32)
            grads[("sb_w_qkv", 0)] = _mm_tn_stack(tag + "_dwqkv", rec["h"], dqkv3, BF16)[None]
        else:
            dymix = _mm_nt(tag + "_dymix", dy, rec["w_out"], F32)
            grads[("lru_w_out", 0)] = _mm_tn(tag + "_dwout", rec["ymix"], dy, BF16).reshape(gathered[("lru_w_out", 0)].shape)
            dgx2, dlru_small, dwr, dwi = _lru_bwd(rec["gx"], rec["hs"], dymix, lru_small, wr_b, wi_b)
            dh = _mm_nt_stack(tag + "_dh", dgx2, rec["w_in"], F32)
            dw_in = _mm_tn_stack(tag + "_dwin", rec["h"], dgx2, BF16)
            grads[("lru_w_in", 0)] = _cols_to_chunks("lru_w_in_chunks", dw_in, eye2)[None]
        prev = saved[idx - 1] if idx > 0 else None
        gw_prev = (prev["w"] * prev["gmul"]) if prev is not None else jnp.zeros((1, d), F32)
        dxo, dy, sums = _adaln_bwd(dh, rec["x"], rec["y"], dxo, rec["g"], rec["scale1p"], rec["w"], gw_prev, tag + "_adaln")
        dmod[layer][sub] = sums[0:3]
        dnorm[layer][sub] = sums[3]
    grad_x = dxo.reshape(x.shape)

    dmod_mine = jnp.stack([jnp.stack(dmod[layer]) for layer in range(n_layers)])
    dnorm_mine = jnp.stack([jnp.stack(dnorm[layer]) for layer in range(n_layers)])
    small_shapes = [(n_layers, 9 * d), (n_layers, 3, d), (8, r_dim), (d,), (nblk, LRU_BLOCK_W, LRU_BLOCK_W),
                    (nblk, LRU_BLOCK_W, LRU_BLOCK_W)]
    small3 = _pack_flat([dmod_mine, dnorm_mine, dlru_small, dgf, dwr, dwi], LANES, 512, F32)
    n_small3 = small3.shape[0]
    all3 = _allgather(small3[None], "gather_small_grads").reshape(N_DEV, n_small3, LANES)
    gsum = _sum_devices(all3, "sum_small_grads").reshape(-1)
    g_mod_b, g_norm_full, g_lru_small, g_final, g_wr, g_wi = _unpack_flat(gsum, small_shapes)
    dmod_all = all3.reshape(N_DEV, -1)[:, :n_layers * 9 * d].reshape(N_DEV, n_layers, N_DEV, mod_cols)
    dmod_cols = jnp.transpose(lax.dynamic_index_in_dim(dmod_all, me, axis=2, keepdims=False), (1, 0, 2))

    out_g, out_d, out_m, out_v = {}, {}, {}, {}
    out_g["mod_w"], out_d["mod_w"], out_m["mod_w"], out_v["mod_w"] = _mod_w_update(c_all, dmod_cols, mod_w, m_mod_w, v_mod_w)

    g_norm_shard = lax.dynamic_slice_in_dim(g_norm_full, me * ng, ng, axis=2)
    g_lru_shard = lax.dynamic_slice_in_dim(g_lru_small, me * rs, rs, axis=1)
    small_grads = dict(mod_b=g_mod_b, norm_g=g_norm_shard, lru_conv_w=g_lru_shard[0:4].reshape(lru_conv_w.shape),
                       lru_conv_b=g_lru_shard[4:5], lru_b_r=g_lru_shard[5:6], lru_b_i=g_lru_shard[6:7],
                       lru_lambda=g_lru_shard[7:8], final_norm_g=g_final, lru_w_r=g_wr.reshape(lru_w_r.shape),
                       lru_w_i=g_wi.reshape(lru_w_i.shape))
    small_names = list(small_grads)
    sw = _pack_flat([weights[n] for n in small_names], LANES, 256, F32)
    sg = _pack_flat([small_grads[n] for n in small_names], LANES, 256, F32)
    sm = _pack_flat([mom_m[n] for n in small_names], LANES, 256, F32)
    sv = _pack_flat([mom_v[n] for n in small_names], LANES, 256, F32)
    s_outs = _adam_update("adam_small", sw, sm, sv, [sg])
    small_shapes2 = [weights[n].shape for n in small_names]
    for dst, flat in zip((out_g, out_d, out_m, out_v), s_outs):
        for n, arr in zip(small_names, _unpack_flat(flat.reshape(-1), small_shapes2)):
            dst[n] = arr

    rest = [key for key in grads if key not in reduced]
    part4 = pair_sums(rest, "rest")
    recv3 = _run_comm(_exchange_plan(part4, [False] * len(rest), 3, _chip_route), "rs_chips_rest")
    reduced.update({key: (p4, r3) for key, p4, r3 in zip(rest, part4, recv3)})
    for n in ["ffn_w_gu", "ffn_w_down", "sb_w_qkv", "sb_w_o", "lru_w_in", "lru_w_out"]:
        shp = weights[n].shape
        shard3 = (math.prod(shp[:-2]),) + shp[-2:]
        view = lambda arr: arr.reshape(shard3)
        outs = None
        for q in range(shard3[0]):
            fills = outs if outs is not None else [lax.empty(shard3, F32) for _ in range(4)]
            p4, r3 = reduced[(n, q)]
            outs = _adam_shard(f"adam_{n}{q}", view(weights[n]), view(mom_m[n]), view(mom_v[n]), p4, r3, chip_idx,
                               first=q, fills=fills if shard3[0] > 1 else None)
        out_g[n], out_d[n], out_m[n], out_v[n] = [o.reshape(shp) for o in outs]

    return (loss, grad_x, *[out_g[n] for n in names], *[out_d[n] for n in names], *[out_m[n] for n in names],
            *[out_v[n] for n in names])
```

```python
import functools
import math

import jax
import jax.numpy as jnp
from jax import lax
from jax.experimental import pallas as pl
from jax.experimental.pallas import tpu as pltpu

F32 = jnp.float32
BF16 = jnp.bfloat16
SDS = jax.ShapeDtypeStruct
MESH = pl.DeviceIdType.MESH
ANY = pl.BlockSpec(memory_space=pl.ANY)

N_DEV = 8
LANES = 128
HEAD_DIM = 64
LRU_BLOCK_W = 128
LRU_C = 8.0
MACARON_W = 0.5
NORM_EPS = 1e-6
ADAM_LR = 0.001
ADAM_B1 = 0.9
ADAM_B2 = 0.999
ADAM_EPS = 1e-08
ADAM_WD = 0.01
ADAM_STEP = 10
VMEM_LIMIT = 56 * 1024 * 1024
GELU_C = math.sqrt(2.0 / math.pi)
GELU_K = 0.044715

DIMS = {
    "nn": (((1,), (0,)), ((), ())),
    "nt": (((1,), (1,)), ((), ())),
    "tn": (((0,), (0,)), ((), ())),
}


def _pcall(body, **kw):
    return pl.pallas_call(body, **kw)


def _params(sem=None):
    return pltpu.CompilerParams(dimension_semantics=sem, vmem_limit_bytes=VMEM_LIMIT)


def _tile(n, prefs):
    for p in prefs:
        if n % p == 0:
            return p
    return n


def _dot(a, b, dims):
    return lax.dot_general(a, b, DIMS[dims], preferred_element_type=F32)


def _softplus(z):
    return jnp.maximum(z, 0.0) + jnp.log(1.0 + jnp.exp(-jnp.abs(z)))


def _mesh_pos():
    return lax.axis_index("x"), lax.axis_index("y"), lax.axis_index("c")


def _allgather(xs, name, cols=False):
    return _run_comm(_gather_plan([xs], [cols]), name)[0]


class _CommPlan:
    def __init__(self, ins, outs, n_remote, n_local, phases):
        self.ins, self.outs, self.n_remote, self.n_local, self.phases = ins, outs, n_remote, n_local, phases

    def scratch(self):
        return [pltpu.SemaphoreType.DMA((self.n_remote,)), pltpu.SemaphoreType.DMA((self.n_remote,)),
                pltpu.SemaphoreType.DMA((max(self.n_local, 1),))]


def _run_comm(plan, name):
    n_in, n_out = len(plan.ins), len(plan.outs)

    def body(*refs):
        in_refs, out_refs, sems = refs[:n_in], refs[n_in:n_in + n_out], refs[n_in + n_out:]
        for phase in plan.phases:
            phase(in_refs, out_refs, *sems)

    return _pcall(body, name=name, out_shape=plan.outs, in_specs=[ANY] * n_in, out_specs=[ANY] * n_out,
                  scratch_shapes=plan.scratch())(*plan.ins)


def _col_window(ref, idx, width):
    return ref.at[:, :, pl.ds(pl.multiple_of(idx * width, math.gcd(width, LANES)), width)]


def _gather_plan(shards, cols):
    n = len(shards)
    outs = [SDS((s.shape[0], s.shape[1], N_DEV * s.shape[2]) if cl else (s.shape[0], N_DEV) + s.shape[1:], s.dtype)
            for s, cl in zip(shards, cols)]

    def copies(a, in_refs, out_refs, send_sems, recv_sems, local_sems):
        x, y, c = _mesh_pos()
        sibling = (x, y, 1 - c)
        chips = [(1 - x, y), (x, 1 - y), (1 - x, 1 - y)]
        width = shards[a].shape[2]

        def block(px, py, pc):
            idx = 4 * px + 2 * py + pc
            return _col_window(out_refs[a], idx, width) if cols[a] else out_refs[a].at[:, idx]

        def copy(k, owner, to, src=None):
            return pltpu.make_async_remote_copy(
                src_ref=block(*owner) if src is None else src, dst_ref=block(*owner),
                send_sem=send_sems.at[7 * a + k], recv_sem=recv_sems.at[7 * a + k], device_id=to, device_id_type=MESH)

        me = (x, y, c)
        first = [copy(0, me, sibling, src=in_refs[a])]
        first += [copy(1 + j, me, (*chip, c), src=in_refs[a]) for j, chip in enumerate(chips)]
        passed = [copy(4 + j, (*chip, c), sibling) for j, chip in enumerate(chips)]
        landed = [copy(1 + j, (*chip, c), me) for j, chip in enumerate(chips)]
        from_sibling = [copy(0, sibling, me)] + [copy(4 + j, (*chip, 1 - c), me) for j, chip in enumerate(chips)]
        mine = pltpu.make_async_copy(in_refs[a], block(*me), local_sems.at[a])
        return first, passed, landed, from_sibling, mine

    def start(*refs):
        for a in range(n):
            first, _, _, _, mine = copies(a, *refs)
            mine.start()
            for cp in first:
                cp.start()

    def pass_on(*refs):
        for a in range(n):
            _, passed, landed, _, _ = copies(a, *refs)
            for cp, fwd in zip(landed, passed):
                cp.wait_recv()
                fwd.start()

    def finish(*refs):
        for a in range(n):
            first, passed, _, from_sibling, mine = copies(a, *refs)
            for cp in from_sibling:
                cp.wait_recv()
            for cp in first + passed:
                cp.wait_send()
            mine.wait()

    return _CommPlan(list(shards), outs, 7 * n, n, [start, pass_on, finish])


def _exchange_plan(srcs, cols, n_slots, route):
    n = len(srcs)
    outs = []
    for g, cl in zip(srcs, cols):
        shard = (g.shape[0], g.shape[1], g.shape[2] // N_DEV) if cl else (g.shape[0],) + g.shape[2:]
        outs.append(SDS((n_slots,) + shard, g.dtype))

    def copies(in_refs, out_refs, send_sems, recv_sems, local_sems):
        x, y, c = _mesh_pos()
        made = []
        for a in range(n):
            for s in range(n_slots):
                chunk, target = route(x, y, c, s)
                src = _col_window(in_refs[a], chunk, outs[a].shape[3]) if cols[a] else in_refs[a].at[:, chunk]
                made.append(pltpu.make_async_remote_copy(
                    src_ref=src, dst_ref=out_refs[a].at[s], send_sem=send_sems.at[a * n_slots + s],
                    recv_sem=recv_sems.at[a * n_slots + s], device_id=target, device_id_type=MESH))
        return made

    def start(*refs):
        for cp in copies(*refs):
            cp.start()

    def nothing(*refs):
        pass

    def finish(*refs):
        made = copies(*refs)
        for cp in made:
            cp.wait_recv()
        for cp in made:
            cp.wait_send()

    return _CommPlan(list(srcs), outs, n * n_slots, 0, [start, nothing, finish])


def _sibling_route(x, y, c, k):
    return 2 * k + 1 - c, (x, y, 1 - c)


def _chip_route(x, y, c, j):
    px, py = [(1 - x, y), (x, 1 - y), (1 - x, 1 - y)][j]
    return 2 * px + py, (px, py, c)


def _pair_sum(grads, recv4, c_idx, name, cols=False):
    _, p, r, cdim = recv4.shape
    tr = _tile(r, (512, 256, 176, 160, 128, 64, 32, 16))

    def body(c_ref, a_ref, b_ref, o_ref):
        o_ref[...] = (a_ref[...].astype(F32) + b_ref[...].astype(F32)).astype(o_ref.dtype)

    blk = (None, None, tr, cdim)
    if cols:
        own = pl.BlockSpec((None, tr, cdim), lambda k, q, i, c_ref: (q, i, 2 * k + c_ref[0]))
    else:
        own = pl.BlockSpec(blk, lambda k, q, i, c_ref: (q, 2 * k + c_ref[0], i, 0))
    grid_spec = pltpu.PrefetchScalarGridSpec(
        num_scalar_prefetch=1, grid=(4, p, r // tr),
        in_specs=[own, pl.BlockSpec(blk, lambda k, q, i, c_ref: (k, q, i, 0))],
        out_specs=pl.BlockSpec(blk, lambda k, q, i, c_ref: (q, k, i, 0)))
    return _pcall(body, name=name, grid_spec=grid_spec, out_shape=SDS((p, 4, r, cdim), grads.dtype),
                  compiler_params=_params(("parallel", "parallel", "parallel")))(c_idx, grads, recv4)


def _mm(name, ins, prods, n_acc, acc_shape, epi_idx, epilogue, out_shapes, out_specs, grid, dims, fill=None):
    n_in, n_out, nk = len(ins), len(out_shapes), grid[2]
    aliases = {}
    if fill is not None:
        ins = list(ins) + [(fill, ANY)]
        aliases = {n_in: 0}

    n_refs_in = len(ins)

    def body(*refs):
        in_refs, out_refs, acc_refs = refs[:n_in], refs[n_refs_in:n_refs_in + n_out], refs[n_refs_in + n_out:]

        def finish(accs):
            outs = epilogue(accs, [in_refs[i][...] for i in epi_idx])
            for o_ref, o in zip(out_refs, outs):
                if isinstance(o, tuple):
                    for plane, part in enumerate(o):
                        o_ref[plane] = part.astype(o_ref.dtype)
                else:
                    o_ref[...] = o.astype(o_ref.dtype)

        if nk == 1:
            accs = [None] * n_acc
            for ia, ib, iacc in prods:
                term = _dot(in_refs[ia][...], in_refs[ib][...], dims)
                accs[iacc] = term if accs[iacc] is None else accs[iacc] + term
            finish(accs)
            return

        k = pl.program_id(2)

        @pl.when(k == 0)
        def _():
            for acc in acc_refs:
                acc[...] = jnp.zeros_like(acc)

        for ia, ib, iacc in prods:
            acc_refs[iacc][...] += _dot(in_refs[ia][...], in_refs[ib][...], dims)

        @pl.when(k == nk - 1)
        def _():
            finish([acc[...] for acc in acc_refs])

    return _pcall(
        body, name=name, grid=grid, in_specs=[s for _, s in ins], out_specs=out_specs, out_shape=out_shapes,
        scratch_shapes=[pltpu.VMEM(acc_shape, F32) for _ in range(n_acc if nk > 1 else 0)], input_output_aliases=aliases,
        compiler_params=_params(("parallel", "parallel", "arbitrary")),
    )(*[a for a, _ in ins])


def _plain(accs, _):
    return accs


def _mm_nn(name, a, b, out_dtype, extras=(), epilogue=_plain, n_out=1):
    m, kd = a.shape
    n = b.shape[1]
    tm, tn, tk = _tile(m, (1024, 512, 256, 128)), _tile(n, (640, 512, 256, 128)), _tile(kd, (1280, 1024, 512, 256, 128))
    ins = [(a, pl.BlockSpec((tm, tk), lambda i, j, k: (i, k))), (b, pl.BlockSpec((tk, tn), lambda i, j, k: (k, j)))]
    for arr, kind in extras:
        if kind == "tile":
            ins.append((arr, pl.BlockSpec((tm, tn), lambda i, j, k: (i, j))))
        else:
            ins.append((arr, pl.BlockSpec((1, tn), lambda i, j, k: (0, j))))
    dts = out_dtype if isinstance(out_dtype, (list, tuple)) else [out_dtype] * n_out
    return _mm(name, ins, [(0, 1, 0)], 1, (tm, tn), list(range(2, len(ins))), epilogue,
               [SDS((m, n), dt) for dt in dts], [pl.BlockSpec((tm, tn), lambda i, j, k: (i, j)) for _ in dts],
               (m // tm, n // tn, kd // tk), "nn")


def _mm_nt(name, a, b, out_dtype):
    m, kd = a.shape
    n = b.shape[0]
    tm, tn, tk = _tile(m, (1024, 512, 256, 128)), _tile(n, (640, 512, 256, 128)), _tile(kd, (1024, 512, 256, 128))
    ins = [(a, pl.BlockSpec((tm, tk), lambda i, j, k: (i, k))), (b, pl.BlockSpec((tn, tk), lambda i, j, k: (j, k)))]
    return _mm(name, ins, [(0, 1, 0)], 1, (tm, tn), [], _plain, [SDS((m, n), out_dtype)],
               [pl.BlockSpec((tm, tn), lambda i, j, k: (i, j))], (m // tm, n // tn, kd // tk), "nt")[0]


def _mm_tn(name, a, b, out_dtype):
    t, m = a.shape
    n = b.shape[1]
    tm, tn, tk = _tile(m, (640, 512, 256, 128)), _tile(n, (1024, 512, 256, 128)), t
    ins = [(a, pl.BlockSpec((tk, tm), lambda i, j, k: (k, i))), (b, pl.BlockSpec((tk, tn), lambda i, j, k: (k, j)))]
    return _mm(name, ins, [(0, 1, 0)], 1, (tm, tn), [], _plain, [SDS((m, n), out_dtype)],
               [pl.BlockSpec((tm, tn), lambda i, j, k: (i, j))], (m // tm, n // tn, t // tk), "tn")[0]


def _mm_nt_stack(name, a3, b, out_dtype):
    cc, m, kd = a3.shape
    n = b.shape[0]
    tm, tn, tk = _tile(m, (1024, 512, 256, 128)), _tile(n, (1024, 512, 256, 128)), _tile(kd, (1280, 1024, 512, 256, 128))
    nk = kd // tk
    ins = [(a3, pl.BlockSpec((None, tm, tk), lambda i, j, k: (k // nk, i, k % nk))),
           (b, pl.BlockSpec((tn, tk), lambda i, j, k: (j, k)))]
    return _mm(name, ins, [(0, 1, 0)], 1, (tm, tn), [], _plain, [SDS((m, n), out_dtype)],
               [pl.BlockSpec((tm, tn), lambda i, j, k: (i, j))], (m // tm, n // tn, cc * nk), "nt")[0]


def _mm_tn_stack(name, a, b3, out_dtype):
    t, m = a.shape
    cc, _, n = b3.shape
    tm, tn, tk = _tile(m, (512, 256, 128)), _tile(n, (1280, 1024, 512, 256, 128)), t
    nj = n // tn
    ins = [(a, pl.BlockSpec((tk, tm), lambda i, j, k: (k, i))),
           (b3, pl.BlockSpec((None, tk, tn), lambda i, j, k: (j // nj, k, j % nj)))]
    return _mm(name, ins, [(0, 1, 0)], 1, (tm, tn), [], _plain, [SDS((m, cc * n), out_dtype)],
               [pl.BlockSpec((tm, tn), lambda i, j, k: (i, j))], (m // tm, cc * nj, t // tk), "tn")[0]


def _chunks_to_cols(name, wc, eye2):
    nch, d, cw = wc.shape
    tm = _tile(d, (1024, 512, 256, 128))
    ins = [(wc, pl.BlockSpec((None, tm, cw), lambda i, j, k: (2 * j + k, i, 0))),
           (eye2, pl.BlockSpec((None, cw, 2 * cw), lambda i, j, k: (k, 0, 0)))]
    return _mm(name, ins, [(0, 1, 0)], 1, (tm, 2 * cw), [], _plain, [SDS((d, nch * cw), wc.dtype)],
               [pl.BlockSpec((tm, 2 * cw), lambda i, j, k: (i, j))], (d // tm, nch // 2, 2), "nn")[0]


def _cols_to_chunks(name, full, eye2):
    d, n = full.shape
    _, cw, _ = eye2.shape
    nch = n // cw
    tm = _tile(d, (1024, 512, 256, 128))
    ins = [(full, pl.BlockSpec((tm, 2 * cw), lambda i, j, k: (i, j // 2))),
           (eye2, pl.BlockSpec((None, cw, 2 * cw), lambda i, j, k: (j % 2, 0, 0)))]
    return _mm(name, ins, [(0, 1, 0)], 1, (tm, cw), [], _plain, [SDS((nch, d, cw), full.dtype)],
               [pl.BlockSpec((None, tm, cw), lambda i, j, k: (j, i, 0))], (d // tm, nch, 1), "nt")[0]


def _row_tile(t):
    return _tile(t, (256, 128, 64, 32, 16, 8))


def _norm_fwd(x, g, scale1p, shift, name):
    t, d = x.shape
    tr = _row_tile(t)

    def body(x_ref, g_ref, s_ref, b_ref, h_ref):
        xv = x_ref[...]
        inv = lax.rsqrt(jnp.mean(xv * xv, axis=-1, keepdims=True) + NORM_EPS)
        h_ref[...] = ((xv * inv) * g_ref[...] * s_ref[...] + b_ref[...]).astype(h_ref.dtype)

    vec = pl.BlockSpec((1, d), lambda i: (0, 0))
    return _pcall(body, name=name, grid=(t // tr,), in_specs=[pl.BlockSpec((tr, d), lambda i: (i, 0)), vec, vec, vec],
                  out_specs=pl.BlockSpec((tr, d), lambda i: (i, 0)), out_shape=SDS((t, d), BF16),
                  compiler_params=_params(("parallel",)))(x, g, scale1p, shift)


def _adaln_bwd(dh, x, y, dxo, g, scale1p, w_sub, gw_prev, name):
    t, d = x.shape
    tr = _row_tile(t)

    def body(dh_ref, x_ref, y_ref, dxo_ref, g_ref, s_ref, gw_ref, dx_ref, dyp_ref, sums_ref):
        i = pl.program_id(0)

        @pl.when(i == 0)
        def _():
            sums_ref[...] = jnp.zeros_like(sums_ref)

        xv, dhv, dxov = x_ref[...], dh_ref[...], dxo_ref[...]
        inv = lax.rsqrt(jnp.mean(xv * xv, axis=-1, keepdims=True) + NORM_EPS)
        xn = xv * inv
        gv = g_ref[...]
        dn = dhv * s_ref[...]
        dxn = dn * gv
        dx = inv * (dxn - xn * jnp.mean(dxn * xn, axis=-1, keepdims=True)) + dxov
        dx_ref[...] = dx
        dyp_ref[...] = (gw_ref[...] * dx).astype(dyp_ref.dtype)
        sums_ref[0:1, :] += jnp.sum(dhv, axis=0, keepdims=True)
        sums_ref[1:2, :] += jnp.sum(dhv * (xn * gv), axis=0, keepdims=True)
        sums_ref[2:3, :] += jnp.sum(w_sub * y_ref[...] * dxov, axis=0, keepdims=True)
        sums_ref[3:4, :] += jnp.sum(dn * xn, axis=0, keepdims=True)

    blk = pl.BlockSpec((tr, d), lambda i: (i, 0))
    vec = pl.BlockSpec((1, d), lambda i: (0, 0))
    return _pcall(
        body, name=name, grid=(t // tr,), in_specs=[blk, blk, blk, blk, vec, vec, vec],
        out_specs=[blk, blk, pl.BlockSpec((8, d), lambda i: (0, 0))],
        out_shape=[SDS((t, d), F32), SDS((t, d), BF16), SDS((8, d), F32)],
        compiler_params=_params(("arbitrary",)))(dh, x, y, dxo, g, scale1p, gw_prev)


def _loss_head(x, target, gf, gw_prev):
    t, d = x.shape
    tr = _row_tile(t)
    nt = t // tr

    def body(x_ref, tg_ref, g_ref, gw_ref, dx_ref, dyp_ref, sums_ref):
        i = pl.program_id(0)

        @pl.when(i == 0)
        def _():
            sums_ref[...] = jnp.zeros_like(sums_ref)

        xv = x_ref[...]
        inv = lax.rsqrt(jnp.mean(xv * xv, axis=-1, keepdims=True) + NORM_EPS)
        xn = xv * inv
        gv = g_ref[...]
        err = xn * gv - tg_ref[...]
        dyv = err * (1.0 / d)
        dxn = dyv * gv
        dx = inv * (dxn - xn * jnp.mean(dxn * xn, axis=-1, keepdims=True))
        dx_ref[...] = dx
        dyp_ref[...] = (gw_ref[...] * dx).astype(dyp_ref.dtype)
        sums_ref[0:1, :] += jnp.sum(dyv * xn, axis=0, keepdims=True)
        sums_ref[1:2, :] += jnp.sum(err * err, axis=0, keepdims=True)

        @pl.when(i == nt - 1)
        def _():
            tot = jnp.sum(sums_ref[1:2, :], axis=1, keepdims=True) * (0.5 / d)
            sums_ref[1:2, :] = jnp.broadcast_to(tot, (1, d))

    blk = pl.BlockSpec((tr, d), lambda i: (i, 0))
    vec = pl.BlockSpec((1, d), lambda i: (0, 0))
    return _pcall(
        body, name="loss_head", grid=(nt,), in_specs=[blk, blk, vec, vec],
        out_specs=[blk, blk, pl.BlockSpec((8, d), lambda i: (0, 0))],
        out_shape=[SDS((t, d), F32), SDS((t, d), BF16), SDS((8, d), F32)],
        compiler_params=_params(("arbitrary",)))(x, target, gf, gw_prev)


HIDDEN_CHUNKS = N_DEV // 2


def _ffn_fwd(tag, lj, x, h, wgu, wd4, gmul):
    t, d = h.shape
    fc, nc = wgu.shape[3], HIDDEN_CHUNKS
    tm, tn, tk = _tile(t, (1024, 512, 256, 128)), _tile(d, (1024, 512, 256, 128)), _tile(d, (1024, 512, 256, 128))

    def epi_gu(accs, _):
        gpre, up = accs
        return (gpre, up), gpre * jax.nn.sigmoid(gpre) * up

    wblk = (None, None, tk, fc)
    ins = [(h, pl.BlockSpec((tm, tk), lambda i, c, k: (i, k))),
           (wgu, pl.BlockSpec(wblk, lambda i, c, k: (lj, c, k, 0))),
           (wgu, pl.BlockSpec(wblk, lambda i, c, k: (lj, c + nc, k, 0)))]
    gu2, a = _mm(tag + "_gu", ins, [(0, 1, 0), (0, 2, 1)], 2, (tm, fc), [], epi_gu,
                 [SDS((2, nc, t, fc), BF16), SDS((nc, t, fc), BF16)],
                 [pl.BlockSpec((2, None, tm, fc), lambda i, c, k: (0, c, i, 0)),
                  pl.BlockSpec((None, tm, fc), lambda i, c, k: (c, i, 0))],
                 (t // tm, nc, d // tk), "nn")

    def epi_down(accs, ex):
        (yv,), (xv, gm) = accs, ex
        return yv, xv + MACARON_W * gm * yv

    ins = [(a, pl.BlockSpec((None, tm, fc), lambda i, j, k: (k, i, 0))),
           (wd4, pl.BlockSpec((None, None, fc, tn), lambda i, j, k: (lj, k, 0, j))),
           (x, pl.BlockSpec((tm, tn), lambda i, j, k: (i, j))), (gmul, pl.BlockSpec((1, tn), lambda i, j, k: (0, j)))]
    oblk = pl.BlockSpec((tm, tn), lambda i, j, k: (i, j))
    y, x_new = _mm(tag + "_down", ins, [(0, 1, 0)], 1, (tm, tn), [2, 3], epi_down, [SDS((t, d), BF16), SDS((t, d), F32)],
                   [oblk, oblk], (t // tm, d // tn, nc), "nn")
    return gu2, a, y, x_new


def _ffn_bwd(tag, lj, dy, h, gu2, a, wgu, wd4):
    t, d = dy.shape
    fc, nc = wgu.shape[3], HIDDEN_CHUNKS
    tm, tn, tk = _tile(t, (1024, 512, 256, 128)), _tile(d, (1024, 512, 256, 128)), _tile(d, (1024, 512, 256, 128))
    tt = t

    def epi_da(accs, ex):
        (da,), (gu,) = accs, ex
        gpre, up = gu[0].astype(F32), gu[1].astype(F32)
        s = jax.nn.sigmoid(gpre)
        silu = gpre * s
        dg = da * up * (s * (1.0 + gpre * (1.0 - s)))
        return ((dg, da * silu),)

    gblk = pl.BlockSpec((2, None, tm, fc), lambda i, c, k: (0, c, i, 0))
    ins = [(dy, pl.BlockSpec((tm, tk), lambda i, c, k: (i, k))),
           (wd4, pl.BlockSpec((None, None, fc, tk), lambda i, c, k: (lj, c, 0, k))), (gu2, gblk)]
    dgu2 = _mm(tag + "_da", ins, [(0, 1, 0)], 1, (tm, fc), [2], epi_da, [SDS((2, nc, t, fc), BF16)], [gblk],
               (t // tm, nc, d // tk), "nt")[0]

    ins = [(a, pl.BlockSpec((None, tt, fc), lambda c, j, k: (c, k, 0))), (dy, pl.BlockSpec((tt, tn), lambda c, j, k: (k, j)))]
    dwd = _mm(tag + "_dwd", ins, [(0, 1, 0)], 1, (fc, tn), [], _plain, [SDS((1, nc, fc, d), BF16)],
              [pl.BlockSpec((None, None, fc, tn), lambda c, j, k: (0, c, 0, j))], (nc, d // tn, t // tt), "tn")[0]

    ins = [(dgu2, pl.BlockSpec((None, None, tm, fc), lambda i, j, k: (k // nc, k % nc, i, 0))),
           (wgu, pl.BlockSpec((None, None, tn, fc), lambda i, j, k: (lj, k, j, 0)))]
    dh = _mm(tag + "_dh", ins, [(0, 1, 0)], 1, (tm, tn), [], _plain, [SDS((t, d), F32)],
             [pl.BlockSpec((tm, tn), lambda i, j, k: (i, j))], (t // tm, d // tn, 2 * nc), "nt")[0]

    ins = [(h, pl.BlockSpec((tt, tn), lambda i, c, k: (k, i))),
           (dgu2, pl.BlockSpec((None, None, tt, fc), lambda i, c, k: (c // nc, c % nc, k, 0)))]
    dwgu = _mm(tag + "_dwgu", ins, [(0, 1, 0)], 1, (tn, fc), [], _plain, [SDS((1, 2 * nc, d, fc), BF16)],
               [pl.BlockSpec((None, None, tn, fc), lambda i, c, k: (0, c, i, 0))], (d // tn, 2 * nc, t // tt), "tn")[0]
    return dh, dwgu, dwd


def _sb_block(t):
    return 256 if t >= 1024 else 128


def _split_hi_lo(v):
    hi = v.astype(BF16)
    return hi, (v - hi.astype(F32)).astype(BF16)


def _host_call(core, name, steps, ins, in_specs, out_shapes, out_specs, scratch, plan):
    n_in, n_out, n_scr = len(ins), len(out_shapes), len(scratch)
    c_ins, c_outs = (plan.ins, plan.outs) if plan else ([], [])
    n_cin, n_cout = len(c_ins), len(c_outs)

    def body(*refs):
        in_refs, c_in = refs[:n_in], refs[n_in:n_in + n_cin]
        rest = refs[n_in + n_cin:]
        out_refs, c_out = rest[:n_out], rest[n_out:n_out + n_cout]
        rest = rest[n_out + n_cout:]
        scr, sems = rest[:n_scr], rest[n_scr:]
        step = pl.program_id(0)
        if plan:
            @pl.when(step == 0)
            def _():
                plan.phases[0](c_in, c_out, *sems)

        core(in_refs, out_refs, scr)
        if plan:
            @pl.when(step == steps - 1)
            def _():
                plan.phases[1](c_in, c_out, *sems)
                plan.phases[2](c_in, c_out, *sems)

    return _pcall(
        body, name=name, grid=(steps,), in_specs=list(in_specs) + [ANY] * n_cin,
        out_specs=list(out_specs) + [ANY] * n_cout, out_shape=list(out_shapes) + list(c_outs),
        scratch_shapes=list(scratch) + (plan.scratch() if plan else []),
        compiler_params=_params(("arbitrary",)))(*ins, *c_ins)


def _sb_fwd(qkv, d, plan=None):
    t = qkv.shape[0]
    blk = _sb_block(t)
    nq = t // blk
    npair = d // LANES
    scale = HEAD_DIM ** -0.5

    def body(in_refs, out_refs, _):
        (q_ref, k_ref, v_ref), (o_ref, l_ref) = in_refs, out_refs
        lane = lax.broadcasted_iota(jnp.int32, (blk, LANES), 1)
        head0 = lane < HEAD_DIM
        row = lax.broadcasted_iota(jnp.int32, (blk, blk), 0)
        col = lax.broadcasted_iota(jnp.int32, (blk, blk), 1)
        causal = col < row
        after = (row > col).astype(BF16)

        def tile(qh, kb, carry, masked):
            cl, oacc = carry
            start = pl.multiple_of(kb * blk, blk)
            kv = k_ref[pl.ds(start, blk), :]
            vv = v_ref[pl.ds(start, blk), :]
            z = _dot(qh, kv, "nt") * scale
            sp = _softplus(z)
            lk = jnp.where(causal, -sp, 0.0) if masked else -sp
            hi, lo = _split_hi_lo(lk)
            later = _dot(hi, after, "nn") + _dot(lo, after, "nn") + cl
            logw = z - sp + later
            if masked:
                logw = jnp.where(causal, logw, -1e30)
            w = jnp.exp(logw)
            oacc = oacc + _dot(w.astype(BF16), vv, "nn")
            return cl + jnp.sum(lk, axis=1, keepdims=True), oacc

        def qblock(qi, _):
            qstart = pl.multiple_of(qi * blk, blk)
            qv = q_ref[pl.ds(qstart, blk), :]
            qhs = [jnp.where(head0 if hh == 0 else ~head0, qv, jnp.zeros_like(qv)) for hh in range(2)]
            zero = (jnp.zeros((blk, 1), F32), jnp.zeros((blk, LANES), F32))

            def both(kb, carries, masked):
                return tuple(tile(qh, kb, cr, masked) for qh, cr in zip(qhs, carries))

            outs = both(qi, (zero, zero), True)
            outs = lax.fori_loop(0, qi, lambda j, crs: both(qi - 1 - j, crs, False), outs)
            o_ref[pl.ds(qstart, blk), :] = jnp.where(head0, outs[0][1], outs[1][1]).astype(o_ref.dtype)
            l_ref[pl.ds(qstart, blk), :] = jnp.where(head0, outs[0][0], outs[1][0])
            return 0

        lax.fori_loop(0, nq, qblock, 0)

    return _host_call(
        body, "sb_fwd", npair, [qkv, qkv, qkv],
        [pl.BlockSpec((t, LANES), lambda p: (0, p)), pl.BlockSpec((t, LANES), lambda p: (0, npair + p)),
         pl.BlockSpec((t, LANES), lambda p: (0, 2 * npair + p))],
        [SDS((t, d), BF16), SDS((t, d), F32)],
        [pl.BlockSpec((t, LANES), lambda p: (0, p)), pl.BlockSpec((t, LANES), lambda p: (0, p))], [], plan)


def _sb_bwd(qkv, do, ltot, d, plan=None):
    t = qkv.shape[0]
    blk = _sb_block(t)
    nq = t // blk
    npair = d // LANES
    scale = HEAD_DIM ** -0.5

    def body(in_refs, out_refs, scr):
        (q_ref, k_ref, v_ref, do_ref, l_ref), (out_ref,), (dq_s, dk_s, dv_s) = in_refs, out_refs, scr
        lane = lax.broadcasted_iota(jnp.int32, (blk, LANES), 1)
        head0 = lane < HEAD_DIM
        row = lax.broadcasted_iota(jnp.int32, (blk, blk), 0)
        col = lax.broadcasted_iota(jnp.int32, (blk, blk), 1)
        causal = col < row
        upto = (row <= col).astype(BF16)
        before = (row < col).astype(BF16)
        dk_s[...] = jnp.zeros_like(dk_s)
        dv_s[...] = jnp.zeros_like(dv_s)

        def tile(qh, doh, lt, kb, carry, masked):
            plk, pda, dqacc = carry
            start = pl.multiple_of(kb * blk, blk)
            kv = k_ref[pl.ds(start, blk), :]
            vv = v_ref[pl.ds(start, blk), :]
            z = _dot(qh, kv, "nt") * scale
            sp = _softplus(z)
            lk = jnp.where(causal, -sp, 0.0) if masked else -sp
            hi, lo = _split_hi_lo(lk)
            later = lt - (plk + _dot(hi, upto, "nn") + _dot(lo, upto, "nn"))
            logw = z - sp + later
            if masked:
                logw = jnp.where(causal, logw, -1e30)
            w = jnp.exp(logw)
            da = _dot(doh, vv, "nt") * w
            pex = pda + _dot(da.astype(BF16), before, "nn")
            sig = jnp.exp(z - sp)
            dz = da * (1.0 - sig) - sig * pex
            if masked:
                dz = jnp.where(causal, dz, 0.0)
            dzs = (dz * scale).astype(BF16)
            dqacc = dqacc + _dot(dzs, kv, "nn")
            dk_s[pl.ds(start, blk), :] += _dot(dzs, qh, "tn")
            dv_s[pl.ds(start, blk), :] += _dot(w.astype(BF16), doh, "tn")
            return plk + jnp.sum(lk, axis=1, keepdims=True), pda + jnp.sum(da, axis=1, keepdims=True), dqacc

        def qblock(qi, _):
            qstart = pl.multiple_of(qi * blk, blk)
            qv = q_ref[pl.ds(qstart, blk), :]
            dov = do_ref[pl.ds(qstart, blk), :]
            lv = l_ref[pl.ds(qstart, blk), :]
            heads = []
            for hh in range(2):
                sel = head0 if hh == 0 else ~head0
                heads.append((jnp.where(sel, qv, jnp.zeros_like(qv)), jnp.where(sel, dov, jnp.zeros_like(dov)),
                              jnp.max(jnp.where(sel, lv, -jnp.inf), axis=1, keepdims=True)))
            zero = (jnp.zeros((blk, 1), F32), jnp.zeros((blk, 1), F32), jnp.zeros((blk, LANES), F32))

            def both(kb, carries, masked):
                return tuple(tile(qh, doh, lt, kb, cr, masked) for (qh, doh, lt), cr in zip(heads, carries))

            carries = lax.fori_loop(0, qi, lambda kb, crs: both(kb, crs, False), (zero, zero))
            carries = both(qi, carries, True)
            dq_s[pl.ds(qstart, blk), :] = jnp.where(head0, carries[0][2], carries[1][2])
            return 0

        lax.fori_loop(0, nq, qblock, 0)
        out_ref[0] = dq_s[...].astype(out_ref.dtype)
        out_ref[1] = dk_s[...].astype(out_ref.dtype)
        out_ref[2] = dv_s[...].astype(out_ref.dtype)

    col_blk = lambda off: pl.BlockSpec((t, LANES), lambda p: (0, off + p))
    return _host_call(
        body, "sb_bwd", npair, [qkv, qkv, qkv, do, ltot],
        [col_blk(0), col_blk(npair), col_blk(2 * npair), col_blk(0), col_blk(0)],
        [SDS((3, t, d), BF16)], [pl.BlockSpec((3, t, LANES), lambda p: (0, 0, p))],
        [pltpu.VMEM((t, LANES), F32) for _ in range(3)], plan)


def _roll_rows(v, shift):
    return pltpu.roll(v, shift, 0)


def _shift_down(v, dist, fill, row):
    return jnp.where(row >= dist, _roll_rows(v, dist), fill)


def _shift_up(v, dist, fill, row):
    t = v.shape[0]
    return jnp.where(row < t - dist, _roll_rows(v, t - dist), fill)


def _lru_gates(xb, small, wr, wi, row):
    xs = [_shift_down(xb, 3 - tap, 0.0, row) if tap < 3 else xb for tap in range(4)]
    xc = small[4:5, :] + xs[0] * small[0:1, :]
    for tap in range(1, 4):
        xc = xc + xs[tap] * small[tap:tap + 1, :]
    xcb = xc.astype(BF16)
    r = jax.nn.sigmoid(_dot(xcb, wr, "nn") + small[5:6, :])
    ig = jax.nn.sigmoid(_dot(xcb, wi, "nn") + small[6:7, :])
    sp = _softplus(-small[7:8, :])
    la = -LRU_C * r * sp
    a = jnp.exp(la)
    th = jnp.tanh(la)
    mult = jnp.sqrt(-2.0 * th / (1.0 - th))
    return xs, xc, xcb, r, ig, sp, a, mult


def _gelu_parts(gate):
    inner = GELU_C * (gate + GELU_K * gate * gate * gate)
    th = jnp.tanh(inner)
    gelu = 0.5 * gate * (1.0 + th)
    dgelu = 0.5 * (1.0 + th) + 0.5 * gate * (1.0 - th * th) * GELU_C * (1.0 + 3.0 * GELU_K * gate * gate)
    return gelu, dgelu


def _scan_steps(t):
    steps, dist = [], 1
    while dist < t:
        steps.append(dist)
        dist *= 2
    return steps


def _lru_fwd(gx, small, wr, wi):
    t = gx.shape[0]
    r_dim = gx.shape[1] // 2
    nb = r_dim // LRU_BLOCK_W

    def body(gate_ref, xb_ref, small_ref, wr_ref, wi_ref, y_ref, hs_ref):
        row = lax.broadcasted_iota(jnp.int32, (t, LRU_BLOCK_W), 0)
        xb = xb_ref[...]
        _, xc, _, _, ig, _, a, mult = _lru_gates(xb, small_ref, wr_ref[...], wi_ref[...], row)
        b = mult * (ig * xc)
        for dist in _scan_steps(t):
            b = a * _shift_down(b, dist, 0.0, row) + b
            a = a * _shift_down(a, dist, 1.0, row)
        hs_ref[...] = b
        gelu, _ = _gelu_parts(gate_ref[...])
        y_ref[...] = (gelu * b).astype(y_ref.dtype)

    colb = lambda off: pl.BlockSpec((t, LRU_BLOCK_W), lambda n: (0, off + n))
    wspec = pl.BlockSpec((None, LRU_BLOCK_W, LRU_BLOCK_W), lambda n: (n, 0, 0))
    return _pcall(
        body, name="lru_fwd", grid=(nb,),
        in_specs=[colb(0), colb(nb), pl.BlockSpec((8, LRU_BLOCK_W), lambda n: (0, n)), wspec, wspec],
        out_specs=[colb(0), colb(0)], out_shape=[SDS((t, r_dim), BF16), SDS((t, r_dim), F32)],
        compiler_params=_params(("parallel",)))(gx, gx, small, wr, wi)


def _lru_bwd(gx, hs, dy, small, wr, wi):
    t = gx.shape[0]
    r_dim = gx.shape[1] // 2
    nb = r_dim // LRU_BLOCK_W

    def body(gate_ref, xb_ref, hs_ref, dy_ref, small_ref, wr_ref, wi_ref, dgx_ref, dsm_ref, dwr_ref, dwi_ref):
        row = lax.broadcasted_iota(jnp.int32, (t, LRU_BLOCK_W), 0)
        xb, hsv, dyv, smallv = xb_ref[...], hs_ref[...], dy_ref[...], small_ref
        wrv, wiv = wr_ref[...], wi_ref[...]
        xs, xc, xcb, r, ig, sp, a, mult = _lru_gates(xb, smallv, wrv, wiv, row)
        gelu, dgelu = _gelu_parts(gate_ref[...])
        dgx_ref[0] = (dyv * hsv * dgelu).astype(dgx_ref.dtype)
        dacc = dyv * gelu
        an = _shift_up(a, 1, 1.0, row)
        for dist in _scan_steps(t):
            dacc = dacc + an * _shift_up(dacc, dist, 0.0, row)
            an = an * _shift_up(an, dist, 1.0, row)
        da = dacc * _shift_down(hsv, 1, 0.0, row)
        dmult = dacc * (ig * xc)
        dixc = dacc * mult
        dla = da * a - dmult * (a * a) / mult
        dr = dla * (-LRU_C * sp)
        dsp = jnp.sum(dla * (-LRU_C * r), axis=0, keepdims=True)
        dpr = dr * r * (1.0 - r)
        dpi = dixc * xc * ig * (1.0 - ig)
        dprb, dpib = dpr.astype(BF16), dpi.astype(BF16)
        dwr_ref[...] = _dot(xcb, dprb, "tn")
        dwi_ref[...] = _dot(xcb, dpib, "tn")
        dxc = dixc * ig + _dot(dprb, wrv, "nt") + _dot(dpib, wiv, "nt")
        dxb = dxc * smallv[3:4, :]
        for tap in range(3):
            dxb = dxb + _shift_up(dxc, 3 - tap, 0.0, row) * smallv[tap:tap + 1, :]
        dgx_ref[1] = dxb.astype(dgx_ref.dtype)
        lam = smallv[7:8, :]
        rows = [jnp.sum(dxc * xs[tap], axis=0, keepdims=True) for tap in range(4)]
        rows.append(jnp.sum(dxc, axis=0, keepdims=True))
        rows.append(jnp.sum(dpr, axis=0, keepdims=True))
        rows.append(jnp.sum(dpi, axis=0, keepdims=True))
        rows.append(-dsp * jax.nn.sigmoid(-lam))
        for k, rv in enumerate(rows):
            dsm_ref[k:k + 1, :] = rv

    colb = lambda off: pl.BlockSpec((t, LRU_BLOCK_W), lambda n: (0, off + n))
    wspec = pl.BlockSpec((None, LRU_BLOCK_W, LRU_BLOCK_W), lambda n: (n, 0, 0))
    sspec = pl.BlockSpec((8, LRU_BLOCK_W), lambda n: (0, n))
    return _pcall(
        body, name="lru_bwd", grid=(nb,),
        in_specs=[colb(0), colb(nb), colb(0), colb(0), sspec, wspec, wspec],
        out_specs=[pl.BlockSpec((2, t, LRU_BLOCK_W), lambda n: (0, 0, n)), sspec, wspec, wspec],
        out_shape=[SDS((2, t, r_dim), BF16), SDS((8, r_dim), F32), SDS((nb, LRU_BLOCK_W, LRU_BLOCK_W), F32),
                   SDS((nb, LRU_BLOCK_W, LRU_BLOCK_W), F32)],
        compiler_params=_params(("parallel",)))(gx, gx, hs, dy, small, wr, wi)


def _adam(w, g, m, v):
    m2 = ADAM_B1 * m + (1.0 - ADAM_B1) * g
    v2 = ADAM_B2 * v + (1.0 - ADAM_B2) * (g * g)
    m_hat = m2 / (1.0 - ADAM_B1 ** ADAM_STEP)
    v_hat = v2 / (1.0 - ADAM_B2 ** ADAM_STEP)
    return -ADAM_LR * (m_hat / (jnp.sqrt(v_hat) + ADAM_EPS) + ADAM_WD * w), m2, v2


def _mod_fwd(c_all, mod_w, mod_b_cols):
    nl, d, cols = mod_w.shape
    nbatch = c_all.shape[0]

    def body(c_ref, w_ref, b_ref, o_ref):
        cv = c_ref[...]
        ca = (cv * jax.nn.sigmoid(cv)).astype(BF16)
        o_ref[...] = _dot(ca, w_ref[...].astype(BF16), "nn") + b_ref[...]

    return _pcall(
        body, name="mod_fwd", grid=(nl,),
        in_specs=[pl.BlockSpec((nbatch, d), lambda l: (0, 0)), pl.BlockSpec((None, d, cols), lambda l: (l, 0, 0)),
                  pl.BlockSpec((None, 1, cols), lambda l: (l, 0, 0))],
        out_specs=pl.BlockSpec((None, nbatch, cols), lambda l: (l, 0, 0)), out_shape=SDS((nl, nbatch, cols), F32),
        compiler_params=_params(("parallel",)))(c_all, mod_w, mod_b_cols)


def _mod_w_update(c_all, dmod_cols, w, m, v):
    nl, d, cols = w.shape
    nbatch = c_all.shape[0]
    tr = _tile(d, (256, 128))

    def body(c_ref, dm_ref, w_ref, m_ref, v_ref, g_ref, dl_ref, m2_ref, v2_ref):
        cv = c_ref[...]
        ca = (cv * jax.nn.sigmoid(cv)).astype(BF16)
        g = _dot(ca, dm_ref[...].astype(BF16), "tn")
        g_ref[...] = g
        dl_ref[...], m2_ref[...], v2_ref[...] = _adam(w_ref[...], g, m_ref[...], v_ref[...])

    wblk = pl.BlockSpec((None, tr, cols), lambda l, i: (l, i, 0))
    return _pcall(
        body, name="mod_w_update", grid=(nl, d // tr),
        in_specs=[pl.BlockSpec((nbatch, tr), lambda l, i: (0, i)), pl.BlockSpec((None, nbatch, cols), lambda l, i: (l, 0, 0)),
                  wblk, wblk, wblk],
        out_specs=[wblk] * 4, out_shape=[SDS(w.shape, F32)] * 4,
        compiler_params=_params(("parallel", "parallel")))(c_all, dmod_cols, w, m, v)


def _adam_update(name, w, m, v, gparts):
    rows, cols = w.shape
    tr = _tile(rows, (256, 128, 64, 32, 16, 8))
    npart = len(gparts)

    def body(*refs):
        w_ref, m_ref, v_ref = refs[:3]
        g_refs = refs[3:3 + npart]
        g_ref, dl_ref, m2_ref, v2_ref = refs[3 + npart:]
        g = g_refs[0][...].astype(F32)
        for gr in g_refs[1:]:
            g = g + gr[...].astype(F32)
        g_ref[...] = g
        dl_ref[...], m2_ref[...], v2_ref[...] = _adam(w_ref[...], g, m_ref[...], v_ref[...])

    blk = pl.BlockSpec((tr, cols), lambda i: (i, 0))
    return _pcall(body, name=name, grid=(rows // tr,), in_specs=[blk] * (3 + npart), out_specs=[blk] * 4,
                  out_shape=[SDS((rows, cols), F32)] * 4, compiler_params=_params(("parallel",)))(w, m, v, *gparts)


def _adam_shard(name, w, m, v, part4, recv3, chip_idx, first=0, fills=None):
    p, r, cdim = w.shape
    pg = part4.shape[0]
    tr = _tile(r, (256, 176, 160, 128, 64, 32, 16))

    def body(chip_ref, w_ref, m_ref, v_ref, own_ref, r0_ref, r1_ref, r2_ref, *rest):
        g_ref, dl_ref, m2_ref, v2_ref = rest[-4:]
        g = own_ref[...].astype(F32) + r0_ref[...].astype(F32) + r1_ref[...].astype(F32) + r2_ref[...].astype(F32)
        g_ref[...] = g
        dl_ref[...], m2_ref[...], v2_ref[...] = _adam(w_ref[...], g, m_ref[...], v_ref[...])

    blk = pl.BlockSpec((None, tr, cdim), lambda q, i, chip_ref: (first + q, i, 0))
    blk4 = (None, None, tr, cdim)
    slot = lambda s: pl.BlockSpec(blk4, lambda q, i, chip_ref: (s, q, i, 0))
    fills = list(fills or [])
    grid_spec = pltpu.PrefetchScalarGridSpec(
        num_scalar_prefetch=1, grid=(pg, r // tr),
        in_specs=[blk, blk, blk, pl.BlockSpec(blk4, lambda q, i, chip_ref: (q, chip_ref[0], i, 0)), slot(0), slot(1), slot(2)]
        + [ANY] * len(fills),
        out_specs=[blk] * 4)
    return _pcall(body, name=name, grid_spec=grid_spec, out_shape=[SDS((p, r, cdim), F32)] * 4,
                  input_output_aliases={8 + k: k for k in range(len(fills))},
                  compiler_params=_params(("parallel", "parallel")))(chip_idx, w, m, v, part4, recv3, recv3, recv3, *fills)


def _sum_devices(gathered, name):
    _, rows, cols = gathered.shape
    tr = _tile(rows, (512, 256, 128, 64, 32, 16, 8))

    def body(g_ref, o_ref):
        acc = g_ref[0]
        for k in range(1, N_DEV):
            acc = acc + g_ref[k]
        o_ref[...] = acc

    return _pcall(body, name=name, grid=(rows // tr,), in_specs=[pl.BlockSpec((N_DEV, tr, cols), lambda i: (0, i, 0))],
                  out_specs=pl.BlockSpec((tr, cols), lambda i: (i, 0)), out_shape=SDS((rows, cols), F32),
                  compiler_params=_params(("parallel",)))(gathered)


def _pack_flat(parts, width, row_mult, dtype):
    flat = jnp.concatenate([p.reshape(-1).astype(dtype) for p in parts])
    unit = width * row_mult
    pad = (-flat.shape[0]) % unit
    if pad:
        flat = jnp.concatenate([flat, jnp.zeros((pad,), dtype)])
    return flat.reshape(-1, width)


def _unpack_flat(flat, shapes):
    out, off = [], 0
    for shp in shapes:
        size = math.prod(shp)
        out.append(flat[off:off + size].reshape(shp))
        off += size
    return out


def kernel(x, c, mod_w, mod_b, norm_g, ffn_w_gu, ffn_w_down, sb_w_qkv, sb_w_o, lru_w_in, lru_conv_w, lru_conv_b, lru_w_r, lru_b_r, lru_w_i, lru_b_i, lru_lambda, lru_w_out, final_norm_g, loss_target, m_mod_w, m_mod_b, m_norm_g, m_ffn_w_gu, m_ffn_w_down, m_sb_w_qkv, m_sb_w_o, m_lru_w_in, m_lru_conv_w, m_lru_conv_b, m_lru_w_r, m_lru_b_r, m_lru_w_i, m_lru_b_i, m_lru_lambda, m_lru_w_out, m_final_norm_g, v_mod_w, v_mod_b, v_norm_g, v_ffn_w_gu, v_ffn_w_down, v_sb_w_qkv, v_sb_w_o, v_lru_w_in, v_lru_conv_w, v_lru_conv_b, v_lru_w_r, v_lru_b_r, v_lru_w_i, v_lru_b_i, v_lru_lambda, v_lru_w_out, v_final_norm_g):
    weights = dict(mod_w=mod_w, mod_b=mod_b, norm_g=norm_g, ffn_w_gu=ffn_w_gu, ffn_w_down=ffn_w_down, sb_w_qkv=sb_w_qkv,
                   sb_w_o=sb_w_o, lru_w_in=lru_w_in, lru_conv_w=lru_conv_w, lru_conv_b=lru_conv_b, lru_w_r=lru_w_r,
                   lru_b_r=lru_b_r, lru_w_i=lru_w_i, lru_b_i=lru_b_i, lru_lambda=lru_lambda, lru_w_out=lru_w_out,
                   final_norm_g=final_norm_g)
    mom_m = dict(mod_w=m_mod_w, mod_b=m_mod_b, norm_g=m_norm_g, ffn_w_gu=m_ffn_w_gu, ffn_w_down=m_ffn_w_down,
                 sb_w_qkv=m_sb_w_qkv, sb_w_o=m_sb_w_o, lru_w_in=m_lru_w_in, lru_conv_w=m_lru_conv_w,
                 lru_conv_b=m_lru_conv_b, lru_w_r=m_lru_w_r, lru_b_r=m_lru_b_r, lru_w_i=m_lru_w_i, lru_b_i=m_lru_b_i,
                 lru_lambda=m_lru_lambda, lru_w_out=m_lru_w_out, final_norm_g=m_final_norm_g)
    mom_v = dict(mod_w=v_mod_w, mod_b=v_mod_b, norm_g=v_norm_g, ffn_w_gu=v_ffn_w_gu, ffn_w_down=v_ffn_w_down,
                 sb_w_qkv=v_sb_w_qkv, sb_w_o=v_sb_w_o, lru_w_in=v_lru_w_in, lru_conv_w=v_lru_conv_w,
                 lru_conv_b=v_lru_conv_b, lru_w_r=v_lru_w_r, lru_b_r=v_lru_b_r, lru_w_i=v_lru_w_i, lru_b_i=v_lru_b_i,
                 lru_lambda=v_lru_lambda, lru_w_out=v_lru_w_out, final_norm_g=v_final_norm_g)
    names = list(weights)

    t, d = x.shape[1], x.shape[2]
    n_layers = mod_w.shape[0]
    r_dim = lru_w_out.shape[1] * N_DEV
    ng, rs = d // N_DEV, r_dim // N_DEV
    mod_cols = mod_w.shape[2]
    nblk = lru_w_r.shape[1]
    xi, yi, ci = _mesh_pos()
    me = 4 * xi + 2 * yi + ci
    chip = 2 * xi + yi
    x2, target = x.reshape(t, d), loss_target.reshape(t, d)

    lru_small_shard = jnp.concatenate([lru_conv_w[0], lru_conv_b, lru_b_r, lru_b_i, lru_lambda], axis=0)
    small1 = _pack_flat([c, norm_g, lru_small_shard], LANES, 8, F32)
    n_small1 = small1.shape[0]
    all1 = _allgather(small1[None], "gather_small").reshape(N_DEV, n_small1 * LANES)
    c_all = all1[:, :d]
    norm_full = jnp.transpose(all1[:, d:d + 6 * ng].reshape(N_DEV, n_layers, 3, ng), (1, 2, 0, 3)).reshape(n_layers, 3, d)
    lru_small = jnp.transpose(all1[:, d + 6 * ng:d + 6 * ng + 8 * rs].reshape(N_DEV, 8, rs), (1, 0, 2)).reshape(8, r_dim)

    mod_b_cols = lax.dynamic_slice_in_dim(mod_b, me * mod_cols, mod_cols, axis=1).reshape(n_layers, 1, mod_cols)
    mod_part = _mod_fwd(c_all, mod_w, mod_b_cols)
    mod_all = _allgather(mod_part, "gather_mod")
    mod_mine = lax.dynamic_index_in_dim(mod_all, me, axis=2, keepdims=False)
    mod_mine = mod_mine.reshape(n_layers, 3, 3, d)

    assert sb_w_qkv.shape[0] == 1 and lru_w_in.shape[0] == 1, "one stick-breaking and one RG-LRU layer"
    n_ffn = 2 * n_layers
    fc = ffn_w_gu.shape[3]
    cw_in = lru_w_in.shape[2]
    pieces = {("ffn_w_gu", q): ffn_w_gu[q // 2, q % 2][None] for q in range(n_ffn)}
    pieces.update({("ffn_w_down", q): ffn_w_down[q // 2, q % 2][None] for q in range(n_ffn)})
    pieces.update({("sb_w_qkv", 0): sb_w_qkv, ("sb_w_o", 0): sb_w_o, ("lru_w_in", 0): lru_w_in, ("lru_w_out", 0): lru_w_out})
    col_window = {("sb_w_qkv", 0)}
    early = [("ffn_w_gu", 0), ("ffn_w_down", 0), ("sb_w_qkv", 0), ("sb_w_o", 0)]
    late = [key for key in pieces if key not in early]
    gathered = {}

    def gather_plan(keys):
        return _gather_plan([pieces[key].astype(BF16) for key in keys], [key in col_window for key in keys])

    gathered.update(zip(early, _run_comm(gather_plan(early), "gather_early")))
    wr_b, wi_b = lru_w_r[0].astype(BF16), lru_w_i[0].astype(BF16)
    eye2 = jnp.eye(2 * cw_in, dtype=BF16).reshape(2, cw_in, 2 * cw_in)

    def w_gu(q):
        return gathered[("ffn_w_gu", q)]

    def w_d4(q):
        return gathered[("ffn_w_down", q)].reshape(1, HIDDEN_CHUNKS, fc, d)

    saved = []
    xcur = x2
    for layer in range(n_layers):
        for sub in range(3):
            gvec = norm_full[layer, sub].reshape(1, d)
            shift = mod_mine[layer, sub, 0].reshape(1, d)
            scale1p = 1.0 + mod_mine[layer, sub, 1].reshape(1, d)
            gmul = 1.0 + mod_mine[layer, sub, 2].reshape(1, d)
            tag = f"l{layer}s{sub}"
            h = _norm_fwd(xcur, gvec, scale1p, shift, tag + "_norm")
            rec = dict(x=xcur, h=h, g=gvec, scale1p=scale1p, gmul=gmul, w=MACARON_W if sub != 1 else 1.0)
            if sub != 1:
                lj = layer * 2 + sub // 2
                gu2, a, yv, xcur = _ffn_fwd(tag, 0, xcur, h, w_gu(lj), w_d4(lj), gmul)
                rec.update(kind="ffn", lj=lj, gu2=gu2, a=a, y=yv)
            elif layer % 2 == 0:
                w_qkv = gathered[("sb_w_qkv", 0)][0]
                w_o = gathered[("sb_w_o", 0)].reshape(d, d)
                qkv = _mm_nn(tag + "_qkv", h, w_qkv, BF16)[0]
                o, ltot, *landed = _sb_fwd(qkv, d, gather_plan(late))
                gathered.update(zip(late, landed))
                yv, xcur = _mm_nn(tag + "_wo", o, w_o, [BF16, F32], extras=[(xcur, "tile"), (gmul, "row")],
                                  epilogue=lambda accs, ex: (accs[0], ex[0] + ex[1] * accs[0]))
                rec.update(kind="sb", qkv=qkv, o=o, ltot=ltot, y=yv, w_qkv=w_qkv, w_o=w_o)
            else:
                w_in = _chunks_to_cols("lru_w_in_cols", gathered[("lru_w_in", 0)][0], eye2)
                w_out = gathered[("lru_w_out", 0)].reshape(r_dim, d)
                gx = _mm_nn(tag + "_win", h, w_in, F32)[0]
                ymix, hs = _lru_fwd(gx, lru_small, wr_b, wi_b)
                yv, xcur = _mm_nn(tag + "_wout", ymix, w_out, [BF16, F32], extras=[(xcur, "tile"), (gmul, "row")],
                                  epilogue=lambda accs, ex: (accs[0], ex[0] + ex[1] * accs[0]))
                rec.update(kind="lru", gx=gx, hs=hs, ymix=ymix, y=yv, w_in=w_in, w_out=w_out)
            saved.append(rec)

    last = saved[-1]
    dxo, dy, head_sums = _loss_head(xcur, target, final_norm_g.reshape(1, d), (last["w"] * last["gmul"]))
    loss = lax.psum(head_sums[1, 0], ("x", "y", "c"))
    dgf = head_sums[0]

    c_idx = jnp.reshape(ci, (1,)).astype(jnp.int32)
    chip_idx = jnp.reshape(chip, (1,)).astype(jnp.int32)
    grads, reduced = {}, {}

    def pair_sums(keys, tag):
        arrs, cols = [grads[key] for key in keys], [key in col_window for key in keys]
        recv4 = _run_comm(_exchange_plan(arrs, cols, 4, _sibling_route), "rs_sibling_" + tag)
        return [_pair_sum(g, r4, c_idx, f"rs_pair_sum_{key[0]}{key[1]}", cols=cl)
                for key, g, r4, cl in zip(keys, arrs, recv4, cols)]

    dmod = [[None] * 3 for _ in range(n_layers)]
    dnorm = [[None] * 3 for _ in range(n_layers)]
    dlru_small = dwr = dwi = None
    for idx in reversed(range(len(saved))):
        rec = saved[idx]
        layer, sub = divmod(idx, 3)
        tag = f"l{layer}s{sub}b"
        if rec["kind"] == "ffn":
            lj = rec["lj"]
            dh, dwgu, dwd = _ffn_bwd(tag, 0, dy, rec["h"], rec["gu2"], rec["a"], w_gu(lj), w_d4(lj))
            grads[("ffn_w_gu", lj)] = dwgu
            grads[("ffn_w_down", lj)] = dwd.reshape(gathered[("ffn_w_down", lj)].shape)
        elif rec["kind"] == "sb":
            ready = list(grads)
            part4 = pair_sums(ready, "ready")
            do = _mm_nt(tag + "_do", dy, rec["w_o"], BF16)
            dwo = _mm_tn(tag + "_dwo", rec["o"], dy, BF16)
            dqkv3, *recv3 = _sb_bwd(rec["qkv"], do, rec["ltot"], d, _exchange_plan(part4, [False] * len(ready), 3, _chip_route))
            reduced.update({key: (p4, r3) for key, p4, r3 in zip(ready, part4, recv3)})
            grads[("sb_w_o", 0)] = dwo.reshape(gathered[("sb_w_o", 0)].shape)
            dh = _mm_nt_stack(tag + "_dh", dqkv3, rec["w_qkv"], F32)
            grads[("sb_w_qkv", 0)] = _mm_tn_stack(tag + "_dwqkv", rec["h"], dqkv3, BF16)[None]
        else:
            dymix = _mm_nt(tag + "_dymix", dy, rec["w_out"], F32)
            grads[("lru_w_out", 0)] = _mm_tn(tag + "_dwout", rec["ymix"], dy, BF16).reshape(gathered[("lru_w_out", 0)].shape)
            dgx2, dlru_small, dwr, dwi = _lru_bwd(rec["gx"], rec["hs"], dymix, lru_small, wr_b, wi_b)
            dh = _mm_nt_stack(tag + "_dh", dgx2, rec["w_in"], F32)
            dw_in = _mm_tn_stack(tag + "_dwin", rec["h"], dgx2, BF16)
            grads[("lru_w_in", 0)] = _cols_to_chunks("lru_w_in_chunks", dw_in, eye2)[None]
        prev = saved[idx - 1] if idx > 0 else None
        gw_prev = (prev["w"] * prev["gmul"]) if prev is not None else jnp.zeros((1, d), F32)
        dxo, dy, sums = _adaln_bwd(dh, rec["x"], rec["y"], dxo, rec["g"], rec["scale1p"], rec["w"], gw_prev, tag + "_adaln")
        dmod[layer][sub] = sums[0:3]
        dnorm[layer][sub] = sums[3]
    grad_x = dxo.reshape(x.shape)

    dmod_mine = jnp.stack([jnp.stack(dmod[layer]) for layer in range(n_layers)])
    dnorm_mine = jnp.stack([jnp.stack(dnorm[layer]) for layer in range(n_layers)])
    small_shapes = [(n_layers, 9 * d), (n_layers, 3, d), (8, r_dim), (d,), (nblk, LRU_BLOCK_W, LRU_BLOCK_W),
                    (nblk, LRU_BLOCK_W, LRU_BLOCK_W)]
    small3 = _pack_flat([dmod_mine, dnorm_mine, dlru_small, dgf, dwr, dwi], LANES, 512, F32)
    n_small3 = small3.shape[0]
    all3 = _allgather(small3[None], "gather_small_grads").reshape(N_DEV, n_small3, LANES)
    gsum = _sum_devices(all3, "sum_small_grads").reshape(-1)
    g_mod_b, g_norm_full, g_lru_small, g_final, g_wr, g_wi = _unpack_flat(gsum, small_shapes)
    dmod_all = all3.reshape(N_DEV, -1)[:, :n_layers * 9 * d].reshape(N_DEV, n_layers, N_DEV, mod_cols)
    dmod_cols = jnp.transpose(lax.dynamic_index_in_dim(dmod_all, me, axis=2, keepdims=False), (1, 0, 2))

    out_g, out_d, out_m, out_v = {}, {}, {}, {}
    out_g["mod_w"], out_d["mod_w"], out_m["mod_w"], out_v["mod_w"] = _mod_w_update(c_all, dmod_cols, mod_w, m_mod_w, v_mod_w)

    g_norm_shard = lax.dynamic_slice_in_dim(g_norm_full, me * ng, ng, axis=2)
    g_lru_shard = lax.dynamic_slice_in_dim(g_lru_small, me * rs, rs, axis=1)
    small_grads = dict(mod_b=g_mod_b, norm_g=g_norm_shard, lru_conv_w=g_lru_shard[0:4].reshape(lru_conv_w.shape),
                       lru_conv_b=g_lru_shard[4:5], lru_b_r=g_lru_shard[5:6], lru_b_i=g_lru_shard[6:7],
                       lru_lambda=g_lru_shard[7:8], final_norm_g=g_final, lru_w_r=g_wr.reshape(lru_w_r.shape),
                       lru_w_i=g_wi.reshape(lru_w_i.shape))
    small_names = list(small_grads)
    sw = _pack_flat([weights[n] for n in small_names], LANES, 256, F32)
    sg = _pack_flat([small_grads[n] for n in small_names], LANES, 256, F32)
    sm = _pack_flat([mom_m[n] for n in small_names], LANES, 256, F32)
    sv = _pack_flat([mom_v[n] for n in small_names], LANES, 256, F32)
    s_outs = _adam_update("adam_small", sw, sm, sv, [sg])
    small_shapes2 = [weights[n].shape for n in small_names]
    for dst, flat in zip((out_g, out_d, out_m, out_v), s_outs):
        for n, arr in zip(small_names, _unpack_flat(flat.reshape(-1), small_shapes2)):
            dst[n] = arr

    rest = [key for key in grads if key not in reduced]
    part4 = pair_sums(rest, "rest")
    recv3 = _run_comm(_exchange_plan(part4, [False] * len(rest), 3, _chip_route), "rs_chips_rest")
    reduced.update({key: (p4, r3) for key, p4, r3 in zip(rest, part4, recv3)})
    for n in ["ffn_w_gu", "ffn_w_down", "sb_w_qkv", "sb_w_o", "lru_w_in", "lru_w_out"]:
        shp = weights[n].shape
        shard3 = (math.prod(shp[:-2]),) + shp[-2:]
        view = lambda arr: arr.reshape(shard3)
        outs = None
        for q in range(shard3[0]):
            fills = outs if outs is not None else [lax.empty(shard3, F32) for _ in range(4)]
            p4, r3 = reduced[(n, q)]
            outs = _adam_shard(f"adam_{n}{q}", view(weights[n]), view(mom_m[n]), view(mom_v[n]), p4, r3, chip_idx,
                               first=q, fills=fills if shard3[0] > 1 else None)
        out_g[n], out_d[n], out_m[n], out_v[n] = [o.reshape(shp) for o in outs]

    return (loss, grad_x, *[out_g[n] for n in names], *[out_d[n] for n in names], *[out_m[n] for n in names],
            *[out_v[n] for n in names])
```

```python
import functools
import math

import jax
import jax.numpy as jnp
from jax import lax
from jax.experimental import pallas as pl
from jax.experimental.pallas import tpu as pltpu

F32 = jnp.float32
BF16 = jnp.bfloat16
SDS = jax.ShapeDtypeStruct
MESH = pl.DeviceIdType.MESH
ANY = pl.BlockSpec(memory_space=pl.ANY)

N_DEV = 8
LANES = 128
HEAD_DIM = 64
LRU_BLOCK_W = 128
LRU_C = 8.0
MACARON_W = 0.5
NORM_EPS = 1e-6
ADAM_LR = 0.001
ADAM_B1 = 0.9
ADAM_B2 = 0.999
ADAM_EPS = 1e-08
ADAM_WD = 0.01
ADAM_STEP = 10
VMEM_LIMIT = 56 * 1024 * 1024
GELU_C = math.sqrt(2.0 / math.pi)
GELU_K = 0.044715

DIMS = {
    "nn": (((1,), (0,)), ((), ())),
    "nt": (((1,), (1,)), ((), ())),
    "tn": (((0,), (0,)), ((), ())),
}


def _pcall(body, **kw):
    return pl.pallas_call(body, **kw)


def _params(sem=None):
    return pltpu.CompilerParams(dimension_semantics=sem, vmem_limit_bytes=VMEM_LIMIT)


def _tile(n, prefs):
    for p in prefs:
        if n % p == 0:
            return p
    return n


def _dot(a, b, dims):
    return lax.dot_general(a, b, DIMS[dims], preferred_element_type=F32)


def _softplus(z):
    return jnp.maximum(z, 0.0) + jnp.log(1.0 + jnp.exp(-jnp.abs(z)))


def _mesh_pos():
    return lax.axis_index("x"), lax.axis_index("y"), lax.axis_index("c")


def _allgather(xs, name, cols=False):
    return _run_comm(_gather_plan([xs], [cols]), name)[0]


class _CommPlan:
    def __init__(self, ins, outs, n_remote, n_local, phases):
        self.ins, self.outs, self.n_remote, self.n_local, self.phases = ins, outs, n_remote, n_local, phases

    def scratch(self):
        return [pltpu.SemaphoreType.DMA((self.n_remote,)), pltpu.SemaphoreType.DMA((self.n_remote,)),
                pltpu.SemaphoreType.DMA((max(self.n_local, 1),))]


def _merge_plans(plans):
    plans = [p for p in plans if p is not None]
    if len(plans) <= 1:
        return plans[0] if plans else None

    def phase(k):
        def run(in_refs, out_refs, send_sems, recv_sems, local_sems, r0=0, l0=0):
            i0 = o0 = 0
            for p in plans:
                p.phases[k](in_refs[i0:i0 + len(p.ins)], out_refs[o0:o0 + len(p.outs)], send_sems, recv_sems, local_sems, r0, l0)
                i0, o0, r0, l0 = i0 + len(p.ins), o0 + len(p.outs), r0 + p.n_remote, l0 + p.n_local
        return run

    return _CommPlan(sum([p.ins for p in plans], []), sum([p.outs for p in plans], []), sum(p.n_remote for p in plans),
                     sum(p.n_local for p in plans), [phase(0), phase(1), phase(2)])


def _run_comm(plan, name):
    n_in, n_out = len(plan.ins), len(plan.outs)

    def body(*refs):
        in_refs, out_refs, sems = refs[:n_in], refs[n_in:n_in + n_out], refs[n_in + n_out:]
        for phase in plan.phases:
            phase(in_refs, out_refs, *sems)

    return _pcall(body, name=name, out_shape=plan.outs, in_specs=[ANY] * n_in, out_specs=[ANY] * n_out,
                  scratch_shapes=plan.scratch())(*plan.ins)


def _col_window(ref, idx, width):
    return ref.at[:, :, pl.ds(pl.multiple_of(idx * width, math.gcd(width, LANES)), width)]


def _gather_plan(shards, cols):
    n = len(shards)
    outs = [SDS((s.shape[0], s.shape[1], N_DEV * s.shape[2]) if cl else (s.shape[0], N_DEV) + s.shape[1:], s.dtype)
            for s, cl in zip(shards, cols)]

    def copies(a, in_refs, out_refs, send_sems, recv_sems, local_sems, r0=0, l0=0):
        x, y, c = _mesh_pos()
        sibling = (x, y, 1 - c)
        chips = [(1 - x, y), (x, 1 - y), (1 - x, 1 - y)]
        width = shards[a].shape[2]

        def block(px, py, pc):
            idx = 4 * px + 2 * py + pc
            return _col_window(out_refs[a], idx, width) if cols[a] else out_refs[a].at[:, idx]

        def copy(k, owner, to, src=None):
            sem = r0 + 7 * a + k
            return pltpu.make_async_remote_copy(
                src_ref=block(*owner) if src is None else src, dst_ref=block(*owner),
                send_sem=send_sems.at[sem], recv_sem=recv_sems.at[sem], device_id=to, device_id_type=MESH)

        me = (x, y, c)
        first = [copy(0, me, sibling, src=in_refs[a])]
        first += [copy(1 + j, me, (*chip, c), src=in_refs[a]) for j, chip in enumerate(chips)]
        passed = [copy(4 + j, (*chip, c), sibling) for j, chip in enumerate(chips)]
        landed = [copy(1 + j, (*chip, c), me) for j, chip in enumerate(chips)]
        from_sibling = [copy(0, sibling, me)] + [copy(4 + j, (*chip, 1 - c), me) for j, chip in enumerate(chips)]
        mine = pltpu.make_async_copy(in_refs[a], block(*me), local_sems.at[l0 + a])
        return first, passed, landed, from_sibling, mine

    def start(*refs):
        for a in range(n):
            first, _, _, _, mine = copies(a, *refs)
            mine.start()
            for cp in first:
                cp.start()

    def pass_on(*refs):
        for a in range(n):
            _, passed, landed, _, _ = copies(a, *refs)
            for cp, fwd in zip(landed, passed):
                cp.wait_recv()
                fwd.start()

    def finish(*refs):
        for a in range(n):
            first, passed, _, from_sibling, mine = copies(a, *refs)
            for cp in from_sibling:
                cp.wait_recv()
            for cp in first + passed:
                cp.wait_send()
            mine.wait()

    return _CommPlan(list(shards), outs, 7 * n, n, [start, pass_on, finish])


def _exchange_plan(srcs, cols, n_slots, route):
    n = len(srcs)
    outs = []
    for g, cl in zip(srcs, cols):
        shard = (g.shape[0], g.shape[1], g.shape[2] // N_DEV) if cl else (g.shape[0],) + g.shape[2:]
        outs.append(SDS((n_slots,) + shard, g.dtype))

    def copies(in_refs, out_refs, send_sems, recv_sems, local_sems, r0=0, l0=0):
        x, y, c = _mesh_pos()
        made = []
        for a in range(n):
            for s in range(n_slots):
                chunk, target = route(x, y, c, s)
                src = _col_window(in_refs[a], chunk, outs[a].shape[3]) if cols[a] else in_refs[a].at[:, chunk]
                sem = r0 + a * n_slots + s
                made.append(pltpu.make_async_remote_copy(
                    src_ref=src, dst_ref=out_refs[a].at[s], send_sem=send_sems.at[sem], recv_sem=recv_sems.at[sem],
                    device_id=target, device_id_type=MESH))
        return made

    def start(*refs):
        for cp in copies(*refs):
            cp.start()

    def nothing(*refs):
        pass

    def finish(*refs):
        made = copies(*refs)
        for cp in made:
            cp.wait_recv()
        for cp in made:
            cp.wait_send()

    return _CommPlan(list(srcs), outs, n * n_slots, 0, [start, nothing, finish])


def _sibling_route(x, y, c, k):
    return 2 * k + 1 - c, (x, y, 1 - c)


def _chip_route(x, y, c, j):
    px, py = [(1 - x, y), (x, 1 - y), (1 - x, 1 - y)][j]
    return 2 * px + py, (px, py, c)


def _pair_sum(grads, recv4, c_idx, name, cols=False):
    _, p, r, cdim = recv4.shape
    tr = _tile(r, (512, 256, 176, 160, 128, 64, 32, 16))

    def body(c_ref, a_ref, b_ref, o_ref):
        o_ref[...] = (a_ref[...].astype(F32) + b_ref[...].astype(F32)).astype(o_ref.dtype)

    blk = (None, None, tr, cdim)
    if cols:
        own = pl.BlockSpec((None, tr, cdim), lambda k, q, i, c_ref: (q, i, 2 * k + c_ref[0]))
    else:
        own = pl.BlockSpec(blk, lambda k, q, i, c_ref: (q, 2 * k + c_ref[0], i, 0))
    grid_spec = pltpu.PrefetchScalarGridSpec(
        num_scalar_prefetch=1, grid=(4, p, r // tr),
        in_specs=[own, pl.BlockSpec(blk, lambda k, q, i, c_ref: (k, q, i, 0))],
        out_specs=pl.BlockSpec(blk, lambda k, q, i, c_ref: (q, k, i, 0)))
    return _pcall(body, name=name, grid_spec=grid_spec, out_shape=SDS((p, 4, r, cdim), grads.dtype),
                  compiler_params=_params(("parallel", "parallel", "parallel")))(c_idx, grads, recv4)


def _mm(name, ins, prods, n_acc, acc_shape, epi_idx, epilogue, out_shapes, out_specs, grid, dims, plan=None):
    n_in, n_out, nk = len(ins), len(out_shapes), grid[2]
    n_acc_refs = n_acc if nk > 1 else 0
    c_ins, c_outs = (plan.ins, plan.outs) if plan else ([], [])
    n_cin, n_cout = len(c_ins), len(c_outs)

    def body(*refs):
        in_refs, c_in = refs[:n_in], refs[n_in:n_in + n_cin]
        rest = refs[n_in + n_cin:]
        out_refs, c_out = rest[:n_out], rest[n_out:n_out + n_cout]
        rest = rest[n_out + n_cout:]
        acc_refs, sems = rest[:n_acc_refs], rest[n_acc_refs:]
        ids = [pl.program_id(axis) for axis in range(3)]
        if plan:
            @pl.when((ids[0] == 0) & (ids[1] == 0) & (ids[2] == 0))
            def _():
                plan.phases[0](c_in, c_out, *sems)

        def finish(accs):
            outs = epilogue(accs, [in_refs[i][...] for i in epi_idx])
            for o_ref, o in zip(out_refs, outs):
                if isinstance(o, tuple):
                    for plane, part in enumerate(o):
                        o_ref[plane] = part.astype(o_ref.dtype)
                else:
                    o_ref[...] = o.astype(o_ref.dtype)

        if nk == 1:
            accs = [None] * n_acc
            for ia, ib, iacc in prods:
                term = _dot(in_refs[ia][...], in_refs[ib][...], dims)
                accs[iacc] = term if accs[iacc] is None else accs[iacc] + term
            finish(accs)
        else:
            @pl.when(ids[2] == 0)
            def _():
                for acc in acc_refs:
                    acc[...] = jnp.zeros_like(acc)

            for ia, ib, iacc in prods:
                acc_refs[iacc][...] += _dot(in_refs[ia][...], in_refs[ib][...], dims)

            @pl.when(ids[2] == nk - 1)
            def _():
                finish([acc[...] for acc in acc_refs])

        if plan:
            @pl.when((ids[0] == grid[0] - 1) & (ids[1] == grid[1] - 1) & (ids[2] == nk - 1))
            def _():
                plan.phases[1](c_in, c_out, *sems)
                plan.phases[2](c_in, c_out, *sems)

    return _pcall(
        body, name=name, grid=grid, in_specs=[s for _, s in ins] + [ANY] * n_cin,
        out_specs=list(out_specs) + [ANY] * n_cout, out_shape=list(out_shapes) + list(c_outs),
        scratch_shapes=[pltpu.VMEM(acc_shape, F32) for _ in range(n_acc_refs)] + (plan.scratch() if plan else []),
        compiler_params=_params(("arbitrary",) * 3 if plan else ("parallel", "parallel", "arbitrary")),
    )(*[a for a, _ in ins], *c_ins)


def _plain(accs, _):
    return accs


def _mm_nn(name, a, b, out_dtype, extras=(), epilogue=_plain, n_out=1):
    m, kd = a.shape
    n = b.shape[1]
    tm, tn, tk = _tile(m, (1024, 512, 256, 128)), _tile(n, (640, 512, 256, 128)), _tile(kd, (1280, 1024, 512, 256, 128))
    ins = [(a, pl.BlockSpec((tm, tk), lambda i, j, k: (i, k))), (b, pl.BlockSpec((tk, tn), lambda i, j, k: (k, j)))]
    for arr, kind in extras:
        if kind == "tile":
            ins.append((arr, pl.BlockSpec((tm, tn), lambda i, j, k: (i, j))))
        else:
            ins.append((arr, pl.BlockSpec((1, tn), lambda i, j, k: (0, j))))
    dts = out_dtype if isinstance(out_dtype, (list, tuple)) else [out_dtype] * n_out
    return _mm(name, ins, [(0, 1, 0)], 1, (tm, tn), list(range(2, len(ins))), epilogue,
               [SDS((m, n), dt) for dt in dts], [pl.BlockSpec((tm, tn), lambda i, j, k: (i, j)) for _ in dts],
               (m // tm, n // tn, kd // tk), "nn")


def _mm_nt(name, a, b, out_dtype):
    m, kd = a.shape
    n = b.shape[0]
    tm, tn, tk = _tile(m, (1024, 512, 256, 128)), _tile(n, (640, 512, 256, 128)), _tile(kd, (1024, 512, 256, 128))
    ins = [(a, pl.BlockSpec((tm, tk), lambda i, j, k: (i, k))), (b, pl.BlockSpec((tn, tk), lambda i, j, k: (j, k)))]
    return _mm(name, ins, [(0, 1, 0)], 1, (tm, tn), [], _plain, [SDS((m, n), out_dtype)],
               [pl.BlockSpec((tm, tn), lambda i, j, k: (i, j))], (m // tm, n // tn, kd // tk), "nt")[0]


def _mm_tn(name, a, b, out_dtype):
    t, m = a.shape
    n = b.shape[1]
    tm, tn, tk = _tile(m, (640, 512, 256, 128)), _tile(n, (1024, 512, 256, 128)), t
    ins = [(a, pl.BlockSpec((tk, tm), lambda i, j, k: (k, i))), (b, pl.BlockSpec((tk, tn), lambda i, j, k: (k, j)))]
    return _mm(name, ins, [(0, 1, 0)], 1, (tm, tn), [], _plain, [SDS((m, n), out_dtype)],
               [pl.BlockSpec((tm, tn), lambda i, j, k: (i, j))], (m // tm, n // tn, t // tk), "tn")[0]


def _mm_nt_stack(name, a3, b, out_dtype):
    cc, m, kd = a3.shape
    n = b.shape[0]
    tm, tn, tk = _tile(m, (1024, 512, 256, 128)), _tile(n, (1024, 512, 256, 128)), _tile(kd, (1280, 1024, 512, 256, 128))
    nk = kd // tk
    ins = [(a3, pl.BlockSpec((None, tm, tk), lambda i, j, k: (k // nk, i, k % nk))),
           (b, pl.BlockSpec((tn, tk), lambda i, j, k: (j, k)))]
    return _mm(name, ins, [(0, 1, 0)], 1, (tm, tn), [], _plain, [SDS((m, n), out_dtype)],
               [pl.BlockSpec((tm, tn), lambda i, j, k: (i, j))], (m // tm, n // tn, cc * nk), "nt")[0]


def _mm_tn_stack(name, a, b3, out_dtype):
    t, m = a.shape
    cc, _, n = b3.shape
    tm, tn, tk = _tile(m, (512, 256, 128)), _tile(n, (1280, 1024, 512, 256, 128)), t
    nj = n // tn
    ins = [(a, pl.BlockSpec((tk, tm), lambda i, j, k: (k, i))),
           (b3, pl.BlockSpec((None, tk, tn), lambda i, j, k: (j // nj, k, j % nj)))]
    return _mm(name, ins, [(0, 1, 0)], 1, (tm, tn), [], _plain, [SDS((m, cc * n), out_dtype)],
               [pl.BlockSpec((tm, tn), lambda i, j, k: (i, j))], (m // tm, cc * nj, t // tk), "tn")[0]


def _chunks_to_cols(name, wc, eye2):
    nch, d, cw = wc.shape
    tm = _tile(d, (1024, 512, 256, 128))
    ins = [(wc, pl.BlockSpec((None, tm, cw), lambda i, j, k: (2 * j + k, i, 0))),
           (eye2, pl.BlockSpec((None, cw, 2 * cw), lambda i, j, k: (k, 0, 0)))]
    return _mm(name, ins, [(0, 1, 0)], 1, (tm, 2 * cw), [], _plain, [SDS((d, nch * cw), wc.dtype)],
               [pl.BlockSpec((tm, 2 * cw), lambda i, j, k: (i, j))], (d // tm, nch // 2, 2), "nn")[0]


def _cols_to_chunks(name, full, eye2):
    d, n = full.shape
    _, cw, _ = eye2.shape
    nch = n // cw
    tm = _tile(d, (1024, 512, 256, 128))
    ins = [(full, pl.BlockSpec((tm, 2 * cw), lambda i, j, k: (i, j // 2))),
           (eye2, pl.BlockSpec((None, cw, 2 * cw), lambda i, j, k: (j % 2, 0, 0)))]
    return _mm(name, ins, [(0, 1, 0)], 1, (tm, cw), [], _plain, [SDS((nch, d, cw), full.dtype)],
               [pl.BlockSpec((None, tm, cw), lambda i, j, k: (j, i, 0))], (d // tm, nch, 1), "nt")[0]


def _row_tile(t):
    return _tile(t, (256, 128, 64, 32, 16, 8))


def _norm_fwd(x, g, scale1p, shift, name):
    t, d = x.shape
    tr = _row_tile(t)

    def body(x_ref, g_ref, s_ref, b_ref, h_ref):
        xv = x_ref[...]
        inv = lax.rsqrt(jnp.mean(xv * xv, axis=-1, keepdims=True) + NORM_EPS)
        h_ref[...] = ((xv * inv) * g_ref[...] * s_ref[...] + b_ref[...]).astype(h_ref.dtype)

    vec = pl.BlockSpec((1, d), lambda i: (0, 0))
    return _pcall(body, name=name, grid=(t // tr,), in_specs=[pl.BlockSpec((tr, d), lambda i: (i, 0)), vec, vec, vec],
                  out_specs=pl.BlockSpec((tr, d), lambda i: (i, 0)), out_shape=SDS((t, d), BF16),
                  compiler_params=_params(("parallel",)))(x, g, scale1p, shift)


def _adaln_bwd(dh, x, y, dxo, g, scale1p, w_sub, gw_prev, name):
    t, d = x.shape
    tr = _row_tile(t)

    def body(dh_ref, x_ref, y_ref, dxo_ref, g_ref, s_ref, gw_ref, dx_ref, dyp_ref, sums_ref):
        i = pl.program_id(0)

        @pl.when(i == 0)
        def _():
            sums_ref[...] = jnp.zeros_like(sums_ref)

        xv, dhv, dxov = x_ref[...], dh_ref[...], dxo_ref[...]
        inv = lax.rsqrt(jnp.mean(xv * xv, axis=-1, keepdims=True) + NORM_EPS)
        xn = xv * inv
        gv = g_ref[...]
        dn = dhv * s_ref[...]
        dxn = dn * gv
        dx = inv * (dxn - xn * jnp.mean(dxn * xn, axis=-1, keepdims=True)) + dxov
        dx_ref[...] = dx
        dyp_ref[...] = (gw_ref[...] * dx).astype(dyp_ref.dtype)
        sums_ref[0:1, :] += jnp.sum(dhv, axis=0, keepdims=True)
        sums_ref[1:2, :] += jnp.sum(dhv * (xn * gv), axis=0, keepdims=True)
        sums_ref[2:3, :] += jnp.sum(w_sub * y_ref[...] * dxov, axis=0, keepdims=True)
        sums_ref[3:4, :] += jnp.sum(dn * xn, axis=0, keepdims=True)

    blk = pl.BlockSpec((tr, d), lambda i: (i, 0))
    vec = pl.BlockSpec((1, d), lambda i: (0, 0))
    return _pcall(
        body, name=name, grid=(t // tr,), in_specs=[blk, blk, blk, blk, vec, vec, vec],
        out_specs=[blk, blk, pl.BlockSpec((8, d), lambda i: (0, 0))],
        out_shape=[SDS((t, d), F32), SDS((t, d), BF16), SDS((8, d), F32)],
        compiler_params=_params(("arbitrary",)))(dh, x, y, dxo, g, scale1p, gw_prev)


def _loss_head(x, target, gf, gw_prev):
    t, d = x.shape
    tr = _row_tile(t)
    nt = t // tr

    def body(x_ref, tg_ref, g_ref, gw_ref, dx_ref, dyp_ref, sums_ref):
        i = pl.program_id(0)

        @pl.when(i == 0)
        def _():
            sums_ref[...] = jnp.zeros_like(sums_ref)

        xv = x_ref[...]
        inv = lax.rsqrt(jnp.mean(xv * xv, axis=-1, keepdims=True) + NORM_EPS)
        xn = xv * inv
        gv = g_ref[...]
        err = xn * gv - tg_ref[...]
        dyv = err * (1.0 / d)
        dxn = dyv * gv
        dx = inv * (dxn - xn * jnp.mean(dxn * xn, axis=-1, keepdims=True))
        dx_ref[...] = dx
        dyp_ref[...] = (gw_ref[...] * dx).astype(dyp_ref.dtype)
        sums_ref[0:1, :] += jnp.sum(dyv * xn, axis=0, keepdims=True)
        sums_ref[1:2, :] += jnp.sum(err * err, axis=0, keepdims=True)

        @pl.when(i == nt - 1)
        def _():
            tot = jnp.sum(sums_ref[1:2, :], axis=1, keepdims=True) * (0.5 / d)
            sums_ref[1:2, :] = jnp.broadcast_to(tot, (1, d))

    blk = pl.BlockSpec((tr, d), lambda i: (i, 0))
    vec = pl.BlockSpec((1, d), lambda i: (0, 0))
    return _pcall(
        body, name="loss_head", grid=(nt,), in_specs=[blk, blk, vec, vec],
        out_specs=[blk, blk, pl.BlockSpec((8, d), lambda i: (0, 0))],
        out_shape=[SDS((t, d), F32), SDS((t, d), BF16), SDS((8, d), F32)],
        compiler_params=_params(("arbitrary",)))(x, target, gf, gw_prev)


HIDDEN_CHUNKS = N_DEV // 2


def _ffn_tiles(t, d):
    return _tile(t, (1024, 512, 256, 128)), _tile(d, (1024, 512, 256, 128))


def _ffn_gu(name, h, wgu, plan=None):
    t, d = h.shape
    fc, nc = wgu.shape[3], HIDDEN_CHUNKS
    tm, _ = _ffn_tiles(t, d)

    def epi_gu(accs, _):
        gpre, up = accs
        return (gpre, up), gpre * jax.nn.sigmoid(gpre) * up

    wblk = (None, None, d, fc)
    ins = [(h, pl.BlockSpec((tm, d), lambda i, c, k: (i, 0))),
           (wgu, pl.BlockSpec(wblk, lambda i, c, k: (0, c, 0, 0))),
           (wgu, pl.BlockSpec(wblk, lambda i, c, k: (0, c + nc, 0, 0)))]
    return _mm(name, ins, [(0, 1, 0), (0, 2, 1)], 2, (tm, fc), [], epi_gu,
               [SDS((2, nc, t, fc), BF16), SDS((nc, t, fc), BF16)],
               [pl.BlockSpec((2, None, tm, fc), lambda i, c, k: (0, c, i, 0)),
                pl.BlockSpec((None, tm, fc), lambda i, c, k: (c, i, 0))],
               (t // tm, nc, 1), "nn", plan=plan)


def _ffn_down(name, a, wd4, x, gmul, plan=None):
    nc, t, fc = a.shape
    d = wd4.shape[3]
    tm, tn = _ffn_tiles(t, d)

    def epi_down(accs, ex):
        (yv,), (xv, gm) = accs, ex
        return yv, xv + MACARON_W * gm * yv

    ins = [(a, pl.BlockSpec((None, tm, fc), lambda i, j, k: (k, i, 0))),
           (wd4, pl.BlockSpec((None, None, fc, tn), lambda i, j, k: (0, k, 0, j))),
           (x, pl.BlockSpec((tm, tn), lambda i, j, k: (i, j))), (gmul, pl.BlockSpec((1, tn), lambda i, j, k: (0, j)))]
    oblk = pl.BlockSpec((tm, tn), lambda i, j, k: (i, j))
    return _mm(name, ins, [(0, 1, 0)], 1, (tm, tn), [2, 3], epi_down, [SDS((t, d), BF16), SDS((t, d), F32)],
               [oblk, oblk], (t // tm, d // tn, nc), "nn", plan=plan)


def _ffn_da(name, dy, wd4, gu2, plan=None):
    t, d = dy.shape
    _, nc, fc, _ = wd4.shape
    tm, _ = _ffn_tiles(t, d)

    def epi_da(accs, ex):
        (da,), (gu,) = accs, ex
        gpre, up = gu[0].astype(F32), gu[1].astype(F32)
        s = jax.nn.sigmoid(gpre)
        silu = gpre * s
        dg = da * up * (s * (1.0 + gpre * (1.0 - s)))
        return ((dg, da * silu),)

    gblk = pl.BlockSpec((2, None, tm, fc), lambda i, c, k: (0, c, i, 0))
    ins = [(dy, pl.BlockSpec((tm, d), lambda i, c, k: (i, 0))),
           (wd4, pl.BlockSpec((None, None, fc, d), lambda i, c, k: (0, c, 0, 0))), (gu2, gblk)]
    return _mm(name, ins, [(0, 1, 0)], 1, (tm, fc), [2], epi_da, [SDS((2, nc, t, fc), BF16)], [gblk],
               (t // tm, nc, 1), "nt", plan=plan)


def _ffn_dwd(name, a, dy, plan=None):
    nc, t, fc = a.shape
    d = dy.shape[1]
    _, tn = _ffn_tiles(t, d)
    ins = [(a, pl.BlockSpec((None, t, fc), lambda c, j, k: (c, 0, 0))), (dy, pl.BlockSpec((t, tn), lambda c, j, k: (0, j)))]
    return _mm(name, ins, [(0, 1, 0)], 1, (fc, tn), [], _plain, [SDS((1, nc, fc, d), BF16)],
               [pl.BlockSpec((None, None, fc, tn), lambda c, j, k: (0, c, 0, j))], (nc, d // tn, 1), "tn", plan=plan)


def _ffn_dwgu(name, h, dgu2, plan=None):
    t, d = h.shape
    _, nc, _, fc = dgu2.shape
    _, tn = _ffn_tiles(t, d)
    ins = [(h, pl.BlockSpec((t, tn), lambda i, c, k: (0, i))),
           (dgu2, pl.BlockSpec((None, None, t, fc), lambda i, c, k: (c // nc, c % nc, 0, 0)))]
    return _mm(name, ins, [(0, 1, 0)], 1, (tn, fc), [], _plain, [SDS((1, 2 * nc, d, fc), BF16)],
               [pl.BlockSpec((None, None, tn, fc), lambda i, c, k: (0, c, i, 0))], (d // tn, 2 * nc, 1), "tn", plan=plan)


def _ffn_dh(name, dgu2, wgu, plan=None):
    _, nc, t, fc = dgu2.shape
    d = wgu.shape[2]
    tm, tn = _ffn_tiles(t, d)
    ins = [(dgu2, pl.BlockSpec((None, None, tm, fc), lambda i, j, k: (k // nc, k % nc, i, 0))),
           (wgu, pl.BlockSpec((None, None, tn, fc), lambda i, j, k: (0, k, j, 0)))]
    return _mm(name, ins, [(0, 1, 0)], 1, (tm, tn), [], _plain, [SDS((t, d), F32)],
               [pl.BlockSpec((tm, tn), lambda i, j, k: (i, j))], (t // tm, d // tn, 2 * nc), "nt", plan=plan)


def _sb_block(t):
    return 256 if t >= 1024 else 128


def _split_hi_lo(v):
    hi = v.astype(BF16)
    return hi, (v - hi.astype(F32)).astype(BF16)


def _host_call(core, name, steps, ins, in_specs, out_shapes, out_specs, scratch, plan):
    n_in, n_out, n_scr = len(ins), len(out_shapes), len(scratch)
    c_ins, c_outs = (plan.ins, plan.outs) if plan else ([], [])
    n_cin, n_cout = len(c_ins), len(c_outs)

    def body(*refs):
        in_refs, c_in = refs[:n_in], refs[n_in:n_in + n_cin]
        rest = refs[n_in + n_cin:]
        out_refs, c_out = rest[:n_out], rest[n_out:n_out + n_cout]
        rest = rest[n_out + n_cout:]
        scr, sems = rest[:n_scr], rest[n_scr:]
        step = pl.program_id(0)
        if plan:
            @pl.when(step == 0)
            def _():
                plan.phases[0](c_in, c_out, *sems)

        core(in_refs, out_refs, scr)
        if plan:
            @pl.when(step == steps - 1)
            def _():
                plan.phases[1](c_in, c_out, *sems)
                plan.phases[2](c_in, c_out, *sems)

    return _pcall(
        body, name=name, grid=(steps,), in_specs=list(in_specs) + [ANY] * n_cin,
        out_specs=list(out_specs) + [ANY] * n_cout, out_shape=list(out_shapes) + list(c_outs),
        scratch_shapes=list(scratch) + (plan.scratch() if plan else []),
        compiler_params=_params(("arbitrary",)))(*ins, *c_ins)


def _sb_fwd(qkv, d, plan=None):
    t = qkv.shape[0]
    blk = _sb_block(t)
    nq = t // blk
    npair = d // LANES
    scale = HEAD_DIM ** -0.5

    def body(in_refs, out_refs, _):
        (q_ref, k_ref, v_ref), (o_ref, l_ref) = in_refs, out_refs
        lane = lax.broadcasted_iota(jnp.int32, (blk, LANES), 1)
        head0 = lane < HEAD_DIM
        row = lax.broadcasted_iota(jnp.int32, (blk, blk), 0)
        col = lax.broadcasted_iota(jnp.int32, (blk, blk), 1)
        causal = col < row
        after = (row > col).astype(BF16)

        def tile(qh, kb, carry, masked):
            cl, oacc = carry
            start = pl.multiple_of(kb * blk, blk)
            kv = k_ref[pl.ds(start, blk), :]
            vv = v_ref[pl.ds(start, blk), :]
            z = _dot(qh, kv, "nt") * scale
            sp = _softplus(z)
            lk = jnp.where(causal, -sp, 0.0) if masked else -sp
            hi, lo = _split_hi_lo(lk)
            later = _dot(hi, after, "nn") + _dot(lo, after, "nn") + cl
            logw = z - sp + later
            if masked:
                logw = jnp.where(causal, logw, -1e30)
            w = jnp.exp(logw)
            oacc = oacc + _dot(w.astype(BF16), vv, "nn")
            return cl + jnp.sum(lk, axis=1, keepdims=True), oacc

        def qblock(qi, _):
            qstart = pl.multiple_of(qi * blk, blk)
            qv = q_ref[pl.ds(qstart, blk), :]
            qhs = [jnp.where(head0 if hh == 0 else ~head0, qv, jnp.zeros_like(qv)) for hh in range(2)]
            zero = (jnp.zeros((blk, 1), F32), jnp.zeros((blk, LANES), F32))

            def both(kb, carries, masked):
                return tuple(tile(qh, kb, cr, masked) for qh, cr in zip(qhs, carries))

            outs = both(qi, (zero, zero), True)
            outs = lax.fori_loop(0, qi, lambda j, crs: both(qi - 1 - j, crs, False), outs)
            o_ref[pl.ds(qstart, blk), :] = jnp.where(head0, outs[0][1], outs[1][1]).astype(o_ref.dtype)
            l_ref[pl.ds(qstart, blk), :] = jnp.where(head0, outs[0][0], outs[1][0])
            return 0

        lax.fori_loop(0, nq, qblock, 0)

    return _host_call(
        body, "sb_fwd", npair, [qkv, qkv, qkv],
        [pl.BlockSpec((t, LANES), lambda p: (0, p)), pl.BlockSpec((t, LANES), lambda p: (0, npair + p)),
         pl.BlockSpec((t, LANES), lambda p: (0, 2 * npair + p))],
        [SDS((t, d), BF16), SDS((t, d), F32)],
        [pl.BlockSpec((t, LANES), lambda p: (0, p)), pl.BlockSpec((t, LANES), lambda p: (0, p))], [], plan)


def _sb_bwd(qkv, do, ltot, d, plan=None):
    t = qkv.shape[0]
    blk = _sb_block(t)
    nq = t // blk
    npair = d // LANES
    scale = HEAD_DIM ** -0.5

    def body(in_refs, out_refs, scr):
        (q_ref, k_ref, v_ref, do_ref, l_ref), (out_ref,), (dq_s, dk_s, dv_s) = in_refs, out_refs, scr
        lane = lax.broadcasted_iota(jnp.int32, (blk, LANES), 1)
        head0 = lane < HEAD_DIM
        row = lax.broadcasted_iota(jnp.int32, (blk, blk), 0)
        col = lax.broadcasted_iota(jnp.int32, (blk, blk), 1)
        causal = col < row
        upto = (row <= col).astype(BF16)
        before = (row < col).astype(BF16)
        dk_s[...] = jnp.zeros_like(dk_s)
        dv_s[...] = jnp.zeros_like(dv_s)

        def tile(qh, doh, lt, kb, carry, masked):
            plk, pda, dqacc = carry
            start = pl.multiple_of(kb * blk, blk)
            kv = k_ref[pl.ds(start, blk), :]
            vv = v_ref[pl.ds(start, blk), :]
            z = _dot(qh, kv, "nt") * scale
            sp = _softplus(z)
            lk = jnp.where(causal, -sp, 0.0) if masked else -sp
            hi, lo = _split_hi_lo(lk)
            later = lt - (plk + _dot(hi, upto, "nn") + _dot(lo, upto, "nn"))
            logw = z - sp + later
            if masked:
                logw = jnp.where(causal, logw, -1e30)
            w = jnp.exp(logw)
            da = _dot(doh, vv, "nt") * w
            pex = pda + _dot(da.astype(BF16), before, "nn")
            sig = jnp.exp(z - sp)
            dz = da * (1.0 - sig) - sig * pex
            if masked:
                dz = jnp.where(causal, dz, 0.0)
            dzs = (dz * scale).astype(BF16)
            dqacc = dqacc + _dot(dzs, kv, "nn")
            dk_s[pl.ds(start, blk), :] += _dot(dzs, qh, "tn")
            dv_s[pl.ds(start, blk), :] += _dot(w.astype(BF16), doh, "tn")
            return plk + jnp.sum(lk, axis=1, keepdims=True), pda + jnp.sum(da, axis=1, keepdims=True), dqacc

        def qblock(qi, _):
            qstart = pl.multiple_of(qi * blk, blk)
            qv = q_ref[pl.ds(qstart, blk), :]
            dov = do_ref[pl.ds(qstart, blk), :]
            lv = l_ref[pl.ds(qstart, blk), :]
            heads = []
            for hh in range(2):
                sel = head0 if hh == 0 else ~head0
                heads.append((jnp.where(sel, qv, jnp.zeros_like(qv)), jnp.where(sel, dov, jnp.zeros_like(dov)),
                              jnp.max(jnp.where(sel, lv, -jnp.inf), axis=1, keepdims=True)))
            zero = (jnp.zeros((blk, 1), F32), jnp.zeros((blk, 1), F32), jnp.zeros((blk, LANES), F32))

            def both(kb, carries, masked):
                return tuple(tile(qh, doh, lt, kb, cr, masked) for (qh, doh, lt), cr in zip(heads, carries))

            carries = lax.fori_loop(0, qi, lambda kb, crs: both(kb, crs, False), (zero, zero))
            carries = both(qi, carries, True)
            dq_s[pl.ds(qstart, blk), :] = jnp.where(head0, carries[0][2], carries[1][2])
            return 0

        lax.fori_loop(0, nq, qblock, 0)
        out_ref[0] = dq_s[...].astype(out_ref.dtype)
        out_ref[1] = dk_s[...].astype(out_ref.dtype)
        out_ref[2] = dv_s[...].astype(out_ref.dtype)

    col_blk = lambda off: pl.BlockSpec((t, LANES), lambda p: (0, off + p))
    return _host_call(
        body, "sb_bwd", npair, [qkv, qkv, qkv, do, ltot],
        [col_blk(0), col_blk(npair), col_blk(2 * npair), col_blk(0), col_blk(0)],
        [SDS((3, t, d), BF16)], [pl.BlockSpec((3, t, LANES), lambda p: (0, 0, p))],
        [pltpu.VMEM((t, LANES), F32) for _ in range(3)], plan)


def _roll_rows(v, shift):
    return pltpu.roll(v, shift, 0)


def _shift_down(v, dist, fill, row):
    return jnp.where(row >= dist, _roll_rows(v, dist), fill)


def _shift_up(v, dist, fill, row):
    t = v.shape[0]
    return jnp.where(row < t - dist, _roll_rows(v, t - dist), fill)


def _lru_gates(xb, small, wr, wi, row):
    xs = [_shift_down(xb, 3 - tap, 0.0, row) if tap < 3 else xb for tap in range(4)]
    xc = small[4:5, :] + xs[0] * small[0:1, :]
    for tap in range(1, 4):
        xc = xc + xs[tap] * small[tap:tap + 1, :]
    xcb = xc.astype(BF16)
    r = jax.nn.sigmoid(_dot(xcb, wr, "nn") + small[5:6, :])
    ig = jax.nn.sigmoid(_dot(xcb, wi, "nn") + small[6:7, :])
    sp = _softplus(-small[7:8, :])
    la = -LRU_C * r * sp
    a = jnp.exp(la)
    th = jnp.tanh(la)
    mult = jnp.sqrt(-2.0 * th / (1.0 - th))
    return xs, xc, xcb, r, ig, sp, a, mult


def _gelu_parts(gate):
    inner = GELU_C * (gate + GELU_K * gate * gate * gate)
    th = jnp.tanh(inner)
    gelu = 0.5 * gate * (1.0 + th)
    dgelu = 0.5 * (1.0 + th) + 0.5 * gate * (1.0 - th * th) * GELU_C * (1.0 + 3.0 * GELU_K * gate * gate)
    return gelu, dgelu


def _scan_steps(t):
    steps, dist = [], 1
    while dist < t:
        steps.append(dist)
        dist *= 2
    return steps


def _lru_fwd(gx, small, wr, wi):
    t = gx.shape[0]
    r_dim = gx.shape[1] // 2
    nb = r_dim // LRU_BLOCK_W

    def body(gate_ref, xb_ref, small_ref, wr_ref, wi_ref, y_ref, hs_ref):
        row = lax.broadcasted_iota(jnp.int32, (t, LRU_BLOCK_W), 0)
        xb = xb_ref[...]
        _, xc, _, _, ig, _, a, mult = _lru_gates(xb, small_ref, wr_ref[...], wi_ref[...], row)
        b = mult * (ig * xc)
        for dist in _scan_steps(t):
            b = a * _shift_down(b, dist, 0.0, row) + b
            a = a * _shift_down(a, dist, 1.0, row)
        hs_ref[...] = b
        gelu, _ = _gelu_parts(gate_ref[...])
        y_ref[...] = (gelu * b).astype(y_ref.dtype)

    colb = lambda off: pl.BlockSpec((t, LRU_BLOCK_W), lambda n: (0, off + n))
    wspec = pl.BlockSpec((None, LRU_BLOCK_W, LRU_BLOCK_W), lambda n: (n, 0, 0))
    return _pcall(
        body, name="lru_fwd", grid=(nb,),
        in_specs=[colb(0), colb(nb), pl.BlockSpec((8, LRU_BLOCK_W), lambda n: (0, n)), wspec, wspec],
        out_specs=[colb(0), colb(0)], out_shape=[SDS((t, r_dim), BF16), SDS((t, r_dim), F32)],
        compiler_params=_params(("parallel",)))(gx, gx, small, wr, wi)


def _lru_bwd(gx, hs, dy, small, wr, wi):
    t = gx.shape[0]
    r_dim = gx.shape[1] // 2
    nb = r_dim // LRU_BLOCK_W

    def body(gate_ref, xb_ref, hs_ref, dy_ref, small_ref, wr_ref, wi_ref, dgx_ref, dsm_ref, dwr_ref, dwi_ref):
        row = lax.broadcasted_iota(jnp.int32, (t, LRU_BLOCK_W), 0)
        xb, hsv, dyv, smallv = xb_ref[...], hs_ref[...], dy_ref[...], small_ref
        wrv, wiv = wr_ref[...], wi_ref[...]
        xs, xc, xcb, r, ig, sp, a, mult = _lru_gates(xb, smallv, wrv, wiv, row)
        gelu, dgelu = _gelu_parts(gate_ref[...])
        dgx_ref[0] = (dyv * hsv * dgelu).astype(dgx_ref.dtype)
        dacc = dyv * gelu
        an = _shift_up(a, 1, 1.0, row)
        for dist in _scan_steps(t):
            dacc = dacc + an * _shift_up(dacc, dist, 0.0, row)
            an = an * _shift_up(an, dist, 1.0, row)
        da = dacc * _shift_down(hsv, 1, 0.0, row)
        dmult = dacc * (ig * xc)
        dixc = dacc * mult
        dla = da * a - dmult * (a * a) / mult
        dr = dla * (-LRU_C * sp)
        dsp = jnp.sum(dla * (-LRU_C * r), axis=0, keepdims=True)
        dpr = dr * r * (1.0 - r)
        dpi = dixc * xc * ig * (1.0 - ig)
        dprb, dpib = dpr.astype(BF16), dpi.astype(BF16)
        dwr_ref[...] = _dot(xcb, dprb, "tn")
        dwi_ref[...] = _dot(xcb, dpib, "tn")
        dxc = dixc * ig + _dot(dprb, wrv, "nt") + _dot(dpib, wiv, "nt")
        dxb = dxc * smallv[3:4, :]
        for tap in range(3):
            dxb = dxb + _shift_up(dxc, 3 - tap, 0.0, row) * smallv[tap:tap + 1, :]
        dgx_ref[1] = dxb.astype(dgx_ref.dtype)
        lam = smallv[7:8, :]
        rows = [jnp.sum(dxc * xs[tap], axis=0, keepdims=True) for tap in range(4)]
        rows.append(jnp.sum(dxc, axis=0, keepdims=True))
        rows.append(jnp.sum(dpr, axis=0, keepdims=True))
        rows.append(jnp.sum(dpi, axis=0, keepdims=True))
        rows.append(-dsp * jax.nn.sigmoid(-lam))
        for k, rv in enumerate(rows):
            dsm_ref[k:k + 1, :] = rv

    colb = lambda off: pl.BlockSpec((t, LRU_BLOCK_W), lambda n: (0, off + n))
    wspec = pl.BlockSpec((None, LRU_BLOCK_W, LRU_BLOCK_W), lambda n: (n, 0, 0))
    sspec = pl.BlockSpec((8, LRU_BLOCK_W), lambda n: (0, n))
    return _pcall(
        body, name="lru_bwd", grid=(nb,),
        in_specs=[colb(0), colb(nb), colb(0), colb(0), sspec, wspec, wspec],
        out_specs=[pl.BlockSpec((2, t, LRU_BLOCK_W), lambda n: (0, 0, n)), sspec, wspec, wspec],
        out_shape=[SDS((2, t, r_dim), BF16), SDS((8, r_dim), F32), SDS((nb, LRU_BLOCK_W, LRU_BLOCK_W), F32),
                   SDS((nb, LRU_BLOCK_W, LRU_BLOCK_W), F32)],
        compiler_params=_params(("parallel",)))(gx, gx, hs, dy, small, wr, wi)


def _adam(w, g, m, v):
    m2 = ADAM_B1 * m + (1.0 - ADAM_B1) * g
    v2 = ADAM_B2 * v + (1.0 - ADAM_B2) * (g * g)
    m_hat = m2 / (1.0 - ADAM_B1 ** ADAM_STEP)
    v_hat = v2 / (1.0 - ADAM_B2 ** ADAM_STEP)
    return -ADAM_LR * (m_hat / (jnp.sqrt(v_hat) + ADAM_EPS) + ADAM_WD * w), m2, v2


def _mod_fwd(c_all, mod_w, mod_b_cols):
    nl, d, cols = mod_w.shape
    nbatch = c_all.shape[0]

    def body(c_ref, w_ref, b_ref, o_ref):
        cv = c_ref[...]
        ca = (cv * jax.nn.sigmoid(cv)).astype(BF16)
        o_ref[...] = _dot(ca, w_ref[...].astype(BF16), "nn") + b_ref[...]

    return _pcall(
        body, name="mod_fwd", grid=(nl,),
        in_specs=[pl.BlockSpec((nbatch, d), lambda l: (0, 0)), pl.BlockSpec((None, d, cols), lambda l: (l, 0, 0)),
                  pl.BlockSpec((None, 1, cols), lambda l: (l, 0, 0))],
        out_specs=pl.BlockSpec((None, nbatch, cols), lambda l: (l, 0, 0)), out_shape=SDS((nl, nbatch, cols), F32),
        compiler_params=_params(("parallel",)))(c_all, mod_w, mod_b_cols)


def _mod_w_update(c_all, dmod_cols, w, m, v):
    nl, d, cols = w.shape
    nbatch = c_all.shape[0]
    tr = _tile(d, (256, 128))

    def body(c_ref, dm_ref, w_ref, m_ref, v_ref, g_ref, dl_ref, m2_ref, v2_ref):
        cv = c_ref[...]
        ca = (cv * jax.nn.sigmoid(cv)).astype(BF16)
        g = _dot(ca, dm_ref[...].astype(BF16), "tn")
        g_ref[...] = g
        dl_ref[...], m2_ref[...], v2_ref[...] = _adam(w_ref[...], g, m_ref[...], v_ref[...])

    wblk = pl.BlockSpec((None, tr, cols), lambda l, i: (l, i, 0))
    return _pcall(
        body, name="mod_w_update", grid=(nl, d // tr),
        in_specs=[pl.BlockSpec((nbatch, tr), lambda l, i: (0, i)), pl.BlockSpec((None, nbatch, cols), lambda l, i: (l, 0, 0)),
                  wblk, wblk, wblk],
        out_specs=[wblk] * 4, out_shape=[SDS(w.shape, F32)] * 4,
        compiler_params=_params(("parallel", "parallel")))(c_all, dmod_cols, w, m, v)


def _adam_update(name, w, m, v, gparts):
    rows, cols = w.shape
    tr = _tile(rows, (256, 128, 64, 32, 16, 8))
    npart = len(gparts)

    def body(*refs):
        w_ref, m_ref, v_ref = refs[:3]
        g_refs = refs[3:3 + npart]
        g_ref, dl_ref, m2_ref, v2_ref = refs[3 + npart:]
        g = g_refs[0][...].astype(F32)
        for gr in g_refs[1:]:
            g = g + gr[...].astype(F32)
        g_ref[...] = g
        dl_ref[...], m2_ref[...], v2_ref[...] = _adam(w_ref[...], g, m_ref[...], v_ref[...])

    blk = pl.BlockSpec((tr, cols), lambda i: (i, 0))
    return _pcall(body, name=name, grid=(rows // tr,), in_specs=[blk] * (3 + npart), out_specs=[blk] * 4,
                  out_shape=[SDS((rows, cols), F32)] * 4, compiler_params=_params(("parallel",)))(w, m, v, *gparts)


def _adam_shard(name, w, m, v, part4, recv3, chip_idx, first=0, fills=None):
    p, r, cdim = w.shape
    pg = part4.shape[0]
    tr = _tile(r, (256, 176, 160, 128, 64, 32, 16))

    def body(chip_ref, w_ref, m_ref, v_ref, own_ref, r0_ref, r1_ref, r2_ref, *rest):
        g_ref, dl_ref, m2_ref, v2_ref = rest[-4:]
        g = own_ref[...].astype(F32) + r0_ref[...].astype(F32) + r1_ref[...].astype(F32) + r2_ref[...].astype(F32)
        g_ref[...] = g
        dl_ref[...], m2_ref[...], v2_ref[...] = _adam(w_ref[...], g, m_ref[...], v_ref[...])

    blk = pl.BlockSpec((None, tr, cdim), lambda q, i, chip_ref: (first + q, i, 0))
    blk4 = (None, None, tr, cdim)
    slot = lambda s: pl.BlockSpec(blk4, lambda q, i, chip_ref: (s, q, i, 0))
    fills = list(fills or [])
    grid_spec = pltpu.PrefetchScalarGridSpec(
        num_scalar_prefetch=1, grid=(pg, r // tr),
        in_specs=[blk, blk, blk, pl.BlockSpec(blk4, lambda q, i, chip_ref: (q, chip_ref[0], i, 0)), slot(0), slot(1), slot(2)]
        + [ANY] * len(fills),
        out_specs=[blk] * 4)
    return _pcall(body, name=name, grid_spec=grid_spec, out_shape=[SDS((p, r, cdim), F32)] * 4,
                  input_output_aliases={8 + k: k for k in range(len(fills))},
                  compiler_params=_params(("parallel", "parallel")))(chip_idx, w, m, v, part4, recv3, recv3, recv3, *fills)


def _sum_devices(gathered, name):
    _, rows, cols = gathered.shape
    tr = _tile(rows, (512, 256, 128, 64, 32, 16, 8))

    def body(g_ref, o_ref):
        acc = g_ref[0].astype(F32)
        for k in range(1, N_DEV):
            acc = acc + g_ref[k].astype(F32)
        o_ref[...] = acc

    return _pcall(body, name=name, grid=(rows // tr,), in_specs=[pl.BlockSpec((N_DEV, tr, cols), lambda i: (0, i, 0))],
                  out_specs=pl.BlockSpec((tr, cols), lambda i: (i, 0)), out_shape=SDS((rows, cols), F32),
                  compiler_params=_params(("parallel",)))(gathered)


def _pack_flat(parts, width, row_mult, dtype):
    flat = jnp.concatenate([p.reshape(-1).astype(dtype) for p in parts])
    unit = width * row_mult
    pad = (-flat.shape[0]) % unit
    if pad:
        flat = jnp.concatenate([flat, jnp.zeros((pad,), dtype)])
    return flat.reshape(-1, width)


def _unpack_flat(flat, shapes):
    out, off = [], 0
    for shp in shapes:
        size = math.prod(shp)
        out.append(flat[off:off + size].reshape(shp))
        off += size
    return out


def kernel(x, c, mod_w, mod_b, norm_g, ffn_w_gu, ffn_w_down, sb_w_qkv, sb_w_o, lru_w_in, lru_conv_w, lru_conv_b, lru_w_r, lru_b_r, lru_w_i, lru_b_i, lru_lambda, lru_w_out, final_norm_g, loss_target, m_mod_w, m_mod_b, m_norm_g, m_ffn_w_gu, m_ffn_w_down, m_sb_w_qkv, m_sb_w_o, m_lru_w_in, m_lru_conv_w, m_lru_conv_b, m_lru_w_r, m_lru_b_r, m_lru_w_i, m_lru_b_i, m_lru_lambda, m_lru_w_out, m_final_norm_g, v_mod_w, v_mod_b, v_norm_g, v_ffn_w_gu, v_ffn_w_down, v_sb_w_qkv, v_sb_w_o, v_lru_w_in, v_lru_conv_w, v_lru_conv_b, v_lru_w_r, v_lru_b_r, v_lru_w_i, v_lru_b_i, v_lru_lambda, v_lru_w_out, v_final_norm_g):
    weights = dict(mod_w=mod_w, mod_b=mod_b, norm_g=norm_g, ffn_w_gu=ffn_w_gu, ffn_w_down=ffn_w_down, sb_w_qkv=sb_w_qkv,
                   sb_w_o=sb_w_o, lru_w_in=lru_w_in, lru_conv_w=lru_conv_w, lru_conv_b=lru_conv_b, lru_w_r=lru_w_r,
                   lru_b_r=lru_b_r, lru_w_i=lru_w_i, lru_b_i=lru_b_i, lru_lambda=lru_lambda, lru_w_out=lru_w_out,
                   final_norm_g=final_norm_g)
    mom_m = dict(mod_w=m_mod_w, mod_b=m_mod_b, norm_g=m_norm_g, ffn_w_gu=m_ffn_w_gu, ffn_w_down=m_ffn_w_down,
                 sb_w_qkv=m_sb_w_qkv, sb_w_o=m_sb_w_o, lru_w_in=m_lru_w_in, lru_conv_w=m_lru_conv_w,
                 lru_conv_b=m_lru_conv_b, lru_w_r=m_lru_w_r, lru_b_r=m_lru_b_r, lru_w_i=m_lru_w_i, lru_b_i=m_lru_b_i,
                 lru_lambda=m_lru_lambda, lru_w_out=m_lru_w_out, final_norm_g=m_final_norm_g)
    mom_v = dict(mod_w=v_mod_w, mod_b=v_mod_b, norm_g=v_norm_g, ffn_w_gu=v_ffn_w_gu, ffn_w_down=v_ffn_w_down,
                 sb_w_qkv=v_sb_w_qkv, sb_w_o=v_sb_w_o, lru_w_in=v_lru_w_in, lru_conv_w=v_lru_conv_w,
                 lru_conv_b=v_lru_conv_b, lru_w_r=v_lru_w_r, lru_b_r=v_lru_b_r, lru_w_i=v_lru_w_i, lru_b_i=v_lru_b_i,
                 lru_lambda=v_lru_lambda, lru_w_out=v_lru_w_out, final_norm_g=v_final_norm_g)
    names = list(weights)

    t, d = x.shape[1], x.shape[2]
    n_layers = mod_w.shape[0]
    r_dim = lru_w_out.shape[1] * N_DEV
    ng, rs = d // N_DEV, r_dim // N_DEV
    mod_cols = mod_w.shape[2]
    nblk = lru_w_r.shape[1]
    xi, yi, ci = _mesh_pos()
    me = 4 * xi + 2 * yi + ci
    chip = 2 * xi + yi
    x2, target = x.reshape(t, d), loss_target.reshape(t, d)

    lru_small_shard = jnp.concatenate([lru_conv_w[0], lru_conv_b, lru_b_r, lru_b_i, lru_lambda], axis=0)
    small1 = _pack_flat([c, norm_g, lru_small_shard], LANES, 8, F32)
    n_small1 = small1.shape[0]
    all1 = _allgather(small1[None], "gather_small").reshape(N_DEV, n_small1 * LANES)
    c_all = all1[:, :d]
    norm_full = jnp.transpose(all1[:, d:d + 6 * ng].reshape(N_DEV, n_layers, 3, ng), (1, 2, 0, 3)).reshape(n_layers, 3, d)
    lru_small = jnp.transpose(all1[:, d + 6 * ng:d + 6 * ng + 8 * rs].reshape(N_DEV, 8, rs), (1, 0, 2)).reshape(8, r_dim)

    mod_b_cols = lax.dynamic_slice_in_dim(mod_b, me * mod_cols, mod_cols, axis=1).reshape(n_layers, 1, mod_cols)
    mod_part = _mod_fwd(c_all, mod_w, mod_b_cols)
    mod_all = _allgather(mod_part, "gather_mod")
    mod_mine = lax.dynamic_index_in_dim(mod_all, me, axis=2, keepdims=False)
    mod_mine = mod_mine.reshape(n_layers, 3, 3, d)

    assert sb_w_qkv.shape[0] == 1 and lru_w_in.shape[0] == 1, "one stick-breaking and one RG-LRU layer"
    n_ffn = 2 * n_layers
    fc = ffn_w_gu.shape[3]
    cw_in = lru_w_in.shape[2]
    pieces = {("ffn_w_gu", q): ffn_w_gu[q // 2, q % 2][None] for q in range(n_ffn)}
    pieces.update({("ffn_w_down", q): ffn_w_down[q // 2, q % 2][None] for q in range(n_ffn)})
    pieces.update({("sb_w_qkv", 0): sb_w_qkv, ("sb_w_o", 0): sb_w_o, ("lru_w_in", 0): lru_w_in, ("lru_w_out", 0): lru_w_out})
    col_window = {("sb_w_qkv", 0)}
    first = [("ffn_w_gu", 0)]
    behind = {"l0s0_gu": [("ffn_w_down", 0)], "l0s0_down": [("sb_w_qkv", 0), ("sb_w_o", 0)]}
    behind["sb_fwd"] = [key for key in pieces if key not in first + behind["l0s0_gu"] + behind["l0s0_down"]]
    gathered = {}

    def gather_plan(keys):
        return _gather_plan([pieces[key].astype(BF16) for key in keys], [key in col_window for key in keys])

    def hosting(name, call):
        keys = behind.get(name, [])
        outs = call(gather_plan(keys) if keys else None)
        gathered.update(zip(keys, outs[len(outs) - len(keys):]))
        return outs[:len(outs) - len(keys)]

    gathered.update(zip(first, _run_comm(gather_plan(first), "gather_first")))
    wr_b, wi_b = lru_w_r[0].astype(BF16), lru_w_i[0].astype(BF16)
    eye2 = jnp.eye(2 * cw_in, dtype=BF16).reshape(2, cw_in, 2 * cw_in)

    def w_gu(q):
        return gathered[("ffn_w_gu", q)]

    def w_d4(q):
        return gathered[("ffn_w_down", q)].reshape(1, HIDDEN_CHUNKS, fc, d)

    saved = []
    xcur = x2
    for layer in range(n_layers):
        for sub in range(3):
            gvec = norm_full[layer, sub].reshape(1, d)
            shift = mod_mine[layer, sub, 0].reshape(1, d)
            scale1p = 1.0 + mod_mine[layer, sub, 1].reshape(1, d)
            gmul = 1.0 + mod_mine[layer, sub, 2].reshape(1, d)
            tag = f"l{layer}s{sub}"
            h = _norm_fwd(xcur, gvec, scale1p, shift, tag + "_norm")
            rec = dict(x=xcur, h=h, g=gvec, scale1p=scale1p, gmul=gmul, w=MACARON_W if sub != 1 else 1.0)
            if sub != 1:
                lj = layer * 2 + sub // 2
                gu2, a = hosting(tag + "_gu", lambda plan: _ffn_gu(tag + "_gu", h, w_gu(lj), plan))
                yv, xcur = hosting(tag + "_down", lambda plan: _ffn_down(tag + "_down", a, w_d4(lj), xcur, gmul, plan))
                rec.update(kind="ffn", lj=lj, gu2=gu2, a=a, y=yv)
            elif layer % 2 == 0:
                w_qkv = gathered[("sb_w_qkv", 0)][0]
                w_o = gathered[("sb_w_o", 0)].reshape(d, d)
                qkv = _mm_nn(tag + "_qkv", h, w_qkv, BF16)[0]
                o, ltot = hosting("sb_fwd", lambda plan: _sb_fwd(qkv, d, plan))
                yv, xcur = _mm_nn(tag + "_wo", o, w_o, [BF16, F32], extras=[(xcur, "tile"), (gmul, "row")],
                                  epilogue=lambda accs, ex: (accs[0], ex[0] + ex[1] * accs[0]))
                rec.update(kind="sb", qkv=qkv, o=o, ltot=ltot, y=yv, w_qkv=w_qkv, w_o=w_o)
            else:
                w_in = _chunks_to_cols("lru_w_in_cols", gathered[("lru_w_in", 0)][0], eye2)
                w_out = gathered[("lru_w_out", 0)].reshape(r_dim, d)
                gx = _mm_nn(tag + "_win", h, w_in, F32)[0]
                ymix, hs = _lru_fwd(gx, lru_small, wr_b, wi_b)
                yv, xcur = _mm_nn(tag + "_wout", ymix, w_out, [BF16, F32], extras=[(xcur, "tile"), (gmul, "row")],
                                  epilogue=lambda accs, ex: (accs[0], ex[0] + ex[1] * accs[0]))
                rec.update(kind="lru", gx=gx, hs=hs, ymix=ymix, y=yv, w_in=w_in, w_out=w_out)
            saved.append(rec)

    last = saved[-1]
    dxo, dy, head_sums = _loss_head(xcur, target, final_norm_g.reshape(1, d), (last["w"] * last["gmul"]))
    loss = lax.psum(head_sums[1, 0], ("x", "y", "c"))
    dgf = head_sums[0]

    c_idx = jnp.reshape(ci, (1,)).astype(jnp.int32)
    chip_idx = jnp.reshape(chip, (1,)).astype(jnp.int32)
    grads, reduced = {}, {}
    to_pair = []
    to_chips = []

    def sibling_plan():
        keys = list(to_pair)
        if not keys:
            return None, keys
        return _exchange_plan([grads[key] for key in keys], [key in col_window for key in keys], 4, _sibling_route), keys

    def sibling_done(keys, recv4):
        for key, r4 in zip(keys, recv4):
            to_pair.remove(key)
            to_chips.append((key, _pair_sum(grads[key], r4, c_idx, f"rs_pair_sum_{key[0]}{key[1]}", cols=key in col_window)))

    def chip_plan():
        items = list(to_chips)
        if not items:
            return None, items
        return _exchange_plan([p4 for _, p4 in items], [False] * len(items), 3, _chip_route), items

    def chips_done(items, recv3):
        for item, r3 in zip(items, recv3):
            to_chips.remove(item)
            reduced[item[0]] = (item[1], r3)

    def behind(call, make_plan, done, more=None):
        plan, items = make_plan()
        n_mine = len(plan.outs) if plan else 0
        n_more = len(more.outs) if more else 0
        outs = call(_merge_plans([plan, more]))
        n_own = len(outs) - n_mine - n_more
        done(items, outs[n_own:n_own + n_mine])
        return list(outs[:n_own]) + list(outs[n_own + n_mine:])

    def at_once(make_plan, done, name):
        plan, items = make_plan()
        if plan:
            done(items, _run_comm(plan, name))

    def add_grad(key, value):
        grads[key] = value
        to_pair.append(key)

    dmod = [[None] * 3 for _ in range(n_layers)]
    dnorm = [[None] * 3 for _ in range(n_layers)]
    dlru_small = wri_all = None
    for idx in reversed(range(len(saved))):
        rec = saved[idx]
        layer, sub = divmod(idx, 3)
        tag = f"l{layer}s{sub}b"
        if rec["kind"] == "ffn" and idx > 0:
            lj = rec["lj"]
            (dgu2,) = behind(lambda plan: _ffn_da(tag + "_da", dy, w_d4(lj), rec["gu2"], plan), sibling_plan, sibling_done)
            add_grad(("ffn_w_down", lj), _ffn_dwd(tag + "_dwd", rec["a"], dy)[0].reshape(gathered[("ffn_w_down", lj)].shape))
            add_grad(("ffn_w_gu", lj), _ffn_dwgu(tag + "_dwgu", rec["h"], dgu2)[0])
            dh = _ffn_dh(tag + "_dh", dgu2, w_gu(lj))[0]
        elif rec["kind"] == "ffn":
            lj = rec["lj"]
            at_once(sibling_plan, sibling_done, "rs_sibling_" + tag)
            (dgu2,) = behind(lambda plan: _ffn_da(tag + "_da", dy, w_d4(lj), rec["gu2"], plan), chip_plan, chips_done)
            add_grad(("ffn_w_down", lj), _ffn_dwd(tag + "_dwd", rec["a"], dy)[0].reshape(gathered[("ffn_w_down", lj)].shape))
            at_once(sibling_plan, sibling_done, "rs_sibling_" + tag + "_dwd")
            (dwgu,) = behind(lambda plan: _ffn_dwgu(tag + "_dwgu", rec["h"], dgu2, plan), chip_plan, chips_done)
            add_grad(("ffn_w_gu", lj), dwgu)
            at_once(sibling_plan, sibling_done, "rs_sibling_" + tag + "_dwgu")
            (dh,) = behind(lambda plan: _ffn_dh(tag + "_dh", dgu2, w_gu(lj), plan), chip_plan, chips_done)
        elif rec["kind"] == "sb":
            at_once(sibling_plan, sibling_done, "rs_sibling_" + tag)
            do = _mm_nt(tag + "_do", dy, rec["w_o"], BF16)
            dwo = _mm_tn(tag + "_dwo", rec["o"], dy, BF16)
            wri = _pack_flat([dwr, dwi], LANES, 512, BF16)[None]
            dqkv3, wri_all = behind(lambda plan: _sb_bwd(rec["qkv"], do, rec["ltot"], d, plan), chip_plan, chips_done,
                                    more=_gather_plan([wri], [False]))
            add_grad(("sb_w_o", 0), dwo.reshape(gathered[("sb_w_o", 0)].shape))
            dh = _mm_nt_stack(tag + "_dh", dqkv3, rec["w_qkv"], F32)
            add_grad(("sb_w_qkv", 0), _mm_tn_stack(tag + "_dwqkv", rec["h"], dqkv3, BF16)[None])
        else:
            dymix = _mm_nt(tag + "_dymix", dy, rec["w_out"], F32)
            add_grad(("lru_w_out", 0), _mm_tn(tag + "_dwout", rec["ymix"], dy, BF16).reshape(gathered[("lru_w_out", 0)].shape))
            dgx2, dlru_small, dwr, dwi = _lru_bwd(rec["gx"], rec["hs"], dymix, lru_small, wr_b, wi_b)
            dh = _mm_nt_stack(tag + "_dh", dgx2, rec["w_in"], F32)
            dw_in = _mm_tn_stack(tag + "_dwin", rec["h"], dgx2, BF16)
            add_grad(("lru_w_in", 0), _cols_to_chunks("lru_w_in_chunks", dw_in, eye2)[None])
        prev = saved[idx - 1] if idx > 0 else None
        gw_prev = (prev["w"] * prev["gmul"]) if prev is not None else jnp.zeros((1, d), F32)
        dxo, dy, sums = _adaln_bwd(dh, rec["x"], rec["y"], dxo, rec["g"], rec["scale1p"], rec["w"], gw_prev, tag + "_adaln")
        dmod[layer][sub] = sums[0:3]
        dnorm[layer][sub] = sums[3]
    grad_x = dxo.reshape(x.shape)

    dmod_mine = jnp.stack([jnp.stack(dmod[layer]) for layer in range(n_layers)])
    dnorm_mine = jnp.stack([jnp.stack(dnorm[layer]) for layer in range(n_layers)])
    assert not to_pair and not to_chips
    small_shapes = [(n_layers, 9 * d), (n_layers, 3, d), (8, r_dim), (d,)]
    small3 = _pack_flat([dmod_mine, dnorm_mine, dlru_small, dgf], LANES, 256, F32)
    n_small3 = small3.shape[0]
    all3 = _allgather(small3[None], "gather_small_grads").reshape(N_DEV, n_small3, LANES)
    gsum = _sum_devices(all3, "sum_small_grads").reshape(-1)
    g_mod_b, g_norm_full, g_lru_small, g_final = _unpack_flat(gsum, small_shapes)
    wri_sum = _sum_devices(wri_all.reshape(N_DEV, -1, LANES), "sum_gate_weight_grads").reshape(-1)
    g_wr, g_wi = _unpack_flat(wri_sum, [lru_w_r.shape, lru_w_i.shape])
    dmod_all = all3.reshape(N_DEV, -1)[:, :n_layers * 9 * d].reshape(N_DEV, n_layers, N_DEV, mod_cols)
    dmod_cols = jnp.transpose(lax.dynamic_index_in_dim(dmod_all, me, axis=2, keepdims=False), (1, 0, 2))

    out_g, out_d, out_m, out_v = {}, {}, {}, {}
    out_g["mod_w"], out_d["mod_w"], out_m["mod_w"], out_v["mod_w"] = _mod_w_update(c_all, dmod_cols, mod_w, m_mod_w, v_mod_w)

    g_norm_shard = lax.dynamic_slice_in_dim(g_norm_full, me * ng, ng, axis=2)
    g_lru_shard = lax.dynamic_slice_in_dim(g_lru_small, me * rs, rs, axis=1)
    small_grads = dict(mod_b=g_mod_b, norm_g=g_norm_shard, lru_conv_w=g_lru_shard[0:4].reshape(lru_conv_w.shape),
                       lru_conv_b=g_lru_shard[4:5], lru_b_r=g_lru_shard[5:6], lru_b_i=g_lru_shard[6:7],
                       lru_lambda=g_lru_shard[7:8], final_norm_g=g_final)
    for n, g in (("lru_w_r", g_wr), ("lru_w_i", g_wi)):
        view = lambda arr: arr.reshape(-1, LRU_BLOCK_W)
        outs = _adam_update("adam_" + n, view(weights[n]), view(mom_m[n]), view(mom_v[n]), [view(g)])
        out_g[n], out_d[n], out_m[n], out_v[n] = [o.reshape(weights[n].shape) for o in outs]
    small_names = list(small_grads)
    sw = _pack_flat([weights[n] for n in small_names], LANES, 256, F32)
    sg = _pack_flat([small_grads[n] for n in small_names], LANES, 256, F32)
    sm = _pack_flat([mom_m[n] for n in small_names], LANES, 256, F32)
    sv = _pack_flat([mom_v[n] for n in small_names], LANES, 256, F32)
    s_outs = _adam_update("adam_small", sw, sm, sv, [sg])
    small_shapes2 = [weights[n].shape for n in small_names]
    for dst, flat in zip((out_g, out_d, out_m, out_v), s_outs):
        for n, arr in zip(small_names, _unpack_flat(flat.reshape(-1), small_shapes2)):
            dst[n] = arr

    for n in ["ffn_w_gu", "ffn_w_down", "sb_w_qkv", "sb_w_o", "lru_w_in", "lru_w_out"]:
        shp = weights[n].shape
        shard3 = (math.prod(shp[:-2]),) + shp[-2:]
        view = lambda arr: arr.reshape(shard3)
        outs = None
        for q in range(shard3[0]):
            fills = outs if outs is not None else [lax.empty(shard3, F32) for _ in range(4)]
            p4, r3 = reduced[(n, q)]
            outs = _adam_shard(f"adam_{n}{q}", view(weights[n]), view(mom_m[n]), view(mom_v[n]), p4, r3, chip_idx,
                               first=q, fills=fills if shard3[0] > 1 else None)
        out_g[n], out_d[n], out_m[n], out_v[n] = [o.reshape(shp) for o in outs]

    return (loss, grad_x, *[out_g[n] for n in names], *[out_d[n] for n in names], *[out_m[n] for n in names],
            *[out_v[n] for n in names])
```

```python
import functools
import math

import jax
import jax.numpy as jnp
from jax import lax
from jax.experimental import pallas as pl
from jax.experimental.pallas import tpu as pltpu

F32 = jnp.float32
BF16 = jnp.bfloat16
SDS = jax.ShapeDtypeStruct
MESH = pl.DeviceIdType.MESH
ANY = pl.BlockSpec(memory_space=pl.ANY)

N_DEV = 8
LANES = 128
HEAD_DIM = 64
LRU_BLOCK_W = 128
LRU_C = 8.0
MACARON_W = 0.5
NORM_EPS = 1e-6
ADAM_LR = 0.001
ADAM_B1 = 0.9
ADAM_B2 = 0.999
ADAM_EPS = 1e-08
ADAM_WD = 0.01
ADAM_STEP = 10
VMEM_LIMIT = 56 * 1024 * 1024
GELU_C = math.sqrt(2.0 / math.pi)
GELU_K = 0.044715

DIMS = {
    "nn": (((1,), (0,)), ((), ())),
    "nt": (((1,), (1,)), ((), ())),
    "tn": (((0,), (0,)), ((), ())),
}


def _pcall(body, **kw):
    return pl.pallas_call(body, **kw)


def _params(sem=None):
    return pltpu.CompilerParams(dimension_semantics=sem, vmem_limit_bytes=VMEM_LIMIT)


def _tile(n, prefs):
    for p in prefs:
        if n % p == 0:
            return p
    return n


def _dot(a, b, dims):
    return lax.dot_general(a, b, DIMS[dims], preferred_element_type=F32)


def _softplus(z):
    return jnp.maximum(z, 0.0) + jnp.log(1.0 + jnp.exp(-jnp.abs(z)))


def _mesh_pos():
    return lax.axis_index("x"), lax.axis_index("y"), lax.axis_index("c")


def _allgather(xs, name, cols=False):
    return _run_comm(_gather_plan([xs], [cols]), name)[0]


class _CommPlan:
    def __init__(self, ins, outs, n_remote, n_local, phases):
        self.ins, self.outs, self.n_remote, self.n_local, self.phases = ins, outs, n_remote, n_local, phases

    def scratch(self):
        return [pltpu.SemaphoreType.DMA((self.n_remote,)), pltpu.SemaphoreType.DMA((self.n_remote,)),
                pltpu.SemaphoreType.DMA((max(self.n_local, 1),))]


def _merge_plans(plans):
    plans = [p for p in plans if p is not None]
    if len(plans) <= 1:
        return plans[0] if plans else None

    def phase(k):
        def run(in_refs, out_refs, send_sems, recv_sems, local_sems, r0=0, l0=0):
            i0 = o0 = 0
            for p in plans:
                p.phases[k](in_refs[i0:i0 + len(p.ins)], out_refs[o0:o0 + len(p.outs)], send_sems, recv_sems, local_sems, r0, l0)
                i0, o0, r0, l0 = i0 + len(p.ins), o0 + len(p.outs), r0 + p.n_remote, l0 + p.n_local
        return run

    return _CommPlan(sum([p.ins for p in plans], []), sum([p.outs for p in plans], []), sum(p.n_remote for p in plans),
                     sum(p.n_local for p in plans), [phase(0), phase(1), phase(2)])


def _run_comm(plan, name):
    n_in, n_out = len(plan.ins), len(plan.outs)

    def body(*refs):
        in_refs, out_refs, sems = refs[:n_in], refs[n_in:n_in + n_out], refs[n_in + n_out:]
        for phase in plan.phases:
            phase(in_refs, out_refs, *sems)

    return _pcall(body, name=name, out_shape=plan.outs, in_specs=[ANY] * n_in, out_specs=[ANY] * n_out,
                  scratch_shapes=plan.scratch())(*plan.ins)


def _col_window(ref, idx, width):
    return ref.at[:, :, pl.ds(pl.multiple_of(idx * width, math.gcd(width, LANES)), width)]


def _gather_plan(shards, cols):
    n = len(shards)
    outs = [SDS((s.shape[0], s.shape[1], N_DEV * s.shape[2]) if cl else (s.shape[0], N_DEV) + s.shape[1:], s.dtype)
            for s, cl in zip(shards, cols)]

    def copies(a, in_refs, out_refs, send_sems, recv_sems, local_sems, r0=0, l0=0):
        x, y, c = _mesh_pos()
        sibling = (x, y, 1 - c)
        chips = [(1 - x, y), (x, 1 - y), (1 - x, 1 - y)]
        width = shards[a].shape[2]

        def block(px, py, pc):
            idx = 4 * px + 2 * py + pc
            return _col_window(out_refs[a], idx, width) if cols[a] else out_refs[a].at[:, idx]

        def copy(k, owner, to, src=None):
            sem = r0 + 7 * a + k
            return pltpu.make_async_remote_copy(
                src_ref=block(*owner) if src is None else src, dst_ref=block(*owner),
                send_sem=send_sems.at[sem], recv_sem=recv_sems.at[sem], device_id=to, device_id_type=MESH)

        me = (x, y, c)
        first = [copy(0, me, sibling, src=in_refs[a])]
        first += [copy(1 + j, me, (*chip, c), src=in_refs[a]) for j, chip in enumerate(chips)]
        passed = [copy(4 + j, (*chip, c), sibling) for j, chip in enumerate(chips)]
        landed = [copy(1 + j, (*chip, c), me) for j, chip in enumerate(chips)]
        from_sibling = [copy(0, sibling, me)] + [copy(4 + j, (*chip, 1 - c), me) for j, chip in enumerate(chips)]
        mine = pltpu.make_async_copy(in_refs[a], block(*me), local_sems.at[l0 + a])
        return first, passed, landed, from_sibling, mine

    def start(*refs):
        for a in range(n):
            first, _, _, _, mine = copies(a, *refs)
            mine.start()
            for cp in first:
                cp.start()

    def pass_on(*refs):
        for a in range(n):
            _, passed, landed, _, _ = copies(a, *refs)
            for cp, fwd in zip(landed, passed):
                cp.wait_recv()
                fwd.start()

    def finish(*refs):
        for a in range(n):
            first, passed, _, from_sibling, mine = copies(a, *refs)
            for cp in from_sibling:
                cp.wait_recv()
            for cp in first + passed:
                cp.wait_send()
            mine.wait()

    return _CommPlan(list(shards), outs, 7 * n, n, [start, pass_on, finish])


def _exchange_plan(srcs, cols, n_slots, route):
    n = len(srcs)
    outs = []
    for g, cl in zip(srcs, cols):
        shard = (g.shape[0], g.shape[1], g.shape[2] // N_DEV) if cl else (g.shape[0],) + g.shape[2:]
        outs.append(SDS((n_slots,) + shard, g.dtype))

    def copies(in_refs, out_refs, send_sems, recv_sems, local_sems, r0=0, l0=0):
        x, y, c = _mesh_pos()
        made = []
        for a in range(n):
            for s in range(n_slots):
                chunk, target = route(x, y, c, s)
                src = _col_window(in_refs[a], chunk, outs[a].shape[3]) if cols[a] else in_refs[a].at[:, chunk]
                sem = r0 + a * n_slots + s
                made.append(pltpu.make_async_remote_copy(
                    src_ref=src, dst_ref=out_refs[a].at[s], send_sem=send_sems.at[sem], recv_sem=recv_sems.at[sem],
                    device_id=target, device_id_type=MESH))
        return made

    def start(*refs):
        for cp in copies(*refs):
            cp.start()

    def nothing(*refs):
        pass

    def finish(*refs):
        made = copies(*refs)
        for cp in made:
            cp.wait_recv()
        for cp in made:
            cp.wait_send()

    return _CommPlan(list(srcs), outs, n * n_slots, 0, [start, nothing, finish])


def _sibling_route(x, y, c, k):
    return 2 * k + 1 - c, (x, y, 1 - c)


def _chip_route(x, y, c, j):
    px, py = [(1 - x, y), (x, 1 - y), (1 - x, 1 - y)][j]
    return 2 * px + py, (px, py, c)


def _pair_sum(grads, recv4, c_idx, name, cols=False):
    _, p, r, cdim = recv4.shape
    tr = _tile(r, (512, 256, 176, 160, 128, 64, 32, 16))

    def body(c_ref, a_ref, b_ref, o_ref):
        o_ref[...] = (a_ref[...].astype(F32) + b_ref[...].astype(F32)).astype(o_ref.dtype)

    blk = (None, None, tr, cdim)
    if cols:
        own = pl.BlockSpec((None, tr, cdim), lambda k, q, i, c_ref: (q, i, 2 * k + c_ref[0]))
    else:
        own = pl.BlockSpec(blk, lambda k, q, i, c_ref: (q, 2 * k + c_ref[0], i, 0))
    grid_spec = pltpu.PrefetchScalarGridSpec(
        num_scalar_prefetch=1, grid=(4, p, r // tr),
        in_specs=[own, pl.BlockSpec(blk, lambda k, q, i, c_ref: (k, q, i, 0))],
        out_specs=pl.BlockSpec(blk, lambda k, q, i, c_ref: (q, k, i, 0)))
    return _pcall(body, name=name, grid_spec=grid_spec, out_shape=SDS((p, 4, r, cdim), grads.dtype),
                  compiler_params=_params(("parallel", "parallel", "parallel")))(c_idx, grads, recv4)


def _mm(name, ins, prods, n_acc, acc_shape, epi_idx, epilogue, out_shapes, out_specs, grid, dims, plan=None):
    n_in, n_out, nk = len(ins), len(out_shapes), grid[2]
    n_acc_refs = n_acc if nk > 1 else 0
    c_ins, c_outs = (plan.ins, plan.outs) if plan else ([], [])
    n_cin, n_cout = len(c_ins), len(c_outs)

    def body(*refs):
        in_refs, c_in = refs[:n_in], refs[n_in:n_in + n_cin]
        rest = refs[n_in + n_cin:]
        out_refs, c_out = rest[:n_out], rest[n_out:n_out + n_cout]
        rest = rest[n_out + n_cout:]
        acc_refs, sems = rest[:n_acc_refs], rest[n_acc_refs:]
        ids = [pl.program_id(axis) for axis in range(3)]
        if plan:
            @pl.when((ids[0] == 0) & (ids[1] == 0) & (ids[2] == 0))
            def _():
                plan.phases[0](c_in, c_out, *sems)

        def finish(accs):
            outs = epilogue(accs, [in_refs[i][...] for i in epi_idx])
            for o_ref, o in zip(out_refs, outs):
                if isinstance(o, tuple):
                    for plane, part in enumerate(o):
                        o_ref[plane] = part.astype(o_ref.dtype)
                else:
                    o_ref[...] = o.astype(o_ref.dtype)

        if nk == 1:
            accs = [None] * n_acc
            for ia, ib, iacc in prods:
                term = _dot(in_refs[ia][...], in_refs[ib][...], dims)
                accs[iacc] = term if accs[iacc] is None else accs[iacc] + term
            finish(accs)
        else:
            @pl.when(ids[2] == 0)
            def _():
                for acc in acc_refs:
                    acc[...] = jnp.zeros_like(acc)

            for ia, ib, iacc in prods:
                acc_refs[iacc][...] += _dot(in_refs[ia][...], in_refs[ib][...], dims)

            @pl.when(ids[2] == nk - 1)
            def _():
                finish([acc[...] for acc in acc_refs])

        if plan:
            @pl.when((ids[0] == grid[0] - 1) & (ids[1] == grid[1] - 1) & (ids[2] == nk - 1))
            def _():
                plan.phases[1](c_in, c_out, *sems)
                plan.phases[2](c_in, c_out, *sems)

    return _pcall(
        body, name=name, grid=grid, in_specs=[s for _, s in ins] + [ANY] * n_cin,
        out_specs=list(out_specs) + [ANY] * n_cout, out_shape=list(out_shapes) + list(c_outs),
        scratch_shapes=[pltpu.VMEM(acc_shape, F32) for _ in range(n_acc_refs)] + (plan.scratch() if plan else []),
        compiler_params=_params(("arbitrary",) * 3 if plan else ("parallel", "parallel", "arbitrary")),
    )(*[a for a, _ in ins], *c_ins)


def _plain(accs, _):
    return accs


def _mm_nn(name, a, b, out_dtype, extras=(), epilogue=_plain, n_out=1):
    m, kd = a.shape
    n = b.shape[1]
    tm, tn, tk = _tile(m, (1024, 512, 256, 128)), _tile(n, (640, 512, 256, 128)), _tile(kd, (1280, 1024, 512, 256, 128))
    ins = [(a, pl.BlockSpec((tm, tk), lambda i, j, k: (i, k))), (b, pl.BlockSpec((tk, tn), lambda i, j, k: (k, j)))]
    for arr, kind in extras:
        if kind == "tile":
            ins.append((arr, pl.BlockSpec((tm, tn), lambda i, j, k: (i, j))))
        else:
            ins.append((arr, pl.BlockSpec((1, tn), lambda i, j, k: (0, j))))
    dts = out_dtype if isinstance(out_dtype, (list, tuple)) else [out_dtype] * n_out
    return _mm(name, ins, [(0, 1, 0)], 1, (tm, tn), list(range(2, len(ins))), epilogue,
               [SDS((m, n), dt) for dt in dts], [pl.BlockSpec((tm, tn), lambda i, j, k: (i, j)) for _ in dts],
               (m // tm, n // tn, kd // tk), "nn")


def _mm_nt(name, a, b, out_dtype):
    m, kd = a.shape
    n = b.shape[0]
    tm, tn, tk = _tile(m, (1024, 512, 256, 128)), _tile(n, (640, 512, 256, 128)), _tile(kd, (1024, 512, 256, 128))
    ins = [(a, pl.BlockSpec((tm, tk), lambda i, j, k: (i, k))), (b, pl.BlockSpec((tn, tk), lambda i, j, k: (j, k)))]
    return _mm(name, ins, [(0, 1, 0)], 1, (tm, tn), [], _plain, [SDS((m, n), out_dtype)],
               [pl.BlockSpec((tm, tn), lambda i, j, k: (i, j))], (m // tm, n // tn, kd // tk), "nt")[0]


def _mm_tn(name, a, b, out_dtype):
    t, m = a.shape
    n = b.shape[1]
    tm, tn, tk = _tile(m, (640, 512, 256, 128)), _tile(n, (1024, 512, 256, 128)), t
    ins = [(a, pl.BlockSpec((tk, tm), lambda i, j, k: (k, i))), (b, pl.BlockSpec((tk, tn), lambda i, j, k: (k, j)))]
    return _mm(name, ins, [(0, 1, 0)], 1, (tm, tn), [], _plain, [SDS((m, n), out_dtype)],
               [pl.BlockSpec((tm, tn), lambda i, j, k: (i, j))], (m // tm, n // tn, t // tk), "tn")[0]


def _mm_nt_stack(name, a3, b, out_dtype):
    cc, m, kd = a3.shape
    n = b.shape[0]
    tm, tn, tk = _tile(m, (1024, 512, 256, 128)), _tile(n, (1024, 512, 256, 128)), _tile(kd, (1280, 1024, 512, 256, 128))
    nk = kd // tk
    ins = [(a3, pl.BlockSpec((None, tm, tk), lambda i, j, k: (k // nk, i, k % nk))),
           (b, pl.BlockSpec((tn, tk), lambda i, j, k: (j, k)))]
    return _mm(name, ins, [(0, 1, 0)], 1, (tm, tn), [], _plain, [SDS((m, n), out_dtype)],
               [pl.BlockSpec((tm, tn), lambda i, j, k: (i, j))], (m // tm, n // tn, cc * nk), "nt")[0]


def _mm_tn_stack(name, a, b3, out_dtype):
    t, m = a.shape
    cc, _, n = b3.shape
    tm, tn, tk = _tile(m, (512, 256, 128)), _tile(n, (1280, 1024, 512, 256, 128)), t
    nj = n // tn
    ins = [(a, pl.BlockSpec((tk, tm), lambda i, j, k: (k, i))),
           (b3, pl.BlockSpec((None, tk, tn), lambda i, j, k: (j // nj, k, j % nj)))]
    return _mm(name, ins, [(0, 1, 0)], 1, (tm, tn), [], _plain, [SDS((m, cc * n), out_dtype)],
               [pl.BlockSpec((tm, tn), lambda i, j, k: (i, j))], (m // tm, cc * nj, t // tk), "tn")[0]


def _chunks_to_cols(name, wc, eye2):
    nch, d, cw = wc.shape
    tm = _tile(d, (1024, 512, 256, 128))
    ins = [(wc, pl.BlockSpec((None, tm, cw), lambda i, j, k: (2 * j + k, i, 0))),
           (eye2, pl.BlockSpec((None, cw, 2 * cw), lambda i, j, k: (k, 0, 0)))]
    return _mm(name, ins, [(0, 1, 0)], 1, (tm, 2 * cw), [], _plain, [SDS((d, nch * cw), wc.dtype)],
               [pl.BlockSpec((tm, 2 * cw), lambda i, j, k: (i, j))], (d // tm, nch // 2, 2), "nn")[0]


def _cols_to_chunks(name, full, eye2):
    d, n = full.shape
    _, cw, _ = eye2.shape
    nch = n // cw
    tm = _tile(d, (1024, 512, 256, 128))
    ins = [(full, pl.BlockSpec((tm, 2 * cw), lambda i, j, k: (i, j // 2))),
           (eye2, pl.BlockSpec((None, cw, 2 * cw), lambda i, j, k: (j % 2, 0, 0)))]
    return _mm(name, ins, [(0, 1, 0)], 1, (tm, cw), [], _plain, [SDS((nch, d, cw), full.dtype)],
               [pl.BlockSpec((None, tm, cw), lambda i, j, k: (j, i, 0))], (d // tm, nch, 1), "nt")[0]


def _row_tile(t):
    return _tile(t, (256, 128, 64, 32, 16, 8))


def _norm_fwd(x, g, scale1p, shift, name):
    t, d = x.shape
    tr = _row_tile(t)

    def body(x_ref, g_ref, s_ref, b_ref, h_ref):
        xv = x_ref[...]
        inv = lax.rsqrt(jnp.mean(xv * xv, axis=-1, keepdims=True) + NORM_EPS)
        h_ref[...] = ((xv * inv) * g_ref[...] * s_ref[...] + b_ref[...]).astype(h_ref.dtype)

    vec = pl.BlockSpec((1, d), lambda i: (0, 0))
    return _pcall(body, name=name, grid=(t // tr,), in_specs=[pl.BlockSpec((tr, d), lambda i: (i, 0)), vec, vec, vec],
                  out_specs=pl.BlockSpec((tr, d), lambda i: (i, 0)), out_shape=SDS((t, d), BF16),
                  compiler_params=_params(("parallel",)))(x, g, scale1p, shift)


def _adaln_bwd(dh, x, y, dxo, g, scale1p, w_sub, gw_prev, name):
    t, d = x.shape
    tr = _row_tile(t)

    def body(dh_ref, x_ref, y_ref, dxo_ref, g_ref, s_ref, gw_ref, dx_ref, dyp_ref, sums_ref):
        i = pl.program_id(0)

        @pl.when(i == 0)
        def _():
            sums_ref[...] = jnp.zeros_like(sums_ref)

        xv, dhv, dxov = x_ref[...], dh_ref[...], dxo_ref[...]
        inv = lax.rsqrt(jnp.mean(xv * xv, axis=-1, keepdims=True) + NORM_EPS)
        xn = xv * inv
        gv = g_ref[...]
        dn = dhv * s_ref[...]
        dxn = dn * gv
        dx = inv * (dxn - xn * jnp.mean(dxn * xn, axis=-1, keepdims=True)) + dxov
        dx_ref[...] = dx
        dyp_ref[...] = (gw_ref[...] * dx).astype(dyp_ref.dtype)
        sums_ref[0:1, :] += jnp.sum(dhv, axis=0, keepdims=True)
        sums_ref[1:2, :] += jnp.sum(dhv * (xn * gv), axis=0, keepdims=True)
        sums_ref[2:3, :] += jnp.sum(w_sub * y_ref[...] * dxov, axis=0, keepdims=True)
        sums_ref[3:4, :] += jnp.sum(dn * xn, axis=0, keepdims=True)

    blk = pl.BlockSpec((tr, d), lambda i: (i, 0))
    vec = pl.BlockSpec((1, d), lambda i: (0, 0))
    return _pcall(
        body, name=name, grid=(t // tr,), in_specs=[blk, blk, blk, blk, vec, vec, vec],
        out_specs=[blk, blk, pl.BlockSpec((8, d), lambda i: (0, 0))],
        out_shape=[SDS((t, d), F32), SDS((t, d), BF16), SDS((8, d), F32)],
        compiler_params=_params(("arbitrary",)))(dh, x, y, dxo, g, scale1p, gw_prev)


def _loss_head(x, target, gf, gw_prev):
    t, d = x.shape
    tr = _row_tile(t)
    nt = t // tr

    def body(x_ref, tg_ref, g_ref, gw_ref, dx_ref, dyp_ref, sums_ref):
        i = pl.program_id(0)

        @pl.when(i == 0)
        def _():
            sums_ref[...] = jnp.zeros_like(sums_ref)

        xv = x_ref[...]
        inv = lax.rsqrt(jnp.mean(xv * xv, axis=-1, keepdims=True) + NORM_EPS)
        xn = xv * inv
        gv = g_ref[...]
        err = xn * gv - tg_ref[...]
        dyv = err * (1.0 / d)
        dxn = dyv * gv
        dx = inv * (dxn - xn * jnp.mean(dxn * xn, axis=-1, keepdims=True))
        dx_ref[...] = dx
        dyp_ref[...] = (gw_ref[...] * dx).astype(dyp_ref.dtype)
        sums_ref[0:1, :] += jnp.sum(dyv * xn, axis=0, keepdims=True)
        sums_ref[1:2, :] += jnp.sum(err * err, axis=0, keepdims=True)

        @pl.when(i == nt - 1)
        def _():
            tot = jnp.sum(sums_ref[1:2, :], axis=1, keepdims=True) * (0.5 / d)
            sums_ref[1:2, :] = jnp.broadcast_to(tot, (1, d))

    blk = pl.BlockSpec((tr, d), lambda i: (i, 0))
    vec = pl.BlockSpec((1, d), lambda i: (0, 0))
    return _pcall(
        body, name="loss_head", grid=(nt,), in_specs=[blk, blk, vec, vec],
        out_specs=[blk, blk, pl.BlockSpec((8, d), lambda i: (0, 0))],
        out_shape=[SDS((t, d), F32), SDS((t, d), BF16), SDS((8, d), F32)],
        compiler_params=_params(("arbitrary",)))(x, target, gf, gw_prev)


HIDDEN_CHUNKS = N_DEV // 2


def _ffn_tiles(t, d):
    return _tile(t, (1024, 512, 256, 128)), _tile(d, (1024, 512, 256, 128))


def _ffn_gu(name, h, wgu, plan=None):
    t, d = h.shape
    fc, nc = wgu.shape[3], HIDDEN_CHUNKS
    tm, _ = _ffn_tiles(t, d)

    def epi_gu(accs, _):
        gpre, up = accs
        return (gpre, up), gpre * jax.nn.sigmoid(gpre) * up

    wblk = (None, None, d, fc)
    ins = [(h, pl.BlockSpec((tm, d), lambda i, c, k: (i, 0))),
           (wgu, pl.BlockSpec(wblk, lambda i, c, k: (0, c, 0, 0))),
           (wgu, pl.BlockSpec(wblk, lambda i, c, k: (0, c + nc, 0, 0)))]
    return _mm(name, ins, [(0, 1, 0), (0, 2, 1)], 2, (tm, fc), [], epi_gu,
               [SDS((2, nc, t, fc), BF16), SDS((nc, t, fc), BF16)],
               [pl.BlockSpec((2, None, tm, fc), lambda i, c, k: (0, c, i, 0)),
                pl.BlockSpec((None, tm, fc), lambda i, c, k: (c, i, 0))],
               (t // tm, nc, 1), "nn", plan=plan)


def _ffn_down(name, a, wd4, x, gmul, plan=None):
    nc, t, fc = a.shape
    d = wd4.shape[3]
    tm, tn = _ffn_tiles(t, d)

    def epi_down(accs, ex):
        (yv,), (xv, gm) = accs, ex
        return yv, xv + MACARON_W * gm * yv

    ins = [(a, pl.BlockSpec((None, tm, fc), lambda i, j, k: (k, i, 0))),
           (wd4, pl.BlockSpec((None, None, fc, tn), lambda i, j, k: (0, k, 0, j))),
           (x, pl.BlockSpec((tm, tn), lambda i, j, k: (i, j))), (gmul, pl.BlockSpec((1, tn), lambda i, j, k: (0, j)))]
    oblk = pl.BlockSpec((tm, tn), lambda i, j, k: (i, j))
    return _mm(name, ins, [(0, 1, 0)], 1, (tm, tn), [2, 3], epi_down, [SDS((t, d), BF16), SDS((t, d), F32)],
               [oblk, oblk], (t // tm, d // tn, nc), "nn", plan=plan)


def _ffn_da(name, dy, wd4, gu2, plan=None):
    t, d = dy.shape
    _, nc, fc, _ = wd4.shape
    tm, _ = _ffn_tiles(t, d)

    def epi_da(accs, ex):
        (da,), (gu,) = accs, ex
        gpre, up = gu[0].astype(F32), gu[1].astype(F32)
        s = jax.nn.sigmoid(gpre)
        silu = gpre * s
        dg = da * up * (s * (1.0 + gpre * (1.0 - s)))
        return ((dg, da * silu),)

    gblk = pl.BlockSpec((2, None, tm, fc), lambda i, c, k: (0, c, i, 0))
    ins = [(dy, pl.BlockSpec((tm, d), lambda i, c, k: (i, 0))),
           (wd4, pl.BlockSpec((None, None, fc, d), lambda i, c, k: (0, c, 0, 0))), (gu2, gblk)]
    return _mm(name, ins, [(0, 1, 0)], 1, (tm, fc), [2], epi_da, [SDS((2, nc, t, fc), BF16)], [gblk],
               (t // tm, nc, 1), "nt", plan=plan)


def _ffn_dwd(name, a, dy, plan=None):
    nc, t, fc = a.shape
    d = dy.shape[1]
    _, tn = _ffn_tiles(t, d)
    ins = [(a, pl.BlockSpec((None, t, fc), lambda c, j, k: (c, 0, 0))), (dy, pl.BlockSpec((t, tn), lambda c, j, k: (0, j)))]
    return _mm(name, ins, [(0, 1, 0)], 1, (fc, tn), [], _plain, [SDS((1, nc, fc, d), BF16)],
               [pl.BlockSpec((None, None, fc, tn), lambda c, j, k: (0, c, 0, j))], (nc, d // tn, 1), "tn", plan=plan)


def _ffn_dwgu(name, h, dgu2, plan=None):
    t, d = h.shape
    _, nc, _, fc = dgu2.shape
    _, tn = _ffn_tiles(t, d)
    ins = [(h, pl.BlockSpec((t, tn), lambda i, c, k: (0, i))),
           (dgu2, pl.BlockSpec((None, None, t, fc), lambda i, c, k: (c // nc, c % nc, 0, 0)))]
    return _mm(name, ins, [(0, 1, 0)], 1, (tn, fc), [], _plain, [SDS((1, 2 * nc, d, fc), BF16)],
               [pl.BlockSpec((None, None, tn, fc), lambda i, c, k: (0, c, i, 0))], (d // tn, 2 * nc, 1), "tn", plan=plan)


def _ffn_dh(name, dgu2, wgu, plan=None):
    _, nc, t, fc = dgu2.shape
    d = wgu.shape[2]
    tm, tn = _ffn_tiles(t, d)
    ins = [(dgu2, pl.BlockSpec((None, None, tm, fc), lambda i, j, k: (k // nc, k % nc, i, 0))),
           (wgu, pl.BlockSpec((None, None, tn, fc), lambda i, j, k: (0, k, j, 0)))]
    return _mm(name, ins, [(0, 1, 0)], 1, (tm, tn), [], _plain, [SDS((t, d), F32)],
               [pl.BlockSpec((tm, tn), lambda i, j, k: (i, j))], (t // tm, d // tn, 2 * nc), "nt", plan=plan)


def _sb_block(t):
    return 256 if t >= 1024 else 128


SB_STRIP = 32


def _sb_strips(blk):
    strip = min(SB_STRIP, blk)
    row = lax.broadcasted_iota(jnp.int32, (strip, blk), 0)
    col = lax.broadcasted_iota(jnp.int32, (strip, blk), 1)
    return [(slice(r0, r0 + strip), col < row + r0) for r0 in range(0, blk, strip)]


def _host_call(core, name, steps, ins, in_specs, out_shapes, out_specs, scratch, plan):
    n_in, n_out, n_scr = len(ins), len(out_shapes), len(scratch)
    c_ins, c_outs = (plan.ins, plan.outs) if plan else ([], [])
    n_cin, n_cout = len(c_ins), len(c_outs)

    def body(*refs):
        in_refs, c_in = refs[:n_in], refs[n_in:n_in + n_cin]
        rest = refs[n_in + n_cin:]
        out_refs, c_out = rest[:n_out], rest[n_out:n_out + n_cout]
        rest = rest[n_out + n_cout:]
        scr, sems = rest[:n_scr], rest[n_scr:]
        step = pl.program_id(0)
        if plan:
            @pl.when(step == 0)
            def _():
                plan.phases[0](c_in, c_out, *sems)

        core(in_refs, out_refs, scr)
        if plan:
            @pl.when(step == steps - 1)
            def _():
                plan.phases[1](c_in, c_out, *sems)
                plan.phases[2](c_in, c_out, *sems)

    return _pcall(
        body, name=name, grid=(steps,), in_specs=list(in_specs) + [ANY] * n_cin,
        out_specs=list(out_specs) + [ANY] * n_cout, out_shape=list(out_shapes) + list(c_outs),
        scratch_shapes=list(scratch) + (plan.scratch() if plan else []),
        compiler_params=_params(("arbitrary",)))(*ins, *c_ins)


def _sb_fwd(qkv, d, plan=None):
    t = qkv.shape[0]
    blk = _sb_block(t)
    nq = t // blk
    npair = d // LANES
    scale = HEAD_DIM ** -0.5

    def body(in_refs, out_refs, scr):
        (q_ref, k_ref, v_ref), (o_ref, l_ref) = in_refs, out_refs
        tri_s, hi_s, lo_s, w_s, zs_s = scr
        lane = lax.broadcasted_iota(jnp.int32, (blk, LANES), 1)
        head0 = lane < HEAD_DIM
        row = lax.broadcasted_iota(jnp.int32, (blk, blk), 0)
        col = lax.broadcasted_iota(jnp.int32, (blk, blk), 1)
        tri_s[...] = (row > col).astype(BF16)
        strips = _sb_strips(blk)

        def both(qhs, kb, carries, masked):
            start = pl.multiple_of(kb * blk, blk)
            kv = k_ref[pl.ds(start, blk), :]
            vv = v_ref[pl.ds(start, blk), :]
            zs = [_dot(qh, kv, "nt") for qh in qhs]
            sums = []
            for hh in range(2):
                parts = []
                for rows, causal in strips:
                    zt = zs[hh][rows, :] * scale
                    sp = _softplus(zt)
                    lk = jnp.where(causal, -sp, 0.0) if masked else -sp
                    hi = lk.astype(BF16)
                    hi_s[hh, rows, :] = hi
                    lo_s[hh, rows, :] = (lk - hi.astype(F32)).astype(BF16)
                    zs_s[hh, rows, :] = zt - sp
                    parts.append(jnp.sum(lk, axis=1, keepdims=True))
                sums.append(jnp.concatenate(parts, axis=0))
            laters = [_dot(hi_s[hh], tri_s[...], "nn") + _dot(lo_s[hh], tri_s[...], "nn") for hh in range(2)]
            for hh in range(2):
                cl = carries[hh][0]
                for rows, causal in strips:
                    logw = zs_s[hh, rows, :] + laters[hh][rows, :] + cl[rows, :]
                    if masked:
                        logw = jnp.where(causal, logw, -1e30)
                    w_s[hh, rows, :] = jnp.exp(logw).astype(BF16)
            return tuple((carries[hh][0] + sums[hh], carries[hh][1] + _dot(w_s[hh], vv, "nn")) for hh in range(2))

        def qblock(qi, _):
            qstart = pl.multiple_of(qi * blk, blk)
            qv = q_ref[pl.ds(qstart, blk), :]
            qhs = [jnp.where(head0 if hh == 0 else ~head0, qv, jnp.zeros_like(qv)) for hh in range(2)]
            zero = (jnp.zeros((blk, 1), F32), jnp.zeros((blk, LANES), F32))
            outs = both(qhs, qi, (zero, zero), True)
            outs = lax.fori_loop(0, qi, lambda j, crs: both(qhs, qi - 1 - j, crs, False), outs)
            o_ref[pl.ds(qstart, blk), :] = jnp.where(head0, outs[0][1], outs[1][1]).astype(o_ref.dtype)
            l_ref[pl.ds(qstart, blk), :] = jnp.where(head0, outs[0][0], outs[1][0])
            return 0

        lax.fori_loop(0, nq, qblock, 0)

    pair_bf16 = pltpu.VMEM((2, blk, blk), BF16)
    return _host_call(
        body, "sb_fwd", npair, [qkv, qkv, qkv],
        [pl.BlockSpec((t, LANES), lambda p: (0, p)), pl.BlockSpec((t, LANES), lambda p: (0, npair + p)),
         pl.BlockSpec((t, LANES), lambda p: (0, 2 * npair + p))],
        [SDS((t, d), BF16), SDS((t, d), F32)],
        [pl.BlockSpec((t, LANES), lambda p: (0, p)), pl.BlockSpec((t, LANES), lambda p: (0, p))],
        [pltpu.VMEM((blk, blk), BF16), pair_bf16, pair_bf16, pair_bf16, pltpu.VMEM((2, blk, blk), F32)], plan)


def _sb_bwd(qkv, do, ltot, d, plan=None):
    t = qkv.shape[0]
    blk = _sb_block(t)
    nq = t // blk
    npair = d // LANES
    scale = HEAD_DIM ** -0.5

    def body(in_refs, out_refs, scr):
        (q_ref, k_ref, v_ref, do_ref, l_ref), (out_ref,) = in_refs, out_refs
        dq_s, dk_s, dv_s, upto_s, before_s, hi_s, lo_s, w_s, dab_s, dzs_s, zs_s, da_s = scr
        lane = lax.broadcasted_iota(jnp.int32, (blk, LANES), 1)
        head0 = lane < HEAD_DIM
        row = lax.broadcasted_iota(jnp.int32, (blk, blk), 0)
        col = lax.broadcasted_iota(jnp.int32, (blk, blk), 1)
        upto_s[...] = (row <= col).astype(BF16)
        before_s[...] = (row < col).astype(BF16)
        dk_s[...] = jnp.zeros_like(dk_s)
        dv_s[...] = jnp.zeros_like(dv_s)
        strips = _sb_strips(blk)

        def both(heads, kb, carries, masked):
            start = pl.multiple_of(kb * blk, blk)
            kv = k_ref[pl.ds(start, blk), :]
            vv = v_ref[pl.ds(start, blk), :]
            zs = [_dot(qh, kv, "nt") for qh, _, _ in heads]
            dws = [_dot(doh, vv, "nt") for _, doh, _ in heads]
            lk_sums, da_sums = [], []
            for hh in range(2):
                parts = []
                for rows, causal in strips:
                    zt = zs[hh][rows, :] * scale
                    sp = _softplus(zt)
                    lk = jnp.where(causal, -sp, 0.0) if masked else -sp
                    hi = lk.astype(BF16)
                    hi_s[hh, rows, :] = hi
                    lo_s[hh, rows, :] = (lk - hi.astype(F32)).astype(BF16)
                    zs_s[hh, rows, :] = zt - sp
                    parts.append(jnp.sum(lk, axis=1, keepdims=True))
                lk_sums.append(jnp.concatenate(parts, axis=0))
            cums = [_dot(hi_s[hh], upto_s[...], "nn") + _dot(lo_s[hh], upto_s[...], "nn") for hh in range(2)]
            for hh in range(2):
                lt, plk = heads[hh][2], carries[hh][0]
                parts = []
                for rows, causal in strips:
                    logw = zs_s[hh, rows, :] + (lt[rows, :] - (plk[rows, :] + cums[hh][rows, :]))
                    if masked:
                        logw = jnp.where(causal, logw, -1e30)
                    w = jnp.exp(logw)
                    w_s[hh, rows, :] = w.astype(BF16)
                    da = dws[hh][rows, :] * w
                    da_s[hh, rows, :] = da
                    dab_s[hh, rows, :] = da.astype(BF16)
                    parts.append(jnp.sum(da, axis=1, keepdims=True))
                da_sums.append(jnp.concatenate(parts, axis=0))
            pres = [_dot(dab_s[hh], before_s[...], "nn") for hh in range(2)]
            for hh in range(2):
                pda = carries[hh][1]
                for rows, causal in strips:
                    sig = jnp.exp(zs_s[hh, rows, :])
                    da = da_s[hh, rows, :]
                    dz = da * (1.0 - sig) - sig * (pda[rows, :] + pres[hh][rows, :])
                    if masked:
                        dz = jnp.where(causal, dz, 0.0)
                    dzs_s[hh, rows, :] = (dz * scale).astype(BF16)
            new = []
            for hh in range(2):
                qh, doh, _ = heads[hh]
                dk_s[pl.ds(start, blk), :] += _dot(dzs_s[hh], qh, "tn")
                dv_s[pl.ds(start, blk), :] += _dot(w_s[hh], doh, "tn")
                new.append((carries[hh][0] + lk_sums[hh], carries[hh][1] + da_sums[hh],
                            carries[hh][2] + _dot(dzs_s[hh], kv, "nn")))
            return tuple(new)

        def qblock(qi, _):
            qstart = pl.multiple_of(qi * blk, blk)
            qv = q_ref[pl.ds(qstart, blk), :]
            dov = do_ref[pl.ds(qstart, blk), :]
            lv = l_ref[pl.ds(qstart, blk), :]
            heads = []
            for hh in range(2):
                sel = head0 if hh == 0 else ~head0
                heads.append((jnp.where(sel, qv, jnp.zeros_like(qv)), jnp.where(sel, dov, jnp.zeros_like(dov)),
                              jnp.max(jnp.where(sel, lv, -jnp.inf), axis=1, keepdims=True)))
            zero = (jnp.zeros((blk, 1), F32), jnp.zeros((blk, 1), F32), jnp.zeros((blk, LANES), F32))
            carries = lax.fori_loop(0, qi, lambda kb, crs: both(heads, kb, crs, False), (zero, zero))
            carries = both(heads, qi, carries, True)
            dq_s[pl.ds(qstart, blk), :] = jnp.where(head0, carries[0][2], carries[1][2])
            return 0

        lax.fori_loop(0, nq, qblock, 0)
        out_ref[0] = dq_s[...].astype(out_ref.dtype)
        out_ref[1] = dk_s[...].astype(out_ref.dtype)
        out_ref[2] = dv_s[...].astype(out_ref.dtype)

    col_blk = lambda off: pl.BlockSpec((t, LANES), lambda p: (0, off + p))
    return _host_call(
        body, "sb_bwd", npair, [qkv, qkv, qkv, do, ltot],
        [col_blk(0), col_blk(npair), col_blk(2 * npair), col_blk(0), col_blk(0)],
        [SDS((3, t, d), BF16)], [pl.BlockSpec((3, t, LANES), lambda p: (0, 0, p))],
        [pltpu.VMEM((t, LANES), F32) for _ in range(3)] + [pltpu.VMEM((blk, blk), BF16) for _ in range(2)]
        + [pltpu.VMEM((2, blk, blk), BF16) for _ in range(5)] + [pltpu.VMEM((2, blk, blk), F32) for _ in range(2)], plan)


def _roll_rows(v, shift):
    return pltpu.roll(v, shift, 0)


def _shift_down(v, dist, fill, row):
    return jnp.where(row >= dist, _roll_rows(v, dist), fill)


def _shift_up(v, dist, fill, row):
    t = v.shape[0]
    return jnp.where(row < t - dist, _roll_rows(v, t - dist), fill)


def _lru_gates(xb, small, wr, wi, row):
    xs = [_shift_down(xb, 3 - tap, 0.0, row) if tap < 3 else xb for tap in range(4)]
    xc = small[4:5, :] + xs[0] * small[0:1, :]
    for tap in range(1, 4):
        xc = xc + xs[tap] * small[tap:tap + 1, :]
    xcb = xc.astype(BF16)
    r = jax.nn.sigmoid(_dot(xcb, wr, "nn") + small[5:6, :])
    ig = jax.nn.sigmoid(_dot(xcb, wi, "nn") + small[6:7, :])
    sp = _softplus(-small[7:8, :])
    la = -LRU_C * r * sp
    a = jnp.exp(la)
    th = jnp.tanh(la)
    mult = jnp.sqrt(-2.0 * th / (1.0 - th))
    return xs, xc, xcb, r, ig, sp, a, mult


def _gelu_parts(gate):
    inner = GELU_C * (gate + GELU_K * gate * gate * gate)
    th = jnp.tanh(inner)
    gelu = 0.5 * gate * (1.0 + th)
    dgelu = 0.5 * (1.0 + th) + 0.5 * gate * (1.0 - th * th) * GELU_C * (1.0 + 3.0 * GELU_K * gate * gate)
    return gelu, dgelu


def _scan_steps(t):
    steps, dist = [], 1
    while dist < t:
        steps.append(dist)
        dist *= 2
    return steps


def _lru_fwd(gx, small, wr, wi):
    t = gx.shape[0]
    r_dim = gx.shape[1] // 2
    nb = r_dim // LRU_BLOCK_W

    def body(gate_ref, xb_ref, small_ref, wr_ref, wi_ref, y_ref, hs_ref):
        row = lax.broadcasted_iota(jnp.int32, (t, LRU_BLOCK_W), 0)
        xb = xb_ref[...]
        _, xc, _, _, ig, _, a, mult = _lru_gates(xb, small_ref, wr_ref[...], wi_ref[...], row)
        b = mult * (ig * xc)
        for dist in _scan_steps(t):
            b = a * _shift_down(b, dist, 0.0, row) + b
            a = a * _shift_down(a, dist, 1.0, row)
        hs_ref[...] = b
        gelu, _ = _gelu_parts(gate_ref[...])
        y_ref[...] = (gelu * b).astype(y_ref.dtype)

    colb = lambda off: pl.BlockSpec((t, LRU_BLOCK_W), lambda n: (0, off + n))
    wspec = pl.BlockSpec((None, LRU_BLOCK_W, LRU_BLOCK_W), lambda n: (n, 0, 0))
    return _pcall(
        body, name="lru_fwd", grid=(nb,),
        in_specs=[colb(0), colb(nb), pl.BlockSpec((8, LRU_BLOCK_W), lambda n: (0, n)), wspec, wspec],
        out_specs=[colb(0), colb(0)], out_shape=[SDS((t, r_dim), BF16), SDS((t, r_dim), F32)],
        compiler_params=_params(("parallel",)))(gx, gx, small, wr, wi)


def _lru_bwd(gx, hs, dy, small, wr, wi):
    t = gx.shape[0]
    r_dim = gx.shape[1] // 2
    nb = r_dim // LRU_BLOCK_W

    def body(gate_ref, xb_ref, hs_ref, dy_ref, small_ref, wr_ref, wi_ref, dgx_ref, dsm_ref, dwr_ref, dwi_ref):
        row = lax.broadcasted_iota(jnp.int32, (t, LRU_BLOCK_W), 0)
        xb, hsv, dyv, smallv = xb_ref[...], hs_ref[...], dy_ref[...], small_ref
        wrv, wiv = wr_ref[...], wi_ref[...]
        xs, xc, xcb, r, ig, sp, a, mult = _lru_gates(xb, smallv, wrv, wiv, row)
        gelu, dgelu = _gelu_parts(gate_ref[...])
        dgx_ref[0] = (dyv * hsv * dgelu).astype(dgx_ref.dtype)
        dacc = dyv * gelu
        an = _shift_up(a, 1, 1.0, row)
        for dist in _scan_steps(t):
            dacc = dacc + an * _shift_up(dacc, dist, 0.0, row)
            an = an * _shift_up(an, dist, 1.0, row)
        da = dacc * _shift_down(hsv, 1, 0.0, row)
        dmult = dacc * (ig * xc)
        dixc = dacc * mult
        dla = da * a - dmult * (a * a) / mult
        dr = dla * (-LRU_C * sp)
        dsp = jnp.sum(dla * (-LRU_C * r), axis=0, keepdims=True)
        dpr = dr * r * (1.0 - r)
        dpi = dixc * xc * ig * (1.0 - ig)
        dprb, dpib = dpr.astype(BF16), dpi.astype(BF16)
        dwr_ref[...] = _dot(xcb, dprb, "tn")
        dwi_ref[...] = _dot(xcb, dpib, "tn")
        dxc = dixc * ig + _dot(dprb, wrv, "nt") + _dot(dpib, wiv, "nt")
        dxb = dxc * smallv[3:4, :]
        for tap in range(3):
            dxb = dxb + _shift_up(dxc, 3 - tap, 0.0, row) * smallv[tap:tap + 1, :]
        dgx_ref[1] = dxb.astype(dgx_ref.dtype)
        lam = smallv[7:8, :]
        rows = [jnp.sum(dxc * xs[tap], axis=0, keepdims=True) for tap in range(4)]
        rows.append(jnp.sum(dxc, axis=0, keepdims=True))
        rows.append(jnp.sum(dpr, axis=0, keepdims=True))
        rows.append(jnp.sum(dpi, axis=0, keepdims=True))
        rows.append(-dsp * jax.nn.sigmoid(-lam))
        for k, rv in enumerate(rows):
            dsm_ref[k:k + 1, :] = rv

    colb = lambda off: pl.BlockSpec((t, LRU_BLOCK_W), lambda n: (0, off + n))
    wspec = pl.BlockSpec((None, LRU_BLOCK_W, LRU_BLOCK_W), lambda n: (n, 0, 0))
    sspec = pl.BlockSpec((8, LRU_BLOCK_W), lambda n: (0, n))
    return _pcall(
        body, name="lru_bwd", grid=(nb,),
        in_specs=[colb(0), colb(nb), colb(0), colb(0), sspec, wspec, wspec],
        out_specs=[pl.BlockSpec((2, t, LRU_BLOCK_W), lambda n: (0, 0, n)), sspec, wspec, wspec],
        out_shape=[SDS((2, t, r_dim), BF16), SDS((8, r_dim), F32), SDS((nb, LRU_BLOCK_W, LRU_BLOCK_W), F32),
                   SDS((nb, LRU_BLOCK_W, LRU_BLOCK_W), F32)],
        compiler_params=_params(("parallel",)))(gx, gx, hs, dy, small, wr, wi)


def _adam(w, g, m, v):
    m2 = ADAM_B1 * m + (1.0 - ADAM_B1) * g
    v2 = ADAM_B2 * v + (1.0 - ADAM_B2) * (g * g)
    m_hat = m2 / (1.0 - ADAM_B1 ** ADAM_STEP)
    v_hat = v2 / (1.0 - ADAM_B2 ** ADAM_STEP)
    return -ADAM_LR * (m_hat / (jnp.sqrt(v_hat) + ADAM_EPS) + ADAM_WD * w), m2, v2


def _mod_fwd(c_all, mod_w, mod_b_cols):
    nl, d, cols = mod_w.shape
    nbatch = c_all.shape[0]

    def body(c_ref, w_ref, b_ref, o_ref):
        cv = c_ref[...]
        ca = (cv * jax.nn.sigmoid(cv)).astype(BF16)
        o_ref[...] = _dot(ca, w_ref[...].astype(BF16), "nn") + b_ref[...]

    return _pcall(
        body, name="mod_fwd", grid=(nl,),
        in_specs=[pl.BlockSpec((nbatch, d), lambda l: (0, 0)), pl.BlockSpec((None, d, cols), lambda l: (l, 0, 0)),
                  pl.BlockSpec((None, 1, cols), lambda l: (l, 0, 0))],
        out_specs=pl.BlockSpec((None, nbatch, cols), lambda l: (l, 0, 0)), out_shape=SDS((nl, nbatch, cols), F32),
        compiler_params=_params(("parallel",)))(c_all, mod_w, mod_b_cols)


def _mod_w_update(c_all, dmod_cols, w, m, v):
    nl, d, cols = w.shape
    nbatch = c_all.shape[0]
    tr = _tile(d, (256, 128))

    def body(c_ref, dm_ref, w_ref, m_ref, v_ref, g_ref, dl_ref, m2_ref, v2_ref):
        cv = c_ref[...]
        ca = (cv * jax.nn.sigmoid(cv)).astype(BF16)
        g = _dot(ca, dm_ref[...].astype(BF16), "tn")
        g_ref[...] = g
        dl_ref[...], m2_ref[...], v2_ref[...] = _adam(w_ref[...], g, m_ref[...], v_ref[...])

    wblk = pl.BlockSpec((None, tr, cols), lambda l, i: (l, i, 0))
    return _pcall(
        body, name="mod_w_update", grid=(nl, d // tr),
        in_specs=[pl.BlockSpec((nbatch, tr), lambda l, i: (0, i)), pl.BlockSpec((None, nbatch, cols), lambda l, i: (l, 0, 0)),
                  wblk, wblk, wblk],
        out_specs=[wblk] * 4, out_shape=[SDS(w.shape, F32)] * 4,
        compiler_params=_params(("parallel", "parallel")))(c_all, dmod_cols, w, m, v)


def _adam_update(name, w, m, v, gparts):
    rows, cols = w.shape
    tr = _tile(rows, (256, 128, 64, 32, 16, 8))
    npart = len(gparts)

    def body(*refs):
        w_ref, m_ref, v_ref = refs[:3]
        g_refs = refs[3:3 + npart]
        g_ref, dl_ref, m2_ref, v2_ref = refs[3 + npart:]
        g = g_refs[0][...].astype(F32)
        for gr in g_refs[1:]:
            g = g + gr[...].astype(F32)
        g_ref[...] = g
        dl_ref[...], m2_ref[...], v2_ref[...] = _adam(w_ref[...], g, m_ref[...], v_ref[...])

    blk = pl.BlockSpec((tr, cols), lambda i: (i, 0))
    return _pcall(body, name=name, grid=(rows // tr,), in_specs=[blk] * (3 + npart), out_specs=[blk] * 4,
                  out_shape=[SDS((rows, cols), F32)] * 4, compiler_params=_params(("parallel",)))(w, m, v, *gparts)


def _adam_shard(name, w, m, v, part4, recv3, chip_idx, first=0, fills=None):
    p, r, cdim = w.shape
    pg = part4.shape[0]
    tr = _tile(r, (256, 176, 160, 128, 64, 32, 16))

    def body(chip_ref, w_ref, m_ref, v_ref, own_ref, r0_ref, r1_ref, r2_ref, *rest):
        g_ref, dl_ref, m2_ref, v2_ref = rest[-4:]
        g = own_ref[...].astype(F32) + r0_ref[...].astype(F32) + r1_ref[...].astype(F32) + r2_ref[...].astype(F32)
        g_ref[...] = g
        dl_ref[...], m2_ref[...], v2_ref[...] = _adam(w_ref[...], g, m_ref[...], v_ref[...])

    blk = pl.BlockSpec((None, tr, cdim), lambda q, i, chip_ref: (first + q, i, 0))
    blk4 = (None, None, tr, cdim)
    slot = lambda s: pl.BlockSpec(blk4, lambda q, i, chip_ref: (s, q, i, 0))
    fills = list(fills or [])
    grid_spec = pltpu.PrefetchScalarGridSpec(
        num_scalar_prefetch=1, grid=(pg, r // tr),
        in_specs=[blk, blk, blk, pl.BlockSpec(blk4, lambda q, i, chip_ref: (q, chip_ref[0], i, 0)), slot(0), slot(1), slot(2)]
        + [ANY] * len(fills),
        out_specs=[blk] * 4)
    return _pcall(body, name=name, grid_spec=grid_spec, out_shape=[SDS((p, r, cdim), F32)] * 4,
                  input_output_aliases={8 + k: k for k in range(len(fills))},
                  compiler_params=_params(("parallel", "parallel")))(chip_idx, w, m, v, part4, recv3, recv3, recv3, *fills)


def _sum_devices(gathered, name):
    _, rows, cols = gathered.shape
    tr = _tile(rows, (512, 256, 128, 64, 32, 16, 8))

    def body(g_ref, o_ref):
        acc = g_ref[0].astype(F32)
        for k in range(1, N_DEV):
            acc = acc + g_ref[k].astype(F32)
        o_ref[...] = acc

    return _pcall(body, name=name, grid=(rows // tr,), in_specs=[pl.BlockSpec((N_DEV, tr, cols), lambda i: (0, i, 0))],
                  out_specs=pl.BlockSpec((tr, cols), lambda i: (i, 0)), out_shape=SDS((rows, cols), F32),
                  compiler_params=_params(("parallel",)))(gathered)


def _pack_flat(parts, width, row_mult, dtype):
    flat = jnp.concatenate([p.reshape(-1).astype(dtype) for p in parts])
    unit = width * row_mult
    pad = (-flat.shape[0]) % unit
    if pad:
        flat = jnp.concatenate([flat, jnp.zeros((pad,), dtype)])
    return flat.reshape(-1, width)


def _unpack_flat(flat, shapes):
    out, off = [], 0
    for shp in shapes:
        size = math.prod(shp)
        out.append(flat[off:off + size].reshape(shp))
        off += size
    return out


def kernel(x, c, mod_w, mod_b, norm_g, ffn_w_gu, ffn_w_down, sb_w_qkv, sb_w_o, lru_w_in, lru_conv_w, lru_conv_b, lru_w_r, lru_b_r, lru_w_i, lru_b_i, lru_lambda, lru_w_out, final_norm_g, loss_target, m_mod_w, m_mod_b, m_norm_g, m_ffn_w_gu, m_ffn_w_down, m_sb_w_qkv, m_sb_w_o, m_lru_w_in, m_lru_conv_w, m_lru_conv_b, m_lru_w_r, m_lru_b_r, m_lru_w_i, m_lru_b_i, m_lru_lambda, m_lru_w_out, m_final_norm_g, v_mod_w, v_mod_b, v_norm_g, v_ffn_w_gu, v_ffn_w_down, v_sb_w_qkv, v_sb_w_o, v_lru_w_in, v_lru_conv_w, v_lru_conv_b, v_lru_w_r, v_lru_b_r, v_lru_w_i, v_lru_b_i, v_lru_lambda, v_lru_w_out, v_final_norm_g):
    weights = dict(mod_w=mod_w, mod_b=mod_b, norm_g=norm_g, ffn_w_gu=ffn_w_gu, ffn_w_down=ffn_w_down, sb_w_qkv=sb_w_qkv,
                   sb_w_o=sb_w_o, lru_w_in=lru_w_in, lru_conv_w=lru_conv_w, lru_conv_b=lru_conv_b, lru_w_r=lru_w_r,
                   lru_b_r=lru_b_r, lru_w_i=lru_w_i, lru_b_i=lru_b_i, lru_lambda=lru_lambda, lru_w_out=lru_w_out,
                   final_norm_g=final_norm_g)
    mom_m = dict(mod_w=m_mod_w, mod_b=m_mod_b, norm_g=m_norm_g, ffn_w_gu=m_ffn_w_gu, ffn_w_down=m_ffn_w_down,
                 sb_w_qkv=m_sb_w_qkv, sb_w_o=m_sb_w_o, lru_w_in=m_lru_w_in, lru_conv_w=m_lru_conv_w,
                 lru_conv_b=m_lru_conv_b, lru_w_r=m_lru_w_r, lru_b_r=m_lru_b_r, lru_w_i=m_lru_w_i, lru_b_i=m_lru_b_i,
                 lru_lambda=m_lru_lambda, lru_w_out=m_lru_w_out, final_norm_g=m_final_norm_g)
    mom_v = dict(mod_w=v_mod_w, mod_b=v_mod_b, norm_g=v_norm_g, ffn_w_gu=v_ffn_w_gu, ffn_w_down=v_ffn_w_down,
                 sb_w_qkv=v_sb_w_qkv, sb_w_o=v_sb_w_o, lru_w_in=v_lru_w_in, lru_conv_w=v_lru_conv_w,
                 lru_conv_b=v_lru_conv_b, lru_w_r=v_lru_w_r, lru_b_r=v_lru_b_r, lru_w_i=v_lru_w_i, lru_b_i=v_lru_b_i,
                 lru_lambda=v_lru_lambda, lru_w_out=v_lru_w_out, final_norm_g=v_final_norm_g)
    names = list(weights)

    t, d = x.shape[1], x.shape[2]
    n_layers = mod_w.shape[0]
    r_dim = lru_w_out.shape[1] * N_DEV
    ng, rs = d // N_DEV, r_dim // N_DEV
    mod_cols = mod_w.shape[2]
    nblk = lru_w_r.shape[1]
    xi, yi, ci = _mesh_pos()
    me = 4 * xi + 2 * yi + ci
    chip = 2 * xi + yi
    x2, target = x.reshape(t, d), loss_target.reshape(t, d)

    lru_small_shard = jnp.concatenate([lru_conv_w[0], lru_conv_b, lru_b_r, lru_b_i, lru_lambda], axis=0)
    small1 = _pack_flat([c, norm_g, lru_small_shard], LANES, 8, F32)
    n_small1 = small1.shape[0]
    all1 = _allgather(small1[None], "gather_small").reshape(N_DEV, n_small1 * LANES)
    c_all = all1[:, :d]
    norm_full = jnp.transpose(all1[:, d:d + 6 * ng].reshape(N_DEV, n_layers, 3, ng), (1, 2, 0, 3)).reshape(n_layers, 3, d)
    lru_small = jnp.transpose(all1[:, d + 6 * ng:d + 6 * ng + 8 * rs].reshape(N_DEV, 8, rs), (1, 0, 2)).reshape(8, r_dim)

    mod_b_cols = lax.dynamic_slice_in_dim(mod_b, me * mod_cols, mod_cols, axis=1).reshape(n_layers, 1, mod_cols)
    mod_part = _mod_fwd(c_all, mod_w, mod_b_cols)
    mod_all = _allgather(mod_part, "gather_mod")
    mod_mine = lax.dynamic_index_in_dim(mod_all, me, axis=2, keepdims=False)
    mod_mine = mod_mine.reshape(n_layers, 3, 3, d)

    assert sb_w_qkv.shape[0] == 1 and lru_w_in.shape[0] == 1, "one stick-breaking and one RG-LRU layer"
    n_ffn = 2 * n_layers
    fc = ffn_w_gu.shape[3]
    cw_in = lru_w_in.shape[2]
    pieces = {("ffn_w_gu", q): ffn_w_gu[q // 2, q % 2][None] for q in range(n_ffn)}
    pieces.update({("ffn_w_down", q): ffn_w_down[q // 2, q % 2][None] for q in range(n_ffn)})
    pieces.update({("sb_w_qkv", 0): sb_w_qkv, ("sb_w_o", 0): sb_w_o, ("lru_w_in", 0): lru_w_in, ("lru_w_out", 0): lru_w_out})
    col_window = {("sb_w_qkv", 0)}
    first = [("ffn_w_gu", 0)]
    behind = {"l0s0_gu": [("ffn_w_down", 0)], "l0s0_down": [("sb_w_qkv", 0), ("sb_w_o", 0)]}
    behind["sb_fwd"] = [key for key in pieces if key not in first + behind["l0s0_gu"] + behind["l0s0_down"]]
    gathered = {}

    def gather_plan(keys):
        return _gather_plan([pieces[key].astype(BF16) for key in keys], [key in col_window for key in keys])

    def hosting(name, call):
        keys = behind.get(name, [])
        outs = call(gather_plan(keys) if keys else None)
        gathered.update(zip(keys, outs[len(outs) - len(keys):]))
        return outs[:len(outs) - len(keys)]

    gathered.update(zip(first, _run_comm(gather_plan(first), "gather_first")))
    wr_b, wi_b = lru_w_r[0].astype(BF16), lru_w_i[0].astype(BF16)
    eye2 = jnp.eye(2 * cw_in, dtype=BF16).reshape(2, cw_in, 2 * cw_in)

    def w_gu(q):
        return gathered[("ffn_w_gu", q)]

    def w_d4(q):
        return gathered[("ffn_w_down", q)].reshape(1, HIDDEN_CHUNKS, fc, d)

    saved = []
    xcur = x2
    for layer in range(n_layers):
        for sub in range(3):
            gvec = norm_full[layer, sub].reshape(1, d)
            shift = mod_mine[layer, sub, 0].reshape(1, d)
            scale1p = 1.0 + mod_mine[layer, sub, 1].reshape(1, d)
            gmul = 1.0 + mod_mine[layer, sub, 2].reshape(1, d)
            tag = f"l{layer}s{sub}"
            h = _norm_fwd(xcur, gvec, scale1p, shift, tag + "_norm")
            rec = dict(x=xcur, h=h, g=gvec, scale1p=scale1p, gmul=gmul, w=MACARON_W if sub != 1 else 1.0)
            if sub != 1:
                lj = layer * 2 + sub // 2
                gu2, a = hosting(tag + "_gu", lambda plan: _ffn_gu(tag + "_gu", h, w_gu(lj), plan))
                yv, xcur = hosting(tag + "_down", lambda plan: _ffn_down(tag + "_down", a, w_d4(lj), xcur, gmul, plan))
                rec.update(kind="ffn", lj=lj, gu2=gu2, a=a, y=yv)
            elif layer % 2 == 0:
                w_qkv = gathered[("sb_w_qkv", 0)][0]
                w_o = gathered[("sb_w_o", 0)].reshape(d, d)
                qkv = _mm_nn(tag + "_qkv", h, w_qkv, BF16)[0]
                o, ltot = hosting("sb_fwd", lambda plan: _sb_fwd(qkv, d, plan))
                yv, xcur = _mm_nn(tag + "_wo", o, w_o, [BF16, F32], extras=[(xcur, "tile"), (gmul, "row")],
                                  epilogue=lambda accs, ex: (accs[0], ex[0] + ex[1] * accs[0]))
                rec.update(kind="sb", qkv=qkv, o=o, ltot=ltot, y=yv, w_qkv=w_qkv, w_o=w_o)
            else:
                w_in = _chunks_to_cols("lru_w_in_cols", gathered[("lru_w_in", 0)][0], eye2)
                w_out = gathered[("lru_w_out", 0)].reshape(r_dim, d)
                gx = _mm_nn(tag + "_win", h, w_in, F32)[0]
                ymix, hs = _lru_fwd(gx, lru_small, wr_b, wi_b)
                yv, xcur = _mm_nn(tag + "_wout", ymix, w_out, [BF16, F32], extras=[(xcur, "tile"), (gmul, "row")],
                                  epilogue=lambda accs, ex: (accs[0], ex[0] + ex[1] * accs[0]))
                rec.update(kind="lru", gx=gx, hs=hs, ymix=ymix, y=yv, w_in=w_in, w_out=w_out)
            saved.append(rec)

    last = saved[-1]
    dxo, dy, head_sums = _loss_head(xcur, target, final_norm_g.reshape(1, d), (last["w"] * last["gmul"]))
    loss = lax.psum(head_sums[1, 0], ("x", "y", "c"))
    dgf = head_sums[0]

    c_idx = jnp.reshape(ci, (1,)).astype(jnp.int32)
    chip_idx = jnp.reshape(chip, (1,)).astype(jnp.int32)
    grads, reduced = {}, {}
    to_pair = []
    to_chips = []

    def sibling_plan():
        keys = list(to_pair)
        if not keys:
            return None, keys
        return _exchange_plan([grads[key] for key in keys], [key in col_window for key in keys], 4, _sibling_route), keys

    def sibling_done(keys, recv4):
        for key, r4 in zip(keys, recv4):
            to_pair.remove(key)
            to_chips.append((key, _pair_sum(grads[key], r4, c_idx, f"rs_pair_sum_{key[0]}{key[1]}", cols=key in col_window)))

    def chip_plan():
        items = list(to_chips)
        if not items:
            return None, items
        return _exchange_plan([p4 for _, p4 in items], [False] * len(items), 3, _chip_route), items

    def chips_done(items, recv3):
        for item, r3 in zip(items, recv3):
            to_chips.remove(item)
            reduced[item[0]] = (item[1], r3)

    def behind(call, make_plan, done, more=None):
        plan, items = make_plan()
        n_mine = len(plan.outs) if plan else 0
        n_more = len(more.outs) if more else 0
        outs = call(_merge_plans([plan, more]))
        n_own = len(outs) - n_mine - n_more
        done(items, outs[n_own:n_own + n_mine])
        return list(outs[:n_own]) + list(outs[n_own + n_mine:])

    def at_once(make_plan, done, name):
        plan, items = make_plan()
        if plan:
            done(items, _run_comm(plan, name))

    def add_grad(key, value):
        grads[key] = value
        to_pair.append(key)

    dmod = [[None] * 3 for _ in range(n_layers)]
    dnorm = [[None] * 3 for _ in range(n_layers)]
    dlru_small = wri_all = None
    for idx in reversed(range(len(saved))):
        rec = saved[idx]
        layer, sub = divmod(idx, 3)
        tag = f"l{layer}s{sub}b"
        if rec["kind"] == "ffn" and idx > 0:
            lj = rec["lj"]
            (dgu2,) = behind(lambda plan: _ffn_da(tag + "_da", dy, w_d4(lj), rec["gu2"], plan), sibling_plan, sibling_done)
            add_grad(("ffn_w_down", lj), _ffn_dwd(tag + "_dwd", rec["a"], dy)[0].reshape(gathered[("ffn_w_down", lj)].shape))
            add_grad(("ffn_w_gu", lj), _ffn_dwgu(tag + "_dwgu", rec["h"], dgu2)[0])
            dh = _ffn_dh(tag + "_dh", dgu2, w_gu(lj))[0]
        elif rec["kind"] == "ffn":
            lj = rec["lj"]
            at_once(sibling_plan, sibling_done, "rs_sibling_" + tag)
            (dgu2,) = behind(lambda plan: _ffn_da(tag + "_da", dy, w_d4(lj), rec["gu2"], plan), chip_plan, chips_done)
            add_grad(("ffn_w_down", lj), _ffn_dwd(tag + "_dwd", rec["a"], dy)[0].reshape(gathered[("ffn_w_down", lj)].shape))
            at_once(sibling_plan, sibling_done, "rs_sibling_" + tag + "_dwd")
            (dwgu,) = behind(lambda plan: _ffn_dwgu(tag + "_dwgu", rec["h"], dgu2, plan), chip_plan, chips_done)
            add_grad(("ffn_w_gu", lj), dwgu)
            at_once(sibling_plan, sibling_done, "rs_sibling_" + tag + "_dwgu")
            (dh,) = behind(lambda plan: _ffn_dh(tag + "_dh", dgu2, w_gu(lj), plan), chip_plan, chips_done)
        elif rec["kind"] == "sb":
            at_once(sibling_plan, sibling_done, "rs_sibling_" + tag)
            do = _mm_nt(tag + "_do", dy, rec["w_o"], BF16)
            dwo = _mm_tn(tag + "_dwo", rec["o"], dy, BF16)
            wri = _pack_flat([dwr, dwi], LANES, 512, BF16)[None]
            dqkv3, wri_all = behind(lambda plan: _sb_bwd(rec["qkv"], do, rec["ltot"], d, plan), chip_plan, chips_done,
                                    more=_gather_plan([wri], [False]))
            add_grad(("sb_w_o", 0), dwo.reshape(gathered[("sb_w_o", 0)].shape))
            dh = _mm_nt_stack(tag + "_dh", dqkv3, rec["w_qkv"], F32)
            add_grad(("sb_w_qkv", 0), _mm_tn_stack(tag + "_dwqkv", rec["h"], dqkv3, BF16)[None])
        else:
            dymix = _mm_nt(tag + "_dymix", dy, rec["w_out"], F32)
            add_grad(("lru_w_out", 0), _mm_tn(tag + "_dwout", rec["ymix"], dy, BF16).reshape(gathered[("lru_w_out", 0)].shape))
            dgx2, dlru_small, dwr, dwi = _lru_bwd(rec["gx"], rec["hs"], dymix, lru_small, wr_b, wi_b)
            dh = _mm_nt_stack(tag + "_dh", dgx2, rec["w_in"], F32)
            dw_in = _mm_tn_stack(tag + "_dwin", rec["h"], dgx2, BF16)
            add_grad(("lru_w_in", 0), _cols_to_chunks("lru_w_in_chunks", dw_in, eye2)[None])
        prev = saved[idx - 1] if idx > 0 else None
        gw_prev = (prev["w"] * prev["gmul"]) if prev is not None else jnp.zeros((1, d), F32)
        dxo, dy, sums = _adaln_bwd(dh, rec["x"], rec["y"], dxo, rec["g"], rec["scale1p"], rec["w"], gw_prev, tag + "_adaln")
        dmod[layer][sub] = sums[0:3]
        dnorm[layer][sub] = sums[3]
    grad_x = dxo.reshape(x.shape)

    dmod_mine = jnp.stack([jnp.stack(dmod[layer]) for layer in range(n_layers)])
    dnorm_mine = jnp.stack([jnp.stack(dnorm[layer]) for layer in range(n_layers)])
    assert not to_pair and not to_chips
    small_shapes = [(n_layers, 9 * d), (n_layers, 3, d), (8, r_dim), (d,)]
    small3 = _pack_flat([dmod_mine, dnorm_mine, dlru_small, dgf], LANES, 256, F32)
    n_small3 = small3.shape[0]
    all3 = _allgather(small3[None], "gather_small_grads").reshape(N_DEV, n_small3, LANES)
    gsum = _sum_devices(all3, "sum_small_grads").reshape(-1)
    g_mod_b, g_norm_full, g_lru_small, g_final = _unpack_flat(gsum, small_shapes)
    wri_sum = _sum_devices(wri_all.reshape(N_DEV, -1, LANES), "sum_gate_weight_grads").reshape(-1)
    g_wr, g_wi = _unpack_flat(wri_sum, [lru_w_r.shape, lru_w_i.shape])
    dmod_all = all3.reshape(N_DEV, -1)[:, :n_layers * 9 * d].reshape(N_DEV, n_layers, N_DEV, mod_cols)
    dmod_cols = jnp.transpose(lax.dynamic_index_in_dim(dmod_all, me, axis=2, keepdims=False), (1, 0, 2))

    out_g, out_d, out_m, out_v = {}, {}, {}, {}
    out_g["mod_w"], out_d["mod_w"], out_m["mod_w"], out_v["mod_w"] = _mod_w_update(c_all, dmod_cols, mod_w, m_mod_w, v_mod_w)

    g_norm_shard = lax.dynamic_slice_in_dim(g_norm_full, me * ng, ng, axis=2)
    g_lru_shard = lax.dynamic_slice_in_dim(g_lru_small, me * rs, rs, axis=1)
    small_grads = dict(mod_b=g_mod_b, norm_g=g_norm_shard, lru_conv_w=g_lru_shard[0:4].reshape(lru_conv_w.shape),
                       lru_conv_b=g_lru_shard[4:5], lru_b_r=g_lru_shard[5:6], lru_b_i=g_lru_shard[6:7],
                       lru_lambda=g_lru_shard[7:8], final_norm_g=g_final)
    for n, g in (("lru_w_r", g_wr), ("lru_w_i", g_wi)):
        view = lambda arr: arr.reshape(-1, LRU_BLOCK_W)
        outs = _adam_update("adam_" + n, view(weights[n]), view(mom_m[n]), view(mom_v[n]), [view(g)])
        out_g[n], out_d[n], out_m[n], out_v[n] = [o.reshape(weights[n].shape) for o in outs]
    small_names = list(small_grads)
    sw = _pack_flat([weights[n] for n in small_names], LANES, 256, F32)
    sg = _pack_flat([small_grads[n] for n in small_names], LANES, 256, F32)
    sm = _pack_flat([mom_m[n] for n in small_names], LANES, 256, F32)
    sv = _pack_flat([mom_v[n] for n in small_names], LANES, 256, F32)
    s_outs = _adam_update("adam_small", sw, sm, sv, [sg])
    small_shapes2 = [weights[n].shape for n in small_names]
    for dst, flat in zip((out_g, out_d, out_m, out_v), s_outs):
        for n, arr in zip(small_names, _unpack_flat(flat.reshape(-1), small_shapes2)):
            dst[n] = arr

    for n in ["ffn_w_gu", "ffn_w_down", "sb_w_qkv", "sb_w_o", "lru_w_in", "lru_w_out"]:
        shp = weights[n].shape
        shard3 = (math.prod(shp[:-2]),) + shp[-2:]
        view = lambda arr: arr.reshape(shard3)
        outs = None
        for q in range(shard3[0]):
            fills = outs if outs is not None else [lax.empty(shard3, F32) for _ in range(4)]
            p4, r3 = reduced[(n, q)]
            outs = _adam_shard(f"adam_{n}{q}", view(weights[n]), view(mom_m[n]), view(mom_v[n]), p4, r3, chip_idx,
                               first=q, fills=fills if shard3[0] > 1 else None)
        out_g[n], out_d[n], out_m[n], out_v[n] = [o.reshape(shp) for o in outs]

    return (loss, grad_x, *[out_g[n] for n in names], *[out_d[n] for n in names], *[out_m[n] for n in names],
            *[out_v[n] for n in names])
```

```python
import functools
import math

import jax
import jax.numpy as jnp
from jax import lax
from jax.experimental import pallas as pl
from jax.experimental.pallas import tpu as pltpu

F32 = jnp.float32
BF16 = jnp.bfloat16
SDS = jax.ShapeDtypeStruct
MESH = pl.DeviceIdType.MESH
ANY = pl.BlockSpec(memory_space=pl.ANY)

N_DEV = 8
LANES = 128
HEAD_DIM = 64
LRU_BLOCK_W = 128
LRU_C = 8.0
MACARON_W = 0.5
NORM_EPS = 1e-6
ADAM_LR = 0.001
ADAM_B1 = 0.9
ADAM_B2 = 0.999
ADAM_EPS = 1e-08
ADAM_WD = 0.01
ADAM_STEP = 10
VMEM_LIMIT = 56 * 1024 * 1024
GELU_C = math.sqrt(2.0 / math.pi)
GELU_K = 0.044715

DIMS = {
    "nn": (((1,), (0,)), ((), ())),
    "nt": (((1,), (1,)), ((), ())),
    "tn": (((0,), (0,)), ((), ())),
}


def _pcall(body, **kw):
    return pl.pallas_call(body, **kw)


def _params(sem=None):
    return pltpu.CompilerParams(dimension_semantics=sem, vmem_limit_bytes=VMEM_LIMIT)


def _tile(n, prefs):
    for p in prefs:
        if n % p == 0:
            return p
    return n


def _dot(a, b, dims):
    return lax.dot_general(a, b, DIMS[dims], preferred_element_type=F32)


def _softplus(z):
    return jnp.maximum(z, 0.0) + jnp.log(1.0 + jnp.exp(-jnp.abs(z)))


def _mesh_pos():
    return lax.axis_index("x"), lax.axis_index("y"), lax.axis_index("c")


def _allgather(xs, name, cols=False):
    return _run_comm(_gather_plan([xs], [cols]), name)[0]


class _CommPlan:
    def __init__(self, ins, outs, n_remote, n_local, phases):
        self.ins, self.outs, self.n_remote, self.n_local, self.phases = ins, outs, n_remote, n_local, phases

    def scratch(self):
        return [pltpu.SemaphoreType.DMA((self.n_remote,)), pltpu.SemaphoreType.DMA((self.n_remote,)),
                pltpu.SemaphoreType.DMA((max(self.n_local, 1),))]


def _merge_plans(plans):
    plans = [p for p in plans if p is not None]
    if len(plans) <= 1:
        return plans[0] if plans else None

    def phase(k):
        def run(in_refs, out_refs, send_sems, recv_sems, local_sems, r0=0, l0=0):
            i0 = o0 = 0
            for p in plans:
                p.phases[k](in_refs[i0:i0 + len(p.ins)], out_refs[o0:o0 + len(p.outs)], send_sems, recv_sems, local_sems, r0, l0)
                i0, o0, r0, l0 = i0 + len(p.ins), o0 + len(p.outs), r0 + p.n_remote, l0 + p.n_local
        return run

    return _CommPlan(sum([p.ins for p in plans], []), sum([p.outs for p in plans], []), sum(p.n_remote for p in plans),
                     sum(p.n_local for p in plans), [phase(0), phase(1), phase(2)])


def _run_comm(plan, name):
    n_in, n_out = len(plan.ins), len(plan.outs)

    def body(*refs):
        in_refs, out_refs, sems = refs[:n_in], refs[n_in:n_in + n_out], refs[n_in + n_out:]
        for phase in plan.phases:
            phase(in_refs, out_refs, *sems)

    return _pcall(body, name=name, out_shape=plan.outs, in_specs=[ANY] * n_in, out_specs=[ANY] * n_out,
                  scratch_shapes=plan.scratch())(*plan.ins)


def _col_window(ref, idx, width):
    return ref.at[:, :, pl.ds(pl.multiple_of(idx * width, math.gcd(width, LANES)), width)]


def _gather_plan(shards, cols):
    n = len(shards)
    outs = [SDS((s.shape[0], s.shape[1], N_DEV * s.shape[2]) if cl else (s.shape[0], N_DEV) + s.shape[1:], s.dtype)
            for s, cl in zip(shards, cols)]

    def copies(a, in_refs, out_refs, send_sems, recv_sems, local_sems, r0=0, l0=0):
        x, y, c = _mesh_pos()
        sibling = (x, y, 1 - c)
        chips = [(1 - x, y), (x, 1 - y), (1 - x, 1 - y)]
        width = shards[a].shape[2]

        def block(px, py, pc):
            idx = 4 * px + 2 * py + pc
            return _col_window(out_refs[a], idx, width) if cols[a] else out_refs[a].at[:, idx]

        def copy(k, owner, to, src=None):
            sem = r0 + 7 * a + k
            return pltpu.make_async_remote_copy(
                src_ref=block(*owner) if src is None else src, dst_ref=block(*owner),
                send_sem=send_sems.at[sem], recv_sem=recv_sems.at[sem], device_id=to, device_id_type=MESH)

        me = (x, y, c)
        first = [copy(0, me, sibling, src=in_refs[a])]
        first += [copy(1 + j, me, (*chip, c), src=in_refs[a]) for j, chip in enumerate(chips)]
        passed = [copy(4 + j, (*chip, c), sibling) for j, chip in enumerate(chips)]
        landed = [copy(1 + j, (*chip, c), me) for j, chip in enumerate(chips)]
        from_sibling = [copy(0, sibling, me)] + [copy(4 + j, (*chip, 1 - c), me) for j, chip in enumerate(chips)]
        mine = pltpu.make_async_copy(in_refs[a], block(*me), local_sems.at[l0 + a])
        return first, passed, landed, from_sibling, mine

    def start(*refs):
        for a in range(n):
            first, _, _, _, mine = copies(a, *refs)
            mine.start()
            for cp in first:
                cp.start()

    def pass_on(*refs):
        for a in range(n):
            _, passed, landed, _, _ = copies(a, *refs)
            for cp, fwd in zip(landed, passed):
                cp.wait_recv()
                fwd.start()

    def finish(*refs):
        for a in range(n):
            first, passed, _, from_sibling, mine = copies(a, *refs)
            for cp in from_sibling:
                cp.wait_recv()
            for cp in first + passed:
                cp.wait_send()
            mine.wait()

    return _CommPlan(list(shards), outs, 7 * n, n, [start, pass_on, finish])


def _exchange_plan(srcs, cols, n_slots, route):
    n = len(srcs)
    outs = []
    for g, cl in zip(srcs, cols):
        shard = (g.shape[0], g.shape[1], g.shape[2] // N_DEV) if cl else (g.shape[0],) + g.shape[2:]
        outs.append(SDS((n_slots,) + shard, g.dtype))

    def copies(in_refs, out_refs, send_sems, recv_sems, local_sems, r0=0, l0=0):
        x, y, c = _mesh_pos()
        made = []
        for a in range(n):
            for s in range(n_slots):
                chunk, target = route(x, y, c, s)
                src = _col_window(in_refs[a], chunk, outs[a].shape[3]) if cols[a] else in_refs[a].at[:, chunk]
                sem = r0 + a * n_slots + s
                made.append(pltpu.make_async_remote_copy(
                    src_ref=src, dst_ref=out_refs[a].at[s], send_sem=send_sems.at[sem], recv_sem=recv_sems.at[sem],
                    device_id=target, device_id_type=MESH))
        return made

    def start(*refs):
        for cp in copies(*refs):
            cp.start()

    def nothing(*refs):
        pass

    def finish(*refs):
        made = copies(*refs)
        for cp in made:
            cp.wait_recv()
        for cp in made:
            cp.wait_send()

    return _CommPlan(list(srcs), outs, n * n_slots, 0, [start, nothing, finish])


def _sibling_route(x, y, c, k):
    return 2 * k + 1 - c, (x, y, 1 - c)


def _chip_route(x, y, c, j):
    px, py = [(1 - x, y), (x, 1 - y), (1 - x, 1 - y)][j]
    return 2 * px + py, (px, py, c)


def _pair_sum(grads, recv4, c_idx, name, cols=False):
    _, p, r, cdim = recv4.shape
    tr = _tile(r, (512, 256, 176, 160, 128, 64, 32, 16))

    def body(c_ref, a_ref, b_ref, o_ref):
        o_ref[...] = (a_ref[...].astype(F32) + b_ref[...].astype(F32)).astype(o_ref.dtype)

    blk = (None, None, tr, cdim)
    if cols:
        own = pl.BlockSpec((None, tr, cdim), lambda k, q, i, c_ref: (q, i, 2 * k + c_ref[0]))
    else:
        own = pl.BlockSpec(blk, lambda k, q, i, c_ref: (q, 2 * k + c_ref[0], i, 0))
    grid_spec = pltpu.PrefetchScalarGridSpec(
        num_scalar_prefetch=1, grid=(4, p, r // tr),
        in_specs=[own, pl.BlockSpec(blk, lambda k, q, i, c_ref: (k, q, i, 0))],
        out_specs=pl.BlockSpec(blk, lambda k, q, i, c_ref: (q, k, i, 0)))
    return _pcall(body, name=name, grid_spec=grid_spec, out_shape=SDS((p, 4, r, cdim), grads.dtype),
                  compiler_params=_params(("parallel", "parallel", "parallel")))(c_idx, grads, recv4)


def _mm(name, ins, prods, n_acc, acc_shape, epi_idx, epilogue, out_shapes, out_specs, grid, dims, plan=None):
    n_in, n_out, nk = len(ins), len(out_shapes), grid[2]
    n_acc_refs = n_acc if nk > 1 else 0
    c_ins, c_outs = (plan.ins, plan.outs) if plan else ([], [])
    n_cin, n_cout = len(c_ins), len(c_outs)

    def body(*refs):
        in_refs, c_in = refs[:n_in], refs[n_in:n_in + n_cin]
        rest = refs[n_in + n_cin:]
        out_refs, c_out = rest[:n_out], rest[n_out:n_out + n_cout]
        rest = rest[n_out + n_cout:]
        acc_refs, sems = rest[:n_acc_refs], rest[n_acc_refs:]
        ids = [pl.program_id(axis) for axis in range(3)]
        if plan:
            @pl.when((ids[0] == 0) & (ids[1] == 0) & (ids[2] == 0))
            def _():
                plan.phases[0](c_in, c_out, *sems)

        def finish(accs):
            outs = epilogue(accs, [in_refs[i][...] for i in epi_idx])
            for o_ref, o in zip(out_refs, outs):
                if isinstance(o, tuple):
                    for plane, part in enumerate(o):
                        o_ref[plane] = part.astype(o_ref.dtype)
                else:
                    o_ref[...] = o.astype(o_ref.dtype)

        if nk == 1:
            accs = [None] * n_acc
            for ia, ib, iacc in prods:
                term = _dot(in_refs[ia][...], in_refs[ib][...], dims)
                accs[iacc] = term if accs[iacc] is None else accs[iacc] + term
            finish(accs)
        else:
            @pl.when(ids[2] == 0)
            def _():
                for acc in acc_refs:
                    acc[...] = jnp.zeros_like(acc)

            for ia, ib, iacc in prods:
                acc_refs[iacc][...] += _dot(in_refs[ia][...], in_refs[ib][...], dims)

            @pl.when(ids[2] == nk - 1)
            def _():
                finish([acc[...] for acc in acc_refs])

        if plan:
            @pl.when((ids[0] == grid[0] - 1) & (ids[1] == grid[1] - 1) & (ids[2] == nk - 1))
            def _():
                plan.phases[1](c_in, c_out, *sems)
                plan.phases[2](c_in, c_out, *sems)

    return _pcall(
        body, name=name, grid=grid, in_specs=[s for _, s in ins] + [ANY] * n_cin,
        out_specs=list(out_specs) + [ANY] * n_cout, out_shape=list(out_shapes) + list(c_outs),
        scratch_shapes=[pltpu.VMEM(acc_shape, F32) for _ in range(n_acc_refs)] + (plan.scratch() if plan else []),
        compiler_params=_params(("arbitrary",) * 3 if plan else ("parallel", "parallel", "arbitrary")),
    )(*[a for a, _ in ins], *c_ins)


def _plain(accs, _):
    return accs


def _mm_nn(name, a, b, out_dtype, extras=(), epilogue=_plain, n_out=1):
    m, kd = a.shape
    n = b.shape[1]
    tm, tn, tk = _tile(m, (1024, 512, 256, 128)), _tile(n, (640, 512, 256, 128)), _tile(kd, (1280, 1024, 512, 256, 128))
    ins = [(a, pl.BlockSpec((tm, tk), lambda i, j, k: (i, k))), (b, pl.BlockSpec((tk, tn), lambda i, j, k: (k, j)))]
    for arr, kind in extras:
        if kind == "tile":
            ins.append((arr, pl.BlockSpec((tm, tn), lambda i, j, k: (i, j))))
        else:
            ins.append((arr, pl.BlockSpec((1, tn), lambda i, j, k: (0, j))))
    dts = out_dtype if isinstance(out_dtype, (list, tuple)) else [out_dtype] * n_out
    return _mm(name, ins, [(0, 1, 0)], 1, (tm, tn), list(range(2, len(ins))), epilogue,
               [SDS((m, n), dt) for dt in dts], [pl.BlockSpec((tm, tn), lambda i, j, k: (i, j)) for _ in dts],
               (m // tm, n // tn, kd // tk), "nn")


def _mm_nt(name, a, b, out_dtype):
    m, kd = a.shape
    n = b.shape[0]
    tm, tn, tk = _tile(m, (1024, 512, 256, 128)), _tile(n, (640, 512, 256, 128)), _tile(kd, (1024, 512, 256, 128))
    ins = [(a, pl.BlockSpec((tm, tk), lambda i, j, k: (i, k))), (b, pl.BlockSpec((tn, tk), lambda i, j, k: (j, k)))]
    return _mm(name, ins, [(0, 1, 0)], 1, (tm, tn), [], _plain, [SDS((m, n), out_dtype)],
               [pl.BlockSpec((tm, tn), lambda i, j, k: (i, j))], (m // tm, n // tn, kd // tk), "nt")[0]


def _mm_tn(name, a, b, out_dtype):
    t, m = a.shape
    n = b.shape[1]
    tm, tn, tk = _tile(m, (640, 512, 256, 128)), _tile(n, (1024, 512, 256, 128)), t
    ins = [(a, pl.BlockSpec((tk, tm), lambda i, j, k: (k, i))), (b, pl.BlockSpec((tk, tn), lambda i, j, k: (k, j)))]
    return _mm(name, ins, [(0, 1, 0)], 1, (tm, tn), [], _plain, [SDS((m, n), out_dtype)],
               [pl.BlockSpec((tm, tn), lambda i, j, k: (i, j))], (m // tm, n // tn, t // tk), "tn")[0]


def _mm_nt_stack(name, a3, b, out_dtype):
    cc, m, kd = a3.shape
    n = b.shape[0]
    tm, tn, tk = _tile(m, (1024, 512, 256, 128)), _tile(n, (1024, 512, 256, 128)), _tile(kd, (1280, 1024, 512, 256, 128))
    nk = kd // tk
    ins = [(a3, pl.BlockSpec((None, tm, tk), lambda i, j, k: (k // nk, i, k % nk))),
           (b, pl.BlockSpec((tn, tk), lambda i, j, k: (j, k)))]
    return _mm(name, ins, [(0, 1, 0)], 1, (tm, tn), [], _plain, [SDS((m, n), out_dtype)],
               [pl.BlockSpec((tm, tn), lambda i, j, k: (i, j))], (m // tm, n // tn, cc * nk), "nt")[0]


def _mm_tn_stack(name, a, b3, out_dtype):
    t, m = a.shape
    cc, _, n = b3.shape
    tm, tn, tk = _tile(m, (512, 256, 128)), _tile(n, (1280, 1024, 512, 256, 128)), t
    nj = n // tn
    ins = [(a, pl.BlockSpec((tk, tm), lambda i, j, k: (k, i))),
           (b3, pl.BlockSpec((None, tk, tn), lambda i, j, k: (j // nj, k, j % nj)))]
    return _mm(name, ins, [(0, 1, 0)], 1, (tm, tn), [], _plain, [SDS((m, cc * n), out_dtype)],
               [pl.BlockSpec((tm, tn), lambda i, j, k: (i, j))], (m // tm, cc * nj, t // tk), "tn")[0]


def _chunks_to_cols(name, wc, eye2):
    nch, d, cw = wc.shape
    tm = _tile(d, (1024, 512, 256, 128))
    ins = [(wc, pl.BlockSpec((None, tm, cw), lambda i, j, k: (2 * j + k, i, 0))),
           (eye2, pl.BlockSpec((None, cw, 2 * cw), lambda i, j, k: (k, 0, 0)))]
    return _mm(name, ins, [(0, 1, 0)], 1, (tm, 2 * cw), [], _plain, [SDS((d, nch * cw), wc.dtype)],
               [pl.BlockSpec((tm, 2 * cw), lambda i, j, k: (i, j))], (d // tm, nch // 2, 2), "nn")[0]


def _cols_to_chunks(name, full, eye2):
    d, n = full.shape
    _, cw, _ = eye2.shape
    nch = n // cw
    tm = _tile(d, (1024, 512, 256, 128))
    ins = [(full, pl.BlockSpec((tm, 2 * cw), lambda i, j, k: (i, j // 2))),
           (eye2, pl.BlockSpec((None, cw, 2 * cw), lambda i, j, k: (j % 2, 0, 0)))]
    return _mm(name, ins, [(0, 1, 0)], 1, (tm, cw), [], _plain, [SDS((nch, d, cw), full.dtype)],
               [pl.BlockSpec((None, tm, cw), lambda i, j, k: (j, i, 0))], (d // tm, nch, 1), "nt")[0]


def _row_tile(t):
    return _tile(t, (256, 128, 64, 32, 16, 8))


def _norm_fwd(x, g, scale1p, shift, name):
    t, d = x.shape
    tr = _row_tile(t)

    def body(x_ref, g_ref, s_ref, b_ref, h_ref):
        xv = x_ref[...]
        inv = lax.rsqrt(jnp.mean(xv * xv, axis=-1, keepdims=True) + NORM_EPS)
        h_ref[...] = ((xv * inv) * g_ref[...] * s_ref[...] + b_ref[...]).astype(h_ref.dtype)

    vec = pl.BlockSpec((1, d), lambda i: (0, 0))
    return _pcall(body, name=name, grid=(t // tr,), in_specs=[pl.BlockSpec((tr, d), lambda i: (i, 0)), vec, vec, vec],
                  out_specs=pl.BlockSpec((tr, d), lambda i: (i, 0)), out_shape=SDS((t, d), BF16),
                  compiler_params=_params(("parallel",)))(x, g, scale1p, shift)


def _adaln_bwd(dh, x, y, dxo, g, scale1p, w_sub, gw_prev, name):
    t, d = x.shape
    tr = _row_tile(t)

    def body(dh_ref, x_ref, y_ref, dxo_ref, g_ref, s_ref, gw_ref, dx_ref, dyp_ref, sums_ref):
        i = pl.program_id(0)

        @pl.when(i == 0)
        def _():
            sums_ref[...] = jnp.zeros_like(sums_ref)

        xv, dhv, dxov = x_ref[...], dh_ref[...], dxo_ref[...]
        inv = lax.rsqrt(jnp.mean(xv * xv, axis=-1, keepdims=True) + NORM_EPS)
        xn = xv * inv
        gv = g_ref[...]
        dn = dhv * s_ref[...]
        dxn = dn * gv
        dx = inv * (dxn - xn * jnp.mean(dxn * xn, axis=-1, keepdims=True)) + dxov
        dx_ref[...] = dx
        dyp_ref[...] = (gw_ref[...] * dx).astype(dyp_ref.dtype)
        sums_ref[0:1, :] += jnp.sum(dhv, axis=0, keepdims=True)
        sums_ref[1:2, :] += jnp.sum(dhv * (xn * gv), axis=0, keepdims=True)
        sums_ref[2:3, :] += jnp.sum(w_sub * y_ref[...] * dxov, axis=0, keepdims=True)
        sums_ref[3:4, :] += jnp.sum(dn * xn, axis=0, keepdims=True)

    blk = pl.BlockSpec((tr, d), lambda i: (i, 0))
    vec = pl.BlockSpec((1, d), lambda i: (0, 0))
    return _pcall(
        body, name=name, grid=(t // tr,), in_specs=[blk, blk, blk, blk, vec, vec, vec],
        out_specs=[blk, blk, pl.BlockSpec((8, d), lambda i: (0, 0))],
        out_shape=[SDS((t, d), F32), SDS((t, d), BF16), SDS((8, d), F32)],
        compiler_params=_params(("arbitrary",)))(dh, x, y, dxo, g, scale1p, gw_prev)


def _loss_head(x, target, gf, gw_prev):
    t, d = x.shape
    tr = _row_tile(t)
    nt = t // tr

    def body(x_ref, tg_ref, g_ref, gw_ref, dx_ref, dyp_ref, sums_ref):
        i = pl.program_id(0)

        @pl.when(i == 0)
        def _():
            sums_ref[...] = jnp.zeros_like(sums_ref)

        xv = x_ref[...]
        inv = lax.rsqrt(jnp.mean(xv * xv, axis=-1, keepdims=True) + NORM_EPS)
        xn = xv * inv
        gv = g_ref[...]
        err = xn * gv - tg_ref[...]
        dyv = err * (1.0 / d)
        dxn = dyv * gv
        dx = inv * (dxn - xn * jnp.mean(dxn * xn, axis=-1, keepdims=True))
        dx_ref[...] = dx
        dyp_ref[...] = (gw_ref[...] * dx).astype(dyp_ref.dtype)
        sums_ref[0:1, :] += jnp.sum(dyv * xn, axis=0, keepdims=True)
        sums_ref[1:2, :] += jnp.sum(err * err, axis=0, keepdims=True)

        @pl.when(i == nt - 1)
        def _():
            tot = jnp.sum(sums_ref[1:2, :], axis=1, keepdims=True) * (0.5 / d)
            sums_ref[1:2, :] = jnp.broadcast_to(tot, (1, d))

    blk = pl.BlockSpec((tr, d), lambda i: (i, 0))
    vec = pl.BlockSpec((1, d), lambda i: (0, 0))
    return _pcall(
        body, name="loss_head", grid=(nt,), in_specs=[blk, blk, vec, vec],
        out_specs=[blk, blk, pl.BlockSpec((8, d), lambda i: (0, 0))],
        out_shape=[SDS((t, d), F32), SDS((t, d), BF16), SDS((8, d), F32)],
        compiler_params=_params(("arbitrary",)))(x, target, gf, gw_prev)


HIDDEN_CHUNKS = N_DEV // 2


def _ffn_tiles(t, d):
    return _tile(t, (1024, 512, 256, 128)), _tile(d, (1024, 512, 256, 128))


def _ffn_gu(name, h, wgu, plan=None):
    t, d = h.shape
    fc, nc = wgu.shape[3], HIDDEN_CHUNKS
    tm, _ = _ffn_tiles(t, d)

    def epi_gu(accs, _):
        gpre, up = accs
        return (gpre, up), gpre * jax.nn.sigmoid(gpre) * up

    wblk = (None, None, d, fc)
    ins = [(h, pl.BlockSpec((tm, d), lambda i, c, k: (i, 0))),
           (wgu, pl.BlockSpec(wblk, lambda i, c, k: (0, c, 0, 0))),
           (wgu, pl.BlockSpec(wblk, lambda i, c, k: (0, c + nc, 0, 0)))]
    return _mm(name, ins, [(0, 1, 0), (0, 2, 1)], 2, (tm, fc), [], epi_gu,
               [SDS((2, nc, t, fc), BF16), SDS((nc, t, fc), BF16)],
               [pl.BlockSpec((2, None, tm, fc), lambda i, c, k: (0, c, i, 0)),
                pl.BlockSpec((None, tm, fc), lambda i, c, k: (c, i, 0))],
               (t // tm, nc, 1), "nn", plan=plan)


def _ffn_down(name, a, wd4, x, gmul, plan=None):
    nc, t, fc = a.shape
    d = wd4.shape[3]
    tm, tn = _ffn_tiles(t, d)

    def epi_down(accs, ex):
        (yv,), (xv, gm) = accs, ex
        return yv, xv + MACARON_W * gm * yv

    ins = [(a, pl.BlockSpec((None, tm, fc), lambda i, j, k: (k, i, 0))),
           (wd4, pl.BlockSpec((None, None, fc, tn), lambda i, j, k: (0, k, 0, j))),
           (x, pl.BlockSpec((tm, tn), lambda i, j, k: (i, j))), (gmul, pl.BlockSpec((1, tn), lambda i, j, k: (0, j)))]
    oblk = pl.BlockSpec((tm, tn), lambda i, j, k: (i, j))
    return _mm(name, ins, [(0, 1, 0)], 1, (tm, tn), [2, 3], epi_down, [SDS((t, d), BF16), SDS((t, d), F32)],
               [oblk, oblk], (t // tm, d // tn, nc), "nn", plan=plan)


def _ffn_da(name, dy, wd4, gu2, plan=None):
    t, d = dy.shape
    _, nc, fc, _ = wd4.shape
    tm, _ = _ffn_tiles(t, d)

    def epi_da(accs, ex):
        (da,), (gu,) = accs, ex
        gpre, up = gu[0].astype(F32), gu[1].astype(F32)
        s = jax.nn.sigmoid(gpre)
        silu = gpre * s
        dg = da * up * (s * (1.0 + gpre * (1.0 - s)))
        return ((dg, da * silu),)

    gblk = pl.BlockSpec((2, None, tm, fc), lambda i, c, k: (0, c, i, 0))
    ins = [(dy, pl.BlockSpec((tm, d), lambda i, c, k: (i, 0))),
           (wd4, pl.BlockSpec((None, None, fc, d), lambda i, c, k: (0, c, 0, 0))), (gu2, gblk)]
    return _mm(name, ins, [(0, 1, 0)], 1, (tm, fc), [2], epi_da, [SDS((2, nc, t, fc), BF16)], [gblk],
               (t // tm, nc, 1), "nt", plan=plan)


def _ffn_dwd(name, a, dy, plan=None):
    nc, t, fc = a.shape
    d = dy.shape[1]
    _, tn = _ffn_tiles(t, d)
    ins = [(a, pl.BlockSpec((None, t, fc), lambda c, j, k: (c, 0, 0))), (dy, pl.BlockSpec((t, tn), lambda c, j, k: (0, j)))]
    return _mm(name, ins, [(0, 1, 0)], 1, (fc, tn), [], _plain, [SDS((1, nc, fc, d), BF16)],
               [pl.BlockSpec((None, None, fc, tn), lambda c, j, k: (0, c, 0, j))], (nc, d // tn, 1), "tn", plan=plan)


def _ffn_dwgu(name, h, dgu2, plan=None):
    t, d = h.shape
    _, nc, _, fc = dgu2.shape
    _, tn = _ffn_tiles(t, d)
    ins = [(h, pl.BlockSpec((t, tn), lambda i, c, k: (0, i))),
           (dgu2, pl.BlockSpec((None, None, t, fc), lambda i, c, k: (c // nc, c % nc, 0, 0)))]
    return _mm(name, ins, [(0, 1, 0)], 1, (tn, fc), [], _plain, [SDS((1, 2 * nc, d, fc), BF16)],
               [pl.BlockSpec((None, None, tn, fc), lambda i, c, k: (0, c, i, 0))], (d // tn, 2 * nc, 1), "tn", plan=plan)


def _ffn_dh(name, dgu2, wgu, plan=None):
    _, nc, t, fc = dgu2.shape
    d = wgu.shape[2]
    tm, tn = _ffn_tiles(t, d)
    ins = [(dgu2, pl.BlockSpec((None, None, tm, fc), lambda i, j, k: (k // nc, k % nc, i, 0))),
           (wgu, pl.BlockSpec((None, None, tn, fc), lambda i, j, k: (0, k, j, 0)))]
    return _mm(name, ins, [(0, 1, 0)], 1, (tm, tn), [], _plain, [SDS((t, d), F32)],
               [pl.BlockSpec((tm, tn), lambda i, j, k: (i, j))], (t // tm, d // tn, 2 * nc), "nt", plan=plan)


def _sb_block(t):
    return 256 if t >= 1024 else 128


SB_STRIP = 32


def _sb_strips(blk):
    strip = min(SB_STRIP, blk)
    row = lax.broadcasted_iota(jnp.int32, (strip, blk), 0)
    col = lax.broadcasted_iota(jnp.int32, (strip, blk), 1)
    return [(slice(r0, r0 + strip), col < row + r0) for r0 in range(0, blk, strip)]


def _host_call(core, name, steps, ins, in_specs, out_shapes, out_specs, scratch, plan):
    n_in, n_out, n_scr = len(ins), len(out_shapes), len(scratch)
    c_ins, c_outs = (plan.ins, plan.outs) if plan else ([], [])
    n_cin, n_cout = len(c_ins), len(c_outs)

    def body(*refs):
        in_refs, c_in = refs[:n_in], refs[n_in:n_in + n_cin]
        rest = refs[n_in + n_cin:]
        out_refs, c_out = rest[:n_out], rest[n_out:n_out + n_cout]
        rest = rest[n_out + n_cout:]
        scr, sems = rest[:n_scr], rest[n_scr:]
        step = pl.program_id(0)
        if plan:
            @pl.when(step == 0)
            def _():
                plan.phases[0](c_in, c_out, *sems)

        core(in_refs, out_refs, scr)
        if plan:
            @pl.when(step == steps - 1)
            def _():
                plan.phases[1](c_in, c_out, *sems)
                plan.phases[2](c_in, c_out, *sems)

    return _pcall(
        body, name=name, grid=(steps,), in_specs=list(in_specs) + [ANY] * n_cin,
        out_specs=list(out_specs) + [ANY] * n_cout, out_shape=list(out_shapes) + list(c_outs),
        scratch_shapes=list(scratch) + (plan.scratch() if plan else []),
        compiler_params=_params(("arbitrary",)))(*ins, *c_ins)


def _sb_fwd(qkv, d, plan=None):
    t = qkv.shape[0]
    blk = _sb_block(t)
    nq = t // blk
    npair = d // LANES
    scale = HEAD_DIM ** -0.5

    def body(in_refs, out_refs, scr):
        (q_ref, k_ref, v_ref), (o_ref, l_ref) = in_refs, out_refs
        tri_s, hi_s, lo_s, w_s, zs_s = scr
        lane = lax.broadcasted_iota(jnp.int32, (blk, LANES), 1)
        head0 = lane < HEAD_DIM
        row = lax.broadcasted_iota(jnp.int32, (blk, blk), 0)
        col = lax.broadcasted_iota(jnp.int32, (blk, blk), 1)
        tri_s[...] = (row > col).astype(BF16)
        strips = _sb_strips(blk)

        def both(qhs, kb, carries, masked):
            start = pl.multiple_of(kb * blk, blk)
            kv = k_ref[pl.ds(start, blk), :]
            vv = v_ref[pl.ds(start, blk), :]
            zs = [_dot(qh, kv, "nt") for qh in qhs]
            sums = []
            for hh in range(2):
                parts = []
                for rows, causal in strips:
                    zt = zs[hh][rows, :]
                    sp = _softplus(zt)
                    lk = jnp.where(causal, -sp, 0.0) if masked else -sp
                    hi = lk.astype(BF16)
                    hi_s[hh, rows, :] = hi
                    lo_s[hh, rows, :] = (lk - hi.astype(F32)).astype(BF16)
                    zs_s[hh, rows, :] = zt - sp
                    parts.append(jnp.sum(lk, axis=1, keepdims=True))
                sums.append(jnp.concatenate(parts, axis=0))
            laters = [_dot(hi_s[hh], tri_s[...], "nn") + _dot(lo_s[hh], tri_s[...], "nn") for hh in range(2)]
            for hh in range(2):
                cl = carries[hh][0]
                for rows, causal in strips:
                    logw = zs_s[hh, rows, :] + laters[hh][rows, :] + cl[rows, :]
                    if masked:
                        logw = jnp.where(causal, logw, -1e30)
                    w_s[hh, rows, :] = jnp.exp(logw).astype(BF16)
            return tuple((carries[hh][0] + sums[hh], carries[hh][1] + _dot(w_s[hh], vv, "nn")) for hh in range(2))

        def qblock(qi, _):
            qstart = pl.multiple_of(qi * blk, blk)
            qv = q_ref[pl.ds(qstart, blk), :] * scale
            qhs = [jnp.where(head0 if hh == 0 else ~head0, qv, jnp.zeros_like(qv)) for hh in range(2)]
            zero = (jnp.zeros((blk, 1), F32), jnp.zeros((blk, LANES), F32))
            outs = both(qhs, qi, (zero, zero), True)
            outs = lax.fori_loop(0, qi, lambda j, crs: both(qhs, qi - 1 - j, crs, False), outs)
            o_ref[pl.ds(qstart, blk), :] = jnp.where(head0, outs[0][1], outs[1][1]).astype(o_ref.dtype)
            l_ref[pl.ds(qstart, blk), :] = jnp.where(head0, outs[0][0], outs[1][0])
            return 0

        lax.fori_loop(0, nq, qblock, 0)

    pair_bf16 = pltpu.VMEM((2, blk, blk), BF16)
    return _host_call(
        body, "sb_fwd", npair, [qkv, qkv, qkv],
        [pl.BlockSpec((t, LANES), lambda p: (0, p)), pl.BlockSpec((t, LANES), lambda p: (0, npair + p)),
         pl.BlockSpec((t, LANES), lambda p: (0, 2 * npair + p))],
        [SDS((t, d), BF16), SDS((t, d), F32)],
        [pl.BlockSpec((t, LANES), lambda p: (0, p)), pl.BlockSpec((t, LANES), lambda p: (0, p))],
        [pltpu.VMEM((blk, blk), BF16), pair_bf16, pair_bf16, pair_bf16, pltpu.VMEM((2, blk, blk), F32)], plan)


def _sb_bwd(qkv, do, ltot, d, plan=None):
    t = qkv.shape[0]
    blk = _sb_block(t)
    nq = t // blk
    npair = d // LANES
    scale = HEAD_DIM ** -0.5

    def body(in_refs, out_refs, scr):
        (q_ref, k_ref, v_ref, do_ref, l_ref), (out_ref,) = in_refs, out_refs
        dq_s, dk_s, dv_s, upto_s, before_s, hi_s, lo_s, w_s, dab_s, dzs_s, zs_s, da_s = scr
        lane = lax.broadcasted_iota(jnp.int32, (blk, LANES), 1)
        head0 = lane < HEAD_DIM
        row = lax.broadcasted_iota(jnp.int32, (blk, blk), 0)
        col = lax.broadcasted_iota(jnp.int32, (blk, blk), 1)
        upto_s[...] = (row <= col).astype(BF16)
        before_s[...] = (row < col).astype(BF16)
        dk_s[...] = jnp.zeros_like(dk_s)
        dv_s[...] = jnp.zeros_like(dv_s)
        strips = _sb_strips(blk)

        def both(heads, kb, carries, masked):
            start = pl.multiple_of(kb * blk, blk)
            kv = k_ref[pl.ds(start, blk), :]
            vv = v_ref[pl.ds(start, blk), :]
            zs = [_dot(qh, kv, "nt") for qh, _, _ in heads]
            dws = [_dot(doh, vv, "nt") for _, doh, _ in heads]
            lk_sums, da_sums = [], []
            for hh in range(2):
                parts = []
                for rows, causal in strips:
                    zt = zs[hh][rows, :]
                    sp = _softplus(zt)
                    lk = jnp.where(causal, -sp, 0.0) if masked else -sp
                    hi = lk.astype(BF16)
                    hi_s[hh, rows, :] = hi
                    lo_s[hh, rows, :] = (lk - hi.astype(F32)).astype(BF16)
                    zs_s[hh, rows, :] = zt - sp
                    parts.append(jnp.sum(lk, axis=1, keepdims=True))
                lk_sums.append(jnp.concatenate(parts, axis=0))
            cums = [_dot(hi_s[hh], upto_s[...], "nn") + _dot(lo_s[hh], upto_s[...], "nn") for hh in range(2)]
            for hh in range(2):
                lt, plk = heads[hh][2], carries[hh][0]
                parts = []
                for rows, causal in strips:
                    logw = zs_s[hh, rows, :] + (lt[rows, :] - (plk[rows, :] + cums[hh][rows, :]))
                    if masked:
                        logw = jnp.where(causal, logw, -1e30)
                    w = jnp.exp(logw)
                    w_s[hh, rows, :] = w.astype(BF16)
                    da = dws[hh][rows, :] * w
                    da_s[hh, rows, :] = da
                    dab_s[hh, rows, :] = da.astype(BF16)
                    parts.append(jnp.sum(da, axis=1, keepdims=True))
                da_sums.append(jnp.concatenate(parts, axis=0))
            pres = [_dot(dab_s[hh], before_s[...], "nn") for hh in range(2)]
            for hh in range(2):
                pda = carries[hh][1]
                for rows, causal in strips:
                    sig = jnp.exp(zs_s[hh, rows, :])
                    da = da_s[hh, rows, :]
                    dz = da * (1.0 - sig) - sig * (pda[rows, :] + pres[hh][rows, :])
                    if masked:
                        dz = jnp.where(causal, dz, 0.0)
                    dzs_s[hh, rows, :] = dz.astype(BF16)
            new = []
            for hh in range(2):
                qh, doh, _ = heads[hh]
                dk_s[pl.ds(start, blk), :] += _dot(dzs_s[hh], qh, "tn")
                dv_s[pl.ds(start, blk), :] += _dot(w_s[hh], doh, "tn")
                new.append((carries[hh][0] + lk_sums[hh], carries[hh][1] + da_sums[hh],
                            carries[hh][2] + _dot(dzs_s[hh], kv, "nn")))
            return tuple(new)

        def qblock(qi, _):
            qstart = pl.multiple_of(qi * blk, blk)
            qv = q_ref[pl.ds(qstart, blk), :] * scale
            dov = do_ref[pl.ds(qstart, blk), :]
            lv = l_ref[pl.ds(qstart, blk), :]
            heads = []
            for hh in range(2):
                sel = head0 if hh == 0 else ~head0
                heads.append((jnp.where(sel, qv, jnp.zeros_like(qv)), jnp.where(sel, dov, jnp.zeros_like(dov)),
                              jnp.max(jnp.where(sel, lv, -jnp.inf), axis=1, keepdims=True)))
            zero = (jnp.zeros((blk, 1), F32), jnp.zeros((blk, 1), F32), jnp.zeros((blk, LANES), F32))
            carries = lax.fori_loop(0, qi, lambda kb, crs: both(heads, kb, crs, False), (zero, zero))
            carries = both(heads, qi, carries, True)
            dq_s[pl.ds(qstart, blk), :] = jnp.where(head0, carries[0][2], carries[1][2]) * scale
            return 0

        lax.fori_loop(0, nq, qblock, 0)
        out_ref[0] = dq_s[...].astype(out_ref.dtype)
        out_ref[1] = dk_s[...].astype(out_ref.dtype)
        out_ref[2] = dv_s[...].astype(out_ref.dtype)

    col_blk = lambda off: pl.BlockSpec((t, LANES), lambda p: (0, off + p))
    return _host_call(
        body, "sb_bwd", npair, [qkv, qkv, qkv, do, ltot],
        [col_blk(0), col_blk(npair), col_blk(2 * npair), col_blk(0), col_blk(0)],
        [SDS((3, t, d), BF16)], [pl.BlockSpec((3, t, LANES), lambda p: (0, 0, p))],
        [pltpu.VMEM((t, LANES), F32) for _ in range(3)] + [pltpu.VMEM((blk, blk), BF16) for _ in range(2)]
        + [pltpu.VMEM((2, blk, blk), BF16) for _ in range(5)] + [pltpu.VMEM((2, blk, blk), F32) for _ in range(2)], plan)


def _roll_rows(v, shift):
    return pltpu.roll(v, shift, 0)


def _shift_down(v, dist, fill, row):
    return jnp.where(row >= dist, _roll_rows(v, dist), fill)


def _shift_up(v, dist, fill, row):
    t = v.shape[0]
    return jnp.where(row < t - dist, _roll_rows(v, t - dist), fill)


def _lru_gates(xb, small, wr, wi, row):
    xs = [_shift_down(xb, 3 - tap, 0.0, row) if tap < 3 else xb for tap in range(4)]
    xc = small[4:5, :] + xs[0] * small[0:1, :]
    for tap in range(1, 4):
        xc = xc + xs[tap] * small[tap:tap + 1, :]
    xcb = xc.astype(BF16)
    r = jax.nn.sigmoid(_dot(xcb, wr, "nn") + small[5:6, :])
    ig = jax.nn.sigmoid(_dot(xcb, wi, "nn") + small[6:7, :])
    sp = _softplus(-small[7:8, :])
    la = -LRU_C * r * sp
    a = jnp.exp(la)
    th = jnp.tanh(la)
    mult = jnp.sqrt(-2.0 * th / (1.0 - th))
    return xs, xc, xcb, r, ig, sp, a, mult


def _gelu_parts(gate):
    inner = GELU_C * (gate + GELU_K * gate * gate * gate)
    th = jnp.tanh(inner)
    gelu = 0.5 * gate * (1.0 + th)
    dgelu = 0.5 * (1.0 + th) + 0.5 * gate * (1.0 - th * th) * GELU_C * (1.0 + 3.0 * GELU_K * gate * gate)
    return gelu, dgelu


def _scan_steps(t):
    steps, dist = [], 1
    while dist < t:
        steps.append(dist)
        dist *= 2
    return steps


SUBLANES = 8


def _linear_scan(a, b, scratch, row, reverse):
    a_s, b_s, carry_s = scratch
    t = a.shape[0]
    groups = t // SUBLANES
    in_group = row & (SUBLANES - 1)
    for dist in _scan_steps(SUBLANES):
        if reverse:
            inside = in_group < SUBLANES - dist
            b = b + a * jnp.where(inside, _roll_rows(b, t - dist), 0.0)
            a = a * jnp.where(inside, _roll_rows(a, t - dist), 1.0)
        else:
            inside = in_group >= dist
            b = a * jnp.where(inside, _roll_rows(b, dist), 0.0) + b
            a = a * jnp.where(inside, _roll_rows(a, dist), 1.0)
    a_s[...] = a
    b_s[...] = b
    end = 0 if reverse else SUBLANES - 1
    ends = pl.ds(end, groups, stride=SUBLANES)
    ae, be = a_s[ends, :], b_s[ends, :]
    grow = lax.broadcasted_iota(jnp.int32, ae.shape, 0)
    shift = _shift_up if reverse else _shift_down
    for dist in _scan_steps(groups):
        be = ae * shift(be, dist, 0.0, grow) + be
        ae = ae * shift(ae, dist, 1.0, grow)
    incoming = shift(be, 1, 0.0, grow)
    for k in range(SUBLANES):
        carry_s[pl.ds(k, groups, stride=SUBLANES), :] = incoming
    return a_s[...] * carry_s[...] + b_s[...]


def _lru_fwd(gx, small, wr, wi):
    t = gx.shape[0]
    r_dim = gx.shape[1] // 2
    nb = r_dim // LRU_BLOCK_W

    def body(gate_ref, xb_ref, small_ref, wr_ref, wi_ref, y_ref, hs_ref, *scratch):
        row = lax.broadcasted_iota(jnp.int32, (t, LRU_BLOCK_W), 0)
        xb = xb_ref[...]
        _, xc, _, _, ig, _, a, mult = _lru_gates(xb, small_ref, wr_ref[...], wi_ref[...], row)
        hsv = _linear_scan(a, mult * (ig * xc), scratch, row, reverse=False)
        hs_ref[...] = hsv
        gelu, _ = _gelu_parts(gate_ref[...])
        y_ref[...] = (gelu * hsv).astype(y_ref.dtype)

    colb = lambda off: pl.BlockSpec((t, LRU_BLOCK_W), lambda n: (0, off + n))
    wspec = pl.BlockSpec((None, LRU_BLOCK_W, LRU_BLOCK_W), lambda n: (n, 0, 0))
    return _pcall(
        body, name="lru_fwd", grid=(nb,),
        in_specs=[colb(0), colb(nb), pl.BlockSpec((8, LRU_BLOCK_W), lambda n: (0, n)), wspec, wspec],
        out_specs=[colb(0), colb(0)], out_shape=[SDS((t, r_dim), BF16), SDS((t, r_dim), F32)],
        scratch_shapes=[pltpu.VMEM((t, LRU_BLOCK_W), F32) for _ in range(3)],
        compiler_params=_params(("parallel",)))(gx, gx, small, wr, wi)


def _lru_bwd(gx, hs, dy, small, wr, wi):
    t = gx.shape[0]
    r_dim = gx.shape[1] // 2
    nb = r_dim // LRU_BLOCK_W

    def body(gate_ref, xb_ref, hs_ref, dy_ref, small_ref, wr_ref, wi_ref, dgx_ref, dsm_ref, dwr_ref, dwi_ref, *scratch):
        row = lax.broadcasted_iota(jnp.int32, (t, LRU_BLOCK_W), 0)
        xb, hsv, dyv, smallv = xb_ref[...], hs_ref[...], dy_ref[...], small_ref
        wrv, wiv = wr_ref[...], wi_ref[...]
        xs, xc, xcb, r, ig, sp, a, mult = _lru_gates(xb, smallv, wrv, wiv, row)
        gelu, dgelu = _gelu_parts(gate_ref[...])
        dgx_ref[0] = (dyv * hsv * dgelu).astype(dgx_ref.dtype)
        dacc = _linear_scan(_shift_up(a, 1, 1.0, row), dyv * gelu, scratch, row, reverse=True)
        da = dacc * _shift_down(hsv, 1, 0.0, row)
        dmult = dacc * (ig * xc)
        dixc = dacc * mult
        dla = da * a - dmult * (a * a) / mult
        dr = dla * (-LRU_C * sp)
        dsp = jnp.sum(dla * (-LRU_C * r), axis=0, keepdims=True)
        dpr = dr * r * (1.0 - r)
        dpi = dixc * xc * ig * (1.0 - ig)
        dprb, dpib = dpr.astype(BF16), dpi.astype(BF16)
        dwr_ref[...] = _dot(xcb, dprb, "tn")
        dwi_ref[...] = _dot(xcb, dpib, "tn")
        dxc = dixc * ig + _dot(dprb, wrv, "nt") + _dot(dpib, wiv, "nt")
        dxb = dxc * smallv[3:4, :]
        for tap in range(3):
            dxb = dxb + _shift_up(dxc, 3 - tap, 0.0, row) * smallv[tap:tap + 1, :]
        dgx_ref[1] = dxb.astype(dgx_ref.dtype)
        lam = smallv[7:8, :]
        rows = [jnp.sum(dxc * xs[tap], axis=0, keepdims=True) for tap in range(4)]
        rows.append(jnp.sum(dxc, axis=0, keepdims=True))
        rows.append(jnp.sum(dpr, axis=0, keepdims=True))
        rows.append(jnp.sum(dpi, axis=0, keepdims=True))
        rows.append(-dsp * jax.nn.sigmoid(-lam))
        for k, rv in enumerate(rows):
            dsm_ref[k:k + 1, :] = rv

    colb = lambda off: pl.BlockSpec((t, LRU_BLOCK_W), lambda n: (0, off + n))
    wspec = pl.BlockSpec((None, LRU_BLOCK_W, LRU_BLOCK_W), lambda n: (n, 0, 0))
    sspec = pl.BlockSpec((8, LRU_BLOCK_W), lambda n: (0, n))
    return _pcall(
        body, name="lru_bwd", grid=(nb,),
        in_specs=[colb(0), colb(nb), colb(0), colb(0), sspec, wspec, wspec],
        out_specs=[pl.BlockSpec((2, t, LRU_BLOCK_W), lambda n: (0, 0, n)), sspec, wspec, wspec],
        out_shape=[SDS((2, t, r_dim), BF16), SDS((8, r_dim), F32), SDS((nb, LRU_BLOCK_W, LRU_BLOCK_W), F32),
                   SDS((nb, LRU_BLOCK_W, LRU_BLOCK_W), F32)],
        scratch_shapes=[pltpu.VMEM((t, LRU_BLOCK_W), F32) for _ in range(3)],
        compiler_params=_params(("parallel",)))(gx, gx, hs, dy, small, wr, wi)


def _adam(w, g, m, v):
    m2 = ADAM_B1 * m + (1.0 - ADAM_B1) * g
    v2 = ADAM_B2 * v + (1.0 - ADAM_B2) * (g * g)
    m_hat = m2 / (1.0 - ADAM_B1 ** ADAM_STEP)
    v_hat = v2 / (1.0 - ADAM_B2 ** ADAM_STEP)
    return -ADAM_LR * (m_hat / (jnp.sqrt(v_hat) + ADAM_EPS) + ADAM_WD * w), m2, v2


def _mod_fwd(c_all, mod_w, mod_b_cols):
    nl, d, cols = mod_w.shape
    nbatch = c_all.shape[0]

    def body(c_ref, w_ref, b_ref, o_ref):
        cv = c_ref[...]
        ca = (cv * jax.nn.sigmoid(cv)).astype(BF16)
        o_ref[...] = _dot(ca, w_ref[...].astype(BF16), "nn") + b_ref[...]

    return _pcall(
        body, name="mod_fwd", grid=(nl,),
        in_specs=[pl.BlockSpec((nbatch, d), lambda l: (0, 0)), pl.BlockSpec((None, d, cols), lambda l: (l, 0, 0)),
                  pl.BlockSpec((None, 1, cols), lambda l: (l, 0, 0))],
        out_specs=pl.BlockSpec((None, nbatch, cols), lambda l: (l, 0, 0)), out_shape=SDS((nl, nbatch, cols), F32),
        compiler_params=_params(("parallel",)))(c_all, mod_w, mod_b_cols)


def _mod_w_update(c_all, dmod_cols, w, m, v):
    nl, d, cols = w.shape
    nbatch = c_all.shape[0]
    tr = _tile(d, (256, 128))

    def body(c_ref, dm_ref, w_ref, m_ref, v_ref, g_ref, dl_ref, m2_ref, v2_ref):
        cv = c_ref[...]
        ca = (cv * jax.nn.sigmoid(cv)).astype(BF16)
        g = _dot(ca, dm_ref[...].astype(BF16), "tn")
        g_ref[...] = g
        dl_ref[...], m2_ref[...], v2_ref[...] = _adam(w_ref[...], g, m_ref[...], v_ref[...])

    wblk = pl.BlockSpec((None, tr, cols), lambda l, i: (l, i, 0))
    return _pcall(
        body, name="mod_w_update", grid=(nl, d // tr),
        in_specs=[pl.BlockSpec((nbatch, tr), lambda l, i: (0, i)), pl.BlockSpec((None, nbatch, cols), lambda l, i: (l, 0, 0)),
                  wblk, wblk, wblk],
        out_specs=[wblk] * 4, out_shape=[SDS(w.shape, F32)] * 4,
        compiler_params=_params(("parallel", "parallel")))(c_all, dmod_cols, w, m, v)


def _adam_update(name, w, m, v, gparts):
    rows, cols = w.shape
    tr = _tile(rows, (256, 128, 64, 32, 16, 8))
    npart = len(gparts)

    def body(*refs):
        w_ref, m_ref, v_ref = refs[:3]
        g_refs = refs[3:3 + npart]
        g_ref, dl_ref, m2_ref, v2_ref = refs[3 + npart:]
        g = g_refs[0][...].astype(F32)
        for gr in g_refs[1:]:
            g = g + gr[...].astype(F32)
        g_ref[...] = g
        dl_ref[...], m2_ref[...], v2_ref[...] = _adam(w_ref[...], g, m_ref[...], v_ref[...])

    blk = pl.BlockSpec((tr, cols), lambda i: (i, 0))
    return _pcall(body, name=name, grid=(rows // tr,), in_specs=[blk] * (3 + npart), out_specs=[blk] * 4,
                  out_shape=[SDS((rows, cols), F32)] * 4, compiler_params=_params(("parallel",)))(w, m, v, *gparts)


def _adam_shard(name, w, m, v, part4, recv3, chip_idx, first=0, fills=None):
    p, r, cdim = w.shape
    pg = part4.shape[0]
    tr = _tile(r, (256, 176, 160, 128, 64, 32, 16))

    def body(chip_ref, w_ref, m_ref, v_ref, own_ref, r0_ref, r1_ref, r2_ref, *rest):
        g_ref, dl_ref, m2_ref, v2_ref = rest[-4:]
        g = own_ref[...].astype(F32) + r0_ref[...].astype(F32) + r1_ref[...].astype(F32) + r2_ref[...].astype(F32)
        g_ref[...] = g
        dl_ref[...], m2_ref[...], v2_ref[...] = _adam(w_ref[...], g, m_ref[...], v_ref[...])

    blk = pl.BlockSpec((None, tr, cdim), lambda q, i, chip_ref: (first + q, i, 0))
    blk4 = (None, None, tr, cdim)
    slot = lambda s: pl.BlockSpec(blk4, lambda q, i, chip_ref: (s, q, i, 0))
    fills = list(fills or [])
    grid_spec = pltpu.PrefetchScalarGridSpec(
        num_scalar_prefetch=1, grid=(pg, r // tr),
        in_specs=[blk, blk, blk, pl.BlockSpec(blk4, lambda q, i, chip_ref: (q, chip_ref[0], i, 0)), slot(0), slot(1), slot(2)]
        + [ANY] * len(fills),
        out_specs=[blk] * 4)
    return _pcall(body, name=name, grid_spec=grid_spec, out_shape=[SDS((p, r, cdim), F32)] * 4,
                  input_output_aliases={8 + k: k for k in range(len(fills))},
                  compiler_params=_params(("parallel", "parallel")))(chip_idx, w, m, v, part4, recv3, recv3, recv3, *fills)


def _sum_devices(gathered, name):
    _, rows, cols = gathered.shape
    tr = _tile(rows, (512, 256, 128, 64, 32, 16, 8))

    def body(g_ref, o_ref):
        acc = g_ref[0].astype(F32)
        for k in range(1, N_DEV):
            acc = acc + g_ref[k].astype(F32)
        o_ref[...] = acc

    return _pcall(body, name=name, grid=(rows // tr,), in_specs=[pl.BlockSpec((N_DEV, tr, cols), lambda i: (0, i, 0))],
                  out_specs=pl.BlockSpec((tr, cols), lambda i: (i, 0)), out_shape=SDS((rows, cols), F32),
                  compiler_params=_params(("parallel",)))(gathered)


def _pack_flat(parts, width, row_mult, dtype):
    flat = jnp.concatenate([p.reshape(-1).astype(dtype) for p in parts])
    unit = width * row_mult
    pad = (-flat.shape[0]) % unit
    if pad:
        flat = jnp.concatenate([flat, jnp.zeros((pad,), dtype)])
    return flat.reshape(-1, width)


def _unpack_flat(flat, shapes):
    out, off = [], 0
    for shp in shapes:
        size = math.prod(shp)
        out.append(flat[off:off + size].reshape(shp))
        off += size
    return out


def kernel(x, c, mod_w, mod_b, norm_g, ffn_w_gu, ffn_w_down, sb_w_qkv, sb_w_o, lru_w_in, lru_conv_w, lru_conv_b, lru_w_r, lru_b_r, lru_w_i, lru_b_i, lru_lambda, lru_w_out, final_norm_g, loss_target, m_mod_w, m_mod_b, m_norm_g, m_ffn_w_gu, m_ffn_w_down, m_sb_w_qkv, m_sb_w_o, m_lru_w_in, m_lru_conv_w, m_lru_conv_b, m_lru_w_r, m_lru_b_r, m_lru_w_i, m_lru_b_i, m_lru_lambda, m_lru_w_out, m_final_norm_g, v_mod_w, v_mod_b, v_norm_g, v_ffn_w_gu, v_ffn_w_down, v_sb_w_qkv, v_sb_w_o, v_lru_w_in, v_lru_conv_w, v_lru_conv_b, v_lru_w_r, v_lru_b_r, v_lru_w_i, v_lru_b_i, v_lru_lambda, v_lru_w_out, v_final_norm_g):
    weights = dict(mod_w=mod_w, mod_b=mod_b, norm_g=norm_g, ffn_w_gu=ffn_w_gu, ffn_w_down=ffn_w_down, sb_w_qkv=sb_w_qkv,
                   sb_w_o=sb_w_o, lru_w_in=lru_w_in, lru_conv_w=lru_conv_w, lru_conv_b=lru_conv_b, lru_w_r=lru_w_r,
                   lru_b_r=lru_b_r, lru_w_i=lru_w_i, lru_b_i=lru_b_i, lru_lambda=lru_lambda, lru_w_out=lru_w_out,
                   final_norm_g=final_norm_g)
    mom_m = dict(mod_w=m_mod_w, mod_b=m_mod_b, norm_g=m_norm_g, ffn_w_gu=m_ffn_w_gu, ffn_w_down=m_ffn_w_down,
                 sb_w_qkv=m_sb_w_qkv, sb_w_o=m_sb_w_o, lru_w_in=m_lru_w_in, lru_conv_w=m_lru_conv_w,
                 lru_conv_b=m_lru_conv_b, lru_w_r=m_lru_w_r, lru_b_r=m_lru_b_r, lru_w_i=m_lru_w_i, lru_b_i=m_lru_b_i,
                 lru_lambda=m_lru_lambda, lru_w_out=m_lru_w_out, final_norm_g=m_final_norm_g)
    mom_v = dict(mod_w=v_mod_w, mod_b=v_mod_b, norm_g=v_norm_g, ffn_w_gu=v_ffn_w_gu, ffn_w_down=v_ffn_w_down,
                 sb_w_qkv=v_sb_w_qkv, sb_w_o=v_sb_w_o, lru_w_in=v_lru_w_in, lru_conv_w=v_lru_conv_w,
                 lru_conv_b=v_lru_conv_b, lru_w_r=v_lru_w_r, lru_b_r=v_lru_b_r, lru_w_i=v_lru_w_i, lru_b_i=v_lru_b_i,
                 lru_lambda=v_lru_lambda, lru_w_out=v_lru_w_out, final_norm_g=v_final_norm_g)
    names = list(weights)

    t, d = x.shape[1], x.shape[2]
    n_layers = mod_w.shape[0]
    r_dim = lru_w_out.shape[1] * N_DEV
    ng, rs = d // N_DEV, r_dim // N_DEV
    mod_cols = mod_w.shape[2]
    nblk = lru_w_r.shape[1]
    xi, yi, ci = _mesh_pos()
    me = 4 * xi + 2 * yi + ci
    chip = 2 * xi + yi
    x2, target = x.reshape(t, d), loss_target.reshape(t, d)

    lru_small_shard = jnp.concatenate([lru_conv_w[0], lru_conv_b, lru_b_r, lru_b_i, lru_lambda], axis=0)
    small1 = _pack_flat([c, norm_g, lru_small_shard], LANES, 8, F32)
    n_small1 = small1.shape[0]
    all1 = _allgather(small1[None], "gather_small").reshape(N_DEV, n_small1 * LANES)
    c_all = all1[:, :d]
    norm_full = jnp.transpose(all1[:, d:d + 6 * ng].reshape(N_DEV, n_layers, 3, ng), (1, 2, 0, 3)).reshape(n_layers, 3, d)
    lru_small = jnp.transpose(all1[:, d + 6 * ng:d + 6 * ng + 8 * rs].reshape(N_DEV, 8, rs), (1, 0, 2)).reshape(8, r_dim)

    mod_b_cols = lax.dynamic_slice_in_dim(mod_b, me * mod_cols, mod_cols, axis=1).reshape(n_layers, 1, mod_cols)
    mod_part = _mod_fwd(c_all, mod_w, mod_b_cols)

    assert sb_w_qkv.shape[0] == 1 and lru_w_in.shape[0] == 1, "one stick-breaking and one RG-LRU layer"
    n_ffn = 2 * n_layers
    fc = ffn_w_gu.shape[3]
    cw_in = lru_w_in.shape[2]
    pieces = {("ffn_w_gu", q): ffn_w_gu[q // 2, q % 2][None] for q in range(n_ffn)}
    pieces.update({("ffn_w_down", q): ffn_w_down[q // 2, q % 2][None] for q in range(n_ffn)})
    pieces.update({("sb_w_qkv", 0): sb_w_qkv, ("sb_w_o", 0): sb_w_o, ("lru_w_in", 0): lru_w_in, ("lru_w_out", 0): lru_w_out})
    col_window = {("sb_w_qkv", 0)}
    first = [("ffn_w_gu", 0)]
    behind = {"l0s0_gu": [("ffn_w_down", 0)], "l0s0_down": [("sb_w_qkv", 0), ("sb_w_o", 0)],
              "l0s2_gu": [("ffn_w_down", n_ffn - 1)]}
    behind["sb_fwd"] = [key for key in pieces if key not in first + sum(behind.values(), [])]
    gathered = {}

    def gather_plan(keys):
        return _gather_plan([pieces[key].astype(BF16) for key in keys], [key in col_window for key in keys])

    def hosting(name, call):
        keys = behind.get(name, [])
        outs = call(gather_plan(keys) if keys else None)
        gathered.update(zip(keys, outs[len(outs) - len(keys):]))
        return outs[:len(outs) - len(keys)]

    mod_all, *landed = _run_comm(_merge_plans([_gather_plan([mod_part], [False]), gather_plan(first)]), "gather_mod_and_first")
    gathered.update(zip(first, landed))
    mod_mine = lax.dynamic_index_in_dim(mod_all, me, axis=2, keepdims=False)
    mod_mine = mod_mine.reshape(n_layers, 3, 3, d)
    wr_b, wi_b = lru_w_r[0].astype(BF16), lru_w_i[0].astype(BF16)
    eye2 = jnp.eye(2 * cw_in, dtype=BF16).reshape(2, cw_in, 2 * cw_in)

    def w_gu(q):
        return gathered[("ffn_w_gu", q)]

    def w_d4(q):
        return gathered[("ffn_w_down", q)].reshape(1, HIDDEN_CHUNKS, fc, d)

    saved = []
    xcur = x2
    for layer in range(n_layers):
        for sub in range(3):
            gvec = norm_full[layer, sub].reshape(1, d)
            shift = mod_mine[layer, sub, 0].reshape(1, d)
            scale1p = 1.0 + mod_mine[layer, sub, 1].reshape(1, d)
            gmul = 1.0 + mod_mine[layer, sub, 2].reshape(1, d)
            tag = f"l{layer}s{sub}"
            h = _norm_fwd(xcur, gvec, scale1p, shift, tag + "_norm")
            rec = dict(x=xcur, h=h, g=gvec, scale1p=scale1p, gmul=gmul, w=MACARON_W if sub != 1 else 1.0)
            if sub != 1:
                lj = layer * 2 + sub // 2
                gu2, a = hosting(tag + "_gu", lambda plan: _ffn_gu(tag + "_gu", h, w_gu(lj), plan))
                yv, xcur = hosting(tag + "_down", lambda plan: _ffn_down(tag + "_down", a, w_d4(lj), xcur, gmul, plan))
                rec.update(kind="ffn", lj=lj, gu2=gu2, a=a, y=yv)
            elif layer % 2 == 0:
                w_qkv = gathered[("sb_w_qkv", 0)][0]
                w_o = gathered[("sb_w_o", 0)].reshape(d, d)
                qkv = _mm_nn(tag + "_qkv", h, w_qkv, BF16)[0]
                o, ltot = hosting("sb_fwd", lambda plan: _sb_fwd(qkv, d, plan))
                yv, xcur = _mm_nn(tag + "_wo", o, w_o, [BF16, F32], extras=[(xcur, "tile"), (gmul, "row")],
                                  epilogue=lambda accs, ex: (accs[0], ex[0] + ex[1] * accs[0]))
                rec.update(kind="sb", qkv=qkv, o=o, ltot=ltot, y=yv, w_qkv=w_qkv, w_o=w_o)
            else:
                w_in = _chunks_to_cols("lru_w_in_cols", gathered[("lru_w_in", 0)][0], eye2)
                w_out = gathered[("lru_w_out", 0)].reshape(r_dim, d)
                gx = _mm_nn(tag + "_win", h, w_in, F32)[0]
                ymix, hs = _lru_fwd(gx, lru_small, wr_b, wi_b)
                yv, xcur = _mm_nn(tag + "_wout", ymix, w_out, [BF16, F32], extras=[(xcur, "tile"), (gmul, "row")],
                                  epilogue=lambda accs, ex: (accs[0], ex[0] + ex[1] * accs[0]))
                rec.update(kind="lru", gx=gx, hs=hs, ymix=ymix, y=yv, w_in=w_in, w_out=w_out)
            saved.append(rec)

    last = saved[-1]
    dxo, dy, head_sums = _loss_head(xcur, target, final_norm_g.reshape(1, d), (last["w"] * last["gmul"]))
    loss = lax.psum(head_sums[1, 0], ("x", "y", "c"))
    dgf = head_sums[0]

    c_idx = jnp.reshape(ci, (1,)).astype(jnp.int32)
    chip_idx = jnp.reshape(chip, (1,)).astype(jnp.int32)
    grads, reduced = {}, {}
    to_pair = []
    to_chips = []

    def sibling_plan():
        keys = list(to_pair)
        if not keys:
            return None, keys
        return _exchange_plan([grads[key] for key in keys], [key in col_window for key in keys], 4, _sibling_route), keys

    def sibling_done(keys, recv4):
        for key, r4 in zip(keys, recv4):
            to_pair.remove(key)
            to_chips.append((key, _pair_sum(grads[key], r4, c_idx, f"rs_pair_sum_{key[0]}{key[1]}", cols=key in col_window)))

    def chip_plan():
        items = list(to_chips)
        if not items:
            return None, items
        return _exchange_plan([p4 for _, p4 in items], [False] * len(items), 3, _chip_route), items

    def chips_done(items, recv3):
        for item, r3 in zip(items, recv3):
            to_chips.remove(item)
            reduced[item[0]] = (item[1], r3)

    def behind(call, make_plan, done, more=None):
        plan, items = make_plan()
        n_mine = len(plan.outs) if plan else 0
        n_more = len(more.outs) if more else 0
        outs = call(_merge_plans([plan, more]))
        n_own = len(outs) - n_mine - n_more
        done(items, outs[n_own:n_own + n_mine])
        return list(outs[:n_own]) + list(outs[n_own + n_mine:])

    def at_once(make_plan, done, name):
        plan, items = make_plan()
        if plan:
            done(items, _run_comm(plan, name))

    def add_grad(key, value):
        grads[key] = value
        to_pair.append(key)

    dmod = [[None] * 3 for _ in range(n_layers)]
    dnorm = [[None] * 3 for _ in range(n_layers)]
    dlru_small = wri_all = None
    for idx in reversed(range(len(saved))):
        rec = saved[idx]
        layer, sub = divmod(idx, 3)
        tag = f"l{layer}s{sub}b"
        if rec["kind"] == "ffn" and idx > 0:
            lj = rec["lj"]
            (dgu2,) = behind(lambda plan: _ffn_da(tag + "_da", dy, w_d4(lj), rec["gu2"], plan), sibling_plan, sibling_done)
            add_grad(("ffn_w_down", lj), _ffn_dwd(tag + "_dwd", rec["a"], dy)[0].reshape(gathered[("ffn_w_down", lj)].shape))
            add_grad(("ffn_w_gu", lj), _ffn_dwgu(tag + "_dwgu", rec["h"], dgu2)[0])
            dh = _ffn_dh(tag + "_dh", dgu2, w_gu(lj))[0]
        elif rec["kind"] == "ffn":
            lj = rec["lj"]
            at_once(sibling_plan, sibling_done, "rs_sibling_" + tag)
            (dgu2,) = behind(lambda plan: _ffn_da(tag + "_da", dy, w_d4(lj), rec["gu2"], plan), chip_plan, chips_done)
            add_grad(("ffn_w_down", lj), _ffn_dwd(tag + "_dwd", rec["a"], dy)[0].reshape(gathered[("ffn_w_down", lj)].shape))
            at_once(sibling_plan, sibling_done, "rs_sibling_" + tag + "_dwd")
            (dwgu,) = behind(lambda plan: _ffn_dwgu(tag + "_dwgu", rec["h"], dgu2, plan), chip_plan, chips_done)
            add_grad(("ffn_w_gu", lj), dwgu)
            at_once(sibling_plan, sibling_done, "rs_sibling_" + tag + "_dwgu")
            (dh,) = behind(lambda plan: _ffn_dh(tag + "_dh", dgu2, w_gu(lj), plan), chip_plan, chips_done)
        elif rec["kind"] == "sb":
            at_once(sibling_plan, sibling_done, "rs_sibling_" + tag)
            do = _mm_nt(tag + "_do", dy, rec["w_o"], BF16)
            dwo = _mm_tn(tag + "_dwo", rec["o"], dy, BF16)
            wri = _pack_flat([dwr, dwi], LANES, 512, BF16)[None]
            dqkv3, wri_all = behind(lambda plan: _sb_bwd(rec["qkv"], do, rec["ltot"], d, plan), chip_plan, chips_done,
                                    more=_gather_plan([wri], [False]))
            add_grad(("sb_w_o", 0), dwo.reshape(gathered[("sb_w_o", 0)].shape))
            dh = _mm_nt_stack(tag + "_dh", dqkv3, rec["w_qkv"], F32)
            add_grad(("sb_w_qkv", 0), _mm_tn_stack(tag + "_dwqkv", rec["h"], dqkv3, BF16)[None])
        else:
            dymix = _mm_nt(tag + "_dymix", dy, rec["w_out"], F32)
            add_grad(("lru_w_out", 0), _mm_tn(tag + "_dwout", rec["ymix"], dy, BF16).reshape(gathered[("lru_w_out", 0)].shape))
            dgx2, dlru_small, dwr, dwi = _lru_bwd(rec["gx"], rec["hs"], dymix, lru_small, wr_b, wi_b)
            dh = _mm_nt_stack(tag + "_dh", dgx2, rec["w_in"], F32)
            dw_in = _mm_tn_stack(tag + "_dwin", rec["h"], dgx2, BF16)
            add_grad(("lru_w_in", 0), _cols_to_chunks("lru_w_in_chunks", dw_in, eye2)[None])
        prev = saved[idx - 1] if idx > 0 else None
        gw_prev = (prev["w"] * prev["gmul"]) if prev is not None else jnp.zeros((1, d), F32)
        dxo, dy, sums = _adaln_bwd(dh, rec["x"], rec["y"], dxo, rec["g"], rec["scale1p"], rec["w"], gw_prev, tag + "_adaln")
        dmod[layer][sub] = sums[0:3]
        dnorm[layer][sub] = sums[3]
    grad_x = dxo.reshape(x.shape)

    dmod_mine = jnp.stack([jnp.stack(dmod[layer]) for layer in range(n_layers)])
    dnorm_mine = jnp.stack([jnp.stack(dnorm[layer]) for layer in range(n_layers)])
    assert not to_pair and not to_chips
    small_shapes = [(n_layers, 9 * d), (n_layers, 3, d), (8, r_dim), (d,)]
    small3 = _pack_flat([dmod_mine, dnorm_mine, dlru_small, dgf], LANES, 256, F32)
    n_small3 = small3.shape[0]
    all3 = _allgather(small3[None], "gather_small_grads").reshape(N_DEV, n_small3, LANES)
    gsum = _sum_devices(all3, "sum_small_grads").reshape(-1)
    g_mod_b, g_norm_full, g_lru_small, g_final = _unpack_flat(gsum, small_shapes)
    wri_sum = _sum_devices(wri_all.reshape(N_DEV, -1, LANES), "sum_gate_weight_grads").reshape(-1)
    g_wr, g_wi = _unpack_flat(wri_sum, [lru_w_r.shape, lru_w_i.shape])
    dmod_all = all3.reshape(N_DEV, -1)[:, :n_layers * 9 * d].reshape(N_DEV, n_layers, N_DEV, mod_cols)
    dmod_cols = jnp.transpose(lax.dynamic_index_in_dim(dmod_all, me, axis=2, keepdims=False), (1, 0, 2))

    out_g, out_d, out_m, out_v = {}, {}, {}, {}
    out_g["mod_w"], out_d["mod_w"], out_m["mod_w"], out_v["mod_w"] = _mod_w_update(c_all, dmod_cols, mod_w, m_mod_w, v_mod_w)

    g_norm_shard = lax.dynamic_slice_in_dim(g_norm_full, me * ng, ng, axis=2)
    g_lru_shard = lax.dynamic_slice_in_dim(g_lru_small, me * rs, rs, axis=1)
    small_grads = dict(mod_b=g_mod_b, norm_g=g_norm_shard, lru_conv_w=g_lru_shard[0:4].reshape(lru_conv_w.shape),
                       lru_conv_b=g_lru_shard[4:5], lru_b_r=g_lru_shard[5:6], lru_b_i=g_lru_shard[6:7],
                       lru_lambda=g_lru_shard[7:8], final_norm_g=g_final)
    for n, g in (("lru_w_r", g_wr), ("lru_w_i", g_wi)):
        view = lambda arr: arr.reshape(-1, LRU_BLOCK_W)
        outs = _adam_update("adam_" + n, view(weights[n]), view(mom_m[n]), view(mom_v[n]), [view(g)])
        out_g[n], out_d[n], out_m[n], out_v[n] = [o.reshape(weights[n].shape) for o in outs]
    small_names = list(small_grads)
    sw = _pack_flat([weights[n] for n in small_names], LANES, 256, F32)
    sg = _pack_flat([small_grads[n] for n in small_names], LANES, 256, F32)
    sm = _pack_flat([mom_m[n] for n in small_names], LANES, 256, F32)
    sv = _pack_flat([mom_v[n] for n in small_names], LANES, 256, F32)
    s_outs = _adam_update("adam_small", sw, sm, sv, [sg])
    small_shapes2 = [weights[n].shape for n in small_names]
    for dst, flat in zip((out_g, out_d, out_m, out_v), s_outs):
        for n, arr in zip(small_names, _unpack_flat(flat.reshape(-1), small_shapes2)):
            dst[n] = arr

    for n in ["ffn_w_gu", "ffn_w_down", "sb_w_qkv", "sb_w_o", "lru_w_in", "lru_w_out"]:
        shp = weights[n].shape
        shard3 = (math.prod(shp[:-2]),) + shp[-2:]
        view = lambda arr: arr.reshape(shard3)
        outs = None
        for q in range(shard3[0]):
            fills = outs if outs is not None else [lax.empty(shard3, F32) for _ in range(4)]
            p4, r3 = reduced[(n, q)]
            outs = _adam_shard(f"adam_{n}{q}", view(weights[n]), view(mom_m[n]), view(mom_v[n]), p4, r3, chip_idx,
                               first=q, fills=fills if shard3[0] > 1 else None)
        out_g[n], out_d[n], out_m[n], out_v[n] = [o.reshape(shp) for o in outs]

    return (loss, grad_x, *[out_g[n] for n in names], *[out_d[n] for n in names], *[out_m[n] for n in names],
            *[out_v[n] for n in names])
```

```python
import functools
import math

import jax
import jax.numpy as jnp
from jax import lax
from jax.experimental import pallas as pl
from jax.experimental.pallas import tpu as pltpu

F32 = jnp.float32
BF16 = jnp.bfloat16
SDS = jax.ShapeDtypeStruct
MESH = pl.DeviceIdType.MESH
ANY = pl.BlockSpec(memory_space=pl.ANY)

N_DEV = 8
LANES = 128
HEAD_DIM = 64
LRU_BLOCK_W = 128
LRU_C = 8.0
MACARON_W = 0.5
NORM_EPS = 1e-6
ADAM_LR = 0.001
ADAM_B1 = 0.9
ADAM_B2 = 0.999
ADAM_EPS = 1e-08
ADAM_WD = 0.01
ADAM_STEP = 10
VMEM_LIMIT = 56 * 1024 * 1024
GELU_C = math.sqrt(2.0 / math.pi)
GELU_K = 0.044715

DIMS = {
    "nn": (((1,), (0,)), ((), ())),
    "nt": (((1,), (1,)), ((), ())),
    "tn": (((0,), (0,)), ((), ())),
}


def _pcall(body, **kw):
    return pl.pallas_call(body, **kw)


def _params(sem=None):
    return pltpu.CompilerParams(dimension_semantics=sem, vmem_limit_bytes=VMEM_LIMIT)


def _tile(n, prefs):
    for p in prefs:
        if n % p == 0:
            return p
    return n


def _dot(a, b, dims):
    return lax.dot_general(a, b, DIMS[dims], preferred_element_type=F32)


def _softplus(z):
    return jnp.maximum(z, 0.0) + jnp.log(1.0 + jnp.exp(-jnp.abs(z)))


def _mesh_pos():
    return lax.axis_index("x"), lax.axis_index("y"), lax.axis_index("c")


def _allgather(xs, name, cols=False):
    return _run_comm(_gather_plan([xs], [cols]), name)[0]


class _CommPlan:
    def __init__(self, ins, outs, n_remote, n_local, phases):
        self.ins, self.outs, self.n_remote, self.n_local, self.phases = ins, outs, n_remote, n_local, phases

    def scratch(self):
        return [pltpu.SemaphoreType.DMA((self.n_remote,)), pltpu.SemaphoreType.DMA((self.n_remote,)),
                pltpu.SemaphoreType.DMA((max(self.n_local, 1),))]


def _merge_plans(plans):
    plans = [p for p in plans if p is not None]
    if len(plans) <= 1:
        return plans[0] if plans else None

    def phase(k):
        def run(in_refs, out_refs, send_sems, recv_sems, local_sems, r0=0, l0=0):
            i0 = o0 = 0
            for p in plans:
                p.phases[k](in_refs[i0:i0 + len(p.ins)], out_refs[o0:o0 + len(p.outs)], send_sems, recv_sems, local_sems, r0, l0)
                i0, o0, r0, l0 = i0 + len(p.ins), o0 + len(p.outs), r0 + p.n_remote, l0 + p.n_local
        return run

    return _CommPlan(sum([p.ins for p in plans], []), sum([p.outs for p in plans], []), sum(p.n_remote for p in plans),
                     sum(p.n_local for p in plans), [phase(0), phase(1), phase(2)])


def _run_comm(plan, name):
    n_in, n_out = len(plan.ins), len(plan.outs)

    def body(*refs):
        in_refs, out_refs, sems = refs[:n_in], refs[n_in:n_in + n_out], refs[n_in + n_out:]
        for phase in plan.phases:
            phase(in_refs, out_refs, *sems)

    return _pcall(body, name=name, out_shape=plan.outs, in_specs=[ANY] * n_in, out_specs=[ANY] * n_out,
                  scratch_shapes=plan.scratch())(*plan.ins)


def _col_window(ref, idx, width):
    return ref.at[:, :, pl.ds(pl.multiple_of(idx * width, math.gcd(width, LANES)), width)]


def _gather_plan(shards, cols):
    n = len(shards)
    outs = [SDS((s.shape[0], s.shape[1], N_DEV * s.shape[2]) if cl else (s.shape[0], N_DEV) + s.shape[1:], s.dtype)
            for s, cl in zip(shards, cols)]

    def copies(a, in_refs, out_refs, send_sems, recv_sems, local_sems, r0=0, l0=0):
        x, y, c = _mesh_pos()
        sibling = (x, y, 1 - c)
        chips = [(1 - x, y), (x, 1 - y), (1 - x, 1 - y)]
        width = shards[a].shape[2]

        def block(px, py, pc):
            idx = 4 * px + 2 * py + pc
            return _col_window(out_refs[a], idx, width) if cols[a] else out_refs[a].at[:, idx]

        def copy(k, owner, to, src=None):
            sem = r0 + 7 * a + k
            return pltpu.make_async_remote_copy(
                src_ref=block(*owner) if src is None else src, dst_ref=block(*owner),
                send_sem=send_sems.at[sem], recv_sem=recv_sems.at[sem], device_id=to, device_id_type=MESH)

        me = (x, y, c)
        first = [copy(0, me, sibling, src=in_refs[a])]
        first += [copy(1 + j, me, (*chip, c), src=in_refs[a]) for j, chip in enumerate(chips)]
        passed = [copy(4 + j, (*chip, c), sibling) for j, chip in enumerate(chips)]
        landed = [copy(1 + j, (*chip, c), me) for j, chip in enumerate(chips)]
        from_sibling = [copy(0, sibling, me)] + [copy(4 + j, (*chip, 1 - c), me) for j, chip in enumerate(chips)]
        mine = pltpu.make_async_copy(in_refs[a], block(*me), local_sems.at[l0 + a])
        return first, passed, landed, from_sibling, mine

    def start(*refs):
        for a in range(n):
            first, _, _, _, mine = copies(a, *refs)
            mine.start()
            for cp in first:
                cp.start()

    def pass_on(*refs):
        for a in range(n):
            _, passed, landed, _, _ = copies(a, *refs)
            for cp, fwd in zip(landed, passed):
                cp.wait_recv()
                fwd.start()

    def finish(*refs):
        for a in range(n):
            first, passed, _, from_sibling, mine = copies(a, *refs)
            for cp in from_sibling:
                cp.wait_recv()
            for cp in first + passed:
                cp.wait_send()
            mine.wait()

    return _CommPlan(list(shards), outs, 7 * n, n, [start, pass_on, finish])


def _exchange_plan(srcs, cols, n_slots, route):
    n = len(srcs)
    outs = []
    for g, cl in zip(srcs, cols):
        shard = (g.shape[0], g.shape[1], g.shape[2] // N_DEV) if cl else (g.shape[0],) + g.shape[2:]
        outs.append(SDS((n_slots,) + shard, g.dtype))

    def copies(in_refs, out_refs, send_sems, recv_sems, local_sems, r0=0, l0=0):
        x, y, c = _mesh_pos()
        made = []
        for a in range(n):
            for s in range(n_slots):
                chunk, target = route(x, y, c, s)
                src = _col_window(in_refs[a], chunk, outs[a].shape[3]) if cols[a] else in_refs[a].at[:, chunk]
                sem = r0 + a * n_slots + s
                made.append(pltpu.make_async_remote_copy(
                    src_ref=src, dst_ref=out_refs[a].at[s], send_sem=send_sems.at[sem], recv_sem=recv_sems.at[sem],
                    device_id=target, device_id_type=MESH))
        return made

    def start(*refs):
        for cp in copies(*refs):
            cp.start()

    def nothing(*refs):
        pass

    def finish(*refs):
        made = copies(*refs)
        for cp in made:
            cp.wait_recv()
        for cp in made:
            cp.wait_send()

    return _CommPlan(list(srcs), outs, n * n_slots, 0, [start, nothing, finish])


def _sibling_route(x, y, c, k):
    return 2 * k + 1 - c, (x, y, 1 - c)


def _chip_route(x, y, c, j):
    px, py = [(1 - x, y), (x, 1 - y), (1 - x, 1 - y)][j]
    return 2 * px + py, (px, py, c)


def _pair_sum(grads, recv4, c_idx, name, cols=False):
    _, p, r, cdim = recv4.shape
    tr = _tile(r, (512, 256, 176, 160, 128, 64, 32, 16))

    def body(c_ref, a_ref, b_ref, o_ref):
        o_ref[...] = (a_ref[...].astype(F32) + b_ref[...].astype(F32)).astype(o_ref.dtype)

    blk = (None, None, tr, cdim)
    if cols:
        own = pl.BlockSpec((None, tr, cdim), lambda k, q, i, c_ref: (q, i, 2 * k + c_ref[0]))
    else:
        own = pl.BlockSpec(blk, lambda k, q, i, c_ref: (q, 2 * k + c_ref[0], i, 0))
    grid_spec = pltpu.PrefetchScalarGridSpec(
        num_scalar_prefetch=1, grid=(4, p, r // tr),
        in_specs=[own, pl.BlockSpec(blk, lambda k, q, i, c_ref: (k, q, i, 0))],
        out_specs=pl.BlockSpec(blk, lambda k, q, i, c_ref: (q, k, i, 0)))
    return _pcall(body, name=name, grid_spec=grid_spec, out_shape=SDS((p, 4, r, cdim), grads.dtype),
                  compiler_params=_params(("parallel", "parallel", "parallel")))(c_idx, grads, recv4)


def _mm(name, ins, prods, n_acc, acc_shape, epi_idx, epilogue, out_shapes, out_specs, grid, dims, plan=None):
    n_in, n_out, nk = len(ins), len(out_shapes), grid[2]
    n_acc_refs = n_acc if nk > 1 else 0
    c_ins, c_outs = (plan.ins, plan.outs) if plan else ([], [])
    n_cin, n_cout = len(c_ins), len(c_outs)

    def body(*refs):
        in_refs, c_in = refs[:n_in], refs[n_in:n_in + n_cin]
        rest = refs[n_in + n_cin:]
        out_refs, c_out = rest[:n_out], rest[n_out:n_out + n_cout]
        rest = rest[n_out + n_cout:]
        acc_refs, sems = rest[:n_acc_refs], rest[n_acc_refs:]
        ids = [pl.program_id(axis) for axis in range(3)]
        if plan:
            @pl.when((ids[0] == 0) & (ids[1] == 0) & (ids[2] == 0))
            def _():
                plan.phases[0](c_in, c_out, *sems)

        def finish(accs):
            outs = epilogue(accs, [in_refs[i][...] for i in epi_idx])
            for o_ref, o in zip(out_refs, outs):
                if isinstance(o, tuple):
                    for plane, part in enumerate(o):
                        o_ref[plane] = part.astype(o_ref.dtype)
                else:
                    o_ref[...] = o.astype(o_ref.dtype)

        if nk == 1:
            accs = [None] * n_acc
            for ia, ib, iacc in prods:
                term = _dot(in_refs[ia][...], in_refs[ib][...], dims)
                accs[iacc] = term if accs[iacc] is None else accs[iacc] + term
            finish(accs)
        else:
            @pl.when(ids[2] == 0)
            def _():
                for acc in acc_refs:
                    acc[...] = jnp.zeros_like(acc)

            for ia, ib, iacc in prods:
                acc_refs[iacc][...] += _dot(in_refs[ia][...], in_refs[ib][...], dims)

            @pl.when(ids[2] == nk - 1)
            def _():
                finish([acc[...] for acc in acc_refs])

        if plan:
            @pl.when((ids[0] == grid[0] - 1) & (ids[1] == grid[1] - 1) & (ids[2] == nk - 1))
            def _():
                plan.phases[1](c_in, c_out, *sems)
                plan.phases[2](c_in, c_out, *sems)

    return _pcall(
        body, name=name, grid=grid, in_specs=[s for _, s in ins] + [ANY] * n_cin,
        out_specs=list(out_specs) + [ANY] * n_cout, out_shape=list(out_shapes) + list(c_outs),
        scratch_shapes=[pltpu.VMEM(acc_shape, F32) for _ in range(n_acc_refs)] + (plan.scratch() if plan else []),
        compiler_params=_params(("arbitrary",) * 3 if plan else ("parallel", "parallel", "arbitrary")),
    )(*[a for a, _ in ins], *c_ins)


def _plain(accs, _):
    return accs


def _mm_nn(name, a, b, out_dtype, extras=(), epilogue=_plain, n_out=1):
    m, kd = a.shape
    n = b.shape[1]
    tm, tn, tk = _tile(m, (1024, 512, 256, 128)), _tile(n, (640, 512, 256, 128)), _tile(kd, (1280, 1024, 512, 256, 128))
    ins = [(a, pl.BlockSpec((tm, tk), lambda i, j, k: (i, k))), (b, pl.BlockSpec((tk, tn), lambda i, j, k: (k, j)))]
    for arr, kind in extras:
        if kind == "tile":
            ins.append((arr, pl.BlockSpec((tm, tn), lambda i, j, k: (i, j))))
        else:
            ins.append((arr, pl.BlockSpec((1, tn), lambda i, j, k: (0, j))))
    dts = out_dtype if isinstance(out_dtype, (list, tuple)) else [out_dtype] * n_out
    return _mm(name, ins, [(0, 1, 0)], 1, (tm, tn), list(range(2, len(ins))), epilogue,
               [SDS((m, n), dt) for dt in dts], [pl.BlockSpec((tm, tn), lambda i, j, k: (i, j)) for _ in dts],
               (m // tm, n // tn, kd // tk), "nn")


def _mm_nt(name, a, b, out_dtype):
    m, kd = a.shape
    n = b.shape[0]
    tm, tn, tk = _tile(m, (1024, 512, 256, 128)), _tile(n, (640, 512, 256, 128)), _tile(kd, (1024, 512, 256, 128))
    ins = [(a, pl.BlockSpec((tm, tk), lambda i, j, k: (i, k))), (b, pl.BlockSpec((tn, tk), lambda i, j, k: (j, k)))]
    return _mm(name, ins, [(0, 1, 0)], 1, (tm, tn), [], _plain, [SDS((m, n), out_dtype)],
               [pl.BlockSpec((tm, tn), lambda i, j, k: (i, j))], (m // tm, n // tn, kd // tk), "nt")[0]


def _mm_tn(name, a, b, out_dtype):
    t, m = a.shape
    n = b.shape[1]
    tm, tn, tk = _tile(m, (640, 512, 256, 128)), _tile(n, (1024, 512, 256, 128)), t
    ins = [(a, pl.BlockSpec((tk, tm), lambda i, j, k: (k, i))), (b, pl.BlockSpec((tk, tn), lambda i, j, k: (k, j)))]
    return _mm(name, ins, [(0, 1, 0)], 1, (tm, tn), [], _plain, [SDS((m, n), out_dtype)],
               [pl.BlockSpec((tm, tn), lambda i, j, k: (i, j))], (m // tm, n // tn, t // tk), "tn")[0]


def _mm_nt_stack(name, a3, b, out_dtype):
    cc, m, kd = a3.shape
    n = b.shape[0]
    tm, tn, tk = _tile(m, (1024, 512, 256, 128)), _tile(n, (1024, 512, 256, 128)), _tile(kd, (1280, 1024, 512, 256, 128))
    nk = kd // tk
    ins = [(a3, pl.BlockSpec((None, tm, tk), lambda i, j, k: (k // nk, i, k % nk))),
           (b, pl.BlockSpec((tn, tk), lambda i, j, k: (j, k)))]
    return _mm(name, ins, [(0, 1, 0)], 1, (tm, tn), [], _plain, [SDS((m, n), out_dtype)],
               [pl.BlockSpec((tm, tn), lambda i, j, k: (i, j))], (m // tm, n // tn, cc * nk), "nt")[0]


def _mm_tn_stack(name, a, b3, out_dtype):
    t, m = a.shape
    cc, _, n = b3.shape
    tm, tn, tk = _tile(m, (512, 256, 128)), _tile(n, (1280, 1024, 512, 256, 128)), t
    nj = n // tn
    ins = [(a, pl.BlockSpec((tk, tm), lambda i, j, k: (k, i))),
           (b3, pl.BlockSpec((None, tk, tn), lambda i, j, k: (j // nj, k, j % nj)))]
    return _mm(name, ins, [(0, 1, 0)], 1, (tm, tn), [], _plain, [SDS((m, cc * n), out_dtype)],
               [pl.BlockSpec((tm, tn), lambda i, j, k: (i, j))], (m // tm, cc * nj, t // tk), "tn")[0]


def _chunks_to_cols(name, wc, eye2):
    nch, d, cw = wc.shape
    tm = _tile(d, (1024, 512, 256, 128))
    ins = [(wc, pl.BlockSpec((None, tm, cw), lambda i, j, k: (2 * j + k, i, 0))),
           (eye2, pl.BlockSpec((None, cw, 2 * cw), lambda i, j, k: (k, 0, 0)))]
    return _mm(name, ins, [(0, 1, 0)], 1, (tm, 2 * cw), [], _plain, [SDS((d, nch * cw), wc.dtype)],
               [pl.BlockSpec((tm, 2 * cw), lambda i, j, k: (i, j))], (d // tm, nch // 2, 2), "nn")[0]


def _cols_to_chunks(name, full, eye2):
    d, n = full.shape
    _, cw, _ = eye2.shape
    nch = n // cw
    tm = _tile(d, (1024, 512, 256, 128))
    ins = [(full, pl.BlockSpec((tm, 2 * cw), lambda i, j, k: (i, j // 2))),
           (eye2, pl.BlockSpec((None, cw, 2 * cw), lambda i, j, k: (j % 2, 0, 0)))]
    return _mm(name, ins, [(0, 1, 0)], 1, (tm, cw), [], _plain, [SDS((nch, d, cw), full.dtype)],
               [pl.BlockSpec((None, tm, cw), lambda i, j, k: (j, i, 0))], (d // tm, nch, 1), "nt")[0]


def _row_tile(t):
    return _tile(t, (256, 128, 64, 32, 16, 8))


def _norm_fwd(x, g, scale1p, shift, name):
    t, d = x.shape
    tr = _row_tile(t)

    def body(x_ref, g_ref, s_ref, b_ref, h_ref):
        xv = x_ref[...]
        inv = lax.rsqrt(jnp.mean(xv * xv, axis=-1, keepdims=True) + NORM_EPS)
        h_ref[...] = ((xv * inv) * g_ref[...] * s_ref[...] + b_ref[...]).astype(h_ref.dtype)

    vec = pl.BlockSpec((1, d), lambda i: (0, 0))
    return _pcall(body, name=name, grid=(t // tr,), in_specs=[pl.BlockSpec((tr, d), lambda i: (i, 0)), vec, vec, vec],
                  out_specs=pl.BlockSpec((tr, d), lambda i: (i, 0)), out_shape=SDS((t, d), BF16),
                  compiler_params=_params(("parallel",)))(x, g, scale1p, shift)


def _adaln_bwd(dh, x, y, dxo, g, scale1p, w_sub, gw_prev, name):
    t, d = x.shape
    tr = _row_tile(t)

    def body(dh_ref, x_ref, y_ref, dxo_ref, g_ref, s_ref, gw_ref, dx_ref, dyp_ref, sums_ref):
        i = pl.program_id(0)

        @pl.when(i == 0)
        def _():
            sums_ref[...] = jnp.zeros_like(sums_ref)

        xv, dhv, dxov = x_ref[...], dh_ref[...], dxo_ref[...]
        inv = lax.rsqrt(jnp.mean(xv * xv, axis=-1, keepdims=True) + NORM_EPS)
        xn = xv * inv
        gv = g_ref[...]
        dn = dhv * s_ref[...]
        dxn = dn * gv
        dx = inv * (dxn - xn * jnp.mean(dxn * xn, axis=-1, keepdims=True)) + dxov
        dx_ref[...] = dx
        dyp_ref[...] = (gw_ref[...] * dx).astype(dyp_ref.dtype)
        sums_ref[0:1, :] += jnp.sum(dhv, axis=0, keepdims=True)
        sums_ref[1:2, :] += jnp.sum(dhv * (xn * gv), axis=0, keepdims=True)
        sums_ref[2:3, :] += jnp.sum(w_sub * y_ref[...] * dxov, axis=0, keepdims=True)
        sums_ref[3:4, :] += jnp.sum(dn * xn, axis=0, keepdims=True)

    blk = pl.BlockSpec((tr, d), lambda i: (i, 0))
    vec = pl.BlockSpec((1, d), lambda i: (0, 0))
    return _pcall(
        body, name=name, grid=(t // tr,), in_specs=[blk, blk, blk, blk, vec, vec, vec],
        out_specs=[blk, blk, pl.BlockSpec((8, d), lambda i: (0, 0))],
        out_shape=[SDS((t, d), F32), SDS((t, d), BF16), SDS((8, d), F32)],
        compiler_params=_params(("arbitrary",)))(dh, x, y, dxo, g, scale1p, gw_prev)


def _loss_head(x, target, gf, gw_prev):
    t, d = x.shape
    tr = _row_tile(t)
    nt = t // tr

    def body(x_ref, tg_ref, g_ref, gw_ref, dx_ref, dyp_ref, sums_ref):
        i = pl.program_id(0)

        @pl.when(i == 0)
        def _():
            sums_ref[...] = jnp.zeros_like(sums_ref)

        xv = x_ref[...]
        inv = lax.rsqrt(jnp.mean(xv * xv, axis=-1, keepdims=True) + NORM_EPS)
        xn = xv * inv
        gv = g_ref[...]
        err = xn * gv - tg_ref[...]
        dyv = err * (1.0 / d)
        dxn = dyv * gv
        dx = inv * (dxn - xn * jnp.mean(dxn * xn, axis=-1, keepdims=True))
        dx_ref[...] = dx
        dyp_ref[...] = (gw_ref[...] * dx).astype(dyp_ref.dtype)
        sums_ref[0:1, :] += jnp.sum(dyv * xn, axis=0, keepdims=True)
        sums_ref[1:2, :] += jnp.sum(err * err, axis=0, keepdims=True)

        @pl.when(i == nt - 1)
        def _():
            tot = jnp.sum(sums_ref[1:2, :], axis=1, keepdims=True) * (0.5 / d)
            sums_ref[1:2, :] = jnp.broadcast_to(tot, (1, d))

    blk = pl.BlockSpec((tr, d), lambda i: (i, 0))
    vec = pl.BlockSpec((1, d), lambda i: (0, 0))
    return _pcall(
        body, name="loss_head", grid=(nt,), in_specs=[blk, blk, vec, vec],
        out_specs=[blk, blk, pl.BlockSpec((8, d), lambda i: (0, 0))],
        out_shape=[SDS((t, d), F32), SDS((t, d), BF16), SDS((8, d), F32)],
        compiler_params=_params(("arbitrary",)))(x, target, gf, gw_prev)


HIDDEN_CHUNKS = N_DEV // 2


def _ffn_tiles(t, d):
    return _tile(t, (1024, 512, 256, 128)), _tile(d, (1024, 512, 256, 128))


def _ffn_gu(name, h, wgu, plan=None):
    t, d = h.shape
    fc, nc = wgu.shape[3], HIDDEN_CHUNKS
    tm, _ = _ffn_tiles(t, d)

    def epi_gu(accs, _):
        gpre, up = accs
        return (gpre, up), gpre * jax.nn.sigmoid(gpre) * up

    wblk = (None, None, d, fc)
    ins = [(h, pl.BlockSpec((tm, d), lambda i, c, k: (i, 0))),
           (wgu, pl.BlockSpec(wblk, lambda i, c, k: (0, c, 0, 0))),
           (wgu, pl.BlockSpec(wblk, lambda i, c, k: (0, c + nc, 0, 0)))]
    return _mm(name, ins, [(0, 1, 0), (0, 2, 1)], 2, (tm, fc), [], epi_gu,
               [SDS((2, nc, t, fc), BF16), SDS((nc, t, fc), BF16)],
               [pl.BlockSpec((2, None, tm, fc), lambda i, c, k: (0, c, i, 0)),
                pl.BlockSpec((None, tm, fc), lambda i, c, k: (c, i, 0))],
               (t // tm, nc, 1), "nn", plan=plan)


def _ffn_down(name, a, wd4, x, gmul, plan=None):
    nc, t, fc = a.shape
    d = wd4.shape[3]
    tm, tn = _ffn_tiles(t, d)

    def epi_down(accs, ex):
        (yv,), (xv, gm) = accs, ex
        return yv, xv + MACARON_W * gm * yv

    ins = [(a, pl.BlockSpec((None, tm, fc), lambda i, j, k: (k, i, 0))),
           (wd4, pl.BlockSpec((None, None, fc, tn), lambda i, j, k: (0, k, 0, j))),
           (x, pl.BlockSpec((tm, tn), lambda i, j, k: (i, j))), (gmul, pl.BlockSpec((1, tn), lambda i, j, k: (0, j)))]
    oblk = pl.BlockSpec((tm, tn), lambda i, j, k: (i, j))
    return _mm(name, ins, [(0, 1, 0)], 1, (tm, tn), [2, 3], epi_down, [SDS((t, d), BF16), SDS((t, d), F32)],
               [oblk, oblk], (t // tm, d // tn, nc), "nn", plan=plan)


def _ffn_da(name, dy, wd4, gu2, plan=None):
    t, d = dy.shape
    _, nc, fc, _ = wd4.shape
    tm, _ = _ffn_tiles(t, d)

    def epi_da(accs, ex):
        (da,), (gu,) = accs, ex
        gpre, up = gu[0].astype(F32), gu[1].astype(F32)
        s = jax.nn.sigmoid(gpre)
        silu = gpre * s
        dg = da * up * (s * (1.0 + gpre * (1.0 - s)))
        return ((dg, da * silu),)

    gblk = pl.BlockSpec((2, None, tm, fc), lambda i, c, k: (0, c, i, 0))
    ins = [(dy, pl.BlockSpec((tm, d), lambda i, c, k: (i, 0))),
           (wd4, pl.BlockSpec((None, None, fc, d), lambda i, c, k: (0, c, 0, 0))), (gu2, gblk)]
    return _mm(name, ins, [(0, 1, 0)], 1, (tm, fc), [2], epi_da, [SDS((2, nc, t, fc), BF16)], [gblk],
               (t // tm, nc, 1), "nt", plan=plan)


def _ffn_dwd(name, a, dy, plan=None):
    nc, t, fc = a.shape
    d = dy.shape[1]
    _, tn = _ffn_tiles(t, d)
    ins = [(a, pl.BlockSpec((None, t, fc), lambda c, j, k: (c, 0, 0))), (dy, pl.BlockSpec((t, tn), lambda c, j, k: (0, j)))]
    return _mm(name, ins, [(0, 1, 0)], 1, (fc, tn), [], _plain, [SDS((1, nc, fc, d), BF16)],
               [pl.BlockSpec((None, None, fc, tn), lambda c, j, k: (0, c, 0, j))], (nc, d // tn, 1), "tn", plan=plan)


def _ffn_dwgu(name, h, dgu2, plan=None):
    t, d = h.shape
    _, nc, _, fc = dgu2.shape
    _, tn = _ffn_tiles(t, d)
    ins = [(h, pl.BlockSpec((t, tn), lambda i, c, k: (0, i))),
           (dgu2, pl.BlockSpec((None, None, t, fc), lambda i, c, k: (c // nc, c % nc, 0, 0)))]
    return _mm(name, ins, [(0, 1, 0)], 1, (tn, fc), [], _plain, [SDS((1, 2 * nc, d, fc), BF16)],
               [pl.BlockSpec((None, None, tn, fc), lambda i, c, k: (0, c, i, 0))], (d // tn, 2 * nc, 1), "tn", plan=plan)


def _ffn_dh(name, dgu2, wgu, plan=None):
    _, nc, t, fc = dgu2.shape
    d = wgu.shape[2]
    tm, tn = _ffn_tiles(t, d)
    ins = [(dgu2, pl.BlockSpec((None, None, tm, fc), lambda i, j, k: (k // nc, k % nc, i, 0))),
           (wgu, pl.BlockSpec((None, None, tn, fc), lambda i, j, k: (0, k, j, 0)))]
    return _mm(name, ins, [(0, 1, 0)], 1, (tm, tn), [], _plain, [SDS((t, d), F32)],
               [pl.BlockSpec((tm, tn), lambda i, j, k: (i, j))], (t // tm, d // tn, 2 * nc), "nt", plan=plan)


def _sb_block(t):
    return 256 if t >= 1024 else 128


SB_STRIP = 64


def _sb_strips(blk):
    strip = min(SB_STRIP, blk)
    row = lax.broadcasted_iota(jnp.int32, (strip, blk), 0)
    col = lax.broadcasted_iota(jnp.int32, (strip, blk), 1)
    return [(slice(r0, r0 + strip), col < row + r0) for r0 in range(0, blk, strip)]


def _host_call(core, name, steps, ins, in_specs, out_shapes, out_specs, scratch, plan):
    n_in, n_out, n_scr = len(ins), len(out_shapes), len(scratch)
    c_ins, c_outs = (plan.ins, plan.outs) if plan else ([], [])
    n_cin, n_cout = len(c_ins), len(c_outs)

    def body(*refs):
        in_refs, c_in = refs[:n_in], refs[n_in:n_in + n_cin]
        rest = refs[n_in + n_cin:]
        out_refs, c_out = rest[:n_out], rest[n_out:n_out + n_cout]
        rest = rest[n_out + n_cout:]
        scr, sems = rest[:n_scr], rest[n_scr:]
        step = pl.program_id(0)
        if plan:
            @pl.when(step == 0)
            def _():
                plan.phases[0](c_in, c_out, *sems)

        core(in_refs, out_refs, scr)
        if plan:
            @pl.when(step == steps - 1)
            def _():
                plan.phases[1](c_in, c_out, *sems)
                plan.phases[2](c_in, c_out, *sems)

    return _pcall(
        body, name=name, grid=(steps,), in_specs=list(in_specs) + [ANY] * n_cin,
        out_specs=list(out_specs) + [ANY] * n_cout, out_shape=list(out_shapes) + list(c_outs),
        scratch_shapes=list(scratch) + (plan.scratch() if plan else []),
        compiler_params=_params(("arbitrary",)))(*ins, *c_ins)


def _sb_fwd(qkv, d, plan=None):
    t = qkv.shape[0]
    blk = _sb_block(t)
    nq = t // blk
    npair = d // LANES
    scale = HEAD_DIM ** -0.5

    def body(in_refs, out_refs, scr):
        (q_ref, k_ref, v_ref), (o_ref, l_ref) = in_refs, out_refs
        tri_s = scr[0]
        hi_s, lo_s, w_s, zs_s = (scr[1 + 2 * k:3 + 2 * k] for k in range(4))
        lane = lax.broadcasted_iota(jnp.int32, (blk, LANES), 1)
        head0 = lane < HEAD_DIM
        row = lax.broadcasted_iota(jnp.int32, (blk, blk), 0)
        col = lax.broadcasted_iota(jnp.int32, (blk, blk), 1)
        tri_s[...] = (row > col).astype(BF16)
        strips = _sb_strips(blk)

        def both(qhs, kb, carries, masked):
            start = pl.multiple_of(kb * blk, blk)
            kv = k_ref[pl.ds(start, blk), :]
            vv = v_ref[pl.ds(start, blk), :]
            zs = [_dot(qh, kv, "nt") for qh in qhs]
            sums = []
            for hh in range(2):
                parts = []
                for rows, causal in strips:
                    zt = zs[hh][rows, :]
                    sp = _softplus(zt)
                    lk = jnp.where(causal, -sp, 0.0) if masked else -sp
                    hi = lk.astype(BF16)
                    hi_s[hh][rows, :] = hi
                    lo_s[hh][rows, :] = (lk - hi.astype(F32)).astype(BF16)
                    zs_s[hh][rows, :] = zt - sp
                    parts.append(jnp.sum(lk, axis=1, keepdims=True))
                sums.append(jnp.concatenate(parts, axis=0))
            laters = [_dot(hi_s[hh][...],tri_s[...], "nn") + _dot(lo_s[hh][...],tri_s[...], "nn") for hh in range(2)]
            for hh in range(2):
                cl = carries[hh][0]
                for rows, causal in strips:
                    logw = zs_s[hh][rows, :] + laters[hh][rows, :] + cl[rows, :]
                    if masked:
                        logw = jnp.where(causal, logw, -1e30)
                    w_s[hh][rows, :] = jnp.exp(logw).astype(BF16)
            return tuple((carries[hh][0] + sums[hh], carries[hh][1] + _dot(w_s[hh][...],vv, "nn")) for hh in range(2))

        def qblock(qi, _):
            qstart = pl.multiple_of(qi * blk, blk)
            qv = q_ref[pl.ds(qstart, blk), :] * scale
            qhs = [jnp.where(head0 if hh == 0 else ~head0, qv, jnp.zeros_like(qv)) for hh in range(2)]
            zero = (jnp.zeros((blk, 1), F32), jnp.zeros((blk, LANES), F32))
            outs = both(qhs, qi, (zero, zero), True)
            outs = lax.fori_loop(0, qi, lambda j, crs: both(qhs, qi - 1 - j, crs, False), outs)
            o_ref[pl.ds(qstart, blk), :] = jnp.where(head0, outs[0][1], outs[1][1]).astype(o_ref.dtype)
            l_ref[pl.ds(qstart, blk), :] = jnp.where(head0, outs[0][0], outs[1][0])
            return 0

        lax.fori_loop(0, nq, qblock, 0)

    tile_bf16, tile_f32 = pltpu.VMEM((blk, blk), BF16), pltpu.VMEM((blk, blk), F32)
    return _host_call(
        body, "sb_fwd", npair, [qkv, qkv, qkv],
        [pl.BlockSpec((t, LANES), lambda p: (0, p)), pl.BlockSpec((t, LANES), lambda p: (0, npair + p)),
         pl.BlockSpec((t, LANES), lambda p: (0, 2 * npair + p))],
        [SDS((t, d), BF16), SDS((t, d), F32)],
        [pl.BlockSpec((t, LANES), lambda p: (0, p)), pl.BlockSpec((t, LANES), lambda p: (0, p))],
        [tile_bf16] * 7 + [tile_f32] * 2, plan)


def _sb_bwd(qkv, do, ltot, d, plan=None):
    t = qkv.shape[0]
    blk = _sb_block(t)
    nq = t // blk
    npair = d // LANES
    scale = HEAD_DIM ** -0.5

    def body(in_refs, out_refs, scr):
        (q_ref, k_ref, v_ref, do_ref, l_ref), (out_ref,) = in_refs, out_refs
        dq_s, dk_s, dv_s, upto_s, before_s = scr[:5]
        hi_s, lo_s, w_s, dab_s, dzs_s, zs_s, da_s = (scr[5 + 4 * k:9 + 4 * k] for k in range(7))
        lane = lax.broadcasted_iota(jnp.int32, (blk, LANES), 1)
        head0 = lane < HEAD_DIM
        row = lax.broadcasted_iota(jnp.int32, (blk, blk), 0)
        col = lax.broadcasted_iota(jnp.int32, (blk, blk), 1)
        upto_s[...] = (row <= col).astype(BF16)
        before_s[...] = (row < col).astype(BF16)
        dk_s[...] = jnp.zeros_like(dk_s)
        dv_s[...] = jnp.zeros_like(dv_s)
        strips = _sb_strips(blk)

        def step(heads, kbs, maskeds, carries):
            chains = [(bi, hh) for bi in range(len(kbs)) for hh in range(2)]
            starts = [pl.multiple_of(kb * blk, blk) for kb in kbs]
            kvs = [k_ref[pl.ds(start, blk), :] for start in starts]
            vvs = [v_ref[pl.ds(start, blk), :] for start in starts]
            zs = [_dot(heads[hh][0], kvs[bi], "nt") for bi, hh in chains]
            dws = [_dot(heads[hh][1], vvs[bi], "nt") for bi, hh in chains]
            lk_sums, da_sums = [], []
            for c, (bi, hh) in enumerate(chains):
                parts = []
                for rows, causal in strips:
                    zt = zs[c][rows, :]
                    sp = _softplus(zt)
                    lk = jnp.where(causal, -sp, 0.0) if maskeds[bi] else -sp
                    hi = lk.astype(BF16)
                    hi_s[c][rows, :] = hi
                    lo_s[c][rows, :] = (lk - hi.astype(F32)).astype(BF16)
                    zs_s[c][rows, :] = zt - sp
                    parts.append(jnp.sum(lk, axis=1, keepdims=True))
                lk_sums.append(jnp.concatenate(parts, axis=0))
            cums = [_dot(hi_s[c][...], upto_s[...], "nn") + _dot(lo_s[c][...], upto_s[...], "nn") for c in range(len(chains))]
            for c, (bi, hh) in enumerate(chains):
                lt, plk = heads[hh][2], carries[hh][0]
                if bi == 1:
                    plk = plk + lk_sums[hh]
                parts = []
                for rows, causal in strips:
                    logw = zs_s[c][rows, :] + (lt[rows, :] - (plk[rows, :] + cums[c][rows, :]))
                    if maskeds[bi]:
                        logw = jnp.where(causal, logw, -1e30)
                    w = jnp.exp(logw)
                    w_s[c][rows, :] = w.astype(BF16)
                    da = dws[c][rows, :] * w
                    da_s[c][rows, :] = da
                    dab_s[c][rows, :] = da.astype(BF16)
                    parts.append(jnp.sum(da, axis=1, keepdims=True))
                da_sums.append(jnp.concatenate(parts, axis=0))
            pres = [_dot(dab_s[c][...], before_s[...], "nn") for c in range(len(chains))]
            for c, (bi, hh) in enumerate(chains):
                pda = carries[hh][1]
                if bi == 1:
                    pda = pda + da_sums[hh]
                for rows, causal in strips:
                    sig = jnp.exp(zs_s[c][rows, :])
                    da = da_s[c][rows, :]
                    dz = da * (1.0 - sig) - sig * (pda[rows, :] + pres[c][rows, :])
                    if maskeds[bi]:
                        dz = jnp.where(causal, dz, 0.0)
                    dzs_s[c][rows, :] = dz.astype(BF16)
            new = [list(carries[hh]) for hh in range(2)]
            for c, (bi, hh) in enumerate(chains):
                dk_s[kbs[bi]] += _dot(heads[hh][3], dzs_s[c][...], "nn")
                dv_s[kbs[bi]] += _dot(heads[hh][4], w_s[c][...], "nn")
                new[hh] = [new[hh][0] + lk_sums[c], new[hh][1] + da_sums[c], new[hh][2] + _dot(dzs_s[c][...], kvs[bi], "nn")]
            return tuple(tuple(cr) for cr in new)

        def qblock(qi, _):
            qstart = pl.multiple_of(qi * blk, blk)
            qv = q_ref[pl.ds(qstart, blk), :] * scale
            dov = do_ref[pl.ds(qstart, blk), :]
            lv = l_ref[pl.ds(qstart, blk), :]
            heads = []
            for hh in range(2):
                sel = head0 if hh == 0 else ~head0
                qh, doh = jnp.where(sel, qv, jnp.zeros_like(qv)), jnp.where(sel, dov, jnp.zeros_like(dov))
                heads.append((qh, doh, jnp.max(jnp.where(sel, lv, -jnp.inf), axis=1, keepdims=True),
                              qh.astype(F32).T.astype(BF16), doh.astype(F32).T.astype(BF16)))
            zero = (jnp.zeros((blk, 1), F32), jnp.zeros((blk, 1), F32), jnp.zeros((blk, LANES), F32))
            carries = lax.fori_loop(0, qi // 2, lambda j, crs: step(heads, [2 * j, 2 * j + 1], [False, False], crs),
                                    (zero, zero))
            carries = lax.cond(qi % 2 == 1,
                               lambda crs: step(heads, [qi - 1, qi], [False, True], crs),
                               lambda crs: step(heads, [qi], [True], crs), carries)
            dq_s[pl.ds(qstart, blk), :] = jnp.where(head0, carries[0][2], carries[1][2]) * scale
            return 0

        lax.fori_loop(0, nq, qblock, 0)
        out_ref[0] = dq_s[...].astype(out_ref.dtype)
        for b in range(nq):
            out_ref[1, b * blk:(b + 1) * blk, :] = dk_s[b].T.astype(out_ref.dtype)
            out_ref[2, b * blk:(b + 1) * blk, :] = dv_s[b].T.astype(out_ref.dtype)

    col_blk = lambda off: pl.BlockSpec((t, LANES), lambda p: (0, off + p))
    return _host_call(
        body, "sb_bwd", npair, [qkv, qkv, qkv, do, ltot],
        [col_blk(0), col_blk(npair), col_blk(2 * npair), col_blk(0), col_blk(0)],
        [SDS((3, t, d), BF16)], [pl.BlockSpec((3, t, LANES), lambda p: (0, 0, p))],
        [pltpu.VMEM((t, LANES), F32)] + [pltpu.VMEM((nq, LANES, blk), F32) for _ in range(2)]
        + [pltpu.VMEM((blk, blk), BF16) for _ in range(2 + 20)]
        + [pltpu.VMEM((blk, blk), F32) for _ in range(8)], plan)


def _roll_rows(v, shift):
    return pltpu.roll(v, shift, 0)


def _shift_down(v, dist, fill, row):
    return jnp.where(row >= dist, _roll_rows(v, dist), fill)


def _shift_up(v, dist, fill, row):
    t = v.shape[0]
    return jnp.where(row < t - dist, _roll_rows(v, t - dist), fill)


def _lru_gates(xb, small, wr, wi, row):
    xs = [_shift_down(xb, 3 - tap, 0.0, row) if tap < 3 else xb for tap in range(4)]
    xc = small[4:5, :] + xs[0] * small[0:1, :]
    for tap in range(1, 4):
        xc = xc + xs[tap] * small[tap:tap + 1, :]
    xcb = xc.astype(BF16)
    r = jax.nn.sigmoid(_dot(xcb, wr, "nn") + small[5:6, :])
    ig = jax.nn.sigmoid(_dot(xcb, wi, "nn") + small[6:7, :])
    sp = _softplus(-small[7:8, :])
    la = -LRU_C * r * sp
    a = jnp.exp(la)
    th = jnp.tanh(la)
    mult = jnp.sqrt(-2.0 * th / (1.0 - th))
    return xs, xc, xcb, r, ig, sp, a, mult


def _gelu_parts(gate):
    inner = GELU_C * (gate + GELU_K * gate * gate * gate)
    th = jnp.tanh(inner)
    gelu = 0.5 * gate * (1.0 + th)
    dgelu = 0.5 * (1.0 + th) + 0.5 * gate * (1.0 - th * th) * GELU_C * (1.0 + 3.0 * GELU_K * gate * gate)
    return gelu, dgelu


def _scan_steps(t):
    steps, dist = [], 1
    while dist < t:
        steps.append(dist)
        dist *= 2
    return steps


SUBLANES = 8


def _linear_scan(a, b, scratch, row, reverse):
    a_s, b_s, carry_s = scratch
    t = a.shape[0]
    groups = t // SUBLANES
    in_group = row & (SUBLANES - 1)
    for dist in _scan_steps(SUBLANES):
        if reverse:
            inside = in_group < SUBLANES - dist
            b = b + a * jnp.where(inside, _roll_rows(b, t - dist), 0.0)
            a = a * jnp.where(inside, _roll_rows(a, t - dist), 1.0)
        else:
            inside = in_group >= dist
            b = a * jnp.where(inside, _roll_rows(b, dist), 0.0) + b
            a = a * jnp.where(inside, _roll_rows(a, dist), 1.0)
    a_s[...] = a
    b_s[...] = b
    end = 0 if reverse else SUBLANES - 1
    ends = pl.ds(end, groups, stride=SUBLANES)
    ae, be = a_s[ends, :], b_s[ends, :]
    grow = lax.broadcasted_iota(jnp.int32, ae.shape, 0)
    shift = _shift_up if reverse else _shift_down
    for dist in _scan_steps(groups):
        be = ae * shift(be, dist, 0.0, grow) + be
        ae = ae * shift(ae, dist, 1.0, grow)
    incoming = shift(be, 1, 0.0, grow)
    for k in range(SUBLANES):
        carry_s[pl.ds(k, groups, stride=SUBLANES), :] = incoming
    return a_s[...] * carry_s[...] + b_s[...]


def _lru_fwd(gx, small, wr, wi):
    t = gx.shape[0]
    r_dim = gx.shape[1] // 2
    nb = r_dim // LRU_BLOCK_W

    def body(gate_ref, xb_ref, small_ref, wr_ref, wi_ref, y_ref, hs_ref, *scratch):
        row = lax.broadcasted_iota(jnp.int32, (t, LRU_BLOCK_W), 0)
        xb = xb_ref[...]
        _, xc, _, _, ig, _, a, mult = _lru_gates(xb, small_ref, wr_ref[...], wi_ref[...], row)
        hsv = _linear_scan(a, mult * (ig * xc), scratch, row, reverse=False)
        hs_ref[...] = hsv
        gelu, _ = _gelu_parts(gate_ref[...])
        y_ref[...] = (gelu * hsv).astype(y_ref.dtype)

    colb = lambda off: pl.BlockSpec((t, LRU_BLOCK_W), lambda n: (0, off + n))
    wspec = pl.BlockSpec((None, LRU_BLOCK_W, LRU_BLOCK_W), lambda n: (n, 0, 0))
    return _pcall(
        body, name="lru_fwd", grid=(nb,),
        in_specs=[colb(0), colb(nb), pl.BlockSpec((8, LRU_BLOCK_W), lambda n: (0, n)), wspec, wspec],
        out_specs=[colb(0), colb(0)], out_shape=[SDS((t, r_dim), BF16), SDS((t, r_dim), F32)],
        scratch_shapes=[pltpu.VMEM((t, LRU_BLOCK_W), F32) for _ in range(3)],
        compiler_params=_params(("parallel",)))(gx, gx, small, wr, wi)


def _lru_bwd(gx, hs, dy, small, wr, wi):
    t = gx.shape[0]
    r_dim = gx.shape[1] // 2
    nb = r_dim // LRU_BLOCK_W

    def body(gate_ref, xb_ref, hs_ref, dy_ref, small_ref, wr_ref, wi_ref, dgx_ref, dsm_ref, dwr_ref, dwi_ref, *scratch):
        row = lax.broadcasted_iota(jnp.int32, (t, LRU_BLOCK_W), 0)
        xb, hsv, dyv, smallv = xb_ref[...], hs_ref[...], dy_ref[...], small_ref
        wrv, wiv = wr_ref[...], wi_ref[...]
        xs, xc, xcb, r, ig, sp, a, mult = _lru_gates(xb, smallv, wrv, wiv, row)
        gelu, dgelu = _gelu_parts(gate_ref[...])
        dgx_ref[0] = (dyv * hsv * dgelu).astype(dgx_ref.dtype)
        dacc = _linear_scan(_shift_up(a, 1, 1.0, row), dyv * gelu, scratch, row, reverse=True)
        da = dacc * _shift_down(hsv, 1, 0.0, row)
        dmult = dacc * (ig * xc)
        dixc = dacc * mult
        dla = da * a - dmult * (a * a) / mult
        dr = dla * (-LRU_C * sp)
        dsp = jnp.sum(dla * (-LRU_C * r), axis=0, keepdims=True)
        dpr = dr * r * (1.0 - r)
        dpi = dixc * xc * ig * (1.0 - ig)
        dprb, dpib = dpr.astype(BF16), dpi.astype(BF16)
        dwr_ref[...] = _dot(xcb, dprb, "tn")
        dwi_ref[...] = _dot(xcb, dpib, "tn")
        dxc = dixc * ig + _dot(dprb, wrv, "nt") + _dot(dpib, wiv, "nt")
        dxb = dxc * smallv[3:4, :]
        for tap in range(3):
            dxb = dxb + _shift_up(dxc, 3 - tap, 0.0, row) * smallv[tap:tap + 1, :]
        dgx_ref[1] = dxb.astype(dgx_ref.dtype)
        lam = smallv[7:8, :]
        rows = [jnp.sum(dxc * xs[tap], axis=0, keepdims=True) for tap in range(4)]
        rows.append(jnp.sum(dxc, axis=0, keepdims=True))
        rows.append(jnp.sum(dpr, axis=0, keepdims=True))
        rows.append(jnp.sum(dpi, axis=0, keepdims=True))
        rows.append(-dsp * jax.nn.sigmoid(-lam))
        for k, rv in enumerate(rows):
            dsm_ref[k:k + 1, :] = rv

    colb = lambda off: pl.BlockSpec((t, LRU_BLOCK_W), lambda n: (0, off + n))
    wspec = pl.BlockSpec((None, LRU_BLOCK_W, LRU_BLOCK_W), lambda n: (n, 0, 0))
    sspec = pl.BlockSpec((8, LRU_BLOCK_W), lambda n: (0, n))
    return _pcall(
        body, name="lru_bwd", grid=(nb,),
        in_specs=[colb(0), colb(nb), colb(0), colb(0), sspec, wspec, wspec],
        out_specs=[pl.BlockSpec((2, t, LRU_BLOCK_W), lambda n: (0, 0, n)), sspec, wspec, wspec],
        out_shape=[SDS((2, t, r_dim), BF16), SDS((8, r_dim), F32), SDS((nb, LRU_BLOCK_W, LRU_BLOCK_W), F32),
                   SDS((nb, LRU_BLOCK_W, LRU_BLOCK_W), F32)],
        scratch_shapes=[pltpu.VMEM((t, LRU_BLOCK_W), F32) for _ in range(3)],
        compiler_params=_params(("parallel",)))(gx, gx, hs, dy, small, wr, wi)


def _adam(w, g, m, v):
    m2 = ADAM_B1 * m + (1.0 - ADAM_B1) * g
    v2 = ADAM_B2 * v + (1.0 - ADAM_B2) * (g * g)
    m_hat = m2 / (1.0 - ADAM_B1 ** ADAM_STEP)
    v_hat = v2 / (1.0 - ADAM_B2 ** ADAM_STEP)
    return -ADAM_LR * (m_hat / (jnp.sqrt(v_hat) + ADAM_EPS) + ADAM_WD * w), m2, v2


def _mod_fwd(c_all, mod_w, mod_b_cols):
    nl, d, cols = mod_w.shape
    nbatch = c_all.shape[0]

    def body(c_ref, w_ref, b_ref, o_ref):
        cv = c_ref[...]
        ca = (cv * jax.nn.sigmoid(cv)).astype(BF16)
        o_ref[...] = _dot(ca, w_ref[...].astype(BF16), "nn") + b_ref[...]

    return _pcall(
        body, name="mod_fwd", grid=(nl,),
        in_specs=[pl.BlockSpec((nbatch, d), lambda l: (0, 0)), pl.BlockSpec((None, d, cols), lambda l: (l, 0, 0)),
                  pl.BlockSpec((None, 1, cols), lambda l: (l, 0, 0))],
        out_specs=pl.BlockSpec((None, nbatch, cols), lambda l: (l, 0, 0)), out_shape=SDS((nl, nbatch, cols), F32),
        compiler_params=_params(("parallel",)))(c_all, mod_w, mod_b_cols)


def _mod_w_update(c_all, dmod_cols, w, m, v):
    nl, d, cols = w.shape
    nbatch = c_all.shape[0]
    tr = _tile(d, (256, 128))

    def body(c_ref, dm_ref, w_ref, m_ref, v_ref, g_ref, dl_ref, m2_ref, v2_ref):
        cv = c_ref[...]
        ca = (cv * jax.nn.sigmoid(cv)).astype(BF16)
        g = _dot(ca, dm_ref[...].astype(BF16), "tn")
        g_ref[...] = g
        dl_ref[...], m2_ref[...], v2_ref[...] = _adam(w_ref[...], g, m_ref[...], v_ref[...])

    wblk = pl.BlockSpec((None, tr, cols), lambda l, i: (l, i, 0))
    return _pcall(
        body, name="mod_w_update", grid=(nl, d // tr),
        in_specs=[pl.BlockSpec((nbatch, tr), lambda l, i: (0, i)), pl.BlockSpec((None, nbatch, cols), lambda l, i: (l, 0, 0)),
                  wblk, wblk, wblk],
        out_specs=[wblk] * 4, out_shape=[SDS(w.shape, F32)] * 4,
        compiler_params=_params(("parallel", "parallel")))(c_all, dmod_cols, w, m, v)


def _adam_update(name, w, m, v, gparts):
    rows, cols = w.shape
    tr = _tile(rows, (256, 128, 64, 32, 16, 8))
    npart = len(gparts)

    def body(*refs):
        w_ref, m_ref, v_ref = refs[:3]
        g_refs = refs[3:3 + npart]
        g_ref, dl_ref, m2_ref, v2_ref = refs[3 + npart:]
        g = g_refs[0][...].astype(F32)
        for gr in g_refs[1:]:
            g = g + gr[...].astype(F32)
        g_ref[...] = g
        dl_ref[...], m2_ref[...], v2_ref[...] = _adam(w_ref[...], g, m_ref[...], v_ref[...])

    blk = pl.BlockSpec((tr, cols), lambda i: (i, 0))
    return _pcall(body, name=name, grid=(rows // tr,), in_specs=[blk] * (3 + npart), out_specs=[blk] * 4,
                  out_shape=[SDS((rows, cols), F32)] * 4, compiler_params=_params(("parallel",)))(w, m, v, *gparts)


def _adam_shard(name, w, m, v, part4, recv3, chip_idx, first=0, fills=None):
    p, r, cdim = w.shape
    pg = part4.shape[0]
    tr = _tile(r, (256, 176, 160, 128, 64, 32, 16))

    def body(chip_ref, w_ref, m_ref, v_ref, own_ref, r0_ref, r1_ref, r2_ref, *rest):
        g_ref, dl_ref, m2_ref, v2_ref = rest[-4:]
        g = own_ref[...].astype(F32) + r0_ref[...].astype(F32) + r1_ref[...].astype(F32) + r2_ref[...].astype(F32)
        g_ref[...] = g
        dl_ref[...], m2_ref[...], v2_ref[...] = _adam(w_ref[...], g, m_ref[...], v_ref[...])

    blk = pl.BlockSpec((None, tr, cdim), lambda q, i, chip_ref: (first + q, i, 0))
    blk4 = (None, None, tr, cdim)
    slot = lambda s: pl.BlockSpec(blk4, lambda q, i, chip_ref: (s, q, i, 0))
    fills = list(fills or [])
    grid_spec = pltpu.PrefetchScalarGridSpec(
        num_scalar_prefetch=1, grid=(pg, r // tr),
        in_specs=[blk, blk, blk, pl.BlockSpec(blk4, lambda q, i, chip_ref: (q, chip_ref[0], i, 0)), slot(0), slot(1), slot(2)]
        + [ANY] * len(fills),
        out_specs=[blk] * 4)
    return _pcall(body, name=name, grid_spec=grid_spec, out_shape=[SDS((p, r, cdim), F32)] * 4,
                  input_output_aliases={8 + k: k for k in range(len(fills))},
                  compiler_params=_params(("parallel", "parallel")))(chip_idx, w, m, v, part4, recv3, recv3, recv3, *fills)


def _sum_devices(gathered, name):
    _, rows, cols = gathered.shape
    tr = _tile(rows, (512, 256, 128, 64, 32, 16, 8))

    def body(g_ref, o_ref):
        acc = g_ref[0].astype(F32)
        for k in range(1, N_DEV):
            acc = acc + g_ref[k].astype(F32)
        o_ref[...] = acc

    return _pcall(body, name=name, grid=(rows // tr,), in_specs=[pl.BlockSpec((N_DEV, tr, cols), lambda i: (0, i, 0))],
                  out_specs=pl.BlockSpec((tr, cols), lambda i: (i, 0)), out_shape=SDS((rows, cols), F32),
                  compiler_params=_params(("parallel",)))(gathered)


def _pack_flat(parts, width, row_mult, dtype):
    flat = jnp.concatenate([p.reshape(-1).astype(dtype) for p in parts])
    unit = width * row_mult
    pad = (-flat.shape[0]) % unit
    if pad:
        flat = jnp.concatenate([flat, jnp.zeros((pad,), dtype)])
    return flat.reshape(-1, width)


def _unpack_flat(flat, shapes):
    out, off = [], 0
    for shp in shapes:
        size = math.prod(shp)
        out.append(flat[off:off + size].reshape(shp))
        off += size
    return out


def kernel(x, c, mod_w, mod_b, norm_g, ffn_w_gu, ffn_w_down, sb_w_qkv, sb_w_o, lru_w_in, lru_conv_w, lru_conv_b, lru_w_r, lru_b_r, lru_w_i, lru_b_i, lru_lambda, lru_w_out, final_norm_g, loss_target, m_mod_w, m_mod_b, m_norm_g, m_ffn_w_gu, m_ffn_w_down, m_sb_w_qkv, m_sb_w_o, m_lru_w_in, m_lru_conv_w, m_lru_conv_b, m_lru_w_r, m_lru_b_r, m_lru_w_i, m_lru_b_i, m_lru_lambda, m_lru_w_out, m_final_norm_g, v_mod_w, v_mod_b, v_norm_g, v_ffn_w_gu, v_ffn_w_down, v_sb_w_qkv, v_sb_w_o, v_lru_w_in, v_lru_conv_w, v_lru_conv_b, v_lru_w_r, v_lru_b_r, v_lru_w_i, v_lru_b_i, v_lru_lambda, v_lru_w_out, v_final_norm_g):
    weights = dict(mod_w=mod_w, mod_b=mod_b, norm_g=norm_g, ffn_w_gu=ffn_w_gu, ffn_w_down=ffn_w_down, sb_w_qkv=sb_w_qkv,
                   sb_w_o=sb_w_o, lru_w_in=lru_w_in, lru_conv_w=lru_conv_w, lru_conv_b=lru_conv_b, lru_w_r=lru_w_r,
                   lru_b_r=lru_b_r, lru_w_i=lru_w_i, lru_b_i=lru_b_i, lru_lambda=lru_lambda, lru_w_out=lru_w_out,
                   final_norm_g=final_norm_g)
    mom_m = dict(mod_w=m_mod_w, mod_b=m_mod_b, norm_g=m_norm_g, ffn_w_gu=m_ffn_w_gu, ffn_w_down=m_ffn_w_down,
                 sb_w_qkv=m_sb_w_qkv, sb_w_o=m_sb_w_o, lru_w_in=m_lru_w_in, lru_conv_w=m_lru_conv_w,
                 lru_conv_b=m_lru_conv_b, lru_w_r=m_lru_w_r, lru_b_r=m_lru_b_r, lru_w_i=m_lru_w_i, lru_b_i=m_lru_b_i,
                 lru_lambda=m_lru_lambda, lru_w_out=m_lru_w_out, final_norm_g=m_final_norm_g)
    mom_v = dict(mod_w=v_mod_w, mod_b=v_mod_b, norm_g=v_norm_g, ffn_w_gu=v_ffn_w_gu, ffn_w_down=v_ffn_w_down,
                 sb_w_qkv=v_sb_w_qkv, sb_w_o=v_sb_w_o, lru_w_in=v_lru_w_in, lru_conv_w=v_lru_conv_w,
                 lru_conv_b=v_lru_conv_b, lru_w_r=v_lru_w_r, lru_b_r=v_lru_b_r, lru_w_i=v_lru_w_i, lru_b_i=v_lru_b_i,
                 lru_lambda=v_lru_lambda, lru_w_out=v_lru_w_out, final_norm_g=v_final_norm_g)
    names = list(weights)

    t, d = x.shape[1], x.shape[2]
    n_layers = mod_w.shape[0]
    r_dim = lru_w_out.shape[1] * N_DEV
    ng, rs = d // N_DEV, r_dim // N_DEV
    mod_cols = mod_w.shape[2]
    nblk = lru_w_r.shape[1]
    xi, yi, ci = _mesh_pos()
    me = 4 * xi + 2 * yi + ci
    chip = 2 * xi + yi
    x2, target = x.reshape(t, d), loss_target.reshape(t, d)

    lru_small_shard = jnp.concatenate([lru_conv_w[0], lru_conv_b, lru_b_r, lru_b_i, lru_lambda], axis=0)
    small1 = _pack_flat([c, norm_g, lru_small_shard], LANES, 8, F32)
    n_small1 = small1.shape[0]
    all1 = _allgather(small1[None], "gather_small").reshape(N_DEV, n_small1 * LANES)
    c_all = all1[:, :d]
    norm_full = jnp.transpose(all1[:, d:d + 6 * ng].reshape(N_DEV, n_layers, 3, ng), (1, 2, 0, 3)).reshape(n_layers, 3, d)
    lru_small = jnp.transpose(all1[:, d + 6 * ng:d + 6 * ng + 8 * rs].reshape(N_DEV, 8, rs), (1, 0, 2)).reshape(8, r_dim)

    mod_b_cols = lax.dynamic_slice_in_dim(mod_b, me * mod_cols, mod_cols, axis=1).reshape(n_layers, 1, mod_cols)
    mod_part = _mod_fwd(c_all, mod_w, mod_b_cols)

    assert sb_w_qkv.shape[0] == 1 and lru_w_in.shape[0] == 1, "one stick-breaking and one RG-LRU layer"
    n_ffn = 2 * n_layers
    fc = ffn_w_gu.shape[3]
    cw_in = lru_w_in.shape[2]
    pieces = {("ffn_w_gu", q): ffn_w_gu[q // 2, q % 2][None] for q in range(n_ffn)}
    pieces.update({("ffn_w_down", q): ffn_w_down[q // 2, q % 2][None] for q in range(n_ffn)})
    pieces.update({("sb_w_qkv", 0): sb_w_qkv, ("sb_w_o", 0): sb_w_o, ("lru_w_in", 0): lru_w_in, ("lru_w_out", 0): lru_w_out})
    col_window = {("sb_w_qkv", 0)}
    first = [("ffn_w_gu", 0)]
    behind = {"l0s0_gu": [("ffn_w_down", 0)], "l0s0_down": [("sb_w_qkv", 0), ("sb_w_o", 0)],
              "l0s2_gu": [("ffn_w_down", n_ffn - 1)], "l0s2_down": [("ffn_w_down", 2)],
              "l1s0_gu": [("lru_w_in", 0)], "l1s0_down": [("lru_w_out", 0)]}
    behind["sb_fwd"] = [key for key in pieces if key not in first + sum(behind.values(), [])]
    gathered = {}

    def gather_plan(keys):
        return _gather_plan([pieces[key].astype(BF16) for key in keys], [key in col_window for key in keys])

    def hosting(name, call):
        keys = behind.get(name, [])
        outs = call(gather_plan(keys) if keys else None)
        gathered.update(zip(keys, outs[len(outs) - len(keys):]))
        return outs[:len(outs) - len(keys)]

    mod_all, *landed = _run_comm(_merge_plans([_gather_plan([mod_part], [False]), gather_plan(first)]), "gather_mod_and_first")
    gathered.update(zip(first, landed))
    mod_mine = lax.dynamic_index_in_dim(mod_all, me, axis=2, keepdims=False)
    mod_mine = mod_mine.reshape(n_layers, 3, 3, d)
    wr_b, wi_b = lru_w_r[0].astype(BF16), lru_w_i[0].astype(BF16)
    eye2 = jnp.eye(2 * cw_in, dtype=BF16).reshape(2, cw_in, 2 * cw_in)

    def w_gu(q):
        return gathered[("ffn_w_gu", q)]

    def w_d4(q):
        return gathered[("ffn_w_down", q)].reshape(1, HIDDEN_CHUNKS, fc, d)

    saved = []
    xcur = x2
    for layer in range(n_layers):
        for sub in range(3):
            gvec = norm_full[layer, sub].reshape(1, d)
            shift = mod_mine[layer, sub, 0].reshape(1, d)
            scale1p = 1.0 + mod_mine[layer, sub, 1].reshape(1, d)
            gmul = 1.0 + mod_mine[layer, sub, 2].reshape(1, d)
            tag = f"l{layer}s{sub}"
            h = _norm_fwd(xcur, gvec, scale1p, shift, tag + "_norm")
            rec = dict(x=xcur, h=h, g=gvec, scale1p=scale1p, gmul=gmul, w=MACARON_W if sub != 1 else 1.0)
            if sub != 1:
                lj = layer * 2 + sub // 2
                gu2, a = hosting(tag + "_gu", lambda plan: _ffn_gu(tag + "_gu", h, w_gu(lj), plan))
                yv, xcur = hosting(tag + "_down", lambda plan: _ffn_down(tag + "_down", a, w_d4(lj), xcur, gmul, plan))
                rec.update(kind="ffn", lj=lj, gu2=gu2, a=a, y=yv)
            elif layer % 2 == 0:
                w_qkv = gathered[("sb_w_qkv", 0)][0]
                w_o = gathered[("sb_w_o", 0)].reshape(d, d)
                qkv = _mm_nn(tag + "_qkv", h, w_qkv, BF16)[0]
                o, ltot = hosting("sb_fwd", lambda plan: _sb_fwd(qkv, d, plan))
                yv, xcur = _mm_nn(tag + "_wo", o, w_o, [BF16, F32], extras=[(xcur, "tile"), (gmul, "row")],
                                  epilogue=lambda accs, ex: (accs[0], ex[0] + ex[1] * accs[0]))
                rec.update(kind="sb", qkv=qkv, o=o, ltot=ltot, y=yv, w_qkv=w_qkv, w_o=w_o)
            else:
                w_in = _chunks_to_cols("lru_w_in_cols", gathered[("lru_w_in", 0)][0], eye2)
                w_out = gathered[("lru_w_out", 0)].reshape(r_dim, d)
                gx = _mm_nn(tag + "_win", h, w_in, F32)[0]
                ymix, hs = _lru_fwd(gx, lru_small, wr_b, wi_b)
                yv, xcur = _mm_nn(tag + "_wout", ymix, w_out, [BF16, F32], extras=[(xcur, "tile"), (gmul, "row")],
                                  epilogue=lambda accs, ex: (accs[0], ex[0] + ex[1] * accs[0]))
                rec.update(kind="lru", gx=gx, hs=hs, ymix=ymix, y=yv, w_in=w_in, w_out=w_out)
            saved.append(rec)

    last = saved[-1]
    dxo, dy, head_sums = _loss_head(xcur, target, final_norm_g.reshape(1, d), (last["w"] * last["gmul"]))
    loss = lax.psum(head_sums[1, 0], ("x", "y", "c"))
    dgf = head_sums[0]

    c_idx = jnp.reshape(ci, (1,)).astype(jnp.int32)
    chip_idx = jnp.reshape(chip, (1,)).astype(jnp.int32)
    grads, reduced = {}, {}
    to_pair = []
    to_chips = []

    def sibling_plan():
        keys = list(to_pair)
        if not keys:
            return None, keys
        return _exchange_plan([grads[key] for key in keys], [key in col_window for key in keys], 4, _sibling_route), keys

    def sibling_done(keys, recv4):
        for key, r4 in zip(keys, recv4):
            to_pair.remove(key)
            to_chips.append((key, _pair_sum(grads[key], r4, c_idx, f"rs_pair_sum_{key[0]}{key[1]}", cols=key in col_window)))

    def chip_plan():
        items = list(to_chips)
        if not items:
            return None, items
        return _exchange_plan([p4 for _, p4 in items], [False] * len(items), 3, _chip_route), items

    def chips_done(items, recv3):
        for item, r3 in zip(items, recv3):
            to_chips.remove(item)
            reduced[item[0]] = (item[1], r3)

    def behind(call, make_plan, done, more=None):
        plan, items = make_plan()
        n_mine = len(plan.outs) if plan else 0
        n_more = len(more.outs) if more else 0
        outs = call(_merge_plans([plan, more]))
        n_own = len(outs) - n_mine - n_more
        done(items, outs[n_own:n_own + n_mine])
        return list(outs[:n_own]) + list(outs[n_own + n_mine:])

    def at_once(make_plan, done, name):
        plan, items = make_plan()
        if plan:
            done(items, _run_comm(plan, name))

    def add_grad(key, value):
        grads[key] = value
        to_pair.append(key)

    dmod = [[None] * 3 for _ in range(n_layers)]
    dnorm = [[None] * 3 for _ in range(n_layers)]
    dlru_small = wri_all = None
    for idx in reversed(range(len(saved))):
        rec = saved[idx]
        layer, sub = divmod(idx, 3)
        tag = f"l{layer}s{sub}b"
        if rec["kind"] == "ffn" and idx > 0:
            lj = rec["lj"]
            (dgu2,) = behind(lambda plan: _ffn_da(tag + "_da", dy, w_d4(lj), rec["gu2"], plan), sibling_plan, sibling_done)
            add_grad(("ffn_w_down", lj), _ffn_dwd(tag + "_dwd", rec["a"], dy)[0].reshape(gathered[("ffn_w_down", lj)].shape))
            add_grad(("ffn_w_gu", lj), _ffn_dwgu(tag + "_dwgu", rec["h"], dgu2)[0])
            dh = _ffn_dh(tag + "_dh", dgu2, w_gu(lj))[0]
        elif rec["kind"] == "ffn":
            lj = rec["lj"]
            at_once(sibling_plan, sibling_done, "rs_sibling_" + tag)
            (dgu2,) = behind(lambda plan: _ffn_da(tag + "_da", dy, w_d4(lj), rec["gu2"], plan), chip_plan, chips_done)
            add_grad(("ffn_w_down", lj), _ffn_dwd(tag + "_dwd", rec["a"], dy)[0].reshape(gathered[("ffn_w_down", lj)].shape))
            at_once(sibling_plan, sibling_done, "rs_sibling_" + tag + "_dwd")
            (dwgu,) = behind(lambda plan: _ffn_dwgu(tag + "_dwgu", rec["h"], dgu2, plan), chip_plan, chips_done)
            add_grad(("ffn_w_gu", lj), dwgu)
            at_once(sibling_plan, sibling_done, "rs_sibling_" + tag + "_dwgu")
            (dh,) = behind(lambda plan: _ffn_dh(tag + "_dh", dgu2, w_gu(lj), plan), chip_plan, chips_done)
        elif rec["kind"] == "sb":
            at_once(sibling_plan, sibling_done, "rs_sibling_" + tag)
            do = _mm_nt(tag + "_do", dy, rec["w_o"], BF16)
            dwo = _mm_tn(tag + "_dwo", rec["o"], dy, BF16)
            wri = _pack_flat([dwr, dwi], LANES, 512, BF16)[None]
            dqkv3, wri_all = behind(lambda plan: _sb_bwd(rec["qkv"], do, rec["ltot"], d, plan), chip_plan, chips_done,
                                    more=_gather_plan([wri], [False]))
            add_grad(("sb_w_o", 0), dwo.reshape(gathered[("sb_w_o", 0)].shape))
            dh = _mm_nt_stack(tag + "_dh", dqkv3, rec["w_qkv"], F32)
            add_grad(("sb_w_qkv", 0), _mm_tn_stack(tag + "_dwqkv", rec["h"], dqkv3, BF16)[None])
        else:
            dymix = _mm_nt(tag + "_dymix", dy, rec["w_out"], F32)
            add_grad(("lru_w_out", 0), _mm_tn(tag + "_dwout", rec["ymix"], dy, BF16).reshape(gathered[("lru_w_out", 0)].shape))
            dgx2, dlru_small, dwr, dwi = _lru_bwd(rec["gx"], rec["hs"], dymix, lru_small, wr_b, wi_b)
            dh = _mm_nt_stack(tag + "_dh", dgx2, rec["w_in"], F32)
            dw_in = _mm_tn_stack(tag + "_dwin", rec["h"], dgx2, BF16)
            add_grad(("lru_w_in", 0), _cols_to_chunks("lru_w_in_chunks", dw_in, eye2)[None])
        prev = saved[idx - 1] if idx > 0 else None
        gw_prev = (prev["w"] * prev["gmul"]) if prev is not None else jnp.zeros((1, d), F32)
        dxo, dy, sums = _adaln_bwd(dh, rec["x"], rec["y"], dxo, rec["g"], rec["scale1p"], rec["w"], gw_prev, tag + "_adaln")
        dmod[layer][sub] = sums[0:3]
        dnorm[layer][sub] = sums[3]
    grad_x = dxo.reshape(x.shape)

    dmod_mine = jnp.stack([jnp.stack(dmod[layer]) for layer in range(n_layers)])
    dnorm_mine = jnp.stack([jnp.stack(dnorm[layer]) for layer in range(n_layers)])
    assert not to_pair and not to_chips
    small_shapes = [(n_layers, 9 * d), (n_layers, 3, d), (8, r_dim), (d,)]
    small3 = _pack_flat([dmod_mine, dnorm_mine, dlru_small, dgf], LANES, 256, F32)
    n_small3 = small3.shape[0]
    all3 = _allgather(small3[None], "gather_small_grads").reshape(N_DEV, n_small3, LANES)
    gsum = _sum_devices(all3, "sum_small_grads").reshape(-1)
    g_mod_b, g_norm_full, g_lru_small, g_final = _unpack_flat(gsum, small_shapes)
    wri_sum = _sum_devices(wri_all.reshape(N_DEV, -1, LANES), "sum_gate_weight_grads").reshape(-1)
    g_wr, g_wi = _unpack_flat(wri_sum, [lru_w_r.shape, lru_w_i.shape])
    dmod_all = all3.reshape(N_DEV, -1)[:, :n_layers * 9 * d].reshape(N_DEV, n_layers, N_DEV, mod_cols)
    dmod_cols = jnp.transpose(lax.dynamic_index_in_dim(dmod_all, me, axis=2, keepdims=False), (1, 0, 2))

    out_g, out_d, out_m, out_v = {}, {}, {}, {}
    out_g["mod_w"], out_d["mod_w"], out_m["mod_w"], out_v["mod_w"] = _mod_w_update(c_all, dmod_cols, mod_w, m_mod_w, v_mod_w)

    g_norm_shard = lax.dynamic_slice_in_dim(g_norm_full, me * ng, ng, axis=2)
    g_lru_shard = lax.dynamic_slice_in_dim(g_lru_small, me * rs, rs, axis=1)
    small_grads = dict(mod_b=g_mod_b, norm_g=g_norm_shard, lru_conv_w=g_lru_shard[0:4].reshape(lru_conv_w.shape),
                       lru_conv_b=g_lru_shard[4:5], lru_b_r=g_lru_shard[5:6], lru_b_i=g_lru_shard[6:7],
                       lru_lambda=g_lru_shard[7:8], final_norm_g=g_final)
    for n, g in (("lru_w_r", g_wr), ("lru_w_i", g_wi)):
        view = lambda arr: arr.reshape(-1, LRU_BLOCK_W)
        outs = _adam_update("adam_" + n, view(weights[n]), view(mom_m[n]), view(mom_v[n]), [view(g)])
        out_g[n], out_d[n], out_m[n], out_v[n] = [o.reshape(weights[n].shape) for o in outs]
    small_names = list(small_grads)
    sw = _pack_flat([weights[n] for n in small_names], LANES, 256, F32)
    sg = _pack_flat([small_grads[n] for n in small_names], LANES, 256, F32)
    sm = _pack_flat([mom_m[n] for n in small_names], LANES, 256, F32)
    sv = _pack_flat([mom_v[n] for n in small_names], LANES, 256, F32)
    s_outs = _adam_update("adam_small", sw, sm, sv, [sg])
    small_shapes2 = [weights[n].shape for n in small_names]
    for dst, flat in zip((out_g, out_d, out_m, out_v), s_outs):
        for n, arr in zip(small_names, _unpack_flat(flat.reshape(-1), small_shapes2)):
            dst[n] = arr

    for n in ["ffn_w_gu", "ffn_w_down", "sb_w_qkv", "sb_w_o", "lru_w_in", "lru_w_out"]:
        shp = weights[n].shape
        shard3 = (math.prod(shp[:-2]),) + shp[-2:]
        view = lambda arr: arr.reshape(shard3)
        outs = None
        for q in range(shard3[0]):
            fills = outs if outs is not None else [lax.empty(shard3, F32) for _ in range(4)]
            p4, r3 = reduced[(n, q)]
            outs = _adam_shard(f"adam_{n}{q}", view(weights[n]), view(mom_m[n]), view(mom_v[n]), p4, r3, chip_idx,
                               first=q, fills=fills if shard3[0] > 1 else None)
        out_g[n], out_d[n], out_m[n], out_v[n] = [o.reshape(shp) for o in outs]

    return (loss, grad_x, *[out_g[n] for n in names], *[out_d[n] for n in names], *[out_m[n] for n in names],
            *[out_v[n] for n in names])
```

```python
import functools
import math

import jax
import jax.numpy as jnp
from jax import lax
from jax.experimental import pallas as pl
from jax.experimental.pallas import tpu as pltpu

F32 = jnp.float32
BF16 = jnp.bfloat16
SDS = jax.ShapeDtypeStruct
MESH = pl.DeviceIdType.MESH
ANY = pl.BlockSpec(memory_space=pl.ANY)

N_DEV = 8
LANES = 128
HEAD_DIM = 64
LRU_BLOCK_W = 128
LRU_C = 8.0
MACARON_W = 0.5
NORM_EPS = 1e-6
ADAM_LR = 0.001
ADAM_B1 = 0.9
ADAM_B2 = 0.999
ADAM_EPS = 1e-08
ADAM_WD = 0.01
ADAM_STEP = 10
VMEM_LIMIT = 56 * 1024 * 1024
GELU_C = math.sqrt(2.0 / math.pi)
GELU_K = 0.044715

DIMS = {
    "nn": (((1,), (0,)), ((), ())),
    "nt": (((1,), (1,)), ((), ())),
    "tn": (((0,), (0,)), ((), ())),
}


def _pcall(body, **kw):
    return pl.pallas_call(body, **kw)


def _params(sem=None):
    return pltpu.CompilerParams(dimension_semantics=sem, vmem_limit_bytes=VMEM_LIMIT)


def _tile(n, prefs):
    for p in prefs:
        if n % p == 0:
            return p
    return n


def _dot(a, b, dims):
    return lax.dot_general(a, b, DIMS[dims], preferred_element_type=F32)


def _softplus(z):
    return jnp.maximum(z, 0.0) + jnp.log(1.0 + jnp.exp(-jnp.abs(z)))


def _mesh_pos():
    return lax.axis_index("x"), lax.axis_index("y"), lax.axis_index("c")


def _allgather(xs, name, cols=False):
    return _run_comm(_gather_plan([xs], [cols]), name)[0]


class _CommPlan:
    def __init__(self, ins, outs, n_remote, n_local, phases):
        self.ins, self.outs, self.n_remote, self.n_local, self.phases = ins, outs, n_remote, n_local, phases

    def scratch(self):
        return [pltpu.SemaphoreType.DMA((self.n_remote,)), pltpu.SemaphoreType.DMA((self.n_remote,)),
                pltpu.SemaphoreType.DMA((max(self.n_local, 1),))]


def _merge_plans(plans):
    plans = [p for p in plans if p is not None]
    if len(plans) <= 1:
        return plans[0] if plans else None

    def phase(k):
        def run(in_refs, out_refs, send_sems, recv_sems, local_sems, r0=0, l0=0):
            i0 = o0 = 0
            for p in plans:
                p.phases[k](in_refs[i0:i0 + len(p.ins)], out_refs[o0:o0 + len(p.outs)], send_sems, recv_sems, local_sems, r0, l0)
                i0, o0, r0, l0 = i0 + len(p.ins), o0 + len(p.outs), r0 + p.n_remote, l0 + p.n_local
        return run

    return _CommPlan(sum([p.ins for p in plans], []), sum([p.outs for p in plans], []), sum(p.n_remote for p in plans),
                     sum(p.n_local for p in plans), [phase(0), phase(1), phase(2)])


def _run_comm(plan, name):
    n_in, n_out = len(plan.ins), len(plan.outs)

    def body(*refs):
        in_refs, out_refs, sems = refs[:n_in], refs[n_in:n_in + n_out], refs[n_in + n_out:]
        for phase in plan.phases:
            phase(in_refs, out_refs, *sems)

    return _pcall(body, name=name, out_shape=plan.outs, in_specs=[ANY] * n_in, out_specs=[ANY] * n_out,
                  scratch_shapes=plan.scratch())(*plan.ins)


def _col_window(ref, idx, width):
    return ref.at[:, :, pl.ds(pl.multiple_of(idx * width, math.gcd(width, LANES)), width)]


def _gather_plan(shards, cols):
    n = len(shards)
    outs = [SDS((s.shape[0], s.shape[1], N_DEV * s.shape[2]) if cl else (s.shape[0], N_DEV) + s.shape[1:], s.dtype)
            for s, cl in zip(shards, cols)]

    def copies(a, in_refs, out_refs, send_sems, recv_sems, local_sems, r0=0, l0=0):
        x, y, c = _mesh_pos()
        sibling = (x, y, 1 - c)
        chips = [(1 - x, y), (x, 1 - y), (1 - x, 1 - y)]
        width = shards[a].shape[2]

        def block(px, py, pc):
            idx = 4 * px + 2 * py + pc
            return _col_window(out_refs[a], idx, width) if cols[a] else out_refs[a].at[:, idx]

        def copy(k, owner, to, src=None):
            sem = r0 + 7 * a + k
            return pltpu.make_async_remote_copy(
                src_ref=block(*owner) if src is None else src, dst_ref=block(*owner),
                send_sem=send_sems.at[sem], recv_sem=recv_sems.at[sem], device_id=to, device_id_type=MESH)

        me = (x, y, c)
        first = [copy(0, me, sibling, src=in_refs[a])]
        first += [copy(1 + j, me, (*chip, c), src=in_refs[a]) for j, chip in enumerate(chips)]
        passed = [copy(4 + j, (*chip, c), sibling) for j, chip in enumerate(chips)]
        landed = [copy(1 + j, (*chip, c), me) for j, chip in enumerate(chips)]
        from_sibling = [copy(0, sibling, me)] + [copy(4 + j, (*chip, 1 - c), me) for j, chip in enumerate(chips)]
        mine = pltpu.make_async_copy(in_refs[a], block(*me), local_sems.at[l0 + a])
        return first, passed, landed, from_sibling, mine

    def start(*refs):
        for a in range(n):
            first, _, _, _, mine = copies(a, *refs)
            mine.start()
            for cp in first:
                cp.start()

    def pass_on(*refs):
        for a in range(n):
            _, passed, landed, _, _ = copies(a, *refs)
            for cp, fwd in zip(landed, passed):
                cp.wait_recv()
                fwd.start()

    def finish(*refs):
        for a in range(n):
            first, passed, _, from_sibling, mine = copies(a, *refs)
            for cp in from_sibling:
                cp.wait_recv()
            for cp in first + passed:
                cp.wait_send()
            mine.wait()

    return _CommPlan(list(shards), outs, 7 * n, n, [start, pass_on, finish])


def _exchange_plan(srcs, cols, n_slots, route):
    n = len(srcs)
    outs = []
    for g, cl in zip(srcs, cols):
        shard = (g.shape[0], g.shape[1], g.shape[2] // N_DEV) if cl else (g.shape[0],) + g.shape[2:]
        outs.append(SDS((n_slots,) + shard, g.dtype))

    def copies(in_refs, out_refs, send_sems, recv_sems, local_sems, r0=0, l0=0):
        x, y, c = _mesh_pos()
        made = []
        for a in range(n):
            for s in range(n_slots):
                chunk, target = route(x, y, c, s)
                src = _col_window(in_refs[a], chunk, outs[a].shape[3]) if cols[a] else in_refs[a].at[:, chunk]
                sem = r0 + a * n_slots + s
                made.append(pltpu.make_async_remote_copy(
                    src_ref=src, dst_ref=out_refs[a].at[s], send_sem=send_sems.at[sem], recv_sem=recv_sems.at[sem],
                    device_id=target, device_id_type=MESH))
        return made

    def start(*refs):
        for cp in copies(*refs):
            cp.start()

    def nothing(*refs):
        pass

    def finish(*refs):
        made = copies(*refs)
        for cp in made:
            cp.wait_recv()
        for cp in made:
            cp.wait_send()

    return _CommPlan(list(srcs), outs, n * n_slots, 0, [start, nothing, finish])


def _sibling_route(x, y, c, k):
    return 2 * k + 1 - c, (x, y, 1 - c)


def _chip_route(x, y, c, j):
    px, py = [(1 - x, y), (x, 1 - y), (1 - x, 1 - y)][j]
    return 2 * px + py, (px, py, c)


def _pair_sum(grads, recv4, c_idx, name, cols=False):
    _, p, r, cdim = recv4.shape
    tr = _tile(r, (512, 256, 176, 160, 128, 64, 32, 16))

    def body(c_ref, a_ref, b_ref, o_ref):
        o_ref[...] = (a_ref[...].astype(F32) + b_ref[...].astype(F32)).astype(o_ref.dtype)

    blk = (None, None, tr, cdim)
    if cols:
        own = pl.BlockSpec((None, tr, cdim), lambda k, q, i, c_ref: (q, i, 2 * k + c_ref[0]))
    else:
        own = pl.BlockSpec(blk, lambda k, q, i, c_ref: (q, 2 * k + c_ref[0], i, 0))
    grid_spec = pltpu.PrefetchScalarGridSpec(
        num_scalar_prefetch=1, grid=(4, p, r // tr),
        in_specs=[own, pl.BlockSpec(blk, lambda k, q, i, c_ref: (k, q, i, 0))],
        out_specs=pl.BlockSpec(blk, lambda k, q, i, c_ref: (q, k, i, 0)))
    return _pcall(body, name=name, grid_spec=grid_spec, out_shape=SDS((p, 4, r, cdim), grads.dtype),
                  compiler_params=_params(("parallel", "parallel", "parallel")))(c_idx, grads, recv4)


def _mm(name, ins, prods, n_acc, acc_shape, epi_idx, epilogue, out_shapes, out_specs, grid, dims, plan=None):
    n_in, n_out, nk = len(ins), len(out_shapes), grid[2]
    n_acc_refs = n_acc if nk > 1 else 0
    c_ins, c_outs = (plan.ins, plan.outs) if plan else ([], [])
    n_cin, n_cout = len(c_ins), len(c_outs)

    def body(*refs):
        in_refs, c_in = refs[:n_in], refs[n_in:n_in + n_cin]
        rest = refs[n_in + n_cin:]
        out_refs, c_out = rest[:n_out], rest[n_out:n_out + n_cout]
        rest = rest[n_out + n_cout:]
        acc_refs, sems = rest[:n_acc_refs], rest[n_acc_refs:]
        ids = [pl.program_id(axis) for axis in range(3)]
        if plan:
            @pl.when((ids[0] == 0) & (ids[1] == 0) & (ids[2] == 0))
            def _():
                plan.phases[0](c_in, c_out, *sems)

        def finish(accs):
            outs = epilogue(accs, [in_refs[i][...] for i in epi_idx])
            for o_ref, o in zip(out_refs, outs):
                if isinstance(o, tuple):
                    for plane, part in enumerate(o):
                        o_ref[plane] = part.astype(o_ref.dtype)
                else:
                    o_ref[...] = o.astype(o_ref.dtype)

        if nk == 1:
            accs = [None] * n_acc
            for ia, ib, iacc in prods:
                term = _dot(in_refs[ia][...], in_refs[ib][...], dims)
                accs[iacc] = term if accs[iacc] is None else accs[iacc] + term
            finish(accs)
        else:
            @pl.when(ids[2] == 0)
            def _():
                for acc in acc_refs:
                    acc[...] = jnp.zeros_like(acc)

            for ia, ib, iacc in prods:
                acc_refs[iacc][...] += _dot(in_refs[ia][...], in_refs[ib][...], dims)

            @pl.when(ids[2] == nk - 1)
            def _():
                finish([acc[...] for acc in acc_refs])

        if plan:
            @pl.when((ids[0] == grid[0] - 1) & (ids[1] == grid[1] - 1) & (ids[2] == nk - 1))
            def _():
                plan.phases[1](c_in, c_out, *sems)
                plan.phases[2](c_in, c_out, *sems)

    return _pcall(
        body, name=name, grid=grid, in_specs=[s for _, s in ins] + [ANY] * n_cin,
        out_specs=list(out_specs) + [ANY] * n_cout, out_shape=list(out_shapes) + list(c_outs),
        scratch_shapes=[pltpu.VMEM(acc_shape, F32) for _ in range(n_acc_refs)] + (plan.scratch() if plan else []),
        compiler_params=_params(("arbitrary",) * 3 if plan else ("parallel", "parallel", "arbitrary")),
    )(*[a for a, _ in ins], *c_ins)


def _plain(accs, _):
    return accs


def _mm_nn(name, a, b, out_dtype, extras=(), epilogue=_plain, n_out=1):
    m, kd = a.shape
    n = b.shape[1]
    tm, tn, tk = _tile(m, (1024, 512, 256, 128)), _tile(n, (640, 512, 256, 128)), _tile(kd, (1280, 1024, 512, 256, 128))
    ins = [(a, pl.BlockSpec((tm, tk), lambda i, j, k: (i, k))), (b, pl.BlockSpec((tk, tn), lambda i, j, k: (k, j)))]
    for arr, kind in extras:
        if kind == "tile":
            ins.append((arr, pl.BlockSpec((tm, tn), lambda i, j, k: (i, j))))
        else:
            ins.append((arr, pl.BlockSpec((1, tn), lambda i, j, k: (0, j))))
    dts = out_dtype if isinstance(out_dtype, (list, tuple)) else [out_dtype] * n_out
    return _mm(name, ins, [(0, 1, 0)], 1, (tm, tn), list(range(2, len(ins))), epilogue,
               [SDS((m, n), dt) for dt in dts], [pl.BlockSpec((tm, tn), lambda i, j, k: (i, j)) for _ in dts],
               (m // tm, n // tn, kd // tk), "nn")


def _mm_nt(name, a, b, out_dtype, plan=None):
    m, kd = a.shape
    n = b.shape[0]
    tm, tn, tk = _tile(m, (1024, 512, 256, 128)), _tile(n, (640, 512, 256, 128)), _tile(kd, (1024, 512, 256, 128))
    ins = [(a, pl.BlockSpec((tm, tk), lambda i, j, k: (i, k))), (b, pl.BlockSpec((tn, tk), lambda i, j, k: (j, k)))]
    outs = _mm(name, ins, [(0, 1, 0)], 1, (tm, tn), [], _plain, [SDS((m, n), out_dtype)],
               [pl.BlockSpec((tm, tn), lambda i, j, k: (i, j))], (m // tm, n // tn, kd // tk), "nt", plan=plan)
    return outs if plan else outs[0]


def _mm_tn(name, a, b, out_dtype):
    t, m = a.shape
    n = b.shape[1]
    tm, tn, tk = _tile(m, (640, 512, 256, 128)), _tile(n, (1024, 512, 256, 128)), t
    ins = [(a, pl.BlockSpec((tk, tm), lambda i, j, k: (k, i))), (b, pl.BlockSpec((tk, tn), lambda i, j, k: (k, j)))]
    return _mm(name, ins, [(0, 1, 0)], 1, (tm, tn), [], _plain, [SDS((m, n), out_dtype)],
               [pl.BlockSpec((tm, tn), lambda i, j, k: (i, j))], (m // tm, n // tn, t // tk), "tn")[0]


def _mm_nt_stack(name, a3, b, out_dtype):
    cc, m, kd = a3.shape
    n = b.shape[0]
    tm, tn, tk = _tile(m, (1024, 512, 256, 128)), _tile(n, (1024, 512, 256, 128)), _tile(kd, (1280, 1024, 512, 256, 128))
    nk = kd // tk
    ins = [(a3, pl.BlockSpec((None, tm, tk), lambda i, j, k: (k // nk, i, k % nk))),
           (b, pl.BlockSpec((tn, tk), lambda i, j, k: (j, k)))]
    return _mm(name, ins, [(0, 1, 0)], 1, (tm, tn), [], _plain, [SDS((m, n), out_dtype)],
               [pl.BlockSpec((tm, tn), lambda i, j, k: (i, j))], (m // tm, n // tn, cc * nk), "nt")[0]


def _mm_tn_stack(name, a, b3, out_dtype):
    t, m = a.shape
    cc, _, n = b3.shape
    tm, tn, tk = _tile(m, (512, 256, 128)), _tile(n, (1280, 1024, 512, 256, 128)), t
    nj = n // tn
    ins = [(a, pl.BlockSpec((tk, tm), lambda i, j, k: (k, i))),
           (b3, pl.BlockSpec((None, tk, tn), lambda i, j, k: (j // nj, k, j % nj)))]
    return _mm(name, ins, [(0, 1, 0)], 1, (tm, tn), [], _plain, [SDS((m, cc * n), out_dtype)],
               [pl.BlockSpec((tm, tn), lambda i, j, k: (i, j))], (m // tm, cc * nj, t // tk), "tn")[0]


def _chunks_to_cols(name, wc, eye2):
    nch, d, cw = wc.shape
    tm = _tile(d, (1024, 512, 256, 128))
    ins = [(wc, pl.BlockSpec((None, tm, cw), lambda i, j, k: (2 * j + k, i, 0))),
           (eye2, pl.BlockSpec((None, cw, 2 * cw), lambda i, j, k: (k, 0, 0)))]
    return _mm(name, ins, [(0, 1, 0)], 1, (tm, 2 * cw), [], _plain, [SDS((d, nch * cw), wc.dtype)],
               [pl.BlockSpec((tm, 2 * cw), lambda i, j, k: (i, j))], (d // tm, nch // 2, 2), "nn")[0]


def _cols_to_chunks(name, full, eye2):
    d, n = full.shape
    _, cw, _ = eye2.shape
    nch = n // cw
    tm = _tile(d, (1024, 512, 256, 128))
    ins = [(full, pl.BlockSpec((tm, 2 * cw), lambda i, j, k: (i, j // 2))),
           (eye2, pl.BlockSpec((None, cw, 2 * cw), lambda i, j, k: (j % 2, 0, 0)))]
    return _mm(name, ins, [(0, 1, 0)], 1, (tm, cw), [], _plain, [SDS((nch, d, cw), full.dtype)],
               [pl.BlockSpec((None, tm, cw), lambda i, j, k: (j, i, 0))], (d // tm, nch, 1), "nt")[0]


def _row_tile(t):
    return _tile(t, (256, 128, 64, 32, 16, 8))


def _norm_fwd(x, g, scale1p, shift, name):
    t, d = x.shape
    tr = _row_tile(t)

    def body(x_ref, g_ref, s_ref, b_ref, h_ref):
        xv = x_ref[...]
        inv = lax.rsqrt(jnp.mean(xv * xv, axis=-1, keepdims=True) + NORM_EPS)
        h_ref[...] = ((xv * inv) * g_ref[...] * s_ref[...] + b_ref[...]).astype(h_ref.dtype)

    vec = pl.BlockSpec((1, d), lambda i: (0, 0))
    return _pcall(body, name=name, grid=(t // tr,), in_specs=[pl.BlockSpec((tr, d), lambda i: (i, 0)), vec, vec, vec],
                  out_specs=pl.BlockSpec((tr, d), lambda i: (i, 0)), out_shape=SDS((t, d), BF16),
                  compiler_params=_params(("parallel",)))(x, g, scale1p, shift)


def _adaln_bwd(dh, x, y, dxo, g, scale1p, w_sub, gw_prev, name):
    t, d = x.shape
    tr = _row_tile(t)

    def body(dh_ref, x_ref, y_ref, dxo_ref, g_ref, s_ref, gw_ref, dx_ref, dyp_ref, sums_ref):
        i = pl.program_id(0)

        @pl.when(i == 0)
        def _():
            sums_ref[...] = jnp.zeros_like(sums_ref)

        xv, dhv, dxov = x_ref[...], dh_ref[...], dxo_ref[...]
        inv = lax.rsqrt(jnp.mean(xv * xv, axis=-1, keepdims=True) + NORM_EPS)
        xn = xv * inv
        gv = g_ref[...]
        dn = dhv * s_ref[...]
        dxn = dn * gv
        dx = inv * (dxn - xn * jnp.mean(dxn * xn, axis=-1, keepdims=True)) + dxov
        dx_ref[...] = dx
        dyp_ref[...] = (gw_ref[...] * dx).astype(dyp_ref.dtype)
        sums_ref[0:1, :] += jnp.sum(dhv, axis=0, keepdims=True)
        sums_ref[1:2, :] += jnp.sum(dhv * (xn * gv), axis=0, keepdims=True)
        sums_ref[2:3, :] += jnp.sum(w_sub * y_ref[...] * dxov, axis=0, keepdims=True)
        sums_ref[3:4, :] += jnp.sum(dn * xn, axis=0, keepdims=True)

    blk = pl.BlockSpec((tr, d), lambda i: (i, 0))
    vec = pl.BlockSpec((1, d), lambda i: (0, 0))
    return _pcall(
        body, name=name, grid=(t // tr,), in_specs=[blk, blk, blk, blk, vec, vec, vec],
        out_specs=[blk, blk, pl.BlockSpec((8, d), lambda i: (0, 0))],
        out_shape=[SDS((t, d), F32), SDS((t, d), BF16), SDS((8, d), F32)],
        compiler_params=_params(("arbitrary",)))(dh, x, y, dxo, g, scale1p, gw_prev)


def _loss_head(x, target, gf, gw_prev):
    t, d = x.shape
    tr = _row_tile(t)
    nt = t // tr

    def body(x_ref, tg_ref, g_ref, gw_ref, dx_ref, dyp_ref, sums_ref):
        i = pl.program_id(0)

        @pl.when(i == 0)
        def _():
            sums_ref[...] = jnp.zeros_like(sums_ref)

        xv = x_ref[...]
        inv = lax.rsqrt(jnp.mean(xv * xv, axis=-1, keepdims=True) + NORM_EPS)
        xn = xv * inv
        gv = g_ref[...]
        err = xn * gv - tg_ref[...]
        dyv = err * (1.0 / d)
        dxn = dyv * gv
        dx = inv * (dxn - xn * jnp.mean(dxn * xn, axis=-1, keepdims=True))
        dx_ref[...] = dx
        dyp_ref[...] = (gw_ref[...] * dx).astype(dyp_ref.dtype)
        sums_ref[0:1, :] += jnp.sum(dyv * xn, axis=0, keepdims=True)
        sums_ref[1:2, :] += jnp.sum(err * err, axis=0, keepdims=True)

        @pl.when(i == nt - 1)
        def _():
            tot = jnp.sum(sums_ref[1:2, :], axis=1, keepdims=True) * (0.5 / d)
            sums_ref[1:2, :] = jnp.broadcast_to(tot, (1, d))

    blk = pl.BlockSpec((tr, d), lambda i: (i, 0))
    vec = pl.BlockSpec((1, d), lambda i: (0, 0))
    return _pcall(
        body, name="loss_head", grid=(nt,), in_specs=[blk, blk, vec, vec],
        out_specs=[blk, blk, pl.BlockSpec((8, d), lambda i: (0, 0))],
        out_shape=[SDS((t, d), F32), SDS((t, d), BF16), SDS((8, d), F32)],
        compiler_params=_params(("arbitrary",)))(x, target, gf, gw_prev)


HIDDEN_CHUNKS = N_DEV // 2


def _ffn_tiles(t, d):
    return _tile(t, (1024, 512, 256, 128)), _tile(d, (1024, 512, 256, 128))


def _ffn_gu(name, h, wgu, plan=None):
    t, d = h.shape
    fc, nc = wgu.shape[3], HIDDEN_CHUNKS
    tm, _ = _ffn_tiles(t, d)

    def epi_gu(accs, _):
        gpre, up = accs
        return (gpre, up), gpre * jax.nn.sigmoid(gpre) * up

    wblk = (None, None, d, fc)
    ins = [(h, pl.BlockSpec((tm, d), lambda i, c, k: (i, 0))),
           (wgu, pl.BlockSpec(wblk, lambda i, c, k: (0, c, 0, 0))),
           (wgu, pl.BlockSpec(wblk, lambda i, c, k: (0, c + nc, 0, 0)))]
    return _mm(name, ins, [(0, 1, 0), (0, 2, 1)], 2, (tm, fc), [], epi_gu,
               [SDS((2, nc, t, fc), BF16), SDS((nc, t, fc), BF16)],
               [pl.BlockSpec((2, None, tm, fc), lambda i, c, k: (0, c, i, 0)),
                pl.BlockSpec((None, tm, fc), lambda i, c, k: (c, i, 0))],
               (t // tm, nc, 1), "nn", plan=plan)


def _ffn_down(name, a, wd4, x, gmul, plan=None):
    nc, t, fc = a.shape
    d = wd4.shape[3]
    tm, tn = _ffn_tiles(t, d)

    def epi_down(accs, ex):
        (yv,), (xv, gm) = accs, ex
        return yv, xv + MACARON_W * gm * yv

    ins = [(a, pl.BlockSpec((None, tm, fc), lambda i, j, k: (k, i, 0))),
           (wd4, pl.BlockSpec((None, None, fc, tn), lambda i, j, k: (0, k, 0, j))),
           (x, pl.BlockSpec((tm, tn), lambda i, j, k: (i, j))), (gmul, pl.BlockSpec((1, tn), lambda i, j, k: (0, j)))]
    oblk = pl.BlockSpec((tm, tn), lambda i, j, k: (i, j))
    return _mm(name, ins, [(0, 1, 0)], 1, (tm, tn), [2, 3], epi_down, [SDS((t, d), BF16), SDS((t, d), F32)],
               [oblk, oblk], (t // tm, d // tn, nc), "nn", plan=plan)


def _ffn_da(name, dy, wd4, gu2, plan=None):
    t, d = dy.shape
    _, nc, fc, _ = wd4.shape
    tm, _ = _ffn_tiles(t, d)

    def epi_da(accs, ex):
        (da,), (gu,) = accs, ex
        gpre, up = gu[0].astype(F32), gu[1].astype(F32)
        s = jax.nn.sigmoid(gpre)
        silu = gpre * s
        dg = da * up * (s * (1.0 + gpre * (1.0 - s)))
        return ((dg, da * silu),)

    gblk = pl.BlockSpec((2, None, tm, fc), lambda i, c, k: (0, c, i, 0))
    ins = [(dy, pl.BlockSpec((tm, d), lambda i, c, k: (i, 0))),
           (wd4, pl.BlockSpec((None, None, fc, d), lambda i, c, k: (0, c, 0, 0))), (gu2, gblk)]
    return _mm(name, ins, [(0, 1, 0)], 1, (tm, fc), [2], epi_da, [SDS((2, nc, t, fc), BF16)], [gblk],
               (t // tm, nc, 1), "nt", plan=plan)


def _ffn_dwd(name, a, dy, plan=None):
    nc, t, fc = a.shape
    d = dy.shape[1]
    _, tn = _ffn_tiles(t, d)
    ins = [(a, pl.BlockSpec((None, t, fc), lambda c, j, k: (c, 0, 0))), (dy, pl.BlockSpec((t, tn), lambda c, j, k: (0, j)))]
    return _mm(name, ins, [(0, 1, 0)], 1, (fc, tn), [], _plain, [SDS((1, nc, fc, d), BF16)],
               [pl.BlockSpec((None, None, fc, tn), lambda c, j, k: (0, c, 0, j))], (nc, d // tn, 1), "tn", plan=plan)


def _ffn_dwgu(name, h, dgu2, plan=None):
    t, d = h.shape
    _, nc, _, fc = dgu2.shape
    _, tn = _ffn_tiles(t, d)
    ins = [(h, pl.BlockSpec((t, tn), lambda i, c, k: (0, i))),
           (dgu2, pl.BlockSpec((None, None, t, fc), lambda i, c, k: (c // nc, c % nc, 0, 0)))]
    return _mm(name, ins, [(0, 1, 0)], 1, (tn, fc), [], _plain, [SDS((1, 2 * nc, d, fc), BF16)],
               [pl.BlockSpec((None, None, tn, fc), lambda i, c, k: (0, c, i, 0))], (d // tn, 2 * nc, 1), "tn", plan=plan)


def _ffn_dh(name, dgu2, wgu, plan=None):
    _, nc, t, fc = dgu2.shape
    d = wgu.shape[2]
    tm, tn = _ffn_tiles(t, d)
    ins = [(dgu2, pl.BlockSpec((None, None, tm, fc), lambda i, j, k: (k // nc, k % nc, i, 0))),
           (wgu, pl.BlockSpec((None, None, tn, fc), lambda i, j, k: (0, k, j, 0)))]
    return _mm(name, ins, [(0, 1, 0)], 1, (tm, tn), [], _plain, [SDS((t, d), F32)],
               [pl.BlockSpec((tm, tn), lambda i, j, k: (i, j))], (t // tm, d // tn, 2 * nc), "nt", plan=plan)


def _sb_block(t):
    return 256 if t >= 1024 else 128


SB_STRIP = 64


def _sb_strips(blk):
    strip = min(SB_STRIP, blk)
    row = lax.broadcasted_iota(jnp.int32, (strip, blk), 0)
    col = lax.broadcasted_iota(jnp.int32, (strip, blk), 1)
    return [(slice(r0, r0 + strip), col < row + r0) for r0 in range(0, blk, strip)]


def _host_call(core, name, steps, ins, in_specs, out_shapes, out_specs, scratch, plan):
    n_in, n_out, n_scr = len(ins), len(out_shapes), len(scratch)
    c_ins, c_outs = (plan.ins, plan.outs) if plan else ([], [])
    n_cin, n_cout = len(c_ins), len(c_outs)

    def body(*refs):
        in_refs, c_in = refs[:n_in], refs[n_in:n_in + n_cin]
        rest = refs[n_in + n_cin:]
        out_refs, c_out = rest[:n_out], rest[n_out:n_out + n_cout]
        rest = rest[n_out + n_cout:]
        scr, sems = rest[:n_scr], rest[n_scr:]
        step = pl.program_id(0)
        if plan:
            @pl.when(step == 0)
            def _():
                plan.phases[0](c_in, c_out, *sems)

        core(in_refs, out_refs, scr)
        if plan:
            @pl.when(step == steps - 1)
            def _():
                plan.phases[1](c_in, c_out, *sems)
                plan.phases[2](c_in, c_out, *sems)

    return _pcall(
        body, name=name, grid=(steps,), in_specs=list(in_specs) + [ANY] * n_cin,
        out_specs=list(out_specs) + [ANY] * n_cout, out_shape=list(out_shapes) + list(c_outs),
        scratch_shapes=list(scratch) + (plan.scratch() if plan else []),
        compiler_params=_params(("arbitrary",)))(*ins, *c_ins)


def _sb_fwd(qkv, d, plan=None):
    t = qkv.shape[0]
    blk = _sb_block(t)
    nq = t // blk
    npair = d // LANES
    scale = HEAD_DIM ** -0.5

    def body(in_refs, out_refs, scr):
        (q_ref, k_ref, v_ref), (o_ref, l_ref) = in_refs, out_refs
        tri_s = scr[0]
        hi_s, lo_s, w_s, zs_s = (scr[1 + 2 * k:3 + 2 * k] for k in range(4))
        lane = lax.broadcasted_iota(jnp.int32, (blk, LANES), 1)
        head0 = lane < HEAD_DIM
        row = lax.broadcasted_iota(jnp.int32, (blk, blk), 0)
        col = lax.broadcasted_iota(jnp.int32, (blk, blk), 1)
        tri_s[...] = (row > col).astype(BF16)
        strips = _sb_strips(blk)

        def both(qhs, kb, carries, masked):
            start = pl.multiple_of(kb * blk, blk)
            kv = k_ref[pl.ds(start, blk), :]
            vv = v_ref[pl.ds(start, blk), :]
            zs = [_dot(qh, kv, "nt") for qh in qhs]
            sums = []
            for hh in range(2):
                parts = []
                for rows, causal in strips:
                    zt = zs[hh][rows, :]
                    sp = _softplus(zt)
                    lk = jnp.where(causal, -sp, 0.0) if masked else -sp
                    hi = lk.astype(BF16)
                    hi_s[hh][rows, :] = hi
                    lo_s[hh][rows, :] = (lk - hi.astype(F32)).astype(BF16)
                    zs_s[hh][rows, :] = zt - sp
                    parts.append(jnp.sum(lk, axis=1, keepdims=True))
                sums.append(jnp.concatenate(parts, axis=0))
            laters = [_dot(hi_s[hh][...],tri_s[...], "nn") + _dot(lo_s[hh][...],tri_s[...], "nn") for hh in range(2)]
            for hh in range(2):
                cl = carries[hh][0]
                for rows, causal in strips:
                    logw = zs_s[hh][rows, :] + laters[hh][rows, :] + cl[rows, :]
                    if masked:
                        logw = jnp.where(causal, logw, -1e30)
                    w_s[hh][rows, :] = jnp.exp(logw).astype(BF16)
            return tuple((carries[hh][0] + sums[hh], carries[hh][1] + _dot(w_s[hh][...],vv, "nn")) for hh in range(2))

        def qblock(qi, _):
            qstart = pl.multiple_of(qi * blk, blk)
            qv = q_ref[pl.ds(qstart, blk), :] * scale
            qhs = [jnp.where(head0 if hh == 0 else ~head0, qv, jnp.zeros_like(qv)) for hh in range(2)]
            zero = (jnp.zeros((blk, 1), F32), jnp.zeros((blk, LANES), F32))
            outs = both(qhs, qi, (zero, zero), True)
            outs = lax.fori_loop(0, qi, lambda j, crs: both(qhs, qi - 1 - j, crs, False), outs)
            o_ref[pl.ds(qstart, blk), :] = jnp.where(head0, outs[0][1], outs[1][1]).astype(o_ref.dtype)
            l_ref[pl.ds(qstart, blk), :] = jnp.where(head0, outs[0][0], outs[1][0])
            return 0

        lax.fori_loop(0, nq, qblock, 0)

    tile_bf16, tile_f32 = pltpu.VMEM((blk, blk), BF16), pltpu.VMEM((blk, blk), F32)
    return _host_call(
        body, "sb_fwd", npair, [qkv, qkv, qkv],
        [pl.BlockSpec((t, LANES), lambda p: (0, p)), pl.BlockSpec((t, LANES), lambda p: (0, npair + p)),
         pl.BlockSpec((t, LANES), lambda p: (0, 2 * npair + p))],
        [SDS((t, d), BF16), SDS((t, d), F32)],
        [pl.BlockSpec((t, LANES), lambda p: (0, p)), pl.BlockSpec((t, LANES), lambda p: (0, p))],
        [tile_bf16] * 7 + [tile_f32] * 2, plan)


def _sb_bwd(qkv, do, ltot, d, plan=None):
    t = qkv.shape[0]
    blk = _sb_block(t)
    nq = t // blk
    npair = d // LANES
    scale = HEAD_DIM ** -0.5

    def body(in_refs, out_refs, scr):
        (q_ref, k_ref, v_ref, do_ref, l_ref), (out_ref,) = in_refs, out_refs
        dq_s, dk_s, dv_s, upto_s, before_s = scr[:5]
        hi_s, lo_s, w_s, dab_s, dzs_s, zs_s, da_s = (scr[5 + 4 * k:9 + 4 * k] for k in range(7))
        lane = lax.broadcasted_iota(jnp.int32, (blk, LANES), 1)
        head0 = lane < HEAD_DIM
        row = lax.broadcasted_iota(jnp.int32, (blk, blk), 0)
        col = lax.broadcasted_iota(jnp.int32, (blk, blk), 1)
        upto_s[...] = (row <= col).astype(BF16)
        before_s[...] = (row < col).astype(BF16)
        dk_s[...] = jnp.zeros_like(dk_s)
        dv_s[...] = jnp.zeros_like(dv_s)
        strips = _sb_strips(blk)

        def step(heads, kbs, maskeds, carries):
            chains = [(bi, hh) for bi in range(len(kbs)) for hh in range(2)]
            starts = [pl.multiple_of(kb * blk, blk) for kb in kbs]
            kvs = [k_ref[pl.ds(start, blk), :] for start in starts]
            vvs = [v_ref[pl.ds(start, blk), :] for start in starts]
            zs = [_dot(heads[hh][0], kvs[bi], "nt") for bi, hh in chains]
            dws = [_dot(heads[hh][1], vvs[bi], "nt") for bi, hh in chains]
            lk_sums, da_sums = [], []
            for c, (bi, hh) in enumerate(chains):
                parts = []
                for rows, causal in strips:
                    zt = zs[c][rows, :]
                    sp = _softplus(zt)
                    lk = jnp.where(causal, -sp, 0.0) if maskeds[bi] else -sp
                    hi = lk.astype(BF16)
                    hi_s[c][rows, :] = hi
                    lo_s[c][rows, :] = (lk - hi.astype(F32)).astype(BF16)
                    zs_s[c][rows, :] = zt - sp
                    parts.append(jnp.sum(lk, axis=1, keepdims=True))
                lk_sums.append(jnp.concatenate(parts, axis=0))
            cums = [_dot(hi_s[c][...], upto_s[...], "nn") + _dot(lo_s[c][...], upto_s[...], "nn") for c in range(len(chains))]
            for c, (bi, hh) in enumerate(chains):
                lt, plk = heads[hh][2], carries[hh][0]
                if bi == 1:
                    plk = plk + lk_sums[hh]
                parts = []
                for rows, causal in strips:
                    logw = zs_s[c][rows, :] + (lt[rows, :] - (plk[rows, :] + cums[c][rows, :]))
                    if maskeds[bi]:
                        logw = jnp.where(causal, logw, -1e30)
                    w = jnp.exp(logw)
                    w_s[c][rows, :] = w.astype(BF16)
                    da = dws[c][rows, :] * w
                    da_s[c][rows, :] = da
                    dab_s[c][rows, :] = da.astype(BF16)
                    parts.append(jnp.sum(da, axis=1, keepdims=True))
                da_sums.append(jnp.concatenate(parts, axis=0))
            pres = [_dot(dab_s[c][...], before_s[...], "nn") for c in range(len(chains))]
            for c, (bi, hh) in enumerate(chains):
                pda = carries[hh][1]
                if bi == 1:
                    pda = pda + da_sums[hh]
                for rows, causal in strips:
                    sig = jnp.exp(zs_s[c][rows, :])
                    da = da_s[c][rows, :]
                    dz = da * (1.0 - sig) - sig * (pda[rows, :] + pres[c][rows, :])
                    if maskeds[bi]:
                        dz = jnp.where(causal, dz, 0.0)
                    dzs_s[c][rows, :] = dz.astype(BF16)
            new = [list(carries[hh]) for hh in range(2)]
            for c, (bi, hh) in enumerate(chains):
                dk_s[kbs[bi]] += _dot(heads[hh][3], dzs_s[c][...], "nn")
                dv_s[kbs[bi]] += _dot(heads[hh][4], w_s[c][...], "nn")
                new[hh] = [new[hh][0] + lk_sums[c], new[hh][1] + da_sums[c], new[hh][2] + _dot(dzs_s[c][...], kvs[bi], "nn")]
            return tuple(tuple(cr) for cr in new)

        def qblock(qi, _):
            qstart = pl.multiple_of(qi * blk, blk)
            qv = q_ref[pl.ds(qstart, blk), :] * scale
            dov = do_ref[pl.ds(qstart, blk), :]
            lv = l_ref[pl.ds(qstart, blk), :]
            heads = []
            for hh in range(2):
                sel = head0 if hh == 0 else ~head0
                qh, doh = jnp.where(sel, qv, jnp.zeros_like(qv)), jnp.where(sel, dov, jnp.zeros_like(dov))
                heads.append((qh, doh, jnp.max(jnp.where(sel, lv, -jnp.inf), axis=1, keepdims=True),
                              qh.astype(F32).T.astype(BF16), doh.astype(F32).T.astype(BF16)))
            zero = (jnp.zeros((blk, 1), F32), jnp.zeros((blk, 1), F32), jnp.zeros((blk, LANES), F32))
            carries = lax.fori_loop(0, qi // 2, lambda j, crs: step(heads, [2 * j, 2 * j + 1], [False, False], crs),
                                    (zero, zero))
            carries = lax.cond(qi % 2 == 1,
                               lambda crs: step(heads, [qi - 1, qi], [False, True], crs),
                               lambda crs: step(heads, [qi], [True], crs), carries)
            dq_s[pl.ds(qstart, blk), :] = jnp.where(head0, carries[0][2], carries[1][2]) * scale
            return 0

        lax.fori_loop(0, nq, qblock, 0)
        out_ref[0] = dq_s[...].astype(out_ref.dtype)
        for b in range(nq):
            out_ref[1, b * blk:(b + 1) * blk, :] = dk_s[b].T.astype(out_ref.dtype)
            out_ref[2, b * blk:(b + 1) * blk, :] = dv_s[b].T.astype(out_ref.dtype)

    col_blk = lambda off: pl.BlockSpec((t, LANES), lambda p: (0, off + p))
    return _host_call(
        body, "sb_bwd", npair, [qkv, qkv, qkv, do, ltot],
        [col_blk(0), col_blk(npair), col_blk(2 * npair), col_blk(0), col_blk(0)],
        [SDS((3, t, d), BF16)], [pl.BlockSpec((3, t, LANES), lambda p: (0, 0, p))],
        [pltpu.VMEM((t, LANES), F32)] + [pltpu.VMEM((nq, LANES, blk), F32) for _ in range(2)]
        + [pltpu.VMEM((blk, blk), BF16) for _ in range(2 + 20)]
        + [pltpu.VMEM((blk, blk), F32) for _ in range(8)], plan)


def _roll_rows(v, shift):
    return pltpu.roll(v, shift, 0)


def _shift_down(v, dist, fill, row):
    return jnp.where(row >= dist, _roll_rows(v, dist), fill)


def _shift_up(v, dist, fill, row):
    t = v.shape[0]
    return jnp.where(row < t - dist, _roll_rows(v, t - dist), fill)


def _lru_gates(xb, small, wr, wi, row):
    xs = [_shift_down(xb, 3 - tap, 0.0, row) if tap < 3 else xb for tap in range(4)]
    xc = small[4:5, :] + xs[0] * small[0:1, :]
    for tap in range(1, 4):
        xc = xc + xs[tap] * small[tap:tap + 1, :]
    xcb = xc.astype(BF16)
    r = jax.nn.sigmoid(_dot(xcb, wr, "nn") + small[5:6, :])
    ig = jax.nn.sigmoid(_dot(xcb, wi, "nn") + small[6:7, :])
    sp = _softplus(-small[7:8, :])
    la = -LRU_C * r * sp
    a = jnp.exp(la)
    th = jnp.tanh(la)
    mult = jnp.sqrt(-2.0 * th / (1.0 - th))
    return xs, xc, xcb, r, ig, sp, a, mult


def _gelu_parts(gate):
    inner = GELU_C * (gate + GELU_K * gate * gate * gate)
    th = jnp.tanh(inner)
    gelu = 0.5 * gate * (1.0 + th)
    dgelu = 0.5 * (1.0 + th) + 0.5 * gate * (1.0 - th * th) * GELU_C * (1.0 + 3.0 * GELU_K * gate * gate)
    return gelu, dgelu


def _scan_steps(t):
    steps, dist = [], 1
    while dist < t:
        steps.append(dist)
        dist *= 2
    return steps


SUBLANES = 8


def _linear_scan(a, b, scratch, row, reverse):
    a_s, b_s, carry_s = scratch
    t = a.shape[0]
    groups = t // SUBLANES
    in_group = row & (SUBLANES - 1)
    for dist in _scan_steps(SUBLANES):
        if reverse:
            inside = in_group < SUBLANES - dist
            b = b + a * jnp.where(inside, _roll_rows(b, t - dist), 0.0)
            a = a * jnp.where(inside, _roll_rows(a, t - dist), 1.0)
        else:
            inside = in_group >= dist
            b = a * jnp.where(inside, _roll_rows(b, dist), 0.0) + b
            a = a * jnp.where(inside, _roll_rows(a, dist), 1.0)
    a_s[...] = a
    b_s[...] = b
    end = 0 if reverse else SUBLANES - 1
    ends = pl.ds(end, groups, stride=SUBLANES)
    ae, be = a_s[ends, :], b_s[ends, :]
    grow = lax.broadcasted_iota(jnp.int32, ae.shape, 0)
    shift = _shift_up if reverse else _shift_down
    for dist in _scan_steps(groups):
        be = ae * shift(be, dist, 0.0, grow) + be
        ae = ae * shift(ae, dist, 1.0, grow)
    incoming = shift(be, 1, 0.0, grow)
    for k in range(SUBLANES):
        carry_s[pl.ds(k, groups, stride=SUBLANES), :] = incoming
    return a_s[...] * carry_s[...] + b_s[...]


def _lru_fwd(gx, small, wr, wi):
    t = gx.shape[0]
    r_dim = gx.shape[1] // 2
    nb = r_dim // LRU_BLOCK_W

    def body(gate_ref, xb_ref, small_ref, wr_ref, wi_ref, y_ref, hs_ref, *scratch):
        row = lax.broadcasted_iota(jnp.int32, (t, LRU_BLOCK_W), 0)
        xb = xb_ref[...]
        _, xc, _, _, ig, _, a, mult = _lru_gates(xb, small_ref, wr_ref[...], wi_ref[...], row)
        hsv = _linear_scan(a, mult * (ig * xc), scratch, row, reverse=False)
        hs_ref[...] = hsv
        gelu, _ = _gelu_parts(gate_ref[...])
        y_ref[...] = (gelu * hsv).astype(y_ref.dtype)

    colb = lambda off: pl.BlockSpec((t, LRU_BLOCK_W), lambda n: (0, off + n))
    wspec = pl.BlockSpec((None, LRU_BLOCK_W, LRU_BLOCK_W), lambda n: (n, 0, 0))
    return _pcall(
        body, name="lru_fwd", grid=(nb,),
        in_specs=[colb(0), colb(nb), pl.BlockSpec((8, LRU_BLOCK_W), lambda n: (0, n)), wspec, wspec],
        out_specs=[colb(0), colb(0)], out_shape=[SDS((t, r_dim), BF16), SDS((t, r_dim), F32)],
        scratch_shapes=[pltpu.VMEM((t, LRU_BLOCK_W), F32) for _ in range(3)],
        compiler_params=_params(("parallel",)))(gx, gx, small, wr, wi)


def _lru_bwd(gx, hs, dy, small, wr, wi, plan=None):
    t = gx.shape[0]
    r_dim = gx.shape[1] // 2
    nb = r_dim // LRU_BLOCK_W

    def body(in_refs, out_refs, scratch):
        (gate_ref, xb_ref, hs_ref, dy_ref, small_ref, wr_ref, wi_ref), (dgx_ref, dsm_ref, dwr_ref, dwi_ref) = in_refs, out_refs
        row = lax.broadcasted_iota(jnp.int32, (t, LRU_BLOCK_W), 0)
        xb, hsv, dyv, smallv = xb_ref[...], hs_ref[...], dy_ref[...], small_ref
        wrv, wiv = wr_ref[...], wi_ref[...]
        xs, xc, xcb, r, ig, sp, a, mult = _lru_gates(xb, smallv, wrv, wiv, row)
        gelu, dgelu = _gelu_parts(gate_ref[...])
        dgx_ref[0] = (dyv * hsv * dgelu).astype(dgx_ref.dtype)
        dacc = _linear_scan(_shift_up(a, 1, 1.0, row), dyv * gelu, scratch, row, reverse=True)
        da = dacc * _shift_down(hsv, 1, 0.0, row)
        dmult = dacc * (ig * xc)
        dixc = dacc * mult
        dla = da * a - dmult * (a * a) / mult
        dr = dla * (-LRU_C * sp)
        dsp = jnp.sum(dla * (-LRU_C * r), axis=0, keepdims=True)
        dpr = dr * r * (1.0 - r)
        dpi = dixc * xc * ig * (1.0 - ig)
        dprb, dpib = dpr.astype(BF16), dpi.astype(BF16)
        dwr_ref[...] = _dot(xcb, dprb, "tn")
        dwi_ref[...] = _dot(xcb, dpib, "tn")
        dxc = dixc * ig + _dot(dprb, wrv, "nt") + _dot(dpib, wiv, "nt")
        dxb = dxc * smallv[3:4, :]
        for tap in range(3):
            dxb = dxb + _shift_up(dxc, 3 - tap, 0.0, row) * smallv[tap:tap + 1, :]
        dgx_ref[1] = dxb.astype(dgx_ref.dtype)
        lam = smallv[7:8, :]
        rows = [jnp.sum(dxc * xs[tap], axis=0, keepdims=True) for tap in range(4)]
        rows.append(jnp.sum(dxc, axis=0, keepdims=True))
        rows.append(jnp.sum(dpr, axis=0, keepdims=True))
        rows.append(jnp.sum(dpi, axis=0, keepdims=True))
        rows.append(-dsp * jax.nn.sigmoid(-lam))
        for k, rv in enumerate(rows):
            dsm_ref[k:k + 1, :] = rv

    colb = lambda off: pl.BlockSpec((t, LRU_BLOCK_W), lambda n: (0, off + n))
    wspec = pl.BlockSpec((None, LRU_BLOCK_W, LRU_BLOCK_W), lambda n: (n, 0, 0))
    sspec = pl.BlockSpec((8, LRU_BLOCK_W), lambda n: (0, n))
    return _host_call(
        body, "lru_bwd", nb, [gx, gx, hs, dy, small, wr, wi],
        [colb(0), colb(nb), colb(0), colb(0), sspec, wspec, wspec],
        [SDS((2, t, r_dim), BF16), SDS((8, r_dim), F32), SDS((nb, LRU_BLOCK_W, LRU_BLOCK_W), F32),
         SDS((nb, LRU_BLOCK_W, LRU_BLOCK_W), F32)],
        [pl.BlockSpec((2, t, LRU_BLOCK_W), lambda n: (0, 0, n)), sspec, wspec, wspec],
        [pltpu.VMEM((t, LRU_BLOCK_W), F32) for _ in range(3)], plan)


def _adam(w, g, m, v):
    m2 = ADAM_B1 * m + (1.0 - ADAM_B1) * g
    v2 = ADAM_B2 * v + (1.0 - ADAM_B2) * (g * g)
    m_hat = m2 / (1.0 - ADAM_B1 ** ADAM_STEP)
    v_hat = v2 / (1.0 - ADAM_B2 ** ADAM_STEP)
    return -ADAM_LR * (m_hat / (jnp.sqrt(v_hat) + ADAM_EPS) + ADAM_WD * w), m2, v2


def _mod_fwd(c_all, mod_w, mod_b_cols):
    nl, d, cols = mod_w.shape
    nbatch = c_all.shape[0]

    def body(c_ref, w_ref, b_ref, o_ref):
        cv = c_ref[...]
        ca = (cv * jax.nn.sigmoid(cv)).astype(BF16)
        o_ref[...] = _dot(ca, w_ref[...].astype(BF16), "nn") + b_ref[...]

    return _pcall(
        body, name="mod_fwd", grid=(nl,),
        in_specs=[pl.BlockSpec((nbatch, d), lambda l: (0, 0)), pl.BlockSpec((None, d, cols), lambda l: (l, 0, 0)),
                  pl.BlockSpec((None, 1, cols), lambda l: (l, 0, 0))],
        out_specs=pl.BlockSpec((None, nbatch, cols), lambda l: (l, 0, 0)), out_shape=SDS((nl, nbatch, cols), F32),
        compiler_params=_params(("parallel",)))(c_all, mod_w, mod_b_cols)


def _mod_w_update(c_all, dmod_cols, w, m, v):
    nl, d, cols = w.shape
    nbatch = c_all.shape[0]
    tr = _tile(d, (256, 128))

    def body(c_ref, dm_ref, w_ref, m_ref, v_ref, g_ref, dl_ref, m2_ref, v2_ref):
        cv = c_ref[...]
        ca = (cv * jax.nn.sigmoid(cv)).astype(BF16)
        g = _dot(ca, dm_ref[...].astype(BF16), "tn")
        g_ref[...] = g
        dl_ref[...], m2_ref[...], v2_ref[...] = _adam(w_ref[...], g, m_ref[...], v_ref[...])

    wblk = pl.BlockSpec((None, tr, cols), lambda l, i: (l, i, 0))
    return _pcall(
        body, name="mod_w_update", grid=(nl, d // tr),
        in_specs=[pl.BlockSpec((nbatch, tr), lambda l, i: (0, i)), pl.BlockSpec((None, nbatch, cols), lambda l, i: (l, 0, 0)),
                  wblk, wblk, wblk],
        out_specs=[wblk] * 4, out_shape=[SDS(w.shape, F32)] * 4,
        compiler_params=_params(("parallel", "parallel")))(c_all, dmod_cols, w, m, v)


def _adam_update(name, w, m, v, gparts):
    rows, cols = w.shape
    tr = _tile(rows, (256, 128, 64, 32, 16, 8))
    npart = len(gparts)

    def body(*refs):
        w_ref, m_ref, v_ref = refs[:3]
        g_refs = refs[3:3 + npart]
        g_ref, dl_ref, m2_ref, v2_ref = refs[3 + npart:]
        g = g_refs[0][...].astype(F32)
        for gr in g_refs[1:]:
            g = g + gr[...].astype(F32)
        g_ref[...] = g
        dl_ref[...], m2_ref[...], v2_ref[...] = _adam(w_ref[...], g, m_ref[...], v_ref[...])

    blk = pl.BlockSpec((tr, cols), lambda i: (i, 0))
    return _pcall(body, name=name, grid=(rows // tr,), in_specs=[blk] * (3 + npart), out_specs=[blk] * 4,
                  out_shape=[SDS((rows, cols), F32)] * 4, compiler_params=_params(("parallel",)))(w, m, v, *gparts)


def _adam_shard(name, w, m, v, part4, recv3, chip_idx, first=0, fills=None):
    p, r, cdim = w.shape
    pg = part4.shape[0]
    tr = _tile(r, (256, 176, 160, 128, 64, 32, 16))

    def body(chip_ref, w_ref, m_ref, v_ref, own_ref, r0_ref, r1_ref, r2_ref, *rest):
        g_ref, dl_ref, m2_ref, v2_ref = rest[-4:]
        g = own_ref[...].astype(F32) + r0_ref[...].astype(F32) + r1_ref[...].astype(F32) + r2_ref[...].astype(F32)
        g_ref[...] = g
        dl_ref[...], m2_ref[...], v2_ref[...] = _adam(w_ref[...], g, m_ref[...], v_ref[...])

    blk = pl.BlockSpec((None, tr, cdim), lambda q, i, chip_ref: (first + q, i, 0))
    blk4 = (None, None, tr, cdim)
    slot = lambda s: pl.BlockSpec(blk4, lambda q, i, chip_ref: (s, q, i, 0))
    fills = list(fills or [])
    grid_spec = pltpu.PrefetchScalarGridSpec(
        num_scalar_prefetch=1, grid=(pg, r // tr),
        in_specs=[blk, blk, blk, pl.BlockSpec(blk4, lambda q, i, chip_ref: (q, chip_ref[0], i, 0)), slot(0), slot(1), slot(2)]
        + [ANY] * len(fills),
        out_specs=[blk] * 4)
    return _pcall(body, name=name, grid_spec=grid_spec, out_shape=[SDS((p, r, cdim), F32)] * 4,
                  input_output_aliases={8 + k: k for k in range(len(fills))},
                  compiler_params=_params(("parallel", "parallel")))(chip_idx, w, m, v, part4, recv3, recv3, recv3, *fills)


def _sum_devices(gathered, name):
    _, rows, cols = gathered.shape
    tr = _tile(rows, (512, 256, 128, 64, 32, 16, 8))

    def body(g_ref, o_ref):
        acc = g_ref[0].astype(F32)
        for k in range(1, N_DEV):
            acc = acc + g_ref[k].astype(F32)
        o_ref[...] = acc

    return _pcall(body, name=name, grid=(rows // tr,), in_specs=[pl.BlockSpec((N_DEV, tr, cols), lambda i: (0, i, 0))],
                  out_specs=pl.BlockSpec((tr, cols), lambda i: (i, 0)), out_shape=SDS((rows, cols), F32),
                  compiler_params=_params(("parallel",)))(gathered)


def _pack_flat(parts, width, row_mult, dtype):
    flat = jnp.concatenate([p.reshape(-1).astype(dtype) for p in parts])
    unit = width * row_mult
    pad = (-flat.shape[0]) % unit
    if pad:
        flat = jnp.concatenate([flat, jnp.zeros((pad,), dtype)])
    return flat.reshape(-1, width)


def _unpack_flat(flat, shapes):
    out, off = [], 0
    for shp in shapes:
        size = math.prod(shp)
        out.append(flat[off:off + size].reshape(shp))
        off += size
    return out


def kernel(x, c, mod_w, mod_b, norm_g, ffn_w_gu, ffn_w_down, sb_w_qkv, sb_w_o, lru_w_in, lru_conv_w, lru_conv_b, lru_w_r, lru_b_r, lru_w_i, lru_b_i, lru_lambda, lru_w_out, final_norm_g, loss_target, m_mod_w, m_mod_b, m_norm_g, m_ffn_w_gu, m_ffn_w_down, m_sb_w_qkv, m_sb_w_o, m_lru_w_in, m_lru_conv_w, m_lru_conv_b, m_lru_w_r, m_lru_b_r, m_lru_w_i, m_lru_b_i, m_lru_lambda, m_lru_w_out, m_final_norm_g, v_mod_w, v_mod_b, v_norm_g, v_ffn_w_gu, v_ffn_w_down, v_sb_w_qkv, v_sb_w_o, v_lru_w_in, v_lru_conv_w, v_lru_conv_b, v_lru_w_r, v_lru_b_r, v_lru_w_i, v_lru_b_i, v_lru_lambda, v_lru_w_out, v_final_norm_g):
    weights = dict(mod_w=mod_w, mod_b=mod_b, norm_g=norm_g, ffn_w_gu=ffn_w_gu, ffn_w_down=ffn_w_down, sb_w_qkv=sb_w_qkv,
                   sb_w_o=sb_w_o, lru_w_in=lru_w_in, lru_conv_w=lru_conv_w, lru_conv_b=lru_conv_b, lru_w_r=lru_w_r,
                   lru_b_r=lru_b_r, lru_w_i=lru_w_i, lru_b_i=lru_b_i, lru_lambda=lru_lambda, lru_w_out=lru_w_out,
                   final_norm_g=final_norm_g)
    mom_m = dict(mod_w=m_mod_w, mod_b=m_mod_b, norm_g=m_norm_g, ffn_w_gu=m_ffn_w_gu, ffn_w_down=m_ffn_w_down,
                 sb_w_qkv=m_sb_w_qkv, sb_w_o=m_sb_w_o, lru_w_in=m_lru_w_in, lru_conv_w=m_lru_conv_w,
                 lru_conv_b=m_lru_conv_b, lru_w_r=m_lru_w_r, lru_b_r=m_lru_b_r, lru_w_i=m_lru_w_i, lru_b_i=m_lru_b_i,
                 lru_lambda=m_lru_lambda, lru_w_out=m_lru_w_out, final_norm_g=m_final_norm_g)
    mom_v = dict(mod_w=v_mod_w, mod_b=v_mod_b, norm_g=v_norm_g, ffn_w_gu=v_ffn_w_gu, ffn_w_down=v_ffn_w_down,
                 sb_w_qkv=v_sb_w_qkv, sb_w_o=v_sb_w_o, lru_w_in=v_lru_w_in, lru_conv_w=v_lru_conv_w,
                 lru_conv_b=v_lru_conv_b, lru_w_r=v_lru_w_r, lru_b_r=v_lru_b_r, lru_w_i=v_lru_w_i, lru_b_i=v_lru_b_i,
                 lru_lambda=v_lru_lambda, lru_w_out=v_lru_w_out, final_norm_g=v_final_norm_g)
    names = list(weights)

    t, d = x.shape[1], x.shape[2]
    n_layers = mod_w.shape[0]
    r_dim = lru_w_out.shape[1] * N_DEV
    ng, rs = d // N_DEV, r_dim // N_DEV
    mod_cols = mod_w.shape[2]
    nblk = lru_w_r.shape[1]
    xi, yi, ci = _mesh_pos()
    me = 4 * xi + 2 * yi + ci
    chip = 2 * xi + yi
    x2, target = x.reshape(t, d), loss_target.reshape(t, d)

    lru_small_shard = jnp.concatenate([lru_conv_w[0], lru_conv_b, lru_b_r, lru_b_i, lru_lambda], axis=0)
    small1 = _pack_flat([c, norm_g, lru_small_shard], LANES, 8, F32)
    n_small1 = small1.shape[0]
    all1 = _allgather(small1[None], "gather_small").reshape(N_DEV, n_small1 * LANES)
    c_all = all1[:, :d]
    norm_full = jnp.transpose(all1[:, d:d + 6 * ng].reshape(N_DEV, n_layers, 3, ng), (1, 2, 0, 3)).reshape(n_layers, 3, d)
    lru_small = jnp.transpose(all1[:, d + 6 * ng:d + 6 * ng + 8 * rs].reshape(N_DEV, 8, rs), (1, 0, 2)).reshape(8, r_dim)

    mod_b_cols = lax.dynamic_slice_in_dim(mod_b, me * mod_cols, mod_cols, axis=1).reshape(n_layers, 1, mod_cols)
    mod_part = _mod_fwd(c_all, mod_w, mod_b_cols)

    assert sb_w_qkv.shape[0] == 1 and lru_w_in.shape[0] == 1, "one stick-breaking and one RG-LRU layer"
    n_ffn = 2 * n_layers
    fc = ffn_w_gu.shape[3]
    cw_in = lru_w_in.shape[2]
    pieces = {("ffn_w_gu", q): ffn_w_gu[q // 2, q % 2][None] for q in range(n_ffn)}
    pieces.update({("ffn_w_down", q): ffn_w_down[q // 2, q % 2][None] for q in range(n_ffn)})
    pieces.update({("sb_w_qkv", 0): sb_w_qkv, ("sb_w_o", 0): sb_w_o, ("lru_w_in", 0): lru_w_in, ("lru_w_out", 0): lru_w_out})
    col_window = {("sb_w_qkv", 0)}
    first = [("ffn_w_gu", 0)]
    behind = {"l0s0_gu": [("ffn_w_down", 0)], "l0s0_down": [("sb_w_qkv", 0), ("sb_w_o", 0)],
              "l0s2_gu": [("ffn_w_down", n_ffn - 1)], "l0s2_down": [("ffn_w_down", 2)],
              "l1s0_gu": [("lru_w_in", 0)], "l1s0_down": [("lru_w_out", 0)]}
    behind["sb_fwd"] = [key for key in pieces if key not in first + sum(behind.values(), [])]
    gathered = {}

    def gather_plan(keys):
        return _gather_plan([pieces[key].astype(BF16) for key in keys], [key in col_window for key in keys])

    def hosting(name, call):
        keys = behind.get(name, [])
        outs = call(gather_plan(keys) if keys else None)
        gathered.update(zip(keys, outs[len(outs) - len(keys):]))
        return outs[:len(outs) - len(keys)]

    mod_all, *landed = _run_comm(_merge_plans([_gather_plan([mod_part], [False]), gather_plan(first)]), "gather_mod_and_first")
    gathered.update(zip(first, landed))
    mod_mine = lax.dynamic_index_in_dim(mod_all, me, axis=2, keepdims=False)
    mod_mine = mod_mine.reshape(n_layers, 3, 3, d)
    wr_b, wi_b = lru_w_r[0].astype(BF16), lru_w_i[0].astype(BF16)
    eye2 = jnp.eye(2 * cw_in, dtype=BF16).reshape(2, cw_in, 2 * cw_in)

    def w_gu(q):
        return gathered[("ffn_w_gu", q)]

    def w_d4(q):
        return gathered[("ffn_w_down", q)].reshape(1, HIDDEN_CHUNKS, fc, d)

    saved = []
    xcur = x2
    for layer in range(n_layers):
        for sub in range(3):
            gvec = norm_full[layer, sub].reshape(1, d)
            shift = mod_mine[layer, sub, 0].reshape(1, d)
            scale1p = 1.0 + mod_mine[layer, sub, 1].reshape(1, d)
            gmul = 1.0 + mod_mine[layer, sub, 2].reshape(1, d)
            tag = f"l{layer}s{sub}"
            h = _norm_fwd(xcur, gvec, scale1p, shift, tag + "_norm")
            rec = dict(x=xcur, h=h, g=gvec, scale1p=scale1p, gmul=gmul, w=MACARON_W if sub != 1 else 1.0)
            if sub != 1:
                lj = layer * 2 + sub // 2
                gu2, a = hosting(tag + "_gu", lambda plan: _ffn_gu(tag + "_gu", h, w_gu(lj), plan))
                yv, xcur = hosting(tag + "_down", lambda plan: _ffn_down(tag + "_down", a, w_d4(lj), xcur, gmul, plan))
                rec.update(kind="ffn", lj=lj, gu2=gu2, a=a, y=yv)
            elif layer % 2 == 0:
                w_qkv = gathered[("sb_w_qkv", 0)][0]
                w_o = gathered[("sb_w_o", 0)].reshape(d, d)
                qkv = _mm_nn(tag + "_qkv", h, w_qkv, BF16)[0]
                o, ltot = hosting("sb_fwd", lambda plan: _sb_fwd(qkv, d, plan))
                yv, xcur = _mm_nn(tag + "_wo", o, w_o, [BF16, F32], extras=[(xcur, "tile"), (gmul, "row")],
                                  epilogue=lambda accs, ex: (accs[0], ex[0] + ex[1] * accs[0]))
                rec.update(kind="sb", qkv=qkv, o=o, ltot=ltot, y=yv, w_qkv=w_qkv, w_o=w_o)
            else:
                w_in = _chunks_to_cols("lru_w_in_cols", gathered[("lru_w_in", 0)][0], eye2)
                w_out = gathered[("lru_w_out", 0)].reshape(r_dim, d)
                gx = _mm_nn(tag + "_win", h, w_in, F32)[0]
                ymix, hs = _lru_fwd(gx, lru_small, wr_b, wi_b)
                yv, xcur = _mm_nn(tag + "_wout", ymix, w_out, [BF16, F32], extras=[(xcur, "tile"), (gmul, "row")],
                                  epilogue=lambda accs, ex: (accs[0], ex[0] + ex[1] * accs[0]))
                rec.update(kind="lru", gx=gx, hs=hs, ymix=ymix, y=yv, w_in=w_in, w_out=w_out)
            saved.append(rec)

    last = saved[-1]
    dxo, dy, head_sums = _loss_head(xcur, target, final_norm_g.reshape(1, d), (last["w"] * last["gmul"]))
    loss = lax.psum(head_sums[1, 0], ("x", "y", "c"))
    dgf = head_sums[0]

    c_idx = jnp.reshape(ci, (1,)).astype(jnp.int32)
    chip_idx = jnp.reshape(chip, (1,)).astype(jnp.int32)
    grads, reduced = {}, {}
    to_pair = []
    to_chips = []

    def sibling_plan(only=None):
        keys = [key for key in to_pair if only is None or key in only]
        if not keys:
            return None, keys
        return _exchange_plan([grads[key] for key in keys], [key in col_window for key in keys], 4, _sibling_route), keys

    def sibling_done(keys, recv4):
        for key, r4 in zip(keys, recv4):
            to_pair.remove(key)
            to_chips.append((key, _pair_sum(grads[key], r4, c_idx, f"rs_pair_sum_{key[0]}{key[1]}", cols=key in col_window)))

    def chip_plan(only=None):
        items = [item for item in to_chips if only is None or item[0] in only]
        if not items:
            return None, items
        return _exchange_plan([p4 for _, p4 in items], [False] * len(items), 3, _chip_route), items

    def chips_done(items, recv3):
        for item, r3 in zip(items, recv3):
            to_chips.remove(item)
            reduced[item[0]] = (item[1], r3)

    def behind(call, make_plan, done, more=None):
        plan, items = make_plan()
        n_mine = len(plan.outs) if plan else 0
        n_more = len(more.outs) if more else 0
        outs = call(_merge_plans([plan, more]))
        n_own = len(outs) - n_mine - n_more
        done(items, outs[n_own:n_own + n_mine])
        return list(outs[:n_own]) + list(outs[n_own + n_mine:])

    carried = {
        "l1s1b_dymix": ("sibling", None), "lru_bwd": ("chips", [("ffn_w_gu", n_ffn - 1)]),
        "l1s0b_da": ("sibling", None), "l1s0b_dwgu": ("chips", [("ffn_w_down", n_ffn - 1)]),
        "l1s0b_dh": ("chips", [("lru_w_out", 0), ("lru_w_in", 0)]),
        "l0s2b_da": ("sibling", None), "l0s2b_dwgu": ("chips", [("ffn_w_down", n_ffn - 2)]),
    }

    def carrying(name, call):
        if name not in carried:
            return call(None)
        stage, only = carried[name]
        if stage == "sibling":
            return behind(call, functools.partial(sibling_plan, only), sibling_done)
        return behind(call, functools.partial(chip_plan, only), chips_done)

    def at_once(make_plan, done, name):
        plan, items = make_plan()
        if plan:
            done(items, _run_comm(plan, name))

    def add_grad(key, value):
        grads[key] = value
        to_pair.append(key)

    dmod = [[None] * 3 for _ in range(n_layers)]
    dnorm = [[None] * 3 for _ in range(n_layers)]
    dlru_small = wri_all = None
    for idx in reversed(range(len(saved))):
        rec = saved[idx]
        layer, sub = divmod(idx, 3)
        tag = f"l{layer}s{sub}b"
        if rec["kind"] == "ffn" and idx > 0:
            lj = rec["lj"]
            (dgu2,) = carrying(tag + "_da", lambda plan: _ffn_da(tag + "_da", dy, w_d4(lj), rec["gu2"], plan))
            dwd = _ffn_dwd(tag + "_dwd", rec["a"], dy)[0].reshape(gathered[("ffn_w_down", lj)].shape)
            (dwgu,) = carrying(tag + "_dwgu", lambda plan: _ffn_dwgu(tag + "_dwgu", rec["h"], dgu2, plan))
            (dh,) = carrying(tag + "_dh", lambda plan: _ffn_dh(tag + "_dh", dgu2, w_gu(lj), plan))
            add_grad(("ffn_w_down", lj), dwd)
            add_grad(("ffn_w_gu", lj), dwgu)
        elif rec["kind"] == "ffn":
            lj = rec["lj"]
            at_once(sibling_plan, sibling_done, "rs_sibling_" + tag)
            (dgu2,) = behind(lambda plan: _ffn_da(tag + "_da", dy, w_d4(lj), rec["gu2"], plan), chip_plan, chips_done)
            add_grad(("ffn_w_down", lj), _ffn_dwd(tag + "_dwd", rec["a"], dy)[0].reshape(gathered[("ffn_w_down", lj)].shape))
            at_once(sibling_plan, sibling_done, "rs_sibling_" + tag + "_dwd")
            (dwgu,) = behind(lambda plan: _ffn_dwgu(tag + "_dwgu", rec["h"], dgu2, plan), chip_plan, chips_done)
            add_grad(("ffn_w_gu", lj), dwgu)
            at_once(sibling_plan, sibling_done, "rs_sibling_" + tag + "_dwgu")
            (dh,) = behind(lambda plan: _ffn_dh(tag + "_dh", dgu2, w_gu(lj), plan), chip_plan, chips_done)
        elif rec["kind"] == "sb":
            at_once(sibling_plan, sibling_done, "rs_sibling_" + tag)
            do = _mm_nt(tag + "_do", dy, rec["w_o"], BF16)
            dwo = _mm_tn(tag + "_dwo", rec["o"], dy, BF16)
            wri = _pack_flat([dwr, dwi], LANES, 512, BF16)[None]
            dqkv3, wri_all = behind(lambda plan: _sb_bwd(rec["qkv"], do, rec["ltot"], d, plan), chip_plan, chips_done,
                                    more=_gather_plan([wri], [False]))
            add_grad(("sb_w_o", 0), dwo.reshape(gathered[("sb_w_o", 0)].shape))
            dh = _mm_nt_stack(tag + "_dh", dqkv3, rec["w_qkv"], F32)
            add_grad(("sb_w_qkv", 0), _mm_tn_stack(tag + "_dwqkv", rec["h"], dqkv3, BF16)[None])
        else:
            (dymix,) = carrying(tag + "_dymix", lambda plan: _mm_nt(tag + "_dymix", dy, rec["w_out"], F32, plan)
                                if plan else [_mm_nt(tag + "_dymix", dy, rec["w_out"], F32)])
            dwout = _mm_tn(tag + "_dwout", rec["ymix"], dy, BF16).reshape(gathered[("lru_w_out", 0)].shape)
            dgx2, dlru_small, dwr, dwi = carrying("lru_bwd", lambda plan: _lru_bwd(rec["gx"], rec["hs"], dymix, lru_small,
                                                                                wr_b, wi_b, plan))
            dh = _mm_nt_stack(tag + "_dh", dgx2, rec["w_in"], F32)
            dw_in = _mm_tn_stack(tag + "_dwin", rec["h"], dgx2, BF16)
            add_grad(("lru_w_out", 0), dwout)
            add_grad(("lru_w_in", 0), _cols_to_chunks("lru_w_in_chunks", dw_in, eye2)[None])
        prev = saved[idx - 1] if idx > 0 else None
        gw_prev = (prev["w"] * prev["gmul"]) if prev is not None else jnp.zeros((1, d), F32)
        dxo, dy, sums = _adaln_bwd(dh, rec["x"], rec["y"], dxo, rec["g"], rec["scale1p"], rec["w"], gw_prev, tag + "_adaln")
        dmod[layer][sub] = sums[0:3]
        dnorm[layer][sub] = sums[3]
    grad_x = dxo.reshape(x.shape)

    dmod_mine = jnp.stack([jnp.stack(dmod[layer]) for layer in range(n_layers)])
    dnorm_mine = jnp.stack([jnp.stack(dnorm[layer]) for layer in range(n_layers)])
    assert not to_pair and not to_chips
    small_shapes = [(n_layers, 9 * d), (n_layers, 3, d), (8, r_dim), (d,)]
    small3 = _pack_flat([dmod_mine, dnorm_mine, dlru_small, dgf], LANES, 256, F32)
    n_small3 = small3.shape[0]
    all3 = _allgather(small3[None], "gather_small_grads").reshape(N_DEV, n_small3, LANES)
    gsum = _sum_devices(all3, "sum_small_grads").reshape(-1)
    g_mod_b, g_norm_full, g_lru_small, g_final = _unpack_flat(gsum, small_shapes)
    wri_sum = _sum_devices(wri_all.reshape(N_DEV, -1, LANES), "sum_gate_weight_grads").reshape(-1)
    g_wr, g_wi = _unpack_flat(wri_sum, [lru_w_r.shape, lru_w_i.shape])
    dmod_all = all3.reshape(N_DEV, -1)[:, :n_layers * 9 * d].reshape(N_DEV, n_layers, N_DEV, mod_cols)
    dmod_cols = jnp.transpose(lax.dynamic_index_in_dim(dmod_all, me, axis=2, keepdims=False), (1, 0, 2))

    out_g, out_d, out_m, out_v = {}, {}, {}, {}
    out_g["mod_w"], out_d["mod_w"], out_m["mod_w"], out_v["mod_w"] = _mod_w_update(c_all, dmod_cols, mod_w, m_mod_w, v_mod_w)

    g_norm_shard = lax.dynamic_slice_in_dim(g_norm_full, me * ng, ng, axis=2)
    g_lru_shard = lax.dynamic_slice_in_dim(g_lru_small, me * rs, rs, axis=1)
    small_grads = dict(mod_b=g_mod_b, norm_g=g_norm_shard, lru_conv_w=g_lru_shard[0:4].reshape(lru_conv_w.shape),
                       lru_conv_b=g_lru_shard[4:5], lru_b_r=g_lru_shard[5:6], lru_b_i=g_lru_shard[6:7],
                       lru_lambda=g_lru_shard[7:8], final_norm_g=g_final)
    for n, g in (("lru_w_r", g_wr), ("lru_w_i", g_wi)):
        view = lambda arr: arr.reshape(-1, LRU_BLOCK_W)
        outs = _adam_update("adam_" + n, view(weights[n]), view(mom_m[n]), view(mom_v[n]), [view(g)])
        out_g[n], out_d[n], out_m[n], out_v[n] = [o.reshape(weights[n].shape) for o in outs]
    small_names = list(small_grads)
    sw = _pack_flat([weights[n] for n in small_names], LANES, 256, F32)
    sg = _pack_flat([small_grads[n] for n in small_names], LANES, 256, F32)
    sm = _pack_flat([mom_m[n] for n in small_names], LANES, 256, F32)
    sv = _pack_flat([mom_v[n] for n in small_names], LANES, 256, F32)
    s_outs = _adam_update("adam_small", sw, sm, sv, [sg])
    small_shapes2 = [weights[n].shape for n in small_names]
    for dst, flat in zip((out_g, out_d, out_m, out_v), s_outs):
        for n, arr in zip(small_names, _unpack_flat(flat.reshape(-1), small_shapes2)):
            dst[n] = arr

    for n in ["ffn_w_gu", "ffn_w_down", "sb_w_qkv", "sb_w_o", "lru_w_in", "lru_w_out"]:
        shp = weights[n].shape
        shard3 = (math.prod(shp[:-2]),) + shp[-2:]
        view = lambda arr: arr.reshape(shard3)
        outs = None
        for q in range(shard3[0]):
            fills = outs if outs is not None else [lax.empty(shard3, F32) for _ in range(4)]
            p4, r3 = reduced[(n, q)]
            outs = _adam_shard(f"adam_{n}{q}", view(weights[n]), view(mom_m[n]), view(mom_v[n]), p4, r3, chip_idx,
                               first=q, fills=fills if shard3[0] > 1 else None)
        out_g[n], out_d[n], out_m[n], out_v[n] = [o.reshape(shp) for o in outs]

    return (loss, grad_x, *[out_g[n] for n in names], *[out_d[n] for n in names], *[out_m[n] for n in names],
            *[out_v[n] for n in names])
```

```python
import functools
import math

import jax
import jax.numpy as jnp
from jax import lax
from jax.experimental import pallas as pl
from jax.experimental.pallas import tpu as pltpu

F32 = jnp.float32
BF16 = jnp.bfloat16
SDS = jax.ShapeDtypeStruct
MESH = pl.DeviceIdType.MESH
ANY = pl.BlockSpec(memory_space=pl.ANY)

N_DEV = 8
LANES = 128
HEAD_DIM = 64
LRU_BLOCK_W = 128
LRU_C = 8.0
MACARON_W = 0.5
NORM_EPS = 1e-6
ADAM_LR = 0.001
ADAM_B1 = 0.9
ADAM_B2 = 0.999
ADAM_EPS = 1e-08
ADAM_WD = 0.01
ADAM_STEP = 10
VMEM_LIMIT = 56 * 1024 * 1024
GELU_C = math.sqrt(2.0 / math.pi)
GELU_K = 0.044715

DIMS = {
    "nn": (((1,), (0,)), ((), ())),
    "nt": (((1,), (1,)), ((), ())),
    "tn": (((0,), (0,)), ((), ())),
}


def _pcall(body, **kw):
    return pl.pallas_call(body, **kw)


def _params(sem=None):
    return pltpu.CompilerParams(dimension_semantics=sem, vmem_limit_bytes=VMEM_LIMIT)


def _tile(n, prefs):
    for p in prefs:
        if n % p == 0:
            return p
    return n


def _dot(a, b, dims):
    return lax.dot_general(a, b, DIMS[dims], preferred_element_type=F32)


def _softplus(z):
    return jnp.maximum(z, 0.0) + jnp.log(1.0 + jnp.exp(-jnp.abs(z)))


def _mesh_pos():
    return lax.axis_index("x"), lax.axis_index("y"), lax.axis_index("c")


def _allgather(xs, name, cols=False):
    return _run_comm(_gather_plan([xs], [cols]), name)[0]


class _CommPlan:
    def __init__(self, ins, outs, n_remote, n_local, phases):
        self.ins, self.outs, self.n_remote, self.n_local, self.phases = ins, outs, n_remote, n_local, phases

    def scratch(self):
        return [pltpu.SemaphoreType.DMA((self.n_remote,)), pltpu.SemaphoreType.DMA((self.n_remote,)),
                pltpu.SemaphoreType.DMA((max(self.n_local, 1),))]


def _merge_plans(plans):
    plans = [p for p in plans if p is not None]
    if len(plans) <= 1:
        return plans[0] if plans else None

    def phase(k):
        def run(in_refs, out_refs, send_sems, recv_sems, local_sems, r0=0, l0=0):
            i0 = o0 = 0
            for p in plans:
                p.phases[k](in_refs[i0:i0 + len(p.ins)], out_refs[o0:o0 + len(p.outs)], send_sems, recv_sems, local_sems, r0, l0)
                i0, o0, r0, l0 = i0 + len(p.ins), o0 + len(p.outs), r0 + p.n_remote, l0 + p.n_local
        return run

    return _CommPlan(sum([p.ins for p in plans], []), sum([p.outs for p in plans], []), sum(p.n_remote for p in plans),
                     sum(p.n_local for p in plans), [phase(0), phase(1), phase(2)])


def _run_comm(plan, name):
    n_in, n_out = len(plan.ins), len(plan.outs)

    def body(*refs):
        in_refs, out_refs, sems = refs[:n_in], refs[n_in:n_in + n_out], refs[n_in + n_out:]
        for phase in plan.phases:
            phase(in_refs, out_refs, *sems)

    return _pcall(body, name=name, out_shape=plan.outs, in_specs=[ANY] * n_in, out_specs=[ANY] * n_out,
                  scratch_shapes=plan.scratch())(*plan.ins)


def _col_window(ref, idx, width):
    return ref.at[:, :, pl.ds(pl.multiple_of(idx * width, math.gcd(width, LANES)), width)]


def _gather_plan(shards, cols):
    n = len(shards)
    outs = [SDS((s.shape[0], s.shape[1], N_DEV * s.shape[2]) if cl else (s.shape[0], N_DEV) + s.shape[1:], s.dtype)
            for s, cl in zip(shards, cols)]

    def copies(a, in_refs, out_refs, send_sems, recv_sems, local_sems, r0=0, l0=0):
        x, y, c = _mesh_pos()
        sibling = (x, y, 1 - c)
        chips = [(1 - x, y), (x, 1 - y), (1 - x, 1 - y)]
        width = shards[a].shape[2]

        def block(px, py, pc):
            idx = 4 * px + 2 * py + pc
            return _col_window(out_refs[a], idx, width) if cols[a] else out_refs[a].at[:, idx]

        def copy(k, owner, to, src=None):
            sem = r0 + 7 * a + k
            return pltpu.make_async_remote_copy(
                src_ref=block(*owner) if src is None else src, dst_ref=block(*owner),
                send_sem=send_sems.at[sem], recv_sem=recv_sems.at[sem], device_id=to, device_id_type=MESH)

        me = (x, y, c)
        first = [copy(0, me, sibling, src=in_refs[a])]
        first += [copy(1 + j, me, (*chip, c), src=in_refs[a]) for j, chip in enumerate(chips)]
        passed = [copy(4 + j, (*chip, c), sibling) for j, chip in enumerate(chips)]
        landed = [copy(1 + j, (*chip, c), me) for j, chip in enumerate(chips)]
        from_sibling = [copy(0, sibling, me)] + [copy(4 + j, (*chip, 1 - c), me) for j, chip in enumerate(chips)]
        mine = pltpu.make_async_copy(in_refs[a], block(*me), local_sems.at[l0 + a])
        return first, passed, landed, from_sibling, mine

    def start(*refs):
        for a in range(n):
            first, _, _, _, mine = copies(a, *refs)
            mine.start()
            for cp in first:
                cp.start()

    def pass_on(*refs):
        for a in range(n):
            _, passed, landed, _, _ = copies(a, *refs)
            for cp, fwd in zip(landed, passed):
                cp.wait_recv()
                fwd.start()

    def finish(*refs):
        for a in range(n):
            first, passed, _, from_sibling, mine = copies(a, *refs)
            for cp in from_sibling:
                cp.wait_recv()
            for cp in first + passed:
                cp.wait_send()
            mine.wait()

    return _CommPlan(list(shards), outs, 7 * n, n, [start, pass_on, finish])


def _exchange_plan(srcs, cols, n_slots, route):
    n = len(srcs)
    outs = []
    for g, cl in zip(srcs, cols):
        shard = (g.shape[0], g.shape[1], g.shape[2] // N_DEV) if cl else (g.shape[0],) + g.shape[2:]
        outs.append(SDS((n_slots,) + shard, g.dtype))

    def copies(in_refs, out_refs, send_sems, recv_sems, local_sems, r0=0, l0=0):
        x, y, c = _mesh_pos()
        made = []
        for a in range(n):
            for s in range(n_slots):
                chunk, target = route(x, y, c, s)
                src = _col_window(in_refs[a], chunk, outs[a].shape[3]) if cols[a] else in_refs[a].at[:, chunk]
                sem = r0 + a * n_slots + s
                made.append(pltpu.make_async_remote_copy(
                    src_ref=src, dst_ref=out_refs[a].at[s], send_sem=send_sems.at[sem], recv_sem=recv_sems.at[sem],
                    device_id=target, device_id_type=MESH))
        return made

    def start(*refs):
        for cp in copies(*refs):
            cp.start()

    def nothing(*refs):
        pass

    def finish(*refs):
        made = copies(*refs)
        for cp in made:
            cp.wait_recv()
        for cp in made:
            cp.wait_send()

    return _CommPlan(list(srcs), outs, n * n_slots, 0, [start, nothing, finish])


def _sibling_route(x, y, c, k):
    return 2 * k + 1 - c, (x, y, 1 - c)


def _chip_route(x, y, c, j):
    px, py = [(1 - x, y), (x, 1 - y), (1 - x, 1 - y)][j]
    return 2 * px + py, (px, py, c)


def _pair_sum(grads, recv4, c_idx, name, cols=False):
    _, p, r, cdim = recv4.shape
    tr = _tile(r, (512, 256, 176, 160, 128, 64, 32, 16))

    def body(c_ref, a_ref, b_ref, o_ref):
        o_ref[...] = (a_ref[...].astype(F32) + b_ref[...].astype(F32)).astype(o_ref.dtype)

    blk = (None, None, tr, cdim)
    if cols:
        own = pl.BlockSpec((None, tr, cdim), lambda k, q, i, c_ref: (q, i, 2 * k + c_ref[0]))
    else:
        own = pl.BlockSpec(blk, lambda k, q, i, c_ref: (q, 2 * k + c_ref[0], i, 0))
    grid_spec = pltpu.PrefetchScalarGridSpec(
        num_scalar_prefetch=1, grid=(4, p, r // tr),
        in_specs=[own, pl.BlockSpec(blk, lambda k, q, i, c_ref: (k, q, i, 0))],
        out_specs=pl.BlockSpec(blk, lambda k, q, i, c_ref: (q, k, i, 0)))
    return _pcall(body, name=name, grid_spec=grid_spec, out_shape=SDS((p, 4, r, cdim), grads.dtype),
                  compiler_params=_params(("parallel", "parallel", "parallel")))(c_idx, grads, recv4)


def _mm(name, ins, prods, n_acc, acc_shape, epi_idx, epilogue, out_shapes, out_specs, grid, dims, plan=None):
    n_in, n_out, nk = len(ins), len(out_shapes), grid[2]
    n_acc_refs = n_acc if nk > 1 else 0
    c_ins, c_outs = (plan.ins, plan.outs) if plan else ([], [])
    n_cin, n_cout = len(c_ins), len(c_outs)

    def body(*refs):
        in_refs, c_in = refs[:n_in], refs[n_in:n_in + n_cin]
        rest = refs[n_in + n_cin:]
        out_refs, c_out = rest[:n_out], rest[n_out:n_out + n_cout]
        rest = rest[n_out + n_cout:]
        acc_refs, sems = rest[:n_acc_refs], rest[n_acc_refs:]
        ids = [pl.program_id(axis) for axis in range(3)]
        if plan:
            @pl.when((ids[0] == 0) & (ids[1] == 0) & (ids[2] == 0))
            def _():
                plan.phases[0](c_in, c_out, *sems)

        def finish(accs):
            outs = epilogue(accs, [in_refs[i][...] for i in epi_idx])
            for o_ref, o in zip(out_refs, outs):
                if isinstance(o, tuple):
                    for plane, part in enumerate(o):
                        o_ref[plane] = part.astype(o_ref.dtype)
                else:
                    o_ref[...] = o.astype(o_ref.dtype)

        if nk == 1:
            accs = [None] * n_acc
            for ia, ib, iacc in prods:
                term = _dot(in_refs[ia][...], in_refs[ib][...], dims)
                accs[iacc] = term if accs[iacc] is None else accs[iacc] + term
            finish(accs)
        else:
            @pl.when(ids[2] == 0)
            def _():
                for acc in acc_refs:
                    acc[...] = jnp.zeros_like(acc)

            for ia, ib, iacc in prods:
                acc_refs[iacc][...] += _dot(in_refs[ia][...], in_refs[ib][...], dims)

            @pl.when(ids[2] == nk - 1)
            def _():
                finish([acc[...] for acc in acc_refs])

        if plan:
            @pl.when((ids[0] == grid[0] - 1) & (ids[1] == grid[1] - 1) & (ids[2] == nk - 1))
            def _():
                plan.phases[1](c_in, c_out, *sems)
                plan.phases[2](c_in, c_out, *sems)

    return _pcall(
        body, name=name, grid=grid, in_specs=[s for _, s in ins] + [ANY] * n_cin,
        out_specs=list(out_specs) + [ANY] * n_cout, out_shape=list(out_shapes) + list(c_outs),
        scratch_shapes=[pltpu.VMEM(acc_shape, F32) for _ in range(n_acc_refs)] + (plan.scratch() if plan else []),
        compiler_params=_params(("arbitrary",) * 3 if plan else ("parallel", "parallel", "arbitrary")),
    )(*[a for a, _ in ins], *c_ins)


def _plain(accs, _):
    return accs


def _mm_nn(name, a, b, out_dtype, extras=(), epilogue=_plain, n_out=1):
    m, kd = a.shape
    n = b.shape[1]
    tm, tn, tk = _tile(m, (1024, 512, 256, 128)), _tile(n, (640, 512, 256, 128)), _tile(kd, (1280, 1024, 512, 256, 128))
    ins = [(a, pl.BlockSpec((tm, tk), lambda i, j, k: (i, k))), (b, pl.BlockSpec((tk, tn), lambda i, j, k: (k, j)))]
    for arr, kind in extras:
        if kind == "tile":
            ins.append((arr, pl.BlockSpec((tm, tn), lambda i, j, k: (i, j))))
        else:
            ins.append((arr, pl.BlockSpec((1, tn), lambda i, j, k: (0, j))))
    dts = out_dtype if isinstance(out_dtype, (list, tuple)) else [out_dtype] * n_out
    return _mm(name, ins, [(0, 1, 0)], 1, (tm, tn), list(range(2, len(ins))), epilogue,
               [SDS((m, n), dt) for dt in dts], [pl.BlockSpec((tm, tn), lambda i, j, k: (i, j)) for _ in dts],
               (m // tm, n // tn, kd // tk), "nn")


def _mm_nt(name, a, b, out_dtype, plan=None):
    m, kd = a.shape
    n = b.shape[0]
    tm, tn, tk = _tile(m, (1024, 512, 256, 128)), _tile(n, (640, 512, 256, 128)), _tile(kd, (1024, 512, 256, 128))
    ins = [(a, pl.BlockSpec((tm, tk), lambda i, j, k: (i, k))), (b, pl.BlockSpec((tn, tk), lambda i, j, k: (j, k)))]
    outs = _mm(name, ins, [(0, 1, 0)], 1, (tm, tn), [], _plain, [SDS((m, n), out_dtype)],
               [pl.BlockSpec((tm, tn), lambda i, j, k: (i, j))], (m // tm, n // tn, kd // tk), "nt", plan=plan)
    return outs if plan else outs[0]


def _mm_tn(name, a, b, out_dtype):
    t, m = a.shape
    n = b.shape[1]
    tm, tn, tk = _tile(m, (640, 512, 256, 128)), _tile(n, (1024, 512, 256, 128)), t
    ins = [(a, pl.BlockSpec((tk, tm), lambda i, j, k: (k, i))), (b, pl.BlockSpec((tk, tn), lambda i, j, k: (k, j)))]
    return _mm(name, ins, [(0, 1, 0)], 1, (tm, tn), [], _plain, [SDS((m, n), out_dtype)],
               [pl.BlockSpec((tm, tn), lambda i, j, k: (i, j))], (m // tm, n // tn, t // tk), "tn")[0]


def _mm_nt_stack(name, a3, b, out_dtype):
    cc, m, kd = a3.shape
    n = b.shape[0]
    tm, tn, tk = _tile(m, (1024, 512, 256, 128)), _tile(n, (1024, 512, 256, 128)), _tile(kd, (1280, 1024, 512, 256, 128))
    nk = kd // tk
    ins = [(a3, pl.BlockSpec((None, tm, tk), lambda i, j, k: (k // nk, i, k % nk))),
           (b, pl.BlockSpec((tn, tk), lambda i, j, k: (j, k)))]
    return _mm(name, ins, [(0, 1, 0)], 1, (tm, tn), [], _plain, [SDS((m, n), out_dtype)],
               [pl.BlockSpec((tm, tn), lambda i, j, k: (i, j))], (m // tm, n // tn, cc * nk), "nt")[0]


def _mm_tn_stack(name, a, b3, out_dtype):
    t, m = a.shape
    cc, _, n = b3.shape
    tm, tn, tk = _tile(m, (512, 256, 128)), _tile(n, (1280, 1024, 512, 256, 128)), t
    nj = n // tn
    ins = [(a, pl.BlockSpec((tk, tm), lambda i, j, k: (k, i))),
           (b3, pl.BlockSpec((None, tk, tn), lambda i, j, k: (j // nj, k, j % nj)))]
    return _mm(name, ins, [(0, 1, 0)], 1, (tm, tn), [], _plain, [SDS((m, cc * n), out_dtype)],
               [pl.BlockSpec((tm, tn), lambda i, j, k: (i, j))], (m // tm, cc * nj, t // tk), "tn")[0]


def _chunks_to_cols(name, wc, eye2):
    nch, d, cw = wc.shape
    tm = _tile(d, (1024, 512, 256, 128))
    ins = [(wc, pl.BlockSpec((None, tm, cw), lambda i, j, k: (2 * j + k, i, 0))),
           (eye2, pl.BlockSpec((None, cw, 2 * cw), lambda i, j, k: (k, 0, 0)))]
    return _mm(name, ins, [(0, 1, 0)], 1, (tm, 2 * cw), [], _plain, [SDS((d, nch * cw), wc.dtype)],
               [pl.BlockSpec((tm, 2 * cw), lambda i, j, k: (i, j))], (d // tm, nch // 2, 2), "nn")[0]


def _cols_to_chunks(name, full, eye2):
    d, n = full.shape
    _, cw, _ = eye2.shape
    nch = n // cw
    tm = _tile(d, (1024, 512, 256, 128))
    ins = [(full, pl.BlockSpec((tm, 2 * cw), lambda i, j, k: (i, j // 2))),
           (eye2, pl.BlockSpec((None, cw, 2 * cw), lambda i, j, k: (j % 2, 0, 0)))]
    return _mm(name, ins, [(0, 1, 0)], 1, (tm, cw), [], _plain, [SDS((nch, d, cw), full.dtype)],
               [pl.BlockSpec((None, tm, cw), lambda i, j, k: (j, i, 0))], (d // tm, nch, 1), "nt")[0]


def _row_tile(t):
    return _tile(t, (256, 128, 64, 32, 16, 8))


def _norm_fwd(x, g, scale1p, shift, name):
    t, d = x.shape
    tr = _row_tile(t)

    def body(x_ref, g_ref, s_ref, b_ref, h_ref):
        xv = x_ref[...]
        inv = lax.rsqrt(jnp.mean(xv * xv, axis=-1, keepdims=True) + NORM_EPS)
        h_ref[...] = ((xv * inv) * g_ref[...] * s_ref[...] + b_ref[...]).astype(h_ref.dtype)

    vec = pl.BlockSpec((1, d), lambda i: (0, 0))
    return _pcall(body, name=name, grid=(t // tr,), in_specs=[pl.BlockSpec((tr, d), lambda i: (i, 0)), vec, vec, vec],
                  out_specs=pl.BlockSpec((tr, d), lambda i: (i, 0)), out_shape=SDS((t, d), BF16),
                  compiler_params=_params(("parallel",)))(x, g, scale1p, shift)


def _adaln_bwd(dh, x, y, dxo, g, scale1p, w_sub, gw_prev, name):
    t, d = x.shape
    tr = _row_tile(t)

    def body(dh_ref, x_ref, y_ref, dxo_ref, g_ref, s_ref, gw_ref, dx_ref, dyp_ref, sums_ref):
        i = pl.program_id(0)

        @pl.when(i == 0)
        def _():
            sums_ref[...] = jnp.zeros_like(sums_ref)

        xv, dhv, dxov = x_ref[...], dh_ref[...], dxo_ref[...]
        inv = lax.rsqrt(jnp.mean(xv * xv, axis=-1, keepdims=True) + NORM_EPS)
        xn = xv * inv
        gv = g_ref[...]
        dn = dhv * s_ref[...]
        dxn = dn * gv
        dx = inv * (dxn - xn * jnp.mean(dxn * xn, axis=-1, keepdims=True)) + dxov
        dx_ref[...] = dx
        dyp_ref[...] = (gw_ref[...] * dx).astype(dyp_ref.dtype)
        sums_ref[0:1, :] += jnp.sum(dhv, axis=0, keepdims=True)
        sums_ref[1:2, :] += jnp.sum(dhv * (xn * gv), axis=0, keepdims=True)
        sums_ref[2:3, :] += jnp.sum(w_sub * y_ref[...] * dxov, axis=0, keepdims=True)
        sums_ref[3:4, :] += jnp.sum(dn * xn, axis=0, keepdims=True)

    blk = pl.BlockSpec((tr, d), lambda i: (i, 0))
    vec = pl.BlockSpec((1, d), lambda i: (0, 0))
    return _pcall(
        body, name=name, grid=(t // tr,), in_specs=[blk, blk, blk, blk, vec, vec, vec],
        out_specs=[blk, blk, pl.BlockSpec((8, d), lambda i: (0, 0))],
        out_shape=[SDS((t, d), F32), SDS((t, d), BF16), SDS((8, d), F32)],
        compiler_params=_params(("arbitrary",)))(dh, x, y, dxo, g, scale1p, gw_prev)


def _loss_head(x, target, gf, gw_prev):
    t, d = x.shape
    tr = _row_tile(t)
    nt = t // tr

    def body(x_ref, tg_ref, g_ref, gw_ref, dx_ref, dyp_ref, sums_ref):
        i = pl.program_id(0)

        @pl.when(i == 0)
        def _():
            sums_ref[...] = jnp.zeros_like(sums_ref)

        xv = x_ref[...]
        inv = lax.rsqrt(jnp.mean(xv * xv, axis=-1, keepdims=True) + NORM_EPS)
        xn = xv * inv
        gv = g_ref[...]
        err = xn * gv - tg_ref[...]
        dyv = err * (1.0 / d)
        dxn = dyv * gv
        dx = inv * (dxn - xn * jnp.mean(dxn * xn, axis=-1, keepdims=True))
        dx_ref[...] = dx
        dyp_ref[...] = (gw_ref[...] * dx).astype(dyp_ref.dtype)
        sums_ref[0:1, :] += jnp.sum(dyv * xn, axis=0, keepdims=True)
        sums_ref[1:2, :] += jnp.sum(err * err, axis=0, keepdims=True)

        @pl.when(i == nt - 1)
        def _():
            tot = jnp.sum(sums_ref[1:2, :], axis=1, keepdims=True) * (0.5 / d)
            sums_ref[1:2, :] = jnp.broadcast_to(tot, (1, d))

    blk = pl.BlockSpec((tr, d), lambda i: (i, 0))
    vec = pl.BlockSpec((1, d), lambda i: (0, 0))
    return _pcall(
        body, name="loss_head", grid=(nt,), in_specs=[blk, blk, vec, vec],
        out_specs=[blk, blk, pl.BlockSpec((8, d), lambda i: (0, 0))],
        out_shape=[SDS((t, d), F32), SDS((t, d), BF16), SDS((8, d), F32)],
        compiler_params=_params(("arbitrary",)))(x, target, gf, gw_prev)


HIDDEN_CHUNKS = N_DEV // 2


def _ffn_tiles(t, d):
    return _tile(t, (1024, 512, 256, 128)), _tile(d, (1024, 512, 256, 128))


def _ffn_gu(name, h, wgu, plan=None):
    t, d = h.shape
    fc, nc = wgu.shape[3], HIDDEN_CHUNKS
    tm, _ = _ffn_tiles(t, d)

    def epi_gu(accs, _):
        gpre, up = accs
        return (gpre, up), gpre * jax.nn.sigmoid(gpre) * up

    wblk = (None, None, d, fc)
    ins = [(h, pl.BlockSpec((tm, d), lambda i, c, k: (i, 0))),
           (wgu, pl.BlockSpec(wblk, lambda i, c, k: (0, c, 0, 0))),
           (wgu, pl.BlockSpec(wblk, lambda i, c, k: (0, c + nc, 0, 0)))]
    return _mm(name, ins, [(0, 1, 0), (0, 2, 1)], 2, (tm, fc), [], epi_gu,
               [SDS((2, nc, t, fc), BF16), SDS((nc, t, fc), BF16)],
               [pl.BlockSpec((2, None, tm, fc), lambda i, c, k: (0, c, i, 0)),
                pl.BlockSpec((None, tm, fc), lambda i, c, k: (c, i, 0))],
               (t // tm, nc, 1), "nn", plan=plan)


def _ffn_down(name, a, wd4, x, gmul, plan=None):
    nc, t, fc = a.shape
    d = wd4.shape[3]
    tm, tn = _ffn_tiles(t, d)

    def epi_down(accs, ex):
        (yv,), (xv, gm) = accs, ex
        return yv, xv + MACARON_W * gm * yv

    ins = [(a, pl.BlockSpec((None, tm, fc), lambda i, j, k: (k, i, 0))),
           (wd4, pl.BlockSpec((None, None, fc, tn), lambda i, j, k: (0, k, 0, j))),
           (x, pl.BlockSpec((tm, tn), lambda i, j, k: (i, j))), (gmul, pl.BlockSpec((1, tn), lambda i, j, k: (0, j)))]
    oblk = pl.BlockSpec((tm, tn), lambda i, j, k: (i, j))
    return _mm(name, ins, [(0, 1, 0)], 1, (tm, tn), [2, 3], epi_down, [SDS((t, d), BF16), SDS((t, d), F32)],
               [oblk, oblk], (t // tm, d // tn, nc), "nn", plan=plan)


def _ffn_da(name, dy, wd4, gu2, plan=None):
    t, d = dy.shape
    _, nc, fc, _ = wd4.shape
    tm, _ = _ffn_tiles(t, d)

    def epi_da(accs, ex):
        (da,), (gu,) = accs, ex
        gpre, up = gu[0].astype(F32), gu[1].astype(F32)
        s = jax.nn.sigmoid(gpre)
        silu = gpre * s
        dg = da * up * (s * (1.0 + gpre * (1.0 - s)))
        return ((dg, da * silu),)

    gblk = pl.BlockSpec((2, None, tm, fc), lambda i, c, k: (0, c, i, 0))
    ins = [(dy, pl.BlockSpec((tm, d), lambda i, c, k: (i, 0))),
           (wd4, pl.BlockSpec((None, None, fc, d), lambda i, c, k: (0, c, 0, 0))), (gu2, gblk)]
    return _mm(name, ins, [(0, 1, 0)], 1, (tm, fc), [2], epi_da, [SDS((2, nc, t, fc), BF16)], [gblk],
               (t // tm, nc, 1), "nt", plan=plan)


def _ffn_dwd(name, a, dy, plan=None):
    nc, t, fc = a.shape
    d = dy.shape[1]
    _, tn = _ffn_tiles(t, d)
    ins = [(a, pl.BlockSpec((None, t, fc), lambda c, j, k: (c, 0, 0))), (dy, pl.BlockSpec((t, tn), lambda c, j, k: (0, j)))]
    return _mm(name, ins, [(0, 1, 0)], 1, (fc, tn), [], _plain, [SDS((1, nc, fc, d), BF16)],
               [pl.BlockSpec((None, None, fc, tn), lambda c, j, k: (0, c, 0, j))], (nc, d // tn, 1), "tn", plan=plan)


def _ffn_dwgu(name, h, dgu2, plan=None):
    t, d = h.shape
    _, nc, _, fc = dgu2.shape
    _, tn = _ffn_tiles(t, d)
    ins = [(h, pl.BlockSpec((t, tn), lambda i, c, k: (0, i))),
           (dgu2, pl.BlockSpec((None, None, t, fc), lambda i, c, k: (c // nc, c % nc, 0, 0)))]
    return _mm(name, ins, [(0, 1, 0)], 1, (tn, fc), [], _plain, [SDS((1, 2 * nc, d, fc), BF16)],
               [pl.BlockSpec((None, None, tn, fc), lambda i, c, k: (0, c, i, 0))], (d // tn, 2 * nc, 1), "tn", plan=plan)


def _ffn_dh(name, dgu2, wgu, plan=None):
    _, nc, t, fc = dgu2.shape
    d = wgu.shape[2]
    tm, tn = _ffn_tiles(t, d)
    ins = [(dgu2, pl.BlockSpec((None, None, tm, fc), lambda i, j, k: (k // nc, k % nc, i, 0))),
           (wgu, pl.BlockSpec((None, None, tn, fc), lambda i, j, k: (0, k, j, 0)))]
    return _mm(name, ins, [(0, 1, 0)], 1, (tm, tn), [], _plain, [SDS((t, d), F32)],
               [pl.BlockSpec((tm, tn), lambda i, j, k: (i, j))], (t // tm, d // tn, 2 * nc), "nt", plan=plan)


def _sb_block(t):
    return 256 if t >= 1024 else 128


SB_STRIP = 64


def _sb_strips(blk):
    strip = min(SB_STRIP, blk)
    row = lax.broadcasted_iota(jnp.int32, (strip, blk), 0)
    col = lax.broadcasted_iota(jnp.int32, (strip, blk), 1)
    return [(slice(r0, r0 + strip), col < row + r0) for r0 in range(0, blk, strip)]


def _host_call(core, name, steps, ins, in_specs, out_shapes, out_specs, scratch, plan):
    n_in, n_out, n_scr = len(ins), len(out_shapes), len(scratch)
    c_ins, c_outs = (plan.ins, plan.outs) if plan else ([], [])
    n_cin, n_cout = len(c_ins), len(c_outs)

    def body(*refs):
        in_refs, c_in = refs[:n_in], refs[n_in:n_in + n_cin]
        rest = refs[n_in + n_cin:]
        out_refs, c_out = rest[:n_out], rest[n_out:n_out + n_cout]
        rest = rest[n_out + n_cout:]
        scr, sems = rest[:n_scr], rest[n_scr:]
        step = pl.program_id(0)
        if plan:
            @pl.when(step == 0)
            def _():
                plan.phases[0](c_in, c_out, *sems)

        core(in_refs, out_refs, scr)
        if plan:
            @pl.when(step == steps - 1)
            def _():
                plan.phases[1](c_in, c_out, *sems)
                plan.phases[2](c_in, c_out, *sems)

    return _pcall(
        body, name=name, grid=(steps,), in_specs=list(in_specs) + [ANY] * n_cin,
        out_specs=list(out_specs) + [ANY] * n_cout, out_shape=list(out_shapes) + list(c_outs),
        scratch_shapes=list(scratch) + (plan.scratch() if plan else []),
        compiler_params=_params(("arbitrary",)))(*ins, *c_ins)


def _sb_fwd(qkv, d, plan=None):
    t = qkv.shape[0]
    blk = _sb_block(t)
    nq = t // blk
    npair = d // LANES
    scale = HEAD_DIM ** -0.5

    def body(in_refs, out_refs, scr):
        (q_ref, k_ref, v_ref), (o_ref, l_ref) = in_refs, out_refs
        tri_s = scr[0]
        hi_s, lo_s, w_s, zs_s = (scr[1 + 4 * k:5 + 4 * k] for k in range(4))
        lane = lax.broadcasted_iota(jnp.int32, (blk, LANES), 1)
        head0 = lane < HEAD_DIM
        row = lax.broadcasted_iota(jnp.int32, (blk, blk), 0)
        col = lax.broadcasted_iota(jnp.int32, (blk, blk), 1)
        tri_s[...] = (row > col).astype(BF16)
        strips = _sb_strips(blk)

        def step(qhs, kbs, maskeds, carries):
            chains = [(bi, hh) for bi in range(len(kbs)) for hh in range(2)]
            starts = [pl.multiple_of(kb * blk, blk) for kb in kbs]
            kvs = [k_ref[pl.ds(start, blk), :] for start in starts]
            vvs = [v_ref[pl.ds(start, blk), :] for start in starts]
            zs = [_dot(qhs[hh], kvs[bi], "nt") for bi, hh in chains]
            sums = []
            for c, (bi, hh) in enumerate(chains):
                parts = []
                for rows, causal in strips:
                    zt = zs[c][rows, :]
                    sp = _softplus(zt)
                    lk = jnp.where(causal, -sp, 0.0) if maskeds[bi] else -sp
                    hi = lk.astype(BF16)
                    hi_s[c][rows, :] = hi
                    lo_s[c][rows, :] = (lk - hi.astype(F32)).astype(BF16)
                    zs_s[c][rows, :] = zt - sp
                    parts.append(jnp.sum(lk, axis=1, keepdims=True))
                sums.append(jnp.concatenate(parts, axis=0))
            laters = [_dot(hi_s[c][...], tri_s[...], "nn") + _dot(lo_s[c][...], tri_s[...], "nn") for c in range(len(chains))]
            for c, (bi, hh) in enumerate(chains):
                cl = carries[hh][0]
                if bi == 1:
                    cl = cl + sums[hh]
                for rows, causal in strips:
                    logw = zs_s[c][rows, :] + laters[c][rows, :] + cl[rows, :]
                    if maskeds[bi]:
                        logw = jnp.where(causal, logw, -1e30)
                    w_s[c][rows, :] = jnp.exp(logw).astype(BF16)
            new = [list(carries[hh]) for hh in range(2)]
            for c, (bi, hh) in enumerate(chains):
                new[hh] = [new[hh][0] + sums[c], new[hh][1] + _dot(w_s[c][...], vvs[bi], "nn")]
            return tuple(tuple(cr) for cr in new)

        def qblock(qi, _):
            qstart = pl.multiple_of(qi * blk, blk)
            qv = q_ref[pl.ds(qstart, blk), :] * scale
            qhs = [jnp.where(head0 if hh == 0 else ~head0, qv, jnp.zeros_like(qv)) for hh in range(2)]
            zero = (jnp.zeros((blk, 1), F32), jnp.zeros((blk, LANES), F32))
            outs = lax.cond(qi % 2 == 1,
                            lambda crs: step(qhs, [qi, qi - 1], [True, False], crs),
                            lambda crs: step(qhs, [qi], [True], crs), (zero, zero))
            top = qi - 1 - qi % 2
            outs = lax.fori_loop(0, qi // 2, lambda j, crs: step(qhs, [top - 2 * j, top - 2 * j - 1], [False, False], crs),
                                 outs)
            o_ref[pl.ds(qstart, blk), :] = jnp.where(head0, outs[0][1], outs[1][1]).astype(o_ref.dtype)
            l_ref[pl.ds(qstart, blk), :] = jnp.where(head0, outs[0][0], outs[1][0])
            return 0

        lax.fori_loop(0, nq, qblock, 0)

    tile_bf16, tile_f32 = pltpu.VMEM((blk, blk), BF16), pltpu.VMEM((blk, blk), F32)
    return _host_call(
        body, "sb_fwd", npair, [qkv, qkv, qkv],
        [pl.BlockSpec((t, LANES), lambda p: (0, p)), pl.BlockSpec((t, LANES), lambda p: (0, npair + p)),
         pl.BlockSpec((t, LANES), lambda p: (0, 2 * npair + p))],
        [SDS((t, d), BF16), SDS((t, d), F32)],
        [pl.BlockSpec((t, LANES), lambda p: (0, p)), pl.BlockSpec((t, LANES), lambda p: (0, p))],
        [tile_bf16] * 13 + [tile_f32] * 4, plan)


def _sb_bwd(qkv, do, ltot, d, plan=None):
    t = qkv.shape[0]
    blk = _sb_block(t)
    nq = t // blk
    npair = d // LANES
    scale = HEAD_DIM ** -0.5

    def body(in_refs, out_refs, scr):
        (q_ref, k_ref, v_ref, do_ref, l_ref), (out_ref,) = in_refs, out_refs
        dq_s, dk_s, dv_s, upto_s, before_s = scr[:5]
        hi_s, lo_s, w_s, dab_s, dzs_s, zs_s, da_s = (scr[5 + 4 * k:9 + 4 * k] for k in range(7))
        lane = lax.broadcasted_iota(jnp.int32, (blk, LANES), 1)
        head0 = lane < HEAD_DIM
        row = lax.broadcasted_iota(jnp.int32, (blk, blk), 0)
        col = lax.broadcasted_iota(jnp.int32, (blk, blk), 1)
        upto_s[...] = (row <= col).astype(BF16)
        before_s[...] = (row < col).astype(BF16)
        dk_s[...] = jnp.zeros_like(dk_s)
        dv_s[...] = jnp.zeros_like(dv_s)
        strips = _sb_strips(blk)

        def step(heads, kbs, maskeds, carries):
            chains = [(bi, hh) for bi in range(len(kbs)) for hh in range(2)]
            starts = [pl.multiple_of(kb * blk, blk) for kb in kbs]
            kvs = [k_ref[pl.ds(start, blk), :] for start in starts]
            vvs = [v_ref[pl.ds(start, blk), :] for start in starts]
            zs = [_dot(heads[hh][0], kvs[bi], "nt") for bi, hh in chains]
            dws = [_dot(heads[hh][1], vvs[bi], "nt") for bi, hh in chains]
            lk_sums, da_sums = [], []
            for c, (bi, hh) in enumerate(chains):
                parts = []
                for rows, causal in strips:
                    zt = zs[c][rows, :]
                    sp = _softplus(zt)
                    lk = jnp.where(causal, -sp, 0.0) if maskeds[bi] else -sp
                    hi = lk.astype(BF16)
                    hi_s[c][rows, :] = hi
                    lo_s[c][rows, :] = (lk - hi.astype(F32)).astype(BF16)
                    zs_s[c][rows, :] = zt - sp
                    parts.append(jnp.sum(lk, axis=1, keepdims=True))
                lk_sums.append(jnp.concatenate(parts, axis=0))
            cums = [_dot(hi_s[c][...], upto_s[...], "nn") + _dot(lo_s[c][...], upto_s[...], "nn") for c in range(len(chains))]
            for c, (bi, hh) in enumerate(chains):
                lt, plk = heads[hh][2], carries[hh][0]
                if bi == 1:
                    plk = plk + lk_sums[hh]
                parts = []
                for rows, causal in strips:
                    logw = zs_s[c][rows, :] + (lt[rows, :] - (plk[rows, :] + cums[c][rows, :]))
                    if maskeds[bi]:
                        logw = jnp.where(causal, logw, -1e30)
                    w = jnp.exp(logw)
                    w_s[c][rows, :] = w.astype(BF16)
                    da = dws[c][rows, :] * w
                    da_s[c][rows, :] = da
                    dab_s[c][rows, :] = da.astype(BF16)
                    parts.append(jnp.sum(da, axis=1, keepdims=True))
                da_sums.append(jnp.concatenate(parts, axis=0))
            pres = [_dot(dab_s[c][...], before_s[...], "nn") for c in range(len(chains))]
            for c, (bi, hh) in enumerate(chains):
                pda = carries[hh][1]
                if bi == 1:
                    pda = pda + da_sums[hh]
                for rows, causal in strips:
                    sig = jnp.exp(zs_s[c][rows, :])
                    da = da_s[c][rows, :]
                    dz = da * (1.0 - sig) - sig * (pda[rows, :] + pres[c][rows, :])
                    if maskeds[bi]:
                        dz = jnp.where(causal, dz, 0.0)
                    dzs_s[c][rows, :] = dz.astype(BF16)
            new = [list(carries[hh]) for hh in range(2)]
            for c, (bi, hh) in enumerate(chains):
                dk_s[kbs[bi]] += _dot(heads[hh][3], dzs_s[c][...], "nn")
                dv_s[kbs[bi]] += _dot(heads[hh][4], w_s[c][...], "nn")
                new[hh] = [new[hh][0] + lk_sums[c], new[hh][1] + da_sums[c], new[hh][2] + _dot(dzs_s[c][...], kvs[bi], "nn")]
            return tuple(tuple(cr) for cr in new)

        def qblock(qi, _):
            qstart = pl.multiple_of(qi * blk, blk)
            qv = q_ref[pl.ds(qstart, blk), :] * scale
            dov = do_ref[pl.ds(qstart, blk), :]
            lv = l_ref[pl.ds(qstart, blk), :]
            heads = []
            for hh in range(2):
                sel = head0 if hh == 0 else ~head0
                qh, doh = jnp.where(sel, qv, jnp.zeros_like(qv)), jnp.where(sel, dov, jnp.zeros_like(dov))
                heads.append((qh, doh, jnp.max(jnp.where(sel, lv, -jnp.inf), axis=1, keepdims=True),
                              qh.astype(F32).T.astype(BF16), doh.astype(F32).T.astype(BF16)))
            zero = (jnp.zeros((blk, 1), F32), jnp.zeros((blk, 1), F32), jnp.zeros((blk, LANES), F32))
            carries = lax.fori_loop(0, qi // 2, lambda j, crs: step(heads, [2 * j, 2 * j + 1], [False, False], crs),
                                    (zero, zero))
            carries = lax.cond(qi % 2 == 1,
                               lambda crs: step(heads, [qi - 1, qi], [False, True], crs),
                               lambda crs: step(heads, [qi], [True], crs), carries)
            dq_s[pl.ds(qstart, blk), :] = jnp.where(head0, carries[0][2], carries[1][2]) * scale
            return 0

        lax.fori_loop(0, nq, qblock, 0)
        out_ref[0] = dq_s[...].astype(out_ref.dtype)
        for b in range(nq):
            out_ref[1, b * blk:(b + 1) * blk, :] = dk_s[b].T.astype(out_ref.dtype)
            out_ref[2, b * blk:(b + 1) * blk, :] = dv_s[b].T.astype(out_ref.dtype)

    col_blk = lambda off: pl.BlockSpec((t, LANES), lambda p: (0, off + p))
    return _host_call(
        body, "sb_bwd", npair, [qkv, qkv, qkv, do, ltot],
        [col_blk(0), col_blk(npair), col_blk(2 * npair), col_blk(0), col_blk(0)],
        [SDS((3, t, d), BF16)], [pl.BlockSpec((3, t, LANES), lambda p: (0, 0, p))],
        [pltpu.VMEM((t, LANES), F32)] + [pltpu.VMEM((nq, LANES, blk), F32) for _ in range(2)]
        + [pltpu.VMEM((blk, blk), BF16) for _ in range(2 + 20)]
        + [pltpu.VMEM((blk, blk), F32) for _ in range(8)], plan)


def _roll_rows(v, shift):
    return pltpu.roll(v, shift, 0)


def _shift_down(v, dist, fill, row):
    return jnp.where(row >= dist, _roll_rows(v, dist), fill)


def _shift_up(v, dist, fill, row):
    t = v.shape[0]
    return jnp.where(row < t - dist, _roll_rows(v, t - dist), fill)


def _lru_gates(xb, small, wr, wi, row):
    xs = [_shift_down(xb, 3 - tap, 0.0, row) if tap < 3 else xb for tap in range(4)]
    xc = small[4:5, :] + xs[0] * small[0:1, :]
    for tap in range(1, 4):
        xc = xc + xs[tap] * small[tap:tap + 1, :]
    xcb = xc.astype(BF16)
    r = jax.nn.sigmoid(_dot(xcb, wr, "nn") + small[5:6, :])
    ig = jax.nn.sigmoid(_dot(xcb, wi, "nn") + small[6:7, :])
    sp = _softplus(-small[7:8, :])
    la = -LRU_C * r * sp
    a = jnp.exp(la)
    th = jnp.tanh(la)
    mult = jnp.sqrt(-2.0 * th / (1.0 - th))
    return xs, xc, xcb, r, ig, sp, a, mult


def _gelu_parts(gate):
    inner = GELU_C * (gate + GELU_K * gate * gate * gate)
    th = jnp.tanh(inner)
    gelu = 0.5 * gate * (1.0 + th)
    dgelu = 0.5 * (1.0 + th) + 0.5 * gate * (1.0 - th * th) * GELU_C * (1.0 + 3.0 * GELU_K * gate * gate)
    return gelu, dgelu


def _scan_steps(t):
    steps, dist = [], 1
    while dist < t:
        steps.append(dist)
        dist *= 2
    return steps


SUBLANES = 8


def _linear_scan(a, b, scratch, row, reverse):
    a_s, b_s, carry_s = scratch
    t = a.shape[0]
    groups = t // SUBLANES
    in_group = row & (SUBLANES - 1)
    for dist in _scan_steps(SUBLANES):
        if reverse:
            inside = in_group < SUBLANES - dist
            b = b + a * jnp.where(inside, _roll_rows(b, t - dist), 0.0)
            a = a * jnp.where(inside, _roll_rows(a, t - dist), 1.0)
        else:
            inside = in_group >= dist
            b = a * jnp.where(inside, _roll_rows(b, dist), 0.0) + b
            a = a * jnp.where(inside, _roll_rows(a, dist), 1.0)
    a_s[...] = a
    b_s[...] = b
    end = 0 if reverse else SUBLANES - 1
    ends = pl.ds(end, groups, stride=SUBLANES)
    ae, be = a_s[ends, :], b_s[ends, :]
    grow = lax.broadcasted_iota(jnp.int32, ae.shape, 0)
    shift = _shift_up if reverse else _shift_down
    for dist in _scan_steps(groups):
        be = ae * shift(be, dist, 0.0, grow) + be
        ae = ae * shift(ae, dist, 1.0, grow)
    incoming = shift(be, 1, 0.0, grow)
    for k in range(SUBLANES):
        carry_s[pl.ds(k, groups, stride=SUBLANES), :] = incoming
    return a_s[...] * carry_s[...] + b_s[...]


def _lru_fwd(gx, small, wr, wi):
    t = gx.shape[0]
    r_dim = gx.shape[1] // 2
    nb = r_dim // LRU_BLOCK_W

    def body(gate_ref, xb_ref, small_ref, wr_ref, wi_ref, y_ref, hs_ref, *scratch):
        row = lax.broadcasted_iota(jnp.int32, (t, LRU_BLOCK_W), 0)
        xb = xb_ref[...]
        _, xc, _, _, ig, _, a, mult = _lru_gates(xb, small_ref, wr_ref[...], wi_ref[...], row)
        hsv = _linear_scan(a, mult * (ig * xc), scratch, row, reverse=False)
        hs_ref[...] = hsv
        gelu, _ = _gelu_parts(gate_ref[...])
        y_ref[...] = (gelu * hsv).astype(y_ref.dtype)

    colb = lambda off: pl.BlockSpec((t, LRU_BLOCK_W), lambda n: (0, off + n))
    wspec = pl.BlockSpec((None, LRU_BLOCK_W, LRU_BLOCK_W), lambda n: (n, 0, 0))
    return _pcall(
        body, name="lru_fwd", grid=(nb,),
        in_specs=[colb(0), colb(nb), pl.BlockSpec((8, LRU_BLOCK_W), lambda n: (0, n)), wspec, wspec],
        out_specs=[colb(0), colb(0)], out_shape=[SDS((t, r_dim), BF16), SDS((t, r_dim), F32)],
        scratch_shapes=[pltpu.VMEM((t, LRU_BLOCK_W), F32) for _ in range(3)],
        compiler_params=_params(("parallel",)))(gx, gx, small, wr, wi)


def _lru_bwd(gx, hs, dy, small, wr, wi, plan=None):
    t = gx.shape[0]
    r_dim = gx.shape[1] // 2
    nb = r_dim // LRU_BLOCK_W

    def body(in_refs, out_refs, scratch):
        (gate_ref, xb_ref, hs_ref, dy_ref, small_ref, wr_ref, wi_ref), (dgx_ref, dsm_ref, dwr_ref, dwi_ref) = in_refs, out_refs
        row = lax.broadcasted_iota(jnp.int32, (t, LRU_BLOCK_W), 0)
        xb, hsv, dyv, smallv = xb_ref[...], hs_ref[...], dy_ref[...], small_ref
        wrv, wiv = wr_ref[...], wi_ref[...]
        xs, xc, xcb, r, ig, sp, a, mult = _lru_gates(xb, smallv, wrv, wiv, row)
        gelu, dgelu = _gelu_parts(gate_ref[...])
        dgx_ref[0] = (dyv * hsv * dgelu).astype(dgx_ref.dtype)
        dacc = _linear_scan(_shift_up(a, 1, 1.0, row), dyv * gelu, scratch, row, reverse=True)
        da = dacc * _shift_down(hsv, 1, 0.0, row)
        dmult = dacc * (ig * xc)
        dixc = dacc * mult
        dla = da * a - dmult * (a * a) / mult
        dr = dla * (-LRU_C * sp)
        dsp = jnp.sum(dla * (-LRU_C * r), axis=0, keepdims=True)
        dpr = dr * r * (1.0 - r)
        dpi = dixc * xc * ig * (1.0 - ig)
        dprb, dpib = dpr.astype(BF16), dpi.astype(BF16)
        dwr_ref[...] = _dot(xcb, dprb, "tn")
        dwi_ref[...] = _dot(xcb, dpib, "tn")
        dxc = dixc * ig + _dot(dprb, wrv, "nt") + _dot(dpib, wiv, "nt")
        dxb = dxc * smallv[3:4, :]
        for tap in range(3):
            dxb = dxb + _shift_up(dxc, 3 - tap, 0.0, row) * smallv[tap:tap + 1, :]
        dgx_ref[1] = dxb.astype(dgx_ref.dtype)
        lam = smallv[7:8, :]
        rows = [jnp.sum(dxc * xs[tap], axis=0, keepdims=True) for tap in range(4)]
        rows.append(jnp.sum(dxc, axis=0, keepdims=True))
        rows.append(jnp.sum(dpr, axis=0, keepdims=True))
        rows.append(jnp.sum(dpi, axis=0, keepdims=True))
        rows.append(-dsp * jax.nn.sigmoid(-lam))
        for k, rv in enumerate(rows):
            dsm_ref[k:k + 1, :] = rv

    colb = lambda off: pl.BlockSpec((t, LRU_BLOCK_W), lambda n: (0, off + n))
    wspec = pl.BlockSpec((None, LRU_BLOCK_W, LRU_BLOCK_W), lambda n: (n, 0, 0))
    sspec = pl.BlockSpec((8, LRU_BLOCK_W), lambda n: (0, n))
    return _host_call(
        body, "lru_bwd", nb, [gx, gx, hs, dy, small, wr, wi],
        [colb(0), colb(nb), colb(0), colb(0), sspec, wspec, wspec],
        [SDS((2, t, r_dim), BF16), SDS((8, r_dim), F32), SDS((nb, LRU_BLOCK_W, LRU_BLOCK_W), F32),
         SDS((nb, LRU_BLOCK_W, LRU_BLOCK_W), F32)],
        [pl.BlockSpec((2, t, LRU_BLOCK_W), lambda n: (0, 0, n)), sspec, wspec, wspec],
        [pltpu.VMEM((t, LRU_BLOCK_W), F32) for _ in range(3)], plan)


def _adam(w, g, m, v):
    m2 = ADAM_B1 * m + (1.0 - ADAM_B1) * g
    v2 = ADAM_B2 * v + (1.0 - ADAM_B2) * (g * g)
    m_hat = m2 / (1.0 - ADAM_B1 ** ADAM_STEP)
    v_hat = v2 / (1.0 - ADAM_B2 ** ADAM_STEP)
    return -ADAM_LR * (m_hat / (jnp.sqrt(v_hat) + ADAM_EPS) + ADAM_WD * w), m2, v2


def _mod_fwd(c_all, mod_w, mod_b_cols):
    nl, d, cols = mod_w.shape
    nbatch = c_all.shape[0]

    def body(c_ref, w_ref, b_ref, o_ref):
        cv = c_ref[...]
        ca = (cv * jax.nn.sigmoid(cv)).astype(BF16)
        o_ref[...] = _dot(ca, w_ref[...].astype(BF16), "nn") + b_ref[...]

    return _pcall(
        body, name="mod_fwd", grid=(nl,),
        in_specs=[pl.BlockSpec((nbatch, d), lambda l: (0, 0)), pl.BlockSpec((None, d, cols), lambda l: (l, 0, 0)),
                  pl.BlockSpec((None, 1, cols), lambda l: (l, 0, 0))],
        out_specs=pl.BlockSpec((None, nbatch, cols), lambda l: (l, 0, 0)), out_shape=SDS((nl, nbatch, cols), F32),
        compiler_params=_params(("parallel",)))(c_all, mod_w, mod_b_cols)


def _mod_w_update(c_all, dmod_cols, w, m, v):
    nl, d, cols = w.shape
    nbatch = c_all.shape[0]
    tr = _tile(d, (256, 128))

    def body(c_ref, dm_ref, w_ref, m_ref, v_ref, g_ref, dl_ref, m2_ref, v2_ref):
        cv = c_ref[...]
        ca = (cv * jax.nn.sigmoid(cv)).astype(BF16)
        g = _dot(ca, dm_ref[...].astype(BF16), "tn")
        g_ref[...] = g
        dl_ref[...], m2_ref[...], v2_ref[...] = _adam(w_ref[...], g, m_ref[...], v_ref[...])

    wblk = pl.BlockSpec((None, tr, cols), lambda l, i: (l, i, 0))
    return _pcall(
        body, name="mod_w_update", grid=(nl, d // tr),
        in_specs=[pl.BlockSpec((nbatch, tr), lambda l, i: (0, i)), pl.BlockSpec((None, nbatch, cols), lambda l, i: (l, 0, 0)),
                  wblk, wblk, wblk],
        out_specs=[wblk] * 4, out_shape=[SDS(w.shape, F32)] * 4,
        compiler_params=_params(("parallel", "parallel")))(c_all, dmod_cols, w, m, v)


def _adam_update(name, w, m, v, gparts):
    rows, cols = w.shape
    tr = _tile(rows, (256, 128, 64, 32, 16, 8))
    npart = len(gparts)

    def body(*refs):
        w_ref, m_ref, v_ref = refs[:3]
        g_refs = refs[3:3 + npart]
        g_ref, dl_ref, m2_ref, v2_ref = refs[3 + npart:]
        g = g_refs[0][...].astype(F32)
        for gr in g_refs[1:]:
            g = g + gr[...].astype(F32)
        g_ref[...] = g
        dl_ref[...], m2_ref[...], v2_ref[...] = _adam(w_ref[...], g, m_ref[...], v_ref[...])

    blk = pl.BlockSpec((tr, cols), lambda i: (i, 0))
    return _pcall(body, name=name, grid=(rows // tr,), in_specs=[blk] * (3 + npart), out_specs=[blk] * 4,
                  out_shape=[SDS((rows, cols), F32)] * 4, compiler_params=_params(("parallel",)))(w, m, v, *gparts)


def _adam_shard(name, w, m, v, part4, recv3, chip_idx, first=0, fills=None):
    p, r, cdim = w.shape
    pg = part4.shape[0]
    tr = _tile(r, (256, 176, 160, 128, 64, 32, 16))

    def body(chip_ref, w_ref, m_ref, v_ref, own_ref, r0_ref, r1_ref, r2_ref, *rest):
        g_ref, dl_ref, m2_ref, v2_ref = rest[-4:]
        g = own_ref[...].astype(F32) + r0_ref[...].astype(F32) + r1_ref[...].astype(F32) + r2_ref[...].astype(F32)
        g_ref[...] = g
        dl_ref[...], m2_ref[...], v2_ref[...] = _adam(w_ref[...], g, m_ref[...], v_ref[...])

    blk = pl.BlockSpec((None, tr, cdim), lambda q, i, chip_ref: (first + q, i, 0))
    blk4 = (None, None, tr, cdim)
    slot = lambda s: pl.BlockSpec(blk4, lambda q, i, chip_ref: (s, q, i, 0))
    fills = list(fills or [])
    grid_spec = pltpu.PrefetchScalarGridSpec(
        num_scalar_prefetch=1, grid=(pg, r // tr),
        in_specs=[blk, blk, blk, pl.BlockSpec(blk4, lambda q, i, chip_ref: (q, chip_ref[0], i, 0)), slot(0), slot(1), slot(2)]
        + [ANY] * len(fills),
        out_specs=[blk] * 4)
    return _pcall(body, name=name, grid_spec=grid_spec, out_shape=[SDS((p, r, cdim), F32)] * 4,
                  input_output_aliases={8 + k: k for k in range(len(fills))},
                  compiler_params=_params(("parallel", "parallel")))(chip_idx, w, m, v, part4, recv3, recv3, recv3, *fills)


def _sum_devices(gathered, name):
    _, rows, cols = gathered.shape
    tr = _tile(rows, (512, 256, 128, 64, 32, 16, 8))

    def body(g_ref, o_ref):
        acc = g_ref[0].astype(F32)
        for k in range(1, N_DEV):
            acc = acc + g_ref[k].astype(F32)
        o_ref[...] = acc

    return _pcall(body, name=name, grid=(rows // tr,), in_specs=[pl.BlockSpec((N_DEV, tr, cols), lambda i: (0, i, 0))],
                  out_specs=pl.BlockSpec((tr, cols), lambda i: (i, 0)), out_shape=SDS((rows, cols), F32),
                  compiler_params=_params(("parallel",)))(gathered)


def _pack_flat(parts, width, row_mult, dtype):
    flat = jnp.concatenate([p.reshape(-1).astype(dtype) for p in parts])
    unit = width * row_mult
    pad = (-flat.shape[0]) % unit
    if pad:
        flat = jnp.concatenate([flat, jnp.zeros((pad,), dtype)])
    return flat.reshape(-1, width)


def _unpack_flat(flat, shapes):
    out, off = [], 0
    for shp in shapes:
        size = math.prod(shp)
        out.append(flat[off:off + size].reshape(shp))
        off += size
    return out


def kernel(x, c, mod_w, mod_b, norm_g, ffn_w_gu, ffn_w_down, sb_w_qkv, sb_w_o, lru_w_in, lru_conv_w, lru_conv_b, lru_w_r, lru_b_r, lru_w_i, lru_b_i, lru_lambda, lru_w_out, final_norm_g, loss_target, m_mod_w, m_mod_b, m_norm_g, m_ffn_w_gu, m_ffn_w_down, m_sb_w_qkv, m_sb_w_o, m_lru_w_in, m_lru_conv_w, m_lru_conv_b, m_lru_w_r, m_lru_b_r, m_lru_w_i, m_lru_b_i, m_lru_lambda, m_lru_w_out, m_final_norm_g, v_mod_w, v_mod_b, v_norm_g, v_ffn_w_gu, v_ffn_w_down, v_sb_w_qkv, v_sb_w_o, v_lru_w_in, v_lru_conv_w, v_lru_conv_b, v_lru_w_r, v_lru_b_r, v_lru_w_i, v_lru_b_i, v_lru_lambda, v_lru_w_out, v_final_norm_g):
    weights = dict(mod_w=mod_w, mod_b=mod_b, norm_g=norm_g, ffn_w_gu=ffn_w_gu, ffn_w_down=ffn_w_down, sb_w_qkv=sb_w_qkv,
                   sb_w_o=sb_w_o, lru_w_in=lru_w_in, lru_conv_w=lru_conv_w, lru_conv_b=lru_conv_b, lru_w_r=lru_w_r,
                   lru_b_r=lru_b_r, lru_w_i=lru_w_i, lru_b_i=lru_b_i, lru_lambda=lru_lambda, lru_w_out=lru_w_out,
                   final_norm_g=final_norm_g)
    mom_m = dict(mod_w=m_mod_w, mod_b=m_mod_b, norm_g=m_norm_g, ffn_w_gu=m_ffn_w_gu, ffn_w_down=m_ffn_w_down,
                 sb_w_qkv=m_sb_w_qkv, sb_w_o=m_sb_w_o, lru_w_in=m_lru_w_in, lru_conv_w=m_lru_conv_w,
                 lru_conv_b=m_lru_conv_b, lru_w_r=m_lru_w_r, lru_b_r=m_lru_b_r, lru_w_i=m_lru_w_i, lru_b_i=m_lru_b_i,
                 lru_lambda=m_lru_lambda, lru_w_out=m_lru_w_out, final_norm_g=m_final_norm_g)
    mom_v = dict(mod_w=v_mod_w, mod_b=v_mod_b, norm_g=v_norm_g, ffn_w_gu=v_ffn_w_gu, ffn_w_down=v_ffn_w_down,
                 sb_w_qkv=v_sb_w_qkv, sb_w_o=v_sb_w_o, lru_w_in=v_lru_w_in, lru_conv_w=v_lru_conv_w,
                 lru_conv_b=v_lru_conv_b, lru_w_r=v_lru_w_r, lru_b_r=v_lru_b_r, lru_w_i=v_lru_w_i, lru_b_i=v_lru_b_i,
                 lru_lambda=v_lru_lambda, lru_w_out=v_lru_w_out, final_norm_g=v_final_norm_g)
    names = list(weights)

    t, d = x.shape[1], x.shape[2]
    n_layers = mod_w.shape[0]
    r_dim = lru_w_out.shape[1] * N_DEV
    ng, rs = d // N_DEV, r_dim // N_DEV
    mod_cols = mod_w.shape[2]
    nblk = lru_w_r.shape[1]
    xi, yi, ci = _mesh_pos()
    me = 4 * xi + 2 * yi + ci
    chip = 2 * xi + yi
    x2, target = x.reshape(t, d), loss_target.reshape(t, d)

    lru_small_shard = jnp.concatenate([lru_conv_w[0], lru_conv_b, lru_b_r, lru_b_i, lru_lambda], axis=0)
    small1 = _pack_flat([c, norm_g, lru_small_shard], LANES, 8, F32)
    n_small1 = small1.shape[0]
    all1 = _allgather(small1[None], "gather_small").reshape(N_DEV, n_small1 * LANES)
    c_all = all1[:, :d]
    norm_full = jnp.transpose(all1[:, d:d + 6 * ng].reshape(N_DEV, n_layers, 3, ng), (1, 2, 0, 3)).reshape(n_layers, 3, d)
    lru_small = jnp.transpose(all1[:, d + 6 * ng:d + 6 * ng + 8 * rs].reshape(N_DEV, 8, rs), (1, 0, 2)).reshape(8, r_dim)

    mod_b_cols = lax.dynamic_slice_in_dim(mod_b, me * mod_cols, mod_cols, axis=1).reshape(n_layers, 1, mod_cols)
    mod_part = _mod_fwd(c_all, mod_w, mod_b_cols)

    assert sb_w_qkv.shape[0] == 1 and lru_w_in.shape[0] == 1, "one stick-breaking and one RG-LRU layer"
    n_ffn = 2 * n_layers
    fc = ffn_w_gu.shape[3]
    cw_in = lru_w_in.shape[2]
    pieces = {("ffn_w_gu", q): ffn_w_gu[q // 2, q % 2][None] for q in range(n_ffn)}
    pieces.update({("ffn_w_down", q): ffn_w_down[q // 2, q % 2][None] for q in range(n_ffn)})
    pieces.update({("sb_w_qkv", 0): sb_w_qkv, ("sb_w_o", 0): sb_w_o, ("lru_w_in", 0): lru_w_in, ("lru_w_out", 0): lru_w_out})
    col_window = {("sb_w_qkv", 0)}
    first = [("ffn_w_gu", 0)]
    behind = {"l0s0_gu": [("ffn_w_down", 0)], "l0s0_down": [("sb_w_qkv", 0), ("sb_w_o", 0)],
              "l0s2_gu": [("ffn_w_down", n_ffn - 1)], "l0s2_down": [("ffn_w_down", 2)],
              "l1s0_gu": [("lru_w_in", 0)], "l1s0_down": [("lru_w_out", 0)]}
    behind["sb_fwd"] = [key for key in pieces if key not in first + sum(behind.values(), [])]
    gathered = {}

    def gather_plan(keys):
        return _gather_plan([pieces[key].astype(BF16) for key in keys], [key in col_window for key in keys])

    def hosting(name, call):
        keys = behind.get(name, [])
        outs = call(gather_plan(keys) if keys else None)
        gathered.update(zip(keys, outs[len(outs) - len(keys):]))
        return outs[:len(outs) - len(keys)]

    mod_all, *landed = _run_comm(_merge_plans([_gather_plan([mod_part], [False]), gather_plan(first)]), "gather_mod_and_first")
    gathered.update(zip(first, landed))
    mod_mine = lax.dynamic_index_in_dim(mod_all, me, axis=2, keepdims=False)
    mod_mine = mod_mine.reshape(n_layers, 3, 3, d)
    wr_b, wi_b = lru_w_r[0].astype(BF16), lru_w_i[0].astype(BF16)
    eye2 = jnp.eye(2 * cw_in, dtype=BF16).reshape(2, cw_in, 2 * cw_in)

    def w_gu(q):
        return gathered[("ffn_w_gu", q)]

    def w_d4(q):
        return gathered[("ffn_w_down", q)].reshape(1, HIDDEN_CHUNKS, fc, d)

    saved = []
    xcur = x2
    for layer in range(n_layers):
        for sub in range(3):
            gvec = norm_full[layer, sub].reshape(1, d)
            shift = mod_mine[layer, sub, 0].reshape(1, d)
            scale1p = 1.0 + mod_mine[layer, sub, 1].reshape(1, d)
            gmul = 1.0 + mod_mine[layer, sub, 2].reshape(1, d)
            tag = f"l{layer}s{sub}"
            h = _norm_fwd(xcur, gvec, scale1p, shift, tag + "_norm")
            rec = dict(x=xcur, h=h, g=gvec, scale1p=scale1p, gmul=gmul, w=MACARON_W if sub != 1 else 1.0)
            if sub != 1:
                lj = layer * 2 + sub // 2
                gu2, a = hosting(tag + "_gu", lambda plan: _ffn_gu(tag + "_gu", h, w_gu(lj), plan))
                yv, xcur = hosting(tag + "_down", lambda plan: _ffn_down(tag + "_down", a, w_d4(lj), xcur, gmul, plan))
                rec.update(kind="ffn", lj=lj, gu2=gu2, a=a, y=yv)
            elif layer % 2 == 0:
                w_qkv = gathered[("sb_w_qkv", 0)][0]
                w_o = gathered[("sb_w_o", 0)].reshape(d, d)
                qkv = _mm_nn(tag + "_qkv", h, w_qkv, BF16)[0]
                o, ltot = hosting("sb_fwd", lambda plan: _sb_fwd(qkv, d, plan))
                yv, xcur = _mm_nn(tag + "_wo", o, w_o, [BF16, F32], extras=[(xcur, "tile"), (gmul, "row")],
                                  epilogue=lambda accs, ex: (accs[0], ex[0] + ex[1] * accs[0]))
                rec.update(kind="sb", qkv=qkv, o=o, ltot=ltot, y=yv, w_qkv=w_qkv, w_o=w_o)
            else:
                w_in = _chunks_to_cols("lru_w_in_cols", gathered[("lru_w_in", 0)][0], eye2)
                w_out = gathered[("lru_w_out", 0)].reshape(r_dim, d)
                gx = _mm_nn(tag + "_win", h, w_in, F32)[0]
                ymix, hs = _lru_fwd(gx, lru_small, wr_b, wi_b)
                yv, xcur = _mm_nn(tag + "_wout", ymix, w_out, [BF16, F32], extras=[(xcur, "tile"), (gmul, "row")],
                                  epilogue=lambda accs, ex: (accs[0], ex[0] + ex[1] * accs[0]))
                rec.update(kind="lru", gx=gx, hs=hs, ymix=ymix, y=yv, w_in=w_in, w_out=w_out)
            saved.append(rec)

    last = saved[-1]
    dxo, dy, head_sums = _loss_head(xcur, target, final_norm_g.reshape(1, d), (last["w"] * last["gmul"]))
    loss = lax.psum(head_sums[1, 0], ("x", "y", "c"))
    dgf = head_sums[0]

    c_idx = jnp.reshape(ci, (1,)).astype(jnp.int32)
    chip_idx = jnp.reshape(chip, (1,)).astype(jnp.int32)
    grads, reduced = {}, {}
    to_pair = []
    to_chips = []

    def sibling_plan(only=None):
        keys = [key for key in to_pair if only is None or key in only]
        if not keys:
            return None, keys
        return _exchange_plan([grads[key] for key in keys], [key in col_window for key in keys], 4, _sibling_route), keys

    def sibling_done(keys, recv4):
        for key, r4 in zip(keys, recv4):
            to_pair.remove(key)
            to_chips.append((key, _pair_sum(grads[key], r4, c_idx, f"rs_pair_sum_{key[0]}{key[1]}", cols=key in col_window)))

    def chip_plan(only=None):
        items = [item for item in to_chips if only is None or item[0] in only]
        if not items:
            return None, items
        return _exchange_plan([p4 for _, p4 in items], [False] * len(items), 3, _chip_route), items

    def chips_done(items, recv3):
        for item, r3 in zip(items, recv3):
            to_chips.remove(item)
            reduced[item[0]] = (item[1], r3)

    def behind(call, make_plan, done, more=None):
        plan, items = make_plan()
        n_mine = len(plan.outs) if plan else 0
        n_more = len(more.outs) if more else 0
        outs = call(_merge_plans([plan, more]))
        n_own = len(outs) - n_mine - n_more
        done(items, outs[n_own:n_own + n_mine])
        return list(outs[:n_own]) + list(outs[n_own + n_mine:])

    carried = {
        "l1s1b_dymix": ("sibling", None), "lru_bwd": ("chips", [("ffn_w_gu", n_ffn - 1)]),
        "l1s0b_da": ("sibling", None), "l1s0b_dwgu": ("chips", [("ffn_w_down", n_ffn - 1)]),
        "l1s0b_dh": ("chips", [("lru_w_out", 0), ("lru_w_in", 0)]),
        "l0s2b_da": ("sibling", None), "l0s2b_dwgu": ("chips", [("ffn_w_down", n_ffn - 2)]),
    }

    def carrying(name, call):
        if name not in carried:
            return call(None)
        stage, only = carried[name]
        if stage == "sibling":
            return behind(call, functools.partial(sibling_plan, only), sibling_done)
        return behind(call, functools.partial(chip_plan, only), chips_done)

    def at_once(make_plan, done, name):
        plan, items = make_plan()
        if plan:
            done(items, _run_comm(plan, name))

    def add_grad(key, value):
        grads[key] = value
        to_pair.append(key)

    dmod = [[None] * 3 for _ in range(n_layers)]
    dnorm = [[None] * 3 for _ in range(n_layers)]
    dlru_small = wri_all = None
    for idx in reversed(range(len(saved))):
        rec = saved[idx]
        layer, sub = divmod(idx, 3)
        tag = f"l{layer}s{sub}b"
        if rec["kind"] == "ffn" and idx > 0:
            lj = rec["lj"]
            (dgu2,) = carrying(tag + "_da", lambda plan: _ffn_da(tag + "_da", dy, w_d4(lj), rec["gu2"], plan))
            dwd = _ffn_dwd(tag + "_dwd", rec["a"], dy)[0].reshape(gathered[("ffn_w_down", lj)].shape)
            (dwgu,) = carrying(tag + "_dwgu", lambda plan: _ffn_dwgu(tag + "_dwgu", rec["h"], dgu2, plan))
            (dh,) = carrying(tag + "_dh", lambda plan: _ffn_dh(tag + "_dh", dgu2, w_gu(lj), plan))
            add_grad(("ffn_w_down", lj), dwd)
            add_grad(("ffn_w_gu", lj), dwgu)
        elif rec["kind"] == "ffn":
            lj = rec["lj"]
            at_once(sibling_plan, sibling_done, "rs_sibling_" + tag)
            (dgu2,) = behind(lambda plan: _ffn_da(tag + "_da", dy, w_d4(lj), rec["gu2"], plan), chip_plan, chips_done)
            add_grad(("ffn_w_down", lj), _ffn_dwd(tag + "_dwd", rec["a"], dy)[0].reshape(gathered[("ffn_w_down", lj)].shape))
            at_once(sibling_plan, sibling_done, "rs_sibling_" + tag + "_dwd")
            (dwgu,) = behind(lambda plan: _ffn_dwgu(tag + "_dwgu", rec["h"], dgu2, plan), chip_plan, chips_done)
            add_grad(("ffn_w_gu", lj), dwgu)
            at_once(sibling_plan, sibling_done, "rs_sibling_" + tag + "_dwgu")
            (dh,) = behind(lambda plan: _ffn_dh(tag + "_dh", dgu2, w_gu(lj), plan), chip_plan, chips_done)
        elif rec["kind"] == "sb":
            at_once(sibling_plan, sibling_done, "rs_sibling_" + tag)
            do = _mm_nt(tag + "_do", dy, rec["w_o"], BF16)
            dwo = _mm_tn(tag + "_dwo", rec["o"], dy, BF16)
            wri = _pack_flat([dwr, dwi], LANES, 512, BF16)[None]
            dqkv3, wri_all = behind(lambda plan: _sb_bwd(rec["qkv"], do, rec["ltot"], d, plan), chip_plan, chips_done,
                                    more=_gather_plan([wri], [False]))
            add_grad(("sb_w_o", 0), dwo.reshape(gathered[("sb_w_o", 0)].shape))
            dh = _mm_nt_stack(tag + "_dh", dqkv3, rec["w_qkv"], F32)
            add_grad(("sb_w_qkv", 0), _mm_tn_stack(tag + "_dwqkv", rec["h"], dqkv3, BF16)[None])
        else:
            (dymix,) = carrying(tag + "_dymix", lambda plan: _mm_nt(tag + "_dymix", dy, rec["w_out"], F32, plan)
                                if plan else [_mm_nt(tag + "_dymix", dy, rec["w_out"], F32)])
            dwout = _mm_tn(tag + "_dwout", rec["ymix"], dy, BF16).reshape(gathered[("lru_w_out", 0)].shape)
            dgx2, dlru_small, dwr, dwi = carrying("lru_bwd", lambda plan: _lru_bwd(rec["gx"], rec["hs"], dymix, lru_small,
                                                                                wr_b, wi_b, plan))
            dh = _mm_nt_stack(tag + "_dh", dgx2, rec["w_in"], F32)
            dw_in = _mm_tn_stack(tag + "_dwin", rec["h"], dgx2, BF16)
            add_grad(("lru_w_out", 0), dwout)
            add_grad(("lru_w_in", 0), _cols_to_chunks("lru_w_in_chunks", dw_in, eye2)[None])
        prev = saved[idx - 1] if idx > 0 else None
        gw_prev = (prev["w"] * prev["gmul"]) if prev is not None else jnp.zeros((1, d), F32)
        dxo, dy, sums = _adaln_bwd(dh, rec["x"], rec["y"], dxo, rec["g"], rec["scale1p"], rec["w"], gw_prev, tag + "_adaln")
        dmod[layer][sub] = sums[0:3]
        dnorm[layer][sub] = sums[3]
    grad_x = dxo.reshape(x.shape)

    dmod_mine = jnp.stack([jnp.stack(dmod[layer]) for layer in range(n_layers)])
    dnorm_mine = jnp.stack([jnp.stack(dnorm[layer]) for layer in range(n_layers)])
    assert not to_pair and not to_chips
    small_shapes = [(n_layers, 9 * d), (n_layers, 3, d), (8, r_dim), (d,)]
    small3 = _pack_flat([dmod_mine, dnorm_mine, dlru_small, dgf], LANES, 256, F32)
    n_small3 = small3.shape[0]
    all3 = _allgather(small3[None], "gather_small_grads").reshape(N_DEV, n_small3, LANES)
    gsum = _sum_devices(all3, "sum_small_grads").reshape(-1)
    g_mod_b, g_norm_full, g_lru_small, g_final = _unpack_flat(gsum, small_shapes)
    wri_sum = _sum_devices(wri_all.reshape(N_DEV, -1, LANES), "sum_gate_weight_grads").reshape(-1)
    g_wr, g_wi = _unpack_flat(wri_sum, [lru_w_r.shape, lru_w_i.shape])
    dmod_all = all3.reshape(N_DEV, -1)[:, :n_layers * 9 * d].reshape(N_DEV, n_layers, N_DEV, mod_cols)
    dmod_cols = jnp.transpose(lax.dynamic_index_in_dim(dmod_all, me, axis=2, keepdims=False), (1, 0, 2))

    out_g, out_d, out_m, out_v = {}, {}, {}, {}
    out_g["mod_w"], out_d["mod_w"], out_m["mod_w"], out_v["mod_w"] = _mod_w_update(c_all, dmod_cols, mod_w, m_mod_w, v_mod_w)

    g_norm_shard = lax.dynamic_slice_in_dim(g_norm_full, me * ng, ng, axis=2)
    g_lru_shard = lax.dynamic_slice_in_dim(g_lru_small, me * rs, rs, axis=1)
    small_grads = dict(mod_b=g_mod_b, norm_g=g_norm_shard, lru_conv_w=g_lru_shard[0:4].reshape(lru_conv_w.shape),
                       lru_conv_b=g_lru_shard[4:5], lru_b_r=g_lru_shard[5:6], lru_b_i=g_lru_shard[6:7],
                       lru_lambda=g_lru_shard[7:8], final_norm_g=g_final)
    for n, g in (("lru_w_r", g_wr), ("lru_w_i", g_wi)):
        view = lambda arr: arr.reshape(-1, LRU_BLOCK_W)
        outs = _adam_update("adam_" + n, view(weights[n]), view(mom_m[n]), view(mom_v[n]), [view(g)])
        out_g[n], out_d[n], out_m[n], out_v[n] = [o.reshape(weights[n].shape) for o in outs]
    small_names = list(small_grads)
    sw = _pack_flat([weights[n] for n in small_names], LANES, 256, F32)
    sg = _pack_flat([small_grads[n] for n in small_names], LANES, 256, F32)
    sm = _pack_flat([mom_m[n] for n in small_names], LANES, 256, F32)
    sv = _pack_flat([mom_v[n] for n in small_names], LANES, 256, F32)
    s_outs = _adam_update("adam_small", sw, sm, sv, [sg])
    small_shapes2 = [weights[n].shape for n in small_names]
    for dst, flat in zip((out_g, out_d, out_m, out_v), s_outs):
        for n, arr in zip(small_names, _unpack_flat(flat.reshape(-1), small_shapes2)):
            dst[n] = arr

    for n in ["ffn_w_gu", "ffn_w_down", "sb_w_qkv", "sb_w_o", "lru_w_in", "lru_w_out"]:
        shp = weights[n].shape
        shard3 = (math.prod(shp[:-2]),) + shp[-2:]
        view = lambda arr: arr.reshape(shard3)
        outs = None
        for q in range(shard3[0]):
            fills = outs if outs is not None else [lax.empty(shard3, F32) for _ in range(4)]
            p4, r3 = reduced[(n, q)]
            outs = _adam_shard(f"adam_{n}{q}", view(weights[n]), view(mom_m[n]), view(mom_v[n]), p4, r3, chip_idx,
                               first=q, fills=fills if shard3[0] > 1 else None)
        out_g[n], out_d[n], out_m[n], out_v[n] = [o.reshape(shp) for o in outs]

    return (loss, grad_x, *[out_g[n] for n in names], *[out_d[n] for n in names], *[out_m[n] for n in names],
            *[out_v[n] for n in names])
```

```python
import functools
import math

import jax
import jax.numpy as jnp
from jax import lax
from jax.experimental import pallas as pl
from jax.experimental.pallas import tpu as pltpu

F32 = jnp.float32
BF16 = jnp.bfloat16
SDS = jax.ShapeDtypeStruct
MESH = pl.DeviceIdType.MESH
ANY = pl.BlockSpec(memory_space=pl.ANY)

N_DEV = 8
LANES = 128
HEAD_DIM = 64
LRU_BLOCK_W = 128
LRU_C = 8.0
MACARON_W = 0.5
NORM_EPS = 1e-6
ADAM_LR = 0.001
ADAM_B1 = 0.9
ADAM_B2 = 0.999
ADAM_EPS = 1e-08
ADAM_WD = 0.01
ADAM_STEP = 10
VMEM_LIMIT = 56 * 1024 * 1024
GELU_C = math.sqrt(2.0 / math.pi)
GELU_K = 0.044715

DIMS = {
    "nn": (((1,), (0,)), ((), ())),
    "nt": (((1,), (1,)), ((), ())),
    "tn": (((0,), (0,)), ((), ())),
}


def _pcall(body, **kw):
    return pl.pallas_call(body, **kw)


def _params(sem=None):
    return pltpu.CompilerParams(dimension_semantics=sem, vmem_limit_bytes=VMEM_LIMIT)


def _tile(n, prefs):
    for p in prefs:
        if n % p == 0:
            return p
    return n


def _dot(a, b, dims):
    return lax.dot_general(a, b, DIMS[dims], preferred_element_type=F32)


def _softplus(z):
    return jnp.maximum(z, 0.0) + jnp.log(1.0 + jnp.exp(-jnp.abs(z)))


def _sigmoid(z):
    return 0.5 * jnp.tanh(0.5 * z) + 0.5


def _mesh_pos():
    return lax.axis_index("x"), lax.axis_index("y"), lax.axis_index("c")


def _allgather(xs, name, cols=False):
    return _run_comm(_gather_plan([xs], [cols]), name)[0]


class _CommPlan:
    def __init__(self, ins, outs, n_remote, n_local, phases):
        self.ins, self.outs, self.n_remote, self.n_local, self.phases = ins, outs, n_remote, n_local, phases

    def scratch(self):
        return [pltpu.SemaphoreType.DMA((self.n_remote,)), pltpu.SemaphoreType.DMA((self.n_remote,)),
                pltpu.SemaphoreType.DMA((max(self.n_local, 1),))]


def _merge_plans(plans):
    plans = [p for p in plans if p is not None]
    if len(plans) <= 1:
        return plans[0] if plans else None

    def phase(k):
        def run(in_refs, out_refs, send_sems, recv_sems, local_sems, r0=0, l0=0):
            i0 = o0 = 0
            for p in plans:
                p.phases[k](in_refs[i0:i0 + len(p.ins)], out_refs[o0:o0 + len(p.outs)], send_sems, recv_sems, local_sems, r0, l0)
                i0, o0, r0, l0 = i0 + len(p.ins), o0 + len(p.outs), r0 + p.n_remote, l0 + p.n_local
        return run

    return _CommPlan(sum([p.ins for p in plans], []), sum([p.outs for p in plans], []), sum(p.n_remote for p in plans),
                     sum(p.n_local for p in plans), [phase(0), phase(1), phase(2)])


def _run_comm(plan, name):
    n_in, n_out = len(plan.ins), len(plan.outs)

    def body(*refs):
        in_refs, out_refs, sems = refs[:n_in], refs[n_in:n_in + n_out], refs[n_in + n_out:]
        for phase in plan.phases:
            phase(in_refs, out_refs, *sems)

    return _pcall(body, name=name, out_shape=plan.outs, in_specs=[ANY] * n_in, out_specs=[ANY] * n_out,
                  scratch_shapes=plan.scratch())(*plan.ins)


def _col_window(ref, idx, width):
    return ref.at[:, :, pl.ds(pl.multiple_of(idx * width, math.gcd(width, LANES)), width)]


def _gather_plan(shards, cols):
    n = len(shards)
    outs = [SDS((s.shape[0], s.shape[1], N_DEV * s.shape[2]) if cl else (s.shape[0], N_DEV) + s.shape[1:], s.dtype)
            for s, cl in zip(shards, cols)]

    def copies(a, in_refs, out_refs, send_sems, recv_sems, local_sems, r0=0, l0=0):
        x, y, c = _mesh_pos()
        sibling = (x, y, 1 - c)
        chips = [(1 - x, y), (x, 1 - y), (1 - x, 1 - y)]
        width = shards[a].shape[2]

        def block(px, py, pc):
            idx = 4 * px + 2 * py + pc
            return _col_window(out_refs[a], idx, width) if cols[a] else out_refs[a].at[:, idx]

        def copy(k, owner, to, src=None):
            sem = r0 + 7 * a + k
            return pltpu.make_async_remote_copy(
                src_ref=block(*owner) if src is None else src, dst_ref=block(*owner),
                send_sem=send_sems.at[sem], recv_sem=recv_sems.at[sem], device_id=to, device_id_type=MESH)

        me = (x, y, c)
        first = [copy(0, me, sibling, src=in_refs[a])]
        first += [copy(1 + j, me, (*chip, c), src=in_refs[a]) for j, chip in enumerate(chips)]
        passed = [copy(4 + j, (*chip, c), sibling) for j, chip in enumerate(chips)]
        landed = [copy(1 + j, (*chip, c), me) for j, chip in enumerate(chips)]
        from_sibling = [copy(0, sibling, me)] + [copy(4 + j, (*chip, 1 - c), me) for j, chip in enumerate(chips)]
        mine = pltpu.make_async_copy(in_refs[a], block(*me), local_sems.at[l0 + a])
        return first, passed, landed, from_sibling, mine

    def start(*refs):
        for a in range(n):
            first, _, _, _, mine = copies(a, *refs)
            mine.start()
            for cp in first:
                cp.start()

    def pass_on(*refs):
        for a in range(n):
            _, passed, landed, _, _ = copies(a, *refs)
            for cp, fwd in zip(landed, passed):
                cp.wait_recv()
                fwd.start()

    def finish(*refs):
        for a in range(n):
            first, passed, _, from_sibling, mine = copies(a, *refs)
            for cp in from_sibling:
                cp.wait_recv()
            for cp in first + passed:
                cp.wait_send()
            mine.wait()

    return _CommPlan(list(shards), outs, 7 * n, n, [start, pass_on, finish])


def _exchange_plan(srcs, cols, n_slots, route):
    n = len(srcs)
    outs = []
    for g, cl in zip(srcs, cols):
        shard = (g.shape[0], g.shape[1], g.shape[2] // N_DEV) if cl else (g.shape[0],) + g.shape[2:]
        outs.append(SDS((n_slots,) + shard, g.dtype))

    def copies(in_refs, out_refs, send_sems, recv_sems, local_sems, r0=0, l0=0):
        x, y, c = _mesh_pos()
        made = []
        for a in range(n):
            for s in range(n_slots):
                chunk, target = route(x, y, c, s)
                src = _col_window(in_refs[a], chunk, outs[a].shape[3]) if cols[a] else in_refs[a].at[:, chunk]
                sem = r0 + a * n_slots + s
                made.append(pltpu.make_async_remote_copy(
                    src_ref=src, dst_ref=out_refs[a].at[s], send_sem=send_sems.at[sem], recv_sem=recv_sems.at[sem],
                    device_id=target, device_id_type=MESH))
        return made

    def start(*refs):
        for cp in copies(*refs):
            cp.start()

    def nothing(*refs):
        pass

    def finish(*refs):
        made = copies(*refs)
        for cp in made:
            cp.wait_recv()
        for cp in made:
            cp.wait_send()

    return _CommPlan(list(srcs), outs, n * n_slots, 0, [start, nothing, finish])


def _sibling_route(x, y, c, k):
    return 2 * k + 1 - c, (x, y, 1 - c)


def _chip_route(x, y, c, j):
    px, py = [(1 - x, y), (x, 1 - y), (1 - x, 1 - y)][j]
    return 2 * px + py, (px, py, c)


def _pair_sum(grads, recv4, c_idx, name, cols=False):
    _, p, r, cdim = recv4.shape
    tr = _tile(r, (512, 256, 176, 160, 128, 64, 32, 16))

    def body(c_ref, a_ref, b_ref, o_ref):
        o_ref[...] = (a_ref[...].astype(F32) + b_ref[...].astype(F32)).astype(o_ref.dtype)

    blk = (None, None, tr, cdim)
    if cols:
        own = pl.BlockSpec((None, tr, cdim), lambda k, q, i, c_ref: (q, i, 2 * k + c_ref[0]))
    else:
        own = pl.BlockSpec(blk, lambda k, q, i, c_ref: (q, 2 * k + c_ref[0], i, 0))
    grid_spec = pltpu.PrefetchScalarGridSpec(
        num_scalar_prefetch=1, grid=(4, p, r // tr),
        in_specs=[own, pl.BlockSpec(blk, lambda k, q, i, c_ref: (k, q, i, 0))],
        out_specs=pl.BlockSpec(blk, lambda k, q, i, c_ref: (q, k, i, 0)))
    return _pcall(body, name=name, grid_spec=grid_spec, out_shape=SDS((p, 4, r, cdim), grads.dtype),
                  compiler_params=_params(("parallel", "parallel", "parallel")))(c_idx, grads, recv4)


def _mm(name, ins, prods, n_acc, acc_shape, epi_idx, epilogue, out_shapes, out_specs, grid, dims, plan=None):
    n_in, n_out, nk = len(ins), len(out_shapes), grid[2]
    n_acc_refs = n_acc if nk > 1 else 0
    c_ins, c_outs = (plan.ins, plan.outs) if plan else ([], [])
    n_cin, n_cout = len(c_ins), len(c_outs)

    def body(*refs):
        in_refs, c_in = refs[:n_in], refs[n_in:n_in + n_cin]
        rest = refs[n_in + n_cin:]
        out_refs, c_out = rest[:n_out], rest[n_out:n_out + n_cout]
        rest = rest[n_out + n_cout:]
        acc_refs, sems = rest[:n_acc_refs], rest[n_acc_refs:]
        ids = [pl.program_id(axis) for axis in range(3)]
        if plan:
            @pl.when((ids[0] == 0) & (ids[1] == 0) & (ids[2] == 0))
            def _():
                plan.phases[0](c_in, c_out, *sems)

        def finish(accs):
            outs = epilogue(accs, [in_refs[i][...] for i in epi_idx])
            for o_ref, o in zip(out_refs, outs):
                if isinstance(o, tuple):
                    for plane, part in enumerate(o):
                        o_ref[plane] = part.astype(o_ref.dtype)
                else:
                    o_ref[...] = o.astype(o_ref.dtype)

        def operand(ref_idx):
            if isinstance(ref_idx, tuple):
                return in_refs[ref_idx[0]][ref_idx[1]]
            return in_refs[ref_idx][...]

        if nk == 1:
            accs = [None] * n_acc
            for ia, ib, iacc in prods:
                term = _dot(operand(ia), operand(ib), dims)
                accs[iacc] = term if accs[iacc] is None else accs[iacc] + term
            finish(accs)
        else:
            @pl.when(ids[2] == 0)
            def _():
                for acc in acc_refs:
                    acc[...] = jnp.zeros_like(acc)

            for ia, ib, iacc in prods:
                acc_refs[iacc][...] += _dot(operand(ia), operand(ib), dims)

            @pl.when(ids[2] == nk - 1)
            def _():
                finish([acc[...] for acc in acc_refs])

        if plan:
            @pl.when((ids[0] == grid[0] - 1) & (ids[1] == grid[1] - 1) & (ids[2] == nk - 1))
            def _():
                plan.phases[1](c_in, c_out, *sems)
                plan.phases[2](c_in, c_out, *sems)

    return _pcall(
        body, name=name, grid=grid, in_specs=[s for _, s in ins] + [ANY] * n_cin,
        out_specs=list(out_specs) + [ANY] * n_cout, out_shape=list(out_shapes) + list(c_outs),
        scratch_shapes=[pltpu.VMEM(acc_shape, F32) for _ in range(n_acc_refs)] + (plan.scratch() if plan else []),
        compiler_params=_params(("arbitrary",) * 3 if plan else ("parallel", "parallel", "arbitrary")),
    )(*[a for a, _ in ins], *c_ins)


def _plain(accs, _):
    return accs


def _mm_nn(name, a, b, out_dtype, extras=(), epilogue=_plain, n_out=1):
    m, kd = a.shape
    n = b.shape[1]
    tm, tn, tk = _tile(m, (1024, 512, 256, 128)), _tile(n, (640, 512, 256, 128)), _tile(kd, (1280, 1024, 512, 256, 128))
    ins = [(a, pl.BlockSpec((tm, tk), lambda i, j, k: (i, k))), (b, pl.BlockSpec((tk, tn), lambda i, j, k: (k, j)))]
    for arr, kind in extras:
        if kind == "tile":
            ins.append((arr, pl.BlockSpec((tm, tn), lambda i, j, k: (i, j))))
        else:
            ins.append((arr, pl.BlockSpec((1, tn), lambda i, j, k: (0, j))))
    dts = out_dtype if isinstance(out_dtype, (list, tuple)) else [out_dtype] * n_out
    return _mm(name, ins, [(0, 1, 0)], 1, (tm, tn), list(range(2, len(ins))), epilogue,
               [SDS((m, n), dt) for dt in dts], [pl.BlockSpec((tm, tn), lambda i, j, k: (i, j)) for _ in dts],
               (m // tm, n // tn, kd // tk), "nn")


def _mm_nt(name, a, b, out_dtype, plan=None):
    m, kd = a.shape
    n = b.shape[0]
    tm, tn, tk = _tile(m, (1024, 512, 256, 128)), _tile(n, (640, 512, 256, 128)), _tile(kd, (1024, 512, 256, 128))
    ins = [(a, pl.BlockSpec((tm, tk), lambda i, j, k: (i, k))), (b, pl.BlockSpec((tn, tk), lambda i, j, k: (j, k)))]
    outs = _mm(name, ins, [(0, 1, 0)], 1, (tm, tn), [], _plain, [SDS((m, n), out_dtype)],
               [pl.BlockSpec((tm, tn), lambda i, j, k: (i, j))], (m // tm, n // tn, kd // tk), "nt", plan=plan)
    return outs if plan else outs[0]


def _mm_tn(name, a, b, out_dtype):
    t, m = a.shape
    n = b.shape[1]
    tm, tn, tk = _tile(m, (640, 512, 256, 128)), _tile(n, (1024, 512, 256, 128)), t
    ins = [(a, pl.BlockSpec((tk, tm), lambda i, j, k: (k, i))), (b, pl.BlockSpec((tk, tn), lambda i, j, k: (k, j)))]
    return _mm(name, ins, [(0, 1, 0)], 1, (tm, tn), [], _plain, [SDS((m, n), out_dtype)],
               [pl.BlockSpec((tm, tn), lambda i, j, k: (i, j))], (m // tm, n // tn, t // tk), "tn")[0]


def _mm_nt_stack(name, a3, b, out_dtype):
    cc, m, kd = a3.shape
    n = b.shape[0]
    tm, tn, tk = _tile(m, (1024, 512, 256, 128)), _tile(n, (1024, 512, 256, 128)), _tile(kd, (1280, 1024, 512, 256, 128))
    nk = kd // tk
    ins = [(a3, pl.BlockSpec((None, tm, tk), lambda i, j, k: (k // nk, i, k % nk))),
           (b, pl.BlockSpec((tn, tk), lambda i, j, k: (j, k)))]
    return _mm(name, ins, [(0, 1, 0)], 1, (tm, tn), [], _plain, [SDS((m, n), out_dtype)],
               [pl.BlockSpec((tm, tn), lambda i, j, k: (i, j))], (m // tm, n // tn, cc * nk), "nt")[0]


def _mm_tn_stack(name, a, b3, out_dtype):
    t, m = a.shape
    cc, _, n = b3.shape
    tm, tn, tk = _tile(m, (512, 256, 128)), _tile(n, (1280, 1024, 512, 256, 128)), t
    nj = n // tn
    ins = [(a, pl.BlockSpec((tk, tm), lambda i, j, k: (k, i))),
           (b3, pl.BlockSpec((None, tk, tn), lambda i, j, k: (j // nj, k, j % nj)))]
    return _mm(name, ins, [(0, 1, 0)], 1, (tm, tn), [], _plain, [SDS((m, cc * n), out_dtype)],
               [pl.BlockSpec((tm, tn), lambda i, j, k: (i, j))], (m // tm, cc * nj, t // tk), "tn")[0]


def _chunks_to_cols(name, wc, eye2):
    nch, d, cw = wc.shape
    tm = _tile(d, (1024, 512, 256, 128))
    ins = [(wc, pl.BlockSpec((None, tm, cw), lambda i, j, k: (2 * j + k, i, 0))),
           (eye2, pl.BlockSpec((None, cw, 2 * cw), lambda i, j, k: (k, 0, 0)))]
    return _mm(name, ins, [(0, 1, 0)], 1, (tm, 2 * cw), [], _plain, [SDS((d, nch * cw), wc.dtype)],
               [pl.BlockSpec((tm, 2 * cw), lambda i, j, k: (i, j))], (d // tm, nch // 2, 2), "nn")[0]


def _cols_to_chunks(name, full, eye2):
    d, n = full.shape
    _, cw, _ = eye2.shape
    nch = n // cw
    tm = _tile(d, (1024, 512, 256, 128))
    ins = [(full, pl.BlockSpec((tm, 2 * cw), lambda i, j, k: (i, j // 2))),
           (eye2, pl.BlockSpec((None, cw, 2 * cw), lambda i, j, k: (j % 2, 0, 0)))]
    return _mm(name, ins, [(0, 1, 0)], 1, (tm, cw), [], _plain, [SDS((nch, d, cw), full.dtype)],
               [pl.BlockSpec((None, tm, cw), lambda i, j, k: (j, i, 0))], (d // tm, nch, 1), "nt")[0]


def _row_tile(t):
    return _tile(t, (256, 128, 64, 32, 16, 8))


def _norm_fwd(x, g, scale1p, shift, name):
    t, d = x.shape
    tr = _row_tile(t)

    def body(x_ref, g_ref, s_ref, b_ref, h_ref):
        xv = x_ref[...]
        inv = lax.rsqrt(jnp.mean(xv * xv, axis=-1, keepdims=True) + NORM_EPS)
        h_ref[...] = ((xv * inv) * g_ref[...] * s_ref[...] + b_ref[...]).astype(h_ref.dtype)

    vec = pl.BlockSpec((1, d), lambda i: (0, 0))
    return _pcall(body, name=name, grid=(t // tr,), in_specs=[pl.BlockSpec((tr, d), lambda i: (i, 0)), vec, vec, vec],
                  out_specs=pl.BlockSpec((tr, d), lambda i: (i, 0)), out_shape=SDS((t, d), BF16),
                  compiler_params=_params(("parallel",)))(x, g, scale1p, shift)


def _adaln_bwd(dh, x, y, dxo, g, scale1p, w_sub, gw_prev, name):
    t, d = x.shape
    tr = _row_tile(t)

    def body(dh_ref, x_ref, y_ref, dxo_ref, g_ref, s_ref, gw_ref, dx_ref, dyp_ref, sums_ref):
        i = pl.program_id(0)

        @pl.when(i == 0)
        def _():
            sums_ref[...] = jnp.zeros_like(sums_ref)

        xv, dhv, dxov = x_ref[...], dh_ref[...], dxo_ref[...]
        inv = lax.rsqrt(jnp.mean(xv * xv, axis=-1, keepdims=True) + NORM_EPS)
        xn = xv * inv
        gv = g_ref[...]
        dn = dhv * s_ref[...]
        dxn = dn * gv
        dx = inv * (dxn - xn * jnp.mean(dxn * xn, axis=-1, keepdims=True)) + dxov
        dx_ref[...] = dx
        dyp_ref[...] = (gw_ref[...] * dx).astype(dyp_ref.dtype)
        sums_ref[0:1, :] += jnp.sum(dhv, axis=0, keepdims=True)
        sums_ref[1:2, :] += jnp.sum(dhv * (xn * gv), axis=0, keepdims=True)
        sums_ref[2:3, :] += jnp.sum(w_sub * y_ref[...] * dxov, axis=0, keepdims=True)
        sums_ref[3:4, :] += jnp.sum(dn * xn, axis=0, keepdims=True)

    blk = pl.BlockSpec((tr, d), lambda i: (i, 0))
    vec = pl.BlockSpec((1, d), lambda i: (0, 0))
    return _pcall(
        body, name=name, grid=(t // tr,), in_specs=[blk, blk, blk, blk, vec, vec, vec],
        out_specs=[blk, blk, pl.BlockSpec((8, d), lambda i: (0, 0))],
        out_shape=[SDS((t, d), F32), SDS((t, d), BF16), SDS((8, d), F32)],
        compiler_params=_params(("arbitrary",)))(dh, x, y, dxo, g, scale1p, gw_prev)


def _loss_head(x, target, gf, gw_prev):
    t, d = x.shape
    tr = _row_tile(t)
    nt = t // tr

    def body(x_ref, tg_ref, g_ref, gw_ref, dx_ref, dyp_ref, sums_ref):
        i = pl.program_id(0)

        @pl.when(i == 0)
        def _():
            sums_ref[...] = jnp.zeros_like(sums_ref)

        xv = x_ref[...]
        inv = lax.rsqrt(jnp.mean(xv * xv, axis=-1, keepdims=True) + NORM_EPS)
        xn = xv * inv
        gv = g_ref[...]
        err = xn * gv - tg_ref[...]
        dyv = err * (1.0 / d)
        dxn = dyv * gv
        dx = inv * (dxn - xn * jnp.mean(dxn * xn, axis=-1, keepdims=True))
        dx_ref[...] = dx
        dyp_ref[...] = (gw_ref[...] * dx).astype(dyp_ref.dtype)
        sums_ref[0:1, :] += jnp.sum(dyv * xn, axis=0, keepdims=True)
        sums_ref[1:2, :] += jnp.sum(err * err, axis=0, keepdims=True)

        @pl.when(i == nt - 1)
        def _():
            tot = jnp.sum(sums_ref[1:2, :], axis=1, keepdims=True) * (0.5 / d)
            sums_ref[1:2, :] = jnp.broadcast_to(tot, (1, d))

    blk = pl.BlockSpec((tr, d), lambda i: (i, 0))
    vec = pl.BlockSpec((1, d), lambda i: (0, 0))
    return _pcall(
        body, name="loss_head", grid=(nt,), in_specs=[blk, blk, vec, vec],
        out_specs=[blk, blk, pl.BlockSpec((8, d), lambda i: (0, 0))],
        out_shape=[SDS((t, d), F32), SDS((t, d), BF16), SDS((8, d), F32)],
        compiler_params=_params(("arbitrary",)))(x, target, gf, gw_prev)


HIDDEN_CHUNKS = N_DEV // 2


def _ffn_tiles(t, d):
    return _tile(t, (1024, 512, 256, 128)), _tile(d, (1024, 512, 256, 128))


def _ffn_gu(name, h, wgu, plan=None):
    t, d = h.shape
    fc, nc = wgu.shape[3], HIDDEN_CHUNKS
    tm, _ = _ffn_tiles(t, d)

    def epi_gu(accs, _):
        gpre, up = accs
        return (gpre, up), gpre * _sigmoid(gpre) * up

    wblk = (None, None, d, fc)
    ins = [(h, pl.BlockSpec((tm, d), lambda i, c, k: (i, 0))),
           (wgu, pl.BlockSpec(wblk, lambda i, c, k: (0, c, 0, 0))),
           (wgu, pl.BlockSpec(wblk, lambda i, c, k: (0, c + nc, 0, 0)))]
    return _mm(name, ins, [(0, 1, 0), (0, 2, 1)], 2, (tm, fc), [], epi_gu,
               [SDS((2, nc, t, fc), BF16), SDS((nc, t, fc), BF16)],
               [pl.BlockSpec((2, None, tm, fc), lambda i, c, k: (0, c, i, 0)),
                pl.BlockSpec((None, tm, fc), lambda i, c, k: (c, i, 0))],
               (t // tm, nc, 1), "nn", plan=plan)


def _ffn_down(name, a, wd4, x, gmul, plan=None):
    nc, t, fc = a.shape
    d = wd4.shape[3]
    tm = _tile(t, (512, 256, 128))

    def epi_down(accs, ex):
        (yv,), (xv, gm) = accs, ex
        return yv, xv + MACARON_W * gm * yv

    ins = [(a, pl.BlockSpec((nc, tm, fc), lambda i, j, k: (0, i, 0))),
           (wd4, pl.BlockSpec((None, nc, fc, d), lambda i, j, k: (0, 0, 0, 0), pipeline_mode=pl.Buffered(1))),
           (x, pl.BlockSpec((tm, d), lambda i, j, k: (i, 0))), (gmul, pl.BlockSpec((1, d), lambda i, j, k: (0, 0)))]
    oblk = pl.BlockSpec((tm, d), lambda i, j, k: (i, 0))
    prods = [((0, (c,)), (1, (c,)), 0) for c in range(nc)]
    return _mm(name, ins, prods, 1, (tm, d), [2, 3], epi_down, [SDS((t, d), BF16), SDS((t, d), F32)],
               [oblk, oblk], (t // tm, 1, 1), "nn", plan=plan)


def _ffn_da(name, dy, wd4, gu2, plan=None):
    t, d = dy.shape
    _, nc, fc, _ = wd4.shape
    tm, _ = _ffn_tiles(t, d)

    def epi_da(accs, ex):
        (da,), (gu,) = accs, ex
        gpre, up = gu[0].astype(F32), gu[1].astype(F32)
        s = _sigmoid(gpre)
        silu = gpre * s
        dg = da * up * (s * (1.0 + gpre * (1.0 - s)))
        return ((dg, da * silu),)

    gblk = pl.BlockSpec((2, None, tm, fc), lambda i, c, k: (0, c, i, 0))
    ins = [(dy, pl.BlockSpec((tm, d), lambda i, c, k: (i, 0))),
           (wd4, pl.BlockSpec((None, None, fc, d), lambda i, c, k: (0, c, 0, 0))), (gu2, gblk)]
    return _mm(name, ins, [(0, 1, 0)], 1, (tm, fc), [2], epi_da, [SDS((2, nc, t, fc), BF16)], [gblk],
               (t // tm, nc, 1), "nt", plan=plan)


def _ffn_dwd(name, a, dy, plan=None):
    nc, t, fc = a.shape
    d = dy.shape[1]
    _, tn = _ffn_tiles(t, d)
    ins = [(a, pl.BlockSpec((None, t, fc), lambda c, j, k: (c, 0, 0))), (dy, pl.BlockSpec((t, tn), lambda c, j, k: (0, j)))]
    return _mm(name, ins, [(0, 1, 0)], 1, (fc, tn), [], _plain, [SDS((1, nc, fc, d), BF16)],
               [pl.BlockSpec((None, None, fc, tn), lambda c, j, k: (0, c, 0, j))], (nc, d // tn, 1), "tn", plan=plan)


def _ffn_dwgu(name, h, dgu2, plan=None):
    t, d = h.shape
    _, nc, _, fc = dgu2.shape
    _, tn = _ffn_tiles(t, d)
    ins = [(h, pl.BlockSpec((t, tn), lambda i, c, k: (0, i))),
           (dgu2, pl.BlockSpec((None, None, t, fc), lambda i, c, k: (c // nc, c % nc, 0, 0)))]
    return _mm(name, ins, [(0, 1, 0)], 1, (tn, fc), [], _plain, [SDS((1, 2 * nc, d, fc), BF16)],
               [pl.BlockSpec((None, None, tn, fc), lambda i, c, k: (0, c, i, 0))], (d // tn, 2 * nc, 1), "tn", plan=plan)


def _ffn_dh(name, dgu2, wgu, plan=None):
    _, nc, t, fc = dgu2.shape
    d = wgu.shape[2]
    tm = _tile(t, (512, 256, 128))
    ins = [(dgu2, pl.BlockSpec((2, nc, tm, fc), lambda i, j, k: (0, 0, i, 0))),
           (wgu, pl.BlockSpec((None, 2 * nc, d, fc), lambda i, j, k: (0, 0, 0, 0), pipeline_mode=pl.Buffered(1)))]
    prods = [((0, (s, c)), (1, (nc * s + c,)), 0) for s in range(2) for c in range(nc)]
    return _mm(name, ins, prods, 1, (tm, d), [], _plain, [SDS((t, d), F32)],
               [pl.BlockSpec((tm, d), lambda i, j, k: (i, 0))], (t // tm, 1, 1), "nt", plan=plan)


def _sb_block(t):
    return 256 if t >= 1024 else 128


SB_STRIP = 64


def _sb_strips(blk):
    strip = min(SB_STRIP, blk)
    row = lax.broadcasted_iota(jnp.int32, (strip, blk), 0)
    col = lax.broadcasted_iota(jnp.int32, (strip, blk), 1)
    return [(slice(r0, r0 + strip), col < row + r0) for r0 in range(0, blk, strip)]


def _host_call(core, name, steps, ins, in_specs, out_shapes, out_specs, scratch, plan):
    n_in, n_out, n_scr = len(ins), len(out_shapes), len(scratch)
    c_ins, c_outs = (plan.ins, plan.outs) if plan else ([], [])
    n_cin, n_cout = len(c_ins), len(c_outs)

    def body(*refs):
        in_refs, c_in = refs[:n_in], refs[n_in:n_in + n_cin]
        rest = refs[n_in + n_cin:]
        out_refs, c_out = rest[:n_out], rest[n_out:n_out + n_cout]
        rest = rest[n_out + n_cout:]
        scr, sems = rest[:n_scr], rest[n_scr:]
        step = pl.program_id(0)
        if plan:
            @pl.when(step == 0)
            def _():
                plan.phases[0](c_in, c_out, *sems)

        core(in_refs, out_refs, scr)
        if plan:
            @pl.when(step == steps - 1)
            def _():
                plan.phases[1](c_in, c_out, *sems)
                plan.phases[2](c_in, c_out, *sems)

    return _pcall(
        body, name=name, grid=(steps,), in_specs=list(in_specs) + [ANY] * n_cin,
        out_specs=list(out_specs) + [ANY] * n_cout, out_shape=list(out_shapes) + list(c_outs),
        scratch_shapes=list(scratch) + (plan.scratch() if plan else []),
        compiler_params=_params(("arbitrary",)))(*ins, *c_ins)


def _sb_fwd(qkv, d, plan=None):
    t = qkv.shape[0]
    blk = _sb_block(t)
    nq = t // blk
    npair = d // LANES
    scale = HEAD_DIM ** -0.5

    def body(in_refs, out_refs, scr):
        (q_ref, k_ref, v_ref), (o_ref, l_ref) = in_refs, out_refs
        tri_s = scr[0]
        hi_s, lo_s, w_s, zs_s = (scr[1 + 4 * k:5 + 4 * k] for k in range(4))
        lane = lax.broadcasted_iota(jnp.int32, (blk, LANES), 1)
        head0 = lane < HEAD_DIM
        row = lax.broadcasted_iota(jnp.int32, (blk, blk), 0)
        col = lax.broadcasted_iota(jnp.int32, (blk, blk), 1)
        tri_s[...] = (row > col).astype(BF16)
        strips = _sb_strips(blk)

        def step(qhs, kbs, maskeds, carries):
            chains = [(bi, hh) for bi in range(len(kbs)) for hh in range(2)]
            starts = [pl.multiple_of(kb * blk, blk) for kb in kbs]
            kvs = [k_ref[pl.ds(start, blk), :] for start in starts]
            vvs = [v_ref[pl.ds(start, blk), :] for start in starts]
            zs = [_dot(qhs[hh], kvs[bi], "nt") for bi, hh in chains]
            sums = []
            for c, (bi, hh) in enumerate(chains):
                parts = []
                for rows, causal in strips:
                    zt = zs[c][rows, :]
                    sp = _softplus(zt)
                    lk = jnp.where(causal, -sp, 0.0) if maskeds[bi] else -sp
                    hi = lk.astype(BF16)
                    hi_s[c][rows, :] = hi
                    lo_s[c][rows, :] = (lk - hi.astype(F32)).astype(BF16)
                    zs_s[c][rows, :] = zt - sp
                    parts.append(jnp.sum(lk, axis=1, keepdims=True))
                sums.append(jnp.concatenate(parts, axis=0))
            laters = [_dot(hi_s[c][...], tri_s[...], "nn") + _dot(lo_s[c][...], tri_s[...], "nn") for c in range(len(chains))]
            for c, (bi, hh) in enumerate(chains):
                cl = carries[hh][0]
                if bi == 1:
                    cl = cl + sums[hh]
                for rows, causal in strips:
                    logw = zs_s[c][rows, :] + laters[c][rows, :] + cl[rows, :]
                    if maskeds[bi]:
                        logw = jnp.where(causal, logw, -1e30)
                    w_s[c][rows, :] = jnp.exp(logw).astype(BF16)
            new = [list(carries[hh]) for hh in range(2)]
            for c, (bi, hh) in enumerate(chains):
                new[hh] = [new[hh][0] + sums[c], new[hh][1] + _dot(w_s[c][...], vvs[bi], "nn")]
            return tuple(tuple(cr) for cr in new)

        def qblock(qi, _):
            qstart = pl.multiple_of(qi * blk, blk)
            qv = q_ref[pl.ds(qstart, blk), :] * scale
            qhs = [jnp.where(head0 if hh == 0 else ~head0, qv, jnp.zeros_like(qv)) for hh in range(2)]
            zero = (jnp.zeros((blk, 1), F32), jnp.zeros((blk, LANES), F32))
            outs = lax.cond(qi % 2 == 1,
                            lambda crs: step(qhs, [qi, qi - 1], [True, False], crs),
                            lambda crs: step(qhs, [qi], [True], crs), (zero, zero))
            top = qi - 1 - qi % 2
            outs = lax.fori_loop(0, qi // 2, lambda j, crs: step(qhs, [top - 2 * j, top - 2 * j - 1], [False, False], crs),
                                 outs)
            o_ref[pl.ds(qstart, blk), :] = jnp.where(head0, outs[0][1], outs[1][1]).astype(o_ref.dtype)
            l_ref[pl.ds(qstart, blk), :] = jnp.where(head0, outs[0][0], outs[1][0])
            return 0

        lax.fori_loop(0, nq, qblock, 0)

    tile_bf16, tile_f32 = pltpu.VMEM((blk, blk), BF16), pltpu.VMEM((blk, blk), F32)
    return _host_call(
        body, "sb_fwd", npair, [qkv, qkv, qkv],
        [pl.BlockSpec((t, LANES), lambda p: (0, p)), pl.BlockSpec((t, LANES), lambda p: (0, npair + p)),
         pl.BlockSpec((t, LANES), lambda p: (0, 2 * npair + p))],
        [SDS((t, d), BF16), SDS((t, d), F32)],
        [pl.BlockSpec((t, LANES), lambda p: (0, p)), pl.BlockSpec((t, LANES), lambda p: (0, p))],
        [tile_bf16] * 13 + [tile_f32] * 4, plan)


def _sb_bwd(qkv, do, ltot, d, plan=None):
    t = qkv.shape[0]
    blk = _sb_block(t)
    nq = t // blk
    npair = d // LANES
    scale = HEAD_DIM ** -0.5

    def body(in_refs, out_refs, scr):
        (q_ref, k_ref, v_ref, do_ref, l_ref), (out_ref,) = in_refs, out_refs
        dq_s, dk_s, dv_s, upto_s, before_s = scr[:5]
        hi_s, lo_s, w_s, dab_s, dzs_s, zs_s, da_s = (scr[5 + 4 * k:9 + 4 * k] for k in range(7))
        lane = lax.broadcasted_iota(jnp.int32, (blk, LANES), 1)
        head0 = lane < HEAD_DIM
        row = lax.broadcasted_iota(jnp.int32, (blk, blk), 0)
        col = lax.broadcasted_iota(jnp.int32, (blk, blk), 1)
        upto_s[...] = (row <= col).astype(BF16)
        before_s[...] = (row < col).astype(BF16)
        dk_s[...] = jnp.zeros_like(dk_s)
        dv_s[...] = jnp.zeros_like(dv_s)
        strips = _sb_strips(blk)

        def step(heads, kbs, maskeds, carries):
            chains = [(bi, hh) for bi in range(len(kbs)) for hh in range(2)]
            starts = [pl.multiple_of(kb * blk, blk) for kb in kbs]
            kvs = [k_ref[pl.ds(start, blk), :] for start in starts]
            vvs = [v_ref[pl.ds(start, blk), :] for start in starts]
            zs = [_dot(heads[hh][0], kvs[bi], "nt") for bi, hh in chains]
            dws = [_dot(heads[hh][1], vvs[bi], "nt") for bi, hh in chains]
            lk_sums, da_sums = [], []
            for c, (bi, hh) in enumerate(chains):
                parts = []
                for rows, causal in strips:
                    zt = zs[c][rows, :]
                    sp = _softplus(zt)
                    lk = jnp.where(causal, -sp, 0.0) if maskeds[bi] else -sp
                    hi = lk.astype(BF16)
                    hi_s[c][rows, :] = hi
                    lo_s[c][rows, :] = (lk - hi.astype(F32)).astype(BF16)
                    zs_s[c][rows, :] = zt - sp
                    parts.append(jnp.sum(lk, axis=1, keepdims=True))
                lk_sums.append(jnp.concatenate(parts, axis=0))
            cums = [_dot(hi_s[c][...], upto_s[...], "nn") + _dot(lo_s[c][...], upto_s[...], "nn") for c in range(len(chains))]
            for c, (bi, hh) in enumerate(chains):
                lt, plk = heads[hh][2], carries[hh][0]
                if bi == 1:
                    plk = plk + lk_sums[hh]
                parts = []
                for rows, causal in strips:
                    logw = zs_s[c][rows, :] + (lt[rows, :] - (plk[rows, :] + cums[c][rows, :]))
                    if maskeds[bi]:
                        logw = jnp.where(causal, logw, -1e30)
                    w = jnp.exp(logw)
                    w_s[c][rows, :] = w.astype(BF16)
                    da = dws[c][rows, :] * w
                    da_s[c][rows, :] = da
                    dab_s[c][rows, :] = da.astype(BF16)
                    parts.append(jnp.sum(da, axis=1, keepdims=True))
                da_sums.append(jnp.concatenate(parts, axis=0))
            pres = [_dot(dab_s[c][...], before_s[...], "nn") for c in range(len(chains))]
            for c, (bi, hh) in enumerate(chains):
                pda = carries[hh][1]
                if bi == 1:
                    pda = pda + da_sums[hh]
                for rows, causal in strips:
                    sig = jnp.exp(zs_s[c][rows, :])
                    da = da_s[c][rows, :]
                    dz = da * (1.0 - sig) - sig * (pda[rows, :] + pres[c][rows, :])
                    if maskeds[bi]:
                        dz = jnp.where(causal, dz, 0.0)
                    dzs_s[c][rows, :] = dz.astype(BF16)
            new = [list(carries[hh]) for hh in range(2)]
            for c, (bi, hh) in enumerate(chains):
                dk_s[kbs[bi]] += _dot(heads[hh][3], dzs_s[c][...], "nn")
                dv_s[kbs[bi]] += _dot(heads[hh][4], w_s[c][...], "nn")
                new[hh] = [new[hh][0] + lk_sums[c], new[hh][1] + da_sums[c], new[hh][2] + _dot(dzs_s[c][...], kvs[bi], "nn")]
            return tuple(tuple(cr) for cr in new)

        def qblock(qi, _):
            qstart = pl.multiple_of(qi * blk, blk)
            qv = q_ref[pl.ds(qstart, blk), :] * scale
            dov = do_ref[pl.ds(qstart, blk), :]
            lv = l_ref[pl.ds(qstart, blk), :]
            heads = []
            for hh in range(2):
                sel = head0 if hh == 0 else ~head0
                qh, doh = jnp.where(sel, qv, jnp.zeros_like(qv)), jnp.where(sel, dov, jnp.zeros_like(dov))
                heads.append((qh, doh, jnp.max(jnp.where(sel, lv, -jnp.inf), axis=1, keepdims=True),
                              qh.astype(F32).T.astype(BF16), doh.astype(F32).T.astype(BF16)))
            zero = (jnp.zeros((blk, 1), F32), jnp.zeros((blk, 1), F32), jnp.zeros((blk, LANES), F32))
            carries = lax.fori_loop(0, qi // 2, lambda j, crs: step(heads, [2 * j, 2 * j + 1], [False, False], crs),
                                    (zero, zero))
            carries = lax.cond(qi % 2 == 1,
                               lambda crs: step(heads, [qi - 1, qi], [False, True], crs),
                               lambda crs: step(heads, [qi], [True], crs), carries)
            dq_s[pl.ds(qstart, blk), :] = jnp.where(head0, carries[0][2], carries[1][2]) * scale
            return 0

        lax.fori_loop(0, nq, qblock, 0)
        out_ref[0] = dq_s[...].astype(out_ref.dtype)
        for b in range(nq):
            out_ref[1, b * blk:(b + 1) * blk, :] = dk_s[b].T.astype(out_ref.dtype)
            out_ref[2, b * blk:(b + 1) * blk, :] = dv_s[b].T.astype(out_ref.dtype)

    col_blk = lambda off: pl.BlockSpec((t, LANES), lambda p: (0, off + p))
    return _host_call(
        body, "sb_bwd", npair, [qkv, qkv, qkv, do, ltot],
        [col_blk(0), col_blk(npair), col_blk(2 * npair), col_blk(0), col_blk(0)],
        [SDS((3, t, d), BF16)], [pl.BlockSpec((3, t, LANES), lambda p: (0, 0, p))],
        [pltpu.VMEM((t, LANES), F32)] + [pltpu.VMEM((nq, LANES, blk), F32) for _ in range(2)]
        + [pltpu.VMEM((blk, blk), BF16) for _ in range(2 + 20)]
        + [pltpu.VMEM((blk, blk), F32) for _ in range(8)], plan)


def _roll_rows(v, shift):
    return pltpu.roll(v, shift, 0)


def _shift_down(v, dist, fill, row):
    return jnp.where(row >= dist, _roll_rows(v, dist), fill)


def _shift_up(v, dist, fill, row):
    t = v.shape[0]
    return jnp.where(row < t - dist, _roll_rows(v, t - dist), fill)


def _lru_gates(xb, small, wr, wi, row):
    xs = [_shift_down(xb, 3 - tap, 0.0, row) if tap < 3 else xb for tap in range(4)]
    xc = small[4:5, :] + xs[0] * small[0:1, :]
    for tap in range(1, 4):
        xc = xc + xs[tap] * small[tap:tap + 1, :]
    xcb = xc.astype(BF16)
    r = _sigmoid(_dot(xcb, wr, "nn") + small[5:6, :])
    ig = _sigmoid(_dot(xcb, wi, "nn") + small[6:7, :])
    sp = _softplus(-small[7:8, :])
    la = -LRU_C * r * sp
    a = jnp.exp(la)
    th = jnp.tanh(la)
    m2 = -2.0 * th / (1.0 - th)
    return xs, xc, xcb, r, ig, sp, a, (jnp.sqrt(m2), m2)


def _gelu_parts(gate):
    inner = GELU_C * (gate + GELU_K * gate * gate * gate)
    th = jnp.tanh(inner)
    gelu = 0.5 * gate * (1.0 + th)
    dgelu = 0.5 * (1.0 + th) + 0.5 * gate * (1.0 - th * th) * GELU_C * (1.0 + 3.0 * GELU_K * gate * gate)
    return gelu, dgelu


def _scan_steps(t):
    steps, dist = [], 1
    while dist < t:
        steps.append(dist)
        dist *= 2
    return steps


SUBLANES = 8


def _linear_scan(a, b, scratch, row, reverse):
    a_s, b_s, carry_s = scratch
    t = a.shape[0]
    groups = t // SUBLANES
    in_group = row & (SUBLANES - 1)
    for dist in _scan_steps(SUBLANES):
        if reverse:
            inside = in_group < SUBLANES - dist
            b = b + a * jnp.where(inside, _roll_rows(b, t - dist), 0.0)
            a = a * jnp.where(inside, _roll_rows(a, t - dist), 1.0)
        else:
            inside = in_group >= dist
            b = a * jnp.where(inside, _roll_rows(b, dist), 0.0) + b
            a = a * jnp.where(inside, _roll_rows(a, dist), 1.0)
    a_s[...] = a
    b_s[...] = b
    end = 0 if reverse else SUBLANES - 1
    ends = pl.ds(end, groups, stride=SUBLANES)
    ae, be = a_s[ends, :], b_s[ends, :]
    grow = lax.broadcasted_iota(jnp.int32, ae.shape, 0)
    shift = _shift_up if reverse else _shift_down
    for dist in _scan_steps(groups):
        be = ae * shift(be, dist, 0.0, grow) + be
        ae = ae * shift(ae, dist, 1.0, grow)
    incoming = shift(be, 1, 0.0, grow)
    for k in range(SUBLANES):
        carry_s[pl.ds(k, groups, stride=SUBLANES), :] = incoming
    return a_s[...] * carry_s[...] + b_s[...]


def _lru_fwd(gx, small, wr, wi):
    t = gx.shape[0]
    r_dim = gx.shape[1] // 2
    nb = r_dim // LRU_BLOCK_W

    def body(gate_ref, xb_ref, small_ref, wr_ref, wi_ref, y_ref, hs_ref, *scratch):
        row = lax.broadcasted_iota(jnp.int32, (t, LRU_BLOCK_W), 0)
        xb = xb_ref[...]
        _, xc, _, _, ig, _, a, (mult, _) = _lru_gates(xb, small_ref, wr_ref[...], wi_ref[...], row)
        hsv = _linear_scan(a, mult * (ig * xc), scratch, row, reverse=False)
        hs_ref[...] = hsv
        gelu, _ = _gelu_parts(gate_ref[...])
        y_ref[...] = (gelu * hsv).astype(y_ref.dtype)

    colb = lambda off: pl.BlockSpec((t, LRU_BLOCK_W), lambda n: (0, off + n))
    wspec = pl.BlockSpec((None, LRU_BLOCK_W, LRU_BLOCK_W), lambda n: (n, 0, 0))
    return _pcall(
        body, name="lru_fwd", grid=(nb,),
        in_specs=[colb(0), colb(nb), pl.BlockSpec((8, LRU_BLOCK_W), lambda n: (0, n)), wspec, wspec],
        out_specs=[colb(0), colb(0)], out_shape=[SDS((t, r_dim), BF16), SDS((t, r_dim), F32)],
        scratch_shapes=[pltpu.VMEM((t, LRU_BLOCK_W), F32) for _ in range(3)],
        compiler_params=_params(("parallel",)))(gx, gx, small, wr, wi)


def _lru_bwd(gx, hs, dy, small, wr, wi, plan=None):
    t = gx.shape[0]
    r_dim = gx.shape[1] // 2
    nb = r_dim // LRU_BLOCK_W

    def body(in_refs, out_refs, scratch):
        (gate_ref, xb_ref, hs_ref, dy_ref, small_ref, wr_ref, wi_ref), (dgx_ref, dsm_ref, dwr_ref, dwi_ref) = in_refs, out_refs
        row = lax.broadcasted_iota(jnp.int32, (t, LRU_BLOCK_W), 0)
        xb, hsv, dyv, smallv = xb_ref[...], hs_ref[...], dy_ref[...], small_ref
        wrv, wiv = wr_ref[...], wi_ref[...]
        xs, xc, xcb, r, ig, sp, a, (mult, m2) = _lru_gates(xb, smallv, wrv, wiv, row)
        gelu, dgelu = _gelu_parts(gate_ref[...])
        dgx_ref[0] = (dyv * hsv * dgelu).astype(dgx_ref.dtype)
        dacc = _linear_scan(_shift_up(a, 1, 1.0, row), dyv * gelu, scratch, row, reverse=True)
        da = dacc * _shift_down(hsv, 1, 0.0, row)
        dmult = dacc * (ig * xc)
        dixc = dacc * mult
        dla = da * a - dmult * (a * a) * lax.rsqrt(m2)
        dr = dla * (-LRU_C * sp)
        dsp = jnp.sum(dla * (-LRU_C * r), axis=0, keepdims=True)
        dpr = dr * r * (1.0 - r)
        dpi = dixc * xc * ig * (1.0 - ig)
        dprb, dpib = dpr.astype(BF16), dpi.astype(BF16)
        dwr_ref[...] = _dot(xcb, dprb, "tn")
        dwi_ref[...] = _dot(xcb, dpib, "tn")
        dxc = dixc * ig + _dot(dprb, wrv, "nt") + _dot(dpib, wiv, "nt")
        dxb = dxc * smallv[3:4, :]
        for tap in range(3):
            dxb = dxb + _shift_up(dxc, 3 - tap, 0.0, row) * smallv[tap:tap + 1, :]
        dgx_ref[1] = dxb.astype(dgx_ref.dtype)
        lam = smallv[7:8, :]
        rows = [jnp.sum(dxc * xs[tap], axis=0, keepdims=True) for tap in range(4)]
        rows.append(jnp.sum(dxc, axis=0, keepdims=True))
        rows.append(jnp.sum(dpr, axis=0, keepdims=True))
        rows.append(jnp.sum(dpi, axis=0, keepdims=True))
        rows.append(-dsp * _sigmoid(-lam))
        for k, rv in enumerate(rows):
            dsm_ref[k:k + 1, :] = rv

    colb = lambda off: pl.BlockSpec((t, LRU_BLOCK_W), lambda n: (0, off + n))
    wspec = pl.BlockSpec((None, LRU_BLOCK_W, LRU_BLOCK_W), lambda n: (n, 0, 0))
    sspec = pl.BlockSpec((8, LRU_BLOCK_W), lambda n: (0, n))
    return _host_call(
        body, "lru_bwd", nb, [gx, gx, hs, dy, small, wr, wi],
        [colb(0), colb(nb), colb(0), colb(0), sspec, wspec, wspec],
        [SDS((2, t, r_dim), BF16), SDS((8, r_dim), F32), SDS((nb, LRU_BLOCK_W, LRU_BLOCK_W), F32),
         SDS((nb, LRU_BLOCK_W, LRU_BLOCK_W), F32)],
        [pl.BlockSpec((2, t, LRU_BLOCK_W), lambda n: (0, 0, n)), sspec, wspec, wspec],
        [pltpu.VMEM((t, LRU_BLOCK_W), F32) for _ in range(3)], plan)


def _adam(w, g, m, v):
    m2 = ADAM_B1 * m + (1.0 - ADAM_B1) * g
    v2 = ADAM_B2 * v + (1.0 - ADAM_B2) * (g * g)
    m_hat = m2 / (1.0 - ADAM_B1 ** ADAM_STEP)
    v_hat = v2 / (1.0 - ADAM_B2 ** ADAM_STEP)
    return -ADAM_LR * (m_hat / (jnp.sqrt(v_hat) + ADAM_EPS) + ADAM_WD * w), m2, v2


def _mod_fwd(c_all, mod_w, mod_b_cols):
    nl, d, cols = mod_w.shape
    nbatch = c_all.shape[0]

    def body(c_ref, w_ref, b_ref, o_ref):
        cv = c_ref[...]
        ca = (cv * _sigmoid(cv)).astype(BF16)
        o_ref[...] = _dot(ca, w_ref[...].astype(BF16), "nn") + b_ref[...]

    return _pcall(
        body, name="mod_fwd", grid=(nl,),
        in_specs=[pl.BlockSpec((nbatch, d), lambda l: (0, 0)), pl.BlockSpec((None, d, cols), lambda l: (l, 0, 0)),
                  pl.BlockSpec((None, 1, cols), lambda l: (l, 0, 0))],
        out_specs=pl.BlockSpec((None, nbatch, cols), lambda l: (l, 0, 0)), out_shape=SDS((nl, nbatch, cols), F32),
        compiler_params=_params(("parallel",)))(c_all, mod_w, mod_b_cols)


def _mod_w_update(c_all, dmod_cols, w, m, v):
    nl, d, cols = w.shape
    nbatch = c_all.shape[0]
    tr = _tile(d, (256, 128))

    def body(c_ref, dm_ref, w_ref, m_ref, v_ref, g_ref, dl_ref, m2_ref, v2_ref):
        cv = c_ref[...]
        ca = (cv * _sigmoid(cv)).astype(BF16)
        g = _dot(ca, dm_ref[...].astype(BF16), "tn")
        g_ref[...] = g
        dl_ref[...], m2_ref[...], v2_ref[...] = _adam(w_ref[...], g, m_ref[...], v_ref[...])

    wblk = pl.BlockSpec((None, tr, cols), lambda l, i: (l, i, 0))
    return _pcall(
        body, name="mod_w_update", grid=(nl, d // tr),
        in_specs=[pl.BlockSpec((nbatch, tr), lambda l, i: (0, i)), pl.BlockSpec((None, nbatch, cols), lambda l, i: (l, 0, 0)),
                  wblk, wblk, wblk],
        out_specs=[wblk] * 4, out_shape=[SDS(w.shape, F32)] * 4,
        compiler_params=_params(("parallel", "parallel")))(c_all, dmod_cols, w, m, v)


def _adam_update(name, w, m, v, gparts):
    rows, cols = w.shape
    tr = _tile(rows, (256, 128, 64, 32, 16, 8))
    npart = len(gparts)

    def body(*refs):
        w_ref, m_ref, v_ref = refs[:3]
        g_refs = refs[3:3 + npart]
        g_ref, dl_ref, m2_ref, v2_ref = refs[3 + npart:]
        g = g_refs[0][...].astype(F32)
        for gr in g_refs[1:]:
            g = g + gr[...].astype(F32)
        g_ref[...] = g
        dl_ref[...], m2_ref[...], v2_ref[...] = _adam(w_ref[...], g, m_ref[...], v_ref[...])

    blk = pl.BlockSpec((tr, cols), lambda i: (i, 0))
    return _pcall(body, name=name, grid=(rows // tr,), in_specs=[blk] * (3 + npart), out_specs=[blk] * 4,
                  out_shape=[SDS((rows, cols), F32)] * 4, compiler_params=_params(("parallel",)))(w, m, v, *gparts)


def _adam_shard(name, w, m, v, part4, recv3, chip_idx, first=0, fills=None):
    p, r, cdim = w.shape
    pg = part4.shape[0]
    tr = _tile(r, (256, 176, 160, 128, 64, 32, 16))

    def body(chip_ref, w_ref, m_ref, v_ref, own_ref, r0_ref, r1_ref, r2_ref, *rest):
        g_ref, dl_ref, m2_ref, v2_ref = rest[-4:]
        g = own_ref[...].astype(F32) + r0_ref[...].astype(F32) + r1_ref[...].astype(F32) + r2_ref[...].astype(F32)
        g_ref[...] = g
        dl_ref[...], m2_ref[...], v2_ref[...] = _adam(w_ref[...], g, m_ref[...], v_ref[...])

    blk = pl.BlockSpec((None, tr, cdim), lambda q, i, chip_ref: (first + q, i, 0))
    blk4 = (None, None, tr, cdim)
    slot = lambda s: pl.BlockSpec(blk4, lambda q, i, chip_ref: (s, q, i, 0))
    fills = list(fills or [])
    grid_spec = pltpu.PrefetchScalarGridSpec(
        num_scalar_prefetch=1, grid=(pg, r // tr),
        in_specs=[blk, blk, blk, pl.BlockSpec(blk4, lambda q, i, chip_ref: (q, chip_ref[0], i, 0)), slot(0), slot(1), slot(2)]
        + [ANY] * len(fills),
        out_specs=[blk] * 4)
    return _pcall(body, name=name, grid_spec=grid_spec, out_shape=[SDS((p, r, cdim), F32)] * 4,
                  input_output_aliases={8 + k: k for k in range(len(fills))},
                  compiler_params=_params(("parallel", "parallel")))(chip_idx, w, m, v, part4, recv3, recv3, recv3, *fills)


def _sum_devices(gathered, name):
    _, rows, cols = gathered.shape
    tr = _tile(rows, (512, 256, 128, 64, 32, 16, 8))

    def body(g_ref, o_ref):
        acc = g_ref[0].astype(F32)
        for k in range(1, N_DEV):
            acc = acc + g_ref[k].astype(F32)
        o_ref[...] = acc

    return _pcall(body, name=name, grid=(rows // tr,), in_specs=[pl.BlockSpec((N_DEV, tr, cols), lambda i: (0, i, 0))],
                  out_specs=pl.BlockSpec((tr, cols), lambda i: (i, 0)), out_shape=SDS((rows, cols), F32),
                  compiler_params=_params(("parallel",)))(gathered)


def _pack_flat(parts, width, row_mult, dtype):
    flat = jnp.concatenate([p.reshape(-1).astype(dtype) for p in parts])
    unit = width * row_mult
    pad = (-flat.shape[0]) % unit
    if pad:
        flat = jnp.concatenate([flat, jnp.zeros((pad,), dtype)])
    return flat.reshape(-1, width)


def _unpack_flat(flat, shapes):
    out, off = [], 0
    for shp in shapes:
        size = math.prod(shp)
        out.append(flat[off:off + size].reshape(shp))
        off += size
    return out


def kernel(x, c, mod_w, mod_b, norm_g, ffn_w_gu, ffn_w_down, sb_w_qkv, sb_w_o, lru_w_in, lru_conv_w, lru_conv_b, lru_w_r, lru_b_r, lru_w_i, lru_b_i, lru_lambda, lru_w_out, final_norm_g, loss_target, m_mod_w, m_mod_b, m_norm_g, m_ffn_w_gu, m_ffn_w_down, m_sb_w_qkv, m_sb_w_o, m_lru_w_in, m_lru_conv_w, m_lru_conv_b, m_lru_w_r, m_lru_b_r, m_lru_w_i, m_lru_b_i, m_lru_lambda, m_lru_w_out, m_final_norm_g, v_mod_w, v_mod_b, v_norm_g, v_ffn_w_gu, v_ffn_w_down, v_sb_w_qkv, v_sb_w_o, v_lru_w_in, v_lru_conv_w, v_lru_conv_b, v_lru_w_r, v_lru_b_r, v_lru_w_i, v_lru_b_i, v_lru_lambda, v_lru_w_out, v_final_norm_g):
    weights = dict(mod_w=mod_w, mod_b=mod_b, norm_g=norm_g, ffn_w_gu=ffn_w_gu, ffn_w_down=ffn_w_down, sb_w_qkv=sb_w_qkv,
                   sb_w_o=sb_w_o, lru_w_in=lru_w_in, lru_conv_w=lru_conv_w, lru_conv_b=lru_conv_b, lru_w_r=lru_w_r,
                   lru_b_r=lru_b_r, lru_w_i=lru_w_i, lru_b_i=lru_b_i, lru_lambda=lru_lambda, lru_w_out=lru_w_out,
                   final_norm_g=final_norm_g)
    mom_m = dict(mod_w=m_mod_w, mod_b=m_mod_b, norm_g=m_norm_g, ffn_w_gu=m_ffn_w_gu, ffn_w_down=m_ffn_w_down,
                 sb_w_qkv=m_sb_w_qkv, sb_w_o=m_sb_w_o, lru_w_in=m_lru_w_in, lru_conv_w=m_lru_conv_w,
                 lru_conv_b=m_lru_conv_b, lru_w_r=m_lru_w_r, lru_b_r=m_lru_b_r, lru_w_i=m_lru_w_i, lru_b_i=m_lru_b_i,
                 lru_lambda=m_lru_lambda, lru_w_out=m_lru_w_out, final_norm_g=m_final_norm_g)
    mom_v = dict(mod_w=v_mod_w, mod_b=v_mod_b, norm_g=v_norm_g, ffn_w_gu=v_ffn_w_gu, ffn_w_down=v_ffn_w_down,
                 sb_w_qkv=v_sb_w_qkv, sb_w_o=v_sb_w_o, lru_w_in=v_lru_w_in, lru_conv_w=v_lru_conv_w,
                 lru_conv_b=v_lru_conv_b, lru_w_r=v_lru_w_r, lru_b_r=v_lru_b_r, lru_w_i=v_lru_w_i, lru_b_i=v_lru_b_i,
                 lru_lambda=v_lru_lambda, lru_w_out=v_lru_w_out, final_norm_g=v_final_norm_g)
    names = list(weights)

    t, d = x.shape[1], x.shape[2]
    n_layers = mod_w.shape[0]
    r_dim = lru_w_out.shape[1] * N_DEV
    ng, rs = d // N_DEV, r_dim // N_DEV
    mod_cols = mod_w.shape[2]
    nblk = lru_w_r.shape[1]
    xi, yi, ci = _mesh_pos()
    me = 4 * xi + 2 * yi + ci
    chip = 2 * xi + yi
    x2, target = x.reshape(t, d), loss_target.reshape(t, d)

    lru_small_shard = jnp.concatenate([lru_conv_w[0], lru_conv_b, lru_b_r, lru_b_i, lru_lambda], axis=0)
    small1 = _pack_flat([c, norm_g, lru_small_shard], LANES, 8, F32)
    n_small1 = small1.shape[0]
    all1 = _allgather(small1[None], "gather_small").reshape(N_DEV, n_small1 * LANES)
    c_all = all1[:, :d]
    norm_full = jnp.transpose(all1[:, d:d + 6 * ng].reshape(N_DEV, n_layers, 3, ng), (1, 2, 0, 3)).reshape(n_layers, 3, d)
    lru_small = jnp.transpose(all1[:, d + 6 * ng:d + 6 * ng + 8 * rs].reshape(N_DEV, 8, rs), (1, 0, 2)).reshape(8, r_dim)

    mod_b_cols = lax.dynamic_slice_in_dim(mod_b, me * mod_cols, mod_cols, axis=1).reshape(n_layers, 1, mod_cols)
    mod_part = _mod_fwd(c_all, mod_w, mod_b_cols)

    assert sb_w_qkv.shape[0] == 1 and lru_w_in.shape[0] == 1, "one stick-breaking and one RG-LRU layer"
    n_ffn = 2 * n_layers
    fc = ffn_w_gu.shape[3]
    cw_in = lru_w_in.shape[2]
    pieces = {("ffn_w_gu", q): ffn_w_gu[q // 2, q % 2][None] for q in range(n_ffn)}
    pieces.update({("ffn_w_down", q): ffn_w_down[q // 2, q % 2][None] for q in range(n_ffn)})
    pieces.update({("sb_w_qkv", 0): sb_w_qkv, ("sb_w_o", 0): sb_w_o, ("lru_w_in", 0): lru_w_in, ("lru_w_out", 0): lru_w_out})
    col_window = {("sb_w_qkv", 0)}
    first = [("ffn_w_gu", 0)]
    behind = {"l0s0_gu": [("ffn_w_down", 0)], "l0s0_down": [("sb_w_qkv", 0), ("sb_w_o", 0)],
              "l0s2_gu": [("ffn_w_down", n_ffn - 1)], "l0s2_down": [("ffn_w_down", 2)],
              "l1s0_gu": [("lru_w_in", 0)], "l1s0_down": [("lru_w_out", 0)]}
    behind["sb_fwd"] = [key for key in pieces if key not in first + sum(behind.values(), [])]
    gathered = {}

    def gather_plan(keys):
        return _gather_plan([pieces[key].astype(BF16) for key in keys], [key in col_window for key in keys])

    def hosting(name, call):
        keys = behind.get(name, [])
        outs = call(gather_plan(keys) if keys else None)
        gathered.update(zip(keys, outs[len(outs) - len(keys):]))
        return outs[:len(outs) - len(keys)]

    mod_all, *landed = _run_comm(_merge_plans([_gather_plan([mod_part], [False]), gather_plan(first)]), "gather_mod_and_first")
    gathered.update(zip(first, landed))
    mod_mine = lax.dynamic_index_in_dim(mod_all, me, axis=2, keepdims=False)
    mod_mine = mod_mine.reshape(n_layers, 3, 3, d)
    wr_b, wi_b = lru_w_r[0].astype(BF16), lru_w_i[0].astype(BF16)
    eye2 = jnp.eye(2 * cw_in, dtype=BF16).reshape(2, cw_in, 2 * cw_in)

    def w_gu(q):
        return gathered[("ffn_w_gu", q)]

    def w_d4(q):
        return gathered[("ffn_w_down", q)].reshape(1, HIDDEN_CHUNKS, fc, d)

    saved = []
    xcur = x2
    for layer in range(n_layers):
        for sub in range(3):
            gvec = norm_full[layer, sub].reshape(1, d)
            shift = mod_mine[layer, sub, 0].reshape(1, d)
            scale1p = 1.0 + mod_mine[layer, sub, 1].reshape(1, d)
            gmul = 1.0 + mod_mine[layer, sub, 2].reshape(1, d)
            tag = f"l{layer}s{sub}"
            h = _norm_fwd(xcur, gvec, scale1p, shift, tag + "_norm")
            rec = dict(x=xcur, h=h, g=gvec, scale1p=scale1p, gmul=gmul, w=MACARON_W if sub != 1 else 1.0)
            if sub != 1:
                lj = layer * 2 + sub // 2
                gu2, a = hosting(tag + "_gu", lambda plan: _ffn_gu(tag + "_gu", h, w_gu(lj), plan))
                yv, xcur = hosting(tag + "_down", lambda plan: _ffn_down(tag + "_down", a, w_d4(lj), xcur, gmul, plan))
                rec.update(kind="ffn", lj=lj, gu2=gu2, a=a, y=yv)
            elif layer % 2 == 0:
                w_qkv = gathered[("sb_w_qkv", 0)][0]
                w_o = gathered[("sb_w_o", 0)].reshape(d, d)
                qkv = _mm_nn(tag + "_qkv", h, w_qkv, BF16)[0]
                o, ltot = hosting("sb_fwd", lambda plan: _sb_fwd(qkv, d, plan))
                yv, xcur = _mm_nn(tag + "_wo", o, w_o, [BF16, F32], extras=[(xcur, "tile"), (gmul, "row")],
                                  epilogue=lambda accs, ex: (accs[0], ex[0] + ex[1] * accs[0]))
                rec.update(kind="sb", qkv=qkv, o=o, ltot=ltot, y=yv, w_qkv=w_qkv, w_o=w_o)
            else:
                w_in = _chunks_to_cols("lru_w_in_cols", gathered[("lru_w_in", 0)][0], eye2)
                w_out = gathered[("lru_w_out", 0)].reshape(r_dim, d)
                gx = _mm_nn(tag + "_win", h, w_in, F32)[0]
                ymix, hs = _lru_fwd(gx, lru_small, wr_b, wi_b)
                yv, xcur = _mm_nn(tag + "_wout", ymix, w_out, [BF16, F32], extras=[(xcur, "tile"), (gmul, "row")],
                                  epilogue=lambda accs, ex: (accs[0], ex[0] + ex[1] * accs[0]))
                rec.update(kind="lru", gx=gx, hs=hs, ymix=ymix, y=yv, w_in=w_in, w_out=w_out)
            saved.append(rec)

    last = saved[-1]
    dxo, dy, head_sums = _loss_head(xcur, target, final_norm_g.reshape(1, d), (last["w"] * last["gmul"]))
    loss = lax.psum(head_sums[1, 0], ("x", "y", "c"))
    dgf = head_sums[0]

    c_idx = jnp.reshape(ci, (1,)).astype(jnp.int32)
    chip_idx = jnp.reshape(chip, (1,)).astype(jnp.int32)
    grads, reduced = {}, {}
    to_pair = []
    to_chips = []

    def sibling_plan(only=None):
        keys = [key for key in to_pair if only is None or key in only]
        if not keys:
            return None, keys
        return _exchange_plan([grads[key] for key in keys], [key in col_window for key in keys], 4, _sibling_route), keys

    def sibling_done(keys, recv4):
        for key, r4 in zip(keys, recv4):
            to_pair.remove(key)
            to_chips.append((key, _pair_sum(grads[key], r4, c_idx, f"rs_pair_sum_{key[0]}{key[1]}", cols=key in col_window)))

    def chip_plan(only=None):
        items = [item for item in to_chips if only is None or item[0] in only]
        if not items:
            return None, items
        return _exchange_plan([p4 for _, p4 in items], [False] * len(items), 3, _chip_route), items

    def chips_done(items, recv3):
        for item, r3 in zip(items, recv3):
            to_chips.remove(item)
            reduced[item[0]] = (item[1], r3)

    def behind(call, make_plan, done, more=None):
        plan, items = make_plan()
        n_mine = len(plan.outs) if plan else 0
        n_more = len(more.outs) if more else 0
        outs = call(_merge_plans([plan, more]))
        n_own = len(outs) - n_mine - n_more
        done(items, outs[n_own:n_own + n_mine])
        return list(outs[:n_own]) + list(outs[n_own + n_mine:])

    carried = {
        "l1s1b_dymix": ("sibling", None), "lru_bwd": ("chips", [("ffn_w_gu", n_ffn - 1)]),
        "l1s0b_da": ("sibling", None), "l1s0b_dwgu": ("chips", [("ffn_w_down", n_ffn - 1)]),
        "l1s0b_dh": ("chips", [("lru_w_out", 0), ("lru_w_in", 0)]),
        "l0s2b_da": ("sibling", None), "l0s2b_dwgu": ("chips", [("ffn_w_down", n_ffn - 2)]),
    }

    def carrying(name, call):
        if name not in carried:
            return call(None)
        stage, only = carried[name]
        if stage == "sibling":
            return behind(call, functools.partial(sibling_plan, only), sibling_done)
        return behind(call, functools.partial(chip_plan, only), chips_done)

    def at_once(make_plan, done, name):
        plan, items = make_plan()
        if plan:
            done(items, _run_comm(plan, name))

    def add_grad(key, value):
        grads[key] = value
        to_pair.append(key)

    dmod = [[None] * 3 for _ in range(n_layers)]
    dnorm = [[None] * 3 for _ in range(n_layers)]
    dlru_small = wri_all = None
    for idx in reversed(range(len(saved))):
        rec = saved[idx]
        layer, sub = divmod(idx, 3)
        tag = f"l{layer}s{sub}b"
        if rec["kind"] == "ffn" and idx > 0:
            lj = rec["lj"]
            (dgu2,) = carrying(tag + "_da", lambda plan: _ffn_da(tag + "_da", dy, w_d4(lj), rec["gu2"], plan))
            dwd = _ffn_dwd(tag + "_dwd", rec["a"], dy)[0].reshape(gathered[("ffn_w_down", lj)].shape)
            (dwgu,) = carrying(tag + "_dwgu", lambda plan: _ffn_dwgu(tag + "_dwgu", rec["h"], dgu2, plan))
            (dh,) = carrying(tag + "_dh", lambda plan: _ffn_dh(tag + "_dh", dgu2, w_gu(lj), plan))
            add_grad(("ffn_w_down", lj), dwd)
            add_grad(("ffn_w_gu", lj), dwgu)
        elif rec["kind"] == "ffn":
            lj = rec["lj"]
            at_once(sibling_plan, sibling_done, "rs_sibling_" + tag)
            (dgu2,) = behind(lambda plan: _ffn_da(tag + "_da", dy, w_d4(lj), rec["gu2"], plan), chip_plan, chips_done)
            add_grad(("ffn_w_down", lj), _ffn_dwd(tag + "_dwd", rec["a"], dy)[0].reshape(gathered[("ffn_w_down", lj)].shape))
            at_once(sibling_plan, sibling_done, "rs_sibling_" + tag + "_dwd")
            (dwgu,) = behind(lambda plan: _ffn_dwgu(tag + "_dwgu", rec["h"], dgu2, plan), chip_plan, chips_done)
            add_grad(("ffn_w_gu", lj), dwgu)
            at_once(sibling_plan, sibling_done, "rs_sibling_" + tag + "_dwgu")
            (dh,) = behind(lambda plan: _ffn_dh(tag + "_dh", dgu2, w_gu(lj), plan), chip_plan, chips_done)
        elif rec["kind"] == "sb":
            at_once(sibling_plan, sibling_done, "rs_sibling_" + tag)
            do = _mm_nt(tag + "_do", dy, rec["w_o"], BF16)
            dwo = _mm_tn(tag + "_dwo", rec["o"], dy, BF16)
            wri = _pack_flat([dwr, dwi], LANES, 512, BF16)[None]
            dqkv3, wri_all = behind(lambda plan: _sb_bwd(rec["qkv"], do, rec["ltot"], d, plan), chip_plan, chips_done,
                                    more=_gather_plan([wri], [False]))
            add_grad(("sb_w_o", 0), dwo.reshape(gathered[("sb_w_o", 0)].shape))
            dh = _mm_nt_stack(tag + "_dh", dqkv3, rec["w_qkv"], F32)
            add_grad(("sb_w_qkv", 0), _mm_tn_stack(tag + "_dwqkv", rec["h"], dqkv3, BF16)[None])
        else:
            (dymix,) = carrying(tag + "_dymix", lambda plan: _mm_nt(tag + "_dymix", dy, rec["w_out"], F32, plan)
                                if plan else [_mm_nt(tag + "_dymix", dy, rec["w_out"], F32)])
            dwout = _mm_tn(tag + "_dwout", rec["ymix"], dy, BF16).reshape(gathered[("lru_w_out", 0)].shape)
            dgx2, dlru_small, dwr, dwi = carrying("lru_bwd", lambda plan: _lru_bwd(rec["gx"], rec["hs"], dymix, lru_small,
                                                                                wr_b, wi_b, plan))
            dh = _mm_nt_stack(tag + "_dh", dgx2, rec["w_in"], F32)
            dw_in = _mm_tn_stack(tag + "_dwin", rec["h"], dgx2, BF16)
            add_grad(("lru_w_out", 0), dwout)
            add_grad(("lru_w_in", 0), _cols_to_chunks("lru_w_in_chunks", dw_in, eye2)[None])
        prev = saved[idx - 1] if idx > 0 else None
        gw_prev = (prev["w"] * prev["gmul"]) if prev is not None else jnp.zeros((1, d), F32)
        dxo, dy, sums = _adaln_bwd(dh, rec["x"], rec["y"], dxo, rec["g"], rec["scale1p"], rec["w"], gw_prev, tag + "_adaln")
        dmod[layer][sub] = sums[0:3]
        dnorm[layer][sub] = sums[3]
    grad_x = dxo.reshape(x.shape)

    dmod_mine = jnp.stack([jnp.stack(dmod[layer]) for layer in range(n_layers)])
    dnorm_mine = jnp.stack([jnp.stack(dnorm[layer]) for layer in range(n_layers)])
    assert not to_pair and not to_chips
    small_shapes = [(n_layers, 9 * d), (n_layers, 3, d), (8, r_dim), (d,)]
    small3 = _pack_flat([dmod_mine, dnorm_mine, dlru_small, dgf], LANES, 256, F32)
    n_small3 = small3.shape[0]
    all3 = _allgather(small3[None], "gather_small_grads").reshape(N_DEV, n_small3, LANES)
    gsum = _sum_devices(all3, "sum_small_grads").reshape(-1)
    g_mod_b, g_norm_full, g_lru_small, g_final = _unpack_flat(gsum, small_shapes)
    wri_sum = _sum_devices(wri_all.reshape(N_DEV, -1, LANES), "sum_gate_weight_grads").reshape(-1)
    g_wr, g_wi = _unpack_flat(wri_sum, [lru_w_r.shape, lru_w_i.shape])
    dmod_all = all3.reshape(N_DEV, -1)[:, :n_layers * 9 * d].reshape(N_DEV, n_layers, N_DEV, mod_cols)
    dmod_cols = jnp.transpose(lax.dynamic_index_in_dim(dmod_all, me, axis=2, keepdims=False), (1, 0, 2))

    out_g, out_d, out_m, out_v = {}, {}, {}, {}
    out_g["mod_w"], out_d["mod_w"], out_m["mod_w"], out_v["mod_w"] = _mod_w_update(c_all, dmod_cols, mod_w, m_mod_w, v_mod_w)

    g_norm_shard = lax.dynamic_slice_in_dim(g_norm_full, me * ng, ng, axis=2)
    g_lru_shard = lax.dynamic_slice_in_dim(g_lru_small, me * rs, rs, axis=1)
    small_grads = dict(mod_b=g_mod_b, norm_g=g_norm_shard, lru_conv_w=g_lru_shard[0:4].reshape(lru_conv_w.shape),
                       lru_conv_b=g_lru_shard[4:5], lru_b_r=g_lru_shard[5:6], lru_b_i=g_lru_shard[6:7],
                       lru_lambda=g_lru_shard[7:8], final_norm_g=g_final)
    for n, g in (("lru_w_r", g_wr), ("lru_w_i", g_wi)):
        view = lambda arr: arr.reshape(-1, LRU_BLOCK_W)
        outs = _adam_update("adam_" + n, view(weights[n]), view(mom_m[n]), view(mom_v[n]), [view(g)])
        out_g[n], out_d[n], out_m[n], out_v[n] = [o.reshape(weights[n].shape) for o in outs]
    small_names = list(small_grads)
    sw = _pack_flat([weights[n] for n in small_names], LANES, 256, F32)
    sg = _pack_flat([small_grads[n] for n in small_names], LANES, 256, F32)
    sm = _pack_flat([mom_m[n] for n in small_names], LANES, 256, F32)
    sv = _pack_flat([mom_v[n] for n in small_names], LANES, 256, F32)
    s_outs = _adam_update("adam_small", sw, sm, sv, [sg])
    small_shapes2 = [weights[n].shape for n in small_names]
    for dst, flat in zip((out_g, out_d, out_m, out_v), s_outs):
        for n, arr in zip(small_names, _unpack_flat(flat.reshape(-1), small_shapes2)):
            dst[n] = arr

    for n in ["ffn_w_gu", "ffn_w_down", "sb_w_qkv", "sb_w_o", "lru_w_in", "lru_w_out"]:
        shp = weights[n].shape
        shard3 = (math.prod(shp[:-2]),) + shp[-2:]
        view = lambda arr: arr.reshape(shard3)
        outs = None
        for q in range(shard3[0]):
            fills = outs if outs is not None else [lax.empty(shard3, F32) for _ in range(4)]
            p4, r3 = reduced[(n, q)]
            outs = _adam_shard(f"adam_{n}{q}", view(weights[n]), view(mom_m[n]), view(mom_v[n]), p4, r3, chip_idx,
                               first=q, fills=fills if shard3[0] > 1 else None)
        out_g[n], out_d[n], out_m[n], out_v[n] = [o.reshape(shp) for o in outs]

    return (loss, grad_x, *[out_g[n] for n in names], *[out_d[n] for n in names], *[out_m[n] for n in names],
            *[out_v[n] for n in names])
```

```python
import functools
import math

import jax
import jax.numpy as jnp
from jax import lax
from jax.experimental import pallas as pl
from jax.experimental.pallas import tpu as pltpu

F32 = jnp.float32
BF16 = jnp.bfloat16
SDS = jax.ShapeDtypeStruct
MESH = pl.DeviceIdType.MESH
ANY = pl.BlockSpec(memory_space=pl.ANY)

N_DEV = 8
LANES = 128
HEAD_DIM = 64
LRU_BLOCK_W = 128
LRU_C = 8.0
MACARON_W = 0.5
NORM_EPS = 1e-6
ADAM_LR = 0.001
ADAM_B1 = 0.9
ADAM_B2 = 0.999
ADAM_EPS = 1e-08
ADAM_WD = 0.01
ADAM_STEP = 10
VMEM_LIMIT = 56 * 1024 * 1024
GELU_C = math.sqrt(2.0 / math.pi)
GELU_K = 0.044715

DIMS = {
    "nn": (((1,), (0,)), ((), ())),
    "nt": (((1,), (1,)), ((), ())),
    "tn": (((0,), (0,)), ((), ())),
}


def _pcall(body, **kw):
    return pl.pallas_call(body, **kw)


def _params(sem=None):
    return pltpu.CompilerParams(dimension_semantics=sem, vmem_limit_bytes=VMEM_LIMIT)


def _tile(n, prefs):
    for p in prefs:
        if n % p == 0:
            return p
    return n


def _dot(a, b, dims):
    return lax.dot_general(a, b, DIMS[dims], preferred_element_type=F32)


def _softplus(z):
    return jnp.maximum(z, 0.0) + jnp.log(1.0 + jnp.exp(-jnp.abs(z)))


def _sigmoid(z):
    return 0.5 * jnp.tanh(0.5 * z) + 0.5


def _mesh_pos():
    return lax.axis_index("x"), lax.axis_index("y"), lax.axis_index("c")


def _allgather(xs, name, cols=False):
    return _run_comm(_gather_plan([xs], [cols]), name)[0]


class _CommPlan:
    def __init__(self, ins, outs, n_remote, n_local, phases):
        self.ins, self.outs, self.n_remote, self.n_local, self.phases = ins, outs, n_remote, n_local, phases

    def scratch(self):
        return [pltpu.SemaphoreType.DMA((self.n_remote,)), pltpu.SemaphoreType.DMA((self.n_remote,)),
                pltpu.SemaphoreType.DMA((max(self.n_local, 1),))]


def _merge_plans(plans):
    plans = [p for p in plans if p is not None]
    if len(plans) <= 1:
        return plans[0] if plans else None

    def phase(k):
        def run(in_refs, out_refs, send_sems, recv_sems, local_sems, r0=0, l0=0):
            i0 = o0 = 0
            for p in plans:
                p.phases[k](in_refs[i0:i0 + len(p.ins)], out_refs[o0:o0 + len(p.outs)], send_sems, recv_sems, local_sems, r0, l0)
                i0, o0, r0, l0 = i0 + len(p.ins), o0 + len(p.outs), r0 + p.n_remote, l0 + p.n_local
        return run

    return _CommPlan(sum([p.ins for p in plans], []), sum([p.outs for p in plans], []), sum(p.n_remote for p in plans),
                     sum(p.n_local for p in plans), [phase(0), phase(1), phase(2)])


def _run_comm(plan, name):
    n_in, n_out = len(plan.ins), len(plan.outs)

    def body(*refs):
        in_refs, out_refs, sems = refs[:n_in], refs[n_in:n_in + n_out], refs[n_in + n_out:]
        for phase in plan.phases:
            phase(in_refs, out_refs, *sems)

    return _pcall(body, name=name, out_shape=plan.outs, in_specs=[ANY] * n_in, out_specs=[ANY] * n_out,
                  scratch_shapes=plan.scratch())(*plan.ins)


def _col_window(ref, idx, width):
    return ref.at[:, :, pl.ds(pl.multiple_of(idx * width, math.gcd(width, LANES)), width)]


def _gather_plan(shards, cols):
    n = len(shards)
    outs = [SDS((s.shape[0], s.shape[1], N_DEV * s.shape[2]) if cl else (s.shape[0], N_DEV) + s.shape[1:], s.dtype)
            for s, cl in zip(shards, cols)]

    def copies(a, in_refs, out_refs, send_sems, recv_sems, local_sems, r0=0, l0=0):
        x, y, c = _mesh_pos()
        sibling = (x, y, 1 - c)
        chips = [(1 - x, y), (x, 1 - y), (1 - x, 1 - y)]
        width = shards[a].shape[2]

        def block(px, py, pc):
            idx = 4 * px + 2 * py + pc
            return _col_window(out_refs[a], idx, width) if cols[a] else out_refs[a].at[:, idx]

        def copy(k, owner, to, src=None):
            sem = r0 + 7 * a + k
            return pltpu.make_async_remote_copy(
                src_ref=block(*owner) if src is None else src, dst_ref=block(*owner),
                send_sem=send_sems.at[sem], recv_sem=recv_sems.at[sem], device_id=to, device_id_type=MESH)

        me = (x, y, c)
        first = [copy(0, me, sibling, src=in_refs[a])]
        first += [copy(1 + j, me, (*chip, c), src=in_refs[a]) for j, chip in enumerate(chips)]
        passed = [copy(4 + j, (*chip, c), sibling) for j, chip in enumerate(chips)]
        landed = [copy(1 + j, (*chip, c), me) for j, chip in enumerate(chips)]
        from_sibling = [copy(0, sibling, me)] + [copy(4 + j, (*chip, 1 - c), me) for j, chip in enumerate(chips)]
        mine = pltpu.make_async_copy(in_refs[a], block(*me), local_sems.at[l0 + a])
        return first, passed, landed, from_sibling, mine

    def start(*refs):
        for a in range(n):
            first, _, _, _, mine = copies(a, *refs)
            mine.start()
            for cp in first:
                cp.start()

    def pass_on(*refs):
        for a in range(n):
            _, passed, landed, _, _ = copies(a, *refs)
            for cp, fwd in zip(landed, passed):
                cp.wait_recv()
                fwd.start()

    def finish(*refs):
        for a in range(n):
            first, passed, _, from_sibling, mine = copies(a, *refs)
            for cp in from_sibling:
                cp.wait_recv()
            for cp in first + passed:
                cp.wait_send()
            mine.wait()

    return _CommPlan(list(shards), outs, 7 * n, n, [start, pass_on, finish])


def _exchange_plan(srcs, cols, n_slots, route):
    n = len(srcs)
    outs = []
    for g, cl in zip(srcs, cols):
        shard = (g.shape[0], g.shape[1], g.shape[2] // N_DEV) if cl else (g.shape[0],) + g.shape[2:]
        outs.append(SDS((n_slots,) + shard, g.dtype))

    def copies(in_refs, out_refs, send_sems, recv_sems, local_sems, r0=0, l0=0):
        x, y, c = _mesh_pos()
        made = []
        for a in range(n):
            for s in range(n_slots):
                chunk, target = route(x, y, c, s)
                src = _col_window(in_refs[a], chunk, outs[a].shape[3]) if cols[a] else in_refs[a].at[:, chunk]
                sem = r0 + a * n_slots + s
                made.append(pltpu.make_async_remote_copy(
                    src_ref=src, dst_ref=out_refs[a].at[s], send_sem=send_sems.at[sem], recv_sem=recv_sems.at[sem],
                    device_id=target, device_id_type=MESH))
        return made

    def start(*refs):
        for cp in copies(*refs):
            cp.start()

    def nothing(*refs):
        pass

    def finish(*refs):
        made = copies(*refs)
        for cp in made:
            cp.wait_recv()
        for cp in made:
            cp.wait_send()

    return _CommPlan(list(srcs), outs, n * n_slots, 0, [start, nothing, finish])


def _sibling_route(x, y, c, k):
    return 2 * k + 1 - c, (x, y, 1 - c)


def _chip_route(x, y, c, j):
    px, py = [(1 - x, y), (x, 1 - y), (1 - x, 1 - y)][j]
    return 2 * px + py, (px, py, c)


def _pair_sum(grads, recv4, c_idx, name, cols=False):
    _, p, r, cdim = recv4.shape
    tr = _tile(r, (512, 256, 176, 160, 128, 64, 32, 16))

    def body(c_ref, a_ref, b_ref, o_ref):
        o_ref[...] = (a_ref[...].astype(F32) + b_ref[...].astype(F32)).astype(o_ref.dtype)

    blk = (None, None, tr, cdim)
    if cols:
        own = pl.BlockSpec((None, tr, cdim), lambda k, q, i, c_ref: (q, i, 2 * k + c_ref[0]))
    else:
        own = pl.BlockSpec(blk, lambda k, q, i, c_ref: (q, 2 * k + c_ref[0], i, 0))
    grid_spec = pltpu.PrefetchScalarGridSpec(
        num_scalar_prefetch=1, grid=(4, p, r // tr),
        in_specs=[own, pl.BlockSpec(blk, lambda k, q, i, c_ref: (k, q, i, 0))],
        out_specs=pl.BlockSpec(blk, lambda k, q, i, c_ref: (q, k, i, 0)))
    return _pcall(body, name=name, grid_spec=grid_spec, out_shape=SDS((p, 4, r, cdim), grads.dtype),
                  compiler_params=_params(("parallel", "parallel", "parallel")))(c_idx, grads, recv4)


def _mm(name, ins, prods, n_acc, acc_shape, epi_idx, epilogue, out_shapes, out_specs, grid, dims, plan=None):
    n_in, n_out, nk = len(ins), len(out_shapes), grid[2]
    n_acc_refs = n_acc if nk > 1 else 0
    c_ins, c_outs = (plan.ins, plan.outs) if plan else ([], [])
    n_cin, n_cout = len(c_ins), len(c_outs)

    def body(*refs):
        in_refs, c_in = refs[:n_in], refs[n_in:n_in + n_cin]
        rest = refs[n_in + n_cin:]
        out_refs, c_out = rest[:n_out], rest[n_out:n_out + n_cout]
        rest = rest[n_out + n_cout:]
        acc_refs, sems = rest[:n_acc_refs], rest[n_acc_refs:]
        ids = [pl.program_id(axis) for axis in range(3)]
        if plan:
            @pl.when((ids[0] == 0) & (ids[1] == 0) & (ids[2] == 0))
            def _():
                plan.phases[0](c_in, c_out, *sems)

        def finish(accs):
            outs = epilogue(accs, [in_refs[i][...] for i in epi_idx])
            for o_ref, o in zip(out_refs, outs):
                if isinstance(o, tuple):
                    for plane, part in enumerate(o):
                        o_ref[plane] = part.astype(o_ref.dtype)
                else:
                    o_ref[...] = o.astype(o_ref.dtype)

        def operand(ref_idx):
            if isinstance(ref_idx, tuple):
                return in_refs[ref_idx[0]][ref_idx[1]]
            return in_refs[ref_idx][...]

        if nk == 1:
            accs = [None] * n_acc
            for ia, ib, iacc in prods:
                term = _dot(operand(ia), operand(ib), dims)
                accs[iacc] = term if accs[iacc] is None else accs[iacc] + term
            finish(accs)
        else:
            @pl.when(ids[2] == 0)
            def _():
                for acc in acc_refs:
                    acc[...] = jnp.zeros_like(acc)

            for ia, ib, iacc in prods:
                acc_refs[iacc][...] += _dot(operand(ia), operand(ib), dims)

            @pl.when(ids[2] == nk - 1)
            def _():
                finish([acc[...] for acc in acc_refs])

        if plan:
            @pl.when((ids[0] == grid[0] - 1) & (ids[1] == grid[1] - 1) & (ids[2] == nk - 1))
            def _():
                plan.phases[1](c_in, c_out, *sems)
                plan.phases[2](c_in, c_out, *sems)

    return _pcall(
        body, name=name, grid=grid, in_specs=[s for _, s in ins] + [ANY] * n_cin,
        out_specs=list(out_specs) + [ANY] * n_cout, out_shape=list(out_shapes) + list(c_outs),
        scratch_shapes=[pltpu.VMEM(acc_shape, F32) for _ in range(n_acc_refs)] + (plan.scratch() if plan else []),
        compiler_params=_params(("arbitrary",) * 3 if plan else ("parallel", "parallel", "arbitrary")),
    )(*[a for a, _ in ins], *c_ins)


def _plain(accs, _):
    return accs


def _mm_nn(name, a, b, out_dtype, extras=(), epilogue=_plain, n_out=1):
    m, kd = a.shape
    n = b.shape[1]
    tm, tn, tk = _tile(m, (1024, 512, 256, 128)), _tile(n, (640, 512, 256, 128)), _tile(kd, (1280, 1024, 512, 256, 128))
    ins = [(a, pl.BlockSpec((tm, tk), lambda i, j, k: (i, k))), (b, pl.BlockSpec((tk, tn), lambda i, j, k: (k, j)))]
    for arr, kind in extras:
        if kind == "tile":
            ins.append((arr, pl.BlockSpec((tm, tn), lambda i, j, k: (i, j))))
        else:
            ins.append((arr, pl.BlockSpec((1, tn), lambda i, j, k: (0, j))))
    dts = out_dtype if isinstance(out_dtype, (list, tuple)) else [out_dtype] * n_out
    return _mm(name, ins, [(0, 1, 0)], 1, (tm, tn), list(range(2, len(ins))), epilogue,
               [SDS((m, n), dt) for dt in dts], [pl.BlockSpec((tm, tn), lambda i, j, k: (i, j)) for _ in dts],
               (m // tm, n // tn, kd // tk), "nn")


def _mm_nt(name, a, b, out_dtype, plan=None):
    m, kd = a.shape
    n = b.shape[0]
    tm, tn, tk = _tile(m, (1024, 512, 256, 128)), _tile(n, (640, 512, 256, 128)), _tile(kd, (1024, 512, 256, 128))
    ins = [(a, pl.BlockSpec((tm, tk), lambda i, j, k: (i, k))), (b, pl.BlockSpec((tn, tk), lambda i, j, k: (j, k)))]
    outs = _mm(name, ins, [(0, 1, 0)], 1, (tm, tn), [], _plain, [SDS((m, n), out_dtype)],
               [pl.BlockSpec((tm, tn), lambda i, j, k: (i, j))], (m // tm, n // tn, kd // tk), "nt", plan=plan)
    return outs if plan else outs[0]


def _mm_tn(name, a, b, out_dtype):
    t, m = a.shape
    n = b.shape[1]
    tm, tn, tk = _tile(m, (640, 512, 256, 128)), _tile(n, (1024, 512, 256, 128)), t
    ins = [(a, pl.BlockSpec((tk, tm), lambda i, j, k: (k, i))), (b, pl.BlockSpec((tk, tn), lambda i, j, k: (k, j)))]
    return _mm(name, ins, [(0, 1, 0)], 1, (tm, tn), [], _plain, [SDS((m, n), out_dtype)],
               [pl.BlockSpec((tm, tn), lambda i, j, k: (i, j))], (m // tm, n // tn, t // tk), "tn")[0]


def _mm_nt_stack(name, a3, b, out_dtype):
    cc, m, kd = a3.shape
    n = b.shape[0]
    tm, tn, tk = _tile(m, (1024, 512, 256, 128)), _tile(n, (1024, 512, 256, 128)), _tile(kd, (1280, 1024, 512, 256, 128))
    nk = kd // tk
    ins = [(a3, pl.BlockSpec((None, tm, tk), lambda i, j, k: (k // nk, i, k % nk))),
           (b, pl.BlockSpec((tn, tk), lambda i, j, k: (j, k)))]
    return _mm(name, ins, [(0, 1, 0)], 1, (tm, tn), [], _plain, [SDS((m, n), out_dtype)],
               [pl.BlockSpec((tm, tn), lambda i, j, k: (i, j))], (m // tm, n // tn, cc * nk), "nt")[0]


def _mm_tn_stack(name, a, b3, out_dtype):
    t, m = a.shape
    cc, _, n = b3.shape
    tm, tn, tk = _tile(m, (512, 256, 128)), _tile(n, (1280, 1024, 512, 256, 128)), t
    nj = n // tn
    ins = [(a, pl.BlockSpec((tk, tm), lambda i, j, k: (k, i))),
           (b3, pl.BlockSpec((None, tk, tn), lambda i, j, k: (j // nj, k, j % nj)))]
    return _mm(name, ins, [(0, 1, 0)], 1, (tm, tn), [], _plain, [SDS((m, cc * n), out_dtype)],
               [pl.BlockSpec((tm, tn), lambda i, j, k: (i, j))], (m // tm, cc * nj, t // tk), "tn")[0]


def _chunks_to_cols(name, wc, eye2):
    nch, d, cw = wc.shape
    tm = _tile(d, (1024, 512, 256, 128))
    ins = [(wc, pl.BlockSpec((None, tm, cw), lambda i, j, k: (2 * j + k, i, 0))),
           (eye2, pl.BlockSpec((None, cw, 2 * cw), lambda i, j, k: (k, 0, 0)))]
    return _mm(name, ins, [(0, 1, 0)], 1, (tm, 2 * cw), [], _plain, [SDS((d, nch * cw), wc.dtype)],
               [pl.BlockSpec((tm, 2 * cw), lambda i, j, k: (i, j))], (d // tm, nch // 2, 2), "nn")[0]


def _cols_to_chunks(name, full, eye2):
    d, n = full.shape
    _, cw, _ = eye2.shape
    nch = n // cw
    tm = _tile(d, (1024, 512, 256, 128))
    ins = [(full, pl.BlockSpec((tm, 2 * cw), lambda i, j, k: (i, j // 2))),
           (eye2, pl.BlockSpec((None, cw, 2 * cw), lambda i, j, k: (j % 2, 0, 0)))]
    return _mm(name, ins, [(0, 1, 0)], 1, (tm, cw), [], _plain, [SDS((nch, d, cw), full.dtype)],
               [pl.BlockSpec((None, tm, cw), lambda i, j, k: (j, i, 0))], (d // tm, nch, 1), "nt")[0]


def _row_tile(t):
    return _tile(t, (256, 128, 64, 32, 16, 8))


def _norm_fwd(x, g, scale1p, shift, name):
    t, d = x.shape
    tr = _row_tile(t)

    def body(x_ref, g_ref, s_ref, b_ref, h_ref):
        xv = x_ref[...]
        inv = lax.rsqrt(jnp.mean(xv * xv, axis=-1, keepdims=True) + NORM_EPS)
        h_ref[...] = ((xv * inv) * g_ref[...] * s_ref[...] + b_ref[...]).astype(h_ref.dtype)

    vec = pl.BlockSpec((1, d), lambda i: (0, 0))
    return _pcall(body, name=name, grid=(t // tr,), in_specs=[pl.BlockSpec((tr, d), lambda i: (i, 0)), vec, vec, vec],
                  out_specs=pl.BlockSpec((tr, d), lambda i: (i, 0)), out_shape=SDS((t, d), BF16),
                  compiler_params=_params(("parallel",)))(x, g, scale1p, shift)


def _adaln_bwd(dh, x, y, dxo, g, scale1p, w_sub, gw_prev, name, plan=None):
    t, d = x.shape
    tr = _row_tile(t)

    def body(in_refs, out_refs, _):
        (dh_ref, x_ref, y_ref, dxo_ref, g_ref, s_ref, gw_ref), (dx_ref, dyp_ref, sums_ref) = in_refs, out_refs
        i = pl.program_id(0)

        @pl.when(i == 0)
        def _():
            sums_ref[...] = jnp.zeros_like(sums_ref)

        xv, dhv, dxov = x_ref[...], dh_ref[...], dxo_ref[...]
        inv = lax.rsqrt(jnp.mean(xv * xv, axis=-1, keepdims=True) + NORM_EPS)
        xn = xv * inv
        gv = g_ref[...]
        dn = dhv * s_ref[...]
        dxn = dn * gv
        dx = inv * (dxn - xn * jnp.mean(dxn * xn, axis=-1, keepdims=True)) + dxov
        dx_ref[...] = dx
        dyp_ref[...] = (gw_ref[...] * dx).astype(dyp_ref.dtype)
        sums_ref[0:1, :] += jnp.sum(dhv, axis=0, keepdims=True)
        sums_ref[1:2, :] += jnp.sum(dhv * (xn * gv), axis=0, keepdims=True)
        sums_ref[2:3, :] += jnp.sum(w_sub * y_ref[...] * dxov, axis=0, keepdims=True)
        sums_ref[3:4, :] += jnp.sum(dn * xn, axis=0, keepdims=True)

    blk = pl.BlockSpec((tr, d), lambda i: (i, 0))
    vec = pl.BlockSpec((1, d), lambda i: (0, 0))
    return _host_call(
        body, name, t // tr, [dh, x, y, dxo, g, scale1p, gw_prev], [blk, blk, blk, blk, vec, vec, vec],
        [SDS((t, d), F32), SDS((t, d), BF16), SDS((8, d), F32)], [blk, blk, pl.BlockSpec((8, d), lambda i: (0, 0))], [], plan)


def _loss_head(x, target, gf, gw_prev):
    t, d = x.shape
    tr = _row_tile(t)
    nt = t // tr

    def body(x_ref, tg_ref, g_ref, gw_ref, dx_ref, dyp_ref, sums_ref):
        i = pl.program_id(0)

        @pl.when(i == 0)
        def _():
            sums_ref[...] = jnp.zeros_like(sums_ref)

        xv = x_ref[...]
        inv = lax.rsqrt(jnp.mean(xv * xv, axis=-1, keepdims=True) + NORM_EPS)
        xn = xv * inv
        gv = g_ref[...]
        err = xn * gv - tg_ref[...]
        dyv = err * (1.0 / d)
        dxn = dyv * gv
        dx = inv * (dxn - xn * jnp.mean(dxn * xn, axis=-1, keepdims=True))
        dx_ref[...] = dx
        dyp_ref[...] = (gw_ref[...] * dx).astype(dyp_ref.dtype)
        sums_ref[0:1, :] += jnp.sum(dyv * xn, axis=0, keepdims=True)
        sums_ref[1:2, :] += jnp.sum(err * err, axis=0, keepdims=True)

        @pl.when(i == nt - 1)
        def _():
            tot = jnp.sum(sums_ref[1:2, :], axis=1, keepdims=True) * (0.5 / d)
            sums_ref[1:2, :] = jnp.broadcast_to(tot, (1, d))

    blk = pl.BlockSpec((tr, d), lambda i: (i, 0))
    vec = pl.BlockSpec((1, d), lambda i: (0, 0))
    return _pcall(
        body, name="loss_head", grid=(nt,), in_specs=[blk, blk, vec, vec],
        out_specs=[blk, blk, pl.BlockSpec((8, d), lambda i: (0, 0))],
        out_shape=[SDS((t, d), F32), SDS((t, d), BF16), SDS((8, d), F32)],
        compiler_params=_params(("arbitrary",)))(x, target, gf, gw_prev)


HIDDEN_CHUNKS = N_DEV // 2


def _ffn_tiles(t, d):
    return _tile(t, (1024, 512, 256, 128)), _tile(d, (1024, 512, 256, 128))


def _ffn_gu(name, h, wgu, plan=None):
    t, d = h.shape
    fc, nc = wgu.shape[3], HIDDEN_CHUNKS
    tm, _ = _ffn_tiles(t, d)

    def epi_gu(accs, _):
        gpre, up = accs
        return (gpre, up), gpre * _sigmoid(gpre) * up

    wblk = (None, None, d, fc)
    ins = [(h, pl.BlockSpec((tm, d), lambda i, c, k: (i, 0))),
           (wgu, pl.BlockSpec(wblk, lambda i, c, k: (0, c, 0, 0))),
           (wgu, pl.BlockSpec(wblk, lambda i, c, k: (0, c + nc, 0, 0)))]
    return _mm(name, ins, [(0, 1, 0), (0, 2, 1)], 2, (tm, fc), [], epi_gu,
               [SDS((2, nc, t, fc), BF16), SDS((nc, t, fc), BF16)],
               [pl.BlockSpec((2, None, tm, fc), lambda i, c, k: (0, c, i, 0)),
                pl.BlockSpec((None, tm, fc), lambda i, c, k: (c, i, 0))],
               (t // tm, nc, 1), "nn", plan=plan)


def _ffn_down(name, a, wd4, x, gmul, plan=None):
    nc, t, fc = a.shape
    d = wd4.shape[3]
    tm = _tile(t, (512, 256, 128))

    def epi_down(accs, ex):
        (yv,), (xv, gm) = accs, ex
        return yv, xv + MACARON_W * gm * yv

    ins = [(a, pl.BlockSpec((nc, tm, fc), lambda i, j, k: (0, i, 0))),
           (wd4, pl.BlockSpec((None, nc, fc, d), lambda i, j, k: (0, 0, 0, 0), pipeline_mode=pl.Buffered(1))),
           (x, pl.BlockSpec((tm, d), lambda i, j, k: (i, 0))), (gmul, pl.BlockSpec((1, d), lambda i, j, k: (0, 0)))]
    oblk = pl.BlockSpec((tm, d), lambda i, j, k: (i, 0))
    prods = [((0, (c,)), (1, (c,)), 0) for c in range(nc)]
    return _mm(name, ins, prods, 1, (tm, d), [2, 3], epi_down, [SDS((t, d), BF16), SDS((t, d), F32)],
               [oblk, oblk], (t // tm, 1, 1), "nn", plan=plan)


def _ffn_da(name, dy, wd4, gu2, plan=None):
    t, d = dy.shape
    _, nc, fc, _ = wd4.shape
    tm, _ = _ffn_tiles(t, d)

    def epi_da(accs, ex):
        (da,), (gu,) = accs, ex
        gpre, up = gu[0].astype(F32), gu[1].astype(F32)
        s = _sigmoid(gpre)
        silu = gpre * s
        dg = da * up * (s * (1.0 + gpre * (1.0 - s)))
        return ((dg, da * silu),)

    gblk = pl.BlockSpec((2, None, tm, fc), lambda i, c, k: (0, c, i, 0))
    ins = [(dy, pl.BlockSpec((tm, d), lambda i, c, k: (i, 0))),
           (wd4, pl.BlockSpec((None, None, fc, d), lambda i, c, k: (0, c, 0, 0))), (gu2, gblk)]
    return _mm(name, ins, [(0, 1, 0)], 1, (tm, fc), [2], epi_da, [SDS((2, nc, t, fc), BF16)], [gblk],
               (t // tm, nc, 1), "nt", plan=plan)


def _ffn_dwd(name, a, dy, plan=None):
    nc, t, fc = a.shape
    d = dy.shape[1]
    _, tn = _ffn_tiles(t, d)
    ins = [(a, pl.BlockSpec((None, t, fc), lambda c, j, k: (c, 0, 0))), (dy, pl.BlockSpec((t, tn), lambda c, j, k: (0, j)))]
    return _mm(name, ins, [(0, 1, 0)], 1, (fc, tn), [], _plain, [SDS((1, nc, fc, d), BF16)],
               [pl.BlockSpec((None, None, fc, tn), lambda c, j, k: (0, c, 0, j))], (nc, d // tn, 1), "tn", plan=plan)


def _ffn_dwgu(name, h, dgu2, plan=None):
    t, d = h.shape
    _, nc, _, fc = dgu2.shape
    _, tn = _ffn_tiles(t, d)
    ins = [(h, pl.BlockSpec((t, tn), lambda i, c, k: (0, i))),
           (dgu2, pl.BlockSpec((None, None, t, fc), lambda i, c, k: (c // nc, c % nc, 0, 0)))]
    return _mm(name, ins, [(0, 1, 0)], 1, (tn, fc), [], _plain, [SDS((1, 2 * nc, d, fc), BF16)],
               [pl.BlockSpec((None, None, tn, fc), lambda i, c, k: (0, c, i, 0))], (d // tn, 2 * nc, 1), "tn", plan=plan)


def _ffn_dh(name, dgu2, wgu, plan=None):
    _, nc, t, fc = dgu2.shape
    d = wgu.shape[2]
    tm = _tile(t, (512, 256, 128))
    ins = [(dgu2, pl.BlockSpec((2, nc, tm, fc), lambda i, j, k: (0, 0, i, 0))),
           (wgu, pl.BlockSpec((None, 2 * nc, d, fc), lambda i, j, k: (0, 0, 0, 0), pipeline_mode=pl.Buffered(1)))]
    prods = [((0, (s, c)), (1, (nc * s + c,)), 0) for s in range(2) for c in range(nc)]
    return _mm(name, ins, prods, 1, (tm, d), [], _plain, [SDS((t, d), F32)],
               [pl.BlockSpec((tm, d), lambda i, j, k: (i, 0))], (t // tm, 1, 1), "nt", plan=plan)


def _sb_block(t):
    return 256 if t >= 1024 else 128


SB_STRIP = 64


def _sb_strips(blk):
    strip = min(SB_STRIP, blk)
    row = lax.broadcasted_iota(jnp.int32, (strip, blk), 0)
    col = lax.broadcasted_iota(jnp.int32, (strip, blk), 1)
    return [(slice(r0, r0 + strip), col < row + r0) for r0 in range(0, blk, strip)]


def _host_call(core, name, steps, ins, in_specs, out_shapes, out_specs, scratch, plan):
    n_in, n_out, n_scr = len(ins), len(out_shapes), len(scratch)
    c_ins, c_outs = (plan.ins, plan.outs) if plan else ([], [])
    n_cin, n_cout = len(c_ins), len(c_outs)

    def body(*refs):
        in_refs, c_in = refs[:n_in], refs[n_in:n_in + n_cin]
        rest = refs[n_in + n_cin:]
        out_refs, c_out = rest[:n_out], rest[n_out:n_out + n_cout]
        rest = rest[n_out + n_cout:]
        scr, sems = rest[:n_scr], rest[n_scr:]
        step = pl.program_id(0)
        if plan:
            @pl.when(step == 0)
            def _():
                plan.phases[0](c_in, c_out, *sems)

        core(in_refs, out_refs, scr)
        if plan:
            @pl.when(step == steps - 1)
            def _():
                plan.phases[1](c_in, c_out, *sems)
                plan.phases[2](c_in, c_out, *sems)

    return _pcall(
        body, name=name, grid=(steps,), in_specs=list(in_specs) + [ANY] * n_cin,
        out_specs=list(out_specs) + [ANY] * n_cout, out_shape=list(out_shapes) + list(c_outs),
        scratch_shapes=list(scratch) + (plan.scratch() if plan else []),
        compiler_params=_params(("arbitrary",)))(*ins, *c_ins)


def _sb_fwd(qkv, d, plan=None):
    t = qkv.shape[0]
    blk = _sb_block(t)
    nq = t // blk
    npair = d // LANES
    scale = HEAD_DIM ** -0.5

    def body(in_refs, out_refs, scr):
        (q_ref, k_ref, v_ref), (o_ref, l_ref) = in_refs, out_refs
        tri_s = scr[0]
        hi_s, lo_s, w_s, zs_s = (scr[1 + 4 * k:5 + 4 * k] for k in range(4))
        lane = lax.broadcasted_iota(jnp.int32, (blk, LANES), 1)
        head0 = lane < HEAD_DIM
        row = lax.broadcasted_iota(jnp.int32, (blk, blk), 0)
        col = lax.broadcasted_iota(jnp.int32, (blk, blk), 1)
        tri_s[...] = (row > col).astype(BF16)
        strips = _sb_strips(blk)

        def step(qhs, kbs, maskeds, carries):
            chains = [(bi, hh) for bi in range(len(kbs)) for hh in range(2)]
            starts = [pl.multiple_of(kb * blk, blk) for kb in kbs]
            kvs = [k_ref[pl.ds(start, blk), :] for start in starts]
            vvs = [v_ref[pl.ds(start, blk), :] for start in starts]
            zs = [_dot(qhs[hh], kvs[bi], "nt") for bi, hh in chains]
            sums = []
            for c, (bi, hh) in enumerate(chains):
                parts = []
                for rows, causal in strips:
                    zt = zs[c][rows, :]
                    sp = _softplus(zt)
                    lk = jnp.where(causal, -sp, 0.0) if maskeds[bi] else -sp
                    hi = lk.astype(BF16)
                    hi_s[c][rows, :] = hi
                    lo_s[c][rows, :] = (lk - hi.astype(F32)).astype(BF16)
                    zs_s[c][rows, :] = zt - sp
                    parts.append(jnp.sum(lk, axis=1, keepdims=True))
                sums.append(jnp.concatenate(parts, axis=0))
            laters = [_dot(hi_s[c][...], tri_s[...], "nn") + _dot(lo_s[c][...], tri_s[...], "nn") for c in range(len(chains))]
            for c, (bi, hh) in enumerate(chains):
                cl = carries[hh][0]
                if bi == 1:
                    cl = cl + sums[hh]
                for rows, causal in strips:
                    logw = zs_s[c][rows, :] + laters[c][rows, :] + cl[rows, :]
                    if maskeds[bi]:
                        logw = jnp.where(causal, logw, -1e30)
                    w_s[c][rows, :] = jnp.exp(logw).astype(BF16)
            new = [list(carries[hh]) for hh in range(2)]
            for c, (bi, hh) in enumerate(chains):
                new[hh] = [new[hh][0] + sums[c], new[hh][1] + _dot(w_s[c][...], vvs[bi], "nn")]
            return tuple(tuple(cr) for cr in new)

        def qblock(qi, _):
            qstart = pl.multiple_of(qi * blk, blk)
            qv = q_ref[pl.ds(qstart, blk), :] * scale
            qhs = [jnp.where(head0 if hh == 0 else ~head0, qv, jnp.zeros_like(qv)) for hh in range(2)]
            zero = (jnp.zeros((blk, 1), F32), jnp.zeros((blk, LANES), F32))
            outs = lax.cond(qi % 2 == 1,
                            lambda crs: step(qhs, [qi, qi - 1], [True, False], crs),
                            lambda crs: step(qhs, [qi], [True], crs), (zero, zero))
            top = qi - 1 - qi % 2
            outs = lax.fori_loop(0, qi // 2, lambda j, crs: step(qhs, [top - 2 * j, top - 2 * j - 1], [False, False], crs),
                                 outs)
            o_ref[pl.ds(qstart, blk), :] = jnp.where(head0, outs[0][1], outs[1][1]).astype(o_ref.dtype)
            l_ref[pl.ds(qstart, blk), :] = jnp.where(head0, outs[0][0], outs[1][0])
            return 0

        lax.fori_loop(0, nq, qblock, 0)

    tile_bf16, tile_f32 = pltpu.VMEM((blk, blk), BF16), pltpu.VMEM((blk, blk), F32)
    return _host_call(
        body, "sb_fwd", npair, [qkv, qkv, qkv],
        [pl.BlockSpec((t, LANES), lambda p: (0, p)), pl.BlockSpec((t, LANES), lambda p: (0, npair + p)),
         pl.BlockSpec((t, LANES), lambda p: (0, 2 * npair + p))],
        [SDS((t, d), BF16), SDS((t, d), F32)],
        [pl.BlockSpec((t, LANES), lambda p: (0, p)), pl.BlockSpec((t, LANES), lambda p: (0, p))],
        [tile_bf16] * 13 + [tile_f32] * 4, plan)


def _sb_bwd(qkv, do, ltot, d, plan=None):
    t = qkv.shape[0]
    blk = _sb_block(t)
    nq = t // blk
    npair = d // LANES
    scale = HEAD_DIM ** -0.5

    def body(in_refs, out_refs, scr):
        (q_ref, k_ref, v_ref, do_ref, l_ref), (out_ref,) = in_refs, out_refs
        dq_s, dk_s, dv_s, upto_s, before_s = scr[:5]
        hi_s, lo_s, w_s, dab_s, dzs_s, zs_s, da_s = (scr[5 + 4 * k:9 + 4 * k] for k in range(7))
        lane = lax.broadcasted_iota(jnp.int32, (blk, LANES), 1)
        head0 = lane < HEAD_DIM
        row = lax.broadcasted_iota(jnp.int32, (blk, blk), 0)
        col = lax.broadcasted_iota(jnp.int32, (blk, blk), 1)
        upto_s[...] = (row <= col).astype(BF16)
        before_s[...] = (row < col).astype(BF16)
        dk_s[...] = jnp.zeros_like(dk_s)
        dv_s[...] = jnp.zeros_like(dv_s)
        strips = _sb_strips(blk)

        def step(heads, kbs, maskeds, carries):
            chains = [(bi, hh) for bi in range(len(kbs)) for hh in range(2)]
            starts = [pl.multiple_of(kb * blk, blk) for kb in kbs]
            kvs = [k_ref[pl.ds(start, blk), :] for start in starts]
            vvs = [v_ref[pl.ds(start, blk), :] for start in starts]
            zs = [_dot(heads[hh][0], kvs[bi], "nt") for bi, hh in chains]
            dws = [_dot(heads[hh][1], vvs[bi], "nt") for bi, hh in chains]
            lk_sums, da_sums = [], []
            for c, (bi, hh) in enumerate(chains):
                parts = []
                for rows, causal in strips:
                    zt = zs[c][rows, :]
                    sp = _softplus(zt)
                    lk = jnp.where(causal, -sp, 0.0) if maskeds[bi] else -sp
                    hi = lk.astype(BF16)
                    hi_s[c][rows, :] = hi
                    lo_s[c][rows, :] = (lk - hi.astype(F32)).astype(BF16)
                    zs_s[c][rows, :] = zt - sp
                    parts.append(jnp.sum(lk, axis=1, keepdims=True))
                lk_sums.append(jnp.concatenate(parts, axis=0))
            cums = [_dot(hi_s[c][...], upto_s[...], "nn") + _dot(lo_s[c][...], upto_s[...], "nn") for c in range(len(chains))]
            for c, (bi, hh) in enumerate(chains):
                lt, plk = heads[hh][2], carries[hh][0]
                if bi == 1:
                    plk = plk + lk_sums[hh]
                parts = []
                for rows, causal in strips:
                    logw = zs_s[c][rows, :] + (lt[rows, :] - (plk[rows, :] + cums[c][rows, :]))
                    if maskeds[bi]:
                        logw = jnp.where(causal, logw, -1e30)
                    w = jnp.exp(logw)
                    w_s[c][rows, :] = w.astype(BF16)
                    da = dws[c][rows, :] * w
                    da_s[c][rows, :] = da
                    dab_s[c][rows, :] = da.astype(BF16)
                    parts.append(jnp.sum(da, axis=1, keepdims=True))
                da_sums.append(jnp.concatenate(parts, axis=0))
            pres = [_dot(dab_s[c][...], before_s[...], "nn") for c in range(len(chains))]
            for c, (bi, hh) in enumerate(chains):
                pda = carries[hh][1]
                if bi == 1:
                    pda = pda + da_sums[hh]
                for rows, causal in strips:
                    sig = jnp.exp(zs_s[c][rows, :])
                    da = da_s[c][rows, :]
                    dz = da * (1.0 - sig) - sig * (pda[rows, :] + pres[c][rows, :])
                    if maskeds[bi]:
                        dz = jnp.where(causal, dz, 0.0)
                    dzs_s[c][rows, :] = dz.astype(BF16)
            new = [list(carries[hh]) for hh in range(2)]
            for c, (bi, hh) in enumerate(chains):
                dk_s[kbs[bi]] += _dot(heads[hh][3], dzs_s[c][...], "nn")
                dv_s[kbs[bi]] += _dot(heads[hh][4], w_s[c][...], "nn")
                new[hh] = [new[hh][0] + lk_sums[c], new[hh][1] + da_sums[c], new[hh][2] + _dot(dzs_s[c][...], kvs[bi], "nn")]
            return tuple(tuple(cr) for cr in new)

        def qblock(qi, _):
            qstart = pl.multiple_of(qi * blk, blk)
            qv = q_ref[pl.ds(qstart, blk), :] * scale
            dov = do_ref[pl.ds(qstart, blk), :]
            lv = l_ref[pl.ds(qstart, blk), :]
            heads = []
            for hh in range(2):
                sel = head0 if hh == 0 else ~head0
                qh, doh = jnp.where(sel, qv, jnp.zeros_like(qv)), jnp.where(sel, dov, jnp.zeros_like(dov))
                heads.append((qh, doh, jnp.max(jnp.where(sel, lv, -jnp.inf), axis=1, keepdims=True),
                              qh.astype(F32).T.astype(BF16), doh.astype(F32).T.astype(BF16)))
            zero = (jnp.zeros((blk, 1), F32), jnp.zeros((blk, 1), F32), jnp.zeros((blk, LANES), F32))
            carries = lax.fori_loop(0, qi // 2, lambda j, crs: step(heads, [2 * j, 2 * j + 1], [False, False], crs),
                                    (zero, zero))
            carries = lax.cond(qi % 2 == 1,
                               lambda crs: step(heads, [qi - 1, qi], [False, True], crs),
                               lambda crs: step(heads, [qi], [True], crs), carries)
            dq_s[pl.ds(qstart, blk), :] = jnp.where(head0, carries[0][2], carries[1][2]) * scale
            return 0

        lax.fori_loop(0, nq, qblock, 0)
        out_ref[0] = dq_s[...].astype(out_ref.dtype)
        for b in range(nq):
            out_ref[1, b * blk:(b + 1) * blk, :] = dk_s[b].T.astype(out_ref.dtype)
            out_ref[2, b * blk:(b + 1) * blk, :] = dv_s[b].T.astype(out_ref.dtype)

    col_blk = lambda off: pl.BlockSpec((t, LANES), lambda p: (0, off + p))
    return _host_call(
        body, "sb_bwd", npair, [qkv, qkv, qkv, do, ltot],
        [col_blk(0), col_blk(npair), col_blk(2 * npair), col_blk(0), col_blk(0)],
        [SDS((3, t, d), BF16)], [pl.BlockSpec((3, t, LANES), lambda p: (0, 0, p))],
        [pltpu.VMEM((t, LANES), F32)] + [pltpu.VMEM((nq, LANES, blk), F32) for _ in range(2)]
        + [pltpu.VMEM((blk, blk), BF16) for _ in range(2 + 20)]
        + [pltpu.VMEM((blk, blk), F32) for _ in range(8)], plan)


def _roll_rows(v, shift):
    return pltpu.roll(v, shift, 0)


def _shift_down(v, dist, fill, row):
    return jnp.where(row >= dist, _roll_rows(v, dist), fill)


def _shift_up(v, dist, fill, row):
    t = v.shape[0]
    return jnp.where(row < t - dist, _roll_rows(v, t - dist), fill)


def _lru_gates(xb, small, wr, wi, row):
    xs = [_shift_down(xb, 3 - tap, 0.0, row) if tap < 3 else xb for tap in range(4)]
    xc = small[4:5, :] + xs[0] * small[0:1, :]
    for tap in range(1, 4):
        xc = xc + xs[tap] * small[tap:tap + 1, :]
    xcb = xc.astype(BF16)
    r = _sigmoid(_dot(xcb, wr, "nn") + small[5:6, :])
    ig = _sigmoid(_dot(xcb, wi, "nn") + small[6:7, :])
    sp = _softplus(-small[7:8, :])
    la = -LRU_C * r * sp
    a = jnp.exp(la)
    th = jnp.tanh(la)
    m2 = -2.0 * th / (1.0 - th)
    return xs, xc, xcb, r, ig, sp, a, (jnp.sqrt(m2), m2)


def _gelu_parts(gate):
    inner = GELU_C * (gate + GELU_K * gate * gate * gate)
    th = jnp.tanh(inner)
    gelu = 0.5 * gate * (1.0 + th)
    dgelu = 0.5 * (1.0 + th) + 0.5 * gate * (1.0 - th * th) * GELU_C * (1.0 + 3.0 * GELU_K * gate * gate)
    return gelu, dgelu


def _scan_steps(t):
    steps, dist = [], 1
    while dist < t:
        steps.append(dist)
        dist *= 2
    return steps


SUBLANES = 8


def _linear_scan(a, b, scratch, row, reverse):
    a_s, b_s, carry_s = scratch
    t = a.shape[0]
    groups = t // SUBLANES
    in_group = row & (SUBLANES - 1)
    for dist in _scan_steps(SUBLANES):
        if reverse:
            inside = in_group < SUBLANES - dist
            b = b + a * jnp.where(inside, _roll_rows(b, t - dist), 0.0)
            a = a * jnp.where(inside, _roll_rows(a, t - dist), 1.0)
        else:
            inside = in_group >= dist
            b = a * jnp.where(inside, _roll_rows(b, dist), 0.0) + b
            a = a * jnp.where(inside, _roll_rows(a, dist), 1.0)
    a_s[...] = a
    b_s[...] = b
    end = 0 if reverse else SUBLANES - 1
    ends = pl.ds(end, groups, stride=SUBLANES)
    ae, be = a_s[ends, :], b_s[ends, :]
    grow = lax.broadcasted_iota(jnp.int32, ae.shape, 0)
    shift = _shift_up if reverse else _shift_down
    for dist in _scan_steps(groups):
        be = ae * shift(be, dist, 0.0, grow) + be
        ae = ae * shift(ae, dist, 1.0, grow)
    incoming = shift(be, 1, 0.0, grow)
    for k in range(SUBLANES):
        carry_s[pl.ds(k, groups, stride=SUBLANES), :] = incoming
    return a_s[...] * carry_s[...] + b_s[...]


def _lru_fwd(gx, small, wr, wi):
    t = gx.shape[0]
    r_dim = gx.shape[1] // 2
    nb = r_dim // LRU_BLOCK_W

    def body(gate_ref, xb_ref, small_ref, wr_ref, wi_ref, y_ref, hs_ref, *scratch):
        row = lax.broadcasted_iota(jnp.int32, (t, LRU_BLOCK_W), 0)
        xb = xb_ref[...]
        _, xc, _, _, ig, _, a, (mult, _) = _lru_gates(xb, small_ref, wr_ref[...], wi_ref[...], row)
        hsv = _linear_scan(a, mult * (ig * xc), scratch, row, reverse=False)
        hs_ref[...] = hsv
        gelu, _ = _gelu_parts(gate_ref[...])
        y_ref[...] = (gelu * hsv).astype(y_ref.dtype)

    colb = lambda off: pl.BlockSpec((t, LRU_BLOCK_W), lambda n: (0, off + n))
    wspec = pl.BlockSpec((None, LRU_BLOCK_W, LRU_BLOCK_W), lambda n: (n, 0, 0))
    return _pcall(
        body, name="lru_fwd", grid=(nb,),
        in_specs=[colb(0), colb(nb), pl.BlockSpec((8, LRU_BLOCK_W), lambda n: (0, n)), wspec, wspec],
        out_specs=[colb(0), colb(0)], out_shape=[SDS((t, r_dim), BF16), SDS((t, r_dim), F32)],
        scratch_shapes=[pltpu.VMEM((t, LRU_BLOCK_W), F32) for _ in range(3)],
        compiler_params=_params(("parallel",)))(gx, gx, small, wr, wi)


def _lru_bwd(gx, hs, dy, small, wr, wi, plan=None):
    t = gx.shape[0]
    r_dim = gx.shape[1] // 2
    nb = r_dim // LRU_BLOCK_W

    def body(in_refs, out_refs, scratch):
        (gate_ref, xb_ref, hs_ref, dy_ref, small_ref, wr_ref, wi_ref), (dgx_ref, dsm_ref, dwr_ref, dwi_ref) = in_refs, out_refs
        row = lax.broadcasted_iota(jnp.int32, (t, LRU_BLOCK_W), 0)
        xb, hsv, dyv, smallv = xb_ref[...], hs_ref[...], dy_ref[...], small_ref
        wrv, wiv = wr_ref[...], wi_ref[...]
        xs, xc, xcb, r, ig, sp, a, (mult, m2) = _lru_gates(xb, smallv, wrv, wiv, row)
        gelu, dgelu = _gelu_parts(gate_ref[...])
        dgx_ref[0] = (dyv * hsv * dgelu).astype(dgx_ref.dtype)
        dacc = _linear_scan(_shift_up(a, 1, 1.0, row), dyv * gelu, scratch, row, reverse=True)
        da = dacc * _shift_down(hsv, 1, 0.0, row)
        dmult = dacc * (ig * xc)
        dixc = dacc * mult
        dla = da * a - dmult * (a * a) * lax.rsqrt(m2)
        dr = dla * (-LRU_C * sp)
        dsp = jnp.sum(dla * (-LRU_C * r), axis=0, keepdims=True)
        dpr = dr * r * (1.0 - r)
        dpi = dixc * xc * ig * (1.0 - ig)
        dprb, dpib = dpr.astype(BF16), dpi.astype(BF16)
        dwr_ref[...] = _dot(xcb, dprb, "tn")
        dwi_ref[...] = _dot(xcb, dpib, "tn")
        dxc = dixc * ig + _dot(dprb, wrv, "nt") + _dot(dpib, wiv, "nt")
        dxb = dxc * smallv[3:4, :]
        for tap in range(3):
            dxb = dxb + _shift_up(dxc, 3 - tap, 0.0, row) * smallv[tap:tap + 1, :]
        dgx_ref[1] = dxb.astype(dgx_ref.dtype)
        lam = smallv[7:8, :]
        rows = [jnp.sum(dxc * xs[tap], axis=0, keepdims=True) for tap in range(4)]
        rows.append(jnp.sum(dxc, axis=0, keepdims=True))
        rows.append(jnp.sum(dpr, axis=0, keepdims=True))
        rows.append(jnp.sum(dpi, axis=0, keepdims=True))
        rows.append(-dsp * _sigmoid(-lam))
        for k, rv in enumerate(rows):
            dsm_ref[k:k + 1, :] = rv

    colb = lambda off: pl.BlockSpec((t, LRU_BLOCK_W), lambda n: (0, off + n))
    wspec = pl.BlockSpec((None, LRU_BLOCK_W, LRU_BLOCK_W), lambda n: (n, 0, 0))
    sspec = pl.BlockSpec((8, LRU_BLOCK_W), lambda n: (0, n))
    return _host_call(
        body, "lru_bwd", nb, [gx, gx, hs, dy, small, wr, wi],
        [colb(0), colb(nb), colb(0), colb(0), sspec, wspec, wspec],
        [SDS((2, t, r_dim), BF16), SDS((8, r_dim), F32), SDS((nb, LRU_BLOCK_W, LRU_BLOCK_W), F32),
         SDS((nb, LRU_BLOCK_W, LRU_BLOCK_W), F32)],
        [pl.BlockSpec((2, t, LRU_BLOCK_W), lambda n: (0, 0, n)), sspec, wspec, wspec],
        [pltpu.VMEM((t, LRU_BLOCK_W), F32) for _ in range(3)], plan)


def _adam(w, g, m, v):
    m2 = ADAM_B1 * m + (1.0 - ADAM_B1) * g
    v2 = ADAM_B2 * v + (1.0 - ADAM_B2) * (g * g)
    m_hat = m2 / (1.0 - ADAM_B1 ** ADAM_STEP)
    v_hat = v2 / (1.0 - ADAM_B2 ** ADAM_STEP)
    return -ADAM_LR * (m_hat / (jnp.sqrt(v_hat) + ADAM_EPS) + ADAM_WD * w), m2, v2


def _mod_fwd(c_all, mod_w, mod_b_cols):
    nl, d, cols = mod_w.shape
    nbatch = c_all.shape[0]

    def body(c_ref, w_ref, b_ref, o_ref):
        cv = c_ref[...]
        ca = (cv * _sigmoid(cv)).astype(BF16)
        o_ref[...] = _dot(ca, w_ref[...].astype(BF16), "nn") + b_ref[...]

    return _pcall(
        body, name="mod_fwd", grid=(nl,),
        in_specs=[pl.BlockSpec((nbatch, d), lambda l: (0, 0)), pl.BlockSpec((None, d, cols), lambda l: (l, 0, 0)),
                  pl.BlockSpec((None, 1, cols), lambda l: (l, 0, 0))],
        out_specs=pl.BlockSpec((None, nbatch, cols), lambda l: (l, 0, 0)), out_shape=SDS((nl, nbatch, cols), F32),
        compiler_params=_params(("parallel",)))(c_all, mod_w, mod_b_cols)


def _mod_w_update(c_all, dmod_cols, w, m, v):
    nl, d, cols = w.shape
    nbatch = c_all.shape[0]
    tr = _tile(d, (256, 128))

    def body(c_ref, dm_ref, w_ref, m_ref, v_ref, g_ref, dl_ref, m2_ref, v2_ref):
        cv = c_ref[...]
        ca = (cv * _sigmoid(cv)).astype(BF16)
        g = _dot(ca, dm_ref[...].astype(BF16), "tn")
        g_ref[...] = g
        dl_ref[...], m2_ref[...], v2_ref[...] = _adam(w_ref[...], g, m_ref[...], v_ref[...])

    wblk = pl.BlockSpec((None, tr, cols), lambda l, i: (l, i, 0))
    return _pcall(
        body, name="mod_w_update", grid=(nl, d // tr),
        in_specs=[pl.BlockSpec((nbatch, tr), lambda l, i: (0, i)), pl.BlockSpec((None, nbatch, cols), lambda l, i: (l, 0, 0)),
                  wblk, wblk, wblk],
        out_specs=[wblk] * 4, out_shape=[SDS(w.shape, F32)] * 4,
        compiler_params=_params(("parallel", "parallel")))(c_all, dmod_cols, w, m, v)


def _adam_update(name, w, m, v, gparts):
    rows, cols = w.shape
    tr = _tile(rows, (256, 128, 64, 32, 16, 8))
    npart = len(gparts)

    def body(*refs):
        w_ref, m_ref, v_ref = refs[:3]
        g_refs = refs[3:3 + npart]
        g_ref, dl_ref, m2_ref, v2_ref = refs[3 + npart:]
        g = g_refs[0][...].astype(F32)
        for gr in g_refs[1:]:
            g = g + gr[...].astype(F32)
        g_ref[...] = g
        dl_ref[...], m2_ref[...], v2_ref[...] = _adam(w_ref[...], g, m_ref[...], v_ref[...])

    blk = pl.BlockSpec((tr, cols), lambda i: (i, 0))
    return _pcall(body, name=name, grid=(rows // tr,), in_specs=[blk] * (3 + npart), out_specs=[blk] * 4,
                  out_shape=[SDS((rows, cols), F32)] * 4, compiler_params=_params(("parallel",)))(w, m, v, *gparts)


def _adam_shard(name, w, m, v, part4, recv3, chip_idx, first=0, fills=None):
    p, r, cdim = w.shape
    pg = part4.shape[0]
    tr = _tile(r, (256, 176, 160, 128, 64, 32, 16))

    def body(chip_ref, w_ref, m_ref, v_ref, own_ref, r0_ref, r1_ref, r2_ref, *rest):
        g_ref, dl_ref, m2_ref, v2_ref = rest[-4:]
        g = own_ref[...].astype(F32) + r0_ref[...].astype(F32) + r1_ref[...].astype(F32) + r2_ref[...].astype(F32)
        g_ref[...] = g
        dl_ref[...], m2_ref[...], v2_ref[...] = _adam(w_ref[...], g, m_ref[...], v_ref[...])

    blk = pl.BlockSpec((None, tr, cdim), lambda q, i, chip_ref: (first + q, i, 0))
    blk4 = (None, None, tr, cdim)
    slot = lambda s: pl.BlockSpec(blk4, lambda q, i, chip_ref: (s, q, i, 0))
    fills = list(fills or [])
    grid_spec = pltpu.PrefetchScalarGridSpec(
        num_scalar_prefetch=1, grid=(pg, r // tr),
        in_specs=[blk, blk, blk, pl.BlockSpec(blk4, lambda q, i, chip_ref: (q, chip_ref[0], i, 0)), slot(0), slot(1), slot(2)]
        + [ANY] * len(fills),
        out_specs=[blk] * 4)
    return _pcall(body, name=name, grid_spec=grid_spec, out_shape=[SDS((p, r, cdim), F32)] * 4,
                  input_output_aliases={8 + k: k for k in range(len(fills))},
                  compiler_params=_params(("parallel", "parallel")))(chip_idx, w, m, v, part4, recv3, recv3, recv3, *fills)


def _sum_devices(gathered, name):
    _, rows, cols = gathered.shape
    tr = _tile(rows, (512, 256, 128, 64, 32, 16, 8))

    def body(g_ref, o_ref):
        acc = g_ref[0].astype(F32)
        for k in range(1, N_DEV):
            acc = acc + g_ref[k].astype(F32)
        o_ref[...] = acc

    return _pcall(body, name=name, grid=(rows // tr,), in_specs=[pl.BlockSpec((N_DEV, tr, cols), lambda i: (0, i, 0))],
                  out_specs=pl.BlockSpec((tr, cols), lambda i: (i, 0)), out_shape=SDS((rows, cols), F32),
                  compiler_params=_params(("parallel",)))(gathered)


def _pack_flat(parts, width, row_mult, dtype):
    flat = jnp.concatenate([p.reshape(-1).astype(dtype) for p in parts])
    unit = width * row_mult
    pad = (-flat.shape[0]) % unit
    if pad:
        flat = jnp.concatenate([flat, jnp.zeros((pad,), dtype)])
    return flat.reshape(-1, width)


def _unpack_flat(flat, shapes):
    out, off = [], 0
    for shp in shapes:
        size = math.prod(shp)
        out.append(flat[off:off + size].reshape(shp))
        off += size
    return out


def kernel(x, c, mod_w, mod_b, norm_g, ffn_w_gu, ffn_w_down, sb_w_qkv, sb_w_o, lru_w_in, lru_conv_w, lru_conv_b, lru_w_r, lru_b_r, lru_w_i, lru_b_i, lru_lambda, lru_w_out, final_norm_g, loss_target, m_mod_w, m_mod_b, m_norm_g, m_ffn_w_gu, m_ffn_w_down, m_sb_w_qkv, m_sb_w_o, m_lru_w_in, m_lru_conv_w, m_lru_conv_b, m_lru_w_r, m_lru_b_r, m_lru_w_i, m_lru_b_i, m_lru_lambda, m_lru_w_out, m_final_norm_g, v_mod_w, v_mod_b, v_norm_g, v_ffn_w_gu, v_ffn_w_down, v_sb_w_qkv, v_sb_w_o, v_lru_w_in, v_lru_conv_w, v_lru_conv_b, v_lru_w_r, v_lru_b_r, v_lru_w_i, v_lru_b_i, v_lru_lambda, v_lru_w_out, v_final_norm_g):
    weights = dict(mod_w=mod_w, mod_b=mod_b, norm_g=norm_g, ffn_w_gu=ffn_w_gu, ffn_w_down=ffn_w_down, sb_w_qkv=sb_w_qkv,
                   sb_w_o=sb_w_o, lru_w_in=lru_w_in, lru_conv_w=lru_conv_w, lru_conv_b=lru_conv_b, lru_w_r=lru_w_r,
                   lru_b_r=lru_b_r, lru_w_i=lru_w_i, lru_b_i=lru_b_i, lru_lambda=lru_lambda, lru_w_out=lru_w_out,
                   final_norm_g=final_norm_g)
    mom_m = dict(mod_w=m_mod_w, mod_b=m_mod_b, norm_g=m_norm_g, ffn_w_gu=m_ffn_w_gu, ffn_w_down=m_ffn_w_down,
                 sb_w_qkv=m_sb_w_qkv, sb_w_o=m_sb_w_o, lru_w_in=m_lru_w_in, lru_conv_w=m_lru_conv_w,
                 lru_conv_b=m_lru_conv_b, lru_w_r=m_lru_w_r, lru_b_r=m_lru_b_r, lru_w_i=m_lru_w_i, lru_b_i=m_lru_b_i,
                 lru_lambda=m_lru_lambda, lru_w_out=m_lru_w_out, final_norm_g=m_final_norm_g)
    mom_v = dict(mod_w=v_mod_w, mod_b=v_mod_b, norm_g=v_norm_g, ffn_w_gu=v_ffn_w_gu, ffn_w_down=v_ffn_w_down,
                 sb_w_qkv=v_sb_w_qkv, sb_w_o=v_sb_w_o, lru_w_in=v_lru_w_in, lru_conv_w=v_lru_conv_w,
                 lru_conv_b=v_lru_conv_b, lru_w_r=v_lru_w_r, lru_b_r=v_lru_b_r, lru_w_i=v_lru_w_i, lru_b_i=v_lru_b_i,
                 lru_lambda=v_lru_lambda, lru_w_out=v_lru_w_out, final_norm_g=v_final_norm_g)
    names = list(weights)

    t, d = x.shape[1], x.shape[2]
    n_layers = mod_w.shape[0]
    r_dim = lru_w_out.shape[1] * N_DEV
    ng, rs = d // N_DEV, r_dim // N_DEV
    mod_cols = mod_w.shape[2]
    nblk = lru_w_r.shape[1]
    xi, yi, ci = _mesh_pos()
    me = 4 * xi + 2 * yi + ci
    chip = 2 * xi + yi
    x2, target = x.reshape(t, d), loss_target.reshape(t, d)

    lru_small_shard = jnp.concatenate([lru_conv_w[0], lru_conv_b, lru_b_r, lru_b_i, lru_lambda], axis=0)
    small1 = _pack_flat([c, norm_g, lru_small_shard], LANES, 8, F32)
    n_small1 = small1.shape[0]
    all1 = _allgather(small1[None], "gather_small").reshape(N_DEV, n_small1 * LANES)
    c_all = all1[:, :d]
    norm_full = jnp.transpose(all1[:, d:d + 6 * ng].reshape(N_DEV, n_layers, 3, ng), (1, 2, 0, 3)).reshape(n_layers, 3, d)
    lru_small = jnp.transpose(all1[:, d + 6 * ng:d + 6 * ng + 8 * rs].reshape(N_DEV, 8, rs), (1, 0, 2)).reshape(8, r_dim)

    mod_b_cols = lax.dynamic_slice_in_dim(mod_b, me * mod_cols, mod_cols, axis=1).reshape(n_layers, 1, mod_cols)
    mod_part = _mod_fwd(c_all, mod_w, mod_b_cols)

    assert sb_w_qkv.shape[0] == 1 and lru_w_in.shape[0] == 1, "one stick-breaking and one RG-LRU layer"
    n_ffn = 2 * n_layers
    fc = ffn_w_gu.shape[3]
    cw_in = lru_w_in.shape[2]
    pieces = {("ffn_w_gu", q): ffn_w_gu[q // 2, q % 2][None] for q in range(n_ffn)}
    pieces.update({("ffn_w_down", q): ffn_w_down[q // 2, q % 2][None] for q in range(n_ffn)})
    pieces.update({("sb_w_qkv", 0): sb_w_qkv, ("sb_w_o", 0): sb_w_o, ("lru_w_in", 0): lru_w_in, ("lru_w_out", 0): lru_w_out})
    col_window = {("sb_w_qkv", 0)}
    first = [("ffn_w_gu", 0)]
    behind = {"l0s0_gu": [("ffn_w_down", 0)], "l0s0_down": [("sb_w_qkv", 0), ("sb_w_o", 0)],
              "l0s2_gu": [("ffn_w_down", n_ffn - 1)], "l0s2_down": [("ffn_w_down", 2)],
              "l1s0_gu": [("lru_w_in", 0)], "l1s0_down": [("lru_w_out", 0)]}
    behind["sb_fwd"] = [key for key in pieces if key not in first + sum(behind.values(), [])]
    gathered = {}

    def gather_plan(keys):
        return _gather_plan([pieces[key].astype(BF16) for key in keys], [key in col_window for key in keys])

    def hosting(name, call):
        keys = behind.get(name, [])
        outs = call(gather_plan(keys) if keys else None)
        gathered.update(zip(keys, outs[len(outs) - len(keys):]))
        return outs[:len(outs) - len(keys)]

    mod_all, *landed = _run_comm(_merge_plans([_gather_plan([mod_part], [False]), gather_plan(first)]), "gather_mod_and_first")
    gathered.update(zip(first, landed))
    mod_mine = lax.dynamic_index_in_dim(mod_all, me, axis=2, keepdims=False)
    mod_mine = mod_mine.reshape(n_layers, 3, 3, d)
    wr_b, wi_b = lru_w_r[0].astype(BF16), lru_w_i[0].astype(BF16)
    eye2 = jnp.eye(2 * cw_in, dtype=BF16).reshape(2, cw_in, 2 * cw_in)

    def w_gu(q):
        return gathered[("ffn_w_gu", q)]

    def w_d4(q):
        return gathered[("ffn_w_down", q)].reshape(1, HIDDEN_CHUNKS, fc, d)

    saved = []
    xcur = x2
    for layer in range(n_layers):
        for sub in range(3):
            gvec = norm_full[layer, sub].reshape(1, d)
            shift = mod_mine[layer, sub, 0].reshape(1, d)
            scale1p = 1.0 + mod_mine[layer, sub, 1].reshape(1, d)
            gmul = 1.0 + mod_mine[layer, sub, 2].reshape(1, d)
            tag = f"l{layer}s{sub}"
            h = _norm_fwd(xcur, gvec, scale1p, shift, tag + "_norm")
            rec = dict(x=xcur, h=h, g=gvec, scale1p=scale1p, gmul=gmul, w=MACARON_W if sub != 1 else 1.0)
            if sub != 1:
                lj = layer * 2 + sub // 2
                gu2, a = hosting(tag + "_gu", lambda plan: _ffn_gu(tag + "_gu", h, w_gu(lj), plan))
                yv, xcur = hosting(tag + "_down", lambda plan: _ffn_down(tag + "_down", a, w_d4(lj), xcur, gmul, plan))
                rec.update(kind="ffn", lj=lj, gu2=gu2, a=a, y=yv)
            elif layer % 2 == 0:
                w_qkv = gathered[("sb_w_qkv", 0)][0]
                w_o = gathered[("sb_w_o", 0)].reshape(d, d)
                qkv = _mm_nn(tag + "_qkv", h, w_qkv, BF16)[0]
                o, ltot = hosting("sb_fwd", lambda plan: _sb_fwd(qkv, d, plan))
                yv, xcur = _mm_nn(tag + "_wo", o, w_o, [BF16, F32], extras=[(xcur, "tile"), (gmul, "row")],
                                  epilogue=lambda accs, ex: (accs[0], ex[0] + ex[1] * accs[0]))
                rec.update(kind="sb", qkv=qkv, o=o, ltot=ltot, y=yv, w_qkv=w_qkv, w_o=w_o)
            else:
                w_in = _chunks_to_cols("lru_w_in_cols", gathered[("lru_w_in", 0)][0], eye2)
                w_out = gathered[("lru_w_out", 0)].reshape(r_dim, d)
                gx = _mm_nn(tag + "_win", h, w_in, F32)[0]
                ymix, hs = _lru_fwd(gx, lru_small, wr_b, wi_b)
                yv, xcur = _mm_nn(tag + "_wout", ymix, w_out, [BF16, F32], extras=[(xcur, "tile"), (gmul, "row")],
                                  epilogue=lambda accs, ex: (accs[0], ex[0] + ex[1] * accs[0]))
                rec.update(kind="lru", gx=gx, hs=hs, ymix=ymix, y=yv, w_in=w_in, w_out=w_out)
            saved.append(rec)

    last = saved[-1]
    dxo, dy, head_sums = _loss_head(xcur, target, final_norm_g.reshape(1, d), (last["w"] * last["gmul"]))
    loss_mine = head_sums[1, 0:1]
    dgf = head_sums[0]

    c_idx = jnp.reshape(ci, (1,)).astype(jnp.int32)
    chip_idx = jnp.reshape(chip, (1,)).astype(jnp.int32)
    grads, reduced = {}, {}
    to_pair = []
    to_chips = []

    def sibling_plan(only=None):
        keys = [key for key in to_pair if only is None or key in only]
        if not keys:
            return None, keys
        return _exchange_plan([grads[key] for key in keys], [key in col_window for key in keys], 4, _sibling_route), keys

    def sibling_done(keys, recv4):
        for key, r4 in zip(keys, recv4):
            to_pair.remove(key)
            to_chips.append((key, _pair_sum(grads[key], r4, c_idx, f"rs_pair_sum_{key[0]}{key[1]}", cols=key in col_window)))

    def chip_plan(only=None):
        items = [item for item in to_chips if only is None or item[0] in only]
        if not items:
            return None, items
        return _exchange_plan([p4 for _, p4 in items], [False] * len(items), 3, _chip_route), items

    def chips_done(items, recv3):
        for item, r3 in zip(items, recv3):
            to_chips.remove(item)
            reduced[item[0]] = (item[1], r3)

    def behind(call, make_plan, done, more=None):
        plan, items = make_plan()
        n_mine = len(plan.outs) if plan else 0
        n_more = len(more.outs) if more else 0
        outs = call(_merge_plans([plan, more]))
        n_own = len(outs) - n_mine - n_more
        done(items, outs[n_own:n_own + n_mine])
        return list(outs[:n_own]) + list(outs[n_own + n_mine:])

    carried = {
        "l1s1b_dymix": ("sibling", None), "lru_bwd": ("chips", [("ffn_w_gu", n_ffn - 1)]),
        "l1s0b_da": ("sibling", None), "l1s0b_dwgu": ("chips", [("ffn_w_down", n_ffn - 1)]),
        "l1s0b_dh": ("chips", [("lru_w_out", 0), ("lru_w_in", 0)]),
        "l0s2b_da": ("sibling", None), "l0s2b_dwgu": ("chips", [("ffn_w_down", n_ffn - 2)]),
    }

    def carrying(name, call):
        if name not in carried:
            return call(None)
        stage, only = carried[name]
        if stage == "sibling":
            return behind(call, functools.partial(sibling_plan, only), sibling_done)
        return behind(call, functools.partial(chip_plan, only), chips_done)

    def at_once(make_plan, done, name):
        plan, items = make_plan()
        if plan:
            done(items, _run_comm(plan, name))

    def add_grad(key, value):
        grads[key] = value
        to_pair.append(key)

    dmod = [[None] * 3 for _ in range(n_layers)]
    dnorm = [[None] * 3 for _ in range(n_layers)]
    dlru_small = wri_all = None
    for idx in reversed(range(len(saved))):
        rec = saved[idx]
        layer, sub = divmod(idx, 3)
        tag = f"l{layer}s{sub}b"
        if rec["kind"] == "ffn" and idx > 0:
            lj = rec["lj"]
            (dgu2,) = carrying(tag + "_da", lambda plan: _ffn_da(tag + "_da", dy, w_d4(lj), rec["gu2"], plan))
            dwd = _ffn_dwd(tag + "_dwd", rec["a"], dy)[0].reshape(gathered[("ffn_w_down", lj)].shape)
            (dwgu,) = carrying(tag + "_dwgu", lambda plan: _ffn_dwgu(tag + "_dwgu", rec["h"], dgu2, plan))
            (dh,) = carrying(tag + "_dh", lambda plan: _ffn_dh(tag + "_dh", dgu2, w_gu(lj), plan))
            add_grad(("ffn_w_down", lj), dwd)
            add_grad(("ffn_w_gu", lj), dwgu)
        elif rec["kind"] == "ffn":
            lj = rec["lj"]
            at_once(sibling_plan, sibling_done, "rs_sibling_" + tag)
            (dgu2,) = behind(lambda plan: _ffn_da(tag + "_da", dy, w_d4(lj), rec["gu2"], plan), chip_plan, chips_done)
            add_grad(("ffn_w_gu", lj), _ffn_dwgu(tag + "_dwgu", rec["h"], dgu2)[0])
            at_once(sibling_plan, sibling_done, "rs_sibling_" + tag + "_dwgu")
            (dh,) = behind(lambda plan: _ffn_dh(tag + "_dh", dgu2, w_gu(lj), plan), chip_plan, chips_done)
            add_grad(("ffn_w_down", lj), _ffn_dwd(tag + "_dwd", rec["a"], dy)[0].reshape(gathered[("ffn_w_down", lj)].shape))
            at_once(sibling_plan, sibling_done, "rs_sibling_" + tag + "_dwd")
        elif rec["kind"] == "sb":
            at_once(sibling_plan, sibling_done, "rs_sibling_" + tag)
            do = _mm_nt(tag + "_do", dy, rec["w_o"], BF16)
            dwo = _mm_tn(tag + "_dwo", rec["o"], dy, BF16)
            wri = _pack_flat([dwr, dwi], LANES, 512, BF16)[None]
            dqkv3, wri_all = behind(lambda plan: _sb_bwd(rec["qkv"], do, rec["ltot"], d, plan), chip_plan, chips_done,
                                    more=_gather_plan([wri], [False]))
            add_grad(("sb_w_o", 0), dwo.reshape(gathered[("sb_w_o", 0)].shape))
            dh = _mm_nt_stack(tag + "_dh", dqkv3, rec["w_qkv"], F32)
            add_grad(("sb_w_qkv", 0), _mm_tn_stack(tag + "_dwqkv", rec["h"], dqkv3, BF16)[None])
        else:
            (dymix,) = carrying(tag + "_dymix", lambda plan: _mm_nt(tag + "_dymix", dy, rec["w_out"], F32, plan)
                                if plan else [_mm_nt(tag + "_dymix", dy, rec["w_out"], F32)])
            dwout = _mm_tn(tag + "_dwout", rec["ymix"], dy, BF16).reshape(gathered[("lru_w_out", 0)].shape)
            dgx2, dlru_small, dwr, dwi = carrying("lru_bwd", lambda plan: _lru_bwd(rec["gx"], rec["hs"], dymix, lru_small,
                                                                                wr_b, wi_b, plan))
            dh = _mm_nt_stack(tag + "_dh", dgx2, rec["w_in"], F32)
            dw_in = _mm_tn_stack(tag + "_dwin", rec["h"], dgx2, BF16)
            add_grad(("lru_w_out", 0), dwout)
            add_grad(("lru_w_in", 0), _cols_to_chunks("lru_w_in_chunks", dw_in, eye2)[None])
        prev = saved[idx - 1] if idx > 0 else None
        gw_prev = (prev["w"] * prev["gmul"]) if prev is not None else jnp.zeros((1, d), F32)
        dxo, dy, sums = behind(lambda plan: _adaln_bwd(dh, rec["x"], rec["y"], dxo, rec["g"], rec["scale1p"], rec["w"], gw_prev,
                                                       tag + "_adaln", plan),
                               chip_plan if idx == 0 else (lambda: (None, [])), chips_done)
        dmod[layer][sub] = sums[0:3]
        dnorm[layer][sub] = sums[3]
    grad_x = dxo.reshape(x.shape)

    dmod_mine = jnp.stack([jnp.stack(dmod[layer]) for layer in range(n_layers)])
    dnorm_mine = jnp.stack([jnp.stack(dnorm[layer]) for layer in range(n_layers)])
    assert not to_pair and not to_chips
    small_shapes = [(n_layers, 9 * d), (n_layers, 3, d), (8, r_dim), (d,), (1,)]
    small3 = _pack_flat([dmod_mine, dnorm_mine, dlru_small, dgf, loss_mine], LANES, 256, F32)
    n_small3 = small3.shape[0]
    all3 = _allgather(small3[None], "gather_small_grads").reshape(N_DEV, n_small3, LANES)
    gsum = _sum_devices(all3, "sum_small_grads").reshape(-1)
    g_mod_b, g_norm_full, g_lru_small, g_final, loss_sum = _unpack_flat(gsum, small_shapes)
    loss = loss_sum[0]
    wri_sum = _sum_devices(wri_all.reshape(N_DEV, -1, LANES), "sum_gate_weight_grads").reshape(-1)
    g_wr, g_wi = _unpack_flat(wri_sum, [lru_w_r.shape, lru_w_i.shape])
    dmod_all = all3.reshape(N_DEV, -1)[:, :n_layers * 9 * d].reshape(N_DEV, n_layers, N_DEV, mod_cols)
    dmod_cols = jnp.transpose(lax.dynamic_index_in_dim(dmod_all, me, axis=2, keepdims=False), (1, 0, 2))

    out_g, out_d, out_m, out_v = {}, {}, {}, {}
    out_g["mod_w"], out_d["mod_w"], out_m["mod_w"], out_v["mod_w"] = _mod_w_update(c_all, dmod_cols, mod_w, m_mod_w, v_mod_w)

    g_norm_shard = lax.dynamic_slice_in_dim(g_norm_full, me * ng, ng, axis=2)
    g_lru_shard = lax.dynamic_slice_in_dim(g_lru_small, me * rs, rs, axis=1)
    small_grads = dict(mod_b=g_mod_b, norm_g=g_norm_shard, lru_conv_w=g_lru_shard[0:4].reshape(lru_conv_w.shape),
                       lru_conv_b=g_lru_shard[4:5], lru_b_r=g_lru_shard[5:6], lru_b_i=g_lru_shard[6:7],
                       lru_lambda=g_lru_shard[7:8], final_norm_g=g_final)
    for n, g in (("lru_w_r", g_wr), ("lru_w_i", g_wi)):
        view = lambda arr: arr.reshape(-1, LRU_BLOCK_W)
        outs = _adam_update("adam_" + n, view(weights[n]), view(mom_m[n]), view(mom_v[n]), [view(g)])
        out_g[n], out_d[n], out_m[n], out_v[n] = [o.reshape(weights[n].shape) for o in outs]
    small_names = list(small_grads)
    sw = _pack_flat([weights[n] for n in small_names], LANES, 256, F32)
    sg = _pack_flat([small_grads[n] for n in small_names], LANES, 256, F32)
    sm = _pack_flat([mom_m[n] for n in small_names], LANES, 256, F32)
    sv = _pack_flat([mom_v[n] for n in small_names], LANES, 256, F32)
    s_outs = _adam_update("adam_small", sw, sm, sv, [sg])
    small_shapes2 = [weights[n].shape for n in small_names]
    for dst, flat in zip((out_g, out_d, out_m, out_v), s_outs):
        for n, arr in zip(small_names, _unpack_flat(flat.reshape(-1), small_shapes2)):
            dst[n] = arr

    for n in ["ffn_w_gu", "ffn_w_down", "sb_w_qkv", "sb_w_o", "lru_w_in", "lru_w_out"]:
        shp = weights[n].shape
        shard3 = (math.prod(shp[:-2]),) + shp[-2:]
        view = lambda arr: arr.reshape(shard3)
        outs = None
        for q in range(shard3[0]):
            fills = outs if outs is not None else [lax.empty(shard3, F32) for _ in range(4)]
            p4, r3 = reduced[(n, q)]
            outs = _adam_shard(f"adam_{n}{q}", view(weights[n]), view(mom_m[n]), view(mom_v[n]), p4, r3, chip_idx,
                               first=q, fills=fills if shard3[0] > 1 else None)
        out_g[n], out_d[n], out_m[n], out_v[n] = [o.reshape(shp) for o in outs]

    return (loss, grad_x, *[out_g[n] for n in names], *[out_d[n] for n in names], *[out_m[n] for n in names],
            *[out_v[n] for n in names])
```

```python
import functools
import math

import jax
import jax.numpy as jnp
from jax import lax
from jax.experimental import pallas as pl
from jax.experimental.pallas import tpu as pltpu

F32 = jnp.float32
BF16 = jnp.bfloat16
SDS = jax.ShapeDtypeStruct
MESH = pl.DeviceIdType.MESH
ANY = pl.BlockSpec(memory_space=pl.ANY)

N_DEV = 8
LANES = 128
HEAD_DIM = 64
LRU_BLOCK_W = 128
LRU_C = 8.0
MACARON_W = 0.5
NORM_EPS = 1e-6
ADAM_LR = 0.001
ADAM_B1 = 0.9
ADAM_B2 = 0.999
ADAM_EPS = 1e-08
ADAM_WD = 0.01
ADAM_STEP = 10
VMEM_LIMIT = 56 * 1024 * 1024
GELU_C = math.sqrt(2.0 / math.pi)
GELU_K = 0.044715

DIMS = {
    "nn": (((1,), (0,)), ((), ())),
    "nt": (((1,), (1,)), ((), ())),
    "tn": (((0,), (0,)), ((), ())),
}


def _pcall(body, **kw):
    return pl.pallas_call(body, **kw)


def _params(sem=None):
    return pltpu.CompilerParams(dimension_semantics=sem, vmem_limit_bytes=VMEM_LIMIT)


def _tile(n, prefs):
    for p in prefs:
        if n % p == 0:
            return p
    return n


def _dot(a, b, dims):
    return lax.dot_general(a, b, DIMS[dims], preferred_element_type=F32)


def _softplus(z):
    return jnp.maximum(z, 0.0) + jnp.log(1.0 + jnp.exp(-jnp.abs(z)))


def _sigmoid(z):
    return 0.5 * jnp.tanh(0.5 * z) + 0.5


def _mesh_pos():
    return lax.axis_index("x"), lax.axis_index("y"), lax.axis_index("c")


def _allgather(xs, name, cols=False):
    return _run_comm(_gather_plan([xs], [cols]), name)[0]


class _CommPlan:
    def __init__(self, ins, outs, n_remote, n_local, phases):
        self.ins, self.outs, self.n_remote, self.n_local, self.phases = ins, outs, n_remote, n_local, phases

    def scratch(self):
        return [pltpu.SemaphoreType.DMA((self.n_remote,)), pltpu.SemaphoreType.DMA((self.n_remote,)),
                pltpu.SemaphoreType.DMA((max(self.n_local, 1),))]


def _merge_plans(plans):
    plans = [p for p in plans if p is not None]
    if len(plans) <= 1:
        return plans[0] if plans else None

    def phase(k):
        def run(in_refs, out_refs, send_sems, recv_sems, local_sems, r0=0, l0=0):
            i0 = o0 = 0
            for p in plans:
                p.phases[k](in_refs[i0:i0 + len(p.ins)], out_refs[o0:o0 + len(p.outs)], send_sems, recv_sems, local_sems, r0, l0)
                i0, o0, r0, l0 = i0 + len(p.ins), o0 + len(p.outs), r0 + p.n_remote, l0 + p.n_local
        return run

    return _CommPlan(sum([p.ins for p in plans], []), sum([p.outs for p in plans], []), sum(p.n_remote for p in plans),
                     sum(p.n_local for p in plans), [phase(0), phase(1), phase(2)])


def _run_comm(plan, name):
    n_in, n_out = len(plan.ins), len(plan.outs)

    def body(*refs):
        in_refs, out_refs, sems = refs[:n_in], refs[n_in:n_in + n_out], refs[n_in + n_out:]
        for phase in plan.phases:
            phase(in_refs, out_refs, *sems)

    return _pcall(body, name=name, out_shape=plan.outs, in_specs=[ANY] * n_in, out_specs=[ANY] * n_out,
                  scratch_shapes=plan.scratch())(*plan.ins)


def _col_window(ref, idx, width):
    return ref.at[:, :, pl.ds(pl.multiple_of(idx * width, math.gcd(width, LANES)), width)]


def _gather_plan(shards, cols):
    n = len(shards)
    outs = [SDS((s.shape[0], s.shape[1], N_DEV * s.shape[2]) if cl else (s.shape[0], N_DEV) + s.shape[1:], s.dtype)
            for s, cl in zip(shards, cols)]

    def copies(a, in_refs, out_refs, send_sems, recv_sems, local_sems, r0=0, l0=0):
        x, y, c = _mesh_pos()
        sibling = (x, y, 1 - c)
        chips = [(1 - x, y), (x, 1 - y), (1 - x, 1 - y)]
        width = shards[a].shape[2]

        def block(px, py, pc):
            idx = 4 * px + 2 * py + pc
            return _col_window(out_refs[a], idx, width) if cols[a] else out_refs[a].at[:, idx]

        def copy(k, owner, to, src=None):
            sem = r0 + 7 * a + k
            return pltpu.make_async_remote_copy(
                src_ref=block(*owner) if src is None else src, dst_ref=block(*owner),
                send_sem=send_sems.at[sem], recv_sem=recv_sems.at[sem], device_id=to, device_id_type=MESH)

        me = (x, y, c)
        first = [copy(0, me, sibling, src=in_refs[a])]
        first += [copy(1 + j, me, (*chip, c), src=in_refs[a]) for j, chip in enumerate(chips)]
        passed = [copy(4 + j, (*chip, c), sibling) for j, chip in enumerate(chips)]
        landed = [copy(1 + j, (*chip, c), me) for j, chip in enumerate(chips)]
        from_sibling = [copy(0, sibling, me)] + [copy(4 + j, (*chip, 1 - c), me) for j, chip in enumerate(chips)]
        mine = pltpu.make_async_copy(in_refs[a], block(*me), local_sems.at[l0 + a])
        return first, passed, landed, from_sibling, mine

    def start(*refs):
        for a in range(n):
            first, _, _, _, mine = copies(a, *refs)
            mine.start()
            for cp in first:
                cp.start()

    def pass_on(*refs):
        for a in range(n):
            _, passed, landed, _, _ = copies(a, *refs)
            for cp, fwd in zip(landed, passed):
                cp.wait_recv()
                fwd.start()

    def finish(*refs):
        for a in range(n):
            first, passed, _, from_sibling, mine = copies(a, *refs)
            for cp in from_sibling:
                cp.wait_recv()
            for cp in first + passed:
                cp.wait_send()
            mine.wait()

    return _CommPlan(list(shards), outs, 7 * n, n, [start, pass_on, finish])


def _exchange_plan(srcs, cols, n_slots, route):
    n = len(srcs)
    outs = []
    for g, cl in zip(srcs, cols):
        shard = (g.shape[0], g.shape[1], g.shape[2] // N_DEV) if cl else (g.shape[0],) + g.shape[2:]
        outs.append(SDS((n_slots,) + shard, g.dtype))

    def copies(in_refs, out_refs, send_sems, recv_sems, local_sems, r0=0, l0=0):
        x, y, c = _mesh_pos()
        made = []
        for a in range(n):
            for s in range(n_slots):
                chunk, target = route(x, y, c, s)
                src = _col_window(in_refs[a], chunk, outs[a].shape[3]) if cols[a] else in_refs[a].at[:, chunk]
                sem = r0 + a * n_slots + s
                made.append(pltpu.make_async_remote_copy(
                    src_ref=src, dst_ref=out_refs[a].at[s], send_sem=send_sems.at[sem], recv_sem=recv_sems.at[sem],
                    device_id=target, device_id_type=MESH))
        return made

    def start(*refs):
        for cp in copies(*refs):
            cp.start()

    def nothing(*refs):
        pass

    def finish(*refs):
        made = copies(*refs)
        for cp in made:
            cp.wait_recv()
        for cp in made:
            cp.wait_send()

    return _CommPlan(list(srcs), outs, n * n_slots, 0, [start, nothing, finish])


def _sibling_route(x, y, c, k):
    return 2 * k + 1 - c, (x, y, 1 - c)


def _chip_route(x, y, c, j):
    px, py = [(1 - x, y), (x, 1 - y), (1 - x, 1 - y)][j]
    return 2 * px + py, (px, py, c)


def _pair_sum(grads, recv4, c_idx, name, cols=False):
    _, p, r, cdim = recv4.shape
    tr = _tile(r, (512, 256, 176, 160, 128, 64, 32, 16))

    def body(c_ref, a_ref, b_ref, o_ref):
        o_ref[...] = (a_ref[...].astype(F32) + b_ref[...].astype(F32)).astype(o_ref.dtype)

    blk = (None, None, tr, cdim)
    if cols:
        own = pl.BlockSpec((None, tr, cdim), lambda k, q, i, c_ref: (q, i, 2 * k + c_ref[0]))
    else:
        own = pl.BlockSpec(blk, lambda k, q, i, c_ref: (q, 2 * k + c_ref[0], i, 0))
    grid_spec = pltpu.PrefetchScalarGridSpec(
        num_scalar_prefetch=1, grid=(4, p, r // tr),
        in_specs=[own, pl.BlockSpec(blk, lambda k, q, i, c_ref: (k, q, i, 0))],
        out_specs=pl.BlockSpec(blk, lambda k, q, i, c_ref: (q, k, i, 0)))
    return _pcall(body, name=name, grid_spec=grid_spec, out_shape=SDS((p, 4, r, cdim), grads.dtype),
                  compiler_params=_params(("parallel", "parallel", "parallel")))(c_idx, grads, recv4)


def _mm(name, ins, prods, n_acc, acc_shape, epi_idx, epilogue, out_shapes, out_specs, grid, dims, plan=None):
    n_in, n_out, nk = len(ins), len(out_shapes), grid[2]
    n_acc_refs = n_acc if nk > 1 else 0
    c_ins, c_outs = (plan.ins, plan.outs) if plan else ([], [])
    n_cin, n_cout = len(c_ins), len(c_outs)

    def body(*refs):
        in_refs, c_in = refs[:n_in], refs[n_in:n_in + n_cin]
        rest = refs[n_in + n_cin:]
        out_refs, c_out = rest[:n_out], rest[n_out:n_out + n_cout]
        rest = rest[n_out + n_cout:]
        acc_refs, sems = rest[:n_acc_refs], rest[n_acc_refs:]
        ids = [pl.program_id(axis) for axis in range(3)]
        if plan:
            @pl.when((ids[0] == 0) & (ids[1] == 0) & (ids[2] == 0))
            def _():
                plan.phases[0](c_in, c_out, *sems)

        def finish(accs):
            outs = epilogue(accs, [in_refs[i][...] for i in epi_idx])
            for o_ref, o in zip(out_refs, outs):
                if isinstance(o, tuple):
                    for plane, part in enumerate(o):
                        o_ref[plane] = part.astype(o_ref.dtype)
                else:
                    o_ref[...] = o.astype(o_ref.dtype)

        def operand(ref_idx):
            if isinstance(ref_idx, tuple):
                return in_refs[ref_idx[0]][ref_idx[1]]
            return in_refs[ref_idx][...]

        if nk == 1:
            accs = [None] * n_acc
            for ia, ib, iacc in prods:
                term = _dot(operand(ia), operand(ib), dims)
                accs[iacc] = term if accs[iacc] is None else accs[iacc] + term
            finish(accs)
        else:
            @pl.when(ids[2] == 0)
            def _():
                for acc in acc_refs:
                    acc[...] = jnp.zeros_like(acc)

            for ia, ib, iacc in prods:
                acc_refs[iacc][...] += _dot(operand(ia), operand(ib), dims)

            @pl.when(ids[2] == nk - 1)
            def _():
                finish([acc[...] for acc in acc_refs])

        if plan:
            @pl.when((ids[0] == grid[0] - 1) & (ids[1] == grid[1] - 1) & (ids[2] == nk - 1))
            def _():
                plan.phases[1](c_in, c_out, *sems)
                plan.phases[2](c_in, c_out, *sems)

    return _pcall(
        body, name=name, grid=grid, in_specs=[s for _, s in ins] + [ANY] * n_cin,
        out_specs=list(out_specs) + [ANY] * n_cout, out_shape=list(out_shapes) + list(c_outs),
        scratch_shapes=[pltpu.VMEM(acc_shape, F32) for _ in range(n_acc_refs)] + (plan.scratch() if plan else []),
        compiler_params=_params(("arbitrary",) * 3 if plan else ("parallel", "parallel", "arbitrary")),
    )(*[a for a, _ in ins], *c_ins)


def _plain(accs, _):
    return accs


def _mm_nn(name, a, b, out_dtype, extras=(), epilogue=_plain, n_out=1):
    m, kd = a.shape
    n = b.shape[1]
    tm, tn, tk = _tile(m, (1024, 512, 256, 128)), _tile(n, (640, 512, 256, 128)), _tile(kd, (1280, 1024, 512, 256, 128))
    ins = [(a, pl.BlockSpec((tm, tk), lambda i, j, k: (i, k))), (b, pl.BlockSpec((tk, tn), lambda i, j, k: (k, j)))]
    for arr, kind in extras:
        if kind == "tile":
            ins.append((arr, pl.BlockSpec((tm, tn), lambda i, j, k: (i, j))))
        else:
            ins.append((arr, pl.BlockSpec((1, tn), lambda i, j, k: (0, j))))
    dts = out_dtype if isinstance(out_dtype, (list, tuple)) else [out_dtype] * n_out
    return _mm(name, ins, [(0, 1, 0)], 1, (tm, tn), list(range(2, len(ins))), epilogue,
               [SDS((m, n), dt) for dt in dts], [pl.BlockSpec((tm, tn), lambda i, j, k: (i, j)) for _ in dts],
               (m // tm, n // tn, kd // tk), "nn")


def _mm_nt(name, a, b, out_dtype, plan=None):
    m, kd = a.shape
    n = b.shape[0]
    tm, tn, tk = _tile(m, (1024, 512, 256, 128)), _tile(n, (640, 512, 256, 128)), _tile(kd, (1024, 512, 256, 128))
    ins = [(a, pl.BlockSpec((tm, tk), lambda i, j, k: (i, k))), (b, pl.BlockSpec((tn, tk), lambda i, j, k: (j, k)))]
    outs = _mm(name, ins, [(0, 1, 0)], 1, (tm, tn), [], _plain, [SDS((m, n), out_dtype)],
               [pl.BlockSpec((tm, tn), lambda i, j, k: (i, j))], (m // tm, n // tn, kd // tk), "nt", plan=plan)
    return outs if plan else outs[0]


def _mm_tn(name, a, b, out_dtype):
    t, m = a.shape
    n = b.shape[1]
    tm, tn, tk = _tile(m, (640, 512, 256, 128)), _tile(n, (1024, 512, 256, 128)), t
    ins = [(a, pl.BlockSpec((tk, tm), lambda i, j, k: (k, i))), (b, pl.BlockSpec((tk, tn), lambda i, j, k: (k, j)))]
    return _mm(name, ins, [(0, 1, 0)], 1, (tm, tn), [], _plain, [SDS((m, n), out_dtype)],
               [pl.BlockSpec((tm, tn), lambda i, j, k: (i, j))], (m // tm, n // tn, t // tk), "tn")[0]


def _mm_nt_stack(name, a3, b, out_dtype):
    cc, m, kd = a3.shape
    n = b.shape[0]
    tm, tn, tk = _tile(m, (1024, 512, 256, 128)), _tile(n, (1024, 512, 256, 128)), _tile(kd, (1280, 1024, 512, 256, 128))
    nk = kd // tk
    ins = [(a3, pl.BlockSpec((None, tm, tk), lambda i, j, k: (k // nk, i, k % nk))),
           (b, pl.BlockSpec((tn, tk), lambda i, j, k: (j, k)))]
    return _mm(name, ins, [(0, 1, 0)], 1, (tm, tn), [], _plain, [SDS((m, n), out_dtype)],
               [pl.BlockSpec((tm, tn), lambda i, j, k: (i, j))], (m // tm, n // tn, cc * nk), "nt")[0]


def _mm_tn_stack(name, a, b3, out_dtype):
    t, m = a.shape
    cc, _, n = b3.shape
    tm, tn, tk = _tile(m, (512, 256, 128)), _tile(n, (1280, 1024, 512, 256, 128)), t
    nj = n // tn
    ins = [(a, pl.BlockSpec((tk, tm), lambda i, j, k: (k, i))),
           (b3, pl.BlockSpec((None, tk, tn), lambda i, j, k: (j // nj, k, j % nj)))]
    return _mm(name, ins, [(0, 1, 0)], 1, (tm, tn), [], _plain, [SDS((m, cc * n), out_dtype)],
               [pl.BlockSpec((tm, tn), lambda i, j, k: (i, j))], (m // tm, cc * nj, t // tk), "tn")[0]


def _chunks_to_cols(name, wc, eye2):
    nch, d, cw = wc.shape
    tm = _tile(d, (1024, 512, 256, 128))
    ins = [(wc, pl.BlockSpec((None, tm, cw), lambda i, j, k: (2 * j + k, i, 0))),
           (eye2, pl.BlockSpec((None, cw, 2 * cw), lambda i, j, k: (k, 0, 0)))]
    return _mm(name, ins, [(0, 1, 0)], 1, (tm, 2 * cw), [], _plain, [SDS((d, nch * cw), wc.dtype)],
               [pl.BlockSpec((tm, 2 * cw), lambda i, j, k: (i, j))], (d // tm, nch // 2, 2), "nn")[0]


def _cols_to_chunks(name, full, eye2):
    d, n = full.shape
    _, cw, _ = eye2.shape
    nch = n // cw
    tm = _tile(d, (1024, 512, 256, 128))
    ins = [(full, pl.BlockSpec((tm, 2 * cw), lambda i, j, k: (i, j // 2))),
           (eye2, pl.BlockSpec((None, cw, 2 * cw), lambda i, j, k: (j % 2, 0, 0)))]
    return _mm(name, ins, [(0, 1, 0)], 1, (tm, cw), [], _plain, [SDS((nch, d, cw), full.dtype)],
               [pl.BlockSpec((None, tm, cw), lambda i, j, k: (j, i, 0))], (d // tm, nch, 1), "nt")[0]


def _row_tile(t):
    return _tile(t, (256, 128, 64, 32, 16, 8))


def _norm_fwd(x, g, scale1p, shift, name):
    t, d = x.shape
    tr = _row_tile(t)

    def body(x_ref, g_ref, s_ref, b_ref, h_ref):
        xv = x_ref[...]
        inv = lax.rsqrt(jnp.mean(xv * xv, axis=-1, keepdims=True) + NORM_EPS)
        h_ref[...] = ((xv * inv) * g_ref[...] * s_ref[...] + b_ref[...]).astype(h_ref.dtype)

    vec = pl.BlockSpec((1, d), lambda i: (0, 0))
    return _pcall(body, name=name, grid=(t // tr,), in_specs=[pl.BlockSpec((tr, d), lambda i: (i, 0)), vec, vec, vec],
                  out_specs=pl.BlockSpec((tr, d), lambda i: (i, 0)), out_shape=SDS((t, d), BF16),
                  compiler_params=_params(("parallel",)))(x, g, scale1p, shift)


def _adaln_bwd(dh, x, y, dxo, g, scale1p, w_sub, gw_prev, name, plan=None):
    t, d = x.shape
    tr = _row_tile(t)

    def body(in_refs, out_refs, _):
        (dh_ref, x_ref, y_ref, dxo_ref, g_ref, s_ref, gw_ref), (dx_ref, dyp_ref, sums_ref) = in_refs, out_refs
        i = pl.program_id(0)

        @pl.when(i == 0)
        def _():
            sums_ref[...] = jnp.zeros_like(sums_ref)

        xv, dhv, dxov = x_ref[...], dh_ref[...], dxo_ref[...]
        inv = lax.rsqrt(jnp.mean(xv * xv, axis=-1, keepdims=True) + NORM_EPS)
        xn = xv * inv
        gv = g_ref[...]
        dn = dhv * s_ref[...]
        dxn = dn * gv
        dx = inv * (dxn - xn * jnp.mean(dxn * xn, axis=-1, keepdims=True)) + dxov
        dx_ref[...] = dx
        dyp_ref[...] = (gw_ref[...] * dx).astype(dyp_ref.dtype)
        sums_ref[0:1, :] += jnp.sum(dhv, axis=0, keepdims=True)
        sums_ref[1:2, :] += jnp.sum(dhv * (xn * gv), axis=0, keepdims=True)
        sums_ref[2:3, :] += jnp.sum(w_sub * y_ref[...] * dxov, axis=0, keepdims=True)
        sums_ref[3:4, :] += jnp.sum(dn * xn, axis=0, keepdims=True)

    blk = pl.BlockSpec((tr, d), lambda i: (i, 0))
    vec = pl.BlockSpec((1, d), lambda i: (0, 0))
    return _host_call(
        body, name, t // tr, [dh, x, y, dxo, g, scale1p, gw_prev], [blk, blk, blk, blk, vec, vec, vec],
        [SDS((t, d), F32), SDS((t, d), BF16), SDS((8, d), F32)], [blk, blk, pl.BlockSpec((8, d), lambda i: (0, 0))], [], plan)


def _loss_head(x, target, gf, gw_prev):
    t, d = x.shape
    tr = _row_tile(t)
    nt = t // tr

    def body(x_ref, tg_ref, g_ref, gw_ref, dx_ref, dyp_ref, sums_ref):
        i = pl.program_id(0)

        @pl.when(i == 0)
        def _():
            sums_ref[...] = jnp.zeros_like(sums_ref)

        xv = x_ref[...]
        inv = lax.rsqrt(jnp.mean(xv * xv, axis=-1, keepdims=True) + NORM_EPS)
        xn = xv * inv
        gv = g_ref[...]
        err = xn * gv - tg_ref[...]
        dyv = err * (1.0 / d)
        dxn = dyv * gv
        dx = inv * (dxn - xn * jnp.mean(dxn * xn, axis=-1, keepdims=True))
        dx_ref[...] = dx
        dyp_ref[...] = (gw_ref[...] * dx).astype(dyp_ref.dtype)
        sums_ref[0:1, :] += jnp.sum(dyv * xn, axis=0, keepdims=True)
        sums_ref[1:2, :] += jnp.sum(err * err, axis=0, keepdims=True)

        @pl.when(i == nt - 1)
        def _():
            tot = jnp.sum(sums_ref[1:2, :], axis=1, keepdims=True) * (0.5 / d)
            sums_ref[1:2, :] = jnp.broadcast_to(tot, (1, d))

    blk = pl.BlockSpec((tr, d), lambda i: (i, 0))
    vec = pl.BlockSpec((1, d), lambda i: (0, 0))
    return _pcall(
        body, name="loss_head", grid=(nt,), in_specs=[blk, blk, vec, vec],
        out_specs=[blk, blk, pl.BlockSpec((8, d), lambda i: (0, 0))],
        out_shape=[SDS((t, d), F32), SDS((t, d), BF16), SDS((8, d), F32)],
        compiler_params=_params(("arbitrary",)))(x, target, gf, gw_prev)


HIDDEN_CHUNKS = N_DEV // 2


def _ffn_tiles(t, d):
    return _tile(t, (1024, 512, 256, 128)), _tile(d, (1024, 512, 256, 128))


def _ffn_gu(name, h, wgu, plan=None):
    t, d = h.shape
    fc, nc = wgu.shape[3], HIDDEN_CHUNKS
    tm, _ = _ffn_tiles(t, d)

    def epi_gu(accs, _):
        gpre, up = accs
        return (gpre, up), gpre * _sigmoid(gpre) * up

    wblk = (None, None, d, fc)
    ins = [(h, pl.BlockSpec((tm, d), lambda i, c, k: (i, 0))),
           (wgu, pl.BlockSpec(wblk, lambda i, c, k: (0, c, 0, 0))),
           (wgu, pl.BlockSpec(wblk, lambda i, c, k: (0, c + nc, 0, 0)))]
    return _mm(name, ins, [(0, 1, 0), (0, 2, 1)], 2, (tm, fc), [], epi_gu,
               [SDS((2, nc, t, fc), BF16), SDS((nc, t, fc), BF16)],
               [pl.BlockSpec((2, None, tm, fc), lambda i, c, k: (0, c, i, 0)),
                pl.BlockSpec((None, tm, fc), lambda i, c, k: (c, i, 0))],
               (t // tm, nc, 1), "nn", plan=plan)


def _ffn_down(name, a, wd4, x, gmul, plan=None):
    nc, t, fc = a.shape
    d = wd4.shape[3]
    tm = _tile(t, (512, 256, 128))

    def epi_down(accs, ex):
        (yv,), (xv, gm) = accs, ex
        return yv, xv + MACARON_W * gm * yv

    ins = [(a, pl.BlockSpec((nc, tm, fc), lambda i, j, k: (0, i, 0))),
           (wd4, pl.BlockSpec((None, nc, fc, d), lambda i, j, k: (0, 0, 0, 0), pipeline_mode=pl.Buffered(1))),
           (x, pl.BlockSpec((tm, d), lambda i, j, k: (i, 0))), (gmul, pl.BlockSpec((1, d), lambda i, j, k: (0, 0)))]
    oblk = pl.BlockSpec((tm, d), lambda i, j, k: (i, 0))
    prods = [((0, (c,)), (1, (c,)), 0) for c in range(nc)]
    return _mm(name, ins, prods, 1, (tm, d), [2, 3], epi_down, [SDS((t, d), BF16), SDS((t, d), F32)],
               [oblk, oblk], (t // tm, 1, 1), "nn", plan=plan)


def _ffn_da(name, dy, wd4, gu2, plan=None):
    t, d = dy.shape
    _, nc, fc, _ = wd4.shape
    tm, _ = _ffn_tiles(t, d)

    def epi_da(accs, ex):
        (da,), (gu,) = accs, ex
        gpre, up = gu[0].astype(F32), gu[1].astype(F32)
        s = _sigmoid(gpre)
        silu = gpre * s
        dg = da * up * (s * (1.0 + gpre * (1.0 - s)))
        return ((dg, da * silu),)

    gblk = pl.BlockSpec((2, None, tm, fc), lambda i, c, k: (0, c, i, 0))
    ins = [(dy, pl.BlockSpec((tm, d), lambda i, c, k: (i, 0))),
           (wd4, pl.BlockSpec((None, None, fc, d), lambda i, c, k: (0, c, 0, 0))), (gu2, gblk)]
    return _mm(name, ins, [(0, 1, 0)], 1, (tm, fc), [2], epi_da, [SDS((2, nc, t, fc), BF16)], [gblk],
               (t // tm, nc, 1), "nt", plan=plan)


def _ffn_dwd(name, a, dy, plan=None):
    nc, t, fc = a.shape
    d = dy.shape[1]
    _, tn = _ffn_tiles(t, d)
    ins = [(a, pl.BlockSpec((None, t, fc), lambda c, j, k: (c, 0, 0))), (dy, pl.BlockSpec((t, tn), lambda c, j, k: (0, j)))]
    return _mm(name, ins, [(0, 1, 0)], 1, (fc, tn), [], _plain, [SDS((1, nc, fc, d), BF16)],
               [pl.BlockSpec((None, None, fc, tn), lambda c, j, k: (0, c, 0, j))], (nc, d // tn, 1), "tn", plan=plan)


def _ffn_dwgu(name, h, dgu2, plan=None):
    t, d = h.shape
    _, nc, _, fc = dgu2.shape
    _, tn = _ffn_tiles(t, d)
    ins = [(h, pl.BlockSpec((t, tn), lambda i, c, k: (0, i))),
           (dgu2, pl.BlockSpec((None, None, t, fc), lambda i, c, k: (c // nc, c % nc, 0, 0)))]
    return _mm(name, ins, [(0, 1, 0)], 1, (tn, fc), [], _plain, [SDS((1, 2 * nc, d, fc), BF16)],
               [pl.BlockSpec((None, None, tn, fc), lambda i, c, k: (0, c, i, 0))], (d // tn, 2 * nc, 1), "tn", plan=plan)


def _ffn_dh(name, dgu2, wgu, plan=None):
    _, nc, t, fc = dgu2.shape
    d = wgu.shape[2]
    tm = _tile(t, (512, 256, 128))
    ins = [(dgu2, pl.BlockSpec((2, nc, tm, fc), lambda i, j, k: (0, 0, i, 0))),
           (wgu, pl.BlockSpec((None, 2 * nc, d, fc), lambda i, j, k: (0, 0, 0, 0), pipeline_mode=pl.Buffered(1)))]
    prods = [((0, (s, c)), (1, (nc * s + c,)), 0) for s in range(2) for c in range(nc)]
    return _mm(name, ins, prods, 1, (tm, d), [], _plain, [SDS((t, d), F32)],
               [pl.BlockSpec((tm, d), lambda i, j, k: (i, 0))], (t // tm, 1, 1), "nt", plan=plan)


def _sb_block(t):
    return 256 if t >= 1024 else 128


SB_STRIP = 64


def _sb_strips(blk):
    strip = min(SB_STRIP, blk)
    row = lax.broadcasted_iota(jnp.int32, (strip, blk), 0)
    col = lax.broadcasted_iota(jnp.int32, (strip, blk), 1)
    return [(slice(r0, r0 + strip), col < row + r0) for r0 in range(0, blk, strip)]


def _host_call(core, name, steps, ins, in_specs, out_shapes, out_specs, scratch, plan):
    n_in, n_out, n_scr = len(ins), len(out_shapes), len(scratch)
    c_ins, c_outs = (plan.ins, plan.outs) if plan else ([], [])
    n_cin, n_cout = len(c_ins), len(c_outs)

    def body(*refs):
        in_refs, c_in = refs[:n_in], refs[n_in:n_in + n_cin]
        rest = refs[n_in + n_cin:]
        out_refs, c_out = rest[:n_out], rest[n_out:n_out + n_cout]
        rest = rest[n_out + n_cout:]
        scr, sems = rest[:n_scr], rest[n_scr:]
        step = pl.program_id(0)
        if plan:
            @pl.when(step == 0)
            def _():
                plan.phases[0](c_in, c_out, *sems)

        core(in_refs, out_refs, scr)
        if plan:
            @pl.when(step == steps - 1)
            def _():
                plan.phases[1](c_in, c_out, *sems)
                plan.phases[2](c_in, c_out, *sems)

    return _pcall(
        body, name=name, grid=(steps,), in_specs=list(in_specs) + [ANY] * n_cin,
        out_specs=list(out_specs) + [ANY] * n_cout, out_shape=list(out_shapes) + list(c_outs),
        scratch_shapes=list(scratch) + (plan.scratch() if plan else []),
        compiler_params=_params(("arbitrary",)))(*ins, *c_ins)


def _sb_fwd(qkv, d, plan=None):
    t = qkv.shape[0]
    blk = _sb_block(t)
    nq = t // blk
    npair = d // LANES
    scale = HEAD_DIM ** -0.5

    def body(in_refs, out_refs, scr):
        (q_ref, k_ref, v_ref), (o_ref, l_ref) = in_refs, out_refs
        tri_s = scr[0]
        hi_s, lo_s, w_s, zs_s = (scr[1 + 4 * k:5 + 4 * k] for k in range(4))
        lane = lax.broadcasted_iota(jnp.int32, (blk, LANES), 1)
        head0 = lane < HEAD_DIM
        row = lax.broadcasted_iota(jnp.int32, (blk, blk), 0)
        col = lax.broadcasted_iota(jnp.int32, (blk, blk), 1)
        tri_s[...] = (row > col).astype(BF16)
        strips = _sb_strips(blk)

        def step(qhs, kbs, maskeds, carries):
            chains = [(bi, hh) for bi in range(len(kbs)) for hh in range(2)]
            starts = [pl.multiple_of(kb * blk, blk) for kb in kbs]
            kvs = [k_ref[pl.ds(start, blk), :] for start in starts]
            vvs = [v_ref[pl.ds(start, blk), :] for start in starts]
            zs = [_dot(qhs[hh], kvs[bi], "nt") for bi, hh in chains]
            sums = []
            for c, (bi, hh) in enumerate(chains):
                parts = []
                for rows, causal in strips:
                    zt = zs[c][rows, :]
                    sp = _softplus(zt)
                    lk = jnp.where(causal, -sp, 0.0) if maskeds[bi] else -sp
                    hi = lk.astype(BF16)
                    hi_s[c][rows, :] = hi
                    lo_s[c][rows, :] = (lk - hi.astype(F32)).astype(BF16)
                    zs_s[c][rows, :] = zt - sp
                    parts.append(jnp.sum(lk, axis=1, keepdims=True))
                sums.append(jnp.concatenate(parts, axis=0))
            laters = [_dot(hi_s[c][...], tri_s[...], "nn") + _dot(lo_s[c][...], tri_s[...], "nn") for c in range(len(chains))]
            for c, (bi, hh) in enumerate(chains):
                cl = carries[hh][0]
                if bi == 1:
                    cl = cl + sums[hh]
                for rows, causal in strips:
                    logw = zs_s[c][rows, :] + laters[c][rows, :] + cl[rows, :]
                    if maskeds[bi]:
                        logw = jnp.where(causal, logw, -1e30)
                    w_s[c][rows, :] = jnp.exp(logw).astype(BF16)
            new = [list(carries[hh]) for hh in range(2)]
            for c, (bi, hh) in enumerate(chains):
                new[hh] = [new[hh][0] + sums[c], new[hh][1] + _dot(w_s[c][...], vvs[bi], "nn")]
            return tuple(tuple(cr) for cr in new)

        def qblock(qi, _):
            qstart = pl.multiple_of(qi * blk, blk)
            qv = q_ref[pl.ds(qstart, blk), :] * scale
            qhs = [jnp.where(head0 if hh == 0 else ~head0, qv, jnp.zeros_like(qv)) for hh in range(2)]
            zero = (jnp.zeros((blk, 1), F32), jnp.zeros((blk, LANES), F32))
            outs = lax.cond(qi % 2 == 1,
                            lambda crs: step(qhs, [qi, qi - 1], [True, False], crs),
                            lambda crs: step(qhs, [qi], [True], crs), (zero, zero))
            top = qi - 1 - qi % 2
            outs = lax.fori_loop(0, qi // 2, lambda j, crs: step(qhs, [top - 2 * j, top - 2 * j - 1], [False, False], crs),
                                 outs)
            o_ref[pl.ds(qstart, blk), :] = jnp.where(head0, outs[0][1], outs[1][1]).astype(o_ref.dtype)
            l_ref[pl.ds(qstart, blk), :] = jnp.where(head0, outs[0][0], outs[1][0])
            return 0

        lax.fori_loop(0, nq, qblock, 0)

    tile_bf16, tile_f32 = pltpu.VMEM((blk, blk), BF16), pltpu.VMEM((blk, blk), F32)
    return _host_call(
        body, "sb_fwd", npair, [qkv, qkv, qkv],
        [pl.BlockSpec((t, LANES), lambda p: (0, p)), pl.BlockSpec((t, LANES), lambda p: (0, npair + p)),
         pl.BlockSpec((t, LANES), lambda p: (0, 2 * npair + p))],
        [SDS((t, d), BF16), SDS((t, d), F32)],
        [pl.BlockSpec((t, LANES), lambda p: (0, p)), pl.BlockSpec((t, LANES), lambda p: (0, p))],
        [tile_bf16] * 13 + [tile_f32] * 4, plan)


def _sb_bwd(qkv, do, ltot, d, plan=None):
    t = qkv.shape[0]
    blk = _sb_block(t)
    nq = t // blk
    npair = d // LANES
    scale = HEAD_DIM ** -0.5

    def body(in_refs, out_refs, scr):
        (q_ref, k_ref, v_ref, do_ref, l_ref), (out_ref,) = in_refs, out_refs
        dq_s, dk_s, dv_s, upto_s, before_s = scr[:5]
        hi_s, lo_s, w_s, dab_s, dzs_s, zs_s, da_s = (scr[5 + 4 * k:9 + 4 * k] for k in range(7))
        lane = lax.broadcasted_iota(jnp.int32, (blk, LANES), 1)
        head0 = lane < HEAD_DIM
        row = lax.broadcasted_iota(jnp.int32, (blk, blk), 0)
        col = lax.broadcasted_iota(jnp.int32, (blk, blk), 1)
        upto_s[...] = (row <= col).astype(BF16)
        before_s[...] = (row < col).astype(BF16)
        dk_s[...] = jnp.zeros_like(dk_s)
        dv_s[...] = jnp.zeros_like(dv_s)
        strips = _sb_strips(blk)

        def step(heads, kbs, maskeds, carries):
            chains = [(bi, hh) for bi in range(len(kbs)) for hh in range(2)]
            starts = [pl.multiple_of(kb * blk, blk) for kb in kbs]
            kvs = [k_ref[pl.ds(start, blk), :] for start in starts]
            vvs = [v_ref[pl.ds(start, blk), :] for start in starts]
            zs = [_dot(heads[hh][0], kvs[bi], "nt") for bi, hh in chains]
            dws = [_dot(heads[hh][1], vvs[bi], "nt") for bi, hh in chains]
            lk_sums, da_sums = [], []
            for c, (bi, hh) in enumerate(chains):
                parts = []
                for rows, causal in strips:
                    zt = zs[c][rows, :]
                    sp = _softplus(zt)
                    lk = jnp.where(causal, -sp, 0.0) if maskeds[bi] else -sp
                    hi = lk.astype(BF16)
                    hi_s[c][rows, :] = hi
                    lo_s[c][rows, :] = (lk - hi.astype(F32)).astype(BF16)
                    zs_s[c][rows, :] = zt - sp
                    parts.append(jnp.sum(lk, axis=1, keepdims=True))
                lk_sums.append(jnp.concatenate(parts, axis=0))
            cums = [_dot(hi_s[c][...], upto_s[...], "nn") + _dot(lo_s[c][...], upto_s[...], "nn") for c in range(len(chains))]
            for c, (bi, hh) in enumerate(chains):
                lt, plk = heads[hh][2], carries[hh][0]
                if bi == 1:
                    plk = plk + lk_sums[hh]
                parts = []
                for rows, causal in strips:
                    logw = zs_s[c][rows, :] + (lt[rows, :] - (plk[rows, :] + cums[c][rows, :]))
                    if maskeds[bi]:
                        logw = jnp.where(causal, logw, -1e30)
                    w = jnp.exp(logw)
                    w_s[c][rows, :] = w.astype(BF16)
                    da = dws[c][rows, :] * w
                    da_s[c][rows, :] = da
                    dab_s[c][rows, :] = da.astype(BF16)
                    parts.append(jnp.sum(da, axis=1, keepdims=True))
                da_sums.append(jnp.concatenate(parts, axis=0))
            pres = [_dot(dab_s[c][...], before_s[...], "nn") for c in range(len(chains))]
            for c, (bi, hh) in enumerate(chains):
                pda = carries[hh][1]
                if bi == 1:
                    pda = pda + da_sums[hh]
                for rows, causal in strips:
                    sig = jnp.exp(zs_s[c][rows, :])
                    da = da_s[c][rows, :]
                    dz = da * (1.0 - sig) - sig * (pda[rows, :] + pres[c][rows, :])
                    if maskeds[bi]:
                        dz = jnp.where(causal, dz, 0.0)
                    dzs_s[c][rows, :] = dz.astype(BF16)
            new = [list(carries[hh]) for hh in range(2)]
            for c, (bi, hh) in enumerate(chains):
                dk_s[kbs[bi]] += _dot(heads[hh][3], dzs_s[c][...], "nn")
                dv_s[kbs[bi]] += _dot(heads[hh][4], w_s[c][...], "nn")
                new[hh] = [new[hh][0] + lk_sums[c], new[hh][1] + da_sums[c], new[hh][2] + _dot(dzs_s[c][...], kvs[bi], "nn")]
            return tuple(tuple(cr) for cr in new)

        def qblock(qi, _):
            qstart = pl.multiple_of(qi * blk, blk)
            qv = q_ref[pl.ds(qstart, blk), :] * scale
            dov = do_ref[pl.ds(qstart, blk), :]
            lv = l_ref[pl.ds(qstart, blk), :]
            heads = []
            for hh in range(2):
                sel = head0 if hh == 0 else ~head0
                qh, doh = jnp.where(sel, qv, jnp.zeros_like(qv)), jnp.where(sel, dov, jnp.zeros_like(dov))
                heads.append((qh, doh, jnp.max(jnp.where(sel, lv, -jnp.inf), axis=1, keepdims=True),
                              qh.astype(F32).T.astype(BF16), doh.astype(F32).T.astype(BF16)))
            zero = (jnp.zeros((blk, 1), F32), jnp.zeros((blk, 1), F32), jnp.zeros((blk, LANES), F32))
            carries = lax.fori_loop(0, qi // 2, lambda j, crs: step(heads, [2 * j, 2 * j + 1], [False, False], crs),
                                    (zero, zero))
            carries = lax.cond(qi % 2 == 1,
                               lambda crs: step(heads, [qi - 1, qi], [False, True], crs),
                               lambda crs: step(heads, [qi], [True], crs), carries)
            dq_s[pl.ds(qstart, blk), :] = jnp.where(head0, carries[0][2], carries[1][2]) * scale
            return 0

        lax.fori_loop(0, nq, qblock, 0)
        out_ref[0] = dq_s[...].astype(out_ref.dtype)
        for b in range(nq):
            out_ref[1, b * blk:(b + 1) * blk, :] = dk_s[b].T.astype(out_ref.dtype)
            out_ref[2, b * blk:(b + 1) * blk, :] = dv_s[b].T.astype(out_ref.dtype)

    col_blk = lambda off: pl.BlockSpec((t, LANES), lambda p: (0, off + p))
    return _host_call(
        body, "sb_bwd", npair, [qkv, qkv, qkv, do, ltot],
        [col_blk(0), col_blk(npair), col_blk(2 * npair), col_blk(0), col_blk(0)],
        [SDS((3, t, d), BF16)], [pl.BlockSpec((3, t, LANES), lambda p: (0, 0, p))],
        [pltpu.VMEM((t, LANES), F32)] + [pltpu.VMEM((nq, LANES, blk), F32) for _ in range(2)]
        + [pltpu.VMEM((blk, blk), BF16) for _ in range(2 + 20)]
        + [pltpu.VMEM((blk, blk), F32) for _ in range(8)], plan)


def _roll_rows(v, shift):
    return pltpu.roll(v, shift, 0)


def _shift_down(v, dist, fill, row):
    return jnp.where(row >= dist, _roll_rows(v, dist), fill)


def _shift_up(v, dist, fill, row):
    t = v.shape[0]
    return jnp.where(row < t - dist, _roll_rows(v, t - dist), fill)


def _lru_gates(xb, small, wr, wi, row):
    xs = [_shift_down(xb, 3 - tap, 0.0, row) if tap < 3 else xb for tap in range(4)]
    xc = small[4:5, :] + xs[0] * small[0:1, :]
    for tap in range(1, 4):
        xc = xc + xs[tap] * small[tap:tap + 1, :]
    xcb = xc.astype(BF16)
    r = _sigmoid(_dot(xcb, wr, "nn") + small[5:6, :])
    ig = _sigmoid(_dot(xcb, wi, "nn") + small[6:7, :])
    sp = _softplus(-small[7:8, :])
    la = -LRU_C * r * sp
    a = jnp.exp(la)
    th = jnp.tanh(la)
    m2 = -2.0 * th / (1.0 - th)
    return xs, xc, xcb, r, ig, sp, a, (jnp.sqrt(m2), m2)


def _gelu_parts(gate):
    inner = GELU_C * (gate + GELU_K * gate * gate * gate)
    th = jnp.tanh(inner)
    gelu = 0.5 * gate * (1.0 + th)
    dgelu = 0.5 * (1.0 + th) + 0.5 * gate * (1.0 - th * th) * GELU_C * (1.0 + 3.0 * GELU_K * gate * gate)
    return gelu, dgelu


def _scan_steps(t):
    steps, dist = [], 1
    while dist < t:
        steps.append(dist)
        dist *= 2
    return steps


SUBLANES = 8


def _linear_scan(a, b, scratch, row, reverse):
    a_s, b_s, carry_s = scratch
    t = a.shape[0]
    groups = t // SUBLANES
    in_group = row & (SUBLANES - 1)
    for dist in _scan_steps(SUBLANES):
        if reverse:
            inside = in_group < SUBLANES - dist
            b = b + a * jnp.where(inside, _roll_rows(b, t - dist), 0.0)
            a = a * jnp.where(inside, _roll_rows(a, t - dist), 1.0)
        else:
            inside = in_group >= dist
            b = a * jnp.where(inside, _roll_rows(b, dist), 0.0) + b
            a = a * jnp.where(inside, _roll_rows(a, dist), 1.0)
    a_s[...] = a
    b_s[...] = b
    end = 0 if reverse else SUBLANES - 1
    ends = pl.ds(end, groups, stride=SUBLANES)
    ae, be = a_s[ends, :], b_s[ends, :]
    grow = lax.broadcasted_iota(jnp.int32, ae.shape, 0)
    shift = _shift_up if reverse else _shift_down
    for dist in _scan_steps(groups):
        be = ae * shift(be, dist, 0.0, grow) + be
        ae = ae * shift(ae, dist, 1.0, grow)
    incoming = shift(be, 1, 0.0, grow)
    for k in range(SUBLANES):
        carry_s[pl.ds(k, groups, stride=SUBLANES), :] = incoming
    return a_s[...] * carry_s[...] + b_s[...]


def _lru_fwd(gx, small, wr, wi):
    t = gx.shape[0]
    r_dim = gx.shape[1] // 2
    nb = r_dim // LRU_BLOCK_W

    def body(gate_ref, xb_ref, small_ref, wr_ref, wi_ref, y_ref, hs_ref, *scratch):
        row = lax.broadcasted_iota(jnp.int32, (t, LRU_BLOCK_W), 0)
        xb = xb_ref[...]
        _, xc, _, _, ig, _, a, (mult, _) = _lru_gates(xb, small_ref, wr_ref[...], wi_ref[...], row)
        hsv = _linear_scan(a, mult * (ig * xc), scratch, row, reverse=False)
        hs_ref[...] = hsv
        gelu, _ = _gelu_parts(gate_ref[...])
        y_ref[...] = (gelu * hsv).astype(y_ref.dtype)

    colb = lambda off: pl.BlockSpec((t, LRU_BLOCK_W), lambda n: (0, off + n))
    wspec = pl.BlockSpec((None, LRU_BLOCK_W, LRU_BLOCK_W), lambda n: (n, 0, 0))
    return _pcall(
        body, name="lru_fwd", grid=(nb,),
        in_specs=[colb(0), colb(nb), pl.BlockSpec((8, LRU_BLOCK_W), lambda n: (0, n)), wspec, wspec],
        out_specs=[colb(0), colb(0)], out_shape=[SDS((t, r_dim), BF16), SDS((t, r_dim), F32)],
        scratch_shapes=[pltpu.VMEM((t, LRU_BLOCK_W), F32) for _ in range(3)],
        compiler_params=_params(("parallel",)))(gx, gx, small, wr, wi)


def _lru_bwd(gx, hs, dy, small, wr, wi, plan=None):
    t = gx.shape[0]
    r_dim = gx.shape[1] // 2
    nb = r_dim // LRU_BLOCK_W

    def body(in_refs, out_refs, scratch):
        (gate_ref, xb_ref, hs_ref, dy_ref, small_ref, wr_ref, wi_ref), (dgx_ref, dsm_ref, dwr_ref, dwi_ref) = in_refs, out_refs
        row = lax.broadcasted_iota(jnp.int32, (t, LRU_BLOCK_W), 0)
        xb, hsv, dyv, smallv = xb_ref[...], hs_ref[...], dy_ref[...], small_ref
        wrv, wiv = wr_ref[...], wi_ref[...]
        xs, xc, xcb, r, ig, sp, a, (mult, m2) = _lru_gates(xb, smallv, wrv, wiv, row)
        gelu, dgelu = _gelu_parts(gate_ref[...])
        dgx_ref[0] = (dyv * hsv * dgelu).astype(dgx_ref.dtype)
        dacc = _linear_scan(_shift_up(a, 1, 1.0, row), dyv * gelu, scratch, row, reverse=True)
        da = dacc * _shift_down(hsv, 1, 0.0, row)
        dmult = dacc * (ig * xc)
        dixc = dacc * mult
        dla = da * a - dmult * (a * a) * lax.rsqrt(m2)
        dr = dla * (-LRU_C * sp)
        dsp = jnp.sum(dla * (-LRU_C * r), axis=0, keepdims=True)
        dpr = dr * r * (1.0 - r)
        dpi = dixc * xc * ig * (1.0 - ig)
        dprb, dpib = dpr.astype(BF16), dpi.astype(BF16)
        dwr_ref[...] = _dot(xcb, dprb, "tn")
        dwi_ref[...] = _dot(xcb, dpib, "tn")
        dxc = dixc * ig + _dot(dprb, wrv, "nt") + _dot(dpib, wiv, "nt")
        dxb = dxc * smallv[3:4, :]
        for tap in range(3):
            dxb = dxb + _shift_up(dxc, 3 - tap, 0.0, row) * smallv[tap:tap + 1, :]
        dgx_ref[1] = dxb.astype(dgx_ref.dtype)
        lam = smallv[7:8, :]
        rows = [jnp.sum(dxc * xs[tap], axis=0, keepdims=True) for tap in range(4)]
        rows.append(jnp.sum(dxc, axis=0, keepdims=True))
        rows.append(jnp.sum(dpr, axis=0, keepdims=True))
        rows.append(jnp.sum(dpi, axis=0, keepdims=True))
        rows.append(-dsp * _sigmoid(-lam))
        for k, rv in enumerate(rows):
            dsm_ref[k:k + 1, :] = rv

    colb = lambda off: pl.BlockSpec((t, LRU_BLOCK_W), lambda n: (0, off + n))
    wspec = pl.BlockSpec((None, LRU_BLOCK_W, LRU_BLOCK_W), lambda n: (n, 0, 0))
    sspec = pl.BlockSpec((8, LRU_BLOCK_W), lambda n: (0, n))
    return _host_call(
        body, "lru_bwd", nb, [gx, gx, hs, dy, small, wr, wi],
        [colb(0), colb(nb), colb(0), colb(0), sspec, wspec, wspec],
        [SDS((2, t, r_dim), BF16), SDS((8, r_dim), F32), SDS((nb, LRU_BLOCK_W, LRU_BLOCK_W), F32),
         SDS((nb, LRU_BLOCK_W, LRU_BLOCK_W), F32)],
        [pl.BlockSpec((2, t, LRU_BLOCK_W), lambda n: (0, 0, n)), sspec, wspec, wspec],
        [pltpu.VMEM((t, LRU_BLOCK_W), F32) for _ in range(3)], plan)


def _adam(w, g, m, v):
    m2 = ADAM_B1 * m + (1.0 - ADAM_B1) * g
    v2 = ADAM_B2 * v + (1.0 - ADAM_B2) * (g * g)
    m_hat = m2 / (1.0 - ADAM_B1 ** ADAM_STEP)
    v_hat = v2 / (1.0 - ADAM_B2 ** ADAM_STEP)
    return -ADAM_LR * (m_hat / (jnp.sqrt(v_hat) + ADAM_EPS) + ADAM_WD * w), m2, v2


def _mod_fwd(c_all, mod_w, mod_b_cols):
    nl, d, cols = mod_w.shape
    nbatch = c_all.shape[0]

    def body(c_ref, w_ref, b_ref, o_ref):
        cv = c_ref[...]
        ca = (cv * _sigmoid(cv)).astype(BF16)
        o_ref[...] = _dot(ca, w_ref[...].astype(BF16), "nn") + b_ref[...]

    return _pcall(
        body, name="mod_fwd", grid=(nl,),
        in_specs=[pl.BlockSpec((nbatch, d), lambda l: (0, 0)), pl.BlockSpec((None, d, cols), lambda l: (l, 0, 0)),
                  pl.BlockSpec((None, 1, cols), lambda l: (l, 0, 0))],
        out_specs=pl.BlockSpec((None, nbatch, cols), lambda l: (l, 0, 0)), out_shape=SDS((nl, nbatch, cols), F32),
        compiler_params=_params(("parallel",)))(c_all, mod_w, mod_b_cols)


def _mod_w_update(c_all, dmod_cols, w, m, v):
    nl, d, cols = w.shape
    nbatch = c_all.shape[0]
    tr = _tile(d, (256, 128))

    def body(c_ref, dm_ref, w_ref, m_ref, v_ref, g_ref, dl_ref, m2_ref, v2_ref):
        cv = c_ref[...]
        ca = (cv * _sigmoid(cv)).astype(BF16)
        g = _dot(ca, dm_ref[...].astype(BF16), "tn")
        g_ref[...] = g
        dl_ref[...], m2_ref[...], v2_ref[...] = _adam(w_ref[...], g, m_ref[...], v_ref[...])

    wblk = pl.BlockSpec((None, tr, cols), lambda l, i: (l, i, 0))
    return _pcall(
        body, name="mod_w_update", grid=(nl, d // tr),
        in_specs=[pl.BlockSpec((nbatch, tr), lambda l, i: (0, i)), pl.BlockSpec((None, nbatch, cols), lambda l, i: (l, 0, 0)),
                  wblk, wblk, wblk],
        out_specs=[wblk] * 4, out_shape=[SDS(w.shape, F32)] * 4,
        compiler_params=_params(("parallel", "parallel")))(c_all, dmod_cols, w, m, v)


def _adam_update(name, w, m, v, gparts):
    rows, cols = w.shape
    tr = _tile(rows, (256, 128, 64, 32, 16, 8))
    npart = len(gparts)

    def body(*refs):
        w_ref, m_ref, v_ref = refs[:3]
        g_refs = refs[3:3 + npart]
        g_ref, dl_ref, m2_ref, v2_ref = refs[3 + npart:]
        g = g_refs[0][...].astype(F32)
        for gr in g_refs[1:]:
            g = g + gr[...].astype(F32)
        g_ref[...] = g
        dl_ref[...], m2_ref[...], v2_ref[...] = _adam(w_ref[...], g, m_ref[...], v_ref[...])

    blk = pl.BlockSpec((tr, cols), lambda i: (i, 0))
    return _pcall(body, name=name, grid=(rows // tr,), in_specs=[blk] * (3 + npart), out_specs=[blk] * 4,
                  out_shape=[SDS((rows, cols), F32)] * 4, compiler_params=_params(("parallel",)))(w, m, v, *gparts)


def _adam_shard(name, w, m, v, part4, recv3, chip_idx, first=0, fills=None):
    p, r, cdim = w.shape
    pg = part4.shape[0]
    tr = _tile(r, (256, 176, 160, 128, 64, 32, 16))

    def body(chip_ref, w_ref, m_ref, v_ref, own_ref, r0_ref, r1_ref, r2_ref, *rest):
        g_ref, dl_ref, m2_ref, v2_ref = rest[-4:]
        g = own_ref[...].astype(F32) + r0_ref[...].astype(F32) + r1_ref[...].astype(F32) + r2_ref[...].astype(F32)
        g_ref[...] = g
        dl_ref[...], m2_ref[...], v2_ref[...] = _adam(w_ref[...], g, m_ref[...], v_ref[...])

    blk = pl.BlockSpec((None, tr, cdim), lambda q, i, chip_ref: (first + q, i, 0))
    blk4 = (None, None, tr, cdim)
    slot = lambda s: pl.BlockSpec(blk4, lambda q, i, chip_ref: (s, q, i, 0))
    fills = list(fills or [])
    grid_spec = pltpu.PrefetchScalarGridSpec(
        num_scalar_prefetch=1, grid=(pg, r // tr),
        in_specs=[blk, blk, blk, pl.BlockSpec(blk4, lambda q, i, chip_ref: (q, chip_ref[0], i, 0)), slot(0), slot(1), slot(2)]
        + [ANY] * len(fills),
        out_specs=[blk] * 4)
    return _pcall(body, name=name, grid_spec=grid_spec, out_shape=[SDS((p, r, cdim), F32)] * 4,
                  input_output_aliases={8 + k: k for k in range(len(fills))},
                  compiler_params=_params(("parallel", "parallel")))(chip_idx, w, m, v, part4, recv3, recv3, recv3, *fills)


def _sum_devices(gathered, name):
    _, rows, cols = gathered.shape
    tr = _tile(rows, (512, 256, 128, 64, 32, 16, 8))

    def body(g_ref, o_ref):
        acc = g_ref[0].astype(F32)
        for k in range(1, N_DEV):
            acc = acc + g_ref[k].astype(F32)
        o_ref[...] = acc

    return _pcall(body, name=name, grid=(rows // tr,), in_specs=[pl.BlockSpec((N_DEV, tr, cols), lambda i: (0, i, 0))],
                  out_specs=pl.BlockSpec((tr, cols), lambda i: (i, 0)), out_shape=SDS((rows, cols), F32),
                  compiler_params=_params(("parallel",)))(gathered)


def _pack_flat(parts, width, row_mult, dtype):
    flat = jnp.concatenate([p.reshape(-1).astype(dtype) for p in parts])
    unit = width * row_mult
    pad = (-flat.shape[0]) % unit
    if pad:
        flat = jnp.concatenate([flat, jnp.zeros((pad,), dtype)])
    return flat.reshape(-1, width)


def _unpack_flat(flat, shapes):
    out, off = [], 0
    for shp in shapes:
        size = math.prod(shp)
        out.append(flat[off:off + size].reshape(shp))
        off += size
    return out


def kernel(x, c, mod_w, mod_b, norm_g, ffn_w_gu, ffn_w_down, sb_w_qkv, sb_w_o, lru_w_in, lru_conv_w, lru_conv_b, lru_w_r, lru_b_r, lru_w_i, lru_b_i, lru_lambda, lru_w_out, final_norm_g, loss_target, m_mod_w, m_mod_b, m_norm_g, m_ffn_w_gu, m_ffn_w_down, m_sb_w_qkv, m_sb_w_o, m_lru_w_in, m_lru_conv_w, m_lru_conv_b, m_lru_w_r, m_lru_b_r, m_lru_w_i, m_lru_b_i, m_lru_lambda, m_lru_w_out, m_final_norm_g, v_mod_w, v_mod_b, v_norm_g, v_ffn_w_gu, v_ffn_w_down, v_sb_w_qkv, v_sb_w_o, v_lru_w_in, v_lru_conv_w, v_lru_conv_b, v_lru_w_r, v_lru_b_r, v_lru_w_i, v_lru_b_i, v_lru_lambda, v_lru_w_out, v_final_norm_g):
    weights = dict(mod_w=mod_w, mod_b=mod_b, norm_g=norm_g, ffn_w_gu=ffn_w_gu, ffn_w_down=ffn_w_down, sb_w_qkv=sb_w_qkv,
                   sb_w_o=sb_w_o, lru_w_in=lru_w_in, lru_conv_w=lru_conv_w, lru_conv_b=lru_conv_b, lru_w_r=lru_w_r,
                   lru_b_r=lru_b_r, lru_w_i=lru_w_i, lru_b_i=lru_b_i, lru_lambda=lru_lambda, lru_w_out=lru_w_out,
                   final_norm_g=final_norm_g)
    mom_m = dict(mod_w=m_mod_w, mod_b=m_mod_b, norm_g=m_norm_g, ffn_w_gu=m_ffn_w_gu, ffn_w_down=m_ffn_w_down,
                 sb_w_qkv=m_sb_w_qkv, sb_w_o=m_sb_w_o, lru_w_in=m_lru_w_in, lru_conv_w=m_lru_conv_w,
                 lru_conv_b=m_lru_conv_b, lru_w_r=m_lru_w_r, lru_b_r=m_lru_b_r, lru_w_i=m_lru_w_i, lru_b_i=m_lru_b_i,
                 lru_lambda=m_lru_lambda, lru_w_out=m_lru_w_out, final_norm_g=m_final_norm_g)
    mom_v = dict(mod_w=v_mod_w, mod_b=v_mod_b, norm_g=v_norm_g, ffn_w_gu=v_ffn_w_gu, ffn_w_down=v_ffn_w_down,
                 sb_w_qkv=v_sb_w_qkv, sb_w_o=v_sb_w_o, lru_w_in=v_lru_w_in, lru_conv_w=v_lru_conv_w,
                 lru_conv_b=v_lru_conv_b, lru_w_r=v_lru_w_r, lru_b_r=v_lru_b_r, lru_w_i=v_lru_w_i, lru_b_i=v_lru_b_i,
                 lru_lambda=v_lru_lambda, lru_w_out=v_lru_w_out, final_norm_g=v_final_norm_g)
    names = list(weights)

    t, d = x.shape[1], x.shape[2]
    n_layers = mod_w.shape[0]
    r_dim = lru_w_out.shape[1] * N_DEV
    ng, rs = d // N_DEV, r_dim // N_DEV
    mod_cols = mod_w.shape[2]
    nblk = lru_w_r.shape[1]
    xi, yi, ci = _mesh_pos()
    me = 4 * xi + 2 * yi + ci
    chip = 2 * xi + yi
    x2, target = x.reshape(t, d), loss_target.reshape(t, d)

    lru_small_shard = jnp.concatenate([lru_conv_w[0], lru_conv_b, lru_b_r, lru_b_i, lru_lambda], axis=0)
    small1 = _pack_flat([c, norm_g, lru_small_shard], LANES, 8, F32)
    n_small1 = small1.shape[0]
    all1 = _allgather(small1[None], "gather_small").reshape(N_DEV, n_small1 * LANES)
    c_all = all1[:, :d]
    norm_full = jnp.transpose(all1[:, d:d + 6 * ng].reshape(N_DEV, n_layers, 3, ng), (1, 2, 0, 3)).reshape(n_layers, 3, d)
    lru_small = jnp.transpose(all1[:, d + 6 * ng:d + 6 * ng + 8 * rs].reshape(N_DEV, 8, rs), (1, 0, 2)).reshape(8, r_dim)

    mod_b_cols = lax.dynamic_slice_in_dim(mod_b, me * mod_cols, mod_cols, axis=1).reshape(n_layers, 1, mod_cols)
    mod_part = _mod_fwd(c_all, mod_w, mod_b_cols)

    assert sb_w_qkv.shape[0] == 1 and lru_w_in.shape[0] == 1, "one stick-breaking and one RG-LRU layer"
    n_ffn = 2 * n_layers
    fc = ffn_w_gu.shape[3]
    cw_in = lru_w_in.shape[2]
    pieces = {("ffn_w_gu", q): ffn_w_gu[q // 2, q % 2][None] for q in range(n_ffn)}
    pieces.update({("ffn_w_down", q): ffn_w_down[q // 2, q % 2][None] for q in range(n_ffn)})
    pieces.update({("sb_w_qkv", 0): sb_w_qkv, ("sb_w_o", 0): sb_w_o, ("lru_w_in", 0): lru_w_in, ("lru_w_out", 0): lru_w_out})
    col_window = {("sb_w_qkv", 0)}
    first = [("ffn_w_gu", 0)]
    behind = {"l0s0_gu": [("ffn_w_down", 0)], "l0s0_down": [("sb_w_qkv", 0), ("sb_w_o", 0)],
              "l0s2_gu": [("ffn_w_down", n_ffn - 1)], "l0s2_down": [("ffn_w_down", 2)],
              "l1s0_gu": [("lru_w_in", 0)], "l1s0_down": [("lru_w_out", 0)]}
    behind["sb_fwd"] = [key for key in pieces if key not in first + sum(behind.values(), [])]
    gathered = {}

    def gather_plan(keys):
        return _gather_plan([pieces[key].astype(BF16) for key in keys], [key in col_window for key in keys])

    def hosting(name, call):
        keys = behind.get(name, [])
        outs = call(gather_plan(keys) if keys else None)
        gathered.update(zip(keys, outs[len(outs) - len(keys):]))
        return outs[:len(outs) - len(keys)]

    mod_all, *landed = _run_comm(_merge_plans([_gather_plan([mod_part], [False]), gather_plan(first)]), "gather_mod_and_first")
    gathered.update(zip(first, landed))
    mod_mine = lax.dynamic_index_in_dim(mod_all, me, axis=2, keepdims=False)
    mod_mine = mod_mine.reshape(n_layers, 3, 3, d)
    wr_b, wi_b = lru_w_r[0].astype(BF16), lru_w_i[0].astype(BF16)
    eye2 = jnp.eye(2 * cw_in, dtype=BF16).reshape(2, cw_in, 2 * cw_in)

    def w_gu(q):
        return gathered[("ffn_w_gu", q)]

    def w_d4(q):
        return gathered[("ffn_w_down", q)].reshape(1, HIDDEN_CHUNKS, fc, d)

    saved = []
    xcur = x2
    for layer in range(n_layers):
        for sub in range(3):
            gvec = norm_full[layer, sub].reshape(1, d)
            shift = mod_mine[layer, sub, 0].reshape(1, d)
            scale1p = 1.0 + mod_mine[layer, sub, 1].reshape(1, d)
            gmul = 1.0 + mod_mine[layer, sub, 2].reshape(1, d)
            tag = f"l{layer}s{sub}"
            h = _norm_fwd(xcur, gvec, scale1p, shift, tag + "_norm")
            rec = dict(x=xcur, h=h, g=gvec, scale1p=scale1p, gmul=gmul, w=MACARON_W if sub != 1 else 1.0)
            if sub != 1:
                lj = layer * 2 + sub // 2
                gu2, a = hosting(tag + "_gu", lambda plan: _ffn_gu(tag + "_gu", h, w_gu(lj), plan))
                yv, xcur = hosting(tag + "_down", lambda plan: _ffn_down(tag + "_down", a, w_d4(lj), xcur, gmul, plan))
                rec.update(kind="ffn", lj=lj, gu2=gu2, a=a, y=yv)
            elif layer % 2 == 0:
                w_qkv = gathered[("sb_w_qkv", 0)][0]
                w_o = gathered[("sb_w_o", 0)].reshape(d, d)
                qkv = _mm_nn(tag + "_qkv", h, w_qkv, BF16)[0]
                o, ltot = hosting("sb_fwd", lambda plan: _sb_fwd(qkv, d, plan))
                yv, xcur = _mm_nn(tag + "_wo", o, w_o, [BF16, F32], extras=[(xcur, "tile"), (gmul, "row")],
                                  epilogue=lambda accs, ex: (accs[0], ex[0] + ex[1] * accs[0]))
                rec.update(kind="sb", qkv=qkv, o=o, ltot=ltot, y=yv, w_qkv=w_qkv, w_o=w_o)
            else:
                w_in = _chunks_to_cols("lru_w_in_cols", gathered[("lru_w_in", 0)][0], eye2)
                w_out = gathered[("lru_w_out", 0)].reshape(r_dim, d)
                gx = _mm_nn(tag + "_win", h, w_in, F32)[0]
                ymix, hs = _lru_fwd(gx, lru_small, wr_b, wi_b)
                yv, xcur = _mm_nn(tag + "_wout", ymix, w_out, [BF16, F32], extras=[(xcur, "tile"), (gmul, "row")],
                                  epilogue=lambda accs, ex: (accs[0], ex[0] + ex[1] * accs[0]))
                rec.update(kind="lru", gx=gx, hs=hs, ymix=ymix, y=yv, w_in=w_in, w_out=w_out)
            saved.append(rec)

    last = saved[-1]
    dxo, dy, head_sums = _loss_head(xcur, target, final_norm_g.reshape(1, d), (last["w"] * last["gmul"]))
    loss_mine = head_sums[1, 0:1]
    dgf = head_sums[0]

    c_idx = jnp.reshape(ci, (1,)).astype(jnp.int32)
    chip_idx = jnp.reshape(chip, (1,)).astype(jnp.int32)
    grads, reduced = {}, {}
    to_pair = []
    to_chips = []

    def sibling_plan(only=None):
        keys = [key for key in to_pair if only is None or key in only]
        if not keys:
            return None, keys
        return _exchange_plan([grads[key] for key in keys], [key in col_window for key in keys], 4, _sibling_route), keys

    def sibling_done(keys, recv4):
        for key, r4 in zip(keys, recv4):
            to_pair.remove(key)
            to_chips.append((key, _pair_sum(grads[key], r4, c_idx, f"rs_pair_sum_{key[0]}{key[1]}", cols=key in col_window)))

    def chip_plan(only=None):
        items = [item for item in to_chips if only is None or item[0] in only]
        if not items:
            return None, items
        return _exchange_plan([p4 for _, p4 in items], [False] * len(items), 3, _chip_route), items

    def chips_done(items, recv3):
        for item, r3 in zip(items, recv3):
            to_chips.remove(item)
            reduced[item[0]] = (item[1], r3)

    def behind(call, make_plan, done, more=None):
        plan, items = make_plan()
        n_mine = len(plan.outs) if plan else 0
        n_more = len(more.outs) if more else 0
        outs = call(_merge_plans([plan, more]))
        n_own = len(outs) - n_mine - n_more
        done(items, outs[n_own:n_own + n_mine])
        return list(outs[:n_own]) + list(outs[n_own + n_mine:])

    carried = {
        "l1s1b_dymix": ("sibling", None), "lru_bwd": ("chips", [("ffn_w_gu", n_ffn - 1)]),
        "l1s0b_da": ("sibling", None), "l1s0b_dwgu": ("chips", [("ffn_w_down", n_ffn - 1)]),
        "l1s0b_dh": ("chips", [("lru_w_out", 0), ("lru_w_in", 0)]),
        "l0s2b_da": ("sibling", None), "l0s2b_dwgu": ("chips", [("ffn_w_down", n_ffn - 2)]),
    }

    def carrying(name, call):
        if name not in carried:
            return call(None)
        stage, only = carried[name]
        if stage == "sibling":
            return behind(call, functools.partial(sibling_plan, only), sibling_done)
        return behind(call, functools.partial(chip_plan, only), chips_done)

    def at_once(make_plan, done, name):
        plan, items = make_plan()
        if plan:
            done(items, _run_comm(plan, name))

    def add_grad(key, value):
        grads[key] = value
        to_pair.append(key)

    dmod = [[None] * 3 for _ in range(n_layers)]
    dnorm = [[None] * 3 for _ in range(n_layers)]
    dlru_small = wri_all = None
    for idx in reversed(range(len(saved))):
        rec = saved[idx]
        layer, sub = divmod(idx, 3)
        tag = f"l{layer}s{sub}b"
        if rec["kind"] == "ffn" and idx > 0:
            lj = rec["lj"]
            (dgu2,) = carrying(tag + "_da", lambda plan: _ffn_da(tag + "_da", dy, w_d4(lj), rec["gu2"], plan))
            dwd = _ffn_dwd(tag + "_dwd", rec["a"], dy)[0].reshape(gathered[("ffn_w_down", lj)].shape)
            (dwgu,) = carrying(tag + "_dwgu", lambda plan: _ffn_dwgu(tag + "_dwgu", rec["h"], dgu2, plan))
            (dh,) = carrying(tag + "_dh", lambda plan: _ffn_dh(tag + "_dh", dgu2, w_gu(lj), plan))
            add_grad(("ffn_w_down", lj), dwd)
            add_grad(("ffn_w_gu", lj), dwgu)
        elif rec["kind"] == "ffn":
            lj = rec["lj"]
            at_once(sibling_plan, sibling_done, "rs_sibling_" + tag)
            (dgu2,) = behind(lambda plan: _ffn_da(tag + "_da", dy, w_d4(lj), rec["gu2"], plan), chip_plan, chips_done)
            add_grad(("ffn_w_down", lj), _ffn_dwd(tag + "_dwd", rec["a"], dy)[0].reshape(gathered[("ffn_w_down", lj)].shape))
            at_once(sibling_plan, sibling_done, "rs_sibling_" + tag + "_dwd")
            (dwgu,) = behind(lambda plan: _ffn_dwgu(tag + "_dwgu", rec["h"], dgu2, plan), chip_plan, chips_done)
            add_grad(("ffn_w_gu", lj), dwgu)
            at_once(sibling_plan, sibling_done, "rs_sibling_" + tag + "_dwgu")
            (dh,) = behind(lambda plan: _ffn_dh(tag + "_dh", dgu2, w_gu(lj), plan), chip_plan, chips_done)
        elif rec["kind"] == "sb":
            at_once(sibling_plan, sibling_done, "rs_sibling_" + tag)
            do = _mm_nt(tag + "_do", dy, rec["w_o"], BF16)
            dwo = _mm_tn(tag + "_dwo", rec["o"], dy, BF16)
            wri = _pack_flat([dwr, dwi], LANES, 512, BF16)[None]
            dqkv3, wri_all = behind(lambda plan: _sb_bwd(rec["qkv"], do, rec["ltot"], d, plan), chip_plan, chips_done,
                                    more=_gather_plan([wri], [False]))
            add_grad(("sb_w_o", 0), dwo.reshape(gathered[("sb_w_o", 0)].shape))
            dh = _mm_nt_stack(tag + "_dh", dqkv3, rec["w_qkv"], F32)
            add_grad(("sb_w_qkv", 0), _mm_tn_stack(tag + "_dwqkv", rec["h"], dqkv3, BF16)[None])
        else:
            (dymix,) = carrying(tag + "_dymix", lambda plan: _mm_nt(tag + "_dymix", dy, rec["w_out"], F32, plan)
                                if plan else [_mm_nt(tag + "_dymix", dy, rec["w_out"], F32)])
            dwout = _mm_tn(tag + "_dwout", rec["ymix"], dy, BF16).reshape(gathered[("lru_w_out", 0)].shape)
            dgx2, dlru_small, dwr, dwi = carrying("lru_bwd", lambda plan: _lru_bwd(rec["gx"], rec["hs"], dymix, lru_small,
                                                                                wr_b, wi_b, plan))
            dh = _mm_nt_stack(tag + "_dh", dgx2, rec["w_in"], F32)
            dw_in = _mm_tn_stack(tag + "_dwin", rec["h"], dgx2, BF16)
            add_grad(("lru_w_out", 0), dwout)
            add_grad(("lru_w_in", 0), _cols_to_chunks("lru_w_in_chunks", dw_in, eye2)[None])
        prev = saved[idx - 1] if idx > 0 else None
        gw_prev = (prev["w"] * prev["gmul"]) if prev is not None else jnp.zeros((1, d), F32)
        dxo, dy, sums = _adaln_bwd(dh, rec["x"], rec["y"], dxo, rec["g"], rec["scale1p"], rec["w"], gw_prev, tag + "_adaln")
        dmod[layer][sub] = sums[0:3]
        dnorm[layer][sub] = sums[3]
    grad_x = dxo.reshape(x.shape)

    dmod_mine = jnp.stack([jnp.stack(dmod[layer]) for layer in range(n_layers)])
    dnorm_mine = jnp.stack([jnp.stack(dnorm[layer]) for layer in range(n_layers)])
    assert not to_pair and not to_chips
    small_shapes = [(n_layers, 9 * d), (n_layers, 3, d), (8, r_dim), (d,), (1,)]
    small3 = _pack_flat([dmod_mine, dnorm_mine, dlru_small, dgf, loss_mine], LANES, 256, F32)
    n_small3 = small3.shape[0]
    all3 = _allgather(small3[None], "gather_small_grads").reshape(N_DEV, n_small3, LANES)
    gsum = _sum_devices(all3, "sum_small_grads").reshape(-1)
    g_mod_b, g_norm_full, g_lru_small, g_final, loss_sum = _unpack_flat(gsum, small_shapes)
    loss = loss_sum[0]
    wri_sum = _sum_devices(wri_all.reshape(N_DEV, -1, LANES), "sum_gate_weight_grads").reshape(-1)
    g_wr, g_wi = _unpack_flat(wri_sum, [lru_w_r.shape, lru_w_i.shape])
    dmod_all = all3.reshape(N_DEV, -1)[:, :n_layers * 9 * d].reshape(N_DEV, n_layers, N_DEV, mod_cols)
    dmod_cols = jnp.transpose(lax.dynamic_index_in_dim(dmod_all, me, axis=2, keepdims=False), (1, 0, 2))

    out_g, out_d, out_m, out_v = {}, {}, {}, {}
    out_g["mod_w"], out_d["mod_w"], out_m["mod_w"], out_v["mod_w"] = _mod_w_update(c_all, dmod_cols, mod_w, m_mod_w, v_mod_w)

    g_norm_shard = lax.dynamic_slice_in_dim(g_norm_full, me * ng, ng, axis=2)
    g_lru_shard = lax.dynamic_slice_in_dim(g_lru_small, me * rs, rs, axis=1)
    small_grads = dict(mod_b=g_mod_b, norm_g=g_norm_shard, lru_conv_w=g_lru_shard[0:4].reshape(lru_conv_w.shape),
                       lru_conv_b=g_lru_shard[4:5], lru_b_r=g_lru_shard[5:6], lru_b_i=g_lru_shard[6:7],
                       lru_lambda=g_lru_shard[7:8], final_norm_g=g_final)
    for n, g in (("lru_w_r", g_wr), ("lru_w_i", g_wi)):
        view = lambda arr: arr.reshape(-1, LRU_BLOCK_W)
        outs = _adam_update("adam_" + n, view(weights[n]), view(mom_m[n]), view(mom_v[n]), [view(g)])
        out_g[n], out_d[n], out_m[n], out_v[n] = [o.reshape(weights[n].shape) for o in outs]
    small_names = list(small_grads)
    sw = _pack_flat([weights[n] for n in small_names], LANES, 256, F32)
    sg = _pack_flat([small_grads[n] for n in small_names], LANES, 256, F32)
    sm = _pack_flat([mom_m[n] for n in small_names], LANES, 256, F32)
    sv = _pack_flat([mom_v[n] for n in small_names], LANES, 256, F32)
    s_outs = _adam_update("adam_small", sw, sm, sv, [sg])
    small_shapes2 = [weights[n].shape for n in small_names]
    for dst, flat in zip((out_g, out_d, out_m, out_v), s_outs):
        for n, arr in zip(small_names, _unpack_flat(flat.reshape(-1), small_shapes2)):
            dst[n] = arr

    for n in ["ffn_w_gu", "ffn_w_down", "sb_w_qkv", "sb_w_o", "lru_w_in", "lru_w_out"]:
        shp = weights[n].shape
        shard3 = (math.prod(shp[:-2]),) + shp[-2:]
        view = lambda arr: arr.reshape(shard3)
        outs = None
        for q in range(shard3[0]):
            fills = outs if outs is not None else [lax.empty(shard3, F32) for _ in range(4)]
            p4, r3 = reduced[(n, q)]
            outs = _adam_shard(f"adam_{n}{q}", view(weights[n]), view(mom_m[n]), view(mom_v[n]), p4, r3, chip_idx,
                               first=q, fills=fills if shard3[0] > 1 else None)
        out_g[n], out_d[n], out_m[n], out_v[n] = [o.reshape(shp) for o in outs]

    return (loss, grad_x, *[out_g[n] for n in names], *[out_d[n] for n in names], *[out_m[n] for n in names],
            *[out_v[n] for n in names])
```

```python
import functools
import math

import jax
import jax.numpy as jnp
from jax import lax
from jax.experimental import pallas as pl
from jax.experimental.pallas import tpu as pltpu

F32 = jnp.float32
BF16 = jnp.bfloat16
SDS = jax.ShapeDtypeStruct
MESH = pl.DeviceIdType.MESH
ANY = pl.BlockSpec(memory_space=pl.ANY)

N_DEV = 8
LANES = 128
HEAD_DIM = 64
LRU_BLOCK_W = 128
LRU_C = 8.0
MACARON_W = 0.5
NORM_EPS = 1e-6
ADAM_LR = 0.001
ADAM_B1 = 0.9
ADAM_B2 = 0.999
ADAM_EPS = 1e-08
ADAM_WD = 0.01
ADAM_STEP = 10
VMEM_LIMIT = 56 * 1024 * 1024
GELU_C = math.sqrt(2.0 / math.pi)
GELU_K = 0.044715

DIMS = {
    "nn": (((1,), (0,)), ((), ())),
    "nt": (((1,), (1,)), ((), ())),
    "tn": (((0,), (0,)), ((), ())),
}


def _pcall(body, **kw):
    return pl.pallas_call(body, **kw)


def _params(sem=None):
    return pltpu.CompilerParams(dimension_semantics=sem, vmem_limit_bytes=VMEM_LIMIT)


def _tile(n, prefs):
    for p in prefs:
        if n % p == 0:
            return p
    return n


def _dot(a, b, dims):
    return lax.dot_general(a, b, DIMS[dims], preferred_element_type=F32)


def _softplus(z):
    return jnp.maximum(z, 0.0) + jnp.log(1.0 + jnp.exp(-jnp.abs(z)))


def _sigmoid(z):
    return 0.5 * jnp.tanh(0.5 * z) + 0.5


def _mesh_pos():
    return lax.axis_index("x"), lax.axis_index("y"), lax.axis_index("c")


def _allgather(xs, name, cols=False):
    return _run_comm(_gather_plan([xs], [cols]), name)[0]


class _CommPlan:
    def __init__(self, ins, outs, n_remote, n_local, phases):
        self.ins, self.outs, self.n_remote, self.n_local, self.phases = ins, outs, n_remote, n_local, phases

    def scratch(self):
        return [pltpu.SemaphoreType.DMA((self.n_remote,)), pltpu.SemaphoreType.DMA((self.n_remote,)),
                pltpu.SemaphoreType.DMA((max(self.n_local, 1),))]


def _merge_plans(plans):
    plans = [p for p in plans if p is not None]
    if len(plans) <= 1:
        return plans[0] if plans else None

    def phase(k):
        def run(in_refs, out_refs, send_sems, recv_sems, local_sems, r0=0, l0=0):
            i0 = o0 = 0
            for p in plans:
                p.phases[k](in_refs[i0:i0 + len(p.ins)], out_refs[o0:o0 + len(p.outs)], send_sems, recv_sems, local_sems, r0, l0)
                i0, o0, r0, l0 = i0 + len(p.ins), o0 + len(p.outs), r0 + p.n_remote, l0 + p.n_local
        return run

    return _CommPlan(sum([p.ins for p in plans], []), sum([p.outs for p in plans], []), sum(p.n_remote for p in plans),
                     sum(p.n_local for p in plans), [phase(0), phase(1), phase(2)])


def _run_comm(plan, name):
    n_in, n_out = len(plan.ins), len(plan.outs)

    def body(*refs):
        in_refs, out_refs, sems = refs[:n_in], refs[n_in:n_in + n_out], refs[n_in + n_out:]
        for phase in plan.phases:
            phase(in_refs, out_refs, *sems)

    return _pcall(body, name=name, out_shape=plan.outs, in_specs=[ANY] * n_in, out_specs=[ANY] * n_out,
                  scratch_shapes=plan.scratch())(*plan.ins)


def _col_window(ref, idx, width):
    return ref.at[:, :, pl.ds(pl.multiple_of(idx * width, math.gcd(width, LANES)), width)]


def _gather_plan(shards, cols):
    n = len(shards)
    outs = [SDS((s.shape[0], s.shape[1], N_DEV * s.shape[2]) if cl else (s.shape[0], N_DEV) + s.shape[1:], s.dtype)
            for s, cl in zip(shards, cols)]

    def copies(a, in_refs, out_refs, send_sems, recv_sems, local_sems, r0=0, l0=0):
        x, y, c = _mesh_pos()
        sibling = (x, y, 1 - c)
        chips = [(1 - x, y), (x, 1 - y), (1 - x, 1 - y)]
        width = shards[a].shape[2]

        def block(px, py, pc):
            idx = 4 * px + 2 * py + pc
            return _col_window(out_refs[a], idx, width) if cols[a] else out_refs[a].at[:, idx]

        def copy(k, owner, to, src=None):
            sem = r0 + 7 * a + k
            return pltpu.make_async_remote_copy(
                src_ref=block(*owner) if src is None else src, dst_ref=block(*owner),
                send_sem=send_sems.at[sem], recv_sem=recv_sems.at[sem], device_id=to, device_id_type=MESH)

        me = (x, y, c)
        first = [copy(0, me, sibling, src=in_refs[a])]
        first += [copy(1 + j, me, (*chip, c), src=in_refs[a]) for j, chip in enumerate(chips)]
        passed = [copy(4 + j, (*chip, c), sibling) for j, chip in enumerate(chips)]
        landed = [copy(1 + j, (*chip, c), me) for j, chip in enumerate(chips)]
        from_sibling = [copy(0, sibling, me)] + [copy(4 + j, (*chip, 1 - c), me) for j, chip in enumerate(chips)]
        mine = pltpu.make_async_copy(in_refs[a], block(*me), local_sems.at[l0 + a])
        return first, passed, landed, from_sibling, mine

    def start(*refs):
        for a in range(n):
            first, _, _, _, mine = copies(a, *refs)
            mine.start()
            for cp in first:
                cp.start()

    def pass_on(*refs):
        for a in range(n):
            _, passed, landed, _, _ = copies(a, *refs)
            for cp, fwd in zip(landed, passed):
                cp.wait_recv()
                fwd.start()

    def finish(*refs):
        for a in range(n):
            first, passed, _, from_sibling, mine = copies(a, *refs)
            for cp in from_sibling:
                cp.wait_recv()
            for cp in first + passed:
                cp.wait_send()
            mine.wait()

    return _CommPlan(list(shards), outs, 7 * n, n, [start, pass_on, finish])


def _exchange_plan(srcs, cols, n_slots, route):
    n = len(srcs)
    outs = []
    for g, cl in zip(srcs, cols):
        shard = (g.shape[0], g.shape[1], g.shape[2] // N_DEV) if cl else (g.shape[0],) + g.shape[2:]
        outs.append(SDS((n_slots,) + shard, g.dtype))

    def copies(in_refs, out_refs, send_sems, recv_sems, local_sems, r0=0, l0=0):
        x, y, c = _mesh_pos()
        made = []
        for a in range(n):
            for s in range(n_slots):
                chunk, target = route(x, y, c, s)
                src = _col_window(in_refs[a], chunk, outs[a].shape[3]) if cols[a] else in_refs[a].at[:, chunk]
                sem = r0 + a * n_slots + s
                made.append(pltpu.make_async_remote_copy(
                    src_ref=src, dst_ref=out_refs[a].at[s], send_sem=send_sems.at[sem], recv_sem=recv_sems.at[sem],
                    device_id=target, device_id_type=MESH))
        return made

    def start(*refs):
        for cp in copies(*refs):
            cp.start()

    def nothing(*refs):
        pass

    def finish(*refs):
        made = copies(*refs)
        for cp in made:
            cp.wait_recv()
        for cp in made:
            cp.wait_send()

    return _CommPlan(list(srcs), outs, n * n_slots, 0, [start, nothing, finish])


def _sibling_route(x, y, c, k):
    return 2 * k + 1 - c, (x, y, 1 - c)


def _chip_route(x, y, c, j):
    px, py = [(1 - x, y), (x, 1 - y), (1 - x, 1 - y)][j]
    return 2 * px + py, (px, py, c)


def _pair_sum(grads, recv4, c_idx, name, cols=False):
    _, p, r, cdim = recv4.shape
    tr = _tile(r, (512, 256, 176, 160, 128, 64, 32, 16))

    def body(c_ref, a_ref, b_ref, o_ref):
        o_ref[...] = (a_ref[...].astype(F32) + b_ref[...].astype(F32)).astype(o_ref.dtype)

    blk = (None, None, tr, cdim)
    if cols:
        own = pl.BlockSpec((None, tr, cdim), lambda k, q, i, c_ref: (q, i, 2 * k + c_ref[0]))
    else:
        own = pl.BlockSpec(blk, lambda k, q, i, c_ref: (q, 2 * k + c_ref[0], i, 0))
    grid_spec = pltpu.PrefetchScalarGridSpec(
        num_scalar_prefetch=1, grid=(4, p, r // tr),
        in_specs=[own, pl.BlockSpec(blk, lambda k, q, i, c_ref: (k, q, i, 0))],
        out_specs=pl.BlockSpec(blk, lambda k, q, i, c_ref: (q, k, i, 0)))
    return _pcall(body, name=name, grid_spec=grid_spec, out_shape=SDS((p, 4, r, cdim), grads.dtype),
                  compiler_params=_params(("parallel", "parallel", "parallel")))(c_idx, grads, recv4)


def _mm(name, ins, prods, n_acc, acc_shape, epi_idx, epilogue, out_shapes, out_specs, grid, dims, plan=None):
    n_in, n_out, nk = len(ins), len(out_shapes), grid[2]
    n_acc_refs = n_acc if nk > 1 else 0
    c_ins, c_outs = (plan.ins, plan.outs) if plan else ([], [])
    n_cin, n_cout = len(c_ins), len(c_outs)

    def body(*refs):
        in_refs, c_in = refs[:n_in], refs[n_in:n_in + n_cin]
        rest = refs[n_in + n_cin:]
        out_refs, c_out = rest[:n_out], rest[n_out:n_out + n_cout]
        rest = rest[n_out + n_cout:]
        acc_refs, sems = rest[:n_acc_refs], rest[n_acc_refs:]
        ids = [pl.program_id(axis) for axis in range(3)]
        if plan:
            @pl.when((ids[0] == 0) & (ids[1] == 0) & (ids[2] == 0))
            def _():
                plan.phases[0](c_in, c_out, *sems)

        def finish(accs):
            outs = epilogue(accs, [in_refs[i][...] for i in epi_idx])
            for o_ref, o in zip(out_refs, outs):
                if isinstance(o, tuple):
                    for plane, part in enumerate(o):
                        o_ref[plane] = part.astype(o_ref.dtype)
                else:
                    o_ref[...] = o.astype(o_ref.dtype)

        def operand(ref_idx):
            if isinstance(ref_idx, tuple):
                return in_refs[ref_idx[0]][ref_idx[1]]
            return in_refs[ref_idx][...]

        if nk == 1:
            accs = [None] * n_acc
            for ia, ib, iacc in prods:
                term = _dot(operand(ia), operand(ib), dims)
                accs[iacc] = term if accs[iacc] is None else accs[iacc] + term
            finish(accs)
        else:
            @pl.when(ids[2] == 0)
            def _():
                for acc in acc_refs:
                    acc[...] = jnp.zeros_like(acc)

            for ia, ib, iacc in prods:
                acc_refs[iacc][...] += _dot(operand(ia), operand(ib), dims)

            @pl.when(ids[2] == nk - 1)
            def _():
                finish([acc[...] for acc in acc_refs])

        if plan:
            @pl.when((ids[0] == grid[0] - 1) & (ids[1] == grid[1] - 1) & (ids[2] == nk - 1))
            def _():
                plan.phases[1](c_in, c_out, *sems)
                plan.phases[2](c_in, c_out, *sems)

    return _pcall(
        body, name=name, grid=grid, in_specs=[s for _, s in ins] + [ANY] * n_cin,
        out_specs=list(out_specs) + [ANY] * n_cout, out_shape=list(out_shapes) + list(c_outs),
        scratch_shapes=[pltpu.VMEM(acc_shape, F32) for _ in range(n_acc_refs)] + (plan.scratch() if plan else []),
        compiler_params=_params(("arbitrary",) * 3 if plan else ("parallel", "parallel", "arbitrary")),
    )(*[a for a, _ in ins], *c_ins)


def _plain(accs, _):
    return accs


def _mm_nn(name, a, b, out_dtype, extras=(), epilogue=_plain, n_out=1):
    m, kd = a.shape
    n = b.shape[1]
    tm, tn, tk = _tile(m, (1024, 512, 256, 128)), _tile(n, (640, 512, 256, 128)), _tile(kd, (1280, 1024, 512, 256, 128))
    ins = [(a, pl.BlockSpec((tm, tk), lambda i, j, k: (i, k))), (b, pl.BlockSpec((tk, tn), lambda i, j, k: (k, j)))]
    for arr, kind in extras:
        if kind == "tile":
            ins.append((arr, pl.BlockSpec((tm, tn), lambda i, j, k: (i, j))))
        else:
            ins.append((arr, pl.BlockSpec((1, tn), lambda i, j, k: (0, j))))
    dts = out_dtype if isinstance(out_dtype, (list, tuple)) else [out_dtype] * n_out
    return _mm(name, ins, [(0, 1, 0)], 1, (tm, tn), list(range(2, len(ins))), epilogue,
               [SDS((m, n), dt) for dt in dts], [pl.BlockSpec((tm, tn), lambda i, j, k: (i, j)) for _ in dts],
               (m // tm, n // tn, kd // tk), "nn")


def _mm_nt(name, a, b, out_dtype, plan=None):
    m, kd = a.shape
    n = b.shape[0]
    tm, tn, tk = _tile(m, (1024, 512, 256, 128)), _tile(n, (640, 512, 256, 128)), _tile(kd, (1024, 512, 256, 128))
    ins = [(a, pl.BlockSpec((tm, tk), lambda i, j, k: (i, k))), (b, pl.BlockSpec((tn, tk), lambda i, j, k: (j, k)))]
    outs = _mm(name, ins, [(0, 1, 0)], 1, (tm, tn), [], _plain, [SDS((m, n), out_dtype)],
               [pl.BlockSpec((tm, tn), lambda i, j, k: (i, j))], (m // tm, n // tn, kd // tk), "nt", plan=plan)
    return outs if plan else outs[0]


def _mm_tn(name, a, b, out_dtype):
    t, m = a.shape
    n = b.shape[1]
    tm, tn, tk = _tile(m, (640, 512, 256, 128)), _tile(n, (1024, 512, 256, 128)), t
    ins = [(a, pl.BlockSpec((tk, tm), lambda i, j, k: (k, i))), (b, pl.BlockSpec((tk, tn), lambda i, j, k: (k, j)))]
    return _mm(name, ins, [(0, 1, 0)], 1, (tm, tn), [], _plain, [SDS((m, n), out_dtype)],
               [pl.BlockSpec((tm, tn), lambda i, j, k: (i, j))], (m // tm, n // tn, t // tk), "tn")[0]


def _mm_nt_stack(name, a3, b, out_dtype):
    cc, m, kd = a3.shape
    n = b.shape[0]
    tm, tn, tk = _tile(m, (1024, 512, 256, 128)), _tile(n, (1024, 512, 256, 128)), _tile(kd, (1280, 1024, 512, 256, 128))
    nk = kd // tk
    ins = [(a3, pl.BlockSpec((None, tm, tk), lambda i, j, k: (k // nk, i, k % nk))),
           (b, pl.BlockSpec((tn, tk), lambda i, j, k: (j, k)))]
    return _mm(name, ins, [(0, 1, 0)], 1, (tm, tn), [], _plain, [SDS((m, n), out_dtype)],
               [pl.BlockSpec((tm, tn), lambda i, j, k: (i, j))], (m // tm, n // tn, cc * nk), "nt")[0]


def _mm_tn_stack(name, a, b3, out_dtype):
    t, m = a.shape
    cc, _, n = b3.shape
    tm, tn, tk = _tile(m, (512, 256, 128)), _tile(n, (1280, 1024, 512, 256, 128)), t
    nj = n // tn
    ins = [(a, pl.BlockSpec((tk, tm), lambda i, j, k: (k, i))),
           (b3, pl.BlockSpec((None, tk, tn), lambda i, j, k: (j // nj, k, j % nj)))]
    return _mm(name, ins, [(0, 1, 0)], 1, (tm, tn), [], _plain, [SDS((m, cc * n), out_dtype)],
               [pl.BlockSpec((tm, tn), lambda i, j, k: (i, j))], (m // tm, cc * nj, t // tk), "tn")[0]


def _chunks_to_cols(name, wc, eye2):
    nch, d, cw = wc.shape
    tm = _tile(d, (1024, 512, 256, 128))
    ins = [(wc, pl.BlockSpec((None, tm, cw), lambda i, j, k: (2 * j + k, i, 0))),
           (eye2, pl.BlockSpec((None, cw, 2 * cw), lambda i, j, k: (k, 0, 0)))]
    return _mm(name, ins, [(0, 1, 0)], 1, (tm, 2 * cw), [], _plain, [SDS((d, nch * cw), wc.dtype)],
               [pl.BlockSpec((tm, 2 * cw), lambda i, j, k: (i, j))], (d // tm, nch // 2, 2), "nn")[0]


def _cols_to_chunks(name, full, eye2):
    d, n = full.shape
    _, cw, _ = eye2.shape
    nch = n // cw
    tm = _tile(d, (1024, 512, 256, 128))
    ins = [(full, pl.BlockSpec((tm, 2 * cw), lambda i, j, k: (i, j // 2))),
           (eye2, pl.BlockSpec((None, cw, 2 * cw), lambda i, j, k: (j % 2, 0, 0)))]
    return _mm(name, ins, [(0, 1, 0)], 1, (tm, cw), [], _plain, [SDS((nch, d, cw), full.dtype)],
               [pl.BlockSpec((None, tm, cw), lambda i, j, k: (j, i, 0))], (d // tm, nch, 1), "nt")[0]


def _row_tile(t):
    return _tile(t, (256, 128, 64, 32, 16, 8))


def _norm_fwd(x, g, scale1p, shift, name):
    t, d = x.shape
    tr = _row_tile(t)

    def body(x_ref, g_ref, s_ref, b_ref, h_ref):
        xv = x_ref[...]
        inv = lax.rsqrt(jnp.mean(xv * xv, axis=-1, keepdims=True) + NORM_EPS)
        h_ref[...] = ((xv * inv) * g_ref[...] * s_ref[...] + b_ref[...]).astype(h_ref.dtype)

    vec = pl.BlockSpec((1, d), lambda i: (0, 0))
    return _pcall(body, name=name, grid=(t // tr,), in_specs=[pl.BlockSpec((tr, d), lambda i: (i, 0)), vec, vec, vec],
                  out_specs=pl.BlockSpec((tr, d), lambda i: (i, 0)), out_shape=SDS((t, d), BF16),
                  compiler_params=_params(("parallel",)))(x, g, scale1p, shift)


def _adaln_bwd(dh, x, y, dxo, g, scale1p, w_sub, gw_prev, name, plan=None):
    t, d = x.shape
    tr = _row_tile(t)

    def body(in_refs, out_refs, _):
        (dh_ref, x_ref, y_ref, dxo_ref, g_ref, s_ref, gw_ref), (dx_ref, dyp_ref, sums_ref) = in_refs, out_refs
        i = pl.program_id(0)

        @pl.when(i == 0)
        def _():
            sums_ref[...] = jnp.zeros_like(sums_ref)

        xv, dhv, dxov = x_ref[...], dh_ref[...], dxo_ref[...]
        inv = lax.rsqrt(jnp.mean(xv * xv, axis=-1, keepdims=True) + NORM_EPS)
        xn = xv * inv
        gv = g_ref[...]
        dn = dhv * s_ref[...]
        dxn = dn * gv
        dx = inv * (dxn - xn * jnp.mean(dxn * xn, axis=-1, keepdims=True)) + dxov
        dx_ref[...] = dx
        dyp_ref[...] = (gw_ref[...] * dx).astype(dyp_ref.dtype)
        sums_ref[0:1, :] += jnp.sum(dhv, axis=0, keepdims=True)
        sums_ref[1:2, :] += jnp.sum(dhv * (xn * gv), axis=0, keepdims=True)
        sums_ref[2:3, :] += jnp.sum(w_sub * y_ref[...] * dxov, axis=0, keepdims=True)
        sums_ref[3:4, :] += jnp.sum(dn * xn, axis=0, keepdims=True)

    blk = pl.BlockSpec((tr, d), lambda i: (i, 0))
    vec = pl.BlockSpec((1, d), lambda i: (0, 0))
    return _host_call(
        body, name, t // tr, [dh, x, y, dxo, g, scale1p, gw_prev], [blk, blk, blk, blk, vec, vec, vec],
        [SDS((t, d), F32), SDS((t, d), BF16), SDS((8, d), F32)], [blk, blk, pl.BlockSpec((8, d), lambda i: (0, 0))], [], plan)


def _loss_head(x, target, gf, gw_prev):
    t, d = x.shape
    tr = _row_tile(t)
    nt = t // tr

    def body(x_ref, tg_ref, g_ref, gw_ref, dx_ref, dyp_ref, sums_ref):
        i = pl.program_id(0)

        @pl.when(i == 0)
        def _():
            sums_ref[...] = jnp.zeros_like(sums_ref)

        xv = x_ref[...]
        inv = lax.rsqrt(jnp.mean(xv * xv, axis=-1, keepdims=True) + NORM_EPS)
        xn = xv * inv
        gv = g_ref[...]
        err = xn * gv - tg_ref[...]
        dyv = err * (1.0 / d)
        dxn = dyv * gv
        dx = inv * (dxn - xn * jnp.mean(dxn * xn, axis=-1, keepdims=True))
        dx_ref[...] = dx
        dyp_ref[...] = (gw_ref[...] * dx).astype(dyp_ref.dtype)
        sums_ref[0:1, :] += jnp.sum(dyv * xn, axis=0, keepdims=True)
        sums_ref[1:2, :] += jnp.sum(err * err, axis=0, keepdims=True)

        @pl.when(i == nt - 1)
        def _():
            tot = jnp.sum(sums_ref[1:2, :], axis=1, keepdims=True) * (0.5 / d)
            sums_ref[1:2, :] = jnp.broadcast_to(tot, (1, d))

    blk = pl.BlockSpec((tr, d), lambda i: (i, 0))
    vec = pl.BlockSpec((1, d), lambda i: (0, 0))
    return _pcall(
        body, name="loss_head", grid=(nt,), in_specs=[blk, blk, vec, vec],
        out_specs=[blk, blk, pl.BlockSpec((8, d), lambda i: (0, 0))],
        out_shape=[SDS((t, d), F32), SDS((t, d), BF16), SDS((8, d), F32)],
        compiler_params=_params(("arbitrary",)))(x, target, gf, gw_prev)


HIDDEN_CHUNKS = N_DEV // 2


def _ffn_tiles(t, d):
    return _tile(t, (1024, 512, 256, 128)), _tile(d, (1024, 512, 256, 128))


def _ffn_gu(name, h, wgu, plan=None):
    t, d = h.shape
    fc, nc = wgu.shape[3], HIDDEN_CHUNKS
    tm, _ = _ffn_tiles(t, d)

    def epi_gu(accs, _):
        gpre, up = accs
        return (gpre, up), gpre * _sigmoid(gpre) * up

    wblk = (None, None, d, fc)
    ins = [(h, pl.BlockSpec((tm, d), lambda i, c, k: (i, 0))),
           (wgu, pl.BlockSpec(wblk, lambda i, c, k: (0, c, 0, 0))),
           (wgu, pl.BlockSpec(wblk, lambda i, c, k: (0, c + nc, 0, 0)))]
    return _mm(name, ins, [(0, 1, 0), (0, 2, 1)], 2, (tm, fc), [], epi_gu,
               [SDS((2, nc, t, fc), BF16), SDS((nc, t, fc), BF16)],
               [pl.BlockSpec((2, None, tm, fc), lambda i, c, k: (0, c, i, 0)),
                pl.BlockSpec((None, tm, fc), lambda i, c, k: (c, i, 0))],
               (t // tm, nc, 1), "nn", plan=plan)


def _ffn_down(name, a, wd4, x, gmul, plan=None):
    nc, t, fc = a.shape
    d = wd4.shape[3]
    tm = _tile(t, (512, 256, 128))

    def epi_down(accs, ex):
        (yv,), (xv, gm) = accs, ex
        return yv, xv + MACARON_W * gm * yv

    ins = [(a, pl.BlockSpec((nc, tm, fc), lambda i, j, k: (0, i, 0))),
           (wd4, pl.BlockSpec((None, nc, fc, d), lambda i, j, k: (0, 0, 0, 0), pipeline_mode=pl.Buffered(1))),
           (x, pl.BlockSpec((tm, d), lambda i, j, k: (i, 0))), (gmul, pl.BlockSpec((1, d), lambda i, j, k: (0, 0)))]
    oblk = pl.BlockSpec((tm, d), lambda i, j, k: (i, 0))
    prods = [((0, (c,)), (1, (c,)), 0) for c in range(nc)]
    return _mm(name, ins, prods, 1, (tm, d), [2, 3], epi_down, [SDS((t, d), BF16), SDS((t, d), F32)],
               [oblk, oblk], (t // tm, 1, 1), "nn", plan=plan)


def _ffn_da(name, dy, wd4, gu2, plan=None):
    t, d = dy.shape
    _, nc, fc, _ = wd4.shape
    tm, _ = _ffn_tiles(t, d)

    def epi_da(accs, ex):
        (da,), (gu,) = accs, ex
        gpre, up = gu[0].astype(F32), gu[1].astype(F32)
        s = _sigmoid(gpre)
        silu = gpre * s
        dg = da * up * (s * (1.0 + gpre * (1.0 - s)))
        return ((dg, da * silu),)

    gblk = pl.BlockSpec((2, None, tm, fc), lambda i, c, k: (0, c, i, 0))
    ins = [(dy, pl.BlockSpec((tm, d), lambda i, c, k: (i, 0))),
           (wd4, pl.BlockSpec((None, None, fc, d), lambda i, c, k: (0, c, 0, 0))), (gu2, gblk)]
    return _mm(name, ins, [(0, 1, 0)], 1, (tm, fc), [2], epi_da, [SDS((2, nc, t, fc), BF16)], [gblk],
               (t // tm, nc, 1), "nt", plan=plan)


def _ffn_dwd(name, a, dy, plan=None):
    nc, t, fc = a.shape
    d = dy.shape[1]
    _, tn = _ffn_tiles(t, d)
    ins = [(a, pl.BlockSpec((None, t, fc), lambda c, j, k: (c, 0, 0))), (dy, pl.BlockSpec((t, tn), lambda c, j, k: (0, j)))]
    return _mm(name, ins, [(0, 1, 0)], 1, (fc, tn), [], _plain, [SDS((1, nc, fc, d), BF16)],
               [pl.BlockSpec((None, None, fc, tn), lambda c, j, k: (0, c, 0, j))], (nc, d // tn, 1), "tn", plan=plan)


def _ffn_dwgu(name, h, dgu2, plan=None):
    t, d = h.shape
    _, nc, _, fc = dgu2.shape
    _, tn = _ffn_tiles(t, d)
    ins = [(h, pl.BlockSpec((t, tn), lambda i, c, k: (0, i))),
           (dgu2, pl.BlockSpec((None, None, t, fc), lambda i, c, k: (c // nc, c % nc, 0, 0)))]
    return _mm(name, ins, [(0, 1, 0)], 1, (tn, fc), [], _plain, [SDS((1, 2 * nc, d, fc), BF16)],
               [pl.BlockSpec((None, None, tn, fc), lambda i, c, k: (0, c, i, 0))], (d // tn, 2 * nc, 1), "tn", plan=plan)


def _ffn_dh(name, dgu2, wgu, plan=None):
    _, nc, t, fc = dgu2.shape
    d = wgu.shape[2]
    tm = _tile(t, (512, 256, 128))
    ins = [(dgu2, pl.BlockSpec((2, nc, tm, fc), lambda i, j, k: (0, 0, i, 0))),
           (wgu, pl.BlockSpec((None, 2 * nc, d, fc), lambda i, j, k: (0, 0, 0, 0), pipeline_mode=pl.Buffered(1)))]
    prods = [((0, (s, c)), (1, (nc * s + c,)), 0) for s in range(2) for c in range(nc)]
    return _mm(name, ins, prods, 1, (tm, d), [], _plain, [SDS((t, d), F32)],
               [pl.BlockSpec((tm, d), lambda i, j, k: (i, 0))], (t // tm, 1, 1), "nt", plan=plan)


def _sb_block(t):
    return 256 if t >= 1024 else 128


SB_STRIP = 64


def _sb_strips(blk):
    strip = min(SB_STRIP, blk)
    row = lax.broadcasted_iota(jnp.int32, (strip, blk), 0)
    col = lax.broadcasted_iota(jnp.int32, (strip, blk), 1)
    return [(slice(r0, r0 + strip), col < row + r0) for r0 in range(0, blk, strip)]


def _host_call(core, name, steps, ins, in_specs, out_shapes, out_specs, scratch, plan):
    n_in, n_out, n_scr = len(ins), len(out_shapes), len(scratch)
    c_ins, c_outs = (plan.ins, plan.outs) if plan else ([], [])
    n_cin, n_cout = len(c_ins), len(c_outs)

    def body(*refs):
        in_refs, c_in = refs[:n_in], refs[n_in:n_in + n_cin]
        rest = refs[n_in + n_cin:]
        out_refs, c_out = rest[:n_out], rest[n_out:n_out + n_cout]
        rest = rest[n_out + n_cout:]
        scr, sems = rest[:n_scr], rest[n_scr:]
        step = pl.program_id(0)
        if plan:
            @pl.when(step == 0)
            def _():
                plan.phases[0](c_in, c_out, *sems)

        core(in_refs, out_refs, scr)
        if plan:
            @pl.when(step == steps - 1)
            def _():
                plan.phases[1](c_in, c_out, *sems)
                plan.phases[2](c_in, c_out, *sems)

    return _pcall(
        body, name=name, grid=(steps,), in_specs=list(in_specs) + [ANY] * n_cin,
        out_specs=list(out_specs) + [ANY] * n_cout, out_shape=list(out_shapes) + list(c_outs),
        scratch_shapes=list(scratch) + (plan.scratch() if plan else []),
        compiler_params=_params(("arbitrary",)))(*ins, *c_ins)


def _sb_fwd(qkv, d, plan=None):
    t = qkv.shape[0]
    blk = _sb_block(t)
    nq = t // blk
    npair = d // LANES
    scale = HEAD_DIM ** -0.5

    def body(in_refs, out_refs, scr):
        (q_ref, k_ref, v_ref), (o_ref, l_ref) = in_refs, out_refs
        tri_s = scr[0]
        hi_s, lo_s, w_s, zs_s = (scr[1 + 4 * k:5 + 4 * k] for k in range(4))
        lane = lax.broadcasted_iota(jnp.int32, (blk, LANES), 1)
        head0 = lane < HEAD_DIM
        row = lax.broadcasted_iota(jnp.int32, (blk, blk), 0)
        col = lax.broadcasted_iota(jnp.int32, (blk, blk), 1)
        tri_s[...] = (row > col).astype(BF16)
        strips = _sb_strips(blk)

        def step(qhs, kbs, maskeds, carries):
            chains = [(bi, hh) for bi in range(len(kbs)) for hh in range(2)]
            starts = [pl.multiple_of(kb * blk, blk) for kb in kbs]
            kvs = [k_ref[pl.ds(start, blk), :] for start in starts]
            vvs = [v_ref[pl.ds(start, blk), :] for start in starts]
            zs = [_dot(qhs[hh], kvs[bi], "nt") for bi, hh in chains]
            sums = []
            for c, (bi, hh) in enumerate(chains):
                parts = []
                for rows, causal in strips:
                    zt = zs[c][rows, :]
                    sp = _softplus(zt)
                    lk = jnp.where(causal, -sp, 0.0) if maskeds[bi] else -sp
                    hi = lk.astype(BF16)
                    hi_s[c][rows, :] = hi
                    lo_s[c][rows, :] = (lk - hi.astype(F32)).astype(BF16)
                    zs_s[c][rows, :] = zt - sp
                    parts.append(jnp.sum(lk, axis=1, keepdims=True))
                sums.append(jnp.concatenate(parts, axis=0))
            laters = [_dot(hi_s[c][...], tri_s[...], "nn") + _dot(lo_s[c][...], tri_s[...], "nn") for c in range(len(chains))]
            for c, (bi, hh) in enumerate(chains):
                cl = carries[hh][0]
                if bi == 1:
                    cl = cl + sums[hh]
                for rows, causal in strips:
                    logw = zs_s[c][rows, :] + laters[c][rows, :] + cl[rows, :]
                    if maskeds[bi]:
                        logw = jnp.where(causal, logw, -1e30)
                    w_s[c][rows, :] = jnp.exp(logw).astype(BF16)
            new = [list(carries[hh]) for hh in range(2)]
            for c, (bi, hh) in enumerate(chains):
                new[hh] = [new[hh][0] + sums[c], new[hh][1] + _dot(w_s[c][...], vvs[bi], "nn")]
            return tuple(tuple(cr) for cr in new)

        def qblock(qi, _):
            qstart = pl.multiple_of(qi * blk, blk)
            qv = q_ref[pl.ds(qstart, blk), :] * scale
            qhs = [jnp.where(head0 if hh == 0 else ~head0, qv, jnp.zeros_like(qv)) for hh in range(2)]
            zero = (jnp.zeros((blk, 1), F32), jnp.zeros((blk, LANES), F32))
            outs = lax.cond(qi % 2 == 1,
                            lambda crs: step(qhs, [qi, qi - 1], [True, False], crs),
                            lambda crs: step(qhs, [qi], [True], crs), (zero, zero))
            top = qi - 1 - qi % 2
            outs = lax.fori_loop(0, qi // 2, lambda j, crs: step(qhs, [top - 2 * j, top - 2 * j - 1], [False, False], crs),
                                 outs)
            o_ref[pl.ds(qstart, blk), :] = jnp.where(head0, outs[0][1], outs[1][1]).astype(o_ref.dtype)
            l_ref[pl.ds(qstart, blk), :] = jnp.where(head0, outs[0][0], outs[1][0])
            return 0

        lax.fori_loop(0, nq, qblock, 0)

    tile_bf16, tile_f32 = pltpu.VMEM((blk, blk), BF16), pltpu.VMEM((blk, blk), F32)
    return _host_call(
        body, "sb_fwd", npair, [qkv, qkv, qkv],
        [pl.BlockSpec((t, LANES), lambda p: (0, p)), pl.BlockSpec((t, LANES), lambda p: (0, npair + p)),
         pl.BlockSpec((t, LANES), lambda p: (0, 2 * npair + p))],
        [SDS((t, d), BF16), SDS((t, d), F32)],
        [pl.BlockSpec((t, LANES), lambda p: (0, p)), pl.BlockSpec((t, LANES), lambda p: (0, p))],
        [tile_bf16] * 13 + [tile_f32] * 4, plan)


def _sb_bwd(qkv, do, ltot, d, plan=None):
    t = qkv.shape[0]
    blk = _sb_block(t)
    nq = t // blk
    npair = d // LANES
    scale = HEAD_DIM ** -0.5

    def body(in_refs, out_refs, scr):
        (q_ref, k_ref, v_ref, do_ref, l_ref), (out_ref,) = in_refs, out_refs
        dq_s, dk_s, dv_s, upto_s, before_s = scr[:5]
        hi_s, lo_s, w_s, dab_s, dzs_s, zs_s, da_s = (scr[5 + 4 * k:9 + 4 * k] for k in range(7))
        lane = lax.broadcasted_iota(jnp.int32, (blk, LANES), 1)
        head0 = lane < HEAD_DIM
        row = lax.broadcasted_iota(jnp.int32, (blk, blk), 0)
        col = lax.broadcasted_iota(jnp.int32, (blk, blk), 1)
        upto_s[...] = (row <= col).astype(BF16)
        before_s[...] = (row < col).astype(BF16)
        dk_s[...] = jnp.zeros_like(dk_s)
        dv_s[...] = jnp.zeros_like(dv_s)
        strips = _sb_strips(blk)

        def step(heads, kbs, maskeds, carries):
            chains = [(bi, hh) for bi in range(len(kbs)) for hh in range(2)]
            starts = [pl.multiple_of(kb * blk, blk) for kb in kbs]
            kvs = [k_ref[pl.ds(start, blk), :] for start in starts]
            vvs = [v_ref[pl.ds(start, blk), :] for start in starts]
            zs = [_dot(heads[hh][0], kvs[bi], "nt") for bi, hh in chains]
            dws = [_dot(heads[hh][1], vvs[bi], "nt") for bi, hh in chains]
            lk_sums, da_sums = [], []
            for c, (bi, hh) in enumerate(chains):
                parts = []
                for rows, causal in strips:
                    zt = zs[c][rows, :]
                    sp = _softplus(zt)
                    lk = jnp.where(causal, -sp, 0.0) if maskeds[bi] else -sp
                    hi = lk.astype(BF16)
                    hi_s[c][rows, :] = hi
                    lo_s[c][rows, :] = (lk - hi.astype(F32)).astype(BF16)
                    zs_s[c][rows, :] = zt - sp
                    parts.append(jnp.sum(lk, axis=1, keepdims=True))
                lk_sums.append(jnp.concatenate(parts, axis=0))
            cums = [_dot(hi_s[c][...], upto_s[...], "nn") + _dot(lo_s[c][...], upto_s[...], "nn") for c in range(len(chains))]
            for c, (bi, hh) in enumerate(chains):
                lt, plk = heads[hh][2], carries[hh][0]
                if bi == 1:
                    plk = plk + lk_sums[hh]
                parts = []
                for rows, causal in strips:
                    logw = zs_s[c][rows, :] + (lt[rows, :] - (plk[rows, :] + cums[c][rows, :]))
                    if maskeds[bi]:
                        logw = jnp.where(causal, logw, -1e30)
                    w = jnp.exp(logw)
                    w_s[c][rows, :] = w.astype(BF16)
                    da = dws[c][rows, :] * w
                    da_s[c][rows, :] = da
                    dab_s[c][rows, :] = da.astype(BF16)
                    parts.append(jnp.sum(da, axis=1, keepdims=True))
                da_sums.append(jnp.concatenate(parts, axis=0))
            pres = [_dot(dab_s[c][...], before_s[...], "nn") for c in range(len(chains))]
            for c, (bi, hh) in enumerate(chains):
                pda = carries[hh][1]
                if bi == 1:
                    pda = pda + da_sums[hh]
                for rows, causal in strips:
                    sig = jnp.exp(zs_s[c][rows, :])
                    da = da_s[c][rows, :]
                    dz = da * (1.0 - sig) - sig * (pda[rows, :] + pres[c][rows, :])
                    if maskeds[bi]:
                        dz = jnp.where(causal, dz, 0.0)
                    dzs_s[c][rows, :] = dz.astype(BF16)
            new = [list(carries[hh]) for hh in range(2)]
            for c, (bi, hh) in enumerate(chains):
                dk_s[kbs[bi]] += _dot(heads[hh][3], dzs_s[c][...], "nn")
                dv_s[kbs[bi]] += _dot(heads[hh][4], w_s[c][...], "nn")
                new[hh] = [new[hh][0] + lk_sums[c], new[hh][1] + da_sums[c], new[hh][2] + _dot(dzs_s[c][...], kvs[bi], "nn")]
            return tuple(tuple(cr) for cr in new)

        def qblock(qi, _):
            qstart = pl.multiple_of(qi * blk, blk)
            qv = q_ref[pl.ds(qstart, blk), :] * scale
            dov = do_ref[pl.ds(qstart, blk), :]
            lv = l_ref[pl.ds(qstart, blk), :]
            heads = []
            for hh in range(2):
                sel = head0 if hh == 0 else ~head0
                qh, doh = jnp.where(sel, qv, jnp.zeros_like(qv)), jnp.where(sel, dov, jnp.zeros_like(dov))
                heads.append((qh, doh, jnp.max(jnp.where(sel, lv, -jnp.inf), axis=1, keepdims=True),
                              qh.astype(F32).T.astype(BF16), doh.astype(F32).T.astype(BF16)))
            zero = (jnp.zeros((blk, 1), F32), jnp.zeros((blk, 1), F32), jnp.zeros((blk, LANES), F32))
            carries = lax.fori_loop(0, qi // 2, lambda j, crs: step(heads, [2 * j, 2 * j + 1], [False, False], crs),
                                    (zero, zero))
            carries = lax.cond(qi % 2 == 1,
                               lambda crs: step(heads, [qi - 1, qi], [False, True], crs),
                               lambda crs: step(heads, [qi], [True], crs), carries)
            dq_s[pl.ds(qstart, blk), :] = jnp.where(head0, carries[0][2], carries[1][2]) * scale
            return 0

        lax.fori_loop(0, nq, qblock, 0)
        out_ref[0] = dq_s[...].astype(out_ref.dtype)
        for b in range(nq):
            out_ref[1, b * blk:(b + 1) * blk, :] = dk_s[b].T.astype(out_ref.dtype)
            out_ref[2, b * blk:(b + 1) * blk, :] = dv_s[b].T.astype(out_ref.dtype)

    col_blk = lambda off: pl.BlockSpec((t, LANES), lambda p: (0, off + p))
    return _host_call(
        body, "sb_bwd", npair, [qkv, qkv, qkv, do, ltot],
        [col_blk(0), col_blk(npair), col_blk(2 * npair), col_blk(0), col_blk(0)],
        [SDS((3, t, d), BF16)], [pl.BlockSpec((3, t, LANES), lambda p: (0, 0, p))],
        [pltpu.VMEM((t, LANES), F32)] + [pltpu.VMEM((nq, LANES, blk), F32) for _ in range(2)]
        + [pltpu.VMEM((blk, blk), BF16) for _ in range(2 + 20)]
        + [pltpu.VMEM((blk, blk), F32) for _ in range(8)], plan)


def _roll_rows(v, shift):
    return pltpu.roll(v, shift, 0)


def _shift_down(v, dist, fill, row):
    return jnp.where(row >= dist, _roll_rows(v, dist), fill)


def _shift_up(v, dist, fill, row):
    t = v.shape[0]
    return jnp.where(row < t - dist, _roll_rows(v, t - dist), fill)


def _lru_gates(xb, small, wr, wi, row):
    xs = [_shift_down(xb, 3 - tap, 0.0, row) if tap < 3 else xb for tap in range(4)]
    xc = small[4:5, :] + xs[0] * small[0:1, :]
    for tap in range(1, 4):
        xc = xc + xs[tap] * small[tap:tap + 1, :]
    xcb = xc.astype(BF16)
    r = _sigmoid(_dot(xcb, wr, "nn") + small[5:6, :])
    ig = _sigmoid(_dot(xcb, wi, "nn") + small[6:7, :])
    sp = _softplus(-small[7:8, :])
    la = -LRU_C * r * sp
    a = jnp.exp(la)
    th = jnp.tanh(la)
    m2 = -2.0 * th / (1.0 - th)
    return xs, xc, xcb, r, ig, sp, a, (jnp.sqrt(m2), m2)


def _gelu_parts(gate):
    inner = GELU_C * (gate + GELU_K * gate * gate * gate)
    th = jnp.tanh(inner)
    gelu = 0.5 * gate * (1.0 + th)
    dgelu = 0.5 * (1.0 + th) + 0.5 * gate * (1.0 - th * th) * GELU_C * (1.0 + 3.0 * GELU_K * gate * gate)
    return gelu, dgelu


def _scan_steps(t):
    steps, dist = [], 1
    while dist < t:
        steps.append(dist)
        dist *= 2
    return steps


SUBLANES = 8


def _linear_scan(a, b, scratch, row, reverse):
    a_s, b_s, carry_s = scratch
    t = a.shape[0]
    groups = t // SUBLANES
    in_group = row & (SUBLANES - 1)
    for dist in _scan_steps(SUBLANES):
        if reverse:
            inside = in_group < SUBLANES - dist
            b = b + a * jnp.where(inside, _roll_rows(b, t - dist), 0.0)
            a = a * jnp.where(inside, _roll_rows(a, t - dist), 1.0)
        else:
            inside = in_group >= dist
            b = a * jnp.where(inside, _roll_rows(b, dist), 0.0) + b
            a = a * jnp.where(inside, _roll_rows(a, dist), 1.0)
    a_s[...] = a
    b_s[...] = b
    end = 0 if reverse else SUBLANES - 1
    ends = pl.ds(end, groups, stride=SUBLANES)
    ae, be = a_s[ends, :], b_s[ends, :]
    grow = lax.broadcasted_iota(jnp.int32, ae.shape, 0)
    shift = _shift_up if reverse else _shift_down
    for dist in _scan_steps(groups):
        be = ae * shift(be, dist, 0.0, grow) + be
        ae = ae * shift(ae, dist, 1.0, grow)
    incoming = shift(be, 1, 0.0, grow)
    for k in range(SUBLANES):
        carry_s[pl.ds(k, groups, stride=SUBLANES), :] = incoming
    return a_s[...] * carry_s[...] + b_s[...]


def _lru_fwd(gx, small, wr, wi):
    t = gx.shape[0]
    r_dim = gx.shape[1] // 2
    nb = r_dim // LRU_BLOCK_W

    def body(gate_ref, xb_ref, small_ref, wr_ref, wi_ref, y_ref, hs_ref, *scratch):
        row = lax.broadcasted_iota(jnp.int32, (t, LRU_BLOCK_W), 0)
        xb = xb_ref[...]
        _, xc, _, _, ig, _, a, (mult, _) = _lru_gates(xb, small_ref, wr_ref[...], wi_ref[...], row)
        hsv = _linear_scan(a, mult * (ig * xc), scratch, row, reverse=False)
        hs_ref[...] = hsv
        gelu, _ = _gelu_parts(gate_ref[...])
        y_ref[...] = (gelu * hsv).astype(y_ref.dtype)

    colb = lambda off: pl.BlockSpec((t, LRU_BLOCK_W), lambda n: (0, off + n))
    wspec = pl.BlockSpec((None, LRU_BLOCK_W, LRU_BLOCK_W), lambda n: (n, 0, 0))
    return _pcall(
        body, name="lru_fwd", grid=(nb,),
        in_specs=[colb(0), colb(nb), pl.BlockSpec((8, LRU_BLOCK_W), lambda n: (0, n)), wspec, wspec],
        out_specs=[colb(0), colb(0)], out_shape=[SDS((t, r_dim), BF16), SDS((t, r_dim), F32)],
        scratch_shapes=[pltpu.VMEM((t, LRU_BLOCK_W), F32) for _ in range(3)],
        compiler_params=_params(("parallel",)))(gx, gx, small, wr, wi)


def _lru_bwd(gx, hs, dy, small, wr, wi, plan=None):
    t = gx.shape[0]
    r_dim = gx.shape[1] // 2
    nb = r_dim // LRU_BLOCK_W

    def body(in_refs, out_refs, scratch):
        (gate_ref, xb_ref, hs_ref, dy_ref, small_ref, wr_ref, wi_ref), (dgx_ref, dsm_ref, dwr_ref, dwi_ref) = in_refs, out_refs
        row = lax.broadcasted_iota(jnp.int32, (t, LRU_BLOCK_W), 0)
        xb, hsv, dyv, smallv = xb_ref[...], hs_ref[...], dy_ref[...], small_ref
        wrv, wiv = wr_ref[...], wi_ref[...]
        xs, xc, xcb, r, ig, sp, a, (mult, m2) = _lru_gates(xb, smallv, wrv, wiv, row)
        gelu, dgelu = _gelu_parts(gate_ref[...])
        dgx_ref[0] = (dyv * hsv * dgelu).astype(dgx_ref.dtype)
        dacc = _linear_scan(_shift_up(a, 1, 1.0, row), dyv * gelu, scratch, row, reverse=True)
        da = dacc * _shift_down(hsv, 1, 0.0, row)
        dmult = dacc * (ig * xc)
        dixc = dacc * mult
        dla = da * a - dmult * (a * a) * lax.rsqrt(m2)
        dr = dla * (-LRU_C * sp)
        dsp = jnp.sum(dla * (-LRU_C * r), axis=0, keepdims=True)
        dpr = dr * r * (1.0 - r)
        dpi = dixc * xc * ig * (1.0 - ig)
        dprb, dpib = dpr.astype(BF16), dpi.astype(BF16)
        dwr_ref[...] = _dot(xcb, dprb, "tn")
        dwi_ref[...] = _dot(xcb, dpib, "tn")
        dxc = dixc * ig + _dot(dprb, wrv, "nt") + _dot(dpib, wiv, "nt")
        dxb = dxc * smallv[3:4, :]
        for tap in range(3):
            dxb = dxb + _shift_up(dxc, 3 - tap, 0.0, row) * smallv[tap:tap + 1, :]
        dgx_ref[1] = dxb.astype(dgx_ref.dtype)
        lam = smallv[7:8, :]
        rows = [jnp.sum(dxc * xs[tap], axis=0, keepdims=True) for tap in range(4)]
        rows.append(jnp.sum(dxc, axis=0, keepdims=True))
        rows.append(jnp.sum(dpr, axis=0, keepdims=True))
        rows.append(jnp.sum(dpi, axis=0, keepdims=True))
        rows.append(-dsp * _sigmoid(-lam))
        for k, rv in enumerate(rows):
            dsm_ref[k:k + 1, :] = rv

    colb = lambda off: pl.BlockSpec((t, LRU_BLOCK_W), lambda n: (0, off + n))
    wspec = pl.BlockSpec((None, LRU_BLOCK_W, LRU_BLOCK_W), lambda n: (n, 0, 0))
    sspec = pl.BlockSpec((8, LRU_BLOCK_W), lambda n: (0, n))
    return _host_call(
        body, "lru_bwd", nb, [gx, gx, hs, dy, small, wr, wi],
        [colb(0), colb(nb), colb(0), colb(0), sspec, wspec, wspec],
        [SDS((2, t, r_dim), BF16), SDS((8, r_dim), F32), SDS((nb, LRU_BLOCK_W, LRU_BLOCK_W), F32),
         SDS((nb, LRU_BLOCK_W, LRU_BLOCK_W), F32)],
        [pl.BlockSpec((2, t, LRU_BLOCK_W), lambda n: (0, 0, n)), sspec, wspec, wspec],
        [pltpu.VMEM((t, LRU_BLOCK_W), F32) for _ in range(3)], plan)


def _adam(w, g, m, v):
    m2 = ADAM_B1 * m + (1.0 - ADAM_B1) * g
    v2 = ADAM_B2 * v + (1.0 - ADAM_B2) * (g * g)
    m_hat = m2 / (1.0 - ADAM_B1 ** ADAM_STEP)
    v_hat = v2 / (1.0 - ADAM_B2 ** ADAM_STEP)
    return -ADAM_LR * (m_hat / (jnp.sqrt(v_hat) + ADAM_EPS) + ADAM_WD * w), m2, v2


def _mod_fwd(c_all, mod_w, mod_b_cols):
    nl, d, cols = mod_w.shape
    nbatch = c_all.shape[0]

    def body(c_ref, w_ref, b_ref, o_ref):
        cv = c_ref[...]
        ca = (cv * _sigmoid(cv)).astype(BF16)
        o_ref[...] = _dot(ca, w_ref[...].astype(BF16), "nn") + b_ref[...]

    return _pcall(
        body, name="mod_fwd", grid=(nl,),
        in_specs=[pl.BlockSpec((nbatch, d), lambda l: (0, 0)), pl.BlockSpec((None, d, cols), lambda l: (l, 0, 0)),
                  pl.BlockSpec((None, 1, cols), lambda l: (l, 0, 0))],
        out_specs=pl.BlockSpec((None, nbatch, cols), lambda l: (l, 0, 0)), out_shape=SDS((nl, nbatch, cols), F32),
        compiler_params=_params(("parallel",)))(c_all, mod_w, mod_b_cols)


def _mod_w_update(c_all, dmod_cols, w, m, v):
    nl, d, cols = w.shape
    nbatch = c_all.shape[0]
    tr = _tile(d, (256, 128))

    def body(c_ref, dm_ref, w_ref, m_ref, v_ref, g_ref, dl_ref, m2_ref, v2_ref):
        cv = c_ref[...]
        ca = (cv * _sigmoid(cv)).astype(BF16)
        g = _dot(ca, dm_ref[...].astype(BF16), "tn")
        g_ref[...] = g
        dl_ref[...], m2_ref[...], v2_ref[...] = _adam(w_ref[...], g, m_ref[...], v_ref[...])

    wblk = pl.BlockSpec((None, tr, cols), lambda l, i: (l, i, 0))
    return _pcall(
        body, name="mod_w_update", grid=(nl, d // tr),
        in_specs=[pl.BlockSpec((nbatch, tr), lambda l, i: (0, i)), pl.BlockSpec((None, nbatch, cols), lambda l, i: (l, 0, 0)),
                  wblk, wblk, wblk],
        out_specs=[wblk] * 4, out_shape=[SDS(w.shape, F32)] * 4,
        compiler_params=_params(("parallel", "parallel")))(c_all, dmod_cols, w, m, v)


def _adam_update(name, w, m, v, gparts):
    rows, cols = w.shape
    tr = _tile(rows, (256, 128, 64, 32, 16, 8))
    npart = len(gparts)

    def body(*refs):
        w_ref, m_ref, v_ref = refs[:3]
        g_refs = refs[3:3 + npart]
        g_ref, dl_ref, m2_ref, v2_ref = refs[3 + npart:]
        g = g_refs[0][...].astype(F32)
        for gr in g_refs[1:]:
            g = g + gr[...].astype(F32)
        g_ref[...] = g
        dl_ref[...], m2_ref[...], v2_ref[...] = _adam(w_ref[...], g, m_ref[...], v_ref[...])

    blk = pl.BlockSpec((tr, cols), lambda i: (i, 0))
    return _pcall(body, name=name, grid=(rows // tr,), in_specs=[blk] * (3 + npart), out_specs=[blk] * 4,
                  out_shape=[SDS((rows, cols), F32)] * 4, compiler_params=_params(("parallel",)))(w, m, v, *gparts)


def _adam_shard(name, w, m, v, part4, recv3, chip_idx, first=0, fills=None):
    p, r, cdim = w.shape
    pg = part4.shape[0]
    tr = _tile(r, (256, 176, 160, 128, 64, 32, 16))

    def body(chip_ref, w_ref, m_ref, v_ref, own_ref, r0_ref, r1_ref, r2_ref, *rest):
        g_ref, dl_ref, m2_ref, v2_ref = rest[-4:]
        g = own_ref[...].astype(F32) + r0_ref[...].astype(F32) + r1_ref[...].astype(F32) + r2_ref[...].astype(F32)
        g_ref[...] = g
        dl_ref[...], m2_ref[...], v2_ref[...] = _adam(w_ref[...], g, m_ref[...], v_ref[...])

    blk = pl.BlockSpec((None, tr, cdim), lambda q, i, chip_ref: (first + q, i, 0))
    blk4 = (None, None, tr, cdim)
    slot = lambda s: pl.BlockSpec(blk4, lambda q, i, chip_ref: (s, q, i, 0))
    fills = list(fills or [])
    grid_spec = pltpu.PrefetchScalarGridSpec(
        num_scalar_prefetch=1, grid=(pg, r // tr),
        in_specs=[blk, blk, blk, pl.BlockSpec(blk4, lambda q, i, chip_ref: (q, chip_ref[0], i, 0)), slot(0), slot(1), slot(2)]
        + [ANY] * len(fills),
        out_specs=[blk] * 4)
    return _pcall(body, name=name, grid_spec=grid_spec, out_shape=[SDS((p, r, cdim), F32)] * 4,
                  input_output_aliases={8 + k: k for k in range(len(fills))},
                  compiler_params=_params(("parallel", "parallel")))(chip_idx, w, m, v, part4, recv3, recv3, recv3, *fills)


def _sum_devices(gathered, name):
    _, rows, cols = gathered.shape
    tr = _tile(rows, (512, 256, 128, 64, 32, 16, 8))

    def body(g_ref, o_ref):
        acc = g_ref[0].astype(F32)
        for k in range(1, N_DEV):
            acc = acc + g_ref[k].astype(F32)
        o_ref[...] = acc

    return _pcall(body, name=name, grid=(rows // tr,), in_specs=[pl.BlockSpec((N_DEV, tr, cols), lambda i: (0, i, 0))],
                  out_specs=pl.BlockSpec((tr, cols), lambda i: (i, 0)), out_shape=SDS((rows, cols), F32),
                  compiler_params=_params(("parallel",)))(gathered)


def _pack_flat(parts, width, row_mult, dtype):
    flat = jnp.concatenate([p.reshape(-1).astype(dtype) for p in parts])
    unit = width * row_mult
    pad = (-flat.shape[0]) % unit
    if pad:
        flat = jnp.concatenate([flat, jnp.zeros((pad,), dtype)])
    return flat.reshape(-1, width)


def _unpack_flat(flat, shapes):
    out, off = [], 0
    for shp in shapes:
        size = math.prod(shp)
        out.append(flat[off:off + size].reshape(shp))
        off += size
    return out


def kernel(x, c, mod_w, mod_b, norm_g, ffn_w_gu, ffn_w_down, sb_w_qkv, sb_w_o, lru_w_in, lru_conv_w, lru_conv_b, lru_w_r, lru_b_r, lru_w_i, lru_b_i, lru_lambda, lru_w_out, final_norm_g, loss_target, m_mod_w, m_mod_b, m_norm_g, m_ffn_w_gu, m_ffn_w_down, m_sb_w_qkv, m_sb_w_o, m_lru_w_in, m_lru_conv_w, m_lru_conv_b, m_lru_w_r, m_lru_b_r, m_lru_w_i, m_lru_b_i, m_lru_lambda, m_lru_w_out, m_final_norm_g, v_mod_w, v_mod_b, v_norm_g, v_ffn_w_gu, v_ffn_w_down, v_sb_w_qkv, v_sb_w_o, v_lru_w_in, v_lru_conv_w, v_lru_conv_b, v_lru_w_r, v_lru_b_r, v_lru_w_i, v_lru_b_i, v_lru_lambda, v_lru_w_out, v_final_norm_g):
    weights = dict(mod_w=mod_w, mod_b=mod_b, norm_g=norm_g, ffn_w_gu=ffn_w_gu, ffn_w_down=ffn_w_down, sb_w_qkv=sb_w_qkv,
                   sb_w_o=sb_w_o, lru_w_in=lru_w_in, lru_conv_w=lru_conv_w, lru_conv_b=lru_conv_b, lru_w_r=lru_w_r,
                   lru_b_r=lru_b_r, lru_w_i=lru_w_i, lru_b_i=lru_b_i, lru_lambda=lru_lambda, lru_w_out=lru_w_out,
                   final_norm_g=final_norm_g)
    mom_m = dict(mod_w=m_mod_w, mod_b=m_mod_b, norm_g=m_norm_g, ffn_w_gu=m_ffn_w_gu, ffn_w_down=m_ffn_w_down,
                 sb_w_qkv=m_sb_w_qkv, sb_w_o=m_sb_w_o, lru_w_in=m_lru_w_in, lru_conv_w=m_lru_conv_w,
                 lru_conv_b=m_lru_conv_b, lru_w_r=m_lru_w_r, lru_b_r=m_lru_b_r, lru_w_i=m_lru_w_i, lru_b_i=m_lru_b_i,
                 lru_lambda=m_lru_lambda, lru_w_out=m_lru_w_out, final_norm_g=m_final_norm_g)
    mom_v = dict(mod_w=v_mod_w, mod_b=v_mod_b, norm_g=v_norm_g, ffn_w_gu=v_ffn_w_gu, ffn_w_down=v_ffn_w_down,
                 sb_w_qkv=v_sb_w_qkv, sb_w_o=v_sb_w_o, lru_w_in=v_lru_w_in, lru_conv_w=v_lru_conv_w,
                 lru_conv_b=v_lru_conv_b, lru_w_r=v_lru_w_r, lru_b_r=v_lru_b_r, lru_w_i=v_lru_w_i, lru_b_i=v_lru_b_i,
                 lru_lambda=v_lru_lambda, lru_w_out=v_lru_w_out, final_norm_g=v_final_norm_g)
    names = list(weights)

    t, d = x.shape[1], x.shape[2]
    n_layers = mod_w.shape[0]
    r_dim = lru_w_out.shape[1] * N_DEV
    ng, rs = d // N_DEV, r_dim // N_DEV
    mod_cols = mod_w.shape[2]
    nblk = lru_w_r.shape[1]
    xi, yi, ci = _mesh_pos()
    me = 4 * xi + 2 * yi + ci
    chip = 2 * xi + yi
    x2, target = x.reshape(t, d), loss_target.reshape(t, d)

    lru_small_shard = jnp.concatenate([lru_conv_w[0], lru_conv_b, lru_b_r, lru_b_i, lru_lambda], axis=0)
    small1 = _pack_flat([c, norm_g, lru_small_shard], LANES, 8, F32)
    n_small1 = small1.shape[0]
    all1 = _allgather(small1[None], "gather_small").reshape(N_DEV, n_small1 * LANES)
    c_all = all1[:, :d]
    norm_full = jnp.transpose(all1[:, d:d + 6 * ng].reshape(N_DEV, n_layers, 3, ng), (1, 2, 0, 3)).reshape(n_layers, 3, d)
    lru_small = jnp.transpose(all1[:, d + 6 * ng:d + 6 * ng + 8 * rs].reshape(N_DEV, 8, rs), (1, 0, 2)).reshape(8, r_dim)

    mod_b_cols = lax.dynamic_slice_in_dim(mod_b, me * mod_cols, mod_cols, axis=1).reshape(n_layers, 1, mod_cols)
    mod_part = _mod_fwd(c_all, mod_w, mod_b_cols)

    assert sb_w_qkv.shape[0] == 1 and lru_w_in.shape[0] == 1, "one stick-breaking and one RG-LRU layer"
    n_ffn = 2 * n_layers
    fc = ffn_w_gu.shape[3]
    cw_in = lru_w_in.shape[2]
    pieces = {("ffn_w_gu", q): ffn_w_gu[q // 2, q % 2][None] for q in range(n_ffn)}
    pieces.update({("ffn_w_down", q): ffn_w_down[q // 2, q % 2][None] for q in range(n_ffn)})
    pieces.update({("sb_w_qkv", 0): sb_w_qkv, ("sb_w_o", 0): sb_w_o, ("lru_w_in", 0): lru_w_in, ("lru_w_out", 0): lru_w_out})
    col_window = {("sb_w_qkv", 0)}
    first = [("ffn_w_gu", 0)]
    behind = {"l0s0_gu": [("ffn_w_down", 0)], "l0s0_down": [("sb_w_qkv", 0), ("sb_w_o", 0)],
              "l0s2_gu": [("ffn_w_down", n_ffn - 1)], "l0s2_down": [("ffn_w_down", 2)],
              "l1s0_gu": [("lru_w_in", 0)], "l1s0_down": [("lru_w_out", 0)]}
    behind["sb_fwd"] = [key for key in pieces if key not in first + sum(behind.values(), [])]
    gathered = {}

    def gather_plan(keys):
        return _gather_plan([pieces[key].astype(BF16) for key in keys], [key in col_window for key in keys])

    def hosting(name, call):
        keys = behind.get(name, [])
        outs = call(gather_plan(keys) if keys else None)
        gathered.update(zip(keys, outs[len(outs) - len(keys):]))
        return outs[:len(outs) - len(keys)]

    mod_all, *landed = _run_comm(_merge_plans([_gather_plan([mod_part], [False]), gather_plan(first)]), "gather_mod_and_first")
    gathered.update(zip(first, landed))
    mod_mine = lax.dynamic_index_in_dim(mod_all, me, axis=2, keepdims=False)
    mod_mine = mod_mine.reshape(n_layers, 3, 3, d)
    wr_b, wi_b = lru_w_r[0].astype(BF16), lru_w_i[0].astype(BF16)
    eye2 = jnp.eye(2 * cw_in, dtype=BF16).reshape(2, cw_in, 2 * cw_in)

    def w_gu(q):
        return gathered[("ffn_w_gu", q)]

    def w_d4(q):
        return gathered[("ffn_w_down", q)].reshape(1, HIDDEN_CHUNKS, fc, d)

    saved = []
    xcur = x2
    for layer in range(n_layers):
        for sub in range(3):
            gvec = norm_full[layer, sub].reshape(1, d)
            shift = mod_mine[layer, sub, 0].reshape(1, d)
            scale1p = 1.0 + mod_mine[layer, sub, 1].reshape(1, d)
            gmul = 1.0 + mod_mine[layer, sub, 2].reshape(1, d)
            tag = f"l{layer}s{sub}"
            h = _norm_fwd(xcur, gvec, scale1p, shift, tag + "_norm")
            rec = dict(x=xcur, h=h, g=gvec, scale1p=scale1p, gmul=gmul, w=MACARON_W if sub != 1 else 1.0)
            if sub != 1:
                lj = layer * 2 + sub // 2
                gu2, a = hosting(tag + "_gu", lambda plan: _ffn_gu(tag + "_gu", h, w_gu(lj), plan))
                yv, xcur = hosting(tag + "_down", lambda plan: _ffn_down(tag + "_down", a, w_d4(lj), xcur, gmul, plan))
                rec.update(kind="ffn", lj=lj, gu2=gu2, a=a, y=yv)
            elif layer % 2 == 0:
                w_qkv = gathered[("sb_w_qkv", 0)][0]
                w_o = gathered[("sb_w_o", 0)].reshape(d, d)
                qkv = _mm_nn(tag + "_qkv", h, w_qkv, BF16)[0]
                o, ltot = hosting("sb_fwd", lambda plan: _sb_fwd(qkv, d, plan))
                yv, xcur = _mm_nn(tag + "_wo", o, w_o, [BF16, F32], extras=[(xcur, "tile"), (gmul, "row")],
                                  epilogue=lambda accs, ex: (accs[0], ex[0] + ex[1] * accs[0]))
                rec.update(kind="sb", qkv=qkv, o=o, ltot=ltot, y=yv, w_qkv=w_qkv, w_o=w_o)
            else:
                w_in = _chunks_to_cols("lru_w_in_cols", gathered[("lru_w_in", 0)][0], eye2)
                w_out = gathered[("lru_w_out", 0)].reshape(r_dim, d)
                gx = _mm_nn(tag + "_win", h, w_in, F32)[0]
                ymix, hs = _lru_fwd(gx, lru_small, wr_b, wi_b)
                yv, xcur = _mm_nn(tag + "_wout", ymix, w_out, [BF16, F32], extras=[(xcur, "tile"), (gmul, "row")],
                                  epilogue=lambda accs, ex: (accs[0], ex[0] + ex[1] * accs[0]))
                rec.update(kind="lru", gx=gx, hs=hs, ymix=ymix, y=yv, w_in=w_in, w_out=w_out)
            saved.append(rec)

    last = saved[-1]
    dxo, dy, head_sums = _loss_head(xcur, target, final_norm_g.reshape(1, d), (last["w"] * last["gmul"]))
    loss_mine = head_sums[1, 0:1]
    dgf = head_sums[0]

    c_idx = jnp.reshape(ci, (1,)).astype(jnp.int32)
    chip_idx = jnp.reshape(chip, (1,)).astype(jnp.int32)
    grads, reduced = {}, {}
    to_pair = []
    to_chips = []

    def sibling_plan(only=None):
        keys = [key for key in to_pair if only is None or key in only]
        if not keys:
            return None, keys
        return _exchange_plan([grads[key] for key in keys], [key in col_window for key in keys], 4, _sibling_route), keys

    def sibling_done(keys, recv4):
        for key, r4 in zip(keys, recv4):
            to_pair.remove(key)
            to_chips.append((key, _pair_sum(grads[key], r4, c_idx, f"rs_pair_sum_{key[0]}{key[1]}", cols=key in col_window)))

    def chip_plan(only=None):
        items = [item for item in to_chips if only is None or item[0] in only]
        if not items:
            return None, items
        return _exchange_plan([p4 for _, p4 in items], [False] * len(items), 3, _chip_route), items

    def chips_done(items, recv3):
        for item, r3 in zip(items, recv3):
            to_chips.remove(item)
            reduced[item[0]] = (item[1], r3)

    def behind(call, make_plan, done, more=None):
        plan, items = make_plan()
        n_mine = len(plan.outs) if plan else 0
        n_more = len(more.outs) if more else 0
        outs = call(_merge_plans([plan, more]))
        n_own = len(outs) - n_mine - n_more
        done(items, outs[n_own:n_own + n_mine])
        return list(outs[:n_own]) + list(outs[n_own + n_mine:])

    carried = {
        "l1s1b_dymix": ("sibling", None), "lru_bwd": ("chips", [("ffn_w_gu", n_ffn - 1)]),
        "l1s0b_da": ("sibling", None), "l1s0b_dwgu": ("chips", [("ffn_w_down", n_ffn - 1)]),
        "l0s2b_da": ("sibling", None),
    }

    def carrying(name, call):
        if name not in carried:
            return call(None)
        stage, only = carried[name]
        if stage == "sibling":
            return behind(call, functools.partial(sibling_plan, only), sibling_done)
        return behind(call, functools.partial(chip_plan, only), chips_done)

    def at_once(make_plan, done, name):
        plan, items = make_plan()
        if plan:
            done(items, _run_comm(plan, name))

    def add_grad(key, value):
        grads[key] = value
        to_pair.append(key)

    dmod = [[None] * 3 for _ in range(n_layers)]
    dnorm = [[None] * 3 for _ in range(n_layers)]
    dlru_small = wri_all = None
    for idx in reversed(range(len(saved))):
        rec = saved[idx]
        layer, sub = divmod(idx, 3)
        tag = f"l{layer}s{sub}b"
        if rec["kind"] == "ffn" and idx > 0:
            lj = rec["lj"]
            (dgu2,) = carrying(tag + "_da", lambda plan: _ffn_da(tag + "_da", dy, w_d4(lj), rec["gu2"], plan))
            dwd = _ffn_dwd(tag + "_dwd", rec["a"], dy)[0].reshape(gathered[("ffn_w_down", lj)].shape)
            (dwgu,) = carrying(tag + "_dwgu", lambda plan: _ffn_dwgu(tag + "_dwgu", rec["h"], dgu2, plan))
            (dh,) = carrying(tag + "_dh", lambda plan: _ffn_dh(tag + "_dh", dgu2, w_gu(lj), plan))
            add_grad(("ffn_w_down", lj), dwd)
            add_grad(("ffn_w_gu", lj), dwgu)
        elif rec["kind"] == "ffn":
            lj = rec["lj"]
            at_once(sibling_plan, sibling_done, "rs_sibling_" + tag)
            (dgu2,) = behind(lambda plan: _ffn_da(tag + "_da", dy, w_d4(lj), rec["gu2"], plan), chip_plan, chips_done)
            add_grad(("ffn_w_down", lj), _ffn_dwd(tag + "_dwd", rec["a"], dy)[0].reshape(gathered[("ffn_w_down", lj)].shape))
            at_once(sibling_plan, sibling_done, "rs_sibling_" + tag + "_dwd")
            (dwgu,) = behind(lambda plan: _ffn_dwgu(tag + "_dwgu", rec["h"], dgu2, plan), chip_plan, chips_done)
            add_grad(("ffn_w_gu", lj), dwgu)
            at_once(sibling_plan, sibling_done, "rs_sibling_" + tag + "_dwgu")
            (dh,) = behind(lambda plan: _ffn_dh(tag + "_dh", dgu2, w_gu(lj), plan), chip_plan, chips_done)
        elif rec["kind"] == "sb":
            at_once(sibling_plan, sibling_done, "rs_sibling_" + tag)
            do = _mm_nt(tag + "_do", dy, rec["w_o"], BF16)
            dwo = _mm_tn(tag + "_dwo", rec["o"], dy, BF16)
            wri = _pack_flat([dwr, dwi], LANES, 512, BF16)[None]
            dqkv3, wri_all = behind(lambda plan: _sb_bwd(rec["qkv"], do, rec["ltot"], d, plan), chip_plan, chips_done,
                                    more=_gather_plan([wri], [False]))
            add_grad(("sb_w_o", 0), dwo.reshape(gathered[("sb_w_o", 0)].shape))
            dh = _mm_nt_stack(tag + "_dh", dqkv3, rec["w_qkv"], F32)
            add_grad(("sb_w_qkv", 0), _mm_tn_stack(tag + "_dwqkv", rec["h"], dqkv3, BF16)[None])
        else:
            (dymix,) = carrying(tag + "_dymix", lambda plan: _mm_nt(tag + "_dymix", dy, rec["w_out"], F32, plan)
                                if plan else [_mm_nt(tag + "_dymix", dy, rec["w_out"], F32)])
            dwout = _mm_tn(tag + "_dwout", rec["ymix"], dy, BF16).reshape(gathered[("lru_w_out", 0)].shape)
            dgx2, dlru_small, dwr, dwi = carrying("lru_bwd", lambda plan: _lru_bwd(rec["gx"], rec["hs"], dymix, lru_small,
                                                                                wr_b, wi_b, plan))
            dh = _mm_nt_stack(tag + "_dh", dgx2, rec["w_in"], F32)
            dw_in = _mm_tn_stack(tag + "_dwin", rec["h"], dgx2, BF16)
            add_grad(("lru_w_out", 0), dwout)
            add_grad(("lru_w_in", 0), _cols_to_chunks("lru_w_in_chunks", dw_in, eye2)[None])
        prev = saved[idx - 1] if idx > 0 else None
        gw_prev = (prev["w"] * prev["gmul"]) if prev is not None else jnp.zeros((1, d), F32)
        dxo, dy, sums = _adaln_bwd(dh, rec["x"], rec["y"], dxo, rec["g"], rec["scale1p"], rec["w"], gw_prev, tag + "_adaln")
        dmod[layer][sub] = sums[0:3]
        dnorm[layer][sub] = sums[3]
    grad_x = dxo.reshape(x.shape)

    dmod_mine = jnp.stack([jnp.stack(dmod[layer]) for layer in range(n_layers)])
    dnorm_mine = jnp.stack([jnp.stack(dnorm[layer]) for layer in range(n_layers)])
    assert not to_pair and not to_chips
    small_shapes = [(n_layers, 9 * d), (n_layers, 3, d), (8, r_dim), (d,), (1,)]
    small3 = _pack_flat([dmod_mine, dnorm_mine, dlru_small, dgf, loss_mine], LANES, 256, F32)
    n_small3 = small3.shape[0]
    all3 = _allgather(small3[None], "gather_small_grads").reshape(N_DEV, n_small3, LANES)
    gsum = _sum_devices(all3, "sum_small_grads").reshape(-1)
    g_mod_b, g_norm_full, g_lru_small, g_final, loss_sum = _unpack_flat(gsum, small_shapes)
    loss = loss_sum[0]
    wri_sum = _sum_devices(wri_all.reshape(N_DEV, -1, LANES), "sum_gate_weight_grads").reshape(-1)
    g_wr, g_wi = _unpack_flat(wri_sum, [lru_w_r.shape, lru_w_i.shape])
    dmod_all = all3.reshape(N_DEV, -1)[:, :n_layers * 9 * d].reshape(N_DEV, n_layers, N_DEV, mod_cols)
    dmod_cols = jnp.transpose(lax.dynamic_index_in_dim(dmod_all, me, axis=2, keepdims=False), (1, 0, 2))

    out_g, out_d, out_m, out_v = {}, {}, {}, {}
    out_g["mod_w"], out_d["mod_w"], out_m["mod_w"], out_v["mod_w"] = _mod_w_update(c_all, dmod_cols, mod_w, m_mod_w, v_mod_w)

    g_norm_shard = lax.dynamic_slice_in_dim(g_norm_full, me * ng, ng, axis=2)
    g_lru_shard = lax.dynamic_slice_in_dim(g_lru_small, me * rs, rs, axis=1)
    small_grads = dict(mod_b=g_mod_b, norm_g=g_norm_shard, lru_conv_w=g_lru_shard[0:4].reshape(lru_conv_w.shape),
                       lru_conv_b=g_lru_shard[4:5], lru_b_r=g_lru_shard[5:6], lru_b_i=g_lru_shard[6:7],
                       lru_lambda=g_lru_shard[7:8], final_norm_g=g_final)
    for n, g in (("lru_w_r", g_wr), ("lru_w_i", g_wi)):
        view = lambda arr: arr.reshape(-1, LRU_BLOCK_W)
        outs = _adam_update("adam_" + n, view(weights[n]), view(mom_m[n]), view(mom_v[n]), [view(g)])
        out_g[n], out_d[n], out_m[n], out_v[n] = [o.reshape(weights[n].shape) for o in outs]
    small_names = list(small_grads)
    sw = _pack_flat([weights[n] for n in small_names], LANES, 256, F32)
    sg = _pack_flat([small_grads[n] for n in small_names], LANES, 256, F32)
    sm = _pack_flat([mom_m[n] for n in small_names], LANES, 256, F32)
    sv = _pack_flat([mom_v[n] for n in small_names], LANES, 256, F32)
    s_outs = _adam_update("adam_small", sw, sm, sv, [sg])
    small_shapes2 = [weights[n].shape for n in small_names]
    for dst, flat in zip((out_g, out_d, out_m, out_v), s_outs):
        for n, arr in zip(small_names, _unpack_flat(flat.reshape(-1), small_shapes2)):
            dst[n] = arr

    for n in ["ffn_w_gu", "ffn_w_down", "sb_w_qkv", "sb_w_o", "lru_w_in", "lru_w_out"]:
        shp = weights[n].shape
        shard3 = (math.prod(shp[:-2]),) + shp[-2:]
        view = lambda arr: arr.reshape(shard3)
        outs = None
        for q in range(shard3[0]):
            fills = outs if outs is not None else [lax.empty(shard3, F32) for _ in range(4)]
            p4, r3 = reduced[(n, q)]
            outs = _adam_shard(f"adam_{n}{q}", view(weights[n]), view(mom_m[n]), view(mom_v[n]), p4, r3, chip_idx,
                               first=q, fills=fills if shard3[0] > 1 else None)
        out_g[n], out_d[n], out_m[n], out_v[n] = [o.reshape(shp) for o in outs]

    return (loss, grad_x, *[out_g[n] for n in names], *[out_d[n] for n in names], *[out_m[n] for n in names],
            *[out_v[n] for n in names])
```

```python
import functools
import math

import jax
import jax.numpy as jnp
from jax import lax
from jax.experimental import pallas as pl
from jax.experimental.pallas import tpu as pltpu

F32 = jnp.float32
BF16 = jnp.bfloat16
SDS = jax.ShapeDtypeStruct
MESH = pl.DeviceIdType.MESH
ANY = pl.BlockSpec(memory_space=pl.ANY)

N_DEV = 8
LANES = 128
HEAD_DIM = 64
LRU_BLOCK_W = 128
LRU_C = 8.0
MACARON_W = 0.5
NORM_EPS = 1e-6
ADAM_LR = 0.001
ADAM_B1 = 0.9
ADAM_B2 = 0.999
ADAM_EPS = 1e-08
ADAM_WD = 0.01
ADAM_STEP = 10
VMEM_LIMIT = 56 * 1024 * 1024
GELU_C = math.sqrt(2.0 / math.pi)
GELU_K = 0.044715

DIMS = {
    "nn": (((1,), (0,)), ((), ())),
    "nt": (((1,), (1,)), ((), ())),
    "tn": (((0,), (0,)), ((), ())),
}


def _pcall(body, **kw):
    return pl.pallas_call(body, **kw)


def _params(sem=None):
    return pltpu.CompilerParams(dimension_semantics=sem, vmem_limit_bytes=VMEM_LIMIT)


def _tile(n, prefs):
    for p in prefs:
        if n % p == 0:
            return p
    return n


def _dot(a, b, dims):
    return lax.dot_general(a, b, DIMS[dims], preferred_element_type=F32)


def _softplus(z):
    return jnp.maximum(z, 0.0) + jnp.log(1.0 + jnp.exp(-jnp.abs(z)))


def _sigmoid(z):
    return 0.5 * jnp.tanh(0.5 * z) + 0.5


def _mesh_pos():
    return lax.axis_index("x"), lax.axis_index("y"), lax.axis_index("c")


def _allgather(xs, name, cols=False):
    return _run_comm(_gather_plan([xs], [cols]), name)[0]


class _CommPlan:
    def __init__(self, ins, outs, n_remote, n_local, phases):
        self.ins, self.outs, self.n_remote, self.n_local, self.phases = ins, outs, n_remote, n_local, phases

    def scratch(self):
        return [pltpu.SemaphoreType.DMA((self.n_remote,)), pltpu.SemaphoreType.DMA((self.n_remote,)),
                pltpu.SemaphoreType.DMA((max(self.n_local, 1),))]


def _merge_plans(plans):
    plans = [p for p in plans if p is not None]
    if len(plans) <= 1:
        return plans[0] if plans else None

    def phase(k):
        def run(in_refs, out_refs, send_sems, recv_sems, local_sems, r0=0, l0=0):
            i0 = o0 = 0
            for p in plans:
                p.phases[k](in_refs[i0:i0 + len(p.ins)], out_refs[o0:o0 + len(p.outs)], send_sems, recv_sems, local_sems, r0, l0)
                i0, o0, r0, l0 = i0 + len(p.ins), o0 + len(p.outs), r0 + p.n_remote, l0 + p.n_local
        return run

    return _CommPlan(sum([p.ins for p in plans], []), sum([p.outs for p in plans], []), sum(p.n_remote for p in plans),
                     sum(p.n_local for p in plans), [phase(0), phase(1), phase(2)])


def _run_comm(plan, name):
    n_in, n_out = len(plan.ins), len(plan.outs)

    def body(*refs):
        in_refs, out_refs, sems = refs[:n_in], refs[n_in:n_in + n_out], refs[n_in + n_out:]
        for phase in plan.phases:
            phase(in_refs, out_refs, *sems)

    return _pcall(body, name=name, out_shape=plan.outs, in_specs=[ANY] * n_in, out_specs=[ANY] * n_out,
                  scratch_shapes=plan.scratch())(*plan.ins)


def _col_window(ref, idx, width):
    return ref.at[:, :, pl.ds(pl.multiple_of(idx * width, math.gcd(width, LANES)), width)]


def _gather_plan(shards, cols):
    n = len(shards)
    outs = [SDS((s.shape[0], s.shape[1], N_DEV * s.shape[2]) if cl else (s.shape[0], N_DEV) + s.shape[1:], s.dtype)
            for s, cl in zip(shards, cols)]

    def copies(a, in_refs, out_refs, send_sems, recv_sems, local_sems, r0=0, l0=0):
        x, y, c = _mesh_pos()
        sibling = (x, y, 1 - c)
        chips = [(1 - x, y), (x, 1 - y), (1 - x, 1 - y)]
        width = shards[a].shape[2]

        def block(px, py, pc):
            idx = 4 * px + 2 * py + pc
            return _col_window(out_refs[a], idx, width) if cols[a] else out_refs[a].at[:, idx]

        def copy(k, owner, to, src=None):
            sem = r0 + 7 * a + k
            return pltpu.make_async_remote_copy(
                src_ref=block(*owner) if src is None else src, dst_ref=block(*owner),
                send_sem=send_sems.at[sem], recv_sem=recv_sems.at[sem], device_id=to, device_id_type=MESH)

        me = (x, y, c)
        first = [copy(0, me, sibling, src=in_refs[a])]
        first += [copy(1 + j, me, (*chip, c), src=in_refs[a]) for j, chip in enumerate(chips)]
        passed = [copy(4 + j, (*chip, c), sibling) for j, chip in enumerate(chips)]
        landed = [copy(1 + j, (*chip, c), me) for j, chip in enumerate(chips)]
        from_sibling = [copy(0, sibling, me)] + [copy(4 + j, (*chip, 1 - c), me) for j, chip in enumerate(chips)]
        mine = pltpu.make_async_copy(in_refs[a], block(*me), local_sems.at[l0 + a])
        return first, passed, landed, from_sibling, mine

    def start(*refs):
        for a in range(n):
            first, _, _, _, mine = copies(a, *refs)
            mine.start()
            for cp in first:
                cp.start()

    def pass_on(*refs):
        for a in range(n):
            _, passed, landed, _, _ = copies(a, *refs)
            for cp, fwd in zip(landed, passed):
                cp.wait_recv()
                fwd.start()

    def finish(*refs):
        for a in range(n):
            first, passed, _, from_sibling, mine = copies(a, *refs)
            for cp in from_sibling:
                cp.wait_recv()
            for cp in first + passed:
                cp.wait_send()
            mine.wait()

    return _CommPlan(list(shards), outs, 7 * n, n, [start, pass_on, finish])


def _exchange_plan(srcs, cols, n_slots, route):
    n = len(srcs)
    outs = []
    for g, cl in zip(srcs, cols):
        shard = (g.shape[0], g.shape[1], g.shape[2] // N_DEV) if cl else (g.shape[0],) + g.shape[2:]
        outs.append(SDS((n_slots,) + shard, g.dtype))

    def copies(in_refs, out_refs, send_sems, recv_sems, local_sems, r0=0, l0=0):
        x, y, c = _mesh_pos()
        made = []
        for a in range(n):
            for s in range(n_slots):
                chunk, target = route(x, y, c, s)
                src = _col_window(in_refs[a], chunk, outs[a].shape[3]) if cols[a] else in_refs[a].at[:, chunk]
                sem = r0 + a * n_slots + s
                made.append(pltpu.make_async_remote_copy(
                    src_ref=src, dst_ref=out_refs[a].at[s], send_sem=send_sems.at[sem], recv_sem=recv_sems.at[sem],
                    device_id=target, device_id_type=MESH))
        return made

    def start(*refs):
        for cp in copies(*refs):
            cp.start()

    def nothing(*refs):
        pass

    def finish(*refs):
        made = copies(*refs)
        for cp in made:
            cp.wait_recv()
        for cp in made:
            cp.wait_send()

    return _CommPlan(list(srcs), outs, n * n_slots, 0, [start, nothing, finish])


def _sibling_route(x, y, c, k):
    return 2 * k + 1 - c, (x, y, 1 - c)


def _chip_route(x, y, c, j):
    px, py = [(1 - x, y), (x, 1 - y), (1 - x, 1 - y)][j]
    return 2 * px + py, (px, py, c)


def _pair_sum(grads, recv4, c_idx, name, cols=False):
    _, p, r, cdim = recv4.shape
    tr = _tile(r, (512, 256, 176, 160, 128, 64, 32, 16))

    def body(c_ref, a_ref, b_ref, o_ref):
        o_ref[...] = (a_ref[...].astype(F32) + b_ref[...].astype(F32)).astype(o_ref.dtype)

    blk = (None, None, tr, cdim)
    if cols:
        own = pl.BlockSpec((None, tr, cdim), lambda k, q, i, c_ref: (q, i, 2 * k + c_ref[0]))
    else:
        own = pl.BlockSpec(blk, lambda k, q, i, c_ref: (q, 2 * k + c_ref[0], i, 0))
    grid_spec = pltpu.PrefetchScalarGridSpec(
        num_scalar_prefetch=1, grid=(4, p, r // tr),
        in_specs=[own, pl.BlockSpec(blk, lambda k, q, i, c_ref: (k, q, i, 0))],
        out_specs=pl.BlockSpec(blk, lambda k, q, i, c_ref: (q, k, i, 0)))
    return _pcall(body, name=name, grid_spec=grid_spec, out_shape=SDS((p, 4, r, cdim), grads.dtype),
                  compiler_params=_params(("parallel", "parallel", "parallel")))(c_idx, grads, recv4)


def _mm(name, ins, prods, n_acc, acc_shape, epi_idx, epilogue, out_shapes, out_specs, grid, dims, plan=None):
    n_in, n_out, nk = len(ins), len(out_shapes), grid[2]
    n_acc_refs = n_acc if nk > 1 else 0
    c_ins, c_outs = (plan.ins, plan.outs) if plan else ([], [])
    n_cin, n_cout = len(c_ins), len(c_outs)

    def body(*refs):
        in_refs, c_in = refs[:n_in], refs[n_in:n_in + n_cin]
        rest = refs[n_in + n_cin:]
        out_refs, c_out = rest[:n_out], rest[n_out:n_out + n_cout]
        rest = rest[n_out + n_cout:]
        acc_refs, sems = rest[:n_acc_refs], rest[n_acc_refs:]
        ids = [pl.program_id(axis) for axis in range(3)]
        if plan:
            @pl.when((ids[0] == 0) & (ids[1] == 0) & (ids[2] == 0))
            def _():
                plan.phases[0](c_in, c_out, *sems)

        def finish(accs):
            outs = epilogue(accs, [in_refs[i][...] for i in epi_idx])
            for o_ref, o in zip(out_refs, outs):
                if isinstance(o, tuple):
                    for plane, part in enumerate(o):
                        o_ref[plane] = part.astype(o_ref.dtype)
                else:
                    o_ref[...] = o.astype(o_ref.dtype)

        def operand(ref_idx):
            if isinstance(ref_idx, tuple):
                return in_refs[ref_idx[0]][ref_idx[1]]
            return in_refs[ref_idx][...]

        if nk == 1:
            accs = [None] * n_acc
            for ia, ib, iacc in prods:
                term = _dot(operand(ia), operand(ib), dims)
                accs[iacc] = term if accs[iacc] is None else accs[iacc] + term
            finish(accs)
        else:
            @pl.when(ids[2] == 0)
            def _():
                for acc in acc_refs:
                    acc[...] = jnp.zeros_like(acc)

            for ia, ib, iacc in prods:
                acc_refs[iacc][...] += _dot(operand(ia), operand(ib), dims)

            @pl.when(ids[2] == nk - 1)
            def _():
                finish([acc[...] for acc in acc_refs])

        if plan:
            @pl.when((ids[0] == grid[0] - 1) & (ids[1] == grid[1] - 1) & (ids[2] == nk - 1))
            def _():
                plan.phases[1](c_in, c_out, *sems)
                plan.phases[2](c_in, c_out, *sems)

    return _pcall(
        body, name=name, grid=grid, in_specs=[s for _, s in ins] + [ANY] * n_cin,
        out_specs=list(out_specs) + [ANY] * n_cout, out_shape=list(out_shapes) + list(c_outs),
        scratch_shapes=[pltpu.VMEM(acc_shape, F32) for _ in range(n_acc_refs)] + (plan.scratch() if plan else []),
        compiler_params=_params(("arbitrary",) * 3 if plan else ("parallel", "parallel", "arbitrary")),
    )(*[a for a, _ in ins], *c_ins)


def _plain(accs, _):
    return accs


def _mm_nn(name, a, b, out_dtype, extras=(), epilogue=_plain, n_out=1):
    m, kd = a.shape
    n = b.shape[1]
    tm, tn, tk = _tile(m, (1024, 512, 256, 128)), _tile(n, (640, 512, 256, 128)), _tile(kd, (1280, 1024, 512, 256, 128))
    ins = [(a, pl.BlockSpec((tm, tk), lambda i, j, k: (i, k))), (b, pl.BlockSpec((tk, tn), lambda i, j, k: (k, j)))]
    for arr, kind in extras:
        if kind == "tile":
            ins.append((arr, pl.BlockSpec((tm, tn), lambda i, j, k: (i, j))))
        else:
            ins.append((arr, pl.BlockSpec((1, tn), lambda i, j, k: (0, j))))
    dts = out_dtype if isinstance(out_dtype, (list, tuple)) else [out_dtype] * n_out
    return _mm(name, ins, [(0, 1, 0)], 1, (tm, tn), list(range(2, len(ins))), epilogue,
               [SDS((m, n), dt) for dt in dts], [pl.BlockSpec((tm, tn), lambda i, j, k: (i, j)) for _ in dts],
               (m // tm, n // tn, kd // tk), "nn")


def _mm_nt(name, a, b, out_dtype, plan=None):
    m, kd = a.shape
    n = b.shape[0]
    tm, tn, tk = _tile(m, (1024, 512, 256, 128)), _tile(n, (640, 512, 256, 128)), _tile(kd, (1024, 512, 256, 128))
    ins = [(a, pl.BlockSpec((tm, tk), lambda i, j, k: (i, k))), (b, pl.BlockSpec((tn, tk), lambda i, j, k: (j, k)))]
    outs = _mm(name, ins, [(0, 1, 0)], 1, (tm, tn), [], _plain, [SDS((m, n), out_dtype)],
               [pl.BlockSpec((tm, tn), lambda i, j, k: (i, j))], (m // tm, n // tn, kd // tk), "nt", plan=plan)
    return outs if plan else outs[0]


def _mm_tn(name, a, b, out_dtype):
    t, m = a.shape
    n = b.shape[1]
    tm, tn, tk = _tile(m, (640, 512, 256, 128)), _tile(n, (1024, 512, 256, 128)), t
    ins = [(a, pl.BlockSpec((tk, tm), lambda i, j, k: (k, i))), (b, pl.BlockSpec((tk, tn), lambda i, j, k: (k, j)))]
    return _mm(name, ins, [(0, 1, 0)], 1, (tm, tn), [], _plain, [SDS((m, n), out_dtype)],
               [pl.BlockSpec((tm, tn), lambda i, j, k: (i, j))], (m // tm, n // tn, t // tk), "tn")[0]


def _mm_nt_stack(name, a3, b, out_dtype):
    cc, m, kd = a3.shape
    n = b.shape[0]
    tm, tn, tk = _tile(m, (1024, 512, 256, 128)), _tile(n, (1024, 512, 256, 128)), _tile(kd, (1280, 1024, 512, 256, 128))
    nk = kd // tk
    ins = [(a3, pl.BlockSpec((None, tm, tk), lambda i, j, k: (k // nk, i, k % nk))),
           (b, pl.BlockSpec((tn, tk), lambda i, j, k: (j, k)))]
    return _mm(name, ins, [(0, 1, 0)], 1, (tm, tn), [], _plain, [SDS((m, n), out_dtype)],
               [pl.BlockSpec((tm, tn), lambda i, j, k: (i, j))], (m // tm, n // tn, cc * nk), "nt")[0]


def _mm_tn_stack(name, a, b3, out_dtype):
    t, m = a.shape
    cc, _, n = b3.shape
    tm, tn, tk = _tile(m, (512, 256, 128)), _tile(n, (1280, 1024, 512, 256, 128)), t
    nj = n // tn
    ins = [(a, pl.BlockSpec((tk, tm), lambda i, j, k: (k, i))),
           (b3, pl.BlockSpec((None, tk, tn), lambda i, j, k: (j // nj, k, j % nj)))]
    return _mm(name, ins, [(0, 1, 0)], 1, (tm, tn), [], _plain, [SDS((m, cc * n), out_dtype)],
               [pl.BlockSpec((tm, tn), lambda i, j, k: (i, j))], (m // tm, cc * nj, t // tk), "tn")[0]


def _chunks_to_cols(name, wc, eye2):
    nch, d, cw = wc.shape
    tm = _tile(d, (1024, 512, 256, 128))
    ins = [(wc, pl.BlockSpec((None, tm, cw), lambda i, j, k: (2 * j + k, i, 0))),
           (eye2, pl.BlockSpec((None, cw, 2 * cw), lambda i, j, k: (k, 0, 0)))]
    return _mm(name, ins, [(0, 1, 0)], 1, (tm, 2 * cw), [], _plain, [SDS((d, nch * cw), wc.dtype)],
               [pl.BlockSpec((tm, 2 * cw), lambda i, j, k: (i, j))], (d // tm, nch // 2, 2), "nn")[0]


def _cols_to_chunks(name, full, eye2):
    d, n = full.shape
    _, cw, _ = eye2.shape
    nch = n // cw
    tm = _tile(d, (1024, 512, 256, 128))
    ins = [(full, pl.BlockSpec((tm, 2 * cw), lambda i, j, k: (i, j // 2))),
           (eye2, pl.BlockSpec((None, cw, 2 * cw), lambda i, j, k: (j % 2, 0, 0)))]
    return _mm(name, ins, [(0, 1, 0)], 1, (tm, cw), [], _plain, [SDS((nch, d, cw), full.dtype)],
               [pl.BlockSpec((None, tm, cw), lambda i, j, k: (j, i, 0))], (d // tm, nch, 1), "nt")[0]


def _row_tile(t):
    return _tile(t, (256, 128, 64, 32, 16, 8))


def _norm_fwd(x, g, scale1p, shift, name):
    t, d = x.shape
    tr = _row_tile(t)

    def body(x_ref, g_ref, s_ref, b_ref, h_ref):
        xv = x_ref[...]
        inv = lax.rsqrt(jnp.mean(xv * xv, axis=-1, keepdims=True) + NORM_EPS)
        h_ref[...] = ((xv * inv) * g_ref[...] * s_ref[...] + b_ref[...]).astype(h_ref.dtype)

    vec = pl.BlockSpec((1, d), lambda i: (0, 0))
    return _pcall(body, name=name, grid=(t // tr,), in_specs=[pl.BlockSpec((tr, d), lambda i: (i, 0)), vec, vec, vec],
                  out_specs=pl.BlockSpec((tr, d), lambda i: (i, 0)), out_shape=SDS((t, d), BF16),
                  compiler_params=_params(("parallel",)))(x, g, scale1p, shift)


def _adaln_bwd(dh, x, y, dxo, g, scale1p, w_sub, gw_prev, name, plan=None):
    t, d = x.shape
    tr = _row_tile(t)

    def body(in_refs, out_refs, _):
        (dh_ref, x_ref, y_ref, dxo_ref, g_ref, s_ref, gw_ref), (dx_ref, dyp_ref, sums_ref) = in_refs, out_refs
        i = pl.program_id(0)

        @pl.when(i == 0)
        def _():
            sums_ref[...] = jnp.zeros_like(sums_ref)

        xv, dhv, dxov = x_ref[...], dh_ref[...], dxo_ref[...]
        inv = lax.rsqrt(jnp.mean(xv * xv, axis=-1, keepdims=True) + NORM_EPS)
        xn = xv * inv
        gv = g_ref[...]
        dn = dhv * s_ref[...]
        dxn = dn * gv
        dx = inv * (dxn - xn * jnp.mean(dxn * xn, axis=-1, keepdims=True)) + dxov
        dx_ref[...] = dx
        dyp_ref[...] = (gw_ref[...] * dx).astype(dyp_ref.dtype)
        sums_ref[0:1, :] += jnp.sum(dhv, axis=0, keepdims=True)
        sums_ref[1:2, :] += jnp.sum(dhv * (xn * gv), axis=0, keepdims=True)
        sums_ref[2:3, :] += jnp.sum(w_sub * y_ref[...] * dxov, axis=0, keepdims=True)
        sums_ref[3:4, :] += jnp.sum(dn * xn, axis=0, keepdims=True)

    blk = pl.BlockSpec((tr, d), lambda i: (i, 0))
    vec = pl.BlockSpec((1, d), lambda i: (0, 0))
    return _host_call(
        body, name, t // tr, [dh, x, y, dxo, g, scale1p, gw_prev], [blk, blk, blk, blk, vec, vec, vec],
        [SDS((t, d), F32), SDS((t, d), BF16), SDS((8, d), F32)], [blk, blk, pl.BlockSpec((8, d), lambda i: (0, 0))], [], plan)


def _loss_head(x, target, gf, gw_prev):
    t, d = x.shape
    tr = _row_tile(t)
    nt = t // tr

    def body(x_ref, tg_ref, g_ref, gw_ref, dx_ref, dyp_ref, sums_ref):
        i = pl.program_id(0)

        @pl.when(i == 0)
        def _():
            sums_ref[...] = jnp.zeros_like(sums_ref)

        xv = x_ref[...]
        inv = lax.rsqrt(jnp.mean(xv * xv, axis=-1, keepdims=True) + NORM_EPS)
        xn = xv * inv
        gv = g_ref[...]
        err = xn * gv - tg_ref[...]
        dyv = err * (1.0 / d)
        dxn = dyv * gv
        dx = inv * (dxn - xn * jnp.mean(dxn * xn, axis=-1, keepdims=True))
        dx_ref[...] = dx
        dyp_ref[...] = (gw_ref[...] * dx).astype(dyp_ref.dtype)
        sums_ref[0:1, :] += jnp.sum(dyv * xn, axis=0, keepdims=True)
        sums_ref[1:2, :] += jnp.sum(err * err, axis=0, keepdims=True)

        @pl.when(i == nt - 1)
        def _():
            tot = jnp.sum(sums_ref[1:2, :], axis=1, keepdims=True) * (0.5 / d)
            sums_ref[1:2, :] = jnp.broadcast_to(tot, (1, d))

    blk = pl.BlockSpec((tr, d), lambda i: (i, 0))
    vec = pl.BlockSpec((1, d), lambda i: (0, 0))
    return _pcall(
        body, name="loss_head", grid=(nt,), in_specs=[blk, blk, vec, vec],
        out_specs=[blk, blk, pl.BlockSpec((8, d), lambda i: (0, 0))],
        out_shape=[SDS((t, d), F32), SDS((t, d), BF16), SDS((8, d), F32)],
        compiler_params=_params(("arbitrary",)))(x, target, gf, gw_prev)


HIDDEN_CHUNKS = N_DEV // 2


def _ffn_tiles(t, d):
    return _tile(t, (1024, 512, 256, 128)), _tile(d, (1024, 512, 256, 128))


def _ffn_gu(name, h, wgu, plan=None):
    t, d = h.shape
    fc, nc = wgu.shape[3], HIDDEN_CHUNKS
    tm, _ = _ffn_tiles(t, d)

    def epi_gu(accs, _):
        gpre, up = accs
        return (gpre, up), gpre * _sigmoid(gpre) * up

    wblk = (None, None, d, fc)
    ins = [(h, pl.BlockSpec((tm, d), lambda i, c, k: (i, 0))),
           (wgu, pl.BlockSpec(wblk, lambda i, c, k: (0, c, 0, 0))),
           (wgu, pl.BlockSpec(wblk, lambda i, c, k: (0, c + nc, 0, 0)))]
    return _mm(name, ins, [(0, 1, 0), (0, 2, 1)], 2, (tm, fc), [], epi_gu,
               [SDS((2, nc, t, fc), BF16), SDS((nc, t, fc), BF16)],
               [pl.BlockSpec((2, None, tm, fc), lambda i, c, k: (0, c, i, 0)),
                pl.BlockSpec((None, tm, fc), lambda i, c, k: (c, i, 0))],
               (t // tm, nc, 1), "nn", plan=plan)


def _ffn_down(name, a, wd4, x, gmul, plan=None):
    nc, t, fc = a.shape
    d = wd4.shape[3]
    tm = _tile(t, (512, 256, 128))

    def epi_down(accs, ex):
        (yv,), (xv, gm) = accs, ex
        return yv, xv + MACARON_W * gm * yv

    ins = [(a, pl.BlockSpec((nc, tm, fc), lambda i, j, k: (0, i, 0))),
           (wd4, pl.BlockSpec((None, nc, fc, d), lambda i, j, k: (0, 0, 0, 0), pipeline_mode=pl.Buffered(1))),
           (x, pl.BlockSpec((tm, d), lambda i, j, k: (i, 0))), (gmul, pl.BlockSpec((1, d), lambda i, j, k: (0, 0)))]
    oblk = pl.BlockSpec((tm, d), lambda i, j, k: (i, 0))
    prods = [((0, (c,)), (1, (c,)), 0) for c in range(nc)]
    return _mm(name, ins, prods, 1, (tm, d), [2, 3], epi_down, [SDS((t, d), BF16), SDS((t, d), F32)],
               [oblk, oblk], (t // tm, 1, 1), "nn", plan=plan)


def _ffn_da(name, dy, wd4, gu2, plan=None):
    t, d = dy.shape
    _, nc, fc, _ = wd4.shape
    tm, _ = _ffn_tiles(t, d)

    def epi_da(accs, ex):
        (da,), (gu,) = accs, ex
        gpre, up = gu[0].astype(F32), gu[1].astype(F32)
        s = _sigmoid(gpre)
        silu = gpre * s
        dg = da * up * (s * (1.0 + gpre * (1.0 - s)))
        return ((dg, da * silu),)

    gblk = pl.BlockSpec((2, None, tm, fc), lambda i, c, k: (0, c, i, 0))
    ins = [(dy, pl.BlockSpec((tm, d), lambda i, c, k: (i, 0))),
           (wd4, pl.BlockSpec((None, None, fc, d), lambda i, c, k: (0, c, 0, 0))), (gu2, gblk)]
    return _mm(name, ins, [(0, 1, 0)], 1, (tm, fc), [2], epi_da, [SDS((2, nc, t, fc), BF16)], [gblk],
               (t // tm, nc, 1), "nt", plan=plan)


def _ffn_dwd(name, a, dy, plan=None):
    nc, t, fc = a.shape
    d = dy.shape[1]
    _, tn = _ffn_tiles(t, d)
    ins = [(a, pl.BlockSpec((None, t, fc), lambda c, j, k: (c, 0, 0))), (dy, pl.BlockSpec((t, tn), lambda c, j, k: (0, j)))]
    return _mm(name, ins, [(0, 1, 0)], 1, (fc, tn), [], _plain, [SDS((1, nc, fc, d), BF16)],
               [pl.BlockSpec((None, None, fc, tn), lambda c, j, k: (0, c, 0, j))], (nc, d // tn, 1), "tn", plan=plan)


def _ffn_dwgu(name, h, dgu2, plan=None):
    t, d = h.shape
    _, nc, _, fc = dgu2.shape
    _, tn = _ffn_tiles(t, d)
    ins = [(h, pl.BlockSpec((t, tn), lambda i, c, k: (0, i))),
           (dgu2, pl.BlockSpec((None, None, t, fc), lambda i, c, k: (c // nc, c % nc, 0, 0)))]
    return _mm(name, ins, [(0, 1, 0)], 1, (tn, fc), [], _plain, [SDS((1, 2 * nc, d, fc), BF16)],
               [pl.BlockSpec((None, None, tn, fc), lambda i, c, k: (0, c, i, 0))], (d // tn, 2 * nc, 1), "tn", plan=plan)


def _ffn_dh(name, dgu2, wgu, plan=None):
    _, nc, t, fc = dgu2.shape
    d = wgu.shape[2]
    tm = _tile(t, (512, 256, 128))
    ins = [(dgu2, pl.BlockSpec((2, nc, tm, fc), lambda i, j, k: (0, 0, i, 0))),
           (wgu, pl.BlockSpec((None, 2 * nc, d, fc), lambda i, j, k: (0, 0, 0, 0), pipeline_mode=pl.Buffered(1)))]
    prods = [((0, (s, c)), (1, (nc * s + c,)), 0) for s in range(2) for c in range(nc)]
    return _mm(name, ins, prods, 1, (tm, d), [], _plain, [SDS((t, d), F32)],
               [pl.BlockSpec((tm, d), lambda i, j, k: (i, 0))], (t // tm, 1, 1), "nt", plan=plan)


def _sb_block(t):
    return 256 if t >= 1024 else 128


SB_STRIP = 64


def _sb_strips(blk):
    strip = min(SB_STRIP, blk)
    row = lax.broadcasted_iota(jnp.int32, (strip, blk), 0)
    col = lax.broadcasted_iota(jnp.int32, (strip, blk), 1)
    return [(slice(r0, r0 + strip), col < row + r0) for r0 in range(0, blk, strip)]


def _host_call(core, name, steps, ins, in_specs, out_shapes, out_specs, scratch, plan):
    n_in, n_out, n_scr = len(ins), len(out_shapes), len(scratch)
    c_ins, c_outs = (plan.ins, plan.outs) if plan else ([], [])
    n_cin, n_cout = len(c_ins), len(c_outs)

    def body(*refs):
        in_refs, c_in = refs[:n_in], refs[n_in:n_in + n_cin]
        rest = refs[n_in + n_cin:]
        out_refs, c_out = rest[:n_out], rest[n_out:n_out + n_cout]
        rest = rest[n_out + n_cout:]
        scr, sems = rest[:n_scr], rest[n_scr:]
        step = pl.program_id(0)
        if plan:
            @pl.when(step == 0)
            def _():
                plan.phases[0](c_in, c_out, *sems)

        core(in_refs, out_refs, scr)
        if plan:
            @pl.when(step == steps - 1)
            def _():
                plan.phases[1](c_in, c_out, *sems)
                plan.phases[2](c_in, c_out, *sems)

    return _pcall(
        body, name=name, grid=(steps,), in_specs=list(in_specs) + [ANY] * n_cin,
        out_specs=list(out_specs) + [ANY] * n_cout, out_shape=list(out_shapes) + list(c_outs),
        scratch_shapes=list(scratch) + (plan.scratch() if plan else []),
        compiler_params=_params(("arbitrary",)))(*ins, *c_ins)


def _sb_fwd(qkv, d, plan=None):
    t = qkv.shape[0]
    blk = _sb_block(t)
    nq = t // blk
    npair = d // LANES
    scale = HEAD_DIM ** -0.5

    def body(in_refs, out_refs, scr):
        (q_ref, k_ref, v_ref), (o_ref, l_ref) = in_refs, out_refs
        tri_s = scr[0]
        hi_s, lo_s, w_s, zs_s = (scr[1 + 4 * k:5 + 4 * k] for k in range(4))
        lane = lax.broadcasted_iota(jnp.int32, (blk, LANES), 1)
        head0 = lane < HEAD_DIM
        row = lax.broadcasted_iota(jnp.int32, (blk, blk), 0)
        col = lax.broadcasted_iota(jnp.int32, (blk, blk), 1)
        tri_s[...] = (row > col).astype(BF16)
        strips = _sb_strips(blk)

        def step(qhs, kbs, maskeds, carries):
            chains = [(bi, hh) for bi in range(len(kbs)) for hh in range(2)]
            starts = [pl.multiple_of(kb * blk, blk) for kb in kbs]
            kvs = [k_ref[pl.ds(start, blk), :] for start in starts]
            vvs = [v_ref[pl.ds(start, blk), :] for start in starts]
            zs = [_dot(qhs[hh], kvs[bi], "nt") for bi, hh in chains]
            sums = []
            for c, (bi, hh) in enumerate(chains):
                parts = []
                for rows, causal in strips:
                    zt = zs[c][rows, :]
                    sp = _softplus(zt)
                    lk = jnp.where(causal, -sp, 0.0) if maskeds[bi] else -sp
                    hi = lk.astype(BF16)
                    hi_s[c][rows, :] = hi
                    lo_s[c][rows, :] = (lk - hi.astype(F32)).astype(BF16)
                    zs_s[c][rows, :] = zt - sp
                    parts.append(jnp.sum(lk, axis=1, keepdims=True))
                sums.append(jnp.concatenate(parts, axis=0))
            laters = [_dot(hi_s[c][...], tri_s[...], "nn") + _dot(lo_s[c][...], tri_s[...], "nn") for c in range(len(chains))]
            for c, (bi, hh) in enumerate(chains):
                cl = carries[hh][0]
                if bi == 1:
                    cl = cl + sums[hh]
                for rows, causal in strips:
                    logw = zs_s[c][rows, :] + laters[c][rows, :] + cl[rows, :]
                    if maskeds[bi]:
                        logw = jnp.where(causal, logw, -1e30)
                    w_s[c][rows, :] = jnp.exp(logw).astype(BF16)
            new = [list(carries[hh]) for hh in range(2)]
            for c, (bi, hh) in enumerate(chains):
                new[hh] = [new[hh][0] + sums[c], new[hh][1] + _dot(w_s[c][...], vvs[bi], "nn")]
            return tuple(tuple(cr) for cr in new)

        def qblock(qi, _):
            qstart = pl.multiple_of(qi * blk, blk)
            qv = q_ref[pl.ds(qstart, blk), :] * scale
            qhs = [jnp.where(head0 if hh == 0 else ~head0, qv, jnp.zeros_like(qv)) for hh in range(2)]
            zero = (jnp.zeros((blk, 1), F32), jnp.zeros((blk, LANES), F32))
            outs = lax.cond(qi % 2 == 1,
                            lambda crs: step(qhs, [qi, qi - 1], [True, False], crs),
                            lambda crs: step(qhs, [qi], [True], crs), (zero, zero))
            top = qi - 1 - qi % 2
            outs = lax.fori_loop(0, qi // 2, lambda j, crs: step(qhs, [top - 2 * j, top - 2 * j - 1], [False, False], crs),
                                 outs)
            o_ref[pl.ds(qstart, blk), :] = jnp.where(head0, outs[0][1], outs[1][1]).astype(o_ref.dtype)
            l_ref[pl.ds(qstart, blk), :] = jnp.where(head0, outs[0][0], outs[1][0])
            return 0

        lax.fori_loop(0, nq, qblock, 0)

    tile_bf16, tile_f32 = pltpu.VMEM((blk, blk), BF16), pltpu.VMEM((blk, blk), F32)
    return _host_call(
        body, "sb_fwd", npair, [qkv, qkv, qkv],
        [pl.BlockSpec((t, LANES), lambda p: (0, p)), pl.BlockSpec((t, LANES), lambda p: (0, npair + p)),
         pl.BlockSpec((t, LANES), lambda p: (0, 2 * npair + p))],
        [SDS((t, d), BF16), SDS((t, d), F32)],
        [pl.BlockSpec((t, LANES), lambda p: (0, p)), pl.BlockSpec((t, LANES), lambda p: (0, p))],
        [tile_bf16] * 13 + [tile_f32] * 4, plan)


def _sb_bwd(qkv, do, ltot, d, plan=None):
    t = qkv.shape[0]
    blk = _sb_block(t)
    nq = t // blk
    npair = d // LANES
    scale = HEAD_DIM ** -0.5

    def body(in_refs, out_refs, scr):
        (q_ref, k_ref, v_ref, do_ref, l_ref), (out_ref,) = in_refs, out_refs
        dq_s, dk_s, dv_s, upto_s, before_s = scr[:5]
        hi_s, lo_s, w_s, dab_s, dzs_s, zs_s, da_s = (scr[5 + 4 * k:9 + 4 * k] for k in range(7))
        lane = lax.broadcasted_iota(jnp.int32, (blk, LANES), 1)
        head0 = lane < HEAD_DIM
        row = lax.broadcasted_iota(jnp.int32, (blk, blk), 0)
        col = lax.broadcasted_iota(jnp.int32, (blk, blk), 1)
        upto_s[...] = (row <= col).astype(BF16)
        before_s[...] = (row < col).astype(BF16)
        dk_s[...] = jnp.zeros_like(dk_s)
        dv_s[...] = jnp.zeros_like(dv_s)
        strips = _sb_strips(blk)

        def step(heads, kbs, maskeds, carries):
            chains = [(bi, hh) for bi in range(len(kbs)) for hh in range(2)]
            starts = [pl.multiple_of(kb * blk, blk) for kb in kbs]
            kvs = [k_ref[pl.ds(start, blk), :] for start in starts]
            vvs = [v_ref[pl.ds(start, blk), :] for start in starts]
            zs = [_dot(heads[hh][0], kvs[bi], "nt") for bi, hh in chains]
            dws = [_dot(heads[hh][1], vvs[bi], "nt") for bi, hh in chains]
            lk_sums, da_sums = [], []
            for c, (bi, hh) in enumerate(chains):
                parts = []
                for rows, causal in strips:
                    zt = zs[c][rows, :]
                    sp = _softplus(zt)
                    lk = jnp.where(causal, -sp, 0.0) if maskeds[bi] else -sp
                    hi = lk.astype(BF16)
                    hi_s[c][rows, :] = hi
                    lo_s[c][rows, :] = (lk - hi.astype(F32)).astype(BF16)
                    zs_s[c][rows, :] = zt - sp
                    parts.append(jnp.sum(lk, axis=1, keepdims=True))
                lk_sums.append(jnp.concatenate(parts, axis=0))
            cums = [_dot(hi_s[c][...], upto_s[...], "nn") + _dot(lo_s[c][...], upto_s[...], "nn") for c in range(len(chains))]
            for c, (bi, hh) in enumerate(chains):
                lt, plk = heads[hh][2], carries[hh][0]
                if bi == 1:
                    plk = plk + lk_sums[hh]
                parts = []
                for rows, causal in strips:
                    logw = zs_s[c][rows, :] + (lt[rows, :] - (plk[rows, :] + cums[c][rows, :]))
                    if maskeds[bi]:
                        logw = jnp.where(causal, logw, -1e30)
                    w = jnp.exp(logw)
                    w_s[c][rows, :] = w.astype(BF16)
                    da = dws[c][rows, :] * w
                    da_s[c][rows, :] = da
                    dab_s[c][rows, :] = da.astype(BF16)
                    parts.append(jnp.sum(da, axis=1, keepdims=True))
                da_sums.append(jnp.concatenate(parts, axis=0))
            pres = [_dot(dab_s[c][...], before_s[...], "nn") for c in range(len(chains))]
            for c, (bi, hh) in enumerate(chains):
                pda = carries[hh][1]
                if bi == 1:
                    pda = pda + da_sums[hh]
                for rows, causal in strips:
                    sig = jnp.exp(zs_s[c][rows, :])
                    da = da_s[c][rows, :]
                    dz = da * (1.0 - sig) - sig * (pda[rows, :] + pres[c][rows, :])
                    if maskeds[bi]:
                        dz = jnp.where(causal, dz, 0.0)
                    dzs_s[c][rows, :] = dz.astype(BF16)
            new = [list(carries[hh]) for hh in range(2)]
            for c, (bi, hh) in enumerate(chains):
                dk_s[kbs[bi]] += _dot(heads[hh][3], dzs_s[c][...], "nn")
                dv_s[kbs[bi]] += _dot(heads[hh][4], w_s[c][...], "nn")
                new[hh] = [new[hh][0] + lk_sums[c], new[hh][1] + da_sums[c], new[hh][2] + _dot(dzs_s[c][...], kvs[bi], "nn")]
            return tuple(tuple(cr) for cr in new)

        def qblock(qi, _):
            qstart = pl.multiple_of(qi * blk, blk)
            qv = q_ref[pl.ds(qstart, blk), :] * scale
            dov = do_ref[pl.ds(qstart, blk), :]
            lv = l_ref[pl.ds(qstart, blk), :]
            heads = []
            for hh in range(2):
                sel = head0 if hh == 0 else ~head0
                qh, doh = jnp.where(sel, qv, jnp.zeros_like(qv)), jnp.where(sel, dov, jnp.zeros_like(dov))
                heads.append((qh, doh, jnp.max(jnp.where(sel, lv, -jnp.inf), axis=1, keepdims=True),
                              qh.astype(F32).T.astype(BF16), doh.astype(F32).T.astype(BF16)))
            zero = (jnp.zeros((blk, 1), F32), jnp.zeros((blk, 1), F32), jnp.zeros((blk, LANES), F32))
            carries = lax.fori_loop(0, qi // 2, lambda j, crs: step(heads, [2 * j, 2 * j + 1], [False, False], crs),
                                    (zero, zero))
            carries = lax.cond(qi % 2 == 1,
                               lambda crs: step(heads, [qi - 1, qi], [False, True], crs),
                               lambda crs: step(heads, [qi], [True], crs), carries)
            dq_s[pl.ds(qstart, blk), :] = jnp.where(head0, carries[0][2], carries[1][2]) * scale
            return 0

        lax.fori_loop(0, nq, qblock, 0)
        out_ref[0] = dq_s[...].astype(out_ref.dtype)
        for b in range(nq):
            out_ref[1, b * blk:(b + 1) * blk, :] = dk_s[b].T.astype(out_ref.dtype)
            out_ref[2, b * blk:(b + 1) * blk, :] = dv_s[b].T.astype(out_ref.dtype)

    col_blk = lambda off: pl.BlockSpec((t, LANES), lambda p: (0, off + p))
    return _host_call(
        body, "sb_bwd", npair, [qkv, qkv, qkv, do, ltot],
        [col_blk(0), col_blk(npair), col_blk(2 * npair), col_blk(0), col_blk(0)],
        [SDS((3, t, d), BF16)], [pl.BlockSpec((3, t, LANES), lambda p: (0, 0, p))],
        [pltpu.VMEM((t, LANES), F32)] + [pltpu.VMEM((nq, LANES, blk), F32) for _ in range(2)]
        + [pltpu.VMEM((blk, blk), BF16) for _ in range(2 + 20)]
        + [pltpu.VMEM((blk, blk), F32) for _ in range(8)], plan)


def _roll_rows(v, shift):
    return pltpu.roll(v, shift, 0)


def _shift_down(v, dist, fill, row):
    return jnp.where(row >= dist, _roll_rows(v, dist), fill)


def _shift_up(v, dist, fill, row):
    t = v.shape[0]
    return jnp.where(row < t - dist, _roll_rows(v, t - dist), fill)


def _lru_gates(xb, small, wr, wi, row):
    xs = [_shift_down(xb, 3 - tap, 0.0, row) if tap < 3 else xb for tap in range(4)]
    xc = small[4:5, :] + xs[0] * small[0:1, :]
    for tap in range(1, 4):
        xc = xc + xs[tap] * small[tap:tap + 1, :]
    xcb = xc.astype(BF16)
    r = _sigmoid(_dot(xcb, wr, "nn") + small[5:6, :])
    ig = _sigmoid(_dot(xcb, wi, "nn") + small[6:7, :])
    sp = _softplus(-small[7:8, :])
    la = -LRU_C * r * sp
    a = jnp.exp(la)
    th = jnp.tanh(la)
    m2 = -2.0 * th / (1.0 - th)
    return xs, xc, xcb, r, ig, sp, a, (jnp.sqrt(m2), m2)


def _gelu_parts(gate):
    inner = GELU_C * (gate + GELU_K * gate * gate * gate)
    th = jnp.tanh(inner)
    gelu = 0.5 * gate * (1.0 + th)
    dgelu = 0.5 * (1.0 + th) + 0.5 * gate * (1.0 - th * th) * GELU_C * (1.0 + 3.0 * GELU_K * gate * gate)
    return gelu, dgelu


def _scan_steps(t):
    steps, dist = [], 1
    while dist < t:
        steps.append(dist)
        dist *= 2
    return steps


SUBLANES = 8


def _linear_scan(a, b, scratch, row, reverse):
    a_s, b_s, carry_s = scratch
    t = a.shape[0]
    groups = t // SUBLANES
    in_group = row & (SUBLANES - 1)
    for dist in _scan_steps(SUBLANES):
        if reverse:
            inside = in_group < SUBLANES - dist
            b = b + a * jnp.where(inside, _roll_rows(b, t - dist), 0.0)
            a = a * jnp.where(inside, _roll_rows(a, t - dist), 1.0)
        else:
            inside = in_group >= dist
            b = a * jnp.where(inside, _roll_rows(b, dist), 0.0) + b
            a = a * jnp.where(inside, _roll_rows(a, dist), 1.0)
    a_s[...] = a
    b_s[...] = b
    end = 0 if reverse else SUBLANES - 1
    ends = pl.ds(end, groups, stride=SUBLANES)
    ae, be = a_s[ends, :], b_s[ends, :]
    grow = lax.broadcasted_iota(jnp.int32, ae.shape, 0)
    shift = _shift_up if reverse else _shift_down
    for dist in _scan_steps(groups):
        be = ae * shift(be, dist, 0.0, grow) + be
        ae = ae * shift(ae, dist, 1.0, grow)
    incoming = shift(be, 1, 0.0, grow)
    for k in range(SUBLANES):
        carry_s[pl.ds(k, groups, stride=SUBLANES), :] = incoming
    return a_s[...] * carry_s[...] + b_s[...]


def _lru_fwd(gx, small, wr, wi):
    t = gx.shape[0]
    r_dim = gx.shape[1] // 2
    nb = r_dim // LRU_BLOCK_W

    def body(gate_ref, xb_ref, small_ref, wr_ref, wi_ref, y_ref, hs_ref, *scratch):
        row = lax.broadcasted_iota(jnp.int32, (t, LRU_BLOCK_W), 0)
        xb = xb_ref[...]
        _, xc, _, _, ig, _, a, (mult, _) = _lru_gates(xb, small_ref, wr_ref[...], wi_ref[...], row)
        hsv = _linear_scan(a, mult * (ig * xc), scratch, row, reverse=False)
        hs_ref[...] = hsv
        gelu, _ = _gelu_parts(gate_ref[...])
        y_ref[...] = (gelu * hsv).astype(y_ref.dtype)

    colb = lambda off: pl.BlockSpec((t, LRU_BLOCK_W), lambda n: (0, off + n))
    wspec = pl.BlockSpec((None, LRU_BLOCK_W, LRU_BLOCK_W), lambda n: (n, 0, 0))
    return _pcall(
        body, name="lru_fwd", grid=(nb,),
        in_specs=[colb(0), colb(nb), pl.BlockSpec((8, LRU_BLOCK_W), lambda n: (0, n)), wspec, wspec],
        out_specs=[colb(0), colb(0)], out_shape=[SDS((t, r_dim), BF16), SDS((t, r_dim), F32)],
        scratch_shapes=[pltpu.VMEM((t, LRU_BLOCK_W), F32) for _ in range(3)],
        compiler_params=_params(("parallel",)))(gx, gx, small, wr, wi)


def _lru_bwd(gx, hs, dy, small, wr, wi, plan=None):
    t = gx.shape[0]
    r_dim = gx.shape[1] // 2
    nb = r_dim // LRU_BLOCK_W

    def body(in_refs, out_refs, scratch):
        (gate_ref, xb_ref, hs_ref, dy_ref, small_ref, wr_ref, wi_ref), (dgx_ref, dsm_ref, dwr_ref, dwi_ref) = in_refs, out_refs
        row = lax.broadcasted_iota(jnp.int32, (t, LRU_BLOCK_W), 0)
        xb, hsv, dyv, smallv = xb_ref[...], hs_ref[...], dy_ref[...], small_ref
        wrv, wiv = wr_ref[...], wi_ref[...]
        xs, xc, xcb, r, ig, sp, a, (mult, m2) = _lru_gates(xb, smallv, wrv, wiv, row)
        gelu, dgelu = _gelu_parts(gate_ref[...])
        dgx_ref[0] = (dyv * hsv * dgelu).astype(dgx_ref.dtype)
        dacc = _linear_scan(_shift_up(a, 1, 1.0, row), dyv * gelu, scratch, row, reverse=True)
        da = dacc * _shift_down(hsv, 1, 0.0, row)
        dmult = dacc * (ig * xc)
        dixc = dacc * mult
        dla = da * a - dmult * (a * a) * lax.rsqrt(m2)
        dr = dla * (-LRU_C * sp)
        dsp = jnp.sum(dla * (-LRU_C * r), axis=0, keepdims=True)
        dpr = dr * r * (1.0 - r)
        dpi = dixc * xc * ig * (1.0 - ig)
        dprb, dpib = dpr.astype(BF16), dpi.astype(BF16)
        dwr_ref[...] = _dot(xcb, dprb, "tn")
        dwi_ref[...] = _dot(xcb, dpib, "tn")
        dxc = dixc * ig + _dot(dprb, wrv, "nt") + _dot(dpib, wiv, "nt")
        dxb = dxc * smallv[3:4, :]
        for tap in range(3):
            dxb = dxb + _shift_up(dxc, 3 - tap, 0.0, row) * smallv[tap:tap + 1, :]
        dgx_ref[1] = dxb.astype(dgx_ref.dtype)
        lam = smallv[7:8, :]
        rows = [jnp.sum(dxc * xs[tap], axis=0, keepdims=True) for tap in range(4)]
        rows.append(jnp.sum(dxc, axis=0, keepdims=True))
        rows.append(jnp.sum(dpr, axis=0, keepdims=True))
        rows.append(jnp.sum(dpi, axis=0, keepdims=True))
        rows.append(-dsp * _sigmoid(-lam))
        for k, rv in enumerate(rows):
            dsm_ref[k:k + 1, :] = rv

    colb = lambda off: pl.BlockSpec((t, LRU_BLOCK_W), lambda n: (0, off + n))
    wspec = pl.BlockSpec((None, LRU_BLOCK_W, LRU_BLOCK_W), lambda n: (n, 0, 0))
    sspec = pl.BlockSpec((8, LRU_BLOCK_W), lambda n: (0, n))
    return _host_call(
        body, "lru_bwd", nb, [gx, gx, hs, dy, small, wr, wi],
        [colb(0), colb(nb), colb(0), colb(0), sspec, wspec, wspec],
        [SDS((2, t, r_dim), BF16), SDS((8, r_dim), F32), SDS((nb, LRU_BLOCK_W, LRU_BLOCK_W), F32),
         SDS((nb, LRU_BLOCK_W, LRU_BLOCK_W), F32)],
        [pl.BlockSpec((2, t, LRU_BLOCK_W), lambda n: (0, 0, n)), sspec, wspec, wspec],
        [pltpu.VMEM((t, LRU_BLOCK_W), F32) for _ in range(3)], plan)


def _adam(w, g, m, v):
    m2 = ADAM_B1 * m + (1.0 - ADAM_B1) * g
    v2 = ADAM_B2 * v + (1.0 - ADAM_B2) * (g * g)
    m_hat = m2 / (1.0 - ADAM_B1 ** ADAM_STEP)
    v_hat = v2 / (1.0 - ADAM_B2 ** ADAM_STEP)
    return -ADAM_LR * (m_hat / (jnp.sqrt(v_hat) + ADAM_EPS) + ADAM_WD * w), m2, v2


def _mod_fwd(c_all, mod_w, mod_b_cols):
    nl, d, cols = mod_w.shape
    nbatch = c_all.shape[0]

    def body(c_ref, w_ref, b_ref, o_ref):
        cv = c_ref[...]
        ca = (cv * _sigmoid(cv)).astype(BF16)
        o_ref[...] = _dot(ca, w_ref[...].astype(BF16), "nn") + b_ref[...]

    return _pcall(
        body, name="mod_fwd", grid=(nl,),
        in_specs=[pl.BlockSpec((nbatch, d), lambda l: (0, 0)), pl.BlockSpec((None, d, cols), lambda l: (l, 0, 0)),
                  pl.BlockSpec((None, 1, cols), lambda l: (l, 0, 0))],
        out_specs=pl.BlockSpec((None, nbatch, cols), lambda l: (l, 0, 0)), out_shape=SDS((nl, nbatch, cols), F32),
        compiler_params=_params(("parallel",)))(c_all, mod_w, mod_b_cols)


def _mod_w_update(c_all, dmod_cols, w, m, v):
    nl, d, cols = w.shape
    nbatch = c_all.shape[0]
    tr = _tile(d, (256, 128))

    def body(c_ref, dm_ref, w_ref, m_ref, v_ref, g_ref, dl_ref, m2_ref, v2_ref):
        cv = c_ref[...]
        ca = (cv * _sigmoid(cv)).astype(BF16)
        g = _dot(ca, dm_ref[...].astype(BF16), "tn")
        g_ref[...] = g
        dl_ref[...], m2_ref[...], v2_ref[...] = _adam(w_ref[...], g, m_ref[...], v_ref[...])

    wblk = pl.BlockSpec((None, tr, cols), lambda l, i: (l, i, 0))
    return _pcall(
        body, name="mod_w_update", grid=(nl, d // tr),
        in_specs=[pl.BlockSpec((nbatch, tr), lambda l, i: (0, i)), pl.BlockSpec((None, nbatch, cols), lambda l, i: (l, 0, 0)),
                  wblk, wblk, wblk],
        out_specs=[wblk] * 4, out_shape=[SDS(w.shape, F32)] * 4,
        compiler_params=_params(("parallel", "parallel")))(c_all, dmod_cols, w, m, v)


def _adam_update(name, w, m, v, gparts):
    rows, cols = w.shape
    tr = _tile(rows, (256, 128, 64, 32, 16, 8))
    npart = len(gparts)

    def body(*refs):
        w_ref, m_ref, v_ref = refs[:3]
        g_refs = refs[3:3 + npart]
        g_ref, dl_ref, m2_ref, v2_ref = refs[3 + npart:]
        g = g_refs[0][...].astype(F32)
        for gr in g_refs[1:]:
            g = g + gr[...].astype(F32)
        g_ref[...] = g
        dl_ref[...], m2_ref[...], v2_ref[...] = _adam(w_ref[...], g, m_ref[...], v_ref[...])

    blk = pl.BlockSpec((tr, cols), lambda i: (i, 0))
    return _pcall(body, name=name, grid=(rows // tr,), in_specs=[blk] * (3 + npart), out_specs=[blk] * 4,
                  out_shape=[SDS((rows, cols), F32)] * 4, compiler_params=_params(("parallel",)))(w, m, v, *gparts)


def _adam_shard(name, w, m, v, part4, recv3, chip_idx, first=0, fills=None):
    p, r, cdim = w.shape
    pg = part4.shape[0]
    tr = _tile(r, (256, 176, 160, 128, 64, 32, 16))

    def body(chip_ref, w_ref, m_ref, v_ref, own_ref, r0_ref, r1_ref, r2_ref, *rest):
        g_ref, dl_ref, m2_ref, v2_ref = rest[-4:]
        g = own_ref[...].astype(F32) + r0_ref[...].astype(F32) + r1_ref[...].astype(F32) + r2_ref[...].astype(F32)
        g_ref[...] = g
        dl_ref[...], m2_ref[...], v2_ref[...] = _adam(w_ref[...], g, m_ref[...], v_ref[...])

    blk = pl.BlockSpec((None, tr, cdim), lambda q, i, chip_ref: (first + q, i, 0))
    blk4 = (None, None, tr, cdim)
    slot = lambda s: pl.BlockSpec(blk4, lambda q, i, chip_ref: (s, q, i, 0))
    fills = list(fills or [])
    grid_spec = pltpu.PrefetchScalarGridSpec(
        num_scalar_prefetch=1, grid=(pg, r // tr),
        in_specs=[blk, blk, blk, pl.BlockSpec(blk4, lambda q, i, chip_ref: (q, chip_ref[0], i, 0)), slot(0), slot(1), slot(2)]
        + [ANY] * len(fills),
        out_specs=[blk] * 4)
    return _pcall(body, name=name, grid_spec=grid_spec, out_shape=[SDS((p, r, cdim), F32)] * 4,
                  input_output_aliases={8 + k: k for k in range(len(fills))},
                  compiler_params=_params(("parallel", "parallel")))(chip_idx, w, m, v, part4, recv3, recv3, recv3, *fills)


def _sum_devices(gathered, name):
    _, rows, cols = gathered.shape
    tr = _tile(rows, (512, 256, 128, 64, 32, 16, 8))

    def body(g_ref, o_ref):
        acc = g_ref[0].astype(F32)
        for k in range(1, N_DEV):
            acc = acc + g_ref[k].astype(F32)
        o_ref[...] = acc

    return _pcall(body, name=name, grid=(rows // tr,), in_specs=[pl.BlockSpec((N_DEV, tr, cols), lambda i: (0, i, 0))],
                  out_specs=pl.BlockSpec((tr, cols), lambda i: (i, 0)), out_shape=SDS((rows, cols), F32),
                  compiler_params=_params(("parallel",)))(gathered)


def _pack_flat(parts, width, row_mult, dtype):
    flat = jnp.concatenate([p.reshape(-1).astype(dtype) for p in parts])
    unit = width * row_mult
    pad = (-flat.shape[0]) % unit
    if pad:
        flat = jnp.concatenate([flat, jnp.zeros((pad,), dtype)])
    return flat.reshape(-1, width)


def _unpack_flat(flat, shapes):
    out, off = [], 0
    for shp in shapes:
        size = math.prod(shp)
        out.append(flat[off:off + size].reshape(shp))
        off += size
    return out


def kernel(x, c, mod_w, mod_b, norm_g, ffn_w_gu, ffn_w_down, sb_w_qkv, sb_w_o, lru_w_in, lru_conv_w, lru_conv_b, lru_w_r, lru_b_r, lru_w_i, lru_b_i, lru_lambda, lru_w_out, final_norm_g, loss_target, m_mod_w, m_mod_b, m_norm_g, m_ffn_w_gu, m_ffn_w_down, m_sb_w_qkv, m_sb_w_o, m_lru_w_in, m_lru_conv_w, m_lru_conv_b, m_lru_w_r, m_lru_b_r, m_lru_w_i, m_lru_b_i, m_lru_lambda, m_lru_w_out, m_final_norm_g, v_mod_w, v_mod_b, v_norm_g, v_ffn_w_gu, v_ffn_w_down, v_sb_w_qkv, v_sb_w_o, v_lru_w_in, v_lru_conv_w, v_lru_conv_b, v_lru_w_r, v_lru_b_r, v_lru_w_i, v_lru_b_i, v_lru_lambda, v_lru_w_out, v_final_norm_g):
    weights = dict(mod_w=mod_w, mod_b=mod_b, norm_g=norm_g, ffn_w_gu=ffn_w_gu, ffn_w_down=ffn_w_down, sb_w_qkv=sb_w_qkv,
                   sb_w_o=sb_w_o, lru_w_in=lru_w_in, lru_conv_w=lru_conv_w, lru_conv_b=lru_conv_b, lru_w_r=lru_w_r,
                   lru_b_r=lru_b_r, lru_w_i=lru_w_i, lru_b_i=lru_b_i, lru_lambda=lru_lambda, lru_w_out=lru_w_out,
                   final_norm_g=final_norm_g)
    mom_m = dict(mod_w=m_mod_w, mod_b=m_mod_b, norm_g=m_norm_g, ffn_w_gu=m_ffn_w_gu, ffn_w_down=m_ffn_w_down,
                 sb_w_qkv=m_sb_w_qkv, sb_w_o=m_sb_w_o, lru_w_in=m_lru_w_in, lru_conv_w=m_lru_conv_w,
                 lru_conv_b=m_lru_conv_b, lru_w_r=m_lru_w_r, lru_b_r=m_lru_b_r, lru_w_i=m_lru_w_i, lru_b_i=m_lru_b_i,
                 lru_lambda=m_lru_lambda, lru_w_out=m_lru_w_out, final_norm_g=m_final_norm_g)
    mom_v = dict(mod_w=v_mod_w, mod_b=v_mod_b, norm_g=v_norm_g, ffn_w_gu=v_ffn_w_gu, ffn_w_down=v_ffn_w_down,
                 sb_w_qkv=v_sb_w_qkv, sb_w_o=v_sb_w_o, lru_w_in=v_lru_w_in, lru_conv_w=v_lru_conv_w,
                 lru_conv_b=v_lru_conv_b, lru_w_r=v_lru_w_r, lru_b_r=v_lru_b_r, lru_w_i=v_lru_w_i, lru_b_i=v_lru_b_i,
                 lru_lambda=v_lru_lambda, lru_w_out=v_lru_w_out, final_norm_g=v_final_norm_g)
    names = list(weights)

    t, d = x.shape[1], x.shape[2]
    n_layers = mod_w.shape[0]
    r_dim = lru_w_out.shape[1] * N_DEV
    ng, rs = d // N_DEV, r_dim // N_DEV
    mod_cols = mod_w.shape[2]
    nblk = lru_w_r.shape[1]
    xi, yi, ci = _mesh_pos()
    me = 4 * xi + 2 * yi + ci
    chip = 2 * xi + yi
    x2, target = x.reshape(t, d), loss_target.reshape(t, d)

    lru_small_shard = jnp.concatenate([lru_conv_w[0], lru_conv_b, lru_b_r, lru_b_i, lru_lambda], axis=0)
    small1 = _pack_flat([c, norm_g, lru_small_shard], LANES, 8, F32)
    n_small1 = small1.shape[0]
    all1 = _allgather(small1[None], "gather_small").reshape(N_DEV, n_small1 * LANES)
    c_all = all1[:, :d]
    norm_full = jnp.transpose(all1[:, d:d + 6 * ng].reshape(N_DEV, n_layers, 3, ng), (1, 2, 0, 3)).reshape(n_layers, 3, d)
    lru_small = jnp.transpose(all1[:, d + 6 * ng:d + 6 * ng + 8 * rs].reshape(N_DEV, 8, rs), (1, 0, 2)).reshape(8, r_dim)

    mod_b_cols = lax.dynamic_slice_in_dim(mod_b, me * mod_cols, mod_cols, axis=1).reshape(n_layers, 1, mod_cols)
    mod_part = _mod_fwd(c_all, mod_w, mod_b_cols)

    assert sb_w_qkv.shape[0] == 1 and lru_w_in.shape[0] == 1, "one stick-breaking and one RG-LRU layer"
    n_ffn = 2 * n_layers
    fc = ffn_w_gu.shape[3]
    cw_in = lru_w_in.shape[2]
    pieces = {("ffn_w_gu", q): ffn_w_gu[q // 2, q % 2][None] for q in range(n_ffn)}
    pieces.update({("ffn_w_down", q): ffn_w_down[q // 2, q % 2][None] for q in range(n_ffn)})
    pieces.update({("sb_w_qkv", 0): sb_w_qkv, ("sb_w_o", 0): sb_w_o, ("lru_w_in", 0): lru_w_in, ("lru_w_out", 0): lru_w_out})
    col_window = {("sb_w_qkv", 0)}
    first = [("ffn_w_gu", 0)]
    behind = {"l0s0_gu": [("ffn_w_down", 0)], "l0s0_down": [("sb_w_qkv", 0), ("sb_w_o", 0)],
              "l0s2_gu": [("ffn_w_down", n_ffn - 1)], "l0s2_down": [("ffn_w_down", 2)],
              "l1s0_gu": [("lru_w_in", 0)], "l1s0_down": [("lru_w_out", 0)]}
    behind["sb_fwd"] = [key for key in pieces if key not in first + sum(behind.values(), [])]
    gathered = {}

    def gather_plan(keys):
        return _gather_plan([pieces[key].astype(BF16) for key in keys], [key in col_window for key in keys])

    def hosting(name, call):
        keys = behind.get(name, [])
        outs = call(gather_plan(keys) if keys else None)
        gathered.update(zip(keys, outs[len(outs) - len(keys):]))
        return outs[:len(outs) - len(keys)]

    mod_all, *landed = _run_comm(_merge_plans([_gather_plan([mod_part], [False]), gather_plan(first)]), "gather_mod_and_first")
    gathered.update(zip(first, landed))
    mod_mine = lax.dynamic_index_in_dim(mod_all, me, axis=2, keepdims=False)
    mod_mine = mod_mine.reshape(n_layers, 3, 3, d)
    wr_b, wi_b = lru_w_r[0].astype(BF16), lru_w_i[0].astype(BF16)
    eye2 = jnp.eye(2 * cw_in, dtype=BF16).reshape(2, cw_in, 2 * cw_in)

    def w_gu(q):
        return gathered[("ffn_w_gu", q)]

    def w_d4(q):
        return gathered[("ffn_w_down", q)].reshape(1, HIDDEN_CHUNKS, fc, d)

    saved = []
    xcur = x2
    for layer in range(n_layers):
        for sub in range(3):
            gvec = norm_full[layer, sub].reshape(1, d)
            shift = mod_mine[layer, sub, 0].reshape(1, d)
            scale1p = 1.0 + mod_mine[layer, sub, 1].reshape(1, d)
            gmul = 1.0 + mod_mine[layer, sub, 2].reshape(1, d)
            tag = f"l{layer}s{sub}"
            h = _norm_fwd(xcur, gvec, scale1p, shift, tag + "_norm")
            rec = dict(x=xcur, h=h, g=gvec, scale1p=scale1p, gmul=gmul, w=MACARON_W if sub != 1 else 1.0)
            if sub != 1:
                lj = layer * 2 + sub // 2
                gu2, a = hosting(tag + "_gu", lambda plan: _ffn_gu(tag + "_gu", h, w_gu(lj), plan))
                yv, xcur = hosting(tag + "_down", lambda plan: _ffn_down(tag + "_down", a, w_d4(lj), xcur, gmul, plan))
                rec.update(kind="ffn", lj=lj, gu2=gu2, a=a, y=yv)
            elif layer % 2 == 0:
                w_qkv = gathered[("sb_w_qkv", 0)][0]
                w_o = gathered[("sb_w_o", 0)].reshape(d, d)
                qkv = _mm_nn(tag + "_qkv", h, w_qkv, BF16)[0]
                o, ltot = hosting("sb_fwd", lambda plan: _sb_fwd(qkv, d, plan))
                yv, xcur = _mm_nn(tag + "_wo", o, w_o, [BF16, F32], extras=[(xcur, "tile"), (gmul, "row")],
                                  epilogue=lambda accs, ex: (accs[0], ex[0] + ex[1] * accs[0]))
                rec.update(kind="sb", qkv=qkv, o=o, ltot=ltot, y=yv, w_qkv=w_qkv, w_o=w_o)
            else:
                w_in = _chunks_to_cols("lru_w_in_cols", gathered[("lru_w_in", 0)][0], eye2)
                w_out = gathered[("lru_w_out", 0)].reshape(r_dim, d)
                gx = _mm_nn(tag + "_win", h, w_in, F32)[0]
                ymix, hs = _lru_fwd(gx, lru_small, wr_b, wi_b)
                yv, xcur = _mm_nn(tag + "_wout", ymix, w_out, [BF16, F32], extras=[(xcur, "tile"), (gmul, "row")],
                                  epilogue=lambda accs, ex: (accs[0], ex[0] + ex[1] * accs[0]))
                rec.update(kind="lru", gx=gx, hs=hs, ymix=ymix, y=yv, w_in=w_in, w_out=w_out)
            saved.append(rec)

    last = saved[-1]
    dxo, dy, head_sums = _loss_head(xcur, target, final_norm_g.reshape(1, d), (last["w"] * last["gmul"]))
    loss_mine = head_sums[1, 0:1]
    dgf = head_sums[0]

    c_idx = jnp.reshape(ci, (1,)).astype(jnp.int32)
    chip_idx = jnp.reshape(chip, (1,)).astype(jnp.int32)
    grads, reduced = {}, {}
    to_pair = []
    to_chips = []

    def sibling_plan(only=None):
        keys = [key for key in to_pair if only is None or key in only]
        if not keys:
            return None, keys
        return _exchange_plan([grads[key] for key in keys], [key in col_window for key in keys], 4, _sibling_route), keys

    def sibling_done(keys, recv4):
        for key, r4 in zip(keys, recv4):
            to_pair.remove(key)
            to_chips.append((key, _pair_sum(grads[key], r4, c_idx, f"rs_pair_sum_{key[0]}{key[1]}", cols=key in col_window)))

    def chip_plan(only=None):
        items = [item for item in to_chips if only is None or item[0] in only]
        if not items:
            return None, items
        return _exchange_plan([p4 for _, p4 in items], [False] * len(items), 3, _chip_route), items

    def chips_done(items, recv3):
        for item, r3 in zip(items, recv3):
            to_chips.remove(item)
            reduced[item[0]] = (item[1], r3)

    def behind(call, make_plan, done, more=None):
        plan, items = make_plan()
        n_mine = len(plan.outs) if plan else 0
        n_more = len(more.outs) if more else 0
        outs = call(_merge_plans([plan, more]))
        n_own = len(outs) - n_mine - n_more
        done(items, outs[n_own:n_own + n_mine])
        return list(outs[:n_own]) + list(outs[n_own + n_mine:])

    carried = {
        "l1s1b_dymix": ("sibling", None), "lru_bwd": ("chips", [("ffn_w_gu", n_ffn - 1)]),
        "l1s0b_da": ("sibling", None),
        "l0s2b_da": ("sibling", None),
    }

    def carrying(name, call):
        if name not in carried:
            return call(None)
        stage, only = carried[name]
        if stage == "sibling":
            return behind(call, functools.partial(sibling_plan, only), sibling_done)
        return behind(call, functools.partial(chip_plan, only), chips_done)

    def at_once(make_plan, done, name):
        plan, items = make_plan()
        if plan:
            done(items, _run_comm(plan, name))

    def add_grad(key, value):
        grads[key] = value
        to_pair.append(key)

    dmod = [[None] * 3 for _ in range(n_layers)]
    dnorm = [[None] * 3 for _ in range(n_layers)]
    dlru_small = wri_all = None
    for idx in reversed(range(len(saved))):
        rec = saved[idx]
        layer, sub = divmod(idx, 3)
        tag = f"l{layer}s{sub}b"
        if rec["kind"] == "ffn" and idx > 0:
            lj = rec["lj"]
            (dgu2,) = carrying(tag + "_da", lambda plan: _ffn_da(tag + "_da", dy, w_d4(lj), rec["gu2"], plan))
            dwd = _ffn_dwd(tag + "_dwd", rec["a"], dy)[0].reshape(gathered[("ffn_w_down", lj)].shape)
            (dwgu,) = carrying(tag + "_dwgu", lambda plan: _ffn_dwgu(tag + "_dwgu", rec["h"], dgu2, plan))
            (dh,) = carrying(tag + "_dh", lambda plan: _ffn_dh(tag + "_dh", dgu2, w_gu(lj), plan))
            add_grad(("ffn_w_down", lj), dwd)
            add_grad(("ffn_w_gu", lj), dwgu)
        elif rec["kind"] == "ffn":
            lj = rec["lj"]
            at_once(sibling_plan, sibling_done, "rs_sibling_" + tag)
            (dgu2,) = behind(lambda plan: _ffn_da(tag + "_da", dy, w_d4(lj), rec["gu2"], plan), chip_plan, chips_done)
            add_grad(("ffn_w_down", lj), _ffn_dwd(tag + "_dwd", rec["a"], dy)[0].reshape(gathered[("ffn_w_down", lj)].shape))
            at_once(sibling_plan, sibling_done, "rs_sibling_" + tag + "_dwd")
            (dwgu,) = behind(lambda plan: _ffn_dwgu(tag + "_dwgu", rec["h"], dgu2, plan), chip_plan, chips_done)
            add_grad(("ffn_w_gu", lj), dwgu)
            at_once(sibling_plan, sibling_done, "rs_sibling_" + tag + "_dwgu")
            (dh,) = behind(lambda plan: _ffn_dh(tag + "_dh", dgu2, w_gu(lj), plan), chip_plan, chips_done)
        elif rec["kind"] == "sb":
            at_once(sibling_plan, sibling_done, "rs_sibling_" + tag)
            do = _mm_nt(tag + "_do", dy, rec["w_o"], BF16)
            dwo = _mm_tn(tag + "_dwo", rec["o"], dy, BF16)
            wri = _pack_flat([dwr, dwi], LANES, 512, BF16)[None]
            dqkv3, wri_all = behind(lambda plan: _sb_bwd(rec["qkv"], do, rec["ltot"], d, plan), chip_plan, chips_done,
                                    more=_gather_plan([wri], [False]))
            add_grad(("sb_w_o", 0), dwo.reshape(gathered[("sb_w_o", 0)].shape))
            dh = _mm_nt_stack(tag + "_dh", dqkv3, rec["w_qkv"], F32)
            add_grad(("sb_w_qkv", 0), _mm_tn_stack(tag + "_dwqkv", rec["h"], dqkv3, BF16)[None])
        else:
            (dymix,) = carrying(tag + "_dymix", lambda plan: _mm_nt(tag + "_dymix", dy, rec["w_out"], F32, plan)
                                if plan else [_mm_nt(tag + "_dymix", dy, rec["w_out"], F32)])
            dwout = _mm_tn(tag + "_dwout", rec["ymix"], dy, BF16).reshape(gathered[("lru_w_out", 0)].shape)
            dgx2, dlru_small, dwr, dwi = carrying("lru_bwd", lambda plan: _lru_bwd(rec["gx"], rec["hs"], dymix, lru_small,
                                                                                wr_b, wi_b, plan))
            dh = _mm_nt_stack(tag + "_dh", dgx2, rec["w_in"], F32)
            dw_in = _mm_tn_stack(tag + "_dwin", rec["h"], dgx2, BF16)
            add_grad(("lru_w_out", 0), dwout)
            add_grad(("lru_w_in", 0), _cols_to_chunks("lru_w_in_chunks", dw_in, eye2)[None])
        prev = saved[idx - 1] if idx > 0 else None
        gw_prev = (prev["w"] * prev["gmul"]) if prev is not None else jnp.zeros((1, d), F32)
        dxo, dy, sums = _adaln_bwd(dh, rec["x"], rec["y"], dxo, rec["g"], rec["scale1p"], rec["w"], gw_prev, tag + "_adaln")
        dmod[layer][sub] = sums[0:3]
        dnorm[layer][sub] = sums[3]
    grad_x = dxo.reshape(x.shape)

    dmod_mine = jnp.stack([jnp.stack(dmod[layer]) for layer in range(n_layers)])
    dnorm_mine = jnp.stack([jnp.stack(dnorm[layer]) for layer in range(n_layers)])
    assert not to_pair and not to_chips
    small_shapes = [(n_layers, 9 * d), (n_layers, 3, d), (8, r_dim), (d,), (1,)]
    small3 = _pack_flat([dmod_mine, dnorm_mine, dlru_small, dgf, loss_mine], LANES, 256, F32)
    n_small3 = small3.shape[0]
    all3 = _allgather(small3[None], "gather_small_grads").reshape(N_DEV, n_small3, LANES)
    gsum = _sum_devices(all3, "sum_small_grads").reshape(-1)
    g_mod_b, g_norm_full, g_lru_small, g_final, loss_sum = _unpack_flat(gsum, small_shapes)
    loss = loss_sum[0]
    wri_sum = _sum_devices(wri_all.reshape(N_DEV, -1, LANES), "sum_gate_weight_grads").reshape(-1)
    g_wr, g_wi = _unpack_flat(wri_sum, [lru_w_r.shape, lru_w_i.shape])
    dmod_all = all3.reshape(N_DEV, -1)[:, :n_layers * 9 * d].reshape(N_DEV, n_layers, N_DEV, mod_cols)
    dmod_cols = jnp.transpose(lax.dynamic_index_in_dim(dmod_all, me, axis=2, keepdims=False), (1, 0, 2))

    out_g, out_d, out_m, out_v = {}, {}, {}, {}
    out_g["mod_w"], out_d["mod_w"], out_m["mod_w"], out_v["mod_w"] = _mod_w_update(c_all, dmod_cols, mod_w, m_mod_w, v_mod_w)

    g_norm_shard = lax.dynamic_slice_in_dim(g_norm_full, me * ng, ng, axis=2)
    g_lru_shard = lax.dynamic_slice_in_dim(g_lru_small, me * rs, rs, axis=1)
    small_grads = dict(mod_b=g_mod_b, norm_g=g_norm_shard, lru_conv_w=g_lru_shard[0:4].reshape(lru_conv_w.shape),
                       lru_conv_b=g_lru_shard[4:5], lru_b_r=g_lru_shard[5:6], lru_b_i=g_lru_shard[6:7],
                       lru_lambda=g_lru_shard[7:8], final_norm_g=g_final)
    for n, g in (("lru_w_r", g_wr), ("lru_w_i", g_wi)):
        view = lambda arr: arr.reshape(-1, LRU_BLOCK_W)
        outs = _adam_update("adam_" + n, view(weights[n]), view(mom_m[n]), view(mom_v[n]), [view(g)])
        out_g[n], out_d[n], out_m[n], out_v[n] = [o.reshape(weights[n].shape) for o in outs]
    small_names = list(small_grads)
    sw = _pack_flat([weights[n] for n in small_names], LANES, 256, F32)
    sg = _pack_flat([small_grads[n] for n in small_names], LANES, 256, F32)
    sm = _pack_flat([mom_m[n] for n in small_names], LANES, 256, F32)
    sv = _pack_flat([mom_v[n] for n in small_names], LANES, 256, F32)
    s_outs = _adam_update("adam_small", sw, sm, sv, [sg])
    small_shapes2 = [weights[n].shape for n in small_names]
    for dst, flat in zip((out_g, out_d, out_m, out_v), s_outs):
        for n, arr in zip(small_names, _unpack_flat(flat.reshape(-1), small_shapes2)):
            dst[n] = arr

    for n in ["ffn_w_gu", "ffn_w_down", "sb_w_qkv", "sb_w_o", "lru_w_in", "lru_w_out"]:
        shp = weights[n].shape
        shard3 = (math.prod(shp[:-2]),) + shp[-2:]
        view = lambda arr: arr.reshape(shard3)
        outs = None
        for q in range(shard3[0]):
            fills = outs if outs is not None else [lax.empty(shard3, F32) for _ in range(4)]
            p4, r3 = reduced[(n, q)]
            outs = _adam_shard(f"adam_{n}{q}", view(weights[n]), view(mom_m[n]), view(mom_v[n]), p4, r3, chip_idx,
                               first=q, fills=fills if shard3[0] > 1 else None)
        out_g[n], out_d[n], out_m[n], out_v[n] = [o.reshape(shp) for o in outs]

    return (loss, grad_x, *[out_g[n] for n in names], *[out_d[n] for n in names], *[out_m[n] for n in names],
            *[out_v[n] for n in names])
```

```python
import functools
import math

import jax
import jax.numpy as jnp
from jax import lax
from jax.experimental import pallas as pl
from jax.experimental.pallas import tpu as pltpu

F32 = jnp.float32
BF16 = jnp.bfloat16
SDS = jax.ShapeDtypeStruct
MESH = pl.DeviceIdType.MESH
ANY = pl.BlockSpec(memory_space=pl.ANY)

N_DEV = 8
LANES = 128
HEAD_DIM = 64
LRU_BLOCK_W = 128
LRU_C = 8.0
MACARON_W = 0.5
NORM_EPS = 1e-6
ADAM_LR = 0.001
ADAM_B1 = 0.9
ADAM_B2 = 0.999
ADAM_EPS = 1e-08
ADAM_WD = 0.01
ADAM_STEP = 10
VMEM_LIMIT = 56 * 1024 * 1024
GELU_C = math.sqrt(2.0 / math.pi)
GELU_K = 0.044715

DIMS = {
    "nn": (((1,), (0,)), ((), ())),
    "nt": (((1,), (1,)), ((), ())),
    "tn": (((0,), (0,)), ((), ())),
}


def _pcall(body, **kw):
    return pl.pallas_call(body, **kw)


def _params(sem=None):
    return pltpu.CompilerParams(dimension_semantics=sem, vmem_limit_bytes=VMEM_LIMIT)


def _tile(n, prefs):
    for p in prefs:
        if n % p == 0:
            return p
    return n


def _dot(a, b, dims):
    return lax.dot_general(a, b, DIMS[dims], preferred_element_type=F32)


def _softplus(z):
    return jnp.maximum(z, 0.0) + jnp.log(1.0 + jnp.exp(-jnp.abs(z)))


def _sigmoid(z):
    return 0.5 * jnp.tanh(0.5 * z) + 0.5


def _mesh_pos():
    return lax.axis_index("x"), lax.axis_index("y"), lax.axis_index("c")


def _allgather(xs, name, cols=False):
    return _run_comm(_gather_plan([xs], [cols]), name)[0]


class _CommPlan:
    def __init__(self, ins, outs, n_remote, n_local, phases):
        self.ins, self.outs, self.n_remote, self.n_local, self.phases = ins, outs, n_remote, n_local, phases

    def scratch(self):
        return [pltpu.SemaphoreType.DMA((self.n_remote,)), pltpu.SemaphoreType.DMA((self.n_remote,)),
                pltpu.SemaphoreType.DMA((max(self.n_local, 1),))]


def _merge_plans(plans):
    plans = [p for p in plans if p is not None]
    if len(plans) <= 1:
        return plans[0] if plans else None

    def phase(k):
        def run(in_refs, out_refs, send_sems, recv_sems, local_sems, r0=0, l0=0):
            i0 = o0 = 0
            for p in plans:
                p.phases[k](in_refs[i0:i0 + len(p.ins)], out_refs[o0:o0 + len(p.outs)], send_sems, recv_sems, local_sems, r0, l0)
                i0, o0, r0, l0 = i0 + len(p.ins), o0 + len(p.outs), r0 + p.n_remote, l0 + p.n_local
        return run

    return _CommPlan(sum([p.ins for p in plans], []), sum([p.outs for p in plans], []), sum(p.n_remote for p in plans),
                     sum(p.n_local for p in plans), [phase(0), phase(1), phase(2)])


def _run_comm(plan, name):
    n_in, n_out = len(plan.ins), len(plan.outs)

    def body(*refs):
        in_refs, out_refs, sems = refs[:n_in], refs[n_in:n_in + n_out], refs[n_in + n_out:]
        for phase in plan.phases:
            phase(in_refs, out_refs, *sems)

    return _pcall(body, name=name, out_shape=plan.outs, in_specs=[ANY] * n_in, out_specs=[ANY] * n_out,
                  scratch_shapes=plan.scratch())(*plan.ins)


def _col_window(ref, idx, width):
    return ref.at[:, :, pl.ds(pl.multiple_of(idx * width, math.gcd(width, LANES)), width)]


def _gather_plan(shards, cols):
    n = len(shards)
    outs = [SDS((s.shape[0], s.shape[1], N_DEV * s.shape[2]) if cl else (s.shape[0], N_DEV) + s.shape[1:], s.dtype)
            for s, cl in zip(shards, cols)]

    def copies(a, in_refs, out_refs, send_sems, recv_sems, local_sems, r0=0, l0=0):
        x, y, c = _mesh_pos()
        sibling = (x, y, 1 - c)
        chips = [(1 - x, y), (x, 1 - y), (1 - x, 1 - y)]
        width = shards[a].shape[2]

        def block(px, py, pc):
            idx = 4 * px + 2 * py + pc
            return _col_window(out_refs[a], idx, width) if cols[a] else out_refs[a].at[:, idx]

        def copy(k, owner, to, src=None):
            sem = r0 + 7 * a + k
            return pltpu.make_async_remote_copy(
                src_ref=block(*owner) if src is None else src, dst_ref=block(*owner),
                send_sem=send_sems.at[sem], recv_sem=recv_sems.at[sem], device_id=to, device_id_type=MESH)

        me = (x, y, c)
        first = [copy(0, me, sibling, src=in_refs[a])]
        first += [copy(1 + j, me, (*chip, c), src=in_refs[a]) for j, chip in enumerate(chips)]
        passed = [copy(4 + j, (*chip, c), sibling) for j, chip in enumerate(chips)]
        landed = [copy(1 + j, (*chip, c), me) for j, chip in enumerate(chips)]
        from_sibling = [copy(0, sibling, me)] + [copy(4 + j, (*chip, 1 - c), me) for j, chip in enumerate(chips)]
        mine = pltpu.make_async_copy(in_refs[a], block(*me), local_sems.at[l0 + a])
        return first, passed, landed, from_sibling, mine

    def start(*refs):
        for a in range(n):
            first, _, _, _, mine = copies(a, *refs)
            mine.start()
            for cp in first:
                cp.start()

    def pass_on(*refs):
        for a in range(n):
            _, passed, landed, _, _ = copies(a, *refs)
            for cp, fwd in zip(landed, passed):
                cp.wait_recv()
                fwd.start()

    def finish(*refs):
        for a in range(n):
            first, passed, _, from_sibling, mine = copies(a, *refs)
            for cp in from_sibling:
                cp.wait_recv()
            for cp in first + passed:
                cp.wait_send()
            mine.wait()

    return _CommPlan(list(shards), outs, 7 * n, n, [start, pass_on, finish])


def _exchange_plan(srcs, cols, n_slots, route):
    n = len(srcs)
    outs = []
    for g, cl in zip(srcs, cols):
        shard = (g.shape[0], g.shape[1], g.shape[2] // N_DEV) if cl else (g.shape[0],) + g.shape[2:]
        outs.append(SDS((n_slots,) + shard, g.dtype))

    def copies(in_refs, out_refs, send_sems, recv_sems, local_sems, r0=0, l0=0):
        x, y, c = _mesh_pos()
        made = []
        for a in range(n):
            for s in range(n_slots):
                chunk, target = route(x, y, c, s)
                src = _col_window(in_refs[a], chunk, outs[a].shape[3]) if cols[a] else in_refs[a].at[:, chunk]
                sem = r0 + a * n_slots + s
                made.append(pltpu.make_async_remote_copy(
                    src_ref=src, dst_ref=out_refs[a].at[s], send_sem=send_sems.at[sem], recv_sem=recv_sems.at[sem],
                    device_id=target, device_id_type=MESH))
        return made

    def start(*refs):
        for cp in copies(*refs):
            cp.start()

    def nothing(*refs):
        pass

    def finish(*refs):
        made = copies(*refs)
        for cp in made:
            cp.wait_recv()
        for cp in made:
            cp.wait_send()

    return _CommPlan(list(srcs), outs, n * n_slots, 0, [start, nothing, finish])


def _sibling_route(x, y, c, k):
    return 2 * k + 1 - c, (x, y, 1 - c)


def _chip_route(x, y, c, j):
    px, py = [(1 - x, y), (x, 1 - y), (1 - x, 1 - y)][j]
    return 2 * px + py, (px, py, c)


def _pair_sum(grads, recv4, c_idx, name, cols=False):
    _, p, r, cdim = recv4.shape
    tr = _tile(r, (512, 256, 176, 160, 128, 64, 32, 16))

    def body(c_ref, a_ref, b_ref, o_ref):
        o_ref[...] = (a_ref[...].astype(F32) + b_ref[...].astype(F32)).astype(o_ref.dtype)

    blk = (None, None, tr, cdim)
    if cols:
        own = pl.BlockSpec((None, tr, cdim), lambda k, q, i, c_ref: (q, i, 2 * k + c_ref[0]))
    else:
        own = pl.BlockSpec(blk, lambda k, q, i, c_ref: (q, 2 * k + c_ref[0], i, 0))
    grid_spec = pltpu.PrefetchScalarGridSpec(
        num_scalar_prefetch=1, grid=(4, p, r // tr),
        in_specs=[own, pl.BlockSpec(blk, lambda k, q, i, c_ref: (k, q, i, 0))],
        out_specs=pl.BlockSpec(blk, lambda k, q, i, c_ref: (q, k, i, 0)))
    return _pcall(body, name=name, grid_spec=grid_spec, out_shape=SDS((p, 4, r, cdim), grads.dtype),
                  compiler_params=_params(("parallel", "parallel", "parallel")))(c_idx, grads, recv4)


def _mm(name, ins, prods, n_acc, acc_shape, epi_idx, epilogue, out_shapes, out_specs, grid, dims, plan=None):
    n_in, n_out, nk = len(ins), len(out_shapes), grid[2]
    n_acc_refs = n_acc if nk > 1 else 0
    c_ins, c_outs = (plan.ins, plan.outs) if plan else ([], [])
    n_cin, n_cout = len(c_ins), len(c_outs)

    def body(*refs):
        in_refs, c_in = refs[:n_in], refs[n_in:n_in + n_cin]
        rest = refs[n_in + n_cin:]
        out_refs, c_out = rest[:n_out], rest[n_out:n_out + n_cout]
        rest = rest[n_out + n_cout:]
        acc_refs, sems = rest[:n_acc_refs], rest[n_acc_refs:]
        ids = [pl.program_id(axis) for axis in range(3)]
        if plan:
            @pl.when((ids[0] == 0) & (ids[1] == 0) & (ids[2] == 0))
            def _():
                plan.phases[0](c_in, c_out, *sems)

        def finish(accs):
            outs = epilogue(accs, [in_refs[i][...] for i in epi_idx])
            for o_ref, o in zip(out_refs, outs):
                if isinstance(o, tuple):
                    for plane, part in enumerate(o):
                        o_ref[plane] = part.astype(o_ref.dtype)
                else:
                    o_ref[...] = o.astype(o_ref.dtype)

        def operand(ref_idx):
            if isinstance(ref_idx, tuple):
                return in_refs[ref_idx[0]][ref_idx[1]]
            return in_refs[ref_idx][...]

        if nk == 1:
            accs = [None] * n_acc
            for ia, ib, iacc in prods:
                term = _dot(operand(ia), operand(ib), dims)
                accs[iacc] = term if accs[iacc] is None else accs[iacc] + term
            finish(accs)
        else:
            @pl.when(ids[2] == 0)
            def _():
                for acc in acc_refs:
                    acc[...] = jnp.zeros_like(acc)

            for ia, ib, iacc in prods:
                acc_refs[iacc][...] += _dot(operand(ia), operand(ib), dims)

            @pl.when(ids[2] == nk - 1)
            def _():
                finish([acc[...] for acc in acc_refs])

        if plan:
            @pl.when((ids[0] == grid[0] - 1) & (ids[1] == grid[1] - 1) & (ids[2] == nk - 1))
            def _():
                plan.phases[1](c_in, c_out, *sems)
                plan.phases[2](c_in, c_out, *sems)

    return _pcall(
        body, name=name, grid=grid, in_specs=[s for _, s in ins] + [ANY] * n_cin,
        out_specs=list(out_specs) + [ANY] * n_cout, out_shape=list(out_shapes) + list(c_outs),
        scratch_shapes=[pltpu.VMEM(acc_shape, F32) for _ in range(n_acc_refs)] + (plan.scratch() if plan else []),
        compiler_params=_params(("arbitrary",) * 3 if plan else ("parallel", "parallel", "arbitrary")),
    )(*[a for a, _ in ins], *c_ins)


def _plain(accs, _):
    return accs


def _mm_nn(name, a, b, out_dtype, extras=(), epilogue=_plain, n_out=1):
    m, kd = a.shape
    n = b.shape[1]
    tm, tn, tk = _tile(m, (1024, 512, 256, 128)), _tile(n, (640, 512, 256, 128)), _tile(kd, (1280, 1024, 512, 256, 128))
    ins = [(a, pl.BlockSpec((tm, tk), lambda i, j, k: (i, k))), (b, pl.BlockSpec((tk, tn), lambda i, j, k: (k, j)))]
    for arr, kind in extras:
        if kind == "tile":
            ins.append((arr, pl.BlockSpec((tm, tn), lambda i, j, k: (i, j))))
        else:
            ins.append((arr, pl.BlockSpec((1, tn), lambda i, j, k: (0, j))))
    dts = out_dtype if isinstance(out_dtype, (list, tuple)) else [out_dtype] * n_out
    return _mm(name, ins, [(0, 1, 0)], 1, (tm, tn), list(range(2, len(ins))), epilogue,
               [SDS((m, n), dt) for dt in dts], [pl.BlockSpec((tm, tn), lambda i, j, k: (i, j)) for _ in dts],
               (m // tm, n // tn, kd // tk), "nn")


def _mm_nt(name, a, b, out_dtype, plan=None):
    m, kd = a.shape
    n = b.shape[0]
    tm, tn, tk = _tile(m, (1024, 512, 256, 128)), _tile(n, (640, 512, 256, 128)), _tile(kd, (1024, 512, 256, 128))
    ins = [(a, pl.BlockSpec((tm, tk), lambda i, j, k: (i, k))), (b, pl.BlockSpec((tn, tk), lambda i, j, k: (j, k)))]
    outs = _mm(name, ins, [(0, 1, 0)], 1, (tm, tn), [], _plain, [SDS((m, n), out_dtype)],
               [pl.BlockSpec((tm, tn), lambda i, j, k: (i, j))], (m // tm, n // tn, kd // tk), "nt", plan=plan)
    return outs if plan else outs[0]


def _mm_tn(name, a, b, out_dtype):
    t, m = a.shape
    n = b.shape[1]
    tm, tn, tk = _tile(m, (640, 512, 256, 128)), _tile(n, (1024, 512, 256, 128)), t
    ins = [(a, pl.BlockSpec((tk, tm), lambda i, j, k: (k, i))), (b, pl.BlockSpec((tk, tn), lambda i, j, k: (k, j)))]
    return _mm(name, ins, [(0, 1, 0)], 1, (tm, tn), [], _plain, [SDS((m, n), out_dtype)],
               [pl.BlockSpec((tm, tn), lambda i, j, k: (i, j))], (m // tm, n // tn, t // tk), "tn")[0]


def _mm_nt_stack(name, a3, b, out_dtype):
    cc, m, kd = a3.shape
    n = b.shape[0]
    tm, tn, tk = _tile(m, (1024, 512, 256, 128)), _tile(n, (1024, 512, 256, 128)), _tile(kd, (1280, 1024, 512, 256, 128))
    nk = kd // tk
    ins = [(a3, pl.BlockSpec((None, tm, tk), lambda i, j, k: (k // nk, i, k % nk))),
           (b, pl.BlockSpec((tn, tk), lambda i, j, k: (j, k)))]
    return _mm(name, ins, [(0, 1, 0)], 1, (tm, tn), [], _plain, [SDS((m, n), out_dtype)],
               [pl.BlockSpec((tm, tn), lambda i, j, k: (i, j))], (m // tm, n // tn, cc * nk), "nt")[0]


def _mm_tn_stack(name, a, b3, out_dtype):
    t, m = a.shape
    cc, _, n = b3.shape
    tm, tn, tk = _tile(m, (512, 256, 128)), _tile(n, (1280, 1024, 512, 256, 128)), t
    nj = n // tn
    ins = [(a, pl.BlockSpec((tk, tm), lambda i, j, k: (k, i))),
           (b3, pl.BlockSpec((None, tk, tn), lambda i, j, k: (j // nj, k, j % nj)))]
    return _mm(name, ins, [(0, 1, 0)], 1, (tm, tn), [], _plain, [SDS((m, cc * n), out_dtype)],
               [pl.BlockSpec((tm, tn), lambda i, j, k: (i, j))], (m // tm, cc * nj, t // tk), "tn")[0]


def _chunks_to_cols(name, wc, eye2):
    nch, d, cw = wc.shape
    tm = _tile(d, (1024, 512, 256, 128))
    ins = [(wc, pl.BlockSpec((None, tm, cw), lambda i, j, k: (2 * j + k, i, 0))),
           (eye2, pl.BlockSpec((None, cw, 2 * cw), lambda i, j, k: (k, 0, 0)))]
    return _mm(name, ins, [(0, 1, 0)], 1, (tm, 2 * cw), [], _plain, [SDS((d, nch * cw), wc.dtype)],
               [pl.BlockSpec((tm, 2 * cw), lambda i, j, k: (i, j))], (d // tm, nch // 2, 2), "nn")[0]


def _cols_to_chunks(name, full, eye2):
    d, n = full.shape
    _, cw, _ = eye2.shape
    nch = n // cw
    tm = _tile(d, (1024, 512, 256, 128))
    ins = [(full, pl.BlockSpec((tm, 2 * cw), lambda i, j, k: (i, j // 2))),
           (eye2, pl.BlockSpec((None, cw, 2 * cw), lambda i, j, k: (j % 2, 0, 0)))]
    return _mm(name, ins, [(0, 1, 0)], 1, (tm, cw), [], _plain, [SDS((nch, d, cw), full.dtype)],
               [pl.BlockSpec((None, tm, cw), lambda i, j, k: (j, i, 0))], (d // tm, nch, 1), "nt")[0]


def _row_tile(t):
    return _tile(t, (256, 128, 64, 32, 16, 8))


def _norm_fwd(x, g, scale1p, shift, name):
    t, d = x.shape
    tr = _row_tile(t)

    def body(x_ref, g_ref, s_ref, b_ref, h_ref):
        xv = x_ref[...]
        inv = lax.rsqrt(jnp.mean(xv * xv, axis=-1, keepdims=True) + NORM_EPS)
        h_ref[...] = ((xv * inv) * g_ref[...] * s_ref[...] + b_ref[...]).astype(h_ref.dtype)

    vec = pl.BlockSpec((1, d), lambda i: (0, 0))
    return _pcall(body, name=name, grid=(t // tr,), in_specs=[pl.BlockSpec((tr, d), lambda i: (i, 0)), vec, vec, vec],
                  out_specs=pl.BlockSpec((tr, d), lambda i: (i, 0)), out_shape=SDS((t, d), BF16),
                  compiler_params=_params(("parallel",)))(x, g, scale1p, shift)


def _adaln_bwd(dh, x, y, dxo, g, scale1p, w_sub, gw_prev, name, plan=None):
    t, d = x.shape
    tr = _row_tile(t)

    def body(in_refs, out_refs, _):
        (dh_ref, x_ref, y_ref, dxo_ref, g_ref, s_ref, gw_ref), (dx_ref, dyp_ref, sums_ref) = in_refs, out_refs
        i = pl.program_id(0)

        @pl.when(i == 0)
        def _():
            sums_ref[...] = jnp.zeros_like(sums_ref)

        xv, dhv, dxov = x_ref[...], dh_ref[...], dxo_ref[...]
        inv = lax.rsqrt(jnp.mean(xv * xv, axis=-1, keepdims=True) + NORM_EPS)
        xn = xv * inv
        gv = g_ref[...]
        dn = dhv * s_ref[...]
        dxn = dn * gv
        dx = inv * (dxn - xn * jnp.mean(dxn * xn, axis=-1, keepdims=True)) + dxov
        dx_ref[...] = dx
        dyp_ref[...] = (gw_ref[...] * dx).astype(dyp_ref.dtype)
        sums_ref[0:1, :] += jnp.sum(dhv, axis=0, keepdims=True)
        sums_ref[1:2, :] += jnp.sum(dhv * (xn * gv), axis=0, keepdims=True)
        sums_ref[2:3, :] += jnp.sum(w_sub * y_ref[...] * dxov, axis=0, keepdims=True)
        sums_ref[3:4, :] += jnp.sum(dn * xn, axis=0, keepdims=True)

    blk = pl.BlockSpec((tr, d), lambda i: (i, 0))
    vec = pl.BlockSpec((1, d), lambda i: (0, 0))
    return _host_call(
        body, name, t // tr, [dh, x, y, dxo, g, scale1p, gw_prev], [blk, blk, blk, blk, vec, vec, vec],
        [SDS((t, d), F32), SDS((t, d), BF16), SDS((8, d), F32)], [blk, blk, pl.BlockSpec((8, d), lambda i: (0, 0))], [], plan)


def _loss_head(x, target, gf, gw_prev):
    t, d = x.shape
    tr = _row_tile(t)
    nt = t // tr

    def body(x_ref, tg_ref, g_ref, gw_ref, dx_ref, dyp_ref, sums_ref):
        i = pl.program_id(0)

        @pl.when(i == 0)
        def _():
            sums_ref[...] = jnp.zeros_like(sums_ref)

        xv = x_ref[...]
        inv = lax.rsqrt(jnp.mean(xv * xv, axis=-1, keepdims=True) + NORM_EPS)
        xn = xv * inv
        gv = g_ref[...]
        err = xn * gv - tg_ref[...]
        dyv = err * (1.0 / d)
        dxn = dyv * gv
        dx = inv * (dxn - xn * jnp.mean(dxn * xn, axis=-1, keepdims=True))
        dx_ref[...] = dx
        dyp_ref[...] = (gw_ref[...] * dx).astype(dyp_ref.dtype)
        sums_ref[0:1, :] += jnp.sum(dyv * xn, axis=0, keepdims=True)
        sums_ref[1:2, :] += jnp.sum(err * err, axis=0, keepdims=True)

        @pl.when(i == nt - 1)
        def _():
            tot = jnp.sum(sums_ref[1:2, :], axis=1, keepdims=True) * (0.5 / d)
            sums_ref[1:2, :] = jnp.broadcast_to(tot, (1, d))

    blk = pl.BlockSpec((tr, d), lambda i: (i, 0))
    vec = pl.BlockSpec((1, d), lambda i: (0, 0))
    return _pcall(
        body, name="loss_head", grid=(nt,), in_specs=[blk, blk, vec, vec],
        out_specs=[blk, blk, pl.BlockSpec((8, d), lambda i: (0, 0))],
        out_shape=[SDS((t, d), F32), SDS((t, d), BF16), SDS((8, d), F32)],
        compiler_params=_params(("arbitrary",)))(x, target, gf, gw_prev)


HIDDEN_CHUNKS = N_DEV // 2


def _ffn_tiles(t, d):
    return _tile(t, (1024, 512, 256, 128)), _tile(d, (1024, 512, 256, 128))


def _ffn_gu(name, h, wgu, plan=None):
    t, d = h.shape
    fc, nc = wgu.shape[3], HIDDEN_CHUNKS
    tm, _ = _ffn_tiles(t, d)

    def epi_gu(accs, _):
        gpre, up = accs
        return (gpre, up), gpre * _sigmoid(gpre) * up

    wblk = (None, None, d, fc)
    ins = [(h, pl.BlockSpec((tm, d), lambda i, c, k: (i, 0))),
           (wgu, pl.BlockSpec(wblk, lambda i, c, k: (0, c, 0, 0))),
           (wgu, pl.BlockSpec(wblk, lambda i, c, k: (0, c + nc, 0, 0)))]
    return _mm(name, ins, [(0, 1, 0), (0, 2, 1)], 2, (tm, fc), [], epi_gu,
               [SDS((2, nc, t, fc), BF16), SDS((nc, t, fc), BF16)],
               [pl.BlockSpec((2, None, tm, fc), lambda i, c, k: (0, c, i, 0)),
                pl.BlockSpec((None, tm, fc), lambda i, c, k: (c, i, 0))],
               (t // tm, nc, 1), "nn", plan=plan)


def _ffn_down(name, a, wd4, x, gmul, plan=None):
    nc, t, fc = a.shape
    d = wd4.shape[3]
    tm = _tile(t, (512, 256, 128))

    def epi_down(accs, ex):
        (yv,), (xv, gm) = accs, ex
        return yv, xv + MACARON_W * gm * yv

    ins = [(a, pl.BlockSpec((nc, tm, fc), lambda i, j, k: (0, i, 0))),
           (wd4, pl.BlockSpec((None, nc, fc, d), lambda i, j, k: (0, 0, 0, 0), pipeline_mode=pl.Buffered(1))),
           (x, pl.BlockSpec((tm, d), lambda i, j, k: (i, 0))), (gmul, pl.BlockSpec((1, d), lambda i, j, k: (0, 0)))]
    oblk = pl.BlockSpec((tm, d), lambda i, j, k: (i, 0))
    prods = [((0, (c,)), (1, (c,)), 0) for c in range(nc)]
    return _mm(name, ins, prods, 1, (tm, d), [2, 3], epi_down, [SDS((t, d), BF16), SDS((t, d), F32)],
               [oblk, oblk], (t // tm, 1, 1), "nn", plan=plan)


def _ffn_da(name, dy, wd4, gu2, plan=None):
    t, d = dy.shape
    _, nc, fc, _ = wd4.shape
    tm, _ = _ffn_tiles(t, d)

    def epi_da(accs, ex):
        (da,), (gu,) = accs, ex
        gpre, up = gu[0].astype(F32), gu[1].astype(F32)
        s = _sigmoid(gpre)
        silu = gpre * s
        dg = da * up * (s * (1.0 + gpre * (1.0 - s)))
        return ((dg, da * silu),)

    gblk = pl.BlockSpec((2, None, tm, fc), lambda i, c, k: (0, c, i, 0))
    ins = [(dy, pl.BlockSpec((tm, d), lambda i, c, k: (i, 0))),
           (wd4, pl.BlockSpec((None, None, fc, d), lambda i, c, k: (0, c, 0, 0))), (gu2, gblk)]
    return _mm(name, ins, [(0, 1, 0)], 1, (tm, fc), [2], epi_da, [SDS((2, nc, t, fc), BF16)], [gblk],
               (t // tm, nc, 1), "nt", plan=plan)


def _ffn_dwd(name, a, dy, plan=None):
    nc, t, fc = a.shape
    d = dy.shape[1]
    _, tn = _ffn_tiles(t, d)
    ins = [(a, pl.BlockSpec((None, t, fc), lambda c, j, k: (c, 0, 0))), (dy, pl.BlockSpec((t, tn), lambda c, j, k: (0, j)))]
    return _mm(name, ins, [(0, 1, 0)], 1, (fc, tn), [], _plain, [SDS((1, nc, fc, d), BF16)],
               [pl.BlockSpec((None, None, fc, tn), lambda c, j, k: (0, c, 0, j))], (nc, d // tn, 1), "tn", plan=plan)


def _ffn_dwgu(name, h, dgu2, plan=None):
    t, d = h.shape
    _, nc, _, fc = dgu2.shape
    _, tn = _ffn_tiles(t, d)
    ins = [(h, pl.BlockSpec((t, tn), lambda i, c, k: (0, i))),
           (dgu2, pl.BlockSpec((None, None, t, fc), lambda i, c, k: (c // nc, c % nc, 0, 0)))]
    return _mm(name, ins, [(0, 1, 0)], 1, (tn, fc), [], _plain, [SDS((1, 2 * nc, d, fc), BF16)],
               [pl.BlockSpec((None, None, tn, fc), lambda i, c, k: (0, c, i, 0))], (d // tn, 2 * nc, 1), "tn", plan=plan)


def _ffn_dh(name, dgu2, wgu, plan=None):
    _, nc, t, fc = dgu2.shape
    d = wgu.shape[2]
    tm = _tile(t, (512, 256, 128))
    ins = [(dgu2, pl.BlockSpec((2, nc, tm, fc), lambda i, j, k: (0, 0, i, 0))),
           (wgu, pl.BlockSpec((None, 2 * nc, d, fc), lambda i, j, k: (0, 0, 0, 0), pipeline_mode=pl.Buffered(1)))]
    prods = [((0, (s, c)), (1, (nc * s + c,)), 0) for s in range(2) for c in range(nc)]
    return _mm(name, ins, prods, 1, (tm, d), [], _plain, [SDS((t, d), F32)],
               [pl.BlockSpec((tm, d), lambda i, j, k: (i, 0))], (t // tm, 1, 1), "nt", plan=plan)


def _sb_block(t):
    return 256 if t >= 1024 else 128


SB_STRIP = 128


def _sb_strips(blk):
    strip = min(SB_STRIP, blk)
    row = lax.broadcasted_iota(jnp.int32, (strip, blk), 0)
    col = lax.broadcasted_iota(jnp.int32, (strip, blk), 1)
    return [(slice(r0, r0 + strip), col < row + r0) for r0 in range(0, blk, strip)]


def _host_call(core, name, steps, ins, in_specs, out_shapes, out_specs, scratch, plan):
    n_in, n_out, n_scr = len(ins), len(out_shapes), len(scratch)
    c_ins, c_outs = (plan.ins, plan.outs) if plan else ([], [])
    n_cin, n_cout = len(c_ins), len(c_outs)

    def body(*refs):
        in_refs, c_in = refs[:n_in], refs[n_in:n_in + n_cin]
        rest = refs[n_in + n_cin:]
        out_refs, c_out = rest[:n_out], rest[n_out:n_out + n_cout]
        rest = rest[n_out + n_cout:]
        scr, sems = rest[:n_scr], rest[n_scr:]
        step = pl.program_id(0)
        if plan:
            @pl.when(step == 0)
            def _():
                plan.phases[0](c_in, c_out, *sems)

        core(in_refs, out_refs, scr)
        if plan:
            @pl.when(step == steps - 1)
            def _():
                plan.phases[1](c_in, c_out, *sems)
                plan.phases[2](c_in, c_out, *sems)

    return _pcall(
        body, name=name, grid=(steps,), in_specs=list(in_specs) + [ANY] * n_cin,
        out_specs=list(out_specs) + [ANY] * n_cout, out_shape=list(out_shapes) + list(c_outs),
        scratch_shapes=list(scratch) + (plan.scratch() if plan else []),
        compiler_params=_params(("arbitrary",)))(*ins, *c_ins)


def _sb_fwd(qkv, d, plan=None):
    t = qkv.shape[0]
    blk = _sb_block(t)
    nq = t // blk
    npair = d // LANES
    scale = HEAD_DIM ** -0.5

    def body(in_refs, out_refs, scr):
        (q_ref, k_ref, v_ref), (o_ref, l_ref) = in_refs, out_refs
        tri_s = scr[0]
        hi_s, lo_s, w_s, zs_s = (scr[1 + 4 * k:5 + 4 * k] for k in range(4))
        lane = lax.broadcasted_iota(jnp.int32, (blk, LANES), 1)
        head0 = lane < HEAD_DIM
        row = lax.broadcasted_iota(jnp.int32, (blk, blk), 0)
        col = lax.broadcasted_iota(jnp.int32, (blk, blk), 1)
        tri_s[...] = (row > col).astype(BF16)
        strips = _sb_strips(blk)

        def step(qhs, kbs, maskeds, carries):
            chains = [(bi, hh) for bi in range(len(kbs)) for hh in range(2)]
            starts = [pl.multiple_of(kb * blk, blk) for kb in kbs]
            kvs = [k_ref[pl.ds(start, blk), :] for start in starts]
            vvs = [v_ref[pl.ds(start, blk), :] for start in starts]
            zs = [_dot(qhs[hh], kvs[bi], "nt") for bi, hh in chains]
            sums = []
            for c, (bi, hh) in enumerate(chains):
                parts = []
                for rows, causal in strips:
                    zt = zs[c][rows, :]
                    sp = _softplus(zt)
                    lk = jnp.where(causal, -sp, 0.0) if maskeds[bi] else -sp
                    hi = lk.astype(BF16)
                    hi_s[c][rows, :] = hi
                    lo_s[c][rows, :] = (lk - hi.astype(F32)).astype(BF16)
                    zs_s[c][rows, :] = zt - sp
                    parts.append(jnp.sum(lk, axis=1, keepdims=True))
                sums.append(jnp.concatenate(parts, axis=0))
            laters = [_dot(hi_s[c][...], tri_s[...], "nn") + _dot(lo_s[c][...], tri_s[...], "nn") for c in range(len(chains))]
            for c, (bi, hh) in enumerate(chains):
                cl = carries[hh][0]
                if bi == 1:
                    cl = cl + sums[hh]
                for rows, causal in strips:
                    logw = zs_s[c][rows, :] + laters[c][rows, :] + cl[rows, :]
                    if maskeds[bi]:
                        logw = jnp.where(causal, logw, -1e30)
                    w_s[c][rows, :] = jnp.exp(logw).astype(BF16)
            new = [list(carries[hh]) for hh in range(2)]
            for c, (bi, hh) in enumerate(chains):
                new[hh] = [new[hh][0] + sums[c], new[hh][1] + _dot(w_s[c][...], vvs[bi], "nn")]
            return tuple(tuple(cr) for cr in new)

        def qblock(qi, _):
            qstart = pl.multiple_of(qi * blk, blk)
            qv = q_ref[pl.ds(qstart, blk), :] * scale
            qhs = [jnp.where(head0 if hh == 0 else ~head0, qv, jnp.zeros_like(qv)) for hh in range(2)]
            zero = (jnp.zeros((blk, 1), F32), jnp.zeros((blk, LANES), F32))
            outs = lax.cond(qi % 2 == 1,
                            lambda crs: step(qhs, [qi, qi - 1], [True, False], crs),
                            lambda crs: step(qhs, [qi], [True], crs), (zero, zero))
            top = qi - 1 - qi % 2
            outs = lax.fori_loop(0, qi // 2, lambda j, crs: step(qhs, [top - 2 * j, top - 2 * j - 1], [False, False], crs),
                                 outs)
            o_ref[pl.ds(qstart, blk), :] = jnp.where(head0, outs[0][1], outs[1][1]).astype(o_ref.dtype)
            l_ref[pl.ds(qstart, blk), :] = jnp.where(head0, outs[0][0], outs[1][0])
            return 0

        lax.fori_loop(0, nq, qblock, 0)

    tile_bf16, tile_f32 = pltpu.VMEM((blk, blk), BF16), pltpu.VMEM((blk, blk), F32)
    return _host_call(
        body, "sb_fwd", npair, [qkv, qkv, qkv],
        [pl.BlockSpec((t, LANES), lambda p: (0, p)), pl.BlockSpec((t, LANES), lambda p: (0, npair + p)),
         pl.BlockSpec((t, LANES), lambda p: (0, 2 * npair + p))],
        [SDS((t, d), BF16), SDS((t, d), F32)],
        [pl.BlockSpec((t, LANES), lambda p: (0, p)), pl.BlockSpec((t, LANES), lambda p: (0, p))],
        [tile_bf16] * 13 + [tile_f32] * 4, plan)


def _sb_bwd(qkv, do, ltot, d, plan=None):
    t = qkv.shape[0]
    blk = _sb_block(t)
    nq = t // blk
    npair = d // LANES
    scale = HEAD_DIM ** -0.5

    def body(in_refs, out_refs, scr):
        (q_ref, k_ref, v_ref, do_ref, l_ref), (out_ref,) = in_refs, out_refs
        dq_s, dk_s, dv_s, upto_s, before_s = scr[:5]
        hi_s, lo_s, w_s, dab_s, dzs_s, zs_s, da_s = (scr[5 + 4 * k:9 + 4 * k] for k in range(7))
        lane = lax.broadcasted_iota(jnp.int32, (blk, LANES), 1)
        head0 = lane < HEAD_DIM
        row = lax.broadcasted_iota(jnp.int32, (blk, blk), 0)
        col = lax.broadcasted_iota(jnp.int32, (blk, blk), 1)
        upto_s[...] = (row <= col).astype(BF16)
        before_s[...] = (row < col).astype(BF16)
        dk_s[...] = jnp.zeros_like(dk_s)
        dv_s[...] = jnp.zeros_like(dv_s)
        strips = _sb_strips(blk)

        def step(heads, kbs, maskeds, carries):
            chains = [(bi, hh) for bi in range(len(kbs)) for hh in range(2)]
            starts = [pl.multiple_of(kb * blk, blk) for kb in kbs]
            kvs = [k_ref[pl.ds(start, blk), :] for start in starts]
            vvs = [v_ref[pl.ds(start, blk), :] for start in starts]
            zs = [_dot(heads[hh][0], kvs[bi], "nt") for bi, hh in chains]
            dws = [_dot(heads[hh][1], vvs[bi], "nt") for bi, hh in chains]
            lk_sums, da_sums = [], []
            for c, (bi, hh) in enumerate(chains):
                parts = []
                for rows, causal in strips:
                    zt = zs[c][rows, :]
                    sp = _softplus(zt)
                    lk = jnp.where(causal, -sp, 0.0) if maskeds[bi] else -sp
                    hi = lk.astype(BF16)
                    hi_s[c][rows, :] = hi
                    lo_s[c][rows, :] = (lk - hi.astype(F32)).astype(BF16)
                    zs_s[c][rows, :] = zt - sp
                    parts.append(jnp.sum(lk, axis=1, keepdims=True))
                lk_sums.append(jnp.concatenate(parts, axis=0))
            cums = [_dot(hi_s[c][...], upto_s[...], "nn") + _dot(lo_s[c][...], upto_s[...], "nn") for c in range(len(chains))]
            for c, (bi, hh) in enumerate(chains):
                lt, plk = heads[hh][2], carries[hh][0]
                if bi == 1:
                    plk = plk + lk_sums[hh]
                parts = []
                for rows, causal in strips:
                    logw = zs_s[c][rows, :] + (lt[rows, :] - (plk[rows, :] + cums[c][rows, :]))
                    if maskeds[bi]:
                        logw = jnp.where(causal, logw, -1e30)
                    w = jnp.exp(logw)
                    w_s[c][rows, :] = w.astype(BF16)
                    da = dws[c][rows, :] * w
                    da_s[c][rows, :] = da
                    dab_s[c][rows, :] = da.astype(BF16)
                    parts.append(jnp.sum(da, axis=1, keepdims=True))
                da_sums.append(jnp.concatenate(parts, axis=0))
            pres = [_dot(dab_s[c][...], before_s[...], "nn") for c in range(len(chains))]
            for c, (bi, hh) in enumerate(chains):
                pda = carries[hh][1]
                if bi == 1:
                    pda = pda + da_sums[hh]
                for rows, causal in strips:
                    sig = jnp.exp(zs_s[c][rows, :])
                    da = da_s[c][rows, :]
                    dz = da * (1.0 - sig) - sig * (pda[rows, :] + pres[c][rows, :])
                    if maskeds[bi]:
                        dz = jnp.where(causal, dz, 0.0)
                    dzs_s[c][rows, :] = dz.astype(BF16)
            new = [list(carries[hh]) for hh in range(2)]
            for c, (bi, hh) in enumerate(chains):
                dk_s[kbs[bi]] += _dot(heads[hh][3], dzs_s[c][...], "nn")
                dv_s[kbs[bi]] += _dot(heads[hh][4], w_s[c][...], "nn")
                new[hh] = [new[hh][0] + lk_sums[c], new[hh][1] + da_sums[c], new[hh][2] + _dot(dzs_s[c][...], kvs[bi], "nn")]
            return tuple(tuple(cr) for cr in new)

        def qblock(qi, _):
            qstart = pl.multiple_of(qi * blk, blk)
            qv = q_ref[pl.ds(qstart, blk), :] * scale
            dov = do_ref[pl.ds(qstart, blk), :]
            lv = l_ref[pl.ds(qstart, blk), :]
            heads = []
            for hh in range(2):
                sel = head0 if hh == 0 else ~head0
                qh, doh = jnp.where(sel, qv, jnp.zeros_like(qv)), jnp.where(sel, dov, jnp.zeros_like(dov))
                heads.append((qh, doh, jnp.max(jnp.where(sel, lv, -jnp.inf), axis=1, keepdims=True),
                              qh.astype(F32).T.astype(BF16), doh.astype(F32).T.astype(BF16)))
            zero = (jnp.zeros((blk, 1), F32), jnp.zeros((blk, 1), F32), jnp.zeros((blk, LANES), F32))
            carries = lax.fori_loop(0, qi // 2, lambda j, crs: step(heads, [2 * j, 2 * j + 1], [False, False], crs),
                                    (zero, zero))
            carries = lax.cond(qi % 2 == 1,
                               lambda crs: step(heads, [qi - 1, qi], [False, True], crs),
                               lambda crs: step(heads, [qi], [True], crs), carries)
            dq_s[pl.ds(qstart, blk), :] = jnp.where(head0, carries[0][2], carries[1][2]) * scale
            return 0

        lax.fori_loop(0, nq, qblock, 0)
        out_ref[0] = dq_s[...].astype(out_ref.dtype)
        for b in range(nq):
            out_ref[1, b * blk:(b + 1) * blk, :] = dk_s[b].T.astype(out_ref.dtype)
            out_ref[2, b * blk:(b + 1) * blk, :] = dv_s[b].T.astype(out_ref.dtype)

    col_blk = lambda off: pl.BlockSpec((t, LANES), lambda p: (0, off + p))
    return _host_call(
        body, "sb_bwd", npair, [qkv, qkv, qkv, do, ltot],
        [col_blk(0), col_blk(npair), col_blk(2 * npair), col_blk(0), col_blk(0)],
        [SDS((3, t, d), BF16)], [pl.BlockSpec((3, t, LANES), lambda p: (0, 0, p))],
        [pltpu.VMEM((t, LANES), F32)] + [pltpu.VMEM((nq, LANES, blk), F32) for _ in range(2)]
        + [pltpu.VMEM((blk, blk), BF16) for _ in range(2 + 20)]
        + [pltpu.VMEM((blk, blk), F32) for _ in range(8)], plan)


def _roll_rows(v, shift):
    return pltpu.roll(v, shift, 0)


def _shift_down(v, dist, fill, row):
    return jnp.where(row >= dist, _roll_rows(v, dist), fill)


def _shift_up(v, dist, fill, row):
    t = v.shape[0]
    return jnp.where(row < t - dist, _roll_rows(v, t - dist), fill)


def _lru_gates(xb, small, wr, wi, row):
    xs = [_shift_down(xb, 3 - tap, 0.0, row) if tap < 3 else xb for tap in range(4)]
    xc = small[4:5, :] + xs[0] * small[0:1, :]
    for tap in range(1, 4):
        xc = xc + xs[tap] * small[tap:tap + 1, :]
    xcb = xc.astype(BF16)
    r = _sigmoid(_dot(xcb, wr, "nn") + small[5:6, :])
    ig = _sigmoid(_dot(xcb, wi, "nn") + small[6:7, :])
    sp = _softplus(-small[7:8, :])
    la = -LRU_C * r * sp
    a = jnp.exp(la)
    th = jnp.tanh(la)
    m2 = -2.0 * th / (1.0 - th)
    return xs, xc, xcb, r, ig, sp, a, (jnp.sqrt(m2), m2)


def _gelu_parts(gate):
    inner = GELU_C * (gate + GELU_K * gate * gate * gate)
    th = jnp.tanh(inner)
    gelu = 0.5 * gate * (1.0 + th)
    dgelu = 0.5 * (1.0 + th) + 0.5 * gate * (1.0 - th * th) * GELU_C * (1.0 + 3.0 * GELU_K * gate * gate)
    return gelu, dgelu


def _scan_steps(t):
    steps, dist = [], 1
    while dist < t:
        steps.append(dist)
        dist *= 2
    return steps


SUBLANES = 8


def _linear_scan(a, b, scratch, row, reverse):
    a_s, b_s, carry_s = scratch
    t = a.shape[0]
    groups = t // SUBLANES
    in_group = row & (SUBLANES - 1)
    for dist in _scan_steps(SUBLANES):
        if reverse:
            inside = in_group < SUBLANES - dist
            b = b + a * jnp.where(inside, _roll_rows(b, t - dist), 0.0)
            a = a * jnp.where(inside, _roll_rows(a, t - dist), 1.0)
        else:
            inside = in_group >= dist
            b = a * jnp.where(inside, _roll_rows(b, dist), 0.0) + b
            a = a * jnp.where(inside, _roll_rows(a, dist), 1.0)
    a_s[...] = a
    b_s[...] = b
    end = 0 if reverse else SUBLANES - 1
    ends = pl.ds(end, groups, stride=SUBLANES)
    ae, be = a_s[ends, :], b_s[ends, :]
    grow = lax.broadcasted_iota(jnp.int32, ae.shape, 0)
    shift = _shift_up if reverse else _shift_down
    for dist in _scan_steps(groups):
        be = ae * shift(be, dist, 0.0, grow) + be
        ae = ae * shift(ae, dist, 1.0, grow)
    incoming = shift(be, 1, 0.0, grow)
    for k in range(SUBLANES):
        carry_s[pl.ds(k, groups, stride=SUBLANES), :] = incoming
    return a_s[...] * carry_s[...] + b_s[...]


def _lru_fwd(gx, small, wr, wi):
    t = gx.shape[0]
    r_dim = gx.shape[1] // 2
    nb = r_dim // LRU_BLOCK_W

    def body(gate_ref, xb_ref, small_ref, wr_ref, wi_ref, y_ref, hs_ref, *scratch):
        row = lax.broadcasted_iota(jnp.int32, (t, LRU_BLOCK_W), 0)
        xb = xb_ref[...]
        _, xc, _, _, ig, _, a, (mult, _) = _lru_gates(xb, small_ref, wr_ref[...], wi_ref[...], row)
        hsv = _linear_scan(a, mult * (ig * xc), scratch, row, reverse=False)
        hs_ref[...] = hsv
        gelu, _ = _gelu_parts(gate_ref[...])
        y_ref[...] = (gelu * hsv).astype(y_ref.dtype)

    colb = lambda off: pl.BlockSpec((t, LRU_BLOCK_W), lambda n: (0, off + n))
    wspec = pl.BlockSpec((None, LRU_BLOCK_W, LRU_BLOCK_W), lambda n: (n, 0, 0))
    return _pcall(
        body, name="lru_fwd", grid=(nb,),
        in_specs=[colb(0), colb(nb), pl.BlockSpec((8, LRU_BLOCK_W), lambda n: (0, n)), wspec, wspec],
        out_specs=[colb(0), colb(0)], out_shape=[SDS((t, r_dim), BF16), SDS((t, r_dim), F32)],
        scratch_shapes=[pltpu.VMEM((t, LRU_BLOCK_W), F32) for _ in range(3)],
        compiler_params=_params(("parallel",)))(gx, gx, small, wr, wi)


def _lru_bwd(gx, hs, dy, small, wr, wi, plan=None):
    t = gx.shape[0]
    r_dim = gx.shape[1] // 2
    nb = r_dim // LRU_BLOCK_W

    def body(in_refs, out_refs, scratch):
        (gate_ref, xb_ref, hs_ref, dy_ref, small_ref, wr_ref, wi_ref), (dgx_ref, dsm_ref, dwr_ref, dwi_ref) = in_refs, out_refs
        row = lax.broadcasted_iota(jnp.int32, (t, LRU_BLOCK_W), 0)
        xb, hsv, dyv, smallv = xb_ref[...], hs_ref[...], dy_ref[...], small_ref
        wrv, wiv = wr_ref[...], wi_ref[...]
        xs, xc, xcb, r, ig, sp, a, (mult, m2) = _lru_gates(xb, smallv, wrv, wiv, row)
        gelu, dgelu = _gelu_parts(gate_ref[...])
        dgx_ref[0] = (dyv * hsv * dgelu).astype(dgx_ref.dtype)
        dacc = _linear_scan(_shift_up(a, 1, 1.0, row), dyv * gelu, scratch, row, reverse=True)
        da = dacc * _shift_down(hsv, 1, 0.0, row)
        dmult = dacc * (ig * xc)
        dixc = dacc * mult
        dla = da * a - dmult * (a * a) * lax.rsqrt(m2)
        dr = dla * (-LRU_C * sp)
        dsp = jnp.sum(dla * (-LRU_C * r), axis=0, keepdims=True)
        dpr = dr * r * (1.0 - r)
        dpi = dixc * xc * ig * (1.0 - ig)
        dprb, dpib = dpr.astype(BF16), dpi.astype(BF16)
        dwr_ref[...] = _dot(xcb, dprb, "tn")
        dwi_ref[...] = _dot(xcb, dpib, "tn")
        dxc = dixc * ig + _dot(dprb, wrv, "nt") + _dot(dpib, wiv, "nt")
        dxb = dxc * smallv[3:4, :]
        for tap in range(3):
            dxb = dxb + _shift_up(dxc, 3 - tap, 0.0, row) * smallv[tap:tap + 1, :]
        dgx_ref[1] = dxb.astype(dgx_ref.dtype)
        lam = smallv[7:8, :]
        rows = [jnp.sum(dxc * xs[tap], axis=0, keepdims=True) for tap in range(4)]
        rows.append(jnp.sum(dxc, axis=0, keepdims=True))
        rows.append(jnp.sum(dpr, axis=0, keepdims=True))
        rows.append(jnp.sum(dpi, axis=0, keepdims=True))
        rows.append(-dsp * _sigmoid(-lam))
        for k, rv in enumerate(rows):
            dsm_ref[k:k + 1, :] = rv

    colb = lambda off: pl.BlockSpec((t, LRU_BLOCK_W), lambda n: (0, off + n))
    wspec = pl.BlockSpec((None, LRU_BLOCK_W, LRU_BLOCK_W), lambda n: (n, 0, 0))
    sspec = pl.BlockSpec((8, LRU_BLOCK_W), lambda n: (0, n))
    return _host_call(
        body, "lru_bwd", nb, [gx, gx, hs, dy, small, wr, wi],
        [colb(0), colb(nb), colb(0), colb(0), sspec, wspec, wspec],
        [SDS((2, t, r_dim), BF16), SDS((8, r_dim), F32), SDS((nb, LRU_BLOCK_W, LRU_BLOCK_W), F32),
         SDS((nb, LRU_BLOCK_W, LRU_BLOCK_W), F32)],
        [pl.BlockSpec((2, t, LRU_BLOCK_W), lambda n: (0, 0, n)), sspec, wspec, wspec],
        [pltpu.VMEM((t, LRU_BLOCK_W), F32) for _ in range(3)], plan)


def _adam(w, g, m, v):
    m2 = ADAM_B1 * m + (1.0 - ADAM_B1) * g
    v2 = ADAM_B2 * v + (1.0 - ADAM_B2) * (g * g)
    m_hat = m2 / (1.0 - ADAM_B1 ** ADAM_STEP)
    v_hat = v2 / (1.0 - ADAM_B2 ** ADAM_STEP)
    return -ADAM_LR * (m_hat / (jnp.sqrt(v_hat) + ADAM_EPS) + ADAM_WD * w), m2, v2


def _mod_fwd(c_all, mod_w, mod_b_cols):
    nl, d, cols = mod_w.shape
    nbatch = c_all.shape[0]

    def body(c_ref, w_ref, b_ref, o_ref):
        cv = c_ref[...]
        ca = (cv * _sigmoid(cv)).astype(BF16)
        o_ref[...] = _dot(ca, w_ref[...].astype(BF16), "nn") + b_ref[...]

    return _pcall(
        body, name="mod_fwd", grid=(nl,),
        in_specs=[pl.BlockSpec((nbatch, d), lambda l: (0, 0)), pl.BlockSpec((None, d, cols), lambda l: (l, 0, 0)),
                  pl.BlockSpec((None, 1, cols), lambda l: (l, 0, 0))],
        out_specs=pl.BlockSpec((None, nbatch, cols), lambda l: (l, 0, 0)), out_shape=SDS((nl, nbatch, cols), F32),
        compiler_params=_params(("parallel",)))(c_all, mod_w, mod_b_cols)


def _mod_w_update(c_all, dmod_cols, w, m, v):
    nl, d, cols = w.shape
    nbatch = c_all.shape[0]
    tr = _tile(d, (256, 128))

    def body(c_ref, dm_ref, w_ref, m_ref, v_ref, g_ref, dl_ref, m2_ref, v2_ref):
        cv = c_ref[...]
        ca = (cv * _sigmoid(cv)).astype(BF16)
        g = _dot(ca, dm_ref[...].astype(BF16), "tn")
        g_ref[...] = g
        dl_ref[...], m2_ref[...], v2_ref[...] = _adam(w_ref[...], g, m_ref[...], v_ref[...])

    wblk = pl.BlockSpec((None, tr, cols), lambda l, i: (l, i, 0))
    return _pcall(
        body, name="mod_w_update", grid=(nl, d // tr),
        in_specs=[pl.BlockSpec((nbatch, tr), lambda l, i: (0, i)), pl.BlockSpec((None, nbatch, cols), lambda l, i: (l, 0, 0)),
                  wblk, wblk, wblk],
        out_specs=[wblk] * 4, out_shape=[SDS(w.shape, F32)] * 4,
        compiler_params=_params(("parallel", "parallel")))(c_all, dmod_cols, w, m, v)


def _adam_update(name, w, m, v, gparts):
    rows, cols = w.shape
    tr = _tile(rows, (256, 128, 64, 32, 16, 8))
    npart = len(gparts)

    def body(*refs):
        w_ref, m_ref, v_ref = refs[:3]
        g_refs = refs[3:3 + npart]
        g_ref, dl_ref, m2_ref, v2_ref = refs[3 + npart:]
        g = g_refs[0][...].astype(F32)
        for gr in g_refs[1:]:
            g = g + gr[...].astype(F32)
        g_ref[...] = g
        dl_ref[...], m2_ref[...], v2_ref[...] = _adam(w_ref[...], g, m_ref[...], v_ref[...])

    blk = pl.BlockSpec((tr, cols), lambda i: (i, 0))
    return _pcall(body, name=name, grid=(rows // tr,), in_specs=[blk] * (3 + npart), out_specs=[blk] * 4,
                  out_shape=[SDS((rows, cols), F32)] * 4, compiler_params=_params(("parallel",)))(w, m, v, *gparts)


def _adam_shard(name, w, m, v, part4, recv3, chip_idx, first=0, fills=None):
    p, r, cdim = w.shape
    pg = part4.shape[0]
    tr = _tile(r, (256, 176, 160, 128, 64, 32, 16))

    def body(chip_ref, w_ref, m_ref, v_ref, own_ref, r0_ref, r1_ref, r2_ref, *rest):
        g_ref, dl_ref, m2_ref, v2_ref = rest[-4:]
        g = own_ref[...].astype(F32) + r0_ref[...].astype(F32) + r1_ref[...].astype(F32) + r2_ref[...].astype(F32)
        g_ref[...] = g
        dl_ref[...], m2_ref[...], v2_ref[...] = _adam(w_ref[...], g, m_ref[...], v_ref[...])

    blk = pl.BlockSpec((None, tr, cdim), lambda q, i, chip_ref: (first + q, i, 0))
    blk4 = (None, None, tr, cdim)
    slot = lambda s: pl.BlockSpec(blk4, lambda q, i, chip_ref: (s, q, i, 0))
    fills = list(fills or [])
    grid_spec = pltpu.PrefetchScalarGridSpec(
        num_scalar_prefetch=1, grid=(pg, r // tr),
        in_specs=[blk, blk, blk, pl.BlockSpec(blk4, lambda q, i, chip_ref: (q, chip_ref[0], i, 0)), slot(0), slot(1), slot(2)]
        + [ANY] * len(fills),
        out_specs=[blk] * 4)
    return _pcall(body, name=name, grid_spec=grid_spec, out_shape=[SDS((p, r, cdim), F32)] * 4,
                  input_output_aliases={8 + k: k for k in range(len(fills))},
                  compiler_params=_params(("parallel", "parallel")))(chip_idx, w, m, v, part4, recv3, recv3, recv3, *fills)


def _sum_devices(gathered, name):
    _, rows, cols = gathered.shape
    tr = _tile(rows, (512, 256, 128, 64, 32, 16, 8))

    def body(g_ref, o_ref):
        acc = g_ref[0].astype(F32)
        for k in range(1, N_DEV):
            acc = acc + g_ref[k].astype(F32)
        o_ref[...] = acc

    return _pcall(body, name=name, grid=(rows // tr,), in_specs=[pl.BlockSpec((N_DEV, tr, cols), lambda i: (0, i, 0))],
                  out_specs=pl.BlockSpec((tr, cols), lambda i: (i, 0)), out_shape=SDS((rows, cols), F32),
                  compiler_params=_params(("parallel",)))(gathered)


def _pack_flat(parts, width, row_mult, dtype):
    flat = jnp.concatenate([p.reshape(-1).astype(dtype) for p in parts])
    unit = width * row_mult
    pad = (-flat.shape[0]) % unit
    if pad:
        flat = jnp.concatenate([flat, jnp.zeros((pad,), dtype)])
    return flat.reshape(-1, width)


def _unpack_flat(flat, shapes):
    out, off = [], 0
    for shp in shapes:
        size = math.prod(shp)
        out.append(flat[off:off + size].reshape(shp))
        off += size
    return out


def kernel(x, c, mod_w, mod_b, norm_g, ffn_w_gu, ffn_w_down, sb_w_qkv, sb_w_o, lru_w_in, lru_conv_w, lru_conv_b, lru_w_r, lru_b_r, lru_w_i, lru_b_i, lru_lambda, lru_w_out, final_norm_g, loss_target, m_mod_w, m_mod_b, m_norm_g, m_ffn_w_gu, m_ffn_w_down, m_sb_w_qkv, m_sb_w_o, m_lru_w_in, m_lru_conv_w, m_lru_conv_b, m_lru_w_r, m_lru_b_r, m_lru_w_i, m_lru_b_i, m_lru_lambda, m_lru_w_out, m_final_norm_g, v_mod_w, v_mod_b, v_norm_g, v_ffn_w_gu, v_ffn_w_down, v_sb_w_qkv, v_sb_w_o, v_lru_w_in, v_lru_conv_w, v_lru_conv_b, v_lru_w_r, v_lru_b_r, v_lru_w_i, v_lru_b_i, v_lru_lambda, v_lru_w_out, v_final_norm_g):
    weights = dict(mod_w=mod_w, mod_b=mod_b, norm_g=norm_g, ffn_w_gu=ffn_w_gu, ffn_w_down=ffn_w_down, sb_w_qkv=sb_w_qkv,
                   sb_w_o=sb_w_o, lru_w_in=lru_w_in, lru_conv_w=lru_conv_w, lru_conv_b=lru_conv_b, lru_w_r=lru_w_r,
                   lru_b_r=lru_b_r, lru_w_i=lru_w_i, lru_b_i=lru_b_i, lru_lambda=lru_lambda, lru_w_out=lru_w_out,
                   final_norm_g=final_norm_g)
    mom_m = dict(mod_w=m_mod_w, mod_b=m_mod_b, norm_g=m_norm_g, ffn_w_gu=m_ffn_w_gu, ffn_w_down=m_ffn_w_down,
                 sb_w_qkv=m_sb_w_qkv, sb_w_o=m_sb_w_o, lru_w_in=m_lru_w_in, lru_conv_w=m_lru_conv_w,
                 lru_conv_b=m_lru_conv_b, lru_w_r=m_lru_w_r, lru_b_r=m_lru_b_r, lru_w_i=m_lru_w_i, lru_b_i=m_lru_b_i,
                 lru_lambda=m_lru_lambda, lru_w_out=m_lru_w_out, final_norm_g=m_final_norm_g)
    mom_v = dict(mod_w=v_mod_w, mod_b=v_mod_b, norm_g=v_norm_g, ffn_w_gu=v_ffn_w_gu, ffn_w_down=v_ffn_w_down,
                 sb_w_qkv=v_sb_w_qkv, sb_w_o=v_sb_w_o, lru_w_in=v_lru_w_in, lru_conv_w=v_lru_conv_w,
                 lru_conv_b=v_lru_conv_b, lru_w_r=v_lru_w_r, lru_b_r=v_lru_b_r, lru_w_i=v_lru_w_i, lru_b_i=v_lru_b_i,
                 lru_lambda=v_lru_lambda, lru_w_out=v_lru_w_out, final_norm_g=v_final_norm_g)
    names = list(weights)

    t, d = x.shape[1], x.shape[2]
    n_layers = mod_w.shape[0]
    r_dim = lru_w_out.shape[1] * N_DEV
    ng, rs = d // N_DEV, r_dim // N_DEV
    mod_cols = mod_w.shape[2]
    nblk = lru_w_r.shape[1]
    xi, yi, ci = _mesh_pos()
    me = 4 * xi + 2 * yi + ci
    chip = 2 * xi + yi
    x2, target = x.reshape(t, d), loss_target.reshape(t, d)

    lru_small_shard = jnp.concatenate([lru_conv_w[0], lru_conv_b, lru_b_r, lru_b_i, lru_lambda], axis=0)
    small1 = _pack_flat([c, norm_g, lru_small_shard], LANES, 8, F32)
    n_small1 = small1.shape[0]
    all1 = _allgather(small1[None], "gather_small").reshape(N_DEV, n_small1 * LANES)
    c_all = all1[:, :d]
    norm_full = jnp.transpose(all1[:, d:d + 6 * ng].reshape(N_DEV, n_layers, 3, ng), (1, 2, 0, 3)).reshape(n_layers, 3, d)
    lru_small = jnp.transpose(all1[:, d + 6 * ng:d + 6 * ng + 8 * rs].reshape(N_DEV, 8, rs), (1, 0, 2)).reshape(8, r_dim)

    mod_b_cols = lax.dynamic_slice_in_dim(mod_b, me * mod_cols, mod_cols, axis=1).reshape(n_layers, 1, mod_cols)
    mod_part = _mod_fwd(c_all, mod_w, mod_b_cols)

    assert sb_w_qkv.shape[0] == 1 and lru_w_in.shape[0] == 1, "one stick-breaking and one RG-LRU layer"
    n_ffn = 2 * n_layers
    fc = ffn_w_gu.shape[3]
    cw_in = lru_w_in.shape[2]
    pieces = {("ffn_w_gu", q): ffn_w_gu[q // 2, q % 2][None] for q in range(n_ffn)}
    pieces.update({("ffn_w_down", q): ffn_w_down[q // 2, q % 2][None] for q in range(n_ffn)})
    pieces.update({("sb_w_qkv", 0): sb_w_qkv, ("sb_w_o", 0): sb_w_o, ("lru_w_in", 0): lru_w_in, ("lru_w_out", 0): lru_w_out})
    col_window = {("sb_w_qkv", 0)}
    first = [("ffn_w_gu", 0)]
    behind = {"l0s0_gu": [("ffn_w_down", 0)], "l0s0_down": [("sb_w_qkv", 0), ("sb_w_o", 0)],
              "l0s2_gu": [("ffn_w_down", n_ffn - 1)], "l0s2_down": [("ffn_w_down", 2)],
              "l1s0_gu": [("lru_w_in", 0)], "l1s0_down": [("lru_w_out", 0)]}
    behind["sb_fwd"] = [key for key in pieces if key not in first + sum(behind.values(), [])]
    gathered = {}

    def gather_plan(keys):
        return _gather_plan([pieces[key].astype(BF16) for key in keys], [key in col_window for key in keys])

    def hosting(name, call):
        keys = behind.get(name, [])
        outs = call(gather_plan(keys) if keys else None)
        gathered.update(zip(keys, outs[len(outs) - len(keys):]))
        return outs[:len(outs) - len(keys)]

    mod_all, *landed = _run_comm(_merge_plans([_gather_plan([mod_part], [False]), gather_plan(first)]), "gather_mod_and_first")
    gathered.update(zip(first, landed))
    mod_mine = lax.dynamic_index_in_dim(mod_all, me, axis=2, keepdims=False)
    mod_mine = mod_mine.reshape(n_layers, 3, 3, d)
    wr_b, wi_b = lru_w_r[0].astype(BF16), lru_w_i[0].astype(BF16)
    eye2 = jnp.eye(2 * cw_in, dtype=BF16).reshape(2, cw_in, 2 * cw_in)

    def w_gu(q):
        return gathered[("ffn_w_gu", q)]

    def w_d4(q):
        return gathered[("ffn_w_down", q)].reshape(1, HIDDEN_CHUNKS, fc, d)

    saved = []
    xcur = x2
    for layer in range(n_layers):
        for sub in range(3):
            gvec = norm_full[layer, sub].reshape(1, d)
            shift = mod_mine[layer, sub, 0].reshape(1, d)
            scale1p = 1.0 + mod_mine[layer, sub, 1].reshape(1, d)
            gmul = 1.0 + mod_mine[layer, sub, 2].reshape(1, d)
            tag = f"l{layer}s{sub}"
            h = _norm_fwd(xcur, gvec, scale1p, shift, tag + "_norm")
            rec = dict(x=xcur, h=h, g=gvec, scale1p=scale1p, gmul=gmul, w=MACARON_W if sub != 1 else 1.0)
            if sub != 1:
                lj = layer * 2 + sub // 2
                gu2, a = hosting(tag + "_gu", lambda plan: _ffn_gu(tag + "_gu", h, w_gu(lj), plan))
                yv, xcur = hosting(tag + "_down", lambda plan: _ffn_down(tag + "_down", a, w_d4(lj), xcur, gmul, plan))
                rec.update(kind="ffn", lj=lj, gu2=gu2, a=a, y=yv)
            elif layer % 2 == 0:
                w_qkv = gathered[("sb_w_qkv", 0)][0]
                w_o = gathered[("sb_w_o", 0)].reshape(d, d)
                qkv = _mm_nn(tag + "_qkv", h, w_qkv, BF16)[0]
                o, ltot = hosting("sb_fwd", lambda plan: _sb_fwd(qkv, d, plan))
                yv, xcur = _mm_nn(tag + "_wo", o, w_o, [BF16, F32], extras=[(xcur, "tile"), (gmul, "row")],
                                  epilogue=lambda accs, ex: (accs[0], ex[0] + ex[1] * accs[0]))
                rec.update(kind="sb", qkv=qkv, o=o, ltot=ltot, y=yv, w_qkv=w_qkv, w_o=w_o)
            else:
                w_in = _chunks_to_cols("lru_w_in_cols", gathered[("lru_w_in", 0)][0], eye2)
                w_out = gathered[("lru_w_out", 0)].reshape(r_dim, d)
                gx = _mm_nn(tag + "_win", h, w_in, F32)[0]
                ymix, hs = _lru_fwd(gx, lru_small, wr_b, wi_b)
                yv, xcur = _mm_nn(tag + "_wout", ymix, w_out, [BF16, F32], extras=[(xcur, "tile"), (gmul, "row")],
                                  epilogue=lambda accs, ex: (accs[0], ex[0] + ex[1] * accs[0]))
                rec.update(kind="lru", gx=gx, hs=hs, ymix=ymix, y=yv, w_in=w_in, w_out=w_out)
            saved.append(rec)

    last = saved[-1]
    dxo, dy, head_sums = _loss_head(xcur, target, final_norm_g.reshape(1, d), (last["w"] * last["gmul"]))
    loss_mine = head_sums[1, 0:1]
    dgf = head_sums[0]

    c_idx = jnp.reshape(ci, (1,)).astype(jnp.int32)
    chip_idx = jnp.reshape(chip, (1,)).astype(jnp.int32)
    grads, reduced = {}, {}
    to_pair = []
    to_chips = []

    def sibling_plan(only=None):
        keys = [key for key in to_pair if only is None or key in only]
        if not keys:
            return None, keys
        return _exchange_plan([grads[key] for key in keys], [key in col_window for key in keys], 4, _sibling_route), keys

    def sibling_done(keys, recv4):
        for key, r4 in zip(keys, recv4):
            to_pair.remove(key)
            to_chips.append((key, _pair_sum(grads[key], r4, c_idx, f"rs_pair_sum_{key[0]}{key[1]}", cols=key in col_window)))

    def chip_plan(only=None):
        items = [item for item in to_chips if only is None or item[0] in only]
        if not items:
            return None, items
        return _exchange_plan([p4 for _, p4 in items], [False] * len(items), 3, _chip_route), items

    def chips_done(items, recv3):
        for item, r3 in zip(items, recv3):
            to_chips.remove(item)
            reduced[item[0]] = (item[1], r3)

    def behind(call, make_plan, done, more=None):
        plan, items = make_plan()
        n_mine = len(plan.outs) if plan else 0
        n_more = len(more.outs) if more else 0
        outs = call(_merge_plans([plan, more]))
        n_own = len(outs) - n_mine - n_more
        done(items, outs[n_own:n_own + n_mine])
        return list(outs[:n_own]) + list(outs[n_own + n_mine:])

    carried = {
        "l1s1b_dymix": ("sibling", None), "lru_bwd": ("chips", [("ffn_w_gu", n_ffn - 1)]),
        "l1s0b_da": ("sibling", None), "l1s0b_dwgu": ("chips", [("ffn_w_down", n_ffn - 1)]),
        "l0s2b_da": ("sibling", None),
    }

    def carrying(name, call):
        if name not in carried:
            return call(None)
        stage, only = carried[name]
        if stage == "sibling":
            return behind(call, functools.partial(sibling_plan, only), sibling_done)
        return behind(call, functools.partial(chip_plan, only), chips_done)

    def at_once(make_plan, done, name):
        plan, items = make_plan()
        if plan:
            done(items, _run_comm(plan, name))

    def add_grad(key, value):
        grads[key] = value
        to_pair.append(key)

    dmod = [[None] * 3 for _ in range(n_layers)]
    dnorm = [[None] * 3 for _ in range(n_layers)]
    dlru_small = wri_all = None
    for idx in reversed(range(len(saved))):
        rec = saved[idx]
        layer, sub = divmod(idx, 3)
        tag = f"l{layer}s{sub}b"
        if rec["kind"] == "ffn" and idx > 0:
            lj = rec["lj"]
            (dgu2,) = carrying(tag + "_da", lambda plan: _ffn_da(tag + "_da", dy, w_d4(lj), rec["gu2"], plan))
            dwd = _ffn_dwd(tag + "_dwd", rec["a"], dy)[0].reshape(gathered[("ffn_w_down", lj)].shape)
            (dwgu,) = carrying(tag + "_dwgu", lambda plan: _ffn_dwgu(tag + "_dwgu", rec["h"], dgu2, plan))
            (dh,) = carrying(tag + "_dh", lambda plan: _ffn_dh(tag + "_dh", dgu2, w_gu(lj), plan))
            add_grad(("ffn_w_down", lj), dwd)
            add_grad(("ffn_w_gu", lj), dwgu)
        elif rec["kind"] == "ffn":
            lj = rec["lj"]
            at_once(sibling_plan, sibling_done, "rs_sibling_" + tag)
            (dgu2,) = behind(lambda plan: _ffn_da(tag + "_da", dy, w_d4(lj), rec["gu2"], plan), chip_plan, chips_done)
            add_grad(("ffn_w_down", lj), _ffn_dwd(tag + "_dwd", rec["a"], dy)[0].reshape(gathered[("ffn_w_down", lj)].shape))
            at_once(sibling_plan, sibling_done, "rs_sibling_" + tag + "_dwd")
            (dwgu,) = behind(lambda plan: _ffn_dwgu(tag + "_dwgu", rec["h"], dgu2, plan), chip_plan, chips_done)
            add_grad(("ffn_w_gu", lj), dwgu)
            at_once(sibling_plan, sibling_done, "rs_sibling_" + tag + "_dwgu")
            (dh,) = behind(lambda plan: _ffn_dh(tag + "_dh", dgu2, w_gu(lj), plan), chip_plan, chips_done)
        elif rec["kind"] == "sb":
            at_once(sibling_plan, sibling_done, "rs_sibling_" + tag)
            do = _mm_nt(tag + "_do", dy, rec["w_o"], BF16)
            dwo = _mm_tn(tag + "_dwo", rec["o"], dy, BF16)
            wri = _pack_flat([dwr, dwi], LANES, 512, BF16)[None]
            dqkv3, wri_all = behind(lambda plan: _sb_bwd(rec["qkv"], do, rec["ltot"], d, plan), chip_plan, chips_done,
                                    more=_gather_plan([wri], [False]))
            add_grad(("sb_w_o", 0), dwo.reshape(gathered[("sb_w_o", 0)].shape))
            dh = _mm_nt_stack(tag + "_dh", dqkv3, rec["w_qkv"], F32)
            add_grad(("sb_w_qkv", 0), _mm_tn_stack(tag + "_dwqkv", rec["h"], dqkv3, BF16)[None])
        else:
            (dymix,) = carrying(tag + "_dymix", lambda plan: _mm_nt(tag + "_dymix", dy, rec["w_out"], F32, plan)
                                if plan else [_mm_nt(tag + "_dymix", dy, rec["w_out"], F32)])
            dwout = _mm_tn(tag + "_dwout", rec["ymix"], dy, BF16).reshape(gathered[("lru_w_out", 0)].shape)
            dgx2, dlru_small, dwr, dwi = carrying("lru_bwd", lambda plan: _lru_bwd(rec["gx"], rec["hs"], dymix, lru_small,
                                                                                wr_b, wi_b, plan))
            dh = _mm_nt_stack(tag + "_dh", dgx2, rec["w_in"], F32)
            dw_in = _mm_tn_stack(tag + "_dwin", rec["h"], dgx2, BF16)
            add_grad(("lru_w_out", 0), dwout)
            add_grad(("lru_w_in", 0), _cols_to_chunks("lru_w_in_chunks", dw_in, eye2)[None])
        prev = saved[idx - 1] if idx > 0 else None
        gw_prev = (prev["w"] * prev["gmul"]) if prev is not None else jnp.zeros((1, d), F32)
        dxo, dy, sums = _adaln_bwd(dh, rec["x"], rec["y"], dxo, rec["g"], rec["scale1p"], rec["w"], gw_prev, tag + "_adaln")
        dmod[layer][sub] = sums[0:3]
        dnorm[layer][sub] = sums[3]
    grad_x = dxo.reshape(x.shape)

    dmod_mine = jnp.stack([jnp.stack(dmod[layer]) for layer in range(n_layers)])
    dnorm_mine = jnp.stack([jnp.stack(dnorm[layer]) for layer in range(n_layers)])
    assert not to_pair and not to_chips
    small_shapes = [(n_layers, 9 * d), (n_layers, 3, d), (8, r_dim), (d,), (1,)]
    small3 = _pack_flat([dmod_mine, dnorm_mine, dlru_small, dgf, loss_mine], LANES, 256, F32)
    n_small3 = small3.shape[0]
    all3 = _allgather(small3[None], "gather_small_grads").reshape(N_DEV, n_small3, LANES)
    gsum = _sum_devices(all3, "sum_small_grads").reshape(-1)
    g_mod_b, g_norm_full, g_lru_small, g_final, loss_sum = _unpack_flat(gsum, small_shapes)
    loss = loss_sum[0]
    wri_sum = _sum_devices(wri_all.reshape(N_DEV, -1, LANES), "sum_gate_weight_grads").reshape(-1)
    g_wr, g_wi = _unpack_flat(wri_sum, [lru_w_r.shape, lru_w_i.shape])
    dmod_all = all3.reshape(N_DEV, -1)[:, :n_layers * 9 * d].reshape(N_DEV, n_layers, N_DEV, mod_cols)
    dmod_cols = jnp.transpose(lax.dynamic_index_in_dim(dmod_all, me, axis=2, keepdims=False), (1, 0, 2))

    out_g, out_d, out_m, out_v = {}, {}, {}, {}
    out_g["mod_w"], out_d["mod_w"], out_m["mod_w"], out_v["mod_w"] = _mod_w_update(c_all, dmod_cols, mod_w, m_mod_w, v_mod_w)

    g_norm_shard = lax.dynamic_slice_in_dim(g_norm_full, me * ng, ng, axis=2)
    g_lru_shard = lax.dynamic_slice_in_dim(g_lru_small, me * rs, rs, axis=1)
    small_grads = dict(mod_b=g_mod_b, norm_g=g_norm_shard, lru_conv_w=g_lru_shard[0:4].reshape(lru_conv_w.shape),
                       lru_conv_b=g_lru_shard[4:5], lru_b_r=g_lru_shard[5:6], lru_b_i=g_lru_shard[6:7],
                       lru_lambda=g_lru_shard[7:8], final_norm_g=g_final)
    for n, g in (("lru_w_r", g_wr), ("lru_w_i", g_wi)):
        view = lambda arr: arr.reshape(-1, LRU_BLOCK_W)
        outs = _adam_update("adam_" + n, view(weights[n]), view(mom_m[n]), view(mom_v[n]), [view(g)])
        out_g[n], out_d[n], out_m[n], out_v[n] = [o.reshape(weights[n].shape) for o in outs]
    small_names = list(small_grads)
    sw = _pack_flat([weights[n] for n in small_names], LANES, 256, F32)
    sg = _pack_flat([small_grads[n] for n in small_names], LANES, 256, F32)
    sm = _pack_flat([mom_m[n] for n in small_names], LANES, 256, F32)
    sv = _pack_flat([mom_v[n] for n in small_names], LANES, 256, F32)
    s_outs = _adam_update("adam_small", sw, sm, sv, [sg])
    small_shapes2 = [weights[n].shape for n in small_names]
    for dst, flat in zip((out_g, out_d, out_m, out_v), s_outs):
        for n, arr in zip(small_names, _unpack_flat(flat.reshape(-1), small_shapes2)):
            dst[n] = arr

    for n in ["ffn_w_gu", "ffn_w_down", "sb_w_qkv", "sb_w_o", "lru_w_in", "lru_w_out"]:
        shp = weights[n].shape
        shard3 = (math.prod(shp[:-2]),) + shp[-2:]
        view = lambda arr: arr.reshape(shard3)
        outs = None
        for q in range(shard3[0]):
            fills = outs if outs is not None else [lax.empty(shard3, F32) for _ in range(4)]
            p4, r3 = reduced[(n, q)]
            outs = _adam_shard(f"adam_{n}{q}", view(weights[n]), view(mom_m[n]), view(mom_v[n]), p4, r3, chip_idx,
                               first=q, fills=fills if shard3[0] > 1 else None)
        out_g[n], out_d[n], out_m[n], out_v[n] = [o.reshape(shp) for o in outs]

    return (loss, grad_x, *[out_g[n] for n in names], *[out_d[n] for n in names], *[out_m[n] for n in names],
            *[out_v[n] for n in names])
```
